```python
import jax, jax.numpy as jnp
from jax import lax
import numpy as np

D_MODEL = 2048
BATCH = 8
SEQ = 4096
DEPTH = 1

MLA_HEADS = 8
QK_NOPE_DIM = 128
QK_ROPE_DIM = 64
V_HEAD_DIM = 128
Q_LORA_RANK = 512
KV_LORA_RANK = 256
ROPE_THETA = 10000.0
Q_BLOCK = 128
DIL_PATTERNS = ((128, 1), (512, 4), (2048, 16))
DIL_GROUPS = 3
DIL_HEADS_PER_GROUP = 4
DIL_HEADS = DIL_GROUPS * DIL_HEADS_PER_GROUP
DIL_HEAD_DIM = 128
DIL_BLOCK = 128
ALIBI_MAX_BIAS = 8.0
D_FF = 5504
CONV_WIDTH = 3
NORM_EPS = 1e-6

MLA_Q_DIM = MLA_HEADS * (QK_NOPE_DIM + QK_ROPE_DIM)
MLA_KV_DIM = MLA_HEADS * (QK_NOPE_DIM + V_HEAD_DIM)
DIL_QKV_DIM = DIL_HEADS * DIL_HEAD_DIM
DIL_OUT_DIM = DIL_HEADS_PER_GROUP * DIL_HEAD_DIM
IN_SPLITS = (Q_LORA_RANK, KV_LORA_RANK, QK_ROPE_DIM, DIL_QKV_DIM, DIL_QKV_DIM, DIL_QKV_DIM, D_MODEL, D_MODEL)
D_IN = Q_LORA_RANK + KV_LORA_RANK + QK_ROPE_DIM + 3 * DIL_QKV_DIM + 2 * D_MODEL

kernel_name = 'hybrid_mla_dilated_convffn'


def rmsnorm(x, g):
    xf = x.astype(jnp.float32)
    y = xf * lax.rsqrt(jnp.mean(xf * xf, axis=-1, keepdims=True) + NORM_EPS)
    return (y * g.astype(jnp.float32)).astype(x.dtype)


def rope(x, cos, sin):
    half = x.shape[-1] // 2
    xf = x.astype(jnp.float32)
    x1, x2 = xf[..., :half], xf[..., half:]
    return jnp.concatenate([x1 * cos - x2 * sin, x2 * cos + x1 * sin], axis=-1).astype(x.dtype)


def mla_attention(c_q, c_kv, k_pe_raw, q_norm_g, w_uq, kv_norm_g, w_ukv):
    B, S, _ = c_q.shape
    q = (rmsnorm(c_q, q_norm_g) @ w_uq).reshape(B, S, MLA_HEADS, QK_NOPE_DIM + QK_ROPE_DIM)
    kv = (rmsnorm(c_kv, kv_norm_g) @ w_ukv).reshape(B, S, MLA_HEADS, QK_NOPE_DIM + V_HEAD_DIM)
    q_nope, q_pe = q[..., :QK_NOPE_DIM], q[..., QK_NOPE_DIM:]
    k_nope, v = kv[..., :QK_NOPE_DIM], kv[..., QK_NOPE_DIM:]
    pos = jnp.arange(S, dtype=jnp.float32)
    inv_freq = ROPE_THETA ** (-jnp.arange(0, QK_ROPE_DIM, 2, dtype=jnp.float32) / QK_ROPE_DIM)
    ang = pos[:, None] * inv_freq[None, :]
    cos, sin = jnp.cos(ang), jnp.sin(ang)
    q_pe = rope(q_pe, cos[:, None, :], sin[:, None, :])
    k_pe = rope(k_pe_raw, cos, sin)
    scale = (QK_NOPE_DIM + QK_ROPE_DIM) ** -0.5
    nb = S // Q_BLOCK
    qn_b = q_nope.reshape(B, nb, Q_BLOCK, MLA_HEADS, QK_NOPE_DIM).transpose(1, 0, 2, 3, 4)
    qp_b = q_pe.reshape(B, nb, Q_BLOCK, MLA_HEADS, QK_ROPE_DIM).transpose(1, 0, 2, 3, 4)
    kpos = jnp.arange(S)

    def one_block(args):
        qn, qp, i = args
        s = (jnp.einsum('bqhd,bkhd->bhqk', qn, k_nope).astype(jnp.float32)
             + jnp.einsum('bqhr,bkr->bhqk', qp, k_pe).astype(jnp.float32)) * scale
        qpos = i * Q_BLOCK + jnp.arange(Q_BLOCK)
        s = jnp.where(kpos[None, :] <= qpos[:, None], s, -jnp.inf)
        p = jax.nn.softmax(s, axis=-1).astype(v.dtype)
        return jnp.einsum('bhqk,bkhd->bqhd', p, v)

    o = lax.map(one_block, (qn_b, qp_b, jnp.arange(nb)))
    return o.transpose(1, 0, 2, 3, 4).reshape(B, S, MLA_HEADS * V_HEAD_DIM)


def dilated_group(q, k, v, window, dil, slopes):
    B, S, H, D = q.shape
    w_sub = window // dil
    L = S // dil
    nb = -(-L // DIL_BLOCK)
    Lp = nb * DIL_BLOCK

    def to_blocks(t):
        t = t.reshape(B, L, dil, H, D).transpose(0, 2, 1, 3, 4)
        t = jnp.pad(t, ((0, 0), (0, 0), (0, Lp - L), (0, 0), (0, 0)))
        return t.reshape(B, dil, nb, DIL_BLOCK, H, D)

    def with_prev(t):
        prev = jnp.pad(t, ((0, 0), (0, 0), (1, 0), (0, 0), (0, 0), (0, 0)))[:, :, :-1]
        return jnp.concatenate([prev, t], axis=3)

    qb = to_blocks(q)
    kk = with_prev(to_blocks(k))
    vv = with_prev(to_blocks(v))
    s = jnp.einsum('brnqhd,brnkhd->brnhqk', qb, kk).astype(jnp.float32) * (D ** -0.5)
    p_idx = jnp.arange(DIL_BLOCK)
    k_idx = jnp.arange(2 * DIL_BLOCK)
    j = p_idx[:, None] + DIL_BLOCK - k_idx[None, :]
    valid = (j >= 0) & (j <= w_sub)
    first = jnp.arange(nb) == 0
    valid = valid[None] & ~(first[:, None, None] & (k_idx < DIL_BLOCK)[None, None, :])
    alibi = -slopes.astype(jnp.float32)[:, None, None] * (dil * j).astype(jnp.float32)[None]
    s = jnp.where(valid[None, None, :, None], s + alibi[None, None, None], -jnp.inf)
    lse = jax.nn.logsumexp(s, axis=-1)
    p = jnp.exp(s - lse[..., None]).astype(v.dtype)
    o = jnp.einsum('brnhqk,brnkhd->brnqhd', p, vv)

    def from_blocks(t):
        t = t.reshape((B, dil, Lp) + t.shape[4:])[:, :, :L]
        t = jnp.moveaxis(t, 1, 2)
        return t.reshape((B, S) + t.shape[3:])

    return from_blocks(o), from_blocks(lse.transpose(0, 1, 2, 4, 3))


def dilated_attention(dq, dk, dv):
    B, S, _ = dq.shape
    shp = (B, S, DIL_GROUPS, DIL_HEADS_PER_GROUP, DIL_HEAD_DIM)
    q, k, v = dq.reshape(shp), dk.reshape(shp), dv.reshape(shp)
    slopes = 2.0 ** (-ALIBI_MAX_BIAS * jnp.arange(1, DIL_HEADS + 1, dtype=jnp.float32) / DIL_HEADS)
    slopes = slopes.reshape(DIL_GROUPS, DIL_HEADS_PER_GROUP)
    outs, lses = [], []
    for g, (window, dil) in enumerate(DIL_PATTERNS):
        o_g, l_g = dilated_group(q[:, :, g], k[:, :, g], v[:, :, g], window, dil, slopes[g])
        outs.append(o_g)
        lses.append(l_g)
    o = jnp.stack(outs, axis=0)
    wts = jax.nn.softmax(jnp.stack(lses, axis=0), axis=0)
    out = jnp.sum(wts[..., None] * o.astype(jnp.float32), axis=0).astype(dq.dtype)
    return out.reshape(B, S, DIL_OUT_DIM)


def causal_dwconv(u, w, b):
    S = u.shape[1]
    upad = jnp.pad(u, ((0, 0), (CONV_WIDTH - 1, 0), (0, 0)))
    out = b
    for t in range(CONV_WIDTH):
        out = out + w[t] * upad[:, t:t + S]
    return out


def _fwd_setup_inputs(seed: int = 0) -> dict:
    key = jax.random.key(seed)
    ks = jax.random.split(key, 17)

    def w(k, shape, fan_in):
        return jax.random.normal(k, shape, jnp.float32) * (fan_in ** -0.5)

    def gain(k, shape):
        return 1.0 + 0.02 * jax.random.normal(k, shape, jnp.float32)

    return {
        'x': jax.random.normal(ks[0], (BATCH, SEQ, D_MODEL), jnp.float32),
        'attn_norm_g': gain(ks[1], (DEPTH, D_MODEL)),
        'w_in': w(ks[2], (DEPTH, D_MODEL, D_IN), D_MODEL),
        'b_gate': 0.02 * jax.random.normal(ks[3], (DEPTH, 2 * D_MODEL), jnp.float32),
        'q_norm_g': gain(ks[4], (DEPTH, Q_LORA_RANK)),
        'w_uq': w(ks[5], (DEPTH, Q_LORA_RANK, MLA_Q_DIM), Q_LORA_RANK),
        'kv_norm_g': gain(ks[6], (DEPTH, KV_LORA_RANK)),
        'w_ukv': w(ks[7], (DEPTH, KV_LORA_RANK, MLA_KV_DIM), KV_LORA_RANK),
        'w_o_mla': w(ks[8], (DEPTH, MLA_HEADS * V_HEAD_DIM, D_MODEL), MLA_HEADS * V_HEAD_DIM),
        'w_o_dil': w(ks[9], (DEPTH, DIL_OUT_DIM, D_MODEL), DIL_OUT_DIM),
        'w_out': w(ks[10], (DEPTH, D_MODEL, D_MODEL), D_MODEL),
        'ffn_norm_g': gain(ks[11], (DEPTH, D_MODEL)),
        'w_up': w(ks[12], (DEPTH, D_MODEL, 2 * D_FF), D_MODEL),
        'conv_w': w(ks[13], (DEPTH, CONV_WIDTH, 2 * D_FF), CONV_WIDTH),
        'conv_b': 0.02 * jax.random.normal(ks[14], (DEPTH, 2 * D_FF), jnp.float32),
        'w_down': w(ks[15], (DEPTH, D_FF, D_MODEL), D_FF),
        'final_norm_g': gain(ks[16], (D_MODEL,)),
    }


def _fwd_reference(x, attn_norm_g, w_in, b_gate, q_norm_g, w_uq, kv_norm_g, w_ukv, w_o_mla, w_o_dil,
              w_out, ffn_norm_g, w_up, conv_w, conv_b, w_down, final_norm_g):
    split_at = [int(c) for c in np.cumsum(IN_SPLITS)[:-1]]
    for l in range(DEPTH):
        h = rmsnorm(x, attn_norm_g[l])
        proj = h @ w_in[l]
        c_q, c_kv, k_pe, dq, dk, dv, ga, gb = jnp.split(proj, split_at, axis=-1)
        gate_a = jax.nn.sigmoid(ga + b_gate[l, :D_MODEL])
        gate_b = jax.nn.sigmoid(gb + b_gate[l, D_MODEL:])
        o_a = mla_attention(c_q, c_kv, k_pe, q_norm_g[l], w_uq[l], kv_norm_g[l], w_ukv[l]) @ w_o_mla[l]
        o_b = dilated_attention(dq, dk, dv) @ w_o_dil[l]
        x = x + (gate_a * o_a + gate_b * o_b) @ w_out[l]
        h2 = rmsnorm(x, ffn_norm_g[l])
        u = causal_dwconv(h2 @ w_up[l], conv_w[l], conv_b[l])
        up, gate = u[..., :D_FF], u[..., D_FF:]
        x = x + (jax.nn.silu(gate) * up) @ w_down[l]
    return rmsnorm(x, final_norm_g)


import jax as _jax
import jax.numpy as _jnp

TWIN_FORMAT = 'train_step'
FWD_PARAMS = ['x', 'attn_norm_g', 'w_in', 'b_gate', 'q_norm_g', 'w_uq', 'kv_norm_g', 'w_ukv', 'w_o_mla', 'w_o_dil', 'w_out', 'ffn_norm_g', 'w_up', 'conv_w', 'conv_b', 'w_down', 'final_norm_g']
TWIN_WEIGHTS = ['attn_norm_g', 'w_in', 'b_gate', 'q_norm_g', 'w_uq', 'kv_norm_g', 'w_ukv', 'w_o_mla', 'w_o_dil', 'w_out', 'ffn_norm_g', 'w_up', 'conv_w', 'conv_b', 'w_down', 'final_norm_g']
TWIN_DIFF_INPUT = 'x'
TWIN_INPUTS = ['x', 'attn_norm_g', 'w_in', 'b_gate', 'q_norm_g', 'w_uq', 'kv_norm_g', 'w_ukv', 'w_o_mla', 'w_o_dil', 'w_out', 'ffn_norm_g', 'w_up', 'conv_w', 'conv_b', 'w_down', 'final_norm_g', 'loss_target', 'm_attn_norm_g', 'm_w_in', 'm_b_gate', 'm_q_norm_g', 'm_w_uq', 'm_kv_norm_g', 'm_w_ukv', 'm_w_o_mla', 'm_w_o_dil', 'm_w_out', 'm_ffn_norm_g', 'm_w_up', 'm_conv_w', 'm_conv_b', 'm_w_down', 'm_final_norm_g', 'v_attn_norm_g', 'v_w_in', 'v_b_gate', 'v_q_norm_g', 'v_w_uq', 'v_kv_norm_g', 'v_w_ukv', 'v_w_o_mla', 'v_w_o_dil', 'v_w_out', 'v_ffn_norm_g', 'v_w_up', 'v_conv_w', 'v_conv_b', 'v_w_down', 'v_final_norm_g']
TWIN_OUTPUTS = ['loss', 'grad_x', 'grad_attn_norm_g', 'grad_w_in', 'grad_b_gate', 'grad_q_norm_g', 'grad_w_uq', 'grad_kv_norm_g', 'grad_w_ukv', 'grad_w_o_mla', 'grad_w_o_dil', 'grad_w_out', 'grad_ffn_norm_g', 'grad_w_up', 'grad_conv_w', 'grad_conv_b', 'grad_w_down', 'grad_final_norm_g', 'delta_attn_norm_g', 'delta_w_in', 'delta_b_gate', 'delta_q_norm_g', 'delta_w_uq', 'delta_kv_norm_g', 'delta_w_ukv', 'delta_w_o_mla', 'delta_w_o_dil', 'delta_w_out', 'delta_ffn_norm_g', 'delta_w_up', 'delta_conv_w', 'delta_conv_b', 'delta_w_down', 'delta_final_norm_g', 'new_m_attn_norm_g', 'new_m_w_in', 'new_m_b_gate', 'new_m_q_norm_g', 'new_m_w_uq', 'new_m_kv_norm_g', 'new_m_w_ukv', 'new_m_w_o_mla', 'new_m_w_o_dil', 'new_m_w_out', 'new_m_ffn_norm_g', 'new_m_w_up', 'new_m_conv_w', 'new_m_conv_b', 'new_m_w_down', 'new_m_final_norm_g', 'new_v_attn_norm_g', 'new_v_w_in', 'new_v_b_gate', 'new_v_q_norm_g', 'new_v_w_uq', 'new_v_kv_norm_g', 'new_v_w_ukv', 'new_v_w_o_mla', 'new_v_w_o_dil', 'new_v_w_out', 'new_v_ffn_norm_g', 'new_v_w_up', 'new_v_conv_w', 'new_v_conv_b', 'new_v_w_down', 'new_v_final_norm_g']
TWIN_LEAF_KINDS = {'loss': 'loss', 'grad_x': 'grad_x', 'grad_attn_norm_g': 'grad_w', 'grad_w_in': 'grad_w', 'grad_b_gate': 'grad_w', 'grad_q_norm_g': 'grad_w', 'grad_w_uq': 'grad_w', 'grad_kv_norm_g': 'grad_w', 'grad_w_ukv': 'grad_w', 'grad_w_o_mla': 'grad_w', 'grad_w_o_dil': 'grad_w', 'grad_w_out': 'grad_w', 'grad_ffn_norm_g': 'grad_w', 'grad_w_up': 'grad_w', 'grad_conv_w': 'grad_w', 'grad_conv_b': 'grad_w', 'grad_w_down': 'grad_w', 'grad_final_norm_g': 'grad_w', 'delta_attn_norm_g': 'delta_w', 'delta_w_in': 'delta_w', 'delta_b_gate': 'delta_w', 'delta_q_norm_g': 'delta_w', 'delta_w_uq': 'delta_w', 'delta_kv_norm_g': 'delta_w', 'delta_w_ukv': 'delta_w', 'delta_w_o_mla': 'delta_w', 'delta_w_o_dil': 'delta_w', 'delta_w_out': 'delta_w', 'delta_ffn_norm_g': 'delta_w', 'delta_w_up': 'delta_w', 'delta_conv_w': 'delta_w', 'delta_conv_b': 'delta_w', 'delta_w_down': 'delta_w', 'delta_final_norm_g': 'delta_w', 'new_m_attn_norm_g': 'new_m', 'new_m_w_in': 'new_m', 'new_m_b_gate': 'new_m', 'new_m_q_norm_g': 'new_m', 'new_m_w_uq': 'new_m', 'new_m_kv_norm_g': 'new_m', 'new_m_w_ukv': 'new_m', 'new_m_w_o_mla': 'new_m', 'new_m_w_o_dil': 'new_m', 'new_m_w_out': 'new_m', 'new_m_ffn_norm_g': 'new_m', 'new_m_w_up': 'new_m', 'new_m_conv_w': 'new_m', 'new_m_conv_b': 'new_m', 'new_m_w_down': 'new_m', 'new_m_final_norm_g': 'new_m', 'new_v_attn_norm_g': 'new_v', 'new_v_w_in': 'new_v', 'new_v_b_gate': 'new_v', 'new_v_q_norm_g': 'new_v', 'new_v_w_uq': 'new_v', 'new_v_kv_norm_g': 'new_v', 'new_v_w_ukv': 'new_v', 'new_v_w_o_mla': 'new_v', 'new_v_w_o_dil': 'new_v', 'new_v_w_out': 'new_v', 'new_v_ffn_norm_g': 'new_v', 'new_v_w_up': 'new_v', 'new_v_conv_w': 'new_v', 'new_v_conv_b': 'new_v', 'new_v_w_down': 'new_v', 'new_v_final_norm_g': 'new_v'}


def _forward(args):
    return _fwd_reference(*[args[k] for k in FWD_PARAMS])


def _output_shape():
    def fwd():
        inp = _fwd_setup_inputs(0)
        return _fwd_reference(*[inp[k] for k in FWD_PARAMS])
    out = _jax.eval_shape(fwd)
    return out.shape, out.dtype

N_MICROBATCH = 1
ADAM_LR = 0.001
ADAM_B1 = 0.9
ADAM_B2 = 0.999
ADAM_EPS = 1e-08
ADAM_WD = 0.01
ADAM_STEP = 10
PER_EXAMPLE_BATCH_AXIS = {'x': 0, 'loss_target': 0}
SHARED_INPUTS = []
_WEIGHT_DTYPES = {'attn_norm_g': _jnp.float32, 'w_in': _jnp.float32, 'b_gate': _jnp.float32, 'q_norm_g': _jnp.float32, 'w_uq': _jnp.float32, 'kv_norm_g': _jnp.float32, 'w_ukv': _jnp.float32, 'w_o_mla': _jnp.float32, 'w_o_dil': _jnp.float32, 'w_out': _jnp.float32, 'ffn_norm_g': _jnp.float32, 'w_up': _jnp.float32, 'conv_w': _jnp.float32, 'conv_b': _jnp.float32, 'w_down': _jnp.float32, 'final_norm_g': _jnp.float32}
MOMENT_SCALE = {'attn_norm_g': 3.294880e-02, 'w_in': 1.528351e-02, 'b_gate': 5.728125e-03, 'q_norm_g': 1.896300e-02, 'w_uq': 1.096352e-02, 'kv_norm_g': 4.249861e-02, 'w_ukv': 1.393742e-02, 'w_o_mla': 1.140400e-02, 'w_o_dil': 1.736454e-02, 'w_out': 2.079611e-02, 'ffn_norm_g': 6.597127e-02, 'w_up': 2.882690e-02, 'conv_w': 2.833868e-02, 'conv_b': 2.893748e-02, 'w_down': 4.652450e-02, 'final_norm_g': 1.600399e+01}


def _to_microbatches(a, axis):
    t = _jnp.moveaxis(a, axis, 0)
    t = t.reshape((N_MICROBATCH, t.shape[0] // N_MICROBATCH) + t.shape[1:])
    return _jnp.moveaxis(t, 1, axis + 1)


def setup_inputs(seed: int = 0) -> dict:
    inp = _fwd_setup_inputs(seed)
    key = _jax.random.fold_in(_jax.random.key(seed), 7919)
    shape, _ = _output_shape()
    out = dict(inp)
    out["loss_target"] = _jax.random.normal(_jax.random.fold_in(key, 0), shape, _jnp.float32)
    for i, name in enumerate(TWIN_WEIGHTS):
        w = inp[name].astype(_jnp.float32)
        if MOMENT_SCALE is None:
            s = _jnp.sqrt(_jnp.mean(_jnp.square(w)) + 1e-30)
        else:
            s = MOMENT_SCALE[name]
        km, kv = _jax.random.split(_jax.random.fold_in(key, i + 1))
        out[name] = w
        out["m_" + name] = s * _jax.random.normal(km, w.shape, _jnp.float32)
        out["v_" + name] = (s * s) * _jax.random.uniform(kv, w.shape, _jnp.float32, 0.5, 1.5)
    if N_MICROBATCH > 1:
        for name, axis in PER_EXAMPLE_BATCH_AXIS.items():
            out[name] = _to_microbatches(out[name], axis)
    return {'x': out['x'], 'attn_norm_g': out['attn_norm_g'], 'w_in': out['w_in'], 'b_gate': out['b_gate'], 'q_norm_g': out['q_norm_g'], 'w_uq': out['w_uq'], 'kv_norm_g': out['kv_norm_g'], 'w_ukv': out['w_ukv'], 'w_o_mla': out['w_o_mla'], 'w_o_dil': out['w_o_dil'], 'w_out': out['w_out'], 'ffn_norm_g': out['ffn_norm_g'], 'w_up': out['w_up'], 'conv_w': out['conv_w'], 'conv_b': out['conv_b'], 'w_down': out['w_down'], 'final_norm_g': out['final_norm_g'], 'loss_target': out['loss_target'], 'm_attn_norm_g': out['m_attn_norm_g'], 'm_w_in': out['m_w_in'], 'm_b_gate': out['m_b_gate'], 'm_q_norm_g': out['m_q_norm_g'], 'm_w_uq': out['m_w_uq'], 'm_kv_norm_g': out['m_kv_norm_g'], 'm_w_ukv': out['m_w_ukv'], 'm_w_o_mla': out['m_w_o_mla'], 'm_w_o_dil': out['m_w_o_dil'], 'm_w_out': out['m_w_out'], 'm_ffn_norm_g': out['m_ffn_norm_g'], 'm_w_up': out['m_w_up'], 'm_conv_w': out['m_conv_w'], 'm_conv_b': out['m_conv_b'], 'm_w_down': out['m_w_down'], 'm_final_norm_g': out['m_final_norm_g'], 'v_attn_norm_g': out['v_attn_norm_g'], 'v_w_in': out['v_w_in'], 'v_b_gate': out['v_b_gate'], 'v_q_norm_g': out['v_q_norm_g'], 'v_w_uq': out['v_w_uq'], 'v_kv_norm_g': out['v_kv_norm_g'], 'v_w_ukv': out['v_w_ukv'], 'v_w_o_mla': out['v_w_o_mla'], 'v_w_o_dil': out['v_w_o_dil'], 'v_w_out': out['v_w_out'], 'v_ffn_norm_g': out['v_ffn_norm_g'], 'v_w_up': out['v_w_up'], 'v_conv_w': out['v_conv_w'], 'v_conv_b': out['v_conv_b'], 'v_w_down': out['v_w_down'], 'v_final_norm_g': out['v_final_norm_g']}


def _loss(weights, diff, rest, loss_target):
    with _jax.named_scope("forward"):
        args = {**rest, TWIN_DIFF_INPUT: diff, **{k: w.astype(_WEIGHT_DTYPES[k]) for k, w in weights.items()}}
        y = _forward(args)
    with _jax.named_scope("loss_head"):
        err = _jnp.square(y.astype(_jnp.float32) - loss_target)
        return 0.5 * _jnp.sum(_jnp.mean(err, axis=-1)) if err.ndim else 0.5 * err


def _adamw(w, g, m, v):
    m = ADAM_B1 * m + (1.0 - ADAM_B1) * g
    v = ADAM_B2 * v + (1.0 - ADAM_B2) * _jnp.square(g)
    m_hat = m / (1.0 - ADAM_B1 ** ADAM_STEP)
    v_hat = v / (1.0 - ADAM_B2 ** ADAM_STEP)
    delta = -ADAM_LR * (m_hat / (_jnp.sqrt(v_hat) + ADAM_EPS) + ADAM_WD * w)
    return delta, m, v


def reference(x, attn_norm_g, w_in, b_gate, q_norm_g, w_uq, kv_norm_g, w_ukv, w_o_mla, w_o_dil, w_out, ffn_norm_g, w_up, conv_w, conv_b, w_down, final_norm_g, loss_target, m_attn_norm_g, m_w_in, m_b_gate, m_q_norm_g, m_w_uq, m_kv_norm_g, m_w_ukv, m_w_o_mla, m_w_o_dil, m_w_out, m_ffn_norm_g, m_w_up, m_conv_w, m_conv_b, m_w_down, m_final_norm_g, v_attn_norm_g, v_w_in, v_b_gate, v_q_norm_g, v_w_uq, v_kv_norm_g, v_w_ukv, v_w_o_mla, v_w_o_dil, v_w_out, v_ffn_norm_g, v_w_up, v_conv_w, v_conv_b, v_w_down, v_final_norm_g):
    given = dict(x=x, attn_norm_g=attn_norm_g, w_in=w_in, b_gate=b_gate, q_norm_g=q_norm_g, w_uq=w_uq, kv_norm_g=kv_norm_g, w_ukv=w_ukv, w_o_mla=w_o_mla, w_o_dil=w_o_dil, w_out=w_out, ffn_norm_g=ffn_norm_g, w_up=w_up, conv_w=conv_w, conv_b=conv_b, w_down=w_down, final_norm_g=final_norm_g, loss_target=loss_target, m_attn_norm_g=m_attn_norm_g, m_w_in=m_w_in, m_b_gate=m_b_gate, m_q_norm_g=m_q_norm_g, m_w_uq=m_w_uq, m_kv_norm_g=m_kv_norm_g, m_w_ukv=m_w_ukv, m_w_o_mla=m_w_o_mla, m_w_o_dil=m_w_o_dil, m_w_out=m_w_out, m_ffn_norm_g=m_ffn_norm_g, m_w_up=m_w_up, m_conv_w=m_conv_w, m_conv_b=m_conv_b, m_w_down=m_w_down, m_final_norm_g=m_final_norm_g, v_attn_norm_g=v_attn_norm_g, v_w_in=v_w_in, v_b_gate=v_b_gate, v_q_norm_g=v_q_norm_g, v_w_uq=v_w_uq, v_kv_norm_g=v_kv_norm_g, v_w_ukv=v_w_ukv, v_w_o_mla=v_w_o_mla, v_w_o_dil=v_w_o_dil, v_w_out=v_w_out, v_ffn_norm_g=v_ffn_norm_g, v_w_up=v_w_up, v_conv_w=v_conv_w, v_conv_b=v_conv_b, v_w_down=v_w_down, v_final_norm_g=v_final_norm_g)
    weights = {n: given[n] for n in TWIN_WEIGHTS}
    shared = {n: given[n] for n in SHARED_INPUTS}
    per_example = {n: given[n] for n in ['x']}
    grad_fn = _jax.value_and_grad(_loss, argnums=(0, 1))

    def one_microbatch(ex, loss_target):
        ex = dict(ex)
        diff = ex.pop(TWIN_DIFF_INPUT)
        return grad_fn(weights, diff, {**shared, **ex}, loss_target)

    if N_MICROBATCH == 1:
        loss, (grad_w, grad_x) = one_microbatch(per_example, given["loss_target"])
    else:
        def body(carry, xs):
            loss_sum, grad_sum = carry
            l_k, (gw_k, gx_k) = one_microbatch(xs[0], xs[1])
            with _jax.named_scope("update"):
                return (loss_sum + l_k, _jax.tree.map(_jnp.add, grad_sum, gw_k)), gx_k

        init = (_jnp.zeros((), _jnp.float32), _jax.tree.map(_jnp.zeros_like, weights))
        (loss, grad_w), grad_x = _jax.lax.scan(body, init, (per_example, given["loss_target"]))
    with _jax.named_scope("update"):
        delta_w, new_m, new_v = {}, {}, {}
        for n in TWIN_WEIGHTS:
            delta_w[n], new_m[n], new_v[n] = _adamw(weights[n], grad_w[n], given["m_" + n], given["v_" + n])
    return (loss, grad_x, *[grad_w[n] for n in TWIN_WEIGHTS], *[delta_w[n] for n in TWIN_WEIGHTS],
            *[new_m[n] for n in TWIN_WEIGHTS], *[new_v[n] for n in TWIN_WEIGHTS])
```

```python
import functools

import jax
import jax.numpy as jnp
import numpy as np
from jax import lax
from jax.experimental import pallas as pl
from jax.experimental.pallas import tpu as pltpu

F32 = jnp.float32
BF16 = jnp.bfloat16

N_DEV = 8
N_CHIP = 4
AXES = ("x", "y", "c")
MESH = pl.DeviceIdType.MESH

D_MODEL = 2048
MLA_HEADS = 8
QK_NOPE = 128
QK_ROPE = 64
V_HEAD = 128
Q_LORA = 512
KV_LORA = 256
ROPE_THETA = 10000.0
HEAD_PAD = 256
DIL_PATTERNS = ((128, 1), (512, 4), (2048, 16))
DIL_GROUPS = 3
DIL_HG = 4
DIL_HEADS = 12
DIL_HD = 128
DIL_BLK = 128
DIL_QKV = DIL_HEADS * DIL_HD
DIL_OUT = DIL_HG * DIL_HD
ALIBI_MAX_BIAS = 8.0
D_FF = 5504
D_FF_PAD = 5632
NORM_EPS = 1e-6
LAT = Q_LORA + KV_LORA + QK_ROPE
LAT_PAD = 896
D_IN = LAT + 3 * DIL_QKV + 2 * D_MODEL
NEG = -1e30

ADAM_LR = 0.001
ADAM_B1 = 0.9
ADAM_B2 = 0.999
ADAM_EPS = 1e-08
ADAM_WD = 0.01
ADAM_STEP = 10

SMALL_ROWS = 56
SMALL_COLS = 1024

PACK_COLS = 2048
IN_ROWS = 1192
IN_ROWS_PAD = 1200
PACK_SEGS = (
    ("w_in", IN_ROWS_PAD),
    ("w_up", 1376),
    ("w_out", 256),
    ("w_down", 688),
    ("w_o_mla", 128),
    ("w_o_dil", 64),
    ("w_uq", 48),
    ("w_ukv", 32),
    ("zero_fill", 16),
)
PACK_ROWS = sum(r for _, r in PACK_SEGS)
PACK_TR = 272
BIG = tuple(n for n, _ in PACK_SEGS if n != "zero_fill")
CONV_SHARD = (3, 1376)

NT = (((1,), (1,)), ((), ()))
TN = (((0,), (0,)), ((), ()))


def _dot(a, b, dims=(((1,), (0,)), ((), ()))):
    return lax.dot_general(a, b, dims, preferred_element_type=F32)


def _mm(a, b, mode, out_dtype, tm, tn, tk, name, bias=None, act=None, res=None, b_koff=0):
    if mode == "nn":
        (M, K), (K2, N) = a.shape, b.shape
        assert (b_koff + 1) * K <= K2, (name, a.shape, b.shape)
        koff, K2 = b_koff * (K // tk), K
        a_spec = pl.BlockSpec((tm, tk), lambda i, j, k: (i, k))
        b_spec = pl.BlockSpec((tk, tn), lambda i, j, k: (k + koff, j))
        dims = (((1,), (0,)), ((), ()))
    elif mode == "nt":
        (M, K), (N, K2) = a.shape, b.shape
        a_spec = pl.BlockSpec((tm, tk), lambda i, j, k: (i, k))
        b_spec = pl.BlockSpec((tn, tk), lambda i, j, k: (j, k))
        dims = NT
    else:
        (K, M), (K2, N) = a.shape, b.shape
        a_spec = pl.BlockSpec((tk, tm), lambda i, j, k: (k, i))
        b_spec = pl.BlockSpec((tk, tn), lambda i, j, k: (k, j))
        dims = TN
    assert K == K2 and M % tm == 0 and N % tn == 0 and K % tk == 0, (name, a.shape, b.shape)
    nk = K // tk
    has_bias, has_res = bias is not None, res is not None

    def body(*refs):
        refs = list(refs)
        a_ref, b_ref = refs[0], refs[1]
        pos = 2
        bias_ref = res_ref = None
        if has_bias:
            bias_ref = refs[pos]
            pos += 1
        if has_res:
            res_ref = refs[pos]
            pos += 1
        o_ref = refs[pos]
        p = _dot(a_ref[...].astype(BF16), b_ref[...].astype(BF16), dims)

        def finish(acc):
            if has_bias:
                acc = acc + bias_ref[...]
            if act == "sigmoid":
                acc = jax.nn.sigmoid(acc)
            if has_res:
                acc = res_ref[...] + acc
            o_ref[...] = acc.astype(o_ref.dtype)

        if nk == 1:
            finish(p)
        else:
            acc_ref = refs[pos + 1]
            k = pl.program_id(2)

            @pl.when(k == 0)
            def _():
                acc_ref[...] = p

            @pl.when(k != 0)
            def _():
                acc_ref[...] += p

            @pl.when(k == nk - 1)
            def _():
                finish(acc_ref[...])

    in_specs = [a_spec, b_spec]
    args = [a, b]
    if has_bias:
        in_specs.append(pl.BlockSpec((1, tn), lambda i, j, k: (0, j)))
        args.append(bias)
    if has_res:
        in_specs.append(pl.BlockSpec((tm, tn), lambda i, j, k: (i, j)))
        args.append(res)
    return pl.pallas_call(
        body,
        name=name,
        grid=(M // tm, N // tn, nk),
        in_specs=in_specs,
        out_specs=pl.BlockSpec((tm, tn), lambda i, j, k: (i, j)),
        out_shape=jax.ShapeDtypeStruct((M, N), out_dtype),
        scratch_shapes=[pltpu.VMEM((tm, tn), F32)] if nk > 1 else [],
        compiler_params=pltpu.CompilerParams(dimension_semantics=("parallel", "parallel", "arbitrary")),
    )(*args)


def _rstd(x):
    return lax.rsqrt(jnp.mean(x * x, axis=-1, keepdims=True) + NORM_EPS)


def _rms_bwd_math(dy, x, g):
    r = _rstd(x)
    xh = x * r
    dg = jnp.sum(dy * xh, axis=0, keepdims=True)
    dxh = dy * g
    dx = r * (dxh - xh * jnp.mean(dxh * xh, axis=-1, keepdims=True))
    return dx, dg


def _rms_fwd(x, g, name, tr=256):
    S, D = x.shape

    def body(x_ref, g_ref, o_ref):
        xv = x_ref[...]
        o_ref[...] = ((xv * _rstd(xv)) * g_ref[...]).astype(o_ref.dtype)

    return pl.pallas_call(
        body,
        name=name,
        grid=(S // tr,),
        in_specs=[pl.BlockSpec((tr, D), lambda i: (i, 0)), pl.BlockSpec((1, D), lambda i: (0, 0))],
        out_specs=pl.BlockSpec((tr, D), lambda i: (i, 0)),
        out_shape=jax.ShapeDtypeStruct((S, D), BF16),
        compiler_params=pltpu.CompilerParams(dimension_semantics=("parallel",)),
    )(x, g)


def _rms_bwd(dy, x, g, res, name, tr=256):
    S, D = x.shape

    def body(dy_ref, x_ref, g_ref, res_ref, dx_ref, dxb_ref, dg_ref):
        dx, dg = _rms_bwd_math(dy_ref[...], x_ref[...], g_ref[...])
        dx = dx + res_ref[...]
        dx_ref[...] = dx
        dxb_ref[...] = dx.astype(BF16)

        @pl.when(pl.program_id(0) == 0)
        def _():
            dg_ref[...] = dg

        @pl.when(pl.program_id(0) != 0)
        def _():
            dg_ref[...] += dg

    row = pl.BlockSpec((tr, D), lambda i: (i, 0))
    vec = pl.BlockSpec((1, D), lambda i: (0, 0))
    return pl.pallas_call(
        body,
        name=name,
        grid=(S // tr,),
        in_specs=[row, row, vec, row],
        out_specs=[row, row, vec],
        out_shape=[jax.ShapeDtypeStruct((S, D), F32), jax.ShapeDtypeStruct((S, D), BF16), jax.ShapeDtypeStruct((1, D), F32)],
        compiler_params=pltpu.CompilerParams(dimension_semantics=("arbitrary",)),
    )(dy, x, g, res)


def _final_loss(x2, g, tgt, name, tr=256):
    S, D = x2.shape

    def body(x_ref, g_ref, t_ref, loss_ref, dx_ref, dxb_ref, dg_ref):
        xv, gv = x_ref[...], g_ref[...]
        y = (xv * _rstd(xv)) * gv
        e = y - t_ref[...]
        part = 0.5 * jnp.sum(jnp.mean(e * e, axis=-1, keepdims=True), axis=0, keepdims=True)
        dx, dg = _rms_bwd_math(e * (1.0 / D), xv, gv)
        dx_ref[...] = dx
        dxb_ref[...] = dx.astype(BF16)
        part = jnp.broadcast_to(part, (1, 128))

        @pl.when(pl.program_id(0) == 0)
        def _():
            dg_ref[...] = dg
            loss_ref[...] = part

        @pl.when(pl.program_id(0) != 0)
        def _():
            dg_ref[...] += dg
            loss_ref[...] += part

    row = pl.BlockSpec((tr, D), lambda i: (i, 0))
    vec = pl.BlockSpec((1, D), lambda i: (0, 0))
    return pl.pallas_call(
        body,
        name=name,
        grid=(S // tr,),
        in_specs=[row, vec, row],
        out_specs=[pl.BlockSpec((1, 128), lambda i: (0, 0)), row, row, vec],
        out_shape=[
            jax.ShapeDtypeStruct((1, 128), F32),
            jax.ShapeDtypeStruct((S, D), F32),
            jax.ShapeDtypeStruct((S, D), BF16),
            jax.ShapeDtypeStruct((1, D), F32),
        ],
        compiler_params=pltpu.CompilerParams(dimension_semantics=("arbitrary",)),
    )(x2, g, tgt)


def _rope_tables(S):
    pos = jnp.arange(S, dtype=F32)
    inv_freq = ROPE_THETA ** (-jnp.arange(0, QK_ROPE, 2, dtype=F32) / QK_ROPE)
    ang = pos[:, None] * inv_freq[None, :]
    cos, sin = jnp.cos(ang), jnp.sin(ang)
    zero = jnp.zeros((S, 128 - QK_ROPE), F32)
    return jnp.concatenate([cos, cos, zero], axis=1), jnp.concatenate([-sin, sin, zero], axis=1)


def _rope_tile(x, cos_t, sin_t):
    lane = lax.broadcasted_iota(jnp.int32, x.shape, 1)
    partner = jnp.where(lane < QK_ROPE // 2, pltpu.roll(x, 128 - QK_ROPE // 2, 1), pltpu.roll(x, QK_ROPE // 2, 1))
    return x * cos_t + partner * sin_t


def _mla_prep1(lat, gq, gkv, cos_t, sin_t, name, tr=256):
    S = lat.shape[0]

    def body(lat_ref, gq_ref, gkv_ref, cos_ref, sin_ref, cq_ref, ckv_ref, kpe_ref):
        cq = lat_ref[:, :Q_LORA]
        ckv = lat_ref[:, Q_LORA : Q_LORA + KV_LORA]
        cq_ref[...] = ((cq * _rstd(cq)) * gq_ref[...]).astype(BF16)
        ckv_ref[...] = ((ckv * _rstd(ckv)) * gkv_ref[...]).astype(BF16)
        kpe_ref[...] = _rope_tile(lat_ref[:, Q_LORA + KV_LORA :], cos_ref[...], sin_ref[...]).astype(BF16)

    def row(n):
        return pl.BlockSpec((tr, n), lambda i: (i, 0))

    def vec(n):
        return pl.BlockSpec((1, n), lambda i: (0, 0))

    return pl.pallas_call(
        body,
        name=name,
        grid=(S // tr,),
        in_specs=[row(LAT_PAD), vec(Q_LORA), vec(KV_LORA), row(128), row(128)],
        out_specs=[row(Q_LORA), row(KV_LORA), row(128)],
        out_shape=[
            jax.ShapeDtypeStruct((S, Q_LORA), BF16),
            jax.ShapeDtypeStruct((S, KV_LORA), BF16),
            jax.ShapeDtypeStruct((S, 128), BF16),
        ],
        compiler_params=pltpu.CompilerParams(dimension_semantics=("parallel",)),
    )(lat, gq, gkv, cos_t, sin_t)


def _mla_prep2(q_raw, kv, kpe, cos_t, sin_t, name, tr=256):
    S = q_raw.shape[0]
    W = MLA_HEADS * HEAD_PAD

    def body(q_ref, kv_ref, kpe_ref, cos_ref, sin_ref, qa_ref, ka_ref):
        cos_v, sin_v, kpe_v = cos_ref[...], sin_ref[...], kpe_ref[...]
        for h in range(MLA_HEADS):
            lo = h * HEAD_PAD
            qa_ref[:, lo : lo + 128] = q_ref[:, lo : lo + 128].astype(BF16)
            qa_ref[:, lo + 128 : lo + 256] = _rope_tile(q_ref[:, lo + 128 : lo + 256], cos_v, sin_v).astype(BF16)
            ka_ref[:, lo : lo + 128] = kv_ref[:, lo : lo + 128]
            ka_ref[:, lo + 128 : lo + 256] = kpe_v

    def row(n):
        return pl.BlockSpec((tr, n), lambda i: (i, 0))

    return pl.pallas_call(
        body,
        name=name,
        grid=(S // tr,),
        in_specs=[row(W), row(W), row(128), row(128), row(128)],
        out_specs=[row(W), row(W)],
        out_shape=[jax.ShapeDtypeStruct((S, W), BF16), jax.ShapeDtypeStruct((S, W), BF16)],
        compiler_params=pltpu.CompilerParams(dimension_semantics=("parallel",)),
    )(q_raw, kv, kpe, cos_t, sin_t)


def _mla_post(dq_att, dk_att, dv, cos_t, sin_t, name, tr=256):
    S = dq_att.shape[0]
    W = MLA_HEADS * HEAD_PAD

    def body(dq_ref, dk_ref, dv_ref, cos_ref, sin_ref, dqr_ref, dkv_ref, dkpe_ref):
        cos_v, nsin_v = cos_ref[...], -sin_ref[...]
        kpe = jnp.zeros((tr, 128), F32)
        for h in range(MLA_HEADS):
            lo = h * HEAD_PAD
            dqr_ref[:, lo : lo + 128] = dq_ref[:, lo : lo + 128].astype(BF16)
            dqr_ref[:, lo + 128 : lo + 256] = _rope_tile(dq_ref[:, lo + 128 : lo + 256], cos_v, nsin_v).astype(BF16)
            dkv_ref[:, lo : lo + 128] = dk_ref[:, lo : lo + 128].astype(BF16)
            dkv_ref[:, lo + 128 : lo + 256] = dv_ref[:, h * 128 : (h + 1) * 128].astype(BF16)
            kpe = kpe + dk_ref[:, lo + 128 : lo + 256]
        dkpe_ref[...] = _rope_tile(kpe, cos_v, nsin_v)

    def row(n):
        return pl.BlockSpec((tr, n), lambda i: (i, 0))

    return pl.pallas_call(
        body,
        name=name,
        grid=(S // tr,),
        in_specs=[row(W), row(W), row(MLA_HEADS * V_HEAD), row(128), row(128)],
        out_specs=[row(W), row(W), row(128)],
        out_shape=[jax.ShapeDtypeStruct((S, W), BF16), jax.ShapeDtypeStruct((S, W), BF16), jax.ShapeDtypeStruct((S, 128), F32)],
        compiler_params=pltpu.CompilerParams(dimension_semantics=("parallel",)),
    )(dq_att, dk_att, dv, cos_t, sin_t)


def _lat_bwd(dcqn, dckvn, dkpe, lat, gq, gkv, name, tr=256):
    S = lat.shape[0]

    def body(dcq_ref, dckv_ref, dkpe_ref, lat_ref, gq_ref, gkv_ref, dlat_ref, dgq_ref, dgkv_ref):
        dq, dgq = _rms_bwd_math(dcq_ref[...], lat_ref[:, :Q_LORA], gq_ref[...])
        dkv, dgkv = _rms_bwd_math(dckv_ref[...], lat_ref[:, Q_LORA : Q_LORA + KV_LORA], gkv_ref[...])
        dlat_ref[:, :Q_LORA] = dq.astype(BF16)
        dlat_ref[:, Q_LORA : Q_LORA + KV_LORA] = dkv.astype(BF16)
        dlat_ref[:, Q_LORA + KV_LORA :] = dkpe_ref[...].astype(BF16)

        @pl.when(pl.program_id(0) == 0)
        def _():
            dgq_ref[...] = dgq
            dgkv_ref[...] = dgkv

        @pl.when(pl.program_id(0) != 0)
        def _():
            dgq_ref[...] += dgq
            dgkv_ref[...] += dgkv

    def row(n):
        return pl.BlockSpec((tr, n), lambda i: (i, 0))

    def vec(n):
        return pl.BlockSpec((1, n), lambda i: (0, 0))

    return pl.pallas_call(
        body,
        name=name,
        grid=(S // tr,),
        in_specs=[row(Q_LORA), row(KV_LORA), row(128), row(LAT_PAD), vec(Q_LORA), vec(KV_LORA)],
        out_specs=[row(LAT_PAD), vec(Q_LORA), vec(KV_LORA)],
        out_shape=[
            jax.ShapeDtypeStruct((S, LAT_PAD), BF16),
            jax.ShapeDtypeStruct((1, Q_LORA), F32),
            jax.ShapeDtypeStruct((1, KV_LORA), F32),
        ],
        compiler_params=pltpu.CompilerParams(dimension_semantics=("arbitrary",)),
    )(dcqn, dckvn, dkpe, lat, gq, gkv)


MLA_SCALE = (QK_NOPE + QK_ROPE) ** -0.5


def _causal_scores(q, k, q0, k0):
    s = _dot(q, k, NT) * MLA_SCALE
    rows = q0 + lax.broadcasted_iota(jnp.int32, s.shape, 0)
    cols = k0 + lax.broadcasted_iota(jnp.int32, s.shape, 1)
    return jnp.where(cols <= rows, s, NEG)


def _flash_fwd(q_att, k_att, kv, name, t=512):
    S = q_att.shape[0]
    n = S // t

    def body(q_ref, k_ref, v_ref, o_ref, lse_ref, m_sc, l_sc, acc_sc):
        qi, kj = pl.program_id(1), pl.program_id(2)

        @pl.when(kj == 0)
        def _():
            m_sc[...] = jnp.full((t, 1), NEG, F32)
            l_sc[...] = jnp.zeros((t, 1), F32)
            acc_sc[...] = jnp.zeros((t, V_HEAD), F32)

        @pl.when(kj <= qi)
        def _():
            s = _causal_scores(q_ref[...], k_ref[...], qi * t, kj * t)
            m_prev = m_sc[...]
            m_new = jnp.maximum(m_prev, jnp.max(s, axis=1, keepdims=True))
            p = jnp.exp(s - m_new)
            alpha = jnp.exp(m_prev - m_new)
            l_sc[...] = alpha * l_sc[...] + jnp.sum(p, axis=1, keepdims=True)
            acc_sc[...] = alpha * acc_sc[...] + _dot(p.astype(BF16), v_ref[...])
            m_sc[...] = m_new

        @pl.when(kj == qi)
        def _():
            l = l_sc[...]
            o_ref[...] = acc_sc[...] / l
            lse_ref[0] = m_sc[...] + jnp.log(l)

    return pl.pallas_call(
        body,
        name=name,
        grid=(MLA_HEADS, n, n),
        in_specs=[
            pl.BlockSpec((t, HEAD_PAD), lambda h, i, j: (i, h)),
            pl.BlockSpec((t, HEAD_PAD), lambda h, i, j: (jnp.minimum(j, i), h)),
            pl.BlockSpec((t, V_HEAD), lambda h, i, j: (jnp.minimum(j, i), 2 * h + 1)),
        ],
        out_specs=[
            pl.BlockSpec((t, V_HEAD), lambda h, i, j: (i, h)),
            pl.BlockSpec((1, t, 1), lambda h, i, j: (h, i, 0)),
        ],
        out_shape=[jax.ShapeDtypeStruct((S, MLA_HEADS * V_HEAD), F32), jax.ShapeDtypeStruct((MLA_HEADS, S, 1), F32)],
        scratch_shapes=[pltpu.VMEM((t, 1), F32), pltpu.VMEM((t, 1), F32), pltpu.VMEM((t, V_HEAD), F32)],
        compiler_params=pltpu.CompilerParams(dimension_semantics=("parallel", "parallel", "arbitrary")),
    )(q_att, k_att, kv)


def _flash_dq(q_att, k_att, kv, o, do, lse, name, t=512):
    S = q_att.shape[0]
    n = S // t

    def body(q_ref, k_ref, v_ref, o_ref, do_ref, lse_ref, dq_ref, acc_sc, dl_sc):
        qi, kj = pl.program_id(1), pl.program_id(2)

        @pl.when(kj == 0)
        def _():
            acc_sc[...] = jnp.zeros((t, HEAD_PAD), F32)
            dl_sc[...] = jnp.sum(do_ref[...].astype(F32) * o_ref[...], axis=1, keepdims=True)

        @pl.when(kj <= qi)
        def _():
            k = k_ref[...]
            s = _causal_scores(q_ref[...], k, qi * t, kj * t)
            p = jnp.exp(s - lse_ref[0])
            dp = _dot(do_ref[...], v_ref[...], NT)
            ds = (p * (dp - dl_sc[...])) * MLA_SCALE
            acc_sc[...] += _dot(ds.astype(BF16), k)

        @pl.when(kj == qi)
        def _():
            dq_ref[...] = acc_sc[...]

    return pl.pallas_call(
        body,
        name=name,
        grid=(MLA_HEADS, n, n),
        in_specs=[
            pl.BlockSpec((t, HEAD_PAD), lambda h, i, j: (i, h)),
            pl.BlockSpec((t, HEAD_PAD), lambda h, i, j: (jnp.minimum(j, i), h)),
            pl.BlockSpec((t, V_HEAD), lambda h, i, j: (jnp.minimum(j, i), 2 * h + 1)),
            pl.BlockSpec((t, V_HEAD), lambda h, i, j: (i, h)),
            pl.BlockSpec((t, V_HEAD), lambda h, i, j: (i, h)),
            pl.BlockSpec((1, t, 1), lambda h, i, j: (h, i, 0)),
        ],
        out_specs=pl.BlockSpec((t, HEAD_PAD), lambda h, i, j: (i, h)),
        out_shape=jax.ShapeDtypeStruct((S, MLA_HEADS * HEAD_PAD), F32),
        scratch_shapes=[pltpu.VMEM((t, HEAD_PAD), F32), pltpu.VMEM((t, 1), F32)],
        compiler_params=pltpu.CompilerParams(dimension_semantics=("parallel", "parallel", "arbitrary")),
    )(q_att, k_att, kv, o, do, lse)


def _flash_dkv(q_att, k_att, kv, o, do, lse, name, t=512):
    S = q_att.shape[0]
    n = S // t

    def body(q_ref, k_ref, v_ref, o_ref, do_ref, lse_ref, dk_ref, dv_ref, dk_sc, dv_sc):
        kj, qi = pl.program_id(1), pl.program_id(2)

        @pl.when(qi == 0)
        def _():
            dk_sc[...] = jnp.zeros((t, HEAD_PAD), F32)
            dv_sc[...] = jnp.zeros((t, V_HEAD), F32)

        @pl.when(qi >= kj)
        def _():
            q, do_v = q_ref[...], do_ref[...]
            s = _causal_scores(q, k_ref[...], qi * t, kj * t)
            p = jnp.exp(s - lse_ref[0])
            dv_sc[...] += _dot(p.astype(BF16), do_v, TN)
            dp = _dot(do_v, v_ref[...], NT)
            dl = jnp.sum(do_v.astype(F32) * o_ref[...], axis=1, keepdims=True)
            ds = (p * (dp - dl)) * MLA_SCALE
            dk_sc[...] += _dot(ds.astype(BF16), q, TN)

        @pl.when(qi == n - 1)
        def _():
            dk_ref[...] = dk_sc[...]
            dv_ref[...] = dv_sc[...]

    return pl.pallas_call(
        body,
        name=name,
        grid=(MLA_HEADS, n, n),
        in_specs=[
            pl.BlockSpec((t, HEAD_PAD), lambda h, j, i: (jnp.maximum(i, j), h)),
            pl.BlockSpec((t, HEAD_PAD), lambda h, j, i: (j, h)),
            pl.BlockSpec((t, V_HEAD), lambda h, j, i: (j, 2 * h + 1)),
            pl.BlockSpec((t, V_HEAD), lambda h, j, i: (jnp.maximum(i, j), h)),
            pl.BlockSpec((t, V_HEAD), lambda h, j, i: (jnp.maximum(i, j), h)),
            pl.BlockSpec((1, t, 1), lambda h, j, i: (h, jnp.maximum(i, j), 0)),
        ],
        out_specs=[
            pl.BlockSpec((t, HEAD_PAD), lambda h, j, i: (j, h)),
            pl.BlockSpec((t, V_HEAD), lambda h, j, i: (j, h)),
        ],
        out_shape=[jax.ShapeDtypeStruct((S, MLA_HEADS * HEAD_PAD), F32), jax.ShapeDtypeStruct((S, MLA_HEADS * V_HEAD), F32)],
        scratch_shapes=[pltpu.VMEM((t, HEAD_PAD), F32), pltpu.VMEM((t, V_HEAD), F32)],
        compiler_params=pltpu.CompilerParams(dimension_semantics=("parallel", "parallel", "arbitrary")),
    )(q_att, k_att, kv, o, do, lse)


DIL_SCALE = DIL_HD**-0.5


def _dil_bias():
    slopes = 2.0 ** (-ALIBI_MAX_BIAS * np.arange(1, DIL_HEADS + 1, dtype=np.float64) / DIL_HEADS)
    slopes = slopes.astype(np.float32).reshape(DIL_GROUPS, DIL_HG)
    p = np.arange(DIL_BLK)[:, None]
    kidx = np.arange(2 * DIL_BLK)[None, :]
    j = p + DIL_BLK - kidx
    out = np.zeros((DIL_GROUPS, DIL_HG, DIL_BLK, 2 * DIL_BLK), np.float32)
    for g, (window, dil) in enumerate(DIL_PATTERNS):
        valid = (j >= 0) & (j <= window // dil)
        for h in range(DIL_HG):
            alibi = -slopes[g, h] * (dil * j).astype(np.float32)
            out[g, h] = np.where(valid, alibi, np.float32(NEG))
    return jnp.asarray(out)


def _dil_fwd_group(view, bias_g, g, dil, name):
    L = view.shape[0]
    nb = L // DIL_BLK
    nqc = 3 * DIL_QKV // 128

    def body(bias_ref, q_ref, k_ref, v_ref, o_ref, lse_ref):
        bias = bias_ref[0]

        def attend(q, kk, vv, b):
            s = _dot(q, kk, NT) * DIL_SCALE + b
            m = jnp.max(s, axis=1, keepdims=True)
            e = jnp.exp(s - m)
            l = jnp.sum(e, axis=1, keepdims=True)
            p = e * (1.0 / l)
            return _dot(p.astype(BF16), vv), m + jnp.log(l)

        o0, l0 = attend(q_ref[0:DIL_BLK, :], k_ref[0:DIL_BLK, :], v_ref[0:DIL_BLK, :], bias[:, DIL_BLK:])
        o_ref[0:DIL_BLK, :] = o0
        lse_ref[0:DIL_BLK, :] = jnp.broadcast_to(l0, (DIL_BLK, 128))

        def step(n, carry):
            r0 = pl.multiple_of(n * DIL_BLK, DIL_BLK)
            p0 = pl.multiple_of((n - 1) * DIL_BLK, DIL_BLK)
            o, l = attend(q_ref[pl.ds(r0, DIL_BLK), :], k_ref[pl.ds(p0, 2 * DIL_BLK), :], v_ref[pl.ds(p0, 2 * DIL_BLK), :], bias)
            o_ref[pl.ds(r0, DIL_BLK), :] = o
            lse_ref[pl.ds(r0, DIL_BLK), :] = jnp.broadcast_to(l, (DIL_BLK, 128))
            return carry

        lax.fori_loop(1, nb, step, 0)

    def col(base):
        return pl.BlockSpec((L, 128), lambda i: (0, (i // DIL_HG) * nqc + base + g * DIL_HG + i % DIL_HG))

    out = pl.BlockSpec((L, 128), lambda i: (0, i))
    return pl.pallas_call(
        body,
        name=name,
        grid=(dil * DIL_HG,),
        in_specs=[pl.BlockSpec((1, DIL_BLK, 2 * DIL_BLK), lambda i: (i % DIL_HG, 0, 0)), col(0), col(DIL_QKV // 128), col(2 * DIL_QKV // 128)],
        out_specs=[out, out],
        out_shape=[jax.ShapeDtypeStruct((L, dil * DIL_OUT), F32), jax.ShapeDtypeStruct((L, dil * DIL_OUT), F32)],
        compiler_params=pltpu.CompilerParams(dimension_semantics=("parallel",)),
    )(bias_g, view, view, view)


def _dil_combine(os_, ls_, name, tr=512):
    S = os_[0].shape[0]

    def body(o0, o1, o2, l0, l1, l2, out_ref, lse_ref):
        a, b, c = l0[...], l1[...], l2[...]
        m = jnp.maximum(jnp.maximum(a, b), c)
        ea, eb, ec = jnp.exp(a - m), jnp.exp(b - m), jnp.exp(c - m)
        den = ea + eb + ec
        inv = 1.0 / den
        out_ref[...] = (ea * inv) * o0[...] + (eb * inv) * o1[...] + (ec * inv) * o2[...]
        lse_ref[...] = m + jnp.log(den)

    row = pl.BlockSpec((tr, DIL_OUT), lambda i: (i, 0))
    return pl.pallas_call(
        body,
        name=name,
        grid=(S // tr,),
        in_specs=[row] * 6,
        out_specs=[row, row],
        out_shape=[jax.ShapeDtypeStruct((S, DIL_OUT), F32)] * 2,
        compiler_params=pltpu.CompilerParams(dimension_semantics=("parallel",)),
    )(*os_, *ls_)


def _dil_rowdot(dod, od, name, tr=512):
    S = dod.shape[0]

    def body(d_ref, o_ref, dd_ref, db_ref):
        db_ref[...] = d_ref[...].astype(BF16)
        for h in range(DIL_HG):
            sl = slice(h * 128, (h + 1) * 128)
            sm = jnp.sum(d_ref[:, sl] * o_ref[:, sl], axis=1, keepdims=True)
            dd_ref[:, sl] = jnp.broadcast_to(sm, (tr, 128))

    row = pl.BlockSpec((tr, DIL_OUT), lambda i: (i, 0))
    return pl.pallas_call(
        body,
        name=name,
        grid=(S // tr,),
        in_specs=[row, row],
        out_specs=[row, row],
        out_shape=[jax.ShapeDtypeStruct((S, DIL_OUT), F32), jax.ShapeDtypeStruct((S, DIL_OUT), BF16)],
        compiler_params=pltpu.CompilerParams(dimension_semantics=("parallel",)),
    )(dod, od)


def _dil_bwd_group(view, bias_g, do_view, dd_view, lse_view, g, dil, name):
    L = view.shape[0]
    nb = L // DIL_BLK
    nqc = 3 * DIL_QKV // 128

    def body(bias_ref, q_ref, k_ref, v_ref, do_ref, dd_ref, lse_ref, dq_ref, dk_ref, dv_ref, dk_sc, dv_sc):
        bias = bias_ref[0]
        dk_sc[...] = jnp.zeros((L, 128), F32)
        dv_sc[...] = jnp.zeros((L, 128), F32)

        def grads(q, kk, vv, do, dd, lse, b):
            s = _dot(q, kk, NT) * DIL_SCALE + b
            p = jnp.exp(s - lse)
            dp = _dot(do, vv, NT)
            ds = ((p * (dp - dd)) * DIL_SCALE).astype(BF16)
            return _dot(ds, kk), _dot(ds, q, TN), _dot(p.astype(BF16), do, TN)

        first = slice(0, DIL_BLK)
        dq0, dk0, dv0 = grads(
            q_ref[first, :], k_ref[first, :], v_ref[first, :], do_ref[first, :], dd_ref[first, 0:1], lse_ref[first, 0:1], bias[:, DIL_BLK:]
        )
        dq_ref[first, :] = dq0.astype(BF16)
        dk_sc[first, :] += dk0
        dv_sc[first, :] += dv0

        def step(n, carry):
            r0 = pl.multiple_of(n * DIL_BLK, DIL_BLK)
            p0 = pl.multiple_of((n - 1) * DIL_BLK, DIL_BLK)
            cur, both = pl.ds(r0, DIL_BLK), pl.ds(p0, 2 * DIL_BLK)
            dq, dk, dv = grads(q_ref[cur, :], k_ref[both, :], v_ref[both, :], do_ref[cur, :], dd_ref[cur, 0:1], lse_ref[cur, 0:1], bias)
            dq_ref[cur, :] = dq.astype(BF16)
            dk_sc[both, :] += dk
            dv_sc[both, :] += dv
            return carry

        lax.fori_loop(1, nb, step, 0)
        dk_ref[...] = dk_sc[...].astype(BF16)
        dv_ref[...] = dv_sc[...].astype(BF16)

    def col(base):
        return pl.BlockSpec((L, 128), lambda i: (0, (i // DIL_HG) * nqc + base + g * DIL_HG + i % DIL_HG))

    out = pl.BlockSpec((L, 128), lambda i: (0, i))
    return pl.pallas_call(
        body,
        name=name,
        grid=(dil * DIL_HG,),
        in_specs=[
            pl.BlockSpec((1, DIL_BLK, 2 * DIL_BLK), lambda i: (i % DIL_HG, 0, 0)),
            col(0),
            col(DIL_QKV // 128),
            col(2 * DIL_QKV // 128),
            out,
            out,
            out,
        ],
        out_specs=[out, out, out],
        out_shape=[jax.ShapeDtypeStruct((L, dil * DIL_OUT), BF16)] * 3,
        scratch_shapes=[pltpu.VMEM((L, 128), F32), pltpu.VMEM((L, 128), F32)],
        compiler_params=pltpu.CompilerParams(dimension_semantics=("parallel",)),
    )(bias_g, view, view, view, do_view, dd_view, lse_view)


def _merge_fwd(gates, o_a, o_b, name, tr=256):
    S = o_a.shape[0]

    def body(ga_ref, gb_ref, oa_ref, ob_ref, m_ref):
        m_ref[...] = (ga_ref[...] * oa_ref[...] + gb_ref[...] * ob_ref[...]).astype(BF16)

    row = pl.BlockSpec((tr, D_MODEL), lambda i: (i, 0))
    return pl.pallas_call(
        body,
        name=name,
        grid=(S // tr,),
        in_specs=[row, pl.BlockSpec((tr, D_MODEL), lambda i: (i, 1)), row, row],
        out_specs=row,
        out_shape=jax.ShapeDtypeStruct((S, D_MODEL), BF16),
        compiler_params=pltpu.CompilerParams(dimension_semantics=("parallel",)),
    )(gates, gates, o_a, o_b)


def _merge_bwd(dmrg, gates, o_a, o_b, name, tr=256):
    S = o_a.shape[0]

    def body(dm_ref, ga_ref, gb_ref, oa_ref, ob_ref, doa_ref, dob_ref, dga_ref, dgb_ref, dba_ref, dbb_ref):
        dm, ga, gb = dm_ref[...], ga_ref[...], gb_ref[...]
        doa_ref[...] = (dm * ga).astype(BF16)
        dob_ref[...] = (dm * gb).astype(BF16)
        dga = (dm * oa_ref[...]) * (ga * (1.0 - ga))
        dgb = (dm * ob_ref[...]) * (gb * (1.0 - gb))
        dga_ref[...] = dga.astype(BF16)
        dgb_ref[...] = dgb.astype(BF16)
        sa = jnp.sum(dga, axis=0, keepdims=True)
        sb = jnp.sum(dgb, axis=0, keepdims=True)

        @pl.when(pl.program_id(0) == 0)
        def _():
            dba_ref[...] = sa
            dbb_ref[...] = sb

        @pl.when(pl.program_id(0) != 0)
        def _():
            dba_ref[...] += sa
            dbb_ref[...] += sb

    row = pl.BlockSpec((tr, D_MODEL), lambda i: (i, 0))
    row1 = pl.BlockSpec((tr, D_MODEL), lambda i: (i, 1))
    vec = pl.BlockSpec((1, D_MODEL), lambda i: (0, 0))
    outs = pl.pallas_call(
        body,
        name=name,
        grid=(S // tr,),
        in_specs=[row, row, row1, row, row],
        out_specs=[row, row, row, row, vec, vec],
        out_shape=[jax.ShapeDtypeStruct((S, D_MODEL), BF16)] * 4 + [jax.ShapeDtypeStruct((1, D_MODEL), F32)] * 2,
        compiler_params=pltpu.CompilerParams(dimension_semantics=("arbitrary",)),
    )(dmrg, gates, gates, o_a, o_b)
    return outs


CONV_TR = 512
CONV_TC = 512
N_FFC = D_FF_PAD // CONV_TC


def _shift_down(x, edge_rows, k):
    out = pltpu.roll(x, k, 0)
    row = lax.broadcasted_iota(jnp.int32, x.shape, 0)
    for i in range(k):
        out = jnp.where(row == i, edge_rows[i], out)
    return out


def _shift_up(x, edge_rows, k):
    n = x.shape[0]
    out = pltpu.roll(x, n - k, 0)
    row = lax.broadcasted_iota(jnp.int32, x.shape, 0)
    for i in range(k):
        out = jnp.where(row == n - k + i, edge_rows[i], out)
    return out


def _conv_taps(x, halo_ref, live, w_ref, b_ref):
    h6, h7 = halo_ref[6:7, :] * live, halo_ref[7:8, :] * live
    s1 = _shift_down(x, [h7], 1)
    s2 = _shift_down(x, [h6, h7], 2)
    u = ((b_ref[...] + w_ref[0:1, :] * s2) + w_ref[1:2, :] * s1) + w_ref[2:3, :] * x
    return u, s1, s2


def _prev_halo(tr):
    return lambda i, j: (jnp.maximum(i * (tr // 8) - 1, 0), j)


def _ffn_fwd(u0, cw, cb, name):
    S = u0.shape[0]
    tr, tc = CONV_TR, CONV_TC

    def body(up_ref, gt_ref, hup_ref, hgt_ref, wu_ref, wg_ref, bu_ref, bg_ref, a_ref):
        live = (pl.program_id(0) > 0).astype(F32)
        up, _, _ = _conv_taps(up_ref[...], hup_ref, live, wu_ref, bu_ref)
        gt, _, _ = _conv_taps(gt_ref[...], hgt_ref, live, wg_ref, bg_ref)
        a_ref[...] = ((gt * jax.nn.sigmoid(gt)) * up).astype(BF16)

    return pl.pallas_call(
        body,
        name=name,
        grid=(S // tr, N_FFC),
        in_specs=[
            pl.BlockSpec((tr, tc), lambda i, j: (i, j)),
            pl.BlockSpec((tr, tc), lambda i, j: (i, j + N_FFC)),
            pl.BlockSpec((8, tc), _prev_halo(tr)),
            pl.BlockSpec((8, tc), lambda i, j: (jnp.maximum(i * (tr // 8) - 1, 0), j + N_FFC)),
            pl.BlockSpec((8, tc), lambda i, j: (0, j)),
            pl.BlockSpec((8, tc), lambda i, j: (0, j + N_FFC)),
            pl.BlockSpec((1, tc), lambda i, j: (0, j)),
            pl.BlockSpec((1, tc), lambda i, j: (0, j + N_FFC)),
        ],
        out_specs=pl.BlockSpec((tr, tc), lambda i, j: (i, j)),
        out_shape=jax.ShapeDtypeStruct((S, D_FF_PAD), BF16),
        compiler_params=pltpu.CompilerParams(dimension_semantics=("parallel", "parallel")),
    )(u0, u0, u0, u0, cw, cw, cb, cb)


def _ffn_bwd_du(u0, da, cw, cb, name):
    S = u0.shape[0]
    tr, tc = CONV_TR, CONV_TC

    def half(j):
        return j % N_FFC

    def body(up_ref, gt_ref, hup_ref, hgt_ref, da_ref, wu_ref, wg_ref, bu_ref, bg_ref, du_ref, dcw_ref, dcb_ref):
        i = pl.program_id(1)
        is_gate = pl.program_id(0) >= N_FFC
        live = (i > 0).astype(F32)
        x_up, x_gt = up_ref[...], gt_ref[...]
        up, up1, up2 = _conv_taps(x_up, hup_ref, live, wu_ref, bu_ref)
        gt, gt1, gt2 = _conv_taps(x_gt, hgt_ref, live, wg_ref, bg_ref)
        sg = jax.nn.sigmoid(gt)
        da_v = da_ref[...]
        d_up = da_v * (gt * sg)
        d_gt = (da_v * up) * (sg * (1.0 + gt * (1.0 - sg)))
        du = jnp.where(is_gate, d_gt, d_up)
        du_ref[...] = du
        x0 = jnp.where(is_gate, x_gt, x_up)
        x1 = jnp.where(is_gate, gt1, up1)
        x2 = jnp.where(is_gate, gt2, up2)
        tap = lax.broadcasted_iota(jnp.int32, (8, tc), 0)
        dcw = jnp.where(
            tap == 0,
            jnp.sum(du * x2, axis=0, keepdims=True),
            jnp.where(tap == 1, jnp.sum(du * x1, axis=0, keepdims=True), jnp.where(tap == 2, jnp.sum(du * x0, axis=0, keepdims=True), 0.0)),
        )
        dcb = jnp.sum(du, axis=0, keepdims=True)

        @pl.when(i == 0)
        def _():
            dcw_ref[...] = dcw
            dcb_ref[...] = dcb

        @pl.when(i != 0)
        def _():
            dcw_ref[...] += dcw
            dcb_ref[...] += dcb

    return pl.pallas_call(
        body,
        name=name,
        grid=(2 * N_FFC, S // tr),
        in_specs=[
            pl.BlockSpec((tr, tc), lambda j, i: (i, half(j))),
            pl.BlockSpec((tr, tc), lambda j, i: (i, half(j) + N_FFC)),
            pl.BlockSpec((8, tc), lambda j, i: (jnp.maximum(i * (tr // 8) - 1, 0), half(j))),
            pl.BlockSpec((8, tc), lambda j, i: (jnp.maximum(i * (tr // 8) - 1, 0), half(j) + N_FFC)),
            pl.BlockSpec((tr, tc), lambda j, i: (i, half(j))),
            pl.BlockSpec((8, tc), lambda j, i: (0, half(j))),
            pl.BlockSpec((8, tc), lambda j, i: (0, half(j) + N_FFC)),
            pl.BlockSpec((1, tc), lambda j, i: (0, half(j))),
            pl.BlockSpec((1, tc), lambda j, i: (0, half(j) + N_FFC)),
        ],
        out_specs=[
            pl.BlockSpec((tr, tc), lambda j, i: (i, j)),
            pl.BlockSpec((8, tc), lambda j, i: (0, j)),
            pl.BlockSpec((1, tc), lambda j, i: (0, j)),
        ],
        out_shape=[
            jax.ShapeDtypeStruct((S, 2 * D_FF_PAD), F32),
            jax.ShapeDtypeStruct((8, 2 * D_FF_PAD), F32),
            jax.ShapeDtypeStruct((1, 2 * D_FF_PAD), F32),
        ],
        compiler_params=pltpu.CompilerParams(dimension_semantics=("parallel", "arbitrary")),
    )(u0, u0, u0, u0, da, cw, cw, cb, cb)


def _ffn_bwd_du0(du, cw, name):
    S = du.shape[0]
    tr, tc = CONV_TR, CONV_TC
    nrow = S // tr

    def body(du_ref, halo_ref, w_ref, o_ref):
        live = (pl.program_id(0) < nrow - 1).astype(F32)
        x = du_ref[...]
        h0, h1 = halo_ref[0:1, :] * live, halo_ref[1:2, :] * live
        n1 = _shift_up(x, [h0], 1)
        n2 = _shift_up(x, [h0, h1], 2)
        o_ref[...] = ((w_ref[2:3, :] * x + w_ref[1:2, :] * n1) + w_ref[0:1, :] * n2).astype(BF16)

    return pl.pallas_call(
        body,
        name=name,
        grid=(nrow, 2 * N_FFC),
        in_specs=[
            pl.BlockSpec((tr, tc), lambda i, j: (i, j)),
            pl.BlockSpec((8, tc), lambda i, j: (jnp.minimum((i + 1) * (tr // 8), S // 8 - 1), j)),
            pl.BlockSpec((8, tc), lambda i, j: (0, j)),
        ],
        out_specs=pl.BlockSpec((tr, tc), lambda i, j: (i, j)),
        out_shape=jax.ShapeDtypeStruct((S, 2 * D_FF_PAD), BF16),
        compiler_params=pltpu.CompilerParams(dimension_semantics=("parallel", "parallel")),
    )(du, du, cw)


def _adamw(w, g, m, v, name):
    R, C = w.shape
    tr = R
    for cand in (256, 128, 64, 32, 16, 8):
        if R % cand == 0 and R > cand:
            tr = cand
            break

    def body(w_ref, g_ref, m_ref, v_ref, d_ref, nm_ref, nv_ref):
        gv = g_ref[...]
        nm = ADAM_B1 * m_ref[...] + (1.0 - ADAM_B1) * gv
        nv = ADAM_B2 * v_ref[...] + (1.0 - ADAM_B2) * (gv * gv)
        m_hat = nm / (1.0 - ADAM_B1**ADAM_STEP)
        v_hat = nv / (1.0 - ADAM_B2**ADAM_STEP)
        d_ref[...] = -ADAM_LR * (m_hat / (jnp.sqrt(v_hat) + ADAM_EPS) + ADAM_WD * w_ref[...])
        nm_ref[...] = nm
        nv_ref[...] = nv

    blk = pl.BlockSpec((tr, C), lambda i: (i, 0))
    return pl.pallas_call(
        body,
        name=name,
        grid=(R // tr,),
        in_specs=[blk] * 4,
        out_specs=[blk] * 3,
        out_shape=[jax.ShapeDtypeStruct((R, C), F32)] * 3,
        compiler_params=pltpu.CompilerParams(dimension_semantics=("parallel",)),
    )(w, g, m, v)


ANY = pl.BlockSpec(memory_space=pl.ANY)


def _all_gather(block, name):
    R, C = block.shape

    def body(x_ref, out_ref, send_sems, recv_sems, local_sem):
        x, y, c = lax.axis_index("x"), lax.axis_index("y"), lax.axis_index("c")
        me, sibling = (x, y, c), (x, y, 1 - c)
        chips = [(1 - x, y), (x, 1 - y), (1 - x, 1 - y)]

        def slot(px, py, pc):
            return out_ref.at[4 * px + 2 * py + pc]

        def copy(k, blk, to, src=None):
            return pltpu.make_async_remote_copy(
                src_ref=slot(*blk) if src is None else src,
                dst_ref=slot(*blk),
                send_sem=send_sems.at[k],
                recv_sem=recv_sems.at[k],
                device_id=to,
                device_id_type=MESH,
            )

        mine = pltpu.make_async_copy(x_ref, slot(*me), local_sem)
        mine.start()
        first = [copy(0, me, sibling, src=x_ref)]
        first += [copy(1 + j, me, (*chip, c), src=x_ref) for j, chip in enumerate(chips)]
        for cp in first:
            cp.start()
        passed = [copy(4 + j, (*chip, c), sibling) for j, chip in enumerate(chips)]
        for j, chip in enumerate(chips):
            copy(1 + j, (*chip, c), me).wait_recv()
            passed[j].start()
        copy(0, sibling, me).wait_recv()
        for j, chip in enumerate(chips):
            copy(4 + j, (*chip, 1 - c), me).wait_recv()
        for cp in first + passed:
            cp.wait_send()
        mine.wait()

    return pl.pallas_call(
        body,
        name=name,
        out_shape=jax.ShapeDtypeStruct((N_DEV, R, C), block.dtype),
        in_specs=[ANY],
        out_specs=ANY,
        scratch_shapes=[pltpu.SemaphoreType.DMA((7,)), pltpu.SemaphoreType.DMA((7,)), pltpu.SemaphoreType.DMA(())],
    )(block)


def _pair_exchange(g, name):
    _, R, C = g.shape

    def body(g_ref, out_ref, send_sems, recv_sems):
        x, y, c = lax.axis_index("x"), lax.axis_index("y"), lax.axis_index("c")
        copies = [
            pltpu.make_async_remote_copy(
                src_ref=g_ref.at[2 * k + (1 - c)],
                dst_ref=out_ref.at[k],
                send_sem=send_sems.at[k],
                recv_sem=recv_sems.at[k],
                device_id=(x, y, 1 - c),
                device_id_type=MESH,
            )
            for k in range(N_CHIP)
        ]
        for cp in copies:
            cp.start()
        for cp in copies:
            cp.wait()

    return pl.pallas_call(
        body,
        name=name,
        out_shape=jax.ShapeDtypeStruct((N_CHIP, R, C), g.dtype),
        in_specs=[ANY],
        out_specs=ANY,
        scratch_shapes=[pltpu.SemaphoreType.DMA((N_CHIP,)), pltpu.SemaphoreType.DMA((N_CHIP,))],
    )(g)


def _pair_add(g, recv, core, name, tr=PACK_TR):
    _, R, C = g.shape

    def body(core_ref, g_ref, r_ref, o_ref):
        o_ref[...] = (g_ref[...].astype(F32) + r_ref[...].astype(F32)).astype(o_ref.dtype)

    return pl.pallas_call(
        body,
        name=name,
        grid_spec=pltpu.PrefetchScalarGridSpec(
            num_scalar_prefetch=1,
            grid=(N_CHIP, R // tr),
            in_specs=[
                pl.BlockSpec((1, tr, C), lambda k, i, core_ref: (2 * k + core_ref[0], i, 0)),
                pl.BlockSpec((1, tr, C), lambda k, i, core_ref: (k, i, 0)),
            ],
            out_specs=pl.BlockSpec((1, tr, C), lambda k, i, core_ref: (k, i, 0)),
        ),
        out_shape=jax.ShapeDtypeStruct((N_CHIP, R, C), g.dtype),
        compiler_params=pltpu.CompilerParams(dimension_semantics=("parallel", "parallel")),
    )(core, g, recv)


def _chip_exchange(pair, name):
    _, R, C = pair.shape

    def body(p_ref, out_ref, send_sems, recv_sems, local_sem):
        x, y, c = lax.axis_index("x"), lax.axis_index("y"), lax.axis_index("c")
        mine = 2 * x + y
        chips = [(1 - x, y), (x, 1 - y), (1 - x, 1 - y)]
        local = pltpu.make_async_copy(p_ref.at[mine], out_ref.at[mine], local_sem)
        local.start()
        copies = [
            pltpu.make_async_remote_copy(
                src_ref=p_ref.at[2 * px + py],
                dst_ref=out_ref.at[mine],
                send_sem=send_sems.at[j],
                recv_sem=recv_sems.at[j],
                device_id=(px, py, c),
                device_id_type=MESH,
            )
            for j, (px, py) in enumerate(chips)
        ]
        for cp in copies:
            cp.start()
        for j, (px, py) in enumerate(chips):
            pltpu.make_async_remote_copy(
                src_ref=p_ref.at[mine],
                dst_ref=out_ref.at[2 * px + py],
                send_sem=send_sems.at[j],
                recv_sem=recv_sems.at[j],
                device_id=(px, py, c),
                device_id_type=MESH,
            ).wait_recv()
        for cp in copies:
            cp.wait_send()
        local.wait()

    return pl.pallas_call(
        body,
        name=name,
        out_shape=jax.ShapeDtypeStruct((N_CHIP, R, C), pair.dtype),
        in_specs=[ANY],
        out_specs=ANY,
        scratch_shapes=[pltpu.SemaphoreType.DMA((3,)), pltpu.SemaphoreType.DMA((3,)), pltpu.SemaphoreType.DMA(())],
    )(pair)


def _slot_sum(parts, name, tr):
    n, R, C = parts.shape

    def body(p_ref, o_ref):
        acc = p_ref[0].astype(F32)
        for k in range(1, n):
            acc = acc + p_ref[k].astype(F32)
        o_ref[...] = acc

    return pl.pallas_call(
        body,
        name=name,
        grid=(R // tr,),
        in_specs=[pl.BlockSpec((n, tr, C), lambda i: (0, i, 0))],
        out_specs=pl.BlockSpec((tr, C), lambda i: (i, 0)),
        out_shape=jax.ShapeDtypeStruct((R, C), F32),
        compiler_params=pltpu.CompilerParams(dimension_semantics=("parallel",)),
    )(parts)


def _upcast_rows(gathered, rows, name, tc=512):
    _, _, C = gathered.shape
    src_rows = -(-rows // 16) * 16

    def body(g_ref, o_ref):
        o_ref[...] = g_ref[0].astype(F32)[:rows, :]

    return pl.pallas_call(
        body,
        name=name,
        grid=(N_DEV, C // tc),
        in_specs=[pl.BlockSpec((1, src_rows, tc), lambda j, k: (j, 0, k))],
        out_specs=pl.BlockSpec((rows, tc), lambda j, k: (j, k)),
        out_shape=jax.ShapeDtypeStruct((N_DEV * rows, C), F32),
        compiler_params=pltpu.CompilerParams(dimension_semantics=("parallel", "parallel")),
    )(gathered)


def _pack_rows(parts):
    some = next(iter(parts.values()))
    lead, dtype = some.shape[:-2], some.dtype
    cols = [parts[n] if n in parts else jnp.zeros(lead + (r, PACK_COLS), dtype) for n, r in PACK_SEGS]
    return jnp.concatenate(cols, axis=-2)


def _pack_weights(w):
    def t(a):
        return a.astype(BF16).T

    return _pack_rows(
        {
            "w_in": jnp.pad(t(w["w_in"]), ((0, IN_ROWS_PAD - IN_ROWS), (0, 0))),
            "w_up": t(w["w_up"]),
            "w_out": w["w_out"].astype(BF16),
            "w_down": w["w_down"].astype(BF16),
            "w_o_mla": t(w["w_o_mla"]).reshape(-1, PACK_COLS),
            "w_o_dil": t(w["w_o_dil"]).reshape(-1, PACK_COLS),
            "w_uq": t(w["w_uq"]).reshape(-1, PACK_COLS),
            "w_ukv": t(w["w_ukv"]).reshape(-1, PACK_COLS),
        }
    )


def _seg(packed, name):
    off = 0
    for n, r in PACK_SEGS:
        if n == name:
            return packed[..., off : off + r, :]
        off += r
    raise KeyError(name)


def _unpack_weights(gathered):
    w_in_t = _upcast_rows(gathered, IN_ROWS, "w_in_upcast")
    lat_t = jnp.pad(w_in_t[:LAT], ((0, LAT_PAD - LAT), (0, 0))).astype(BF16)
    up_t = _seg(gathered, "w_up").reshape(2 * D_FF, D_MODEL)
    zrow = jnp.zeros((D_FF_PAD - D_FF, D_MODEL), BF16)
    uq_t = jnp.pad(_seg(gathered, "w_uq").reshape(MLA_HEADS, QK_NOPE + QK_ROPE, Q_LORA), ((0, 0), (0, HEAD_PAD - QK_NOPE - QK_ROPE), (0, 0)))
    wt = {
        "lat_t": lat_t,
        "dqkv_t": w_in_t[LAT : LAT + 3 * DIL_QKV].astype(BF16),
        "g_t": w_in_t[LAT + 3 * DIL_QKV :].astype(BF16),
        "uq_t": uq_t.reshape(MLA_HEADS * HEAD_PAD, Q_LORA),
        "ukv_t": _seg(gathered, "w_ukv").reshape(MLA_HEADS * HEAD_PAD, KV_LORA),
        "o_mla_t": _seg(gathered, "w_o_mla").reshape(D_MODEL, MLA_HEADS * V_HEAD),
        "o_dil_t": _seg(gathered, "w_o_dil").reshape(D_MODEL, DIL_OUT),
        "w_out": _seg(gathered, "w_out").reshape(D_MODEL, D_MODEL),
        "up_t": jnp.concatenate([up_t[:D_FF], zrow, up_t[D_FF:], zrow], axis=0),
        "w_down": jnp.concatenate([_seg(gathered, "w_down").reshape(D_FF, D_MODEL), zrow], axis=0),
    }
    return wt


def _pack_grads(g):
    w_in_t = jnp.concatenate([g["lat_t"][:LAT], g["dqkv_t"], g["ga_t"], g["gb_t"]], axis=0)
    w_in_t = jnp.pad(w_in_t.reshape(N_DEV, IN_ROWS, D_MODEL), ((0, 0), (0, IN_ROWS_PAD - IN_ROWS), (0, 0))).astype(BF16)
    up_t = jnp.concatenate([g["up_t"][:D_FF], g["up_t"][D_FF_PAD : D_FF_PAD + D_FF]], axis=0)
    uq_t = g["uq_t"].reshape(MLA_HEADS, HEAD_PAD, Q_LORA)[:, : QK_NOPE + QK_ROPE]
    return _pack_rows(
        {
            "w_in": w_in_t,
            "w_up": up_t.reshape(N_DEV, -1, PACK_COLS),
            "w_out": g["w_out"].reshape(N_DEV, -1, PACK_COLS),
            "w_down": g["w_down"][:D_FF].reshape(N_DEV, -1, PACK_COLS),
            "w_o_mla": g["o_mla_t"].reshape(N_DEV, -1, PACK_COLS),
            "w_o_dil": g["o_dil_t"].reshape(N_DEV, -1, PACK_COLS),
            "w_uq": uq_t.reshape(N_DEV, -1, PACK_COLS),
            "w_ukv": g["ukv_t"].reshape(N_DEV, -1, PACK_COLS),
        }
    )


def _unpack_grad_shards(packed):
    return {
        "w_in": _seg(packed, "w_in")[:IN_ROWS].T,
        "w_up": _seg(packed, "w_up").T,
        "w_out": _seg(packed, "w_out"),
        "w_down": _seg(packed, "w_down"),
        "w_o_mla": _seg(packed, "w_o_mla").reshape(256, 1024).T,
        "w_o_dil": _seg(packed, "w_o_dil").reshape(256, 512).T,
        "w_uq": _seg(packed, "w_uq").reshape(192, 512).T,
        "w_ukv": _seg(packed, "w_ukv").reshape(256, 256).T,
    }


def _ffn_pad(a):
    z = jnp.zeros(a.shape[:-1] + (D_FF_PAD - D_FF,), a.dtype)
    return jnp.concatenate([a[..., :D_FF], z, a[..., D_FF:], z], axis=-1)


def _ffn_unpad(a):
    return jnp.concatenate([a[..., :D_FF], a[..., D_FF_PAD : D_FF_PAD + D_FF]], axis=-1)


def _local_step(x, tgt, wt, conv_w, small):
    S = x.shape[0]
    lat_t, dqkv_t, g_t, uq_t, ukv_t = wt["lat_t"], wt["dqkv_t"], wt["g_t"], wt["uq_t"], wt["ukv_t"]
    o_mla_t, o_dil_t, w_out, up_t, w_down = wt["o_mla_t"], wt["o_dil_t"], wt["w_out"], wt["up_t"], wt["w_down"]
    cw = jnp.pad(_ffn_pad(conv_w), ((0, 5), (0, 0)))
    cb = _ffn_pad(small["conv_b"])
    cos_t, sin_t = _rope_tables(S)
    bias = _dil_bias()
    g1, g2, g3 = small["attn_norm_g"], small["ffn_norm_g"], small["final_norm_g"]
    gq, gkv = small["q_norm_g"], small["kv_norm_g"]

    h = _rms_fwd(x, g1, "rms_attn")
    lat = _mm(h, lat_t, "nt", F32, 1024, LAT_PAD, D_MODEL, "proj_lat")
    dqkv = _mm(h, dqkv_t, "nt", BF16, 1024, 512, D_MODEL, "proj_dqkv")
    gates = _mm(h, g_t, "nt", F32, 1024, 512, D_MODEL, "proj_gates", bias=small["b_gate"], act="sigmoid")
    cqn, ckvn, kpe = _mla_prep1(lat, gq, gkv, cos_t, sin_t, "mla_prep1")
    q_raw = _mm(cqn, uq_t, "nt", F32, 1024, 1024, Q_LORA, "mla_uq")
    kv = _mm(ckvn, ukv_t, "nt", BF16, 1024, 1024, KV_LORA, "mla_ukv")
    q_att, k_att = _mla_prep2(q_raw, kv, kpe, cos_t, sin_t, "mla_prep2")
    o, lse = _flash_fwd(q_att, k_att, kv, "mla_flash_fwd")
    o_a = _mm(o, o_mla_t, "nt", F32, 1024, 1024, MLA_HEADS * V_HEAD, "mla_out")

    d_os, d_ls = [], []
    for g, (_, dil) in enumerate(DIL_PATTERNS):
        og, lg = _dil_fwd_group(dqkv.reshape(S // dil, dil * 3 * DIL_QKV), bias[g], g, dil, f"dil_fwd_{g}")
        d_os.append(og.reshape(S, DIL_OUT))
        d_ls.append(lg.reshape(S, DIL_OUT))
    od, dil_lse = _dil_combine(d_os, d_ls, "dil_combine")
    o_b = _mm(od, o_dil_t, "nt", F32, 1024, 1024, DIL_OUT, "dil_out")

    mrg = _merge_fwd(gates, o_a, o_b, "merge_fwd")
    x1 = _mm(mrg, w_out, "nn", F32, 1024, 1024, D_MODEL, "mix_out", res=x)
    h2 = _rms_fwd(x1, g2, "rms_ffn")
    u0 = _mm(h2, up_t, "nt", F32, 1024, 512, D_MODEL, "ffn_up")
    a = _ffn_fwd(u0, cw, cb, "ffn_conv_fwd")
    x2 = _mm(a, w_down, "nn", F32, 1024, 512, D_FF_PAD // 2, "ffn_down", res=x1)
    loss_part, dx2, dx2b, dg3 = _final_loss(x2, g3, tgt, "final_loss")

    da = _mm(dx2b, w_down, "nt", F32, 1024, 512, D_MODEL, "ffn_down_dx")
    gw_down = _mm(a, dx2b, "tn", BF16, 512, 1024, 1024, "ffn_down_dw")
    du, dcw, dcb = _ffn_bwd_du(u0, da, cw, cb, "ffn_conv_bwd_du")
    du0 = _ffn_bwd_du0(du, cw, "ffn_conv_bwd_du0")
    dh2 = _mm(du0, up_t, "nn", F32, 1024, 1024, 1024, "ffn_up_dx")
    gw_up_t = _mm(du0, h2, "tn", BF16, 512, 1024, 1024, "ffn_up_dw")
    dx1, dx1b, dg2 = _rms_bwd(dh2, x1, g2, dx2, "rms_ffn_bwd")

    dmrg = _mm(dx1b, w_out, "nt", F32, 1024, 1024, D_MODEL, "mix_out_dx")
    gw_out = _mm(mrg, dx1b, "tn", BF16, 1024, 1024, 1024, "mix_out_dw")
    do_a, do_b, dga, dgb, dba, dbb = _merge_bwd(dmrg, gates, o_a, o_b, "merge_bwd")

    do = _mm(do_a, o_mla_t, "nn", BF16, 1024, 1024, D_MODEL, "mla_out_dx")
    gw_o_mla_t = _mm(do_a, o, "tn", BF16, 1024, 1024, 1024, "mla_out_dw")
    dod = _mm(do_b, o_dil_t, "nn", F32, 1024, DIL_OUT, D_MODEL, "dil_out_dx")
    gw_o_dil_t = _mm(do_b, od, "tn", BF16, 1024, DIL_OUT, 1024, "dil_out_dw")

    dq_att = _flash_dq(q_att, k_att, kv, o, do, lse, "mla_flash_dq")
    dk_att, dv = _flash_dkv(q_att, k_att, kv, o, do, lse, "mla_flash_dkv")
    dq_raw, dkv, dkpe = _mla_post(dq_att, dk_att, dv, cos_t, sin_t, "mla_post")
    dcqn = _mm(dq_raw, uq_t, "nn", F32, 1024, Q_LORA, MLA_HEADS * HEAD_PAD, "mla_uq_dx")
    gw_uq_t = _mm(dq_raw, cqn, "tn", BF16, 1024, Q_LORA, 1024, "mla_uq_dw")
    dckvn = _mm(dkv, ukv_t, "nn", F32, 1024, KV_LORA, MLA_HEADS * HEAD_PAD, "mla_ukv_dx")
    gw_ukv_t = _mm(dkv, ckvn, "tn", BF16, 1024, KV_LORA, 1024, "mla_ukv_dw")
    dlat, dgq, dgkv = _lat_bwd(dcqn, dckvn, dkpe, lat, gq, gkv, "lat_bwd")

    dd, dodb = _dil_rowdot(dod, od, "dil_rowdot")
    dparts = [[None] * DIL_GROUPS for _ in range(3)]
    for g, (_, dil) in enumerate(DIL_PATTERNS):
        L = S // dil
        outs = _dil_bwd_group(
            dqkv.reshape(L, dil * 3 * DIL_QKV),
            bias[g],
            dodb.reshape(L, dil * DIL_OUT),
            dd.reshape(L, dil * DIL_OUT),
            dil_lse.reshape(L, dil * DIL_OUT),
            g,
            dil,
            f"dil_bwd_{g}",
        )
        for t in range(3):
            dparts[t][g] = outs[t].reshape(S, DIL_OUT)
    ddqkv = jnp.concatenate([p for row in dparts for p in row], axis=1)

    dh = _mm(dlat, lat_t, "nn", F32, 1024, 1024, LAT_PAD, "proj_lat_dx")
    dh = _mm(ddqkv, dqkv_t, "nn", F32, 1024, 1024, 1536, "proj_dqkv_dx", res=dh)
    dh = _mm(dga, g_t, "nn", F32, 1024, 1024, D_MODEL, "proj_ga_dx", res=dh)
    dh = _mm(dgb, g_t, "nn", F32, 1024, 1024, D_MODEL, "proj_gb_dx", res=dh, b_koff=1)
    gw_lat_t = _mm(dlat, h, "tn", F32, LAT_PAD, 1024, 1024, "proj_lat_dw")
    gw_dqkv_t = _mm(ddqkv, h, "tn", F32, 512, 1024, 1024, "proj_dqkv_dw")
    gw_ga_t = _mm(dga, h, "tn", F32, 1024, 1024, 1024, "proj_ga_dw")
    gw_gb_t = _mm(dgb, h, "tn", F32, 1024, 1024, 1024, "proj_gb_dw")
    grad_x, _, dg1 = _rms_bwd(dh, x, g1, dx1, "rms_attn_bwd")

    grads = {
        "lat_t": gw_lat_t,
        "dqkv_t": gw_dqkv_t,
        "ga_t": gw_ga_t,
        "gb_t": gw_gb_t,
        "uq_t": gw_uq_t,
        "ukv_t": gw_ukv_t,
        "o_mla_t": gw_o_mla_t,
        "o_dil_t": gw_o_dil_t,
        "w_out": gw_out,
        "up_t": gw_up_t,
        "w_down": gw_down,
    }
    small_grads = {
        "attn_norm_g": dg1,
        "b_gate": jnp.concatenate([dba, dbb], axis=1),
        "q_norm_g": dgq,
        "kv_norm_g": dgkv,
        "ffn_norm_g": dg2,
        "conv_b": _ffn_unpad(dcb),
        "final_norm_g": dg3,
        "conv_w": _ffn_unpad(dcw[:3]),
    }
    return loss_part, grad_x, grads, small_grads


SMALL_ORDER = ("attn_norm_g", "b_gate", "q_norm_g", "kv_norm_g", "ffn_norm_g", "conv_b", "final_norm_g", "conv_w")
WEIGHT_ORDER = (
    "attn_norm_g", "w_in", "b_gate", "q_norm_g", "w_uq", "kv_norm_g", "w_ukv", "w_o_mla", "w_o_dil", "w_out",
    "ffn_norm_g", "w_up", "conv_w", "conv_b", "w_down", "final_norm_g",
)


def kernel(x, attn_norm_g, w_in, b_gate, q_norm_g, w_uq, kv_norm_g, w_ukv, w_o_mla, w_o_dil, w_out, ffn_norm_g, w_up, conv_w, conv_b, w_down, final_norm_g, loss_target, m_attn_norm_g, m_w_in, m_b_gate, m_q_norm_g, m_w_uq, m_kv_norm_g, m_w_ukv, m_w_o_mla, m_w_o_dil, m_w_out, m_ffn_norm_g, m_w_up, m_conv_w, m_conv_b, m_w_down, m_final_norm_g, v_attn_norm_g, v_w_in, v_b_gate, v_q_norm_g, v_w_uq, v_kv_norm_g, v_w_ukv, v_w_o_mla, v_w_o_dil, v_w_out, v_ffn_norm_g, v_w_up, v_conv_w, v_conv_b, v_w_down, v_final_norm_g):
    env = dict(locals())
    dev = 4 * lax.axis_index("x") + 2 * lax.axis_index("y") + lax.axis_index("c")
    core = lax.axis_index("c").astype(jnp.int32).reshape(1)

    def two_d(a):
        return a.reshape(-1, a.shape[-1])

    w = {n: two_d(env[n]) for n in WEIGHT_ORDER}
    m = {n: two_d(env["m_" + n]) for n in WEIGHT_ORDER}
    v = {n: two_d(env["v_" + n]) for n in WEIGHT_ORDER}

    gathered = _all_gather(_pack_weights(w), "ag_weights")
    full = _unpack_weights(gathered)
    r, c = CONV_SHARD
    conv = jnp.pad(w["conv_w"].reshape(-1), (0, 8 * SMALL_COLS - r * c)).reshape(8, SMALL_COLS)
    conv = _all_gather(conv, "ag_conv_w").reshape(N_DEV, 8 * SMALL_COLS)[:, : r * c].reshape(N_DEV, r, c)
    conv_w_full = conv.transpose(1, 0, 2).reshape(r, N_DEV * c)
    small = {n: w[n] for n in SMALL_ORDER if n != "conv_w"}

    loss_part, grad_x, grads, small_grads = _local_step(x[0], loss_target[0], full, conv_w_full, small)
    loss = lax.psum(loss_part[0, 0], AXES)

    gpack = _pack_grads(grads)
    recv = _pair_exchange(gpack, "rs_pair_exchange")
    pair = _pair_add(gpack, recv, core, "rs_pair_add")
    parts = _chip_exchange(pair, "rs_chip_exchange")
    gshard = _unpack_grad_shards(_slot_sum(parts, "rs_sum", PACK_TR))

    sflat = jnp.concatenate([small_grads[n].reshape(-1) for n in SMALL_ORDER])
    sflat = jnp.pad(sflat, (0, SMALL_ROWS * SMALL_COLS - sflat.shape[0])).reshape(SMALL_ROWS, SMALL_COLS)
    ssum = _slot_sum(_all_gather(sflat, "ag_small_grads"), "small_sum", SMALL_ROWS).reshape(-1)
    gsmall, off = {}, 0
    for n in SMALL_ORDER:
        shape = (3, 2 * D_FF) if n == "conv_w" else w[n].shape
        size = shape[0] * shape[1]
        gsmall[n] = ssum[off : off + size].reshape(shape)
        off += size
    gsmall["conv_w"] = lax.dynamic_slice_in_dim(gsmall["conv_w"], dev * CONV_SHARD[1], CONV_SHARD[1], axis=1)

    g_all = {**gshard, **gsmall}
    out_g, out_d, out_m, out_v = [], [], [], []
    for n in WEIGHT_ORDER:
        d, nm, nv = _adamw(w[n], g_all[n], m[n], v[n], "adamw_" + n)
        shape = env[n].shape
        out_g.append(g_all[n].reshape(shape))
        out_d.append(d.reshape(shape))
        out_m.append(nm.reshape(shape))
        out_v.append(nv.reshape(shape))
    return (loss, grad_x[None], *out_g, *out_d, *out_m, *out_v)
```

```python
import functools

import jax
import jax.numpy as jnp
import numpy as np
from jax import lax
from jax.experimental import pallas as pl
from jax.experimental.pallas import tpu as pltpu

F32 = jnp.float32
BF16 = jnp.bfloat16

N_DEV = 8
N_CHIP = 4
AXES = ("x", "y", "c")
MESH = pl.DeviceIdType.MESH

D_MODEL = 2048
MLA_HEADS = 8
QK_NOPE = 128
QK_ROPE = 64
V_HEAD = 128
Q_LORA = 512
KV_LORA = 256
ROPE_THETA = 10000.0
HEAD_PAD = 256
DIL_PATTERNS = ((128, 1), (512, 4), (2048, 16))
DIL_GROUPS = 3
DIL_HG = 4
DIL_HEADS = 12
DIL_HD = 128
DIL_BLK = 128
DIL_QKV = DIL_HEADS * DIL_HD
DIL_OUT = DIL_HG * DIL_HD
ALIBI_MAX_BIAS = 8.0
D_FF = 5504
D_FF_PAD = 5632
NORM_EPS = 1e-6
LAT = Q_LORA + KV_LORA + QK_ROPE
LAT_PAD = 896
D_IN = LAT + 3 * DIL_QKV + 2 * D_MODEL
NEG = -1e30

ADAM_LR = 0.001
ADAM_B1 = 0.9
ADAM_B2 = 0.999
ADAM_EPS = 1e-08
ADAM_WD = 0.01
ADAM_STEP = 10

SMALL_ROWS = 56
SMALL_COLS = 1024

IN_ROWS = 1192
IN_ROWS_PAD = 1200
FF_GROUP = D_FF // N_DEV
FF_GROUP_PAD = D_FF_PAD // N_DEV
CONV_SHARD = (3, 1376)

NT = (((1,), (1,)), ((), ()))
TN = (((0,), (0,)), ((), ()))


def _dot(a, b, dims=(((1,), (0,)), ((), ()))):
    return lax.dot_general(a, b, dims, preferred_element_type=F32)


def _mm(a, b, mode, out_dtype, tm, tn, tk, name, bias=None, act=None, res=None, b_koff=0, a_halves=1):
    H = a_halves
    if mode == "nn":
        (M, K), (K2, N) = (a.shape[0] // H, a.shape[1] * H), b.shape
        assert (b_koff + 1) * K <= K2, (name, a.shape, b.shape)
        koff, K2 = b_koff * (K // tk), K
        kper, mrows = a.shape[1] // tk, M // tm
        a_spec = pl.BlockSpec((tm, tk), lambda i, j, k: (i + (k // kper) * mrows, k % kper))
        b_spec = pl.BlockSpec((tk, tn), lambda i, j, k: (k + koff, j))
        dims = (((1,), (0,)), ((), ()))
    elif mode == "nt":
        (M, K), (N, K2) = a.shape, b.shape
        a_spec = pl.BlockSpec((tm, tk), lambda i, j, k: (i, k))
        b_spec = pl.BlockSpec((tn, tk), lambda i, j, k: (j, k))
        dims = NT
    else:
        (K, M), (K2, N) = (a.shape[0] // H, a.shape[1] * H), b.shape
        mper, krows = a.shape[1] // tm, K // tk
        a_spec = pl.BlockSpec((tk, tm), lambda i, j, k: (k + (i // mper) * krows, i % mper))
        b_spec = pl.BlockSpec((tk, tn), lambda i, j, k: (k, j))
        dims = TN
    assert K == K2 and M % tm == 0 and N % tn == 0 and K % tk == 0, (name, a.shape, b.shape)
    nk = K // tk
    has_bias, has_res = bias is not None, res is not None

    def body(*refs):
        refs = list(refs)
        a_ref, b_ref = refs[0], refs[1]
        pos = 2
        bias_ref = res_ref = None
        if has_bias:
            bias_ref = refs[pos]
            pos += 1
        if has_res:
            res_ref = refs[pos]
            pos += 1
        o_ref = refs[pos]
        p = _dot(a_ref[...].astype(BF16), b_ref[...].astype(BF16), dims)

        def finish(acc):
            if has_bias:
                acc = acc + bias_ref[...]
            if act == "sigmoid":
                acc = jax.nn.sigmoid(acc)
            if has_res:
                acc = res_ref[...] + acc
            o_ref[...] = acc.astype(o_ref.dtype)

        if nk == 1:
            finish(p)
        else:
            acc_ref = refs[pos + 1]
            k = pl.program_id(2)

            @pl.when(k == 0)
            def _():
                acc_ref[...] = p

            @pl.when(k != 0)
            def _():
                acc_ref[...] += p

            @pl.when(k == nk - 1)
            def _():
                finish(acc_ref[...])

    in_specs = [a_spec, b_spec]
    args = [a, b]
    if has_bias:
        in_specs.append(pl.BlockSpec((1, tn), lambda i, j, k: (0, j)))
        args.append(bias)
    if has_res:
        in_specs.append(pl.BlockSpec((tm, tn), lambda i, j, k: (i, j)))
        args.append(res)
    return pl.pallas_call(
        body,
        name=name,
        grid=(M // tm, N // tn, nk),
        in_specs=in_specs,
        out_specs=pl.BlockSpec((tm, tn), lambda i, j, k: (i, j)),
        out_shape=jax.ShapeDtypeStruct((M, N), out_dtype),
        scratch_shapes=[pltpu.VMEM((tm, tn), F32)] if nk > 1 else [],
        compiler_params=pltpu.CompilerParams(dimension_semantics=("parallel", "parallel", "arbitrary")),
    )(*args)


def _rstd(x):
    return lax.rsqrt(jnp.mean(x * x, axis=-1, keepdims=True) + NORM_EPS)


def _rms_bwd_math(dy, x, g):
    r = _rstd(x)
    xh = x * r
    dg = jnp.sum(dy * xh, axis=0, keepdims=True)
    dxh = dy * g
    dx = r * (dxh - xh * jnp.mean(dxh * xh, axis=-1, keepdims=True))
    return dx, dg


def _rms_fwd(x, g, name, tr=256):
    S, D = x.shape

    def body(x_ref, g_ref, o_ref):
        xv = x_ref[...]
        o_ref[...] = ((xv * _rstd(xv)) * g_ref[...]).astype(o_ref.dtype)

    return pl.pallas_call(
        body,
        name=name,
        grid=(S // tr,),
        in_specs=[pl.BlockSpec((tr, D), lambda i: (i, 0)), pl.BlockSpec((1, D), lambda i: (0, 0))],
        out_specs=pl.BlockSpec((tr, D), lambda i: (i, 0)),
        out_shape=jax.ShapeDtypeStruct((S, D), BF16),
        compiler_params=pltpu.CompilerParams(dimension_semantics=("parallel",)),
    )(x, g)


def _rms_bwd(dy, x, g, res, name, tr=256):
    S, D = x.shape

    def body(dy_ref, x_ref, g_ref, res_ref, dx_ref, dxb_ref, dg_ref):
        dx, dg = _rms_bwd_math(dy_ref[...], x_ref[...], g_ref[...])
        dx = dx + res_ref[...]
        dx_ref[...] = dx
        dxb_ref[...] = dx.astype(BF16)

        @pl.when(pl.program_id(0) == 0)
        def _():
            dg_ref[...] = dg

        @pl.when(pl.program_id(0) != 0)
        def _():
            dg_ref[...] += dg

    row = pl.BlockSpec((tr, D), lambda i: (i, 0))
    vec = pl.BlockSpec((1, D), lambda i: (0, 0))
    return pl.pallas_call(
        body,
        name=name,
        grid=(S // tr,),
        in_specs=[row, row, vec, row],
        out_specs=[row, row, vec],
        out_shape=[jax.ShapeDtypeStruct((S, D), F32), jax.ShapeDtypeStruct((S, D), BF16), jax.ShapeDtypeStruct((1, D), F32)],
        compiler_params=pltpu.CompilerParams(dimension_semantics=("arbitrary",)),
    )(dy, x, g, res)


def _final_loss(x2, g, tgt, name, tr=256):
    S, D = x2.shape

    def body(x_ref, g_ref, t_ref, loss_ref, dx_ref, dxb_ref, dg_ref):
        xv, gv = x_ref[...], g_ref[...]
        y = (xv * _rstd(xv)) * gv
        e = y - t_ref[...]
        part = 0.5 * jnp.sum(jnp.mean(e * e, axis=-1, keepdims=True), axis=0, keepdims=True)
        dx, dg = _rms_bwd_math(e * (1.0 / D), xv, gv)
        dx_ref[...] = dx
        dxb_ref[...] = dx.astype(BF16)
        part = jnp.broadcast_to(part, (1, 128))

        @pl.when(pl.program_id(0) == 0)
        def _():
            dg_ref[...] = dg
            loss_ref[...] = part

        @pl.when(pl.program_id(0) != 0)
        def _():
            dg_ref[...] += dg
            loss_ref[...] += part

    row = pl.BlockSpec((tr, D), lambda i: (i, 0))
    vec = pl.BlockSpec((1, D), lambda i: (0, 0))
    return pl.pallas_call(
        body,
        name=name,
        grid=(S // tr,),
        in_specs=[row, vec, row],
        out_specs=[pl.BlockSpec((1, 128), lambda i: (0, 0)), row, row, vec],
        out_shape=[
            jax.ShapeDtypeStruct((1, 128), F32),
            jax.ShapeDtypeStruct((S, D), F32),
            jax.ShapeDtypeStruct((S, D), BF16),
            jax.ShapeDtypeStruct((1, D), F32),
        ],
        compiler_params=pltpu.CompilerParams(dimension_semantics=("arbitrary",)),
    )(x2, g, tgt)


def _rope_tables(S):
    pos = jnp.arange(S, dtype=F32)
    inv_freq = ROPE_THETA ** (-jnp.arange(0, QK_ROPE, 2, dtype=F32) / QK_ROPE)
    ang = pos[:, None] * inv_freq[None, :]
    cos, sin = jnp.cos(ang), jnp.sin(ang)
    zero = jnp.zeros((S, 128 - QK_ROPE), F32)
    return jnp.concatenate([cos, cos, zero], axis=1), jnp.concatenate([-sin, sin, zero], axis=1)


def _rope_tile(x, cos_t, sin_t):
    lane = lax.broadcasted_iota(jnp.int32, x.shape, 1)
    partner = jnp.where(lane < QK_ROPE // 2, pltpu.roll(x, 128 - QK_ROPE // 2, 1), pltpu.roll(x, QK_ROPE // 2, 1))
    return x * cos_t + partner * sin_t


def _mla_prep1(lat, gq, gkv, cos_t, sin_t, name, tr=256):
    S = lat.shape[0]

    def body(lat_ref, gq_ref, gkv_ref, cos_ref, sin_ref, cq_ref, ckv_ref, kpe_ref):
        cq = lat_ref[:, :Q_LORA]
        ckv = lat_ref[:, Q_LORA : Q_LORA + KV_LORA]
        cq_ref[...] = ((cq * _rstd(cq)) * gq_ref[...]).astype(BF16)
        ckv_ref[...] = ((ckv * _rstd(ckv)) * gkv_ref[...]).astype(BF16)
        kpe_ref[...] = _rope_tile(lat_ref[:, Q_LORA + KV_LORA :], cos_ref[...], sin_ref[...]).astype(BF16)

    def row(n):
        return pl.BlockSpec((tr, n), lambda i: (i, 0))

    def vec(n):
        return pl.BlockSpec((1, n), lambda i: (0, 0))

    return pl.pallas_call(
        body,
        name=name,
        grid=(S // tr,),
        in_specs=[row(LAT_PAD), vec(Q_LORA), vec(KV_LORA), row(128), row(128)],
        out_specs=[row(Q_LORA), row(KV_LORA), row(128)],
        out_shape=[
            jax.ShapeDtypeStruct((S, Q_LORA), BF16),
            jax.ShapeDtypeStruct((S, KV_LORA), BF16),
            jax.ShapeDtypeStruct((S, 128), BF16),
        ],
        compiler_params=pltpu.CompilerParams(dimension_semantics=("parallel",)),
    )(lat, gq, gkv, cos_t, sin_t)


def _mla_prep2(q_raw, kv, kpe, cos_t, sin_t, name, tr=256):
    S = q_raw.shape[0]
    W = MLA_HEADS * HEAD_PAD

    def body(q_ref, kv_ref, kpe_ref, cos_ref, sin_ref, qa_ref, ka_ref):
        cos_v, sin_v, kpe_v = cos_ref[...], sin_ref[...], kpe_ref[...]
        for h in range(MLA_HEADS):
            lo = h * HEAD_PAD
            qa_ref[:, lo : lo + 128] = q_ref[:, lo : lo + 128].astype(BF16)
            qa_ref[:, lo + 128 : lo + 256] = _rope_tile(q_ref[:, lo + 128 : lo + 256], cos_v, sin_v).astype(BF16)
            ka_ref[:, lo : lo + 128] = kv_ref[:, lo : lo + 128]
            ka_ref[:, lo + 128 : lo + 256] = kpe_v

    def row(n):
        return pl.BlockSpec((tr, n), lambda i: (i, 0))

    return pl.pallas_call(
        body,
        name=name,
        grid=(S // tr,),
        in_specs=[row(W), row(W), row(128), row(128), row(128)],
        out_specs=[row(W), row(W)],
        out_shape=[jax.ShapeDtypeStruct((S, W), BF16), jax.ShapeDtypeStruct((S, W), BF16)],
        compiler_params=pltpu.CompilerParams(dimension_semantics=("parallel",)),
    )(q_raw, kv, kpe, cos_t, sin_t)


def _mla_post(dq_att, dk_att, dv, cos_t, sin_t, name, tr=256):
    S = dq_att.shape[0]
    W = MLA_HEADS * HEAD_PAD

    def body(dq_ref, dk_ref, dv_ref, cos_ref, sin_ref, dqr_ref, dkv_ref, dkpe_ref):
        cos_v, nsin_v = cos_ref[...], -sin_ref[...]
        kpe = jnp.zeros((tr, 128), F32)
        for h in range(MLA_HEADS):
            lo = h * HEAD_PAD
            dqr_ref[:, lo : lo + 128] = dq_ref[:, lo : lo + 128].astype(BF16)
            dqr_ref[:, lo + 128 : lo + 256] = _rope_tile(dq_ref[:, lo + 128 : lo + 256], cos_v, nsin_v).astype(BF16)
            dkv_ref[:, lo : lo + 128] = dk_ref[:, lo : lo + 128].astype(BF16)
            dkv_ref[:, lo + 128 : lo + 256] = dv_ref[:, h * 128 : (h + 1) * 128].astype(BF16)
            kpe = kpe + dk_ref[:, lo + 128 : lo + 256]
        dkpe_ref[...] = _rope_tile(kpe, cos_v, nsin_v)

    def row(n):
        return pl.BlockSpec((tr, n), lambda i: (i, 0))

    return pl.pallas_call(
        body,
        name=name,
        grid=(S // tr,),
        in_specs=[row(W), row(W), row(MLA_HEADS * V_HEAD), row(128), row(128)],
        out_specs=[row(W), row(W), row(128)],
        out_shape=[jax.ShapeDtypeStruct((S, W), BF16), jax.ShapeDtypeStruct((S, W), BF16), jax.ShapeDtypeStruct((S, 128), F32)],
        compiler_params=pltpu.CompilerParams(dimension_semantics=("parallel",)),
    )(dq_att, dk_att, dv, cos_t, sin_t)


def _lat_bwd(dcqn, dckvn, dkpe, lat, gq, gkv, name, tr=256):
    S = lat.shape[0]

    def body(dcq_ref, dckv_ref, dkpe_ref, lat_ref, gq_ref, gkv_ref, dlat_ref, dgq_ref, dgkv_ref):
        dq, dgq = _rms_bwd_math(dcq_ref[...], lat_ref[:, :Q_LORA], gq_ref[...])
        dkv, dgkv = _rms_bwd_math(dckv_ref[...], lat_ref[:, Q_LORA : Q_LORA + KV_LORA], gkv_ref[...])
        dlat_ref[:, :Q_LORA] = dq.astype(BF16)
        dlat_ref[:, Q_LORA : Q_LORA + KV_LORA] = dkv.astype(BF16)
        dlat_ref[:, Q_LORA + KV_LORA :] = dkpe_ref[...].astype(BF16)

        @pl.when(pl.program_id(0) == 0)
        def _():
            dgq_ref[...] = dgq
            dgkv_ref[...] = dgkv

        @pl.when(pl.program_id(0) != 0)
        def _():
            dgq_ref[...] += dgq
            dgkv_ref[...] += dgkv

    def row(n):
        return pl.BlockSpec((tr, n), lambda i: (i, 0))

    def vec(n):
        return pl.BlockSpec((1, n), lambda i: (0, 0))

    return pl.pallas_call(
        body,
        name=name,
        grid=(S // tr,),
        in_specs=[row(Q_LORA), row(KV_LORA), row(128), row(LAT_PAD), vec(Q_LORA), vec(KV_LORA)],
        out_specs=[row(LAT_PAD), vec(Q_LORA), vec(KV_LORA)],
        out_shape=[
            jax.ShapeDtypeStruct((S, LAT_PAD), BF16),
            jax.ShapeDtypeStruct((1, Q_LORA), F32),
            jax.ShapeDtypeStruct((1, KV_LORA), F32),
        ],
        compiler_params=pltpu.CompilerParams(dimension_semantics=("arbitrary",)),
    )(dcqn, dckvn, dkpe, lat, gq, gkv)


MLA_SCALE = (QK_NOPE + QK_ROPE) ** -0.5
LOG2E = 1.4426950408889634
MLA_C2 = MLA_SCALE * LOG2E
FLASH_T = 1024


def _causal_pairs(n, by_key):
    pairs = [(i, j) for j in range(n) for i in range(j, n)] if by_key else [(i, j) for i in range(n) for j in range(i + 1)]
    return jnp.asarray([p[0] for p in pairs], jnp.int32), jnp.asarray([p[1] for p in pairs], jnp.int32)


def _lanes(x, n):
    return jnp.tile(x, (1, n // 128))


def _flash_grid(npairs, in_specs, out_specs, scratch):
    return pltpu.PrefetchScalarGridSpec(
        num_scalar_prefetch=2, grid=(MLA_HEADS, npairs), in_specs=in_specs, out_specs=out_specs, scratch_shapes=scratch
    )


def _flash2_fwd(q_att, k_att, kv, name, t=FLASH_T):
    S = q_att.shape[0]
    qi_tab, kj_tab = _causal_pairs(S // t, by_key=False)

    def body(qi_ref, kj_ref, q_ref, k_ref, v_ref, o_ref, lse_ref, m_sc, l_sc, acc_sc):
        step = pl.program_id(1)
        qi, kj = qi_ref[step], kj_ref[step]

        @pl.when(kj == 0)
        def _():
            m_sc[...] = jnp.full((t, 128), NEG, F32)
            l_sc[...] = jnp.zeros((t, 128), F32)
            acc_sc[...] = jnp.zeros((t, V_HEAD), F32)

        def update(s):
            m_prev = m_sc[...]
            m_new = jnp.maximum(m_prev, jnp.max(s, axis=1, keepdims=True))
            p = jnp.exp2((s - _lanes(m_new, t)) * MLA_C2)
            alpha = jnp.exp2((m_prev - m_new) * MLA_C2)
            l_sc[...] = alpha * l_sc[...] + jnp.sum(p, axis=1, keepdims=True)
            acc_sc[...] = alpha * acc_sc[...] + _dot(p.astype(BF16), v_ref[...])
            m_sc[...] = m_new

        @pl.when(kj < qi)
        def _():
            update(_dot(q_ref[...], k_ref[...], NT))

        @pl.when(kj == qi)
        def _():
            s = _dot(q_ref[...], k_ref[...], NT)
            rows = lax.broadcasted_iota(jnp.int32, s.shape, 0)
            cols = lax.broadcasted_iota(jnp.int32, s.shape, 1)
            update(jnp.where(cols <= rows, s, NEG))
            l = l_sc[...]
            o_ref[...] = acc_sc[...] / l
            lse_ref[0] = m_sc[...] * MLA_SCALE + jnp.log(l)

    return pl.pallas_call(
        body,
        name=name,
        grid_spec=_flash_grid(
            qi_tab.shape[0],
            [
                pl.BlockSpec((t, HEAD_PAD), lambda h, p, qi, kj: (qi[p], h)),
                pl.BlockSpec((t, HEAD_PAD), lambda h, p, qi, kj: (kj[p], h)),
                pl.BlockSpec((t, V_HEAD), lambda h, p, qi, kj: (kj[p], 2 * h + 1)),
            ],
            [
                pl.BlockSpec((t, V_HEAD), lambda h, p, qi, kj: (qi[p], h)),
                pl.BlockSpec((1, t, 128), lambda h, p, qi, kj: (h, qi[p], 0)),
            ],
            [pltpu.VMEM((t, 128), F32), pltpu.VMEM((t, 128), F32), pltpu.VMEM((t, V_HEAD), F32)],
        ),
        out_shape=[jax.ShapeDtypeStruct((S, MLA_HEADS * V_HEAD), F32), jax.ShapeDtypeStruct((MLA_HEADS, S, 128), F32)],
        compiler_params=pltpu.CompilerParams(dimension_semantics=("parallel", "arbitrary")),
    )(qi_tab, kj_tab, q_att, k_att, kv)


def _flash2_dq(q_att, k_att, kv, o, do, lse, name, t=FLASH_T):
    S = q_att.shape[0]
    qi_tab, kj_tab = _causal_pairs(S // t, by_key=False)

    def body(qi_ref, kj_ref, q_ref, k_ref, v_ref, o_ref, do_ref, lse_ref, dq_ref, dl_ref, acc_sc):
        step = pl.program_id(1)
        qi, kj = qi_ref[step], kj_ref[step]

        @pl.when(kj == 0)
        def _():
            acc_sc[...] = jnp.zeros((t, HEAD_PAD), F32)
            dl = jnp.sum(do_ref[...].astype(F32) * o_ref[...], axis=1, keepdims=True)
            dl_ref[0] = jnp.broadcast_to(dl, (t, 128))

        def update(s):
            k = k_ref[...]
            p = jnp.exp2(s * MLA_C2 - _lanes(lse_ref[0] * LOG2E, t))
            dp = _dot(do_ref[...], v_ref[...], NT)
            ds = p * (dp - _lanes(dl_ref[0], t))
            acc_sc[...] += _dot(ds.astype(BF16), k)

        @pl.when(kj < qi)
        def _():
            update(_dot(q_ref[...], k_ref[...], NT))

        @pl.when(kj == qi)
        def _():
            s = _dot(q_ref[...], k_ref[...], NT)
            rows = lax.broadcasted_iota(jnp.int32, s.shape, 0)
            cols = lax.broadcasted_iota(jnp.int32, s.shape, 1)
            update(jnp.where(cols <= rows, s, NEG))
            dq_ref[...] = acc_sc[...] * MLA_SCALE

    qrow = lambda h, p, qi, kj: (qi[p], h)
    stat = pl.BlockSpec((1, t, 128), lambda h, p, qi, kj: (h, qi[p], 0))
    return pl.pallas_call(
        body,
        name=name,
        grid_spec=_flash_grid(
            qi_tab.shape[0],
            [
                pl.BlockSpec((t, HEAD_PAD), qrow),
                pl.BlockSpec((t, HEAD_PAD), lambda h, p, qi, kj: (kj[p], h)),
                pl.BlockSpec((t, V_HEAD), lambda h, p, qi, kj: (kj[p], 2 * h + 1)),
                pl.BlockSpec((t, V_HEAD), qrow),
                pl.BlockSpec((t, V_HEAD), qrow),
                stat,
            ],
            [pl.BlockSpec((t, HEAD_PAD), qrow), stat],
            [pltpu.VMEM((t, HEAD_PAD), F32)],
        ),
        out_shape=[jax.ShapeDtypeStruct((S, MLA_HEADS * HEAD_PAD), F32), jax.ShapeDtypeStruct((MLA_HEADS, S, 128), F32)],
        compiler_params=pltpu.CompilerParams(dimension_semantics=("parallel", "arbitrary")),
    )(qi_tab, kj_tab, q_att, k_att, kv, o, do, lse)


def _flash2_dkv(q_att, k_att, kv, do, lse_row, delta_row, name, t=FLASH_T):
    S = q_att.shape[0]
    n = S // t
    qi_tab, kj_tab = _causal_pairs(n, by_key=True)

    def body(qi_ref, kj_ref, q_ref, k_ref, v_ref, do_ref, lse_ref, dl_ref, dk_ref, dv_ref, dk_sc, dv_sc):
        step = pl.program_id(1)
        qi, kj = qi_ref[step], kj_ref[step]

        def update(st):
            q, do_v = q_ref[...], do_ref[...]
            pt = jnp.exp2(st * MLA_C2 - lse_ref[0] * LOG2E)
            dv_sc[...] += _dot(pt.astype(BF16), do_v)
            dpt = _dot(v_ref[...], do_v, NT)
            dst = pt * (dpt - dl_ref[0])
            dk_sc[...] += _dot(dst.astype(BF16), q)

        @pl.when(qi == kj)
        def _():
            dk_sc[...] = jnp.zeros((t, HEAD_PAD), F32)
            dv_sc[...] = jnp.zeros((t, V_HEAD), F32)
            st = _dot(k_ref[...], q_ref[...], NT)
            keys = lax.broadcasted_iota(jnp.int32, st.shape, 0)
            qs = lax.broadcasted_iota(jnp.int32, st.shape, 1)
            update(jnp.where(keys <= qs, st, NEG))

        @pl.when(qi > kj)
        def _():
            update(_dot(k_ref[...], q_ref[...], NT))

        @pl.when(qi == n - 1)
        def _():
            dk_ref[...] = dk_sc[...] * MLA_SCALE
            dv_ref[...] = dv_sc[...]

    qrow = lambda h, p, qi, kj: (qi[p], h)
    krow = lambda h, p, qi, kj: (kj[p], h)
    stat = pl.BlockSpec((1, 1, t), lambda h, p, qi, kj: (h, 0, qi[p]))
    return pl.pallas_call(
        body,
        name=name,
        grid_spec=_flash_grid(
            qi_tab.shape[0],
            [
                pl.BlockSpec((t, HEAD_PAD), qrow),
                pl.BlockSpec((t, HEAD_PAD), krow),
                pl.BlockSpec((t, V_HEAD), lambda h, p, qi, kj: (kj[p], 2 * h + 1)),
                pl.BlockSpec((t, V_HEAD), qrow),
                stat,
                stat,
            ],
            [pl.BlockSpec((t, HEAD_PAD), krow), pl.BlockSpec((t, V_HEAD), krow)],
            [pltpu.VMEM((t, HEAD_PAD), F32), pltpu.VMEM((t, V_HEAD), F32)],
        ),
        out_shape=[jax.ShapeDtypeStruct((S, MLA_HEADS * HEAD_PAD), F32), jax.ShapeDtypeStruct((S, MLA_HEADS * V_HEAD), F32)],
        compiler_params=pltpu.CompilerParams(dimension_semantics=("parallel", "arbitrary")),
    )(qi_tab, kj_tab, q_att, k_att, kv, do, lse_row, delta_row)


DIL_SCALE = DIL_HD**-0.5


def _dil_bias():
    slopes = 2.0 ** (-ALIBI_MAX_BIAS * np.arange(1, DIL_HEADS + 1, dtype=np.float64) / DIL_HEADS)
    slopes = slopes.astype(np.float32).reshape(DIL_GROUPS, DIL_HG)
    p = np.arange(DIL_BLK)[:, None]
    kidx = np.arange(2 * DIL_BLK)[None, :]
    j = p + DIL_BLK - kidx
    out = np.zeros((DIL_GROUPS, DIL_HG, DIL_BLK, 2 * DIL_BLK), np.float32)
    for g, (window, dil) in enumerate(DIL_PATTERNS):
        valid = (j >= 0) & (j <= window // dil)
        for h in range(DIL_HG):
            alibi = -slopes[g, h] * (dil * j).astype(np.float32)
            out[g, h] = np.where(valid, alibi, np.float32(NEG))
    return jnp.asarray(out)


def _dil_fwd_group(view, bias_g, g, dil, name):
    L = view.shape[0]
    nb = L // DIL_BLK
    nqc = 3 * DIL_QKV // 128

    def body(bias_ref, q_ref, k_ref, v_ref, o_ref, lse_ref):
        bias = bias_ref[0]

        def attend(q, kk, vv, b):
            s = _dot(q, kk, NT) * DIL_SCALE + b
            m = jnp.max(s, axis=1, keepdims=True)
            e = jnp.exp(s - m)
            l = jnp.sum(e, axis=1, keepdims=True)
            p = e * (1.0 / l)
            return _dot(p.astype(BF16), vv), m + jnp.log(l)

        o0, l0 = attend(q_ref[0:DIL_BLK, :], k_ref[0:DIL_BLK, :], v_ref[0:DIL_BLK, :], bias[:, DIL_BLK:])
        o_ref[0:DIL_BLK, :] = o0
        lse_ref[0:DIL_BLK, :] = jnp.broadcast_to(l0, (DIL_BLK, 128))

        def step(n, carry):
            r0 = pl.multiple_of(n * DIL_BLK, DIL_BLK)
            p0 = pl.multiple_of((n - 1) * DIL_BLK, DIL_BLK)
            o, l = attend(q_ref[pl.ds(r0, DIL_BLK), :], k_ref[pl.ds(p0, 2 * DIL_BLK), :], v_ref[pl.ds(p0, 2 * DIL_BLK), :], bias)
            o_ref[pl.ds(r0, DIL_BLK), :] = o
            lse_ref[pl.ds(r0, DIL_BLK), :] = jnp.broadcast_to(l, (DIL_BLK, 128))
            return carry

        lax.fori_loop(1, nb, step, 0)

    def col(base):
        return pl.BlockSpec((L, 128), lambda i: (0, (i // DIL_HG) * nqc + base + g * DIL_HG + i % DIL_HG))

    out = pl.BlockSpec((L, 128), lambda i: (0, i))
    return pl.pallas_call(
        body,
        name=name,
        grid=(dil * DIL_HG,),
        in_specs=[pl.BlockSpec((1, DIL_BLK, 2 * DIL_BLK), lambda i: (i % DIL_HG, 0, 0)), col(0), col(DIL_QKV // 128), col(2 * DIL_QKV // 128)],
        out_specs=[out, out],
        out_shape=[jax.ShapeDtypeStruct((L, dil * DIL_OUT), F32), jax.ShapeDtypeStruct((L, dil * DIL_OUT), F32)],
        compiler_params=pltpu.CompilerParams(dimension_semantics=("parallel",)),
    )(bias_g, view, view, view)


def _dil_combine(os_, ls_, name, tr=512):
    S = os_[0].shape[0]

    def body(o0, o1, o2, l0, l1, l2, out_ref, lse_ref):
        a, b, c = l0[...], l1[...], l2[...]
        m = jnp.maximum(jnp.maximum(a, b), c)
        ea, eb, ec = jnp.exp(a - m), jnp.exp(b - m), jnp.exp(c - m)
        den = ea + eb + ec
        inv = 1.0 / den
        out_ref[...] = (ea * inv) * o0[...] + (eb * inv) * o1[...] + (ec * inv) * o2[...]
        lse_ref[...] = m + jnp.log(den)

    row = pl.BlockSpec((tr, DIL_OUT), lambda i: (i, 0))
    return pl.pallas_call(
        body,
        name=name,
        grid=(S // tr,),
        in_specs=[row] * 6,
        out_specs=[row, row],
        out_shape=[jax.ShapeDtypeStruct((S, DIL_OUT), F32)] * 2,
        compiler_params=pltpu.CompilerParams(dimension_semantics=("parallel",)),
    )(*os_, *ls_)


def _dil_rowdot(dod, od, name, tr=512):
    S = dod.shape[0]

    def body(d_ref, o_ref, dd_ref, db_ref):
        db_ref[...] = d_ref[...].astype(BF16)
        for h in range(DIL_HG):
            sl = slice(h * 128, (h + 1) * 128)
            sm = jnp.sum(d_ref[:, sl] * o_ref[:, sl], axis=1, keepdims=True)
            dd_ref[:, sl] = jnp.broadcast_to(sm, (tr, 128))

    row = pl.BlockSpec((tr, DIL_OUT), lambda i: (i, 0))
    return pl.pallas_call(
        body,
        name=name,
        grid=(S // tr,),
        in_specs=[row, row],
        out_specs=[row, row],
        out_shape=[jax.ShapeDtypeStruct((S, DIL_OUT), F32), jax.ShapeDtypeStruct((S, DIL_OUT), BF16)],
        compiler_params=pltpu.CompilerParams(dimension_semantics=("parallel",)),
    )(dod, od)


def _dil_bwd_group(view, bias_g, do_view, dd_view, lse_view, g, dil, name):
    L = view.shape[0]
    nb = L // DIL_BLK
    nqc = 3 * DIL_QKV // 128

    def body(bias_ref, q_ref, k_ref, v_ref, do_ref, dd_ref, lse_ref, dq_ref, dk_ref, dv_ref, dk_sc, dv_sc):
        bias = bias_ref[0]
        dk_sc[...] = jnp.zeros((L, 128), F32)
        dv_sc[...] = jnp.zeros((L, 128), F32)

        def grads(q, kk, vv, do, dd, lse, b):
            s = _dot(q, kk, NT) * DIL_SCALE + b
            p = jnp.exp(s - lse)
            dp = _dot(do, vv, NT)
            ds = ((p * (dp - dd)) * DIL_SCALE).astype(BF16)
            return _dot(ds, kk), _dot(ds, q, TN), _dot(p.astype(BF16), do, TN)

        first = slice(0, DIL_BLK)
        dq0, dk0, dv0 = grads(
            q_ref[first, :], k_ref[first, :], v_ref[first, :], do_ref[first, :], dd_ref[first, 0:1], lse_ref[first, 0:1], bias[:, DIL_BLK:]
        )
        dq_ref[first, :] = dq0.astype(BF16)
        dk_sc[first, :] += dk0
        dv_sc[first, :] += dv0

        def step(n, carry):
            r0 = pl.multiple_of(n * DIL_BLK, DIL_BLK)
            p0 = pl.multiple_of((n - 1) * DIL_BLK, DIL_BLK)
            cur, both = pl.ds(r0, DIL_BLK), pl.ds(p0, 2 * DIL_BLK)
            dq, dk, dv = grads(q_ref[cur, :], k_ref[both, :], v_ref[both, :], do_ref[cur, :], dd_ref[cur, 0:1], lse_ref[cur, 0:1], bias)
            dq_ref[cur, :] = dq.astype(BF16)
            dk_sc[both, :] += dk
            dv_sc[both, :] += dv
            return carry

        lax.fori_loop(1, nb, step, 0)
        dk_ref[...] = dk_sc[...].astype(BF16)
        dv_ref[...] = dv_sc[...].astype(BF16)

    def col(base):
        return pl.BlockSpec((L, 128), lambda i: (0, (i // DIL_HG) * nqc + base + g * DIL_HG + i % DIL_HG))

    out = pl.BlockSpec((L, 128), lambda i: (0, i))
    return pl.pallas_call(
        body,
        name=name,
        grid=(dil * DIL_HG,),
        in_specs=[
            pl.BlockSpec((1, DIL_BLK, 2 * DIL_BLK), lambda i: (i % DIL_HG, 0, 0)),
            col(0),
            col(DIL_QKV // 128),
            col(2 * DIL_QKV // 128),
            out,
            out,
            out,
        ],
        out_specs=[out, out, out],
        out_shape=[jax.ShapeDtypeStruct((L, dil * DIL_OUT), BF16)] * 3,
        scratch_shapes=[pltpu.VMEM((L, 128), F32), pltpu.VMEM((L, 128), F32)],
        compiler_params=pltpu.CompilerParams(dimension_semantics=("parallel",)),
    )(bias_g, view, view, view, do_view, dd_view, lse_view)


def _merge_fwd(gates, o_a, o_b, name, tr=256):
    S = o_a.shape[0]

    def body(ga_ref, gb_ref, oa_ref, ob_ref, m_ref):
        m_ref[...] = (ga_ref[...] * oa_ref[...] + gb_ref[...] * ob_ref[...]).astype(BF16)

    row = pl.BlockSpec((tr, D_MODEL), lambda i: (i, 0))
    return pl.pallas_call(
        body,
        name=name,
        grid=(S // tr,),
        in_specs=[row, pl.BlockSpec((tr, D_MODEL), lambda i: (i, 1)), row, row],
        out_specs=row,
        out_shape=jax.ShapeDtypeStruct((S, D_MODEL), BF16),
        compiler_params=pltpu.CompilerParams(dimension_semantics=("parallel",)),
    )(gates, gates, o_a, o_b)


def _merge_bwd(dmrg, gates, o_a, o_b, name, tr=256):
    S = o_a.shape[0]

    def body(dm_ref, ga_ref, gb_ref, oa_ref, ob_ref, doa_ref, dob_ref, dga_ref, dgb_ref, dba_ref, dbb_ref):
        dm, ga, gb = dm_ref[...], ga_ref[...], gb_ref[...]
        doa_ref[...] = (dm * ga).astype(BF16)
        dob_ref[...] = (dm * gb).astype(BF16)
        dga = (dm * oa_ref[...]) * (ga * (1.0 - ga))
        dgb = (dm * ob_ref[...]) * (gb * (1.0 - gb))
        dga_ref[...] = dga.astype(BF16)
        dgb_ref[...] = dgb.astype(BF16)
        sa = jnp.sum(dga, axis=0, keepdims=True)
        sb = jnp.sum(dgb, axis=0, keepdims=True)

        @pl.when(pl.program_id(0) == 0)
        def _():
            dba_ref[...] = sa
            dbb_ref[...] = sb

        @pl.when(pl.program_id(0) != 0)
        def _():
            dba_ref[...] += sa
            dbb_ref[...] += sb

    row = pl.BlockSpec((tr, D_MODEL), lambda i: (i, 0))
    row1 = pl.BlockSpec((tr, D_MODEL), lambda i: (i, 1))
    vec = pl.BlockSpec((1, D_MODEL), lambda i: (0, 0))
    outs = pl.pallas_call(
        body,
        name=name,
        grid=(S // tr,),
        in_specs=[row, row, row1, row, row],
        out_specs=[row, row, row, row, vec, vec],
        out_shape=[jax.ShapeDtypeStruct((S, D_MODEL), BF16)] * 4 + [jax.ShapeDtypeStruct((1, D_MODEL), F32)] * 2,
        compiler_params=pltpu.CompilerParams(dimension_semantics=("arbitrary",)),
    )(dmrg, gates, gates, o_a, o_b)
    return outs


CONV_TR = 512
CONV_TC = 512
N_FFC = D_FF_PAD // CONV_TC


def _shift_down(x, edge_rows, k):
    out = pltpu.roll(x, k, 0)
    row = lax.broadcasted_iota(jnp.int32, x.shape, 0)
    for i in range(k):
        out = jnp.where(row == i, edge_rows[i], out)
    return out


def _conv_taps(x, halo_ref, live, w_ref, b_ref):
    h6, h7 = halo_ref[6:7, :] * live, halo_ref[7:8, :] * live
    s1 = _shift_down(x, [h7], 1)
    s2 = _shift_down(x, [h6, h7], 2)
    u = ((b_ref[...] + w_ref[0:1, :] * s2) + w_ref[1:2, :] * s1) + w_ref[2:3, :] * x
    return u, s1, s2


def _prev_halo(tr):
    return lambda i, j: (jnp.maximum(i * (tr // 8) - 1, 0), j)


def _ffn_fwd(u0, cw, cb, name):
    S = u0.shape[0]
    tr, tc = CONV_TR, CONV_TC

    def body(up_ref, gt_ref, hup_ref, hgt_ref, wu_ref, wg_ref, bu_ref, bg_ref, a_ref):
        live = (pl.program_id(0) > 0).astype(F32)
        up, _, _ = _conv_taps(up_ref[...], hup_ref, live, wu_ref, bu_ref)
        gt, _, _ = _conv_taps(gt_ref[...], hgt_ref, live, wg_ref, bg_ref)
        a_ref[...] = ((gt * jax.nn.sigmoid(gt)) * up).astype(BF16)

    return pl.pallas_call(
        body,
        name=name,
        grid=(S // tr, N_FFC),
        in_specs=[
            pl.BlockSpec((tr, tc), lambda i, j: (i, j)),
            pl.BlockSpec((tr, tc), lambda i, j: (i, j + N_FFC)),
            pl.BlockSpec((8, tc), _prev_halo(tr)),
            pl.BlockSpec((8, tc), lambda i, j: (jnp.maximum(i * (tr // 8) - 1, 0), j + N_FFC)),
            pl.BlockSpec((8, tc), lambda i, j: (0, j)),
            pl.BlockSpec((8, tc), lambda i, j: (0, j + N_FFC)),
            pl.BlockSpec((1, tc), lambda i, j: (0, j)),
            pl.BlockSpec((1, tc), lambda i, j: (0, j + N_FFC)),
        ],
        out_specs=pl.BlockSpec((tr, tc), lambda i, j: (i, j)),
        out_shape=jax.ShapeDtypeStruct((S, D_FF_PAD), BF16),
        compiler_params=pltpu.CompilerParams(dimension_semantics=("parallel", "parallel")),
    )(u0, u0, u0, u0, cw, cw, cb, cb)


def _ffn_bwd(u0, da, cw, cb, name):
    S = u0.shape[0]
    tr, tc = CONV_TR, CONV_TC
    nrow, te = S // tr, tr + 8

    def body(up_ref, gt_ref, hup_ref, hgt_ref, nup_ref, ngt_ref, da_ref, nda_ref, wu_ref, wg_ref, bu_ref, bg_ref, du0_ref, dcw_ref, dcb_ref):
        i = pl.program_id(1)
        prev_live = (i > 0).astype(F32)
        next_live = (i < nrow - 1).astype(F32)

        def conv(x_ref, nx_ref, h_ref, w_ref, b_ref):
            x = jnp.concatenate([x_ref[...], nx_ref[...] * next_live], axis=0)
            h6, h7 = h_ref[6:7, :] * prev_live, h_ref[7:8, :] * prev_live
            s1 = _shift_down(x, [h7], 1)
            s2 = _shift_down(x, [h6, h7], 2)
            return ((b_ref[...] + w_ref[0:1, :] * s2) + w_ref[1:2, :] * s1) + w_ref[2:3, :] * x, x, s1, s2

        up, xu0, xu1, xu2 = conv(up_ref, nup_ref, hup_ref, wu_ref, bu_ref)
        gt, xg0, xg1, xg2 = conv(gt_ref, ngt_ref, hgt_ref, wg_ref, bg_ref)
        da_v = jnp.concatenate([da_ref[...], nda_ref[...] * next_live], axis=0)
        sg = jax.nn.sigmoid(gt)
        d_up = da_v * (gt * sg)
        d_gt = (da_v * up) * (sg * (1.0 + gt * (1.0 - sg)))
        tap = lax.broadcasted_iota(jnp.int32, (8, tc), 0)

        def finish(half, du, x0, x1, x2, w_ref):
            n1 = pltpu.roll(du, te - 1, 0)
            n2 = pltpu.roll(du, te - 2, 0)
            du0 = (w_ref[2:3, :] * du + w_ref[1:2, :] * n1) + w_ref[0:1, :] * n2
            du0_ref[half] = du0[:tr].astype(BF16)
            d = du[:tr]
            dcw = jnp.where(
                tap == 0,
                jnp.sum(d * x2[:tr], axis=0, keepdims=True),
                jnp.where(tap == 1, jnp.sum(d * x1[:tr], axis=0, keepdims=True), jnp.where(tap == 2, jnp.sum(d * x0[:tr], axis=0, keepdims=True), 0.0)),
            )
            dcb = jnp.sum(d, axis=0, keepdims=True)

            @pl.when(i == 0)
            def _():
                dcw_ref[half] = dcw
                dcb_ref[half] = dcb

            @pl.when(i != 0)
            def _():
                dcw_ref[half] += dcw
                dcb_ref[half] += dcb

        finish(0, d_up, xu0, xu1, xu2, wu_ref)
        finish(1, d_gt, xg0, xg1, xg2, wg_ref)

    def prev8(off):
        return pl.BlockSpec((8, tc), lambda j, i: (jnp.maximum(i * (tr // 8) - 1, 0), j + off))

    def next8(off):
        return pl.BlockSpec((8, tc), lambda j, i: (jnp.minimum((i + 1) * (tr // 8), S // 8 - 1), j + off))

    return pl.pallas_call(
        body,
        name=name,
        grid=(N_FFC, nrow),
        in_specs=[
            pl.BlockSpec((tr, tc), lambda j, i: (i, j)),
            pl.BlockSpec((tr, tc), lambda j, i: (i, j + N_FFC)),
            prev8(0),
            prev8(N_FFC),
            next8(0),
            next8(N_FFC),
            pl.BlockSpec((tr, tc), lambda j, i: (i, j)),
            next8(0),
            pl.BlockSpec((8, tc), lambda j, i: (0, j)),
            pl.BlockSpec((8, tc), lambda j, i: (0, j + N_FFC)),
            pl.BlockSpec((1, tc), lambda j, i: (0, j)),
            pl.BlockSpec((1, tc), lambda j, i: (0, j + N_FFC)),
        ],
        out_specs=[
            pl.BlockSpec((2, tr, tc), lambda j, i: (0, i, j)),
            pl.BlockSpec((2, 8, tc), lambda j, i: (0, 0, j)),
            pl.BlockSpec((2, 1, tc), lambda j, i: (0, 0, j)),
        ],
        out_shape=[
            jax.ShapeDtypeStruct((2, S, D_FF_PAD), BF16),
            jax.ShapeDtypeStruct((2, 8, D_FF_PAD), F32),
            jax.ShapeDtypeStruct((2, 1, D_FF_PAD), F32),
        ],
        compiler_params=pltpu.CompilerParams(dimension_semantics=("parallel", "arbitrary")),
    )(u0, u0, u0, u0, u0, u0, da, da, cw, cw, cb, cb)


def _adamw(w, g, m, v, name):
    R, C = w.shape
    tr = R
    for cand in (256, 128, 64, 32, 16, 8):
        if R % cand == 0 and R > cand:
            tr = cand
            break

    def body(w_ref, g_ref, m_ref, v_ref, d_ref, nm_ref, nv_ref):
        gv = g_ref[...]
        nm = ADAM_B1 * m_ref[...] + (1.0 - ADAM_B1) * gv
        nv = ADAM_B2 * v_ref[...] + (1.0 - ADAM_B2) * (gv * gv)
        m_hat = nm / (1.0 - ADAM_B1**ADAM_STEP)
        v_hat = nv / (1.0 - ADAM_B2**ADAM_STEP)
        d_ref[...] = -ADAM_LR * (m_hat / (jnp.sqrt(v_hat) + ADAM_EPS) + ADAM_WD * w_ref[...])
        nm_ref[...] = nm
        nv_ref[...] = nv

    blk = pl.BlockSpec((tr, C), lambda i: (i, 0))
    return pl.pallas_call(
        body,
        name=name,
        grid=(R // tr,),
        in_specs=[blk] * 4,
        out_specs=[blk] * 3,
        out_shape=[jax.ShapeDtypeStruct((R, C), F32)] * 3,
        compiler_params=pltpu.CompilerParams(dimension_semantics=("parallel",)),
    )(w, g, m, v)


ANY = pl.BlockSpec(memory_space=pl.ANY)


def _all_gather(blocks, name):
    n = len(blocks)

    def body(*refs):
        x_refs, out_refs = refs[:n], refs[n : 2 * n]
        send_sems, recv_sems, local_sems = refs[2 * n :]
        x, y, c = lax.axis_index("x"), lax.axis_index("y"), lax.axis_index("c")
        me, sibling = (x, y, c), (x, y, 1 - c)
        chips = [(1 - x, y), (x, 1 - y), (1 - x, 1 - y)]

        def slot(a, px, py, pc):
            return out_refs[a].at[4 * px + 2 * py + pc]

        def copy(a, k, blk, to, src=None):
            return pltpu.make_async_remote_copy(
                src_ref=slot(a, *blk) if src is None else src,
                dst_ref=slot(a, *blk),
                send_sem=send_sems.at[7 * a + k],
                recv_sem=recv_sems.at[7 * a + k],
                device_id=to,
                device_id_type=MESH,
            )

        mine = [pltpu.make_async_copy(x_refs[a], slot(a, *me), local_sems.at[a]) for a in range(n)]
        sent = []
        for a in range(n):
            mine[a].start()
            first = [copy(a, 0, me, sibling, src=x_refs[a])]
            first += [copy(a, 1 + j, me, (*chip, c), src=x_refs[a]) for j, chip in enumerate(chips)]
            for cp in first:
                cp.start()
            sent += first
        for a in range(n):
            for j, chip in enumerate(chips):
                copy(a, 1 + j, (*chip, c), me).wait_recv()
                passed = copy(a, 4 + j, (*chip, c), sibling)
                passed.start()
                sent.append(passed)
        for a in range(n):
            copy(a, 0, sibling, me).wait_recv()
            for j, chip in enumerate(chips):
                copy(a, 4 + j, (*chip, 1 - c), me).wait_recv()
        for cp in sent:
            cp.wait_send()
        for cp in mine:
            cp.wait()

    return pl.pallas_call(
        body,
        name=name,
        out_shape=[jax.ShapeDtypeStruct((N_DEV,) + b.shape, b.dtype) for b in blocks],
        in_specs=[ANY] * n,
        out_specs=[ANY] * n,
        scratch_shapes=[pltpu.SemaphoreType.DMA((7 * n,)), pltpu.SemaphoreType.DMA((7 * n,)), pltpu.SemaphoreType.DMA((n,))],
    )(*blocks)


def _pair_exchange(gs, name):
    n = len(gs)

    def body(*refs):
        g_refs, out_refs = refs[:n], refs[n : 2 * n]
        send_sems, recv_sems = refs[2 * n :]
        x, y, c = lax.axis_index("x"), lax.axis_index("y"), lax.axis_index("c")
        copies = [
            pltpu.make_async_remote_copy(
                src_ref=g_refs[a].at[2 * k + (1 - c)],
                dst_ref=out_refs[a].at[k],
                send_sem=send_sems.at[N_CHIP * a + k],
                recv_sem=recv_sems.at[N_CHIP * a + k],
                device_id=(x, y, 1 - c),
                device_id_type=MESH,
            )
            for a in range(n)
            for k in range(N_CHIP)
        ]
        for cp in copies:
            cp.start()
        for cp in copies:
            cp.wait()

    return pl.pallas_call(
        body,
        name=name,
        out_shape=[jax.ShapeDtypeStruct((N_CHIP,) + g.shape[1:], g.dtype) for g in gs],
        in_specs=[ANY] * n,
        out_specs=[ANY] * n,
        scratch_shapes=[pltpu.SemaphoreType.DMA((N_CHIP * n,)), pltpu.SemaphoreType.DMA((N_CHIP * n,))],
    )(*gs)


def _row_tile(rows):
    return max(t for t in range(16, 353, 16) if rows % t == 0)


def _pair_add(g, recv, core, name):
    _, R, C = g.shape
    tr = _row_tile(R)

    def body(core_ref, g_ref, r_ref, o_ref):
        o_ref[...] = (g_ref[...].astype(F32) + r_ref[...].astype(F32)).astype(o_ref.dtype)

    return pl.pallas_call(
        body,
        name=name,
        grid_spec=pltpu.PrefetchScalarGridSpec(
            num_scalar_prefetch=1,
            grid=(N_CHIP, R // tr),
            in_specs=[
                pl.BlockSpec((1, tr, C), lambda k, i, core_ref: (2 * k + core_ref[0], i, 0)),
                pl.BlockSpec((1, tr, C), lambda k, i, core_ref: (k, i, 0)),
            ],
            out_specs=pl.BlockSpec((1, tr, C), lambda k, i, core_ref: (k, i, 0)),
        ),
        out_shape=jax.ShapeDtypeStruct((N_CHIP, R, C), g.dtype),
        compiler_params=pltpu.CompilerParams(dimension_semantics=("parallel", "parallel")),
    )(core, g, recv)


def _chip_exchange(pairs, name):
    n = len(pairs)

    def body(*refs):
        p_refs, out_refs = refs[:n], refs[n : 2 * n]
        send_sems, recv_sems, local_sems = refs[2 * n :]
        x, y, c = lax.axis_index("x"), lax.axis_index("y"), lax.axis_index("c")
        mine = 2 * x + y
        chips = [(1 - x, y), (x, 1 - y), (1 - x, 1 - y)]
        local = [pltpu.make_async_copy(p_refs[a].at[mine], out_refs[a].at[mine], local_sems.at[a]) for a in range(n)]

        def copy(a, j, src_slot, dst_slot):
            px, py = chips[j]
            return pltpu.make_async_remote_copy(
                src_ref=p_refs[a].at[src_slot],
                dst_ref=out_refs[a].at[dst_slot],
                send_sem=send_sems.at[3 * a + j],
                recv_sem=recv_sems.at[3 * a + j],
                device_id=(px, py, c),
                device_id_type=MESH,
            )

        sent = []
        for a in range(n):
            local[a].start()
            for j, (px, py) in enumerate(chips):
                cp = copy(a, j, 2 * px + py, mine)
                cp.start()
                sent.append(cp)
        for a in range(n):
            for j, (px, py) in enumerate(chips):
                copy(a, j, mine, 2 * px + py).wait_recv()
        for cp in sent:
            cp.wait_send()
        for cp in local:
            cp.wait()

    return pl.pallas_call(
        body,
        name=name,
        out_shape=[jax.ShapeDtypeStruct(p.shape, p.dtype) for p in pairs],
        in_specs=[ANY] * n,
        out_specs=[ANY] * n,
        scratch_shapes=[pltpu.SemaphoreType.DMA((3 * n,)), pltpu.SemaphoreType.DMA((3 * n,)), pltpu.SemaphoreType.DMA((n,))],
    )(*pairs)


def _slot_sum(parts, name):
    n, R, C = parts.shape
    tr = _row_tile(R) if R % 16 == 0 else R

    def body(p_ref, o_ref):
        acc = p_ref[0].astype(F32)
        for k in range(1, n):
            acc = acc + p_ref[k].astype(F32)
        o_ref[...] = acc

    return pl.pallas_call(
        body,
        name=name,
        grid=(R // tr,),
        in_specs=[pl.BlockSpec((n, tr, C), lambda i: (0, i, 0))],
        out_specs=pl.BlockSpec((tr, C), lambda i: (i, 0)),
        out_shape=jax.ShapeDtypeStruct((R, C), F32),
        compiler_params=pltpu.CompilerParams(dimension_semantics=("parallel",)),
    )(parts)


W_IN_TC = 256
W_IN_BOUNDS = (0, LAT, LAT + 3 * DIL_QKV, LAT + 3 * DIL_QKV + D_MODEL, D_IN)


def _w_in_regroup(slots, name):
    tc = W_IN_TC

    def body(s_ref, lat_ref, dqkv_ref, g_ref, buf):
        for j in range(N_DEV):
            buf[j * IN_ROWS : (j + 1) * IN_ROWS, :] = s_ref[j].astype(F32)[:IN_ROWS, :]
        lat_ref[:LAT, :] = buf[:LAT, :].astype(BF16)
        lat_ref[LAT:, :] = jnp.zeros((LAT_PAD - LAT, tc), BF16)
        dqkv_ref[...] = buf[W_IN_BOUNDS[1] : W_IN_BOUNDS[2], :].astype(BF16)
        g_ref[...] = buf[W_IN_BOUNDS[2] :, :].astype(BF16)

    def col(rows):
        return pl.BlockSpec((rows, tc), lambda k: (0, k))

    return pl.pallas_call(
        body,
        name=name,
        grid=(D_MODEL // tc,),
        in_specs=[pl.BlockSpec((N_DEV, IN_ROWS_PAD, tc), lambda k: (0, 0, k))],
        out_specs=[col(LAT_PAD), col(3 * DIL_QKV), col(2 * D_MODEL)],
        out_shape=[
            jax.ShapeDtypeStruct((LAT_PAD, D_MODEL), BF16),
            jax.ShapeDtypeStruct((3 * DIL_QKV, D_MODEL), BF16),
            jax.ShapeDtypeStruct((2 * D_MODEL, D_MODEL), BF16),
        ],
        scratch_shapes=[pltpu.VMEM((D_IN, tc), F32)],
        compiler_params=pltpu.CompilerParams(dimension_semantics=("parallel",)),
    )(slots)


def _w_in_grad_regroup(g_lat, g_dqkv, g_ga, g_gb, name):
    tc = W_IN_TC

    def body(lat_ref, dqkv_ref, ga_ref, gb_ref, o_ref, buf):
        b = W_IN_BOUNDS
        buf[b[0] : b[1], :] = lat_ref[:LAT, :].astype(F32)
        buf[b[1] : b[2], :] = dqkv_ref[...].astype(F32)
        buf[b[2] : b[3], :] = ga_ref[...].astype(F32)
        buf[b[3] : b[4], :] = gb_ref[...].astype(F32)
        fill = jnp.zeros((IN_ROWS_PAD - IN_ROWS, tc), F32)
        for j in range(N_DEV):
            o_ref[j] = jnp.concatenate([buf[j * IN_ROWS : (j + 1) * IN_ROWS, :], fill], axis=0).astype(BF16)

    def col(rows):
        return pl.BlockSpec((rows, tc), lambda k: (0, k))

    return pl.pallas_call(
        body,
        name=name,
        grid=(D_MODEL // tc,),
        in_specs=[col(LAT_PAD), col(3 * DIL_QKV), col(D_MODEL), col(D_MODEL)],
        out_specs=pl.BlockSpec((N_DEV, IN_ROWS_PAD, tc), lambda k: (0, 0, k)),
        out_shape=jax.ShapeDtypeStruct((N_DEV, IN_ROWS_PAD, D_MODEL), BF16),
        scratch_shapes=[pltpu.VMEM((D_IN, tc), F32)],
        compiler_params=pltpu.CompilerParams(dimension_semantics=("parallel",)),
    )(g_lat, g_dqkv, g_ga, g_gb)


def _ffn_pad(a, axis):
    a = jnp.moveaxis(a, axis, -1)
    g = a.reshape(a.shape[:-1] + (2 * N_DEV, FF_GROUP))
    g = jnp.pad(g, [(0, 0)] * (g.ndim - 1) + [(0, FF_GROUP_PAD - FF_GROUP)])
    return jnp.moveaxis(g.reshape(a.shape[:-1] + (2 * D_FF_PAD,)), -1, axis)


def _ffn_unpad(a, axis):
    a = jnp.moveaxis(a, axis, -1)
    g = a.reshape(a.shape[:-1] + (2 * N_DEV, FF_GROUP_PAD))[..., :FF_GROUP]
    return jnp.moveaxis(g.reshape(a.shape[:-1] + (2 * D_FF,)), -1, axis)


MISC = (("w_o_mla", (256, 1024)), ("w_o_dil", (256, 512)), ("w_uq", (192, 512)), ("w_ukv", (256, 256)))


def _exchange_blocks(w):
    def t(a):
        return a.astype(BF16).T

    up = t(w["w_up"]).reshape(2, FF_GROUP, D_MODEL)
    return [
        jnp.pad(t(w["w_in"]), ((0, IN_ROWS_PAD - IN_ROWS), (0, 0))),
        jnp.pad(up, ((0, 0), (0, FF_GROUP_PAD - FF_GROUP), (0, 0))).reshape(2 * FF_GROUP_PAD, D_MODEL),
        jnp.pad(w["w_down"].astype(BF16), ((0, FF_GROUP_PAD - FF_GROUP), (0, 0))),
        w["w_out"].astype(BF16),
        jnp.concatenate([t(w[n]).reshape(-1, D_MODEL) for n, _ in MISC], axis=0),
    ]


def _misc_split(misc):
    out, off = {}, 0
    for n, (r, c) in MISC:
        rows = r * c // D_MODEL
        out[n] = misc[..., off : off + rows, :].reshape(misc.shape[:-2] + (r, c))
        off += rows
    return out


def _unpack_weights(gathered):
    g_in, g_up, g_down, g_out, g_misc = gathered
    lat_t, dqkv_t, g_t = _w_in_regroup(g_in, "w_in_regroup")
    misc = _misc_split(g_misc)
    uq_t = jnp.pad(misc["w_uq"], ((0, 0), (0, HEAD_PAD - QK_NOPE - QK_ROPE), (0, 0)))
    return {
        "lat_t": lat_t,
        "dqkv_t": dqkv_t,
        "g_t": g_t,
        "uq_t": uq_t.reshape(MLA_HEADS * HEAD_PAD, Q_LORA),
        "ukv_t": misc["w_ukv"].reshape(MLA_HEADS * HEAD_PAD, KV_LORA),
        "o_mla_t": misc["w_o_mla"].reshape(D_MODEL, MLA_HEADS * V_HEAD),
        "o_dil_t": misc["w_o_dil"].reshape(D_MODEL, DIL_OUT),
        "w_out": g_out.reshape(D_MODEL, D_MODEL),
        "up_t": g_up.reshape(2 * D_FF_PAD, D_MODEL),
        "w_down": g_down.reshape(D_FF_PAD, D_MODEL),
    }


def _grad_blocks(g):
    uq_t = g["uq_t"].reshape(MLA_HEADS, HEAD_PAD, Q_LORA)[:, : QK_NOPE + QK_ROPE]
    misc = {"w_o_mla": g["o_mla_t"], "w_o_dil": g["o_dil_t"], "w_uq": uq_t, "w_ukv": g["ukv_t"]}
    return [
        _w_in_grad_regroup(g["lat_t"], g["dqkv_t"], g["ga_t"], g["gb_t"], "w_in_grad_regroup"),
        g["up_t"].reshape(N_DEV, 2 * FF_GROUP_PAD, D_MODEL),
        g["w_down"].reshape(N_DEV, FF_GROUP_PAD, D_MODEL),
        g["w_out"].reshape(N_DEV, -1, D_MODEL),
        jnp.concatenate([misc[n].reshape(N_DEV, -1, D_MODEL) for n, _ in MISC], axis=1),
    ]


def _grad_shards(sums):
    s_in, s_up, s_down, s_out, s_misc = sums
    out = {
        "w_in": s_in[:IN_ROWS].T,
        "w_up": s_up.reshape(2, FF_GROUP_PAD, D_MODEL)[:, :FF_GROUP].reshape(2 * FF_GROUP, D_MODEL).T,
        "w_down": s_down[:FF_GROUP],
        "w_out": s_out,
    }
    out.update({n: v.T for n, v in _misc_split(s_misc).items()})
    return out


def _local_step(x, tgt, wt, conv_w, small):
    S = x.shape[0]
    lat_t, dqkv_t, g_t, uq_t, ukv_t = wt["lat_t"], wt["dqkv_t"], wt["g_t"], wt["uq_t"], wt["ukv_t"]
    o_mla_t, o_dil_t, w_out, up_t, w_down = wt["o_mla_t"], wt["o_dil_t"], wt["w_out"], wt["up_t"], wt["w_down"]
    cw = jnp.pad(_ffn_pad(conv_w, 1), ((0, 5), (0, 0)))
    cb = _ffn_pad(small["conv_b"], 1)
    cos_t, sin_t = _rope_tables(S)
    bias = _dil_bias()
    g1, g2, g3 = small["attn_norm_g"], small["ffn_norm_g"], small["final_norm_g"]
    gq, gkv = small["q_norm_g"], small["kv_norm_g"]

    h = _rms_fwd(x, g1, "rms_attn")
    lat = _mm(h, lat_t, "nt", F32, 1024, LAT_PAD, D_MODEL, "proj_lat")
    dqkv = _mm(h, dqkv_t, "nt", BF16, 1024, 512, D_MODEL, "proj_dqkv")
    gates = _mm(h, g_t, "nt", F32, 1024, 512, D_MODEL, "proj_gates", bias=small["b_gate"], act="sigmoid")
    cqn, ckvn, kpe = _mla_prep1(lat, gq, gkv, cos_t, sin_t, "mla_prep1")
    q_raw = _mm(cqn, uq_t, "nt", F32, 1024, 1024, Q_LORA, "mla_uq")
    kv = _mm(ckvn, ukv_t, "nt", BF16, 1024, 1024, KV_LORA, "mla_ukv")
    q_att, k_att = _mla_prep2(q_raw, kv, kpe, cos_t, sin_t, "mla_prep2")
    o, lse = _flash2_fwd(q_att, k_att, kv, "mla_flash_fwd")
    o_a = _mm(o, o_mla_t, "nt", F32, 1024, 1024, MLA_HEADS * V_HEAD, "mla_out")

    d_os, d_ls = [], []
    for g, (_, dil) in enumerate(DIL_PATTERNS):
        og, lg = _dil_fwd_group(dqkv.reshape(S // dil, dil * 3 * DIL_QKV), bias[g], g, dil, f"dil_fwd_{g}")
        d_os.append(og.reshape(S, DIL_OUT))
        d_ls.append(lg.reshape(S, DIL_OUT))
    od, dil_lse = _dil_combine(d_os, d_ls, "dil_combine")
    o_b = _mm(od, o_dil_t, "nt", F32, 1024, 1024, DIL_OUT, "dil_out")

    mrg = _merge_fwd(gates, o_a, o_b, "merge_fwd")
    x1 = _mm(mrg, w_out, "nn", F32, 1024, 1024, D_MODEL, "mix_out", res=x)
    h2 = _rms_fwd(x1, g2, "rms_ffn")
    u0 = _mm(h2, up_t, "nt", F32, 1024, 512, D_MODEL, "ffn_up")
    a = _ffn_fwd(u0, cw, cb, "ffn_conv_fwd")
    x2 = _mm(a, w_down, "nn", F32, 1024, 512, D_FF_PAD // 2, "ffn_down", res=x1)
    loss_part, dx2, dx2b, dg3 = _final_loss(x2, g3, tgt, "final_loss")

    da = _mm(dx2b, w_down, "nt", F32, 1024, 512, D_MODEL, "ffn_down_dx")
    gw_down = _mm(a, dx2b, "tn", BF16, 512, 1024, 1024, "ffn_down_dw")
    du0, dcw, dcb = _ffn_bwd(u0, da, cw, cb, "ffn_conv_bwd")
    du0 = du0.reshape(2 * S, D_FF_PAD)
    dh2 = _mm(du0, up_t, "nn", F32, 1024, 1024, 512, "ffn_up_dx", a_halves=2)
    gw_up_t = _mm(du0, h2, "tn", BF16, 512, 1024, 1024, "ffn_up_dw", a_halves=2)
    dx1, dx1b, dg2 = _rms_bwd(dh2, x1, g2, dx2, "rms_ffn_bwd")

    dmrg = _mm(dx1b, w_out, "nt", F32, 1024, 1024, D_MODEL, "mix_out_dx")
    gw_out = _mm(mrg, dx1b, "tn", BF16, 1024, 1024, 1024, "mix_out_dw")
    do_a, do_b, dga, dgb, dba, dbb = _merge_bwd(dmrg, gates, o_a, o_b, "merge_bwd")

    do = _mm(do_a, o_mla_t, "nn", BF16, 1024, 1024, D_MODEL, "mla_out_dx")
    gw_o_mla_t = _mm(do_a, o, "tn", BF16, 1024, 1024, 1024, "mla_out_dw")
    dod = _mm(do_b, o_dil_t, "nn", F32, 1024, DIL_OUT, D_MODEL, "dil_out_dx")
    gw_o_dil_t = _mm(do_b, od, "tn", BF16, 1024, DIL_OUT, 1024, "dil_out_dw")

    dq_att, delta = _flash2_dq(q_att, k_att, kv, o, do, lse, "mla_flash_dq")
    lse_row = lse[:, :, 0][:, None, :]
    delta_row = delta[:, :, 0][:, None, :]
    dk_att, dv = _flash2_dkv(q_att, k_att, kv, do, lse_row, delta_row, "mla_flash_dkv")
    dq_raw, dkv, dkpe = _mla_post(dq_att, dk_att, dv, cos_t, sin_t, "mla_post")
    dcqn = _mm(dq_raw, uq_t, "nn", F32, 1024, Q_LORA, MLA_HEADS * HEAD_PAD, "mla_uq_dx")
    gw_uq_t = _mm(dq_raw, cqn, "tn", BF16, 1024, Q_LORA, 1024, "mla_uq_dw")
    dckvn = _mm(dkv, ukv_t, "nn", F32, 1024, KV_LORA, MLA_HEADS * HEAD_PAD, "mla_ukv_dx")
    gw_ukv_t = _mm(dkv, ckvn, "tn", BF16, 1024, KV_LORA, 1024, "mla_ukv_dw")
    dlat, dgq, dgkv = _lat_bwd(dcqn, dckvn, dkpe, lat, gq, gkv, "lat_bwd")

    dd, dodb = _dil_rowdot(dod, od, "dil_rowdot")
    dparts = [[None] * DIL_GROUPS for _ in range(3)]
    for g, (_, dil) in enumerate(DIL_PATTERNS):
        L = S // dil
        outs = _dil_bwd_group(
            dqkv.reshape(L, dil * 3 * DIL_QKV),
            bias[g],
            dodb.reshape(L, dil * DIL_OUT),
            dd.reshape(L, dil * DIL_OUT),
            dil_lse.reshape(L, dil * DIL_OUT),
            g,
            dil,
            f"dil_bwd_{g}",
        )
        for t in range(3):
            dparts[t][g] = outs[t].reshape(S, DIL_OUT)
    ddqkv = jnp.concatenate([p for row in dparts for p in row], axis=1)

    dh = _mm(dlat, lat_t, "nn", F32, 1024, 1024, LAT_PAD, "proj_lat_dx")
    dh = _mm(ddqkv, dqkv_t, "nn", F32, 1024, 1024, 1536, "proj_dqkv_dx", res=dh)
    dh = _mm(dga, g_t, "nn", F32, 1024, 1024, D_MODEL, "proj_ga_dx", res=dh)
    dh = _mm(dgb, g_t, "nn", F32, 1024, 1024, D_MODEL, "proj_gb_dx", res=dh, b_koff=1)
    gw_lat_t = _mm(dlat, h, "tn", BF16, LAT_PAD, 1024, 1024, "proj_lat_dw")
    gw_dqkv_t = _mm(ddqkv, h, "tn", BF16, 512, 1024, 1024, "proj_dqkv_dw")
    gw_ga_t = _mm(dga, h, "tn", BF16, 1024, 1024, 1024, "proj_ga_dw")
    gw_gb_t = _mm(dgb, h, "tn", BF16, 1024, 1024, 1024, "proj_gb_dw")
    grad_x, _, dg1 = _rms_bwd(dh, x, g1, dx1, "rms_attn_bwd")

    grads = {
        "lat_t": gw_lat_t,
        "dqkv_t": gw_dqkv_t,
        "ga_t": gw_ga_t,
        "gb_t": gw_gb_t,
        "uq_t": gw_uq_t,
        "ukv_t": gw_ukv_t,
        "o_mla_t": gw_o_mla_t,
        "o_dil_t": gw_o_dil_t,
        "w_out": gw_out,
        "up_t": gw_up_t,
        "w_down": gw_down,
    }
    small_grads = {
        "attn_norm_g": dg1,
        "b_gate": jnp.concatenate([dba, dbb], axis=1),
        "q_norm_g": dgq,
        "kv_norm_g": dgkv,
        "ffn_norm_g": dg2,
        "conv_b": _ffn_unpad(jnp.concatenate([dcb[0], dcb[1]], axis=1), 1),
        "final_norm_g": dg3,
        "conv_w": _ffn_unpad(jnp.concatenate([dcw[0, :3], dcw[1, :3]], axis=1), 1),
    }
    return loss_part, grad_x, grads, small_grads


SMALL_ORDER = ("attn_norm_g", "b_gate", "q_norm_g", "kv_norm_g", "ffn_norm_g", "conv_b", "final_norm_g", "conv_w")
WEIGHT_ORDER = (
    "attn_norm_g", "w_in", "b_gate", "q_norm_g", "w_uq", "kv_norm_g", "w_ukv", "w_o_mla", "w_o_dil", "w_out",
    "ffn_norm_g", "w_up", "conv_w", "conv_b", "w_down", "final_norm_g",
)


def kernel(x, attn_norm_g, w_in, b_gate, q_norm_g, w_uq, kv_norm_g, w_ukv, w_o_mla, w_o_dil, w_out, ffn_norm_g, w_up, conv_w, conv_b, w_down, final_norm_g, loss_target, m_attn_norm_g, m_w_in, m_b_gate, m_q_norm_g, m_w_uq, m_kv_norm_g, m_w_ukv, m_w_o_mla, m_w_o_dil, m_w_out, m_ffn_norm_g, m_w_up, m_conv_w, m_conv_b, m_w_down, m_final_norm_g, v_attn_norm_g, v_w_in, v_b_gate, v_q_norm_g, v_w_uq, v_kv_norm_g, v_w_ukv, v_w_o_mla, v_w_o_dil, v_w_out, v_ffn_norm_g, v_w_up, v_conv_w, v_conv_b, v_w_down, v_final_norm_g):
    env = dict(locals())
    dev = 4 * lax.axis_index("x") + 2 * lax.axis_index("y") + lax.axis_index("c")
    core = lax.axis_index("c").astype(jnp.int32).reshape(1)

    def two_d(a):
        return a.reshape(-1, a.shape[-1])

    w = {n: two_d(env[n]) for n in WEIGHT_ORDER}
    m = {n: two_d(env["m_" + n]) for n in WEIGHT_ORDER}
    v = {n: two_d(env["v_" + n]) for n in WEIGHT_ORDER}

    r, c = CONV_SHARD
    conv = jnp.pad(w["conv_w"].reshape(-1), (0, 8 * SMALL_COLS - r * c)).reshape(8, SMALL_COLS)
    *gathered, conv = _all_gather(_exchange_blocks(w) + [conv], "ag_weights")
    full = _unpack_weights(gathered)
    conv = conv.reshape(N_DEV, 8 * SMALL_COLS)[:, : r * c].reshape(N_DEV, r, c)
    conv_w_full = conv.transpose(1, 0, 2).reshape(r, N_DEV * c)
    small = {n: w[n] for n in SMALL_ORDER if n != "conv_w"}

    loss_part, grad_x, grads, small_grads = _local_step(x[0], loss_target[0], full, conv_w_full, small)
    loss = lax.psum(loss_part[0, 0], AXES)

    gblocks = _grad_blocks(grads)
    recv = _pair_exchange(gblocks, "rs_pair_exchange")
    pairs = [_pair_add(g, r_, core, f"rs_pair_add_{i}") for i, (g, r_) in enumerate(zip(gblocks, recv))]
    parts = _chip_exchange(pairs, "rs_chip_exchange")
    gshard = _grad_shards([_slot_sum(p, f"rs_sum_{i}") for i, p in enumerate(parts)])

    sflat = jnp.concatenate([small_grads[n].reshape(-1) for n in SMALL_ORDER])
    sflat = jnp.pad(sflat, (0, SMALL_ROWS * SMALL_COLS - sflat.shape[0])).reshape(SMALL_ROWS, SMALL_COLS)
    ssum = _slot_sum(_all_gather([sflat], "ag_small_grads")[0], "small_sum").reshape(-1)
    gsmall, off = {}, 0
    for n in SMALL_ORDER:
        shape = (3, 2 * D_FF) if n == "conv_w" else w[n].shape
        size = shape[0] * shape[1]
        gsmall[n] = ssum[off : off + size].reshape(shape)
        off += size
    gsmall["conv_w"] = lax.dynamic_slice_in_dim(gsmall["conv_w"], dev * CONV_SHARD[1], CONV_SHARD[1], axis=1)

    g_all = {**gshard, **gsmall}
    out_g, out_d, out_m, out_v = [], [], [], []
    for n in WEIGHT_ORDER:
        d, nm, nv = _adamw(w[n], g_all[n], m[n], v[n], "adamw_" + n)
        shape = env[n].shape
        out_g.append(g_all[n].reshape(shape))
        out_d.append(d.reshape(shape))
        out_m.append(nm.reshape(shape))
        out_v.append(nv.reshape(shape))
    return (loss, grad_x[None], *out_g, *out_d, *out_m, *out_v)
```

```python
import functools

import jax
import jax.numpy as jnp
import numpy as np
from jax import lax
from jax.experimental import pallas as pl
from jax.experimental.pallas import tpu as pltpu

F32 = jnp.float32
BF16 = jnp.bfloat16

N_DEV = 8
N_CHIP = 4
AXES = ("x", "y", "c")
MESH = pl.DeviceIdType.MESH

D_MODEL = 2048
MLA_HEADS = 8
QK_NOPE = 128
QK_ROPE = 64
V_HEAD = 128
Q_LORA = 512
KV_LORA = 256
ROPE_THETA = 10000.0
HEAD_PAD = 256
DIL_PATTERNS = ((128, 1), (512, 4), (2048, 16))
DIL_GROUPS = 3
DIL_HG = 4
DIL_HEADS = 12
DIL_HD = 128
DIL_BLK = 128
DIL_QKV = DIL_HEADS * DIL_HD
DIL_OUT = DIL_HG * DIL_HD
ALIBI_MAX_BIAS = 8.0
D_FF = 5504
D_FF_PAD = 5632
NORM_EPS = 1e-6
LAT = Q_LORA + KV_LORA + QK_ROPE
LAT_PAD = 896
D_IN = LAT + 3 * DIL_QKV + 2 * D_MODEL
NEG = -1e30

ADAM_LR = 0.001
ADAM_B1 = 0.9
ADAM_B2 = 0.999
ADAM_EPS = 1e-08
ADAM_WD = 0.01
ADAM_STEP = 10

SMALL_ROWS = 56
SMALL_COLS = 1024

IN_ROWS = 1192
IN_ROWS_PAD = 1200
FF_GROUP = D_FF // N_DEV
FF_GROUP_PAD = D_FF_PAD // N_DEV
CONV_SHARD = (3, 1376)

NT = (((1,), (1,)), ((), ()))
TN = (((0,), (0,)), ((), ()))


def _dot(a, b, dims=(((1,), (0,)), ((), ()))):
    return lax.dot_general(a, b, dims, preferred_element_type=F32)


def _mm(a, b, mode, out_dtype, tm, tn, tk, name, bias=None, act=None, res=None, b_koff=0, a_halves=1):
    H = a_halves
    if mode == "nn":
        (M, K), (K2, N) = (a.shape[0] // H, a.shape[1] * H), b.shape
        assert (b_koff + 1) * K <= K2, (name, a.shape, b.shape)
        koff, K2 = b_koff * (K // tk), K
        kper, mrows = a.shape[1] // tk, M // tm
        a_spec = pl.BlockSpec((tm, tk), lambda i, j, k: (i + (k // kper) * mrows, k % kper))
        b_spec = pl.BlockSpec((tk, tn), lambda i, j, k: (k + koff, j))
        dims = (((1,), (0,)), ((), ()))
    elif mode == "nt":
        (M, K), (N, K2) = a.shape, b.shape
        a_spec = pl.BlockSpec((tm, tk), lambda i, j, k: (i, k))
        b_spec = pl.BlockSpec((tn, tk), lambda i, j, k: (j, k))
        dims = NT
    else:
        (K, M), (K2, N) = (a.shape[0] // H, a.shape[1] * H), b.shape
        mper, krows = a.shape[1] // tm, K // tk
        a_spec = pl.BlockSpec((tk, tm), lambda i, j, k: (k + (i // mper) * krows, i % mper))
        b_spec = pl.BlockSpec((tk, tn), lambda i, j, k: (k, j))
        dims = TN
    assert K == K2 and M % tm == 0 and N % tn == 0 and K % tk == 0, (name, a.shape, b.shape)
    nk = K // tk
    has_bias, has_res = bias is not None, res is not None

    def body(*refs):
        refs = list(refs)
        a_ref, b_ref = refs[0], refs[1]
        pos = 2
        bias_ref = res_ref = None
        if has_bias:
            bias_ref = refs[pos]
            pos += 1
        if has_res:
            res_ref = refs[pos]
            pos += 1
        o_ref = refs[pos]
        p = _dot(a_ref[...].astype(BF16), b_ref[...].astype(BF16), dims)

        def finish(acc):
            if has_bias:
                acc = acc + bias_ref[...]
            if act == "sigmoid":
                acc = jax.nn.sigmoid(acc)
            if has_res:
                acc = res_ref[...] + acc
            o_ref[...] = acc.astype(o_ref.dtype)

        if nk == 1:
            finish(p)
        else:
            acc_ref = refs[pos + 1]
            k = pl.program_id(2)

            @pl.when(k == 0)
            def _():
                acc_ref[...] = p

            @pl.when(k != 0)
            def _():
                acc_ref[...] += p

            @pl.when(k == nk - 1)
            def _():
                finish(acc_ref[...])

    in_specs = [a_spec, b_spec]
    args = [a, b]
    if has_bias:
        in_specs.append(pl.BlockSpec((1, tn), lambda i, j, k: (0, j)))
        args.append(bias)
    if has_res:
        in_specs.append(pl.BlockSpec((tm, tn), lambda i, j, k: (i, j)))
        args.append(res)
    return pl.pallas_call(
        body,
        name=name,
        grid=(M // tm, N // tn, nk),
        in_specs=in_specs,
        out_specs=pl.BlockSpec((tm, tn), lambda i, j, k: (i, j)),
        out_shape=jax.ShapeDtypeStruct((M, N), out_dtype),
        scratch_shapes=[pltpu.VMEM((tm, tn), F32)] if nk > 1 else [],
        compiler_params=pltpu.CompilerParams(dimension_semantics=("parallel", "parallel", "arbitrary")),
    )(*args)


def _rstd(x):
    return lax.rsqrt(jnp.mean(x * x, axis=-1, keepdims=True) + NORM_EPS)


def _rms_bwd_math(dy, x, g):
    r = _rstd(x)
    xh = x * r
    dg = jnp.sum(dy * xh, axis=0, keepdims=True)
    dxh = dy * g
    dx = r * (dxh - xh * jnp.mean(dxh * xh, axis=-1, keepdims=True))
    return dx, dg


def _rms_fwd(x, g, name, tr=256):
    S, D = x.shape

    def body(x_ref, g_ref, o_ref):
        xv = x_ref[...]
        o_ref[...] = ((xv * _rstd(xv)) * g_ref[...]).astype(o_ref.dtype)

    return pl.pallas_call(
        body,
        name=name,
        grid=(S // tr,),
        in_specs=[pl.BlockSpec((tr, D), lambda i: (i, 0)), pl.BlockSpec((1, D), lambda i: (0, 0))],
        out_specs=pl.BlockSpec((tr, D), lambda i: (i, 0)),
        out_shape=jax.ShapeDtypeStruct((S, D), BF16),
        compiler_params=pltpu.CompilerParams(dimension_semantics=("parallel",)),
    )(x, g)


def _rms_bwd(dy, x, g, res, name, tr=256):
    S, D = x.shape

    def body(dy_ref, x_ref, g_ref, res_ref, dx_ref, dxb_ref, dg_ref):
        dx, dg = _rms_bwd_math(dy_ref[...], x_ref[...], g_ref[...])
        dx = dx + res_ref[...]
        dx_ref[...] = dx
        dxb_ref[...] = dx.astype(BF16)

        @pl.when(pl.program_id(0) == 0)
        def _():
            dg_ref[...] = dg

        @pl.when(pl.program_id(0) != 0)
        def _():
            dg_ref[...] += dg

    row = pl.BlockSpec((tr, D), lambda i: (i, 0))
    vec = pl.BlockSpec((1, D), lambda i: (0, 0))
    return pl.pallas_call(
        body,
        name=name,
        grid=(S // tr,),
        in_specs=[row, row, vec, row],
        out_specs=[row, row, vec],
        out_shape=[jax.ShapeDtypeStruct((S, D), F32), jax.ShapeDtypeStruct((S, D), BF16), jax.ShapeDtypeStruct((1, D), F32)],
        compiler_params=pltpu.CompilerParams(dimension_semantics=("arbitrary",)),
    )(dy, x, g, res)


def _final_loss(x2, g, tgt, name, tr=256):
    S, D = x2.shape

    def body(x_ref, g_ref, t_ref, loss_ref, dx_ref, dxb_ref, dg_ref):
        xv, gv = x_ref[...], g_ref[...]
        y = (xv * _rstd(xv)) * gv
        e = y - t_ref[...]
        part = 0.5 * jnp.sum(jnp.mean(e * e, axis=-1, keepdims=True), axis=0, keepdims=True)
        dx, dg = _rms_bwd_math(e * (1.0 / D), xv, gv)
        dx_ref[...] = dx
        dxb_ref[...] = dx.astype(BF16)
        part = jnp.broadcast_to(part, (1, 128))

        @pl.when(pl.program_id(0) == 0)
        def _():
            dg_ref[...] = dg
            loss_ref[...] = part

        @pl.when(pl.program_id(0) != 0)
        def _():
            dg_ref[...] += dg
            loss_ref[...] += part

    row = pl.BlockSpec((tr, D), lambda i: (i, 0))
    vec = pl.BlockSpec((1, D), lambda i: (0, 0))
    return pl.pallas_call(
        body,
        name=name,
        grid=(S // tr,),
        in_specs=[row, vec, row],
        out_specs=[pl.BlockSpec((1, 128), lambda i: (0, 0)), row, row, vec],
        out_shape=[
            jax.ShapeDtypeStruct((1, 128), F32),
            jax.ShapeDtypeStruct((S, D), F32),
            jax.ShapeDtypeStruct((S, D), BF16),
            jax.ShapeDtypeStruct((1, D), F32),
        ],
        compiler_params=pltpu.CompilerParams(dimension_semantics=("arbitrary",)),
    )(x2, g, tgt)


def _rope_tables(S):
    pos = jnp.arange(S, dtype=F32)
    inv_freq = ROPE_THETA ** (-jnp.arange(0, QK_ROPE, 2, dtype=F32) / QK_ROPE)
    ang = pos[:, None] * inv_freq[None, :]
    cos, sin = jnp.cos(ang), jnp.sin(ang)
    zero = jnp.zeros((S, 128 - QK_ROPE), F32)
    return jnp.concatenate([cos, cos, zero], axis=1), jnp.concatenate([-sin, sin, zero], axis=1)


def _rope_tile(x, cos_t, sin_t):
    lane = lax.broadcasted_iota(jnp.int32, x.shape, 1)
    partner = jnp.where(lane < QK_ROPE // 2, pltpu.roll(x, 128 - QK_ROPE // 2, 1), pltpu.roll(x, QK_ROPE // 2, 1))
    return x * cos_t + partner * sin_t


def _mla_prep1(lat, gq, gkv, cos_t, sin_t, name, tr=256):
    S = lat.shape[0]

    def body(lat_ref, gq_ref, gkv_ref, cos_ref, sin_ref, cq_ref, ckv_ref, kpe_ref):
        cq = lat_ref[:, :Q_LORA]
        ckv = lat_ref[:, Q_LORA : Q_LORA + KV_LORA]
        cq_ref[...] = ((cq * _rstd(cq)) * gq_ref[...]).astype(BF16)
        ckv_ref[...] = ((ckv * _rstd(ckv)) * gkv_ref[...]).astype(BF16)
        kpe_ref[...] = _rope_tile(lat_ref[:, Q_LORA + KV_LORA :], cos_ref[...], sin_ref[...]).astype(BF16)

    def row(n):
        return pl.BlockSpec((tr, n), lambda i: (i, 0))

    def vec(n):
        return pl.BlockSpec((1, n), lambda i: (0, 0))

    return pl.pallas_call(
        body,
        name=name,
        grid=(S // tr,),
        in_specs=[row(LAT_PAD), vec(Q_LORA), vec(KV_LORA), row(128), row(128)],
        out_specs=[row(Q_LORA), row(KV_LORA), row(128)],
        out_shape=[
            jax.ShapeDtypeStruct((S, Q_LORA), BF16),
            jax.ShapeDtypeStruct((S, KV_LORA), BF16),
            jax.ShapeDtypeStruct((S, 128), BF16),
        ],
        compiler_params=pltpu.CompilerParams(dimension_semantics=("parallel",)),
    )(lat, gq, gkv, cos_t, sin_t)


def _mla_prep2(q_raw, kv, kpe, cos_t, sin_t, name, tr=256):
    S = q_raw.shape[0]
    W = MLA_HEADS * HEAD_PAD

    def body(q_ref, kv_ref, kpe_ref, cos_ref, sin_ref, qa_ref, ka_ref):
        cos_v, sin_v, kpe_v = cos_ref[...], sin_ref[...], kpe_ref[...]
        for h in range(MLA_HEADS):
            lo = h * HEAD_PAD
            qa_ref[:, lo : lo + 128] = q_ref[:, lo : lo + 128].astype(BF16)
            qa_ref[:, lo + 128 : lo + 256] = _rope_tile(q_ref[:, lo + 128 : lo + 256], cos_v, sin_v).astype(BF16)
            ka_ref[:, lo : lo + 128] = kv_ref[:, lo : lo + 128]
            ka_ref[:, lo + 128 : lo + 256] = kpe_v

    def row(n):
        return pl.BlockSpec((tr, n), lambda i: (i, 0))

    return pl.pallas_call(
        body,
        name=name,
        grid=(S // tr,),
        in_specs=[row(W), row(W), row(128), row(128), row(128)],
        out_specs=[row(W), row(W)],
        out_shape=[jax.ShapeDtypeStruct((S, W), BF16), jax.ShapeDtypeStruct((S, W), BF16)],
        compiler_params=pltpu.CompilerParams(dimension_semantics=("parallel",)),
    )(q_raw, kv, kpe, cos_t, sin_t)


def _mla_post(dq_att, dk_att, dv, cos_t, sin_t, name, tr=256):
    S = dq_att.shape[0]
    W = MLA_HEADS * HEAD_PAD

    def body(dq_ref, dk_ref, dv_ref, cos_ref, sin_ref, dqr_ref, dkv_ref, dkpe_ref):
        cos_v, nsin_v = cos_ref[...], -sin_ref[...]
        kpe = jnp.zeros((tr, 128), F32)
        for h in range(MLA_HEADS):
            lo = h * HEAD_PAD
            dqr_ref[:, lo : lo + 128] = dq_ref[:, lo : lo + 128].astype(BF16)
            dqr_ref[:, lo + 128 : lo + 256] = _rope_tile(dq_ref[:, lo + 128 : lo + 256], cos_v, nsin_v).astype(BF16)
            dkv_ref[:, lo : lo + 128] = dk_ref[:, lo : lo + 128].astype(BF16)
            dkv_ref[:, lo + 128 : lo + 256] = dv_ref[:, h * 128 : (h + 1) * 128].astype(BF16)
            kpe = kpe + dk_ref[:, lo + 128 : lo + 256]
        dkpe_ref[...] = _rope_tile(kpe, cos_v, nsin_v)

    def row(n):
        return pl.BlockSpec((tr, n), lambda i: (i, 0))

    return pl.pallas_call(
        body,
        name=name,
        grid=(S // tr,),
        in_specs=[row(W), row(W), row(MLA_HEADS * V_HEAD), row(128), row(128)],
        out_specs=[row(W), row(W), row(128)],
        out_shape=[jax.ShapeDtypeStruct((S, W), BF16), jax.ShapeDtypeStruct((S, W), BF16), jax.ShapeDtypeStruct((S, 128), F32)],
        compiler_params=pltpu.CompilerParams(dimension_semantics=("parallel",)),
    )(dq_att, dk_att, dv, cos_t, sin_t)


def _lat_bwd(dcqn, dckvn, dkpe, lat, gq, gkv, name, tr=256):
    S = lat.shape[0]

    def body(dcq_ref, dckv_ref, dkpe_ref, lat_ref, gq_ref, gkv_ref, dlat_ref, dgq_ref, dgkv_ref):
        dq, dgq = _rms_bwd_math(dcq_ref[...], lat_ref[:, :Q_LORA], gq_ref[...])
        dkv, dgkv = _rms_bwd_math(dckv_ref[...], lat_ref[:, Q_LORA : Q_LORA + KV_LORA], gkv_ref[...])
        dlat_ref[:, :Q_LORA] = dq.astype(BF16)
        dlat_ref[:, Q_LORA : Q_LORA + KV_LORA] = dkv.astype(BF16)
        dlat_ref[:, Q_LORA + KV_LORA :] = dkpe_ref[...].astype(BF16)

        @pl.when(pl.program_id(0) == 0)
        def _():
            dgq_ref[...] = dgq
            dgkv_ref[...] = dgkv

        @pl.when(pl.program_id(0) != 0)
        def _():
            dgq_ref[...] += dgq
            dgkv_ref[...] += dgkv

    def row(n):
        return pl.BlockSpec((tr, n), lambda i: (i, 0))

    def vec(n):
        return pl.BlockSpec((1, n), lambda i: (0, 0))

    return pl.pallas_call(
        body,
        name=name,
        grid=(S // tr,),
        in_specs=[row(Q_LORA), row(KV_LORA), row(128), row(LAT_PAD), vec(Q_LORA), vec(KV_LORA)],
        out_specs=[row(LAT_PAD), vec(Q_LORA), vec(KV_LORA)],
        out_shape=[
            jax.ShapeDtypeStruct((S, LAT_PAD), BF16),
            jax.ShapeDtypeStruct((1, Q_LORA), F32),
            jax.ShapeDtypeStruct((1, KV_LORA), F32),
        ],
        compiler_params=pltpu.CompilerParams(dimension_semantics=("arbitrary",)),
    )(dcqn, dckvn, dkpe, lat, gq, gkv)


MLA_SCALE = (QK_NOPE + QK_ROPE) ** -0.5
LOG2E = 1.4426950408889634
MLA_C2 = MLA_SCALE * LOG2E
FLASH_T = 1024


def _causal_pairs(n, by_key):
    pairs = [(i, j) for j in range(n) for i in range(j, n)] if by_key else [(i, j) for i in range(n) for j in range(i + 1)]
    return jnp.asarray([p[0] for p in pairs], jnp.int32), jnp.asarray([p[1] for p in pairs], jnp.int32)


def _lanes(x, n):
    return jnp.tile(x, (1, n // 128))


def _flash_grid(npairs, in_specs, out_specs, scratch):
    return pltpu.PrefetchScalarGridSpec(
        num_scalar_prefetch=2, grid=(MLA_HEADS, npairs), in_specs=in_specs, out_specs=out_specs, scratch_shapes=scratch
    )


def _flash2_fwd(q_att, k_att, kv, name, t=FLASH_T):
    S = q_att.shape[0]
    qi_tab, kj_tab = _causal_pairs(S // t, by_key=False)

    def body(qi_ref, kj_ref, q_ref, k_ref, v_ref, o_ref, lse_ref, m_sc, l_sc, acc_sc):
        step = pl.program_id(1)
        qi, kj = qi_ref[step], kj_ref[step]

        @pl.when(kj == 0)
        def _():
            m_sc[...] = jnp.full((t, 128), NEG, F32)
            l_sc[...] = jnp.zeros((t, 128), F32)
            acc_sc[...] = jnp.zeros((t, V_HEAD), F32)

        def update(s):
            m_prev = m_sc[...]
            m_new = jnp.maximum(m_prev, jnp.max(s, axis=1, keepdims=True))
            p = jnp.exp2((s - _lanes(m_new, t)) * MLA_C2)
            alpha = jnp.exp2((m_prev - m_new) * MLA_C2)
            l_sc[...] = alpha * l_sc[...] + jnp.sum(p, axis=1, keepdims=True)
            acc_sc[...] = alpha * acc_sc[...] + _dot(p.astype(BF16), v_ref[...])
            m_sc[...] = m_new

        @pl.when(kj < qi)
        def _():
            update(_dot(q_ref[...], k_ref[...], NT))

        @pl.when(kj == qi)
        def _():
            s = _dot(q_ref[...], k_ref[...], NT)
            rows = lax.broadcasted_iota(jnp.int32, s.shape, 0)
            cols = lax.broadcasted_iota(jnp.int32, s.shape, 1)
            update(jnp.where(cols <= rows, s, NEG))
            l = l_sc[...]
            o_ref[...] = acc_sc[...] / l
            lse_ref[0] = m_sc[...] * MLA_SCALE + jnp.log(l)

    return pl.pallas_call(
        body,
        name=name,
        grid_spec=_flash_grid(
            qi_tab.shape[0],
            [
                pl.BlockSpec((t, HEAD_PAD), lambda h, p, qi, kj: (qi[p], h)),
                pl.BlockSpec((t, HEAD_PAD), lambda h, p, qi, kj: (kj[p], h)),
                pl.BlockSpec((t, V_HEAD), lambda h, p, qi, kj: (kj[p], 2 * h + 1)),
            ],
            [
                pl.BlockSpec((t, V_HEAD), lambda h, p, qi, kj: (qi[p], h)),
                pl.BlockSpec((1, t, 128), lambda h, p, qi, kj: (h, qi[p], 0)),
            ],
            [pltpu.VMEM((t, 128), F32), pltpu.VMEM((t, 128), F32), pltpu.VMEM((t, V_HEAD), F32)],
        ),
        out_shape=[jax.ShapeDtypeStruct((S, MLA_HEADS * V_HEAD), F32), jax.ShapeDtypeStruct((MLA_HEADS, S, 128), F32)],
        compiler_params=pltpu.CompilerParams(dimension_semantics=("parallel", "arbitrary")),
    )(qi_tab, kj_tab, q_att, k_att, kv)


def _flash2_dq(q_att, k_att, kv, o, do, lse, name, t=FLASH_T):
    S = q_att.shape[0]
    qi_tab, kj_tab = _causal_pairs(S // t, by_key=False)

    def body(qi_ref, kj_ref, q_ref, k_ref, v_ref, o_ref, do_ref, lse_ref, dq_ref, dl_ref, acc_sc):
        step = pl.program_id(1)
        qi, kj = qi_ref[step], kj_ref[step]

        @pl.when(kj == 0)
        def _():
            acc_sc[...] = jnp.zeros((t, HEAD_PAD), F32)
            dl = jnp.sum(do_ref[...].astype(F32) * o_ref[...], axis=1, keepdims=True)
            dl_ref[0] = jnp.broadcast_to(dl, (t, 128))

        def update(s):
            k = k_ref[...]
            p = jnp.exp2(s * MLA_C2 - _lanes(lse_ref[0] * LOG2E, t))
            dp = _dot(do_ref[...], v_ref[...], NT)
            ds = p * (dp - _lanes(dl_ref[0], t))
            acc_sc[...] += _dot(ds.astype(BF16), k)

        @pl.when(kj < qi)
        def _():
            update(_dot(q_ref[...], k_ref[...], NT))

        @pl.when(kj == qi)
        def _():
            s = _dot(q_ref[...], k_ref[...], NT)
            rows = lax.broadcasted_iota(jnp.int32, s.shape, 0)
            cols = lax.broadcasted_iota(jnp.int32, s.shape, 1)
            update(jnp.where(cols <= rows, s, NEG))
            dq_ref[...] = acc_sc[...] * MLA_SCALE

    qrow = lambda h, p, qi, kj: (qi[p], h)
    stat = pl.BlockSpec((1, t, 128), lambda h, p, qi, kj: (h, qi[p], 0))
    return pl.pallas_call(
        body,
        name=name,
        grid_spec=_flash_grid(
            qi_tab.shape[0],
            [
                pl.BlockSpec((t, HEAD_PAD), qrow),
                pl.BlockSpec((t, HEAD_PAD), lambda h, p, qi, kj: (kj[p], h)),
                pl.BlockSpec((t, V_HEAD), lambda h, p, qi, kj: (kj[p], 2 * h + 1)),
                pl.BlockSpec((t, V_HEAD), qrow),
                pl.BlockSpec((t, V_HEAD), qrow),
                stat,
            ],
            [pl.BlockSpec((t, HEAD_PAD), qrow), stat],
            [pltpu.VMEM((t, HEAD_PAD), F32)],
        ),
        out_shape=[jax.ShapeDtypeStruct((S, MLA_HEADS * HEAD_PAD), F32), jax.ShapeDtypeStruct((MLA_HEADS, S, 128), F32)],
        compiler_params=pltpu.CompilerParams(dimension_semantics=("parallel", "arbitrary")),
    )(qi_tab, kj_tab, q_att, k_att, kv, o, do, lse)


def _flash2_dkv(q_att, k_att, kv, do, lse_row, delta_row, name, t=FLASH_T):
    S = q_att.shape[0]
    n = S // t
    qi_tab, kj_tab = _causal_pairs(n, by_key=True)

    def body(qi_ref, kj_ref, q_ref, k_ref, v_ref, do_ref, lse_ref, dl_ref, dk_ref, dv_ref, dk_sc, dv_sc):
        step = pl.program_id(1)
        qi, kj = qi_ref[step], kj_ref[step]

        def update(st):
            q, do_v = q_ref[...], do_ref[...]
            pt = jnp.exp2(st * MLA_C2 - lse_ref[0] * LOG2E)
            dv_sc[...] += _dot(pt.astype(BF16), do_v)
            dpt = _dot(v_ref[...], do_v, NT)
            dst = pt * (dpt - dl_ref[0])
            dk_sc[...] += _dot(dst.astype(BF16), q)

        @pl.when(qi == kj)
        def _():
            dk_sc[...] = jnp.zeros((t, HEAD_PAD), F32)
            dv_sc[...] = jnp.zeros((t, V_HEAD), F32)
            st = _dot(k_ref[...], q_ref[...], NT)
            keys = lax.broadcasted_iota(jnp.int32, st.shape, 0)
            qs = lax.broadcasted_iota(jnp.int32, st.shape, 1)
            update(jnp.where(keys <= qs, st, NEG))

        @pl.when(qi > kj)
        def _():
            update(_dot(k_ref[...], q_ref[...], NT))

        @pl.when(qi == n - 1)
        def _():
            dk_ref[...] = dk_sc[...] * MLA_SCALE
            dv_ref[...] = dv_sc[...]

    qrow = lambda h, p, qi, kj: (qi[p], h)
    krow = lambda h, p, qi, kj: (kj[p], h)
    stat = pl.BlockSpec((1, 1, t), lambda h, p, qi, kj: (h, 0, qi[p]))
    return pl.pallas_call(
        body,
        name=name,
        grid_spec=_flash_grid(
            qi_tab.shape[0],
            [
                pl.BlockSpec((t, HEAD_PAD), qrow),
                pl.BlockSpec((t, HEAD_PAD), krow),
                pl.BlockSpec((t, V_HEAD), lambda h, p, qi, kj: (kj[p], 2 * h + 1)),
                pl.BlockSpec((t, V_HEAD), qrow),
                stat,
                stat,
            ],
            [pl.BlockSpec((t, HEAD_PAD), krow), pl.BlockSpec((t, V_HEAD), krow)],
            [pltpu.VMEM((t, HEAD_PAD), F32), pltpu.VMEM((t, V_HEAD), F32)],
        ),
        out_shape=[jax.ShapeDtypeStruct((S, MLA_HEADS * HEAD_PAD), F32), jax.ShapeDtypeStruct((S, MLA_HEADS * V_HEAD), F32)],
        compiler_params=pltpu.CompilerParams(dimension_semantics=("parallel", "arbitrary")),
    )(qi_tab, kj_tab, q_att, k_att, kv, do, lse_row, delta_row)


DIL_SCALE = DIL_HD**-0.5


def _dil_bias():
    slopes = 2.0 ** (-ALIBI_MAX_BIAS * np.arange(1, DIL_HEADS + 1, dtype=np.float64) / DIL_HEADS)
    slopes = slopes.astype(np.float32).reshape(DIL_GROUPS, DIL_HG)
    p = np.arange(DIL_BLK)[:, None]
    kidx = np.arange(2 * DIL_BLK)[None, :]
    j = p + DIL_BLK - kidx
    out = np.zeros((DIL_GROUPS, DIL_HG, DIL_BLK, 2 * DIL_BLK), np.float32)
    for g, (window, dil) in enumerate(DIL_PATTERNS):
        valid = (j >= 0) & (j <= window // dil)
        for h in range(DIL_HG):
            alibi = -slopes[g, h] * (dil * j).astype(np.float32)
            out[g, h] = np.where(valid, alibi, np.float32(NEG))
    return jnp.asarray(out)


def _dil_fwd_group(view, bias_g, g, dil, name):
    L = view.shape[0]
    nb = L // DIL_BLK
    nqc = 3 * DIL_QKV // 128

    def body(bias_ref, q_ref, k_ref, v_ref, o_ref, lse_ref):
        bias = bias_ref[0]

        def attend(q, kk, vv, b):
            s = _dot(q, kk, NT) * DIL_SCALE + b
            m = jnp.max(s, axis=1, keepdims=True)
            e = jnp.exp(s - m)
            l = jnp.sum(e, axis=1, keepdims=True)
            p = e * (1.0 / l)
            return _dot(p.astype(BF16), vv), m + jnp.log(l)

        o0, l0 = attend(q_ref[0:DIL_BLK, :], k_ref[0:DIL_BLK, :], v_ref[0:DIL_BLK, :], bias[:, DIL_BLK:])
        o_ref[0:DIL_BLK, :] = o0
        lse_ref[0:DIL_BLK, :] = jnp.broadcast_to(l0, (DIL_BLK, 128))

        def step(n, carry):
            r0 = pl.multiple_of(n * DIL_BLK, DIL_BLK)
            p0 = pl.multiple_of((n - 1) * DIL_BLK, DIL_BLK)
            o, l = attend(q_ref[pl.ds(r0, DIL_BLK), :], k_ref[pl.ds(p0, 2 * DIL_BLK), :], v_ref[pl.ds(p0, 2 * DIL_BLK), :], bias)
            o_ref[pl.ds(r0, DIL_BLK), :] = o
            lse_ref[pl.ds(r0, DIL_BLK), :] = jnp.broadcast_to(l, (DIL_BLK, 128))
            return carry

        lax.fori_loop(1, nb, step, 0)

    def col(base):
        return pl.BlockSpec((L, 128), lambda i: (0, (i // DIL_HG) * nqc + base + g * DIL_HG + i % DIL_HG))

    out = pl.BlockSpec((L, 128), lambda i: (0, i))
    return pl.pallas_call(
        body,
        name=name,
        grid=(dil * DIL_HG,),
        in_specs=[pl.BlockSpec((1, DIL_BLK, 2 * DIL_BLK), lambda i: (i % DIL_HG, 0, 0)), col(0), col(DIL_QKV // 128), col(2 * DIL_QKV // 128)],
        out_specs=[out, out],
        out_shape=[jax.ShapeDtypeStruct((L, dil * DIL_OUT), F32), jax.ShapeDtypeStruct((L, dil * DIL_OUT), F32)],
        compiler_params=pltpu.CompilerParams(dimension_semantics=("parallel",)),
    )(bias_g, view, view, view)


def _dil_combine(os_, ls_, name, tr=512):
    S = os_[0].shape[0]

    def body(o0, o1, o2, l0, l1, l2, out_ref, lse_ref):
        a, b, c = l0[...], l1[...], l2[...]
        m = jnp.maximum(jnp.maximum(a, b), c)
        ea, eb, ec = jnp.exp(a - m), jnp.exp(b - m), jnp.exp(c - m)
        den = ea + eb + ec
        inv = 1.0 / den
        out_ref[...] = (ea * inv) * o0[...] + (eb * inv) * o1[...] + (ec * inv) * o2[...]
        lse_ref[...] = m + jnp.log(den)

    row = pl.BlockSpec((tr, DIL_OUT), lambda i: (i, 0))
    return pl.pallas_call(
        body,
        name=name,
        grid=(S // tr,),
        in_specs=[row] * 6,
        out_specs=[row, row],
        out_shape=[jax.ShapeDtypeStruct((S, DIL_OUT), F32)] * 2,
        compiler_params=pltpu.CompilerParams(dimension_semantics=("parallel",)),
    )(*os_, *ls_)


def _dil_rowdot(dod, od, name, tr=512):
    S = dod.shape[0]

    def body(d_ref, o_ref, dd_ref, db_ref):
        db_ref[...] = d_ref[...].astype(BF16)
        for h in range(DIL_HG):
            sl = slice(h * 128, (h + 1) * 128)
            sm = jnp.sum(d_ref[:, sl] * o_ref[:, sl], axis=1, keepdims=True)
            dd_ref[:, sl] = jnp.broadcast_to(sm, (tr, 128))

    row = pl.BlockSpec((tr, DIL_OUT), lambda i: (i, 0))
    return pl.pallas_call(
        body,
        name=name,
        grid=(S // tr,),
        in_specs=[row, row],
        out_specs=[row, row],
        out_shape=[jax.ShapeDtypeStruct((S, DIL_OUT), F32), jax.ShapeDtypeStruct((S, DIL_OUT), BF16)],
        compiler_params=pltpu.CompilerParams(dimension_semantics=("parallel",)),
    )(dod, od)


def _dil_bwd_group(view, bias_g, do_view, dd_view, lse_view, g, dil, name):
    L = view.shape[0]
    nb = L // DIL_BLK
    nqc = 3 * DIL_QKV // 128

    def body(bias_ref, q_ref, k_ref, v_ref, do_ref, dd_ref, lse_ref, dq_ref, dk_ref, dv_ref, dk_sc, dv_sc):
        bias = bias_ref[0]
        dk_sc[...] = jnp.zeros((L, 128), F32)
        dv_sc[...] = jnp.zeros((L, 128), F32)

        def grads(q, kk, vv, do, dd, lse, b):
            s = _dot(q, kk, NT) * DIL_SCALE + b
            p = jnp.exp(s - lse)
            dp = _dot(do, vv, NT)
            ds = ((p * (dp - dd)) * DIL_SCALE).astype(BF16)
            return _dot(ds, kk), _dot(ds, q, TN), _dot(p.astype(BF16), do, TN)

        first = slice(0, DIL_BLK)
        dq0, dk0, dv0 = grads(
            q_ref[first, :], k_ref[first, :], v_ref[first, :], do_ref[first, :], dd_ref[first, 0:1], lse_ref[first, 0:1], bias[:, DIL_BLK:]
        )
        dq_ref[first, :] = dq0.astype(BF16)
        dk_sc[first, :] += dk0
        dv_sc[first, :] += dv0

        def step(n, carry):
            r0 = pl.multiple_of(n * DIL_BLK, DIL_BLK)
            p0 = pl.multiple_of((n - 1) * DIL_BLK, DIL_BLK)
            cur, both = pl.ds(r0, DIL_BLK), pl.ds(p0, 2 * DIL_BLK)
            dq, dk, dv = grads(q_ref[cur, :], k_ref[both, :], v_ref[both, :], do_ref[cur, :], dd_ref[cur, 0:1], lse_ref[cur, 0:1], bias)
            dq_ref[cur, :] = dq.astype(BF16)
            dk_sc[both, :] += dk
            dv_sc[both, :] += dv
            return carry

        lax.fori_loop(1, nb, step, 0)
        dk_ref[...] = dk_sc[...].astype(BF16)
        dv_ref[...] = dv_sc[...].astype(BF16)

    def col(base):
        return pl.BlockSpec((L, 128), lambda i: (0, (i // DIL_HG) * nqc + base + g * DIL_HG + i % DIL_HG))

    out = pl.BlockSpec((L, 128), lambda i: (0, i))
    return pl.pallas_call(
        body,
        name=name,
        grid=(dil * DIL_HG,),
        in_specs=[
            pl.BlockSpec((1, DIL_BLK, 2 * DIL_BLK), lambda i: (i % DIL_HG, 0, 0)),
            col(0),
            col(DIL_QKV // 128),
            col(2 * DIL_QKV // 128),
            out,
            out,
            out,
        ],
        out_specs=[out, out, out],
        out_shape=[jax.ShapeDtypeStruct((L, dil * DIL_OUT), BF16)] * 3,
        scratch_shapes=[pltpu.VMEM((L, 128), F32), pltpu.VMEM((L, 128), F32)],
        compiler_params=pltpu.CompilerParams(dimension_semantics=("parallel",)),
    )(bias_g, view, view, view, do_view, dd_view, lse_view)


def _merge_fwd(gates, o_a, o_b, name, tr=256):
    S = o_a.shape[0]

    def body(ga_ref, gb_ref, oa_ref, ob_ref, m_ref):
        m_ref[...] = (ga_ref[...] * oa_ref[...] + gb_ref[...] * ob_ref[...]).astype(BF16)

    row = pl.BlockSpec((tr, D_MODEL), lambda i: (i, 0))
    return pl.pallas_call(
        body,
        name=name,
        grid=(S // tr,),
        in_specs=[row, pl.BlockSpec((tr, D_MODEL), lambda i: (i, 1)), row, row],
        out_specs=row,
        out_shape=jax.ShapeDtypeStruct((S, D_MODEL), BF16),
        compiler_params=pltpu.CompilerParams(dimension_semantics=("parallel",)),
    )(gates, gates, o_a, o_b)


def _merge_bwd(dmrg, gates, o_a, o_b, name, tr=256):
    S = o_a.shape[0]

    def body(dm_ref, ga_ref, gb_ref, oa_ref, ob_ref, doa_ref, dob_ref, dga_ref, dgb_ref, dba_ref, dbb_ref):
        dm, ga, gb = dm_ref[...], ga_ref[...], gb_ref[...]
        doa_ref[...] = (dm * ga).astype(BF16)
        dob_ref[...] = (dm * gb).astype(BF16)
        dga = (dm * oa_ref[...]) * (ga * (1.0 - ga))
        dgb = (dm * ob_ref[...]) * (gb * (1.0 - gb))
        dga_ref[...] = dga.astype(BF16)
        dgb_ref[...] = dgb.astype(BF16)
        sa = jnp.sum(dga, axis=0, keepdims=True)
        sb = jnp.sum(dgb, axis=0, keepdims=True)

        @pl.when(pl.program_id(0) == 0)
        def _():
            dba_ref[...] = sa
            dbb_ref[...] = sb

        @pl.when(pl.program_id(0) != 0)
        def _():
            dba_ref[...] += sa
            dbb_ref[...] += sb

    row = pl.BlockSpec((tr, D_MODEL), lambda i: (i, 0))
    row1 = pl.BlockSpec((tr, D_MODEL), lambda i: (i, 1))
    vec = pl.BlockSpec((1, D_MODEL), lambda i: (0, 0))
    outs = pl.pallas_call(
        body,
        name=name,
        grid=(S // tr,),
        in_specs=[row, row, row1, row, row],
        out_specs=[row, row, row, row, vec, vec],
        out_shape=[jax.ShapeDtypeStruct((S, D_MODEL), BF16)] * 4 + [jax.ShapeDtypeStruct((1, D_MODEL), F32)] * 2,
        compiler_params=pltpu.CompilerParams(dimension_semantics=("arbitrary",)),
    )(dmrg, gates, gates, o_a, o_b)
    return outs


CONV_TR = 512
CONV_TC = 512
N_FFC = D_FF_PAD // CONV_TC


def _shift_down(x, edge_rows, k):
    out = pltpu.roll(x, k, 0)
    row = lax.broadcasted_iota(jnp.int32, x.shape, 0)
    for i in range(k):
        out = jnp.where(row == i, edge_rows[i], out)
    return out


def _conv_taps(x, halo_ref, live, w_ref, b_ref):
    h6, h7 = halo_ref[6:7, :] * live, halo_ref[7:8, :] * live
    s1 = _shift_down(x, [h7], 1)
    s2 = _shift_down(x, [h6, h7], 2)
    u = ((b_ref[...] + w_ref[0:1, :] * s2) + w_ref[1:2, :] * s1) + w_ref[2:3, :] * x
    return u, s1, s2


def _prev_halo(tr):
    return lambda i, j: (jnp.maximum(i * (tr // 8) - 1, 0), j)


def _ffn_fwd(u0, cw, cb, name):
    S = u0.shape[0]
    tr, tc = CONV_TR, CONV_TC

    def body(up_ref, gt_ref, hup_ref, hgt_ref, wu_ref, wg_ref, bu_ref, bg_ref, a_ref):
        live = (pl.program_id(0) > 0).astype(F32)
        up, _, _ = _conv_taps(up_ref[...], hup_ref, live, wu_ref, bu_ref)
        gt, _, _ = _conv_taps(gt_ref[...], hgt_ref, live, wg_ref, bg_ref)
        a_ref[...] = ((gt * jax.nn.sigmoid(gt)) * up).astype(BF16)

    return pl.pallas_call(
        body,
        name=name,
        grid=(S // tr, N_FFC),
        in_specs=[
            pl.BlockSpec((tr, tc), lambda i, j: (i, j)),
            pl.BlockSpec((tr, tc), lambda i, j: (i, j + N_FFC)),
            pl.BlockSpec((8, tc), _prev_halo(tr)),
            pl.BlockSpec((8, tc), lambda i, j: (jnp.maximum(i * (tr // 8) - 1, 0), j + N_FFC)),
            pl.BlockSpec((8, tc), lambda i, j: (0, j)),
            pl.BlockSpec((8, tc), lambda i, j: (0, j + N_FFC)),
            pl.BlockSpec((1, tc), lambda i, j: (0, j)),
            pl.BlockSpec((1, tc), lambda i, j: (0, j + N_FFC)),
        ],
        out_specs=pl.BlockSpec((tr, tc), lambda i, j: (i, j)),
        out_shape=jax.ShapeDtypeStruct((S, D_FF_PAD), BF16),
        compiler_params=pltpu.CompilerParams(dimension_semantics=("parallel", "parallel")),
    )(u0, u0, u0, u0, cw, cw, cb, cb)


def _ffn_bwd(u0, da, cw, cb, name):
    S = u0.shape[0]
    tr, tc = CONV_TR, CONV_TC
    nrow, te = S // tr, tr + 8

    def body(up_ref, gt_ref, hup_ref, hgt_ref, nup_ref, ngt_ref, da_ref, nda_ref, wu_ref, wg_ref, bu_ref, bg_ref, du0_ref, dcw_ref, dcb_ref):
        i = pl.program_id(1)
        prev_live = (i > 0).astype(F32)
        next_live = (i < nrow - 1).astype(F32)

        def conv(x_ref, nx_ref, h_ref, w_ref, b_ref):
            x = jnp.concatenate([x_ref[...], nx_ref[...] * next_live], axis=0)
            h6, h7 = h_ref[6:7, :] * prev_live, h_ref[7:8, :] * prev_live
            s1 = _shift_down(x, [h7], 1)
            s2 = _shift_down(x, [h6, h7], 2)
            return ((b_ref[...] + w_ref[0:1, :] * s2) + w_ref[1:2, :] * s1) + w_ref[2:3, :] * x, x, s1, s2

        up, xu0, xu1, xu2 = conv(up_ref, nup_ref, hup_ref, wu_ref, bu_ref)
        gt, xg0, xg1, xg2 = conv(gt_ref, ngt_ref, hgt_ref, wg_ref, bg_ref)
        da_v = jnp.concatenate([da_ref[...], nda_ref[...] * next_live], axis=0)
        sg = jax.nn.sigmoid(gt)
        d_up = da_v * (gt * sg)
        d_gt = (da_v * up) * (sg * (1.0 + gt * (1.0 - sg)))
        tap = lax.broadcasted_iota(jnp.int32, (8, tc), 0)

        def finish(half, du, x0, x1, x2, w_ref):
            n1 = pltpu.roll(du, te - 1, 0)
            n2 = pltpu.roll(du, te - 2, 0)
            du0 = (w_ref[2:3, :] * du + w_ref[1:2, :] * n1) + w_ref[0:1, :] * n2
            du0_ref[half] = du0[:tr].astype(BF16)
            d = du[:tr]
            dcw = jnp.where(
                tap == 0,
                jnp.sum(d * x2[:tr], axis=0, keepdims=True),
                jnp.where(tap == 1, jnp.sum(d * x1[:tr], axis=0, keepdims=True), jnp.where(tap == 2, jnp.sum(d * x0[:tr], axis=0, keepdims=True), 0.0)),
            )
            dcb = jnp.sum(d, axis=0, keepdims=True)

            @pl.when(i == 0)
            def _():
                dcw_ref[half] = dcw
                dcb_ref[half] = dcb

            @pl.when(i != 0)
            def _():
                dcw_ref[half] += dcw
                dcb_ref[half] += dcb

        finish(0, d_up, xu0, xu1, xu2, wu_ref)
        finish(1, d_gt, xg0, xg1, xg2, wg_ref)

    def prev8(off):
        return pl.BlockSpec((8, tc), lambda j, i: (jnp.maximum(i * (tr // 8) - 1, 0), j + off))

    def next8(off):
        return pl.BlockSpec((8, tc), lambda j, i: (jnp.minimum((i + 1) * (tr // 8), S // 8 - 1), j + off))

    return pl.pallas_call(
        body,
        name=name,
        grid=(N_FFC, nrow),
        in_specs=[
            pl.BlockSpec((tr, tc), lambda j, i: (i, j)),
            pl.BlockSpec((tr, tc), lambda j, i: (i, j + N_FFC)),
            prev8(0),
            prev8(N_FFC),
            next8(0),
            next8(N_FFC),
            pl.BlockSpec((tr, tc), lambda j, i: (i, j)),
            next8(0),
            pl.BlockSpec((8, tc), lambda j, i: (0, j)),
            pl.BlockSpec((8, tc), lambda j, i: (0, j + N_FFC)),
            pl.BlockSpec((1, tc), lambda j, i: (0, j)),
            pl.BlockSpec((1, tc), lambda j, i: (0, j + N_FFC)),
        ],
        out_specs=[
            pl.BlockSpec((2, tr, tc), lambda j, i: (0, i, j)),
            pl.BlockSpec((2, 8, tc), lambda j, i: (0, 0, j)),
            pl.BlockSpec((2, 1, tc), lambda j, i: (0, 0, j)),
        ],
        out_shape=[
            jax.ShapeDtypeStruct((2, S, D_FF_PAD), BF16),
            jax.ShapeDtypeStruct((2, 8, D_FF_PAD), F32),
            jax.ShapeDtypeStruct((2, 1, D_FF_PAD), F32),
        ],
        compiler_params=pltpu.CompilerParams(dimension_semantics=("parallel", "arbitrary")),
    )(u0, u0, u0, u0, u0, u0, da, da, cw, cw, cb, cb)


def _adamw(w, g, m, v, name):
    R, C = w.shape
    tr = R
    for cand in (256, 128, 64, 32, 16, 8):
        if R % cand == 0 and R > cand:
            tr = cand
            break

    def body(w_ref, g_ref, m_ref, v_ref, d_ref, nm_ref, nv_ref):
        gv = g_ref[...]
        nm = ADAM_B1 * m_ref[...] + (1.0 - ADAM_B1) * gv
        nv = ADAM_B2 * v_ref[...] + (1.0 - ADAM_B2) * (gv * gv)
        m_hat = nm / (1.0 - ADAM_B1**ADAM_STEP)
        v_hat = nv / (1.0 - ADAM_B2**ADAM_STEP)
        d_ref[...] = -ADAM_LR * (m_hat / (jnp.sqrt(v_hat) + ADAM_EPS) + ADAM_WD * w_ref[...])
        nm_ref[...] = nm
        nv_ref[...] = nv

    blk = pl.BlockSpec((tr, C), lambda i: (i, 0))
    return pl.pallas_call(
        body,
        name=name,
        grid=(R // tr,),
        in_specs=[blk] * 4,
        out_specs=[blk] * 3,
        out_shape=[jax.ShapeDtypeStruct((R, C), F32)] * 3,
        compiler_params=pltpu.CompilerParams(dimension_semantics=("parallel",)),
    )(w, g, m, v)


ANY = pl.BlockSpec(memory_space=pl.ANY)


def _all_gather(blocks, name):
    n = len(blocks)

    def body(*refs):
        x_refs, out_refs = refs[:n], refs[n : 2 * n]
        send_sems, recv_sems, local_sems = refs[2 * n :]
        x, y, c = lax.axis_index("x"), lax.axis_index("y"), lax.axis_index("c")
        me, sibling = (x, y, c), (x, y, 1 - c)
        chips = [(1 - x, y), (x, 1 - y), (1 - x, 1 - y)]

        def slot(a, px, py, pc):
            return out_refs[a].at[4 * px + 2 * py + pc]

        def copy(a, k, blk, to, src=None):
            return pltpu.make_async_remote_copy(
                src_ref=slot(a, *blk) if src is None else src,
                dst_ref=slot(a, *blk),
                send_sem=send_sems.at[7 * a + k],
                recv_sem=recv_sems.at[7 * a + k],
                device_id=to,
                device_id_type=MESH,
            )

        mine = [pltpu.make_async_copy(x_refs[a], slot(a, *me), local_sems.at[a]) for a in range(n)]
        sent = []
        for a in range(n):
            mine[a].start()
            first = [copy(a, 0, me, sibling, src=x_refs[a])]
            first += [copy(a, 1 + j, me, (*chip, c), src=x_refs[a]) for j, chip in enumerate(chips)]
            for cp in first:
                cp.start()
            sent += first
        for a in range(n):
            for j, chip in enumerate(chips):
                copy(a, 1 + j, (*chip, c), me).wait_recv()
                passed = copy(a, 4 + j, (*chip, c), sibling)
                passed.start()
                sent.append(passed)
        for a in range(n):
            copy(a, 0, sibling, me).wait_recv()
            for j, chip in enumerate(chips):
                copy(a, 4 + j, (*chip, 1 - c), me).wait_recv()
        for cp in sent:
            cp.wait_send()
        for cp in mine:
            cp.wait()

    return pl.pallas_call(
        body,
        name=name,
        out_shape=[jax.ShapeDtypeStruct((N_DEV,) + b.shape, b.dtype) for b in blocks],
        in_specs=[ANY] * n,
        out_specs=[ANY] * n,
        scratch_shapes=[pltpu.SemaphoreType.DMA((7 * n,)), pltpu.SemaphoreType.DMA((7 * n,)), pltpu.SemaphoreType.DMA((n,))],
    )(*blocks)


def _pair_exchange(gs, name):
    n = len(gs)

    def body(*refs):
        g_refs, out_refs = refs[:n], refs[n : 2 * n]
        send_sems, recv_sems = refs[2 * n :]
        x, y, c = lax.axis_index("x"), lax.axis_index("y"), lax.axis_index("c")
        copies = [
            pltpu.make_async_remote_copy(
                src_ref=g_refs[a].at[2 * k + (1 - c)],
                dst_ref=out_refs[a].at[k],
                send_sem=send_sems.at[N_CHIP * a + k],
                recv_sem=recv_sems.at[N_CHIP * a + k],
                device_id=(x, y, 1 - c),
                device_id_type=MESH,
            )
            for a in range(n)
            for k in range(N_CHIP)
        ]
        for cp in copies:
            cp.start()
        for cp in copies:
            cp.wait()

    return pl.pallas_call(
        body,
        name=name,
        out_shape=[jax.ShapeDtypeStruct((N_CHIP,) + g.shape[1:], g.dtype) for g in gs],
        in_specs=[ANY] * n,
        out_specs=[ANY] * n,
        scratch_shapes=[pltpu.SemaphoreType.DMA((N_CHIP * n,)), pltpu.SemaphoreType.DMA((N_CHIP * n,))],
    )(*gs)


def _row_tile(rows):
    return max(t for t in range(16, 353, 16) if rows % t == 0)


def _pair_add(g, recv, core, name):
    _, R, C = g.shape
    tr = _row_tile(R)

    def body(core_ref, g_ref, r_ref, o_ref):
        o_ref[...] = (g_ref[...].astype(F32) + r_ref[...].astype(F32)).astype(o_ref.dtype)

    return pl.pallas_call(
        body,
        name=name,
        grid_spec=pltpu.PrefetchScalarGridSpec(
            num_scalar_prefetch=1,
            grid=(N_CHIP, R // tr),
            in_specs=[
                pl.BlockSpec((1, tr, C), lambda k, i, core_ref: (2 * k + core_ref[0], i, 0)),
                pl.BlockSpec((1, tr, C), lambda k, i, core_ref: (k, i, 0)),
            ],
            out_specs=pl.BlockSpec((1, tr, C), lambda k, i, core_ref: (k, i, 0)),
        ),
        out_shape=jax.ShapeDtypeStruct((N_CHIP, R, C), g.dtype),
        compiler_params=pltpu.CompilerParams(dimension_semantics=("parallel", "parallel")),
    )(core, g, recv)


def _chip_exchange(pairs, name):
    n = len(pairs)

    def body(*refs):
        p_refs, out_refs = refs[:n], refs[n : 2 * n]
        send_sems, recv_sems, local_sems = refs[2 * n :]
        x, y, c = lax.axis_index("x"), lax.axis_index("y"), lax.axis_index("c")
        mine = 2 * x + y
        chips = [(1 - x, y), (x, 1 - y), (1 - x, 1 - y)]
        local = [pltpu.make_async_copy(p_refs[a].at[mine], out_refs[a].at[mine], local_sems.at[a]) for a in range(n)]

        def copy(a, j, src_slot, dst_slot):
            px, py = chips[j]
            return pltpu.make_async_remote_copy(
                src_ref=p_refs[a].at[src_slot],
                dst_ref=out_refs[a].at[dst_slot],
                send_sem=send_sems.at[3 * a + j],
                recv_sem=recv_sems.at[3 * a + j],
                device_id=(px, py, c),
                device_id_type=MESH,
            )

        sent = []
        for a in range(n):
            local[a].start()
            for j, (px, py) in enumerate(chips):
                cp = copy(a, j, 2 * px + py, mine)
                cp.start()
                sent.append(cp)
        for a in range(n):
            for j, (px, py) in enumerate(chips):
                copy(a, j, mine, 2 * px + py).wait_recv()
        for cp in sent:
            cp.wait_send()
        for cp in local:
            cp.wait()

    return pl.pallas_call(
        body,
        name=name,
        out_shape=[jax.ShapeDtypeStruct(p.shape, p.dtype) for p in pairs],
        in_specs=[ANY] * n,
        out_specs=[ANY] * n,
        scratch_shapes=[pltpu.SemaphoreType.DMA((3 * n,)), pltpu.SemaphoreType.DMA((3 * n,)), pltpu.SemaphoreType.DMA((n,))],
    )(*pairs)


HBM = pl.BlockSpec(memory_space=pltpu.HBM)
SEM = pl.BlockSpec(memory_space=pltpu.SEMAPHORE)
EFFECT = pltpu.SideEffectType.DATAFLOW_SIDE_EFFECTING
RELATIONS = tuple((dx, dy, dc) for dx in (0, 1) for dy in (0, 1) for dc in (0, 1))[1:]


def _related(rel):
    x, y, c = lax.axis_index("x"), lax.axis_index("y"), lax.axis_index("c")
    return (1 - x if rel[0] else x, 1 - y if rel[1] else y, 1 - c if rel[2] else c)


def _dev_index(pos):
    return 4 * pos[0] + 2 * pos[1] + pos[2]


def _exchange_start(srcs, by_slot, after, name):
    n = len(srcs)
    lands = [lax.empty((N_DEV,) + (s.shape[1:] if by_slot else s.shape), s.dtype) for s in srcs]

    def body(*refs):
        src_refs, land_refs = refs[:n], refs[n : 2 * n]
        send_sems, recv_sems = refs[2 * n + 1], refs[2 * n + 2]
        token = refs[-1]
        me = _dev_index(_related((0, 0, 0)))
        for a in range(n):
            for k, rel in enumerate(RELATIONS):
                peer = _related(rel)
                pltpu.make_async_remote_copy(
                    src_ref=src_refs[a].at[_dev_index(peer)] if by_slot else src_refs[a],
                    dst_ref=land_refs[a].at[me],
                    send_sem=send_sems.at[7 * a + k],
                    recv_sem=recv_sems.at[7 * a + k],
                    device_id=peer,
                    device_id_type=MESH,
                ).start()
        token[...] = jnp.zeros_like(token)

    def hbm(a):
        return pltpu.HBM(a.shape, a.dtype)

    outs = pl.pallas_call(
        body,
        name=name,
        out_shape=(
            pltpu.SemaphoreType.DMA((7 * n,)),
            pltpu.SemaphoreType.DMA((7 * n,)),
            *[hbm(s) for s in srcs],
            *[hbm(l) for l in lands],
            jax.ShapeDtypeStruct((8, 128), F32),
        ),
        in_specs=[HBM] * (2 * n) + [ANY],
        out_specs=(SEM, SEM, *[HBM] * (2 * n), pl.BlockSpec(memory_space=pltpu.VMEM)),
        input_output_aliases={i: 2 + i for i in range(2 * n)},
        compiler_params=pltpu.CompilerParams(has_side_effects=EFFECT),
    )(*[pltpu.with_memory_space_constraint(a, pltpu.HBM) for a in list(srcs) + lands], after)
    return (outs[0], outs[1], list(outs[2 : 2 + n]), list(outs[2 + n : 2 + 2 * n])), outs[-1]


def _exchange_wait(handle, by_slot, after, name):
    send_sems, recv_sems, srcs, lands = handle
    n = len(srcs)

    def body(*refs):
        src_refs, land_refs = refs[:n], refs[n : 2 * n]
        s_sems, r_sems = refs[2 * n], refs[2 * n + 1]
        for a in range(n):
            for k, rel in enumerate(RELATIONS):
                copy = pltpu.make_async_remote_copy(
                    src_ref=src_refs[a].at[0] if by_slot else src_refs[a],
                    dst_ref=land_refs[a].at[0],
                    send_sem=s_sems.at[7 * a + k],
                    recv_sem=r_sems.at[7 * a + k],
                    device_id=_related(rel),
                    device_id_type=MESH,
                )
                copy.wait_send()
                copy.wait_recv()

    outs = pl.pallas_call(
        body,
        name=name,
        out_shape=tuple(pltpu.HBM(a.shape, a.dtype) for a in srcs + lands),
        in_specs=[HBM] * (2 * n) + [SEM, SEM, ANY],
        out_specs=tuple([HBM] * (2 * n)),
        input_output_aliases={i: i for i in range(2 * n)},
        compiler_params=pltpu.CompilerParams(has_side_effects=EFFECT),
    )(*srcs, *lands, send_sems, recv_sems, after)
    return list(outs[:n]), list(outs[n:])


def _slot_sum(parts, name):
    n, R, C = parts.shape
    tr = _row_tile(R) if R % 16 == 0 else R

    def body(p_ref, o_ref):
        acc = p_ref[0].astype(F32)
        for k in range(1, n):
            acc = acc + p_ref[k].astype(F32)
        o_ref[...] = acc

    return pl.pallas_call(
        body,
        name=name,
        grid=(R // tr,),
        in_specs=[pl.BlockSpec((n, tr, C), lambda i: (0, i, 0))],
        out_specs=pl.BlockSpec((tr, C), lambda i: (i, 0)),
        out_shape=jax.ShapeDtypeStruct((R, C), F32),
        compiler_params=pltpu.CompilerParams(dimension_semantics=("parallel",)),
    )(parts)


W_IN_TC = 256
W_IN_BOUNDS = (0, LAT, LAT + 3 * DIL_QKV, LAT + 3 * DIL_QKV + D_MODEL, D_IN)


def _w_in_regroup(slots, name):
    tc = W_IN_TC

    def body(s_ref, lat_ref, dqkv_ref, g_ref, buf):
        for j in range(N_DEV):
            buf[j * IN_ROWS : (j + 1) * IN_ROWS, :] = s_ref[j].astype(F32)[:IN_ROWS, :]
        lat_ref[:LAT, :] = buf[:LAT, :].astype(BF16)
        lat_ref[LAT:, :] = jnp.zeros((LAT_PAD - LAT, tc), BF16)
        dqkv_ref[...] = buf[W_IN_BOUNDS[1] : W_IN_BOUNDS[2], :].astype(BF16)
        g_ref[...] = buf[W_IN_BOUNDS[2] :, :].astype(BF16)

    def col(rows):
        return pl.BlockSpec((rows, tc), lambda k: (0, k))

    return pl.pallas_call(
        body,
        name=name,
        grid=(D_MODEL // tc,),
        in_specs=[pl.BlockSpec((N_DEV, IN_ROWS_PAD, tc), lambda k: (0, 0, k))],
        out_specs=[col(LAT_PAD), col(3 * DIL_QKV), col(2 * D_MODEL)],
        out_shape=[
            jax.ShapeDtypeStruct((LAT_PAD, D_MODEL), BF16),
            jax.ShapeDtypeStruct((3 * DIL_QKV, D_MODEL), BF16),
            jax.ShapeDtypeStruct((2 * D_MODEL, D_MODEL), BF16),
        ],
        scratch_shapes=[pltpu.VMEM((D_IN, tc), F32)],
        compiler_params=pltpu.CompilerParams(dimension_semantics=("parallel",)),
    )(slots)


def _w_in_grad_regroup(g_lat, g_dqkv, g_ga, g_gb, name):
    tc = W_IN_TC

    def body(lat_ref, dqkv_ref, ga_ref, gb_ref, o_ref, buf):
        b = W_IN_BOUNDS
        buf[b[0] : b[1], :] = lat_ref[:LAT, :].astype(F32)
        buf[b[1] : b[2], :] = dqkv_ref[...].astype(F32)
        buf[b[2] : b[3], :] = ga_ref[...].astype(F32)
        buf[b[3] : b[4], :] = gb_ref[...].astype(F32)
        fill = jnp.zeros((IN_ROWS_PAD - IN_ROWS, tc), F32)
        for j in range(N_DEV):
            o_ref[j] = jnp.concatenate([buf[j * IN_ROWS : (j + 1) * IN_ROWS, :], fill], axis=0).astype(BF16)

    def col(rows):
        return pl.BlockSpec((rows, tc), lambda k: (0, k))

    return pl.pallas_call(
        body,
        name=name,
        grid=(D_MODEL // tc,),
        in_specs=[col(LAT_PAD), col(3 * DIL_QKV), col(D_MODEL), col(D_MODEL)],
        out_specs=pl.BlockSpec((N_DEV, IN_ROWS_PAD, tc), lambda k: (0, 0, k)),
        out_shape=jax.ShapeDtypeStruct((N_DEV, IN_ROWS_PAD, D_MODEL), BF16),
        scratch_shapes=[pltpu.VMEM((D_IN, tc), F32)],
        compiler_params=pltpu.CompilerParams(dimension_semantics=("parallel",)),
    )(g_lat, g_dqkv, g_ga, g_gb)


def _ffn_pad(a, axis):
    a = jnp.moveaxis(a, axis, -1)
    g = a.reshape(a.shape[:-1] + (2 * N_DEV, FF_GROUP))
    g = jnp.pad(g, [(0, 0)] * (g.ndim - 1) + [(0, FF_GROUP_PAD - FF_GROUP)])
    return jnp.moveaxis(g.reshape(a.shape[:-1] + (2 * D_FF_PAD,)), -1, axis)


def _ffn_unpad(a, axis):
    a = jnp.moveaxis(a, axis, -1)
    g = a.reshape(a.shape[:-1] + (2 * N_DEV, FF_GROUP_PAD))[..., :FF_GROUP]
    return jnp.moveaxis(g.reshape(a.shape[:-1] + (2 * D_FF,)), -1, axis)


MISC = (("w_o_mla", (256, 1024)), ("w_o_dil", (256, 512)), ("w_uq", (192, 512)), ("w_ukv", (256, 256)))


def _exchange_blocks(w):
    def t(a):
        return a.astype(BF16).T

    up = t(w["w_up"]).reshape(2, FF_GROUP, D_MODEL)
    return [
        jnp.pad(t(w["w_in"]), ((0, IN_ROWS_PAD - IN_ROWS), (0, 0))),
        jnp.pad(up, ((0, 0), (0, FF_GROUP_PAD - FF_GROUP), (0, 0))).reshape(2 * FF_GROUP_PAD, D_MODEL),
        jnp.pad(w["w_down"].astype(BF16), ((0, FF_GROUP_PAD - FF_GROUP), (0, 0))),
        w["w_out"].astype(BF16),
        jnp.concatenate([t(w[n]).reshape(-1, D_MODEL) for n, _ in MISC], axis=0),
    ]


def _misc_split(misc):
    out, off = {}, 0
    for n, (r, c) in MISC:
        rows = r * c // D_MODEL
        out[n] = misc[..., off : off + rows, :].reshape(misc.shape[:-2] + (r, c))
        off += rows
    return out


def _unpack_weights(gathered):
    g_in, g_misc = gathered
    lat_t, dqkv_t, g_t = _w_in_regroup(g_in, "w_in_regroup")
    misc = _misc_split(g_misc)
    uq_t = jnp.pad(misc["w_uq"], ((0, 0), (0, HEAD_PAD - QK_NOPE - QK_ROPE), (0, 0)))
    return {
        "lat_t": lat_t,
        "dqkv_t": dqkv_t,
        "g_t": g_t,
        "uq_t": uq_t.reshape(MLA_HEADS * HEAD_PAD, Q_LORA),
        "ukv_t": misc["w_ukv"].reshape(MLA_HEADS * HEAD_PAD, KV_LORA),
        "o_mla_t": misc["w_o_mla"].reshape(D_MODEL, MLA_HEADS * V_HEAD),
        "o_dil_t": misc["w_o_dil"].reshape(D_MODEL, DIL_OUT),
    }


def _grad_blocks(g):
    uq_t = g["uq_t"].reshape(MLA_HEADS, HEAD_PAD, Q_LORA)[:, : QK_NOPE + QK_ROPE]
    misc = {"w_o_mla": g["o_mla_t"], "w_o_dil": g["o_dil_t"], "w_uq": uq_t, "w_ukv": g["ukv_t"]}
    return [
        _w_in_grad_regroup(g["lat_t"], g["dqkv_t"], g["ga_t"], g["gb_t"], "w_in_grad_regroup"),
        g["w_out"].reshape(N_DEV, -1, D_MODEL),
        jnp.concatenate([misc[n].reshape(N_DEV, -1, D_MODEL) for n, _ in MISC], axis=1),
    ]


def _grad_shards(sums):
    s_in, s_out, s_misc, s_up, s_down = sums
    out = {
        "w_in": s_in[:IN_ROWS].T,
        "w_up": s_up.reshape(2, FF_GROUP_PAD, D_MODEL)[:, :FF_GROUP].reshape(2 * FF_GROUP, D_MODEL).T,
        "w_down": s_down[:FF_GROUP],
        "w_out": s_out,
    }
    out.update({n: v.T for n, v in _misc_split(s_misc).items()})
    return out


def _local_step(x, tgt, wt, conv_w, small, ffn_weights, send_ffn_grads, start_token):
    S = x.shape[0]
    lat_t, dqkv_t, g_t, uq_t, ukv_t = wt["lat_t"], wt["dqkv_t"], wt["g_t"], wt["uq_t"], wt["ukv_t"]
    o_mla_t, o_dil_t = wt["o_mla_t"], wt["o_dil_t"]
    cw = jnp.pad(_ffn_pad(conv_w, 1), ((0, 5), (0, 0)))
    cb = _ffn_pad(small["conv_b"], 1)
    cos_t, sin_t = _rope_tables(S)
    bias = _dil_bias()
    g1, g2, g3 = small["attn_norm_g"], small["ffn_norm_g"], small["final_norm_g"]
    gq, gkv = small["q_norm_g"], small["kv_norm_g"]

    h = _rms_fwd(x, g1 + start_token, "rms_attn")
    lat = _mm(h, lat_t, "nt", F32, 1024, LAT_PAD, D_MODEL, "proj_lat")
    dqkv = _mm(h, dqkv_t, "nt", BF16, 1024, 512, D_MODEL, "proj_dqkv")
    gates = _mm(h, g_t, "nt", F32, 1024, 512, D_MODEL, "proj_gates", bias=small["b_gate"], act="sigmoid")
    cqn, ckvn, kpe = _mla_prep1(lat, gq, gkv, cos_t, sin_t, "mla_prep1")
    q_raw = _mm(cqn, uq_t, "nt", F32, 1024, 1024, Q_LORA, "mla_uq")
    kv = _mm(ckvn, ukv_t, "nt", BF16, 1024, 1024, KV_LORA, "mla_ukv")
    q_att, k_att = _mla_prep2(q_raw, kv, kpe, cos_t, sin_t, "mla_prep2")
    o, lse = _flash2_fwd(q_att, k_att, kv, "mla_flash_fwd")
    o_a = _mm(o, o_mla_t, "nt", F32, 1024, 1024, MLA_HEADS * V_HEAD, "mla_out")

    d_os, d_ls = [], []
    for g, (_, dil) in enumerate(DIL_PATTERNS):
        og, lg = _dil_fwd_group(dqkv.reshape(S // dil, dil * 3 * DIL_QKV), bias[g], g, dil, f"dil_fwd_{g}")
        d_os.append(og.reshape(S, DIL_OUT))
        d_ls.append(lg.reshape(S, DIL_OUT))
    od, dil_lse = _dil_combine(d_os, d_ls, "dil_combine")
    o_b = _mm(od, o_dil_t, "nt", F32, 1024, 1024, DIL_OUT, "dil_out")

    mrg = _merge_fwd(gates, o_a, o_b, "merge_fwd")
    w_out, up_t, w_down = ffn_weights(mrg)
    x1 = _mm(mrg, w_out, "nn", F32, 1024, 1024, D_MODEL, "mix_out", res=x)
    h2 = _rms_fwd(x1, g2, "rms_ffn")
    u0 = _mm(h2, up_t, "nt", F32, 1024, 512, D_MODEL, "ffn_up")
    a = _ffn_fwd(u0, cw, cb, "ffn_conv_fwd")
    x2 = _mm(a, w_down, "nn", F32, 1024, 512, D_FF_PAD // 2, "ffn_down", res=x1)
    loss_part, dx2, dx2b, dg3 = _final_loss(x2, g3, tgt, "final_loss")

    da = _mm(dx2b, w_down, "nt", F32, 1024, 512, D_MODEL, "ffn_down_dx")
    gw_down = _mm(a, dx2b, "tn", BF16, 512, 1024, 1024, "ffn_down_dw")
    du0, dcw, dcb = _ffn_bwd(u0, da, cw, cb, "ffn_conv_bwd")
    du0 = du0.reshape(2 * S, D_FF_PAD)
    gw_up_t = _mm(du0, h2, "tn", BF16, 512, 1024, 1024, "ffn_up_dw", a_halves=2)
    sent = send_ffn_grads(gw_up_t, gw_down)
    dh2 = _mm(du0, up_t, "nn", F32, 1024, 1024, 1408, "ffn_up_dx", a_halves=2)
    dx1, dx1b, dg2 = _rms_bwd(dh2, x1, g2 + sent, dx2, "rms_ffn_bwd")

    dmrg = _mm(dx1b, w_out, "nt", F32, 1024, 1024, D_MODEL, "mix_out_dx")
    gw_out = _mm(mrg, dx1b, "tn", BF16, 1024, 1024, 1024, "mix_out_dw")
    do_a, do_b, dga, dgb, dba, dbb = _merge_bwd(dmrg, gates, o_a, o_b, "merge_bwd")

    do = _mm(do_a, o_mla_t, "nn", BF16, 1024, 1024, D_MODEL, "mla_out_dx")
    gw_o_mla_t = _mm(do_a, o, "tn", BF16, 1024, 1024, 1024, "mla_out_dw")
    dod = _mm(do_b, o_dil_t, "nn", F32, 1024, DIL_OUT, D_MODEL, "dil_out_dx")
    gw_o_dil_t = _mm(do_b, od, "tn", BF16, 1024, DIL_OUT, 1024, "dil_out_dw")

    dq_att, delta = _flash2_dq(q_att, k_att, kv, o, do, lse, "mla_flash_dq")
    lse_row = lse[:, :, 0][:, None, :]
    delta_row = delta[:, :, 0][:, None, :]
    dk_att, dv = _flash2_dkv(q_att, k_att, kv, do, lse_row, delta_row, "mla_flash_dkv")
    dq_raw, dkv, dkpe = _mla_post(dq_att, dk_att, dv, cos_t, sin_t, "mla_post")
    dcqn = _mm(dq_raw, uq_t, "nn", F32, 1024, Q_LORA, MLA_HEADS * HEAD_PAD, "mla_uq_dx")
    gw_uq_t = _mm(dq_raw, cqn, "tn", BF16, 1024, Q_LORA, 1024, "mla_uq_dw")
    dckvn = _mm(dkv, ukv_t, "nn", F32, 1024, KV_LORA, MLA_HEADS * HEAD_PAD, "mla_ukv_dx")
    gw_ukv_t = _mm(dkv, ckvn, "tn", BF16, 1024, KV_LORA, 1024, "mla_ukv_dw")
    dlat, dgq, dgkv = _lat_bwd(dcqn, dckvn, dkpe, lat, gq, gkv, "lat_bwd")

    dd, dodb = _dil_rowdot(dod, od, "dil_rowdot")
    dparts = [[None] * DIL_GROUPS for _ in range(3)]
    for g, (_, dil) in enumerate(DIL_PATTERNS):
        L = S // dil
        outs = _dil_bwd_group(
            dqkv.reshape(L, dil * 3 * DIL_QKV),
            bias[g],
            dodb.reshape(L, dil * DIL_OUT),
            dd.reshape(L, dil * DIL_OUT),
            dil_lse.reshape(L, dil * DIL_OUT),
            g,
            dil,
            f"dil_bwd_{g}",
        )
        for t in range(3):
            dparts[t][g] = outs[t].reshape(S, DIL_OUT)
    ddqkv = jnp.concatenate([p for row in dparts for p in row], axis=1)

    dh = _mm(dlat, lat_t, "nn", F32, 1024, 1024, LAT_PAD, "proj_lat_dx")
    dh = _mm(ddqkv, dqkv_t, "nn", F32, 1024, 1024, 1536, "proj_dqkv_dx", res=dh)
    dh = _mm(dga, g_t, "nn", F32, 1024, 1024, D_MODEL, "proj_ga_dx", res=dh)
    dh = _mm(dgb, g_t, "nn", F32, 1024, 1024, D_MODEL, "proj_gb_dx", res=dh, b_koff=1)
    gw_lat_t = _mm(dlat, h, "tn", BF16, LAT_PAD, 1024, 1024, "proj_lat_dw")
    gw_dqkv_t = _mm(ddqkv, h, "tn", BF16, 512, 1024, 1024, "proj_dqkv_dw")
    gw_ga_t = _mm(dga, h, "tn", BF16, 1024, 1024, 1024, "proj_ga_dw")
    gw_gb_t = _mm(dgb, h, "tn", BF16, 1024, 1024, 1024, "proj_gb_dw")
    grad_x, _, dg1 = _rms_bwd(dh, x, g1, dx1, "rms_attn_bwd")

    grads = {
        "lat_t": gw_lat_t,
        "dqkv_t": gw_dqkv_t,
        "ga_t": gw_ga_t,
        "gb_t": gw_gb_t,
        "uq_t": gw_uq_t,
        "ukv_t": gw_ukv_t,
        "o_mla_t": gw_o_mla_t,
        "o_dil_t": gw_o_dil_t,
        "w_out": gw_out,
    }
    small_grads = {
        "attn_norm_g": dg1,
        "b_gate": jnp.concatenate([dba, dbb], axis=1),
        "q_norm_g": dgq,
        "kv_norm_g": dgkv,
        "ffn_norm_g": dg2,
        "conv_b": _ffn_unpad(jnp.concatenate([dcb[0], dcb[1]], axis=1), 1),
        "final_norm_g": dg3,
        "conv_w": _ffn_unpad(jnp.concatenate([dcw[0, :3], dcw[1, :3]], axis=1), 1),
    }
    return loss_part, grad_x, grads, small_grads


SMALL_ORDER = ("attn_norm_g", "b_gate", "q_norm_g", "kv_norm_g", "ffn_norm_g", "conv_b", "final_norm_g", "conv_w")
WEIGHT_ORDER = (
    "attn_norm_g", "w_in", "b_gate", "q_norm_g", "w_uq", "kv_norm_g", "w_ukv", "w_o_mla", "w_o_dil", "w_out",
    "ffn_norm_g", "w_up", "conv_w", "conv_b", "w_down", "final_norm_g",
)


def kernel(x, attn_norm_g, w_in, b_gate, q_norm_g, w_uq, kv_norm_g, w_ukv, w_o_mla, w_o_dil, w_out, ffn_norm_g, w_up, conv_w, conv_b, w_down, final_norm_g, loss_target, m_attn_norm_g, m_w_in, m_b_gate, m_q_norm_g, m_w_uq, m_kv_norm_g, m_w_ukv, m_w_o_mla, m_w_o_dil, m_w_out, m_ffn_norm_g, m_w_up, m_conv_w, m_conv_b, m_w_down, m_final_norm_g, v_attn_norm_g, v_w_in, v_b_gate, v_q_norm_g, v_w_uq, v_kv_norm_g, v_w_ukv, v_w_o_mla, v_w_o_dil, v_w_out, v_ffn_norm_g, v_w_up, v_conv_w, v_conv_b, v_w_down, v_final_norm_g):
    env = dict(locals())
    dev = 4 * lax.axis_index("x") + 2 * lax.axis_index("y") + lax.axis_index("c")
    core = lax.axis_index("c").astype(jnp.int32).reshape(1)

    def two_d(a):
        return a.reshape(-1, a.shape[-1])

    w = {n: two_d(env[n]) for n in WEIGHT_ORDER}
    m = {n: two_d(env["m_" + n]) for n in WEIGHT_ORDER}
    v = {n: two_d(env["v_" + n]) for n in WEIGHT_ORDER}

    def own_slot_in(lands, own):
        return [lax.dynamic_update_slice(l, o[None], (dev, 0, 0)) for l, o in zip(lands, own)]

    b_in, b_up, b_down, b_out, b_misc = _exchange_blocks(w)
    r, c = CONV_SHARD
    conv = jnp.pad(w["conv_w"].reshape(-1), (0, 8 * SMALL_COLS - r * c)).reshape(8, SMALL_COLS)
    g_in, g_misc, conv = _all_gather([b_in, b_misc, conv], "ag_attn_weights")
    ffn_gather, started = _exchange_start([b_up, b_down, b_out], False, conv, "ag_ffn_start")
    full = _unpack_weights([g_in, g_misc])
    conv = conv.reshape(N_DEV, 8 * SMALL_COLS)[:, : r * c].reshape(N_DEV, r, c)
    conv_w_full = conv.transpose(1, 0, 2).reshape(r, N_DEV * c)
    small = {n: w[n] for n in SMALL_ORDER if n != "conv_w"}

    def ffn_weights(after):
        own, lands = _exchange_wait(ffn_gather, False, after, "ag_ffn_wait")
        g_up, g_down, g_out = own_slot_in(lands, own)
        return g_out.reshape(D_MODEL, D_MODEL), g_up.reshape(2 * D_FF_PAD, D_MODEL), g_down.reshape(D_FF_PAD, D_MODEL)

    ffn_reduce = []

    def send_ffn_grads(gw_up_t, gw_down):
        blocks = [gw_up_t.reshape(N_DEV, 2 * FF_GROUP_PAD, D_MODEL), gw_down.reshape(N_DEV, FF_GROUP_PAD, D_MODEL)]
        handle, token = _exchange_start(blocks, True, gw_down, "rs_ffn_start")
        ffn_reduce.append(handle)
        return token[0, 0]

    loss_part, grad_x, grads, small_grads = _local_step(
        x[0], loss_target[0], full, conv_w_full, small, ffn_weights, send_ffn_grads, started[0, 0]
    )
    loss = lax.psum(loss_part[0, 0], AXES)

    gblocks = _grad_blocks(grads)
    recv = _pair_exchange(gblocks, "rs_pair_exchange")
    pairs = [_pair_add(g, r_, core, f"rs_pair_add_{i}") for i, (g, r_) in enumerate(zip(gblocks, recv))]
    parts = _chip_exchange(pairs, "rs_chip_exchange")
    sums = [_slot_sum(p, f"rs_sum_{i}") for i, p in enumerate(parts)]
    sent, lands = _exchange_wait(ffn_reduce[0], True, sums[0], "rs_ffn_wait")
    own = [lax.dynamic_index_in_dim(s, dev, 0, keepdims=False) for s in sent]
    sums += [_slot_sum(p, f"rs_ffn_sum_{i}") for i, p in enumerate(own_slot_in(lands, own))]
    gshard = _grad_shards(sums)

    sflat = jnp.concatenate([small_grads[n].reshape(-1) for n in SMALL_ORDER])
    sflat = jnp.pad(sflat, (0, SMALL_ROWS * SMALL_COLS - sflat.shape[0])).reshape(SMALL_ROWS, SMALL_COLS)
    ssum = _slot_sum(_all_gather([sflat], "ag_small_grads")[0], "small_sum").reshape(-1)
    gsmall, off = {}, 0
    for n in SMALL_ORDER:
        shape = (3, 2 * D_FF) if n == "conv_w" else w[n].shape
        size = shape[0] * shape[1]
        gsmall[n] = ssum[off : off + size].reshape(shape)
        off += size
    gsmall["conv_w"] = lax.dynamic_slice_in_dim(gsmall["conv_w"], dev * CONV_SHARD[1], CONV_SHARD[1], axis=1)

    g_all = {**gshard, **gsmall}
    out_g, out_d, out_m, out_v = [], [], [], []
    for n in WEIGHT_ORDER:
        d, nm, nv = _adamw(w[n], g_all[n], m[n], v[n], "adamw_" + n)
        shape = env[n].shape
        out_g.append(g_all[n].reshape(shape))
        out_d.append(d.reshape(shape))
        out_m.append(nm.reshape(shape))
        out_v.append(nv.reshape(shape))
    return (loss, grad_x[None], *out_g, *out_d, *out_m, *out_v)
```

```python
import functools

import jax
import jax.numpy as jnp
import numpy as np
from jax import lax
from jax.experimental import pallas as pl
from jax.experimental.pallas import tpu as pltpu

F32 = jnp.float32
BF16 = jnp.bfloat16

N_DEV = 8
N_CHIP = 4
AXES = ("x", "y", "c")
MESH = pl.DeviceIdType.MESH

D_MODEL = 2048
MLA_HEADS = 8
QK_NOPE = 128
QK_ROPE = 64
V_HEAD = 128
Q_LORA = 512
KV_LORA = 256
ROPE_THETA = 10000.0
HEAD_PAD = 256
DIL_PATTERNS = ((128, 1), (512, 4), (2048, 16))
DIL_GROUPS = 3
DIL_HG = 4
DIL_HEADS = 12
DIL_HD = 128
DIL_BLK = 128
DIL_QKV = DIL_HEADS * DIL_HD
DIL_OUT = DIL_HG * DIL_HD
ALIBI_MAX_BIAS = 8.0
D_FF = 5504
D_FF_PAD = 5632
NORM_EPS = 1e-6
LAT = Q_LORA + KV_LORA + QK_ROPE
LAT_PAD = 896
D_IN = LAT + 3 * DIL_QKV + 2 * D_MODEL
NEG = -1e30

ADAM_LR = 0.001
ADAM_B1 = 0.9
ADAM_B2 = 0.999
ADAM_EPS = 1e-08
ADAM_WD = 0.01
ADAM_STEP = 10

SMALL_ROWS = 56
SMALL_COLS = 1024

IN_ROWS = 1192
IN_ROWS_PAD = 1200
FF_GROUP = D_FF // N_DEV
FF_GROUP_PAD = D_FF_PAD // N_DEV
CONV_SHARD = (3, 1376)

NT = (((1,), (1,)), ((), ()))
TN = (((0,), (0,)), ((), ()))


def _dot(a, b, dims=(((1,), (0,)), ((), ()))):
    return lax.dot_general(a, b, dims, preferred_element_type=F32)


def _mm(a, b, mode, out_dtype, tm, tn, tk, name, bias=None, act=None, res=None, b_koff=0, a_halves=1):
    H = a_halves
    if mode == "nn":
        (M, K), (K2, N) = (a.shape[0] // H, a.shape[1] * H), b.shape
        assert (b_koff + 1) * K <= K2, (name, a.shape, b.shape)
        koff, K2 = b_koff * (K // tk), K
        kper, mrows = a.shape[1] // tk, M // tm
        a_spec = pl.BlockSpec((tm, tk), lambda i, j, k: (i + (k // kper) * mrows, k % kper))
        b_spec = pl.BlockSpec((tk, tn), lambda i, j, k: (k + koff, j))
        dims = (((1,), (0,)), ((), ()))
    elif mode == "nt":
        (M, K), (N, K2) = a.shape, b.shape
        a_spec = pl.BlockSpec((tm, tk), lambda i, j, k: (i, k))
        b_spec = pl.BlockSpec((tn, tk), lambda i, j, k: (j, k))
        dims = NT
    else:
        (K, M), (K2, N) = (a.shape[0] // H, a.shape[1] * H), b.shape
        mper, krows = a.shape[1] // tm, K // tk
        a_spec = pl.BlockSpec((tk, tm), lambda i, j, k: (k + (i // mper) * krows, i % mper))
        b_spec = pl.BlockSpec((tk, tn), lambda i, j, k: (k, j))
        dims = TN
    assert K == K2 and M % tm == 0 and N % tn == 0 and K % tk == 0, (name, a.shape, b.shape)
    nk = K // tk
    has_bias, has_res = bias is not None, res is not None

    def body(*refs):
        refs = list(refs)
        a_ref, b_ref = refs[0], refs[1]
        pos = 2
        bias_ref = res_ref = None
        if has_bias:
            bias_ref = refs[pos]
            pos += 1
        if has_res:
            res_ref = refs[pos]
            pos += 1
        o_ref = refs[pos]
        p = _dot(a_ref[...].astype(BF16), b_ref[...].astype(BF16), dims)

        def finish(acc):
            if has_bias:
                acc = acc + bias_ref[...]
            if act == "sigmoid":
                acc = jax.nn.sigmoid(acc)
            if has_res:
                acc = res_ref[...] + acc
            o_ref[...] = acc.astype(o_ref.dtype)

        if nk == 1:
            finish(p)
        else:
            acc_ref = refs[pos + 1]
            k = pl.program_id(2)

            @pl.when(k == 0)
            def _():
                acc_ref[...] = p

            @pl.when(k != 0)
            def _():
                acc_ref[...] += p

            @pl.when(k == nk - 1)
            def _():
                finish(acc_ref[...])

    in_specs = [a_spec, b_spec]
    args = [a, b]
    if has_bias:
        in_specs.append(pl.BlockSpec((1, tn), lambda i, j, k: (0, j)))
        args.append(bias)
    if has_res:
        in_specs.append(pl.BlockSpec((tm, tn), lambda i, j, k: (i, j)))
        args.append(res)
    return pl.pallas_call(
        body,
        name=name,
        grid=(M // tm, N // tn, nk),
        in_specs=in_specs,
        out_specs=pl.BlockSpec((tm, tn), lambda i, j, k: (i, j)),
        out_shape=jax.ShapeDtypeStruct((M, N), out_dtype),
        scratch_shapes=[pltpu.VMEM((tm, tn), F32)] if nk > 1 else [],
        compiler_params=pltpu.CompilerParams(dimension_semantics=("parallel", "parallel", "arbitrary")),
    )(*args)


def _rstd(x):
    return lax.rsqrt(jnp.mean(x * x, axis=-1, keepdims=True) + NORM_EPS)


def _rms_bwd_math(dy, x, g):
    r = _rstd(x)
    xh = x * r
    dg = jnp.sum(dy * xh, axis=0, keepdims=True)
    dxh = dy * g
    dx = r * (dxh - xh * jnp.mean(dxh * xh, axis=-1, keepdims=True))
    return dx, dg


def _rms_fwd(x, g, name, tr=256):
    S, D = x.shape

    def body(x_ref, g_ref, o_ref):
        xv = x_ref[...]
        o_ref[...] = ((xv * _rstd(xv)) * g_ref[...]).astype(o_ref.dtype)

    return pl.pallas_call(
        body,
        name=name,
        grid=(S // tr,),
        in_specs=[pl.BlockSpec((tr, D), lambda i: (i, 0)), pl.BlockSpec((1, D), lambda i: (0, 0))],
        out_specs=pl.BlockSpec((tr, D), lambda i: (i, 0)),
        out_shape=jax.ShapeDtypeStruct((S, D), BF16),
        compiler_params=pltpu.CompilerParams(dimension_semantics=("parallel",)),
    )(x, g)


def _rms_bwd(dy, x, g, res, name, tr=256):
    S, D = x.shape

    def body(dy_ref, x_ref, g_ref, res_ref, dx_ref, dxb_ref, dg_ref):
        dx, dg = _rms_bwd_math(dy_ref[...], x_ref[...], g_ref[...])
        dx = dx + res_ref[...]
        dx_ref[...] = dx
        dxb_ref[...] = dx.astype(BF16)

        @pl.when(pl.program_id(0) == 0)
        def _():
            dg_ref[...] = dg

        @pl.when(pl.program_id(0) != 0)
        def _():
            dg_ref[...] += dg

    row = pl.BlockSpec((tr, D), lambda i: (i, 0))
    vec = pl.BlockSpec((1, D), lambda i: (0, 0))
    return pl.pallas_call(
        body,
        name=name,
        grid=(S // tr,),
        in_specs=[row, row, vec, row],
        out_specs=[row, row, vec],
        out_shape=[jax.ShapeDtypeStruct((S, D), F32), jax.ShapeDtypeStruct((S, D), BF16), jax.ShapeDtypeStruct((1, D), F32)],
        compiler_params=pltpu.CompilerParams(dimension_semantics=("arbitrary",)),
    )(dy, x, g, res)


def _final_loss(x2, g, tgt, name, tr=256):
    S, D = x2.shape

    def body(x_ref, g_ref, t_ref, loss_ref, dx_ref, dxb_ref, dg_ref):
        xv, gv = x_ref[...], g_ref[...]
        y = (xv * _rstd(xv)) * gv
        e = y - t_ref[...]
        part = 0.5 * jnp.sum(jnp.mean(e * e, axis=-1, keepdims=True), axis=0, keepdims=True)
        dx, dg = _rms_bwd_math(e * (1.0 / D), xv, gv)
        dx_ref[...] = dx
        dxb_ref[...] = dx.astype(BF16)
        part = jnp.broadcast_to(part, (1, 128))

        @pl.when(pl.program_id(0) == 0)
        def _():
            dg_ref[...] = dg
            loss_ref[...] = part

        @pl.when(pl.program_id(0) != 0)
        def _():
            dg_ref[...] += dg
            loss_ref[...] += part

    row = pl.BlockSpec((tr, D), lambda i: (i, 0))
    vec = pl.BlockSpec((1, D), lambda i: (0, 0))
    return pl.pallas_call(
        body,
        name=name,
        grid=(S // tr,),
        in_specs=[row, vec, row],
        out_specs=[pl.BlockSpec((1, 128), lambda i: (0, 0)), row, row, vec],
        out_shape=[
            jax.ShapeDtypeStruct((1, 128), F32),
            jax.ShapeDtypeStruct((S, D), F32),
            jax.ShapeDtypeStruct((S, D), BF16),
            jax.ShapeDtypeStruct((1, D), F32),
        ],
        compiler_params=pltpu.CompilerParams(dimension_semantics=("arbitrary",)),
    )(x2, g, tgt)


def _rope_tables(S):
    pos = jnp.arange(S, dtype=F32)
    inv_freq = ROPE_THETA ** (-jnp.arange(0, QK_ROPE, 2, dtype=F32) / QK_ROPE)
    ang = pos[:, None] * inv_freq[None, :]
    cos, sin = jnp.cos(ang), jnp.sin(ang)
    zero = jnp.zeros((S, 128 - QK_ROPE), F32)
    return jnp.concatenate([cos, cos, zero], axis=1), jnp.concatenate([-sin, sin, zero], axis=1)


def _rope_tile(x, cos_t, sin_t):
    lane = lax.broadcasted_iota(jnp.int32, x.shape, 1)
    partner = jnp.where(lane < QK_ROPE // 2, pltpu.roll(x, 128 - QK_ROPE // 2, 1), pltpu.roll(x, QK_ROPE // 2, 1))
    return x * cos_t + partner * sin_t


def _mla_prep1(lat, gq, gkv, cos_t, sin_t, name, tr=256):
    S = lat.shape[0]

    def body(lat_ref, gq_ref, gkv_ref, cos_ref, sin_ref, cq_ref, ckv_ref, kpe_ref):
        cq = lat_ref[:, :Q_LORA]
        ckv = lat_ref[:, Q_LORA : Q_LORA + KV_LORA]
        cq_ref[...] = ((cq * _rstd(cq)) * gq_ref[...]).astype(BF16)
        ckv_ref[...] = ((ckv * _rstd(ckv)) * gkv_ref[...]).astype(BF16)
        kpe_ref[...] = _rope_tile(lat_ref[:, Q_LORA + KV_LORA :], cos_ref[...], sin_ref[...]).astype(BF16)

    def row(n):
        return pl.BlockSpec((tr, n), lambda i: (i, 0))

    def vec(n):
        return pl.BlockSpec((1, n), lambda i: (0, 0))

    return pl.pallas_call(
        body,
        name=name,
        grid=(S // tr,),
        in_specs=[row(LAT_PAD), vec(Q_LORA), vec(KV_LORA), row(128), row(128)],
        out_specs=[row(Q_LORA), row(KV_LORA), row(128)],
        out_shape=[
            jax.ShapeDtypeStruct((S, Q_LORA), BF16),
            jax.ShapeDtypeStruct((S, KV_LORA), BF16),
            jax.ShapeDtypeStruct((S, 128), BF16),
        ],
        compiler_params=pltpu.CompilerParams(dimension_semantics=("parallel",)),
    )(lat, gq, gkv, cos_t, sin_t)


def _mla_prep2(q_raw, kv, kpe, cos_t, sin_t, name, tr=256):
    S = q_raw.shape[0]
    W = MLA_HEADS * HEAD_PAD

    def body(q_ref, kv_ref, kpe_ref, cos_ref, sin_ref, qa_ref, ka_ref):
        cos_v, sin_v, kpe_v = cos_ref[...], sin_ref[...], kpe_ref[...]
        for h in range(MLA_HEADS):
            lo = h * HEAD_PAD
            qa_ref[:, lo : lo + 128] = q_ref[:, lo : lo + 128].astype(BF16)
            qa_ref[:, lo + 128 : lo + 256] = _rope_tile(q_ref[:, lo + 128 : lo + 256], cos_v, sin_v).astype(BF16)
            ka_ref[:, lo : lo + 128] = kv_ref[:, lo : lo + 128]
            ka_ref[:, lo + 128 : lo + 256] = kpe_v

    def row(n):
        return pl.BlockSpec((tr, n), lambda i: (i, 0))

    return pl.pallas_call(
        body,
        name=name,
        grid=(S // tr,),
        in_specs=[row(W), row(W), row(128), row(128), row(128)],
        out_specs=[row(W), row(W)],
        out_shape=[jax.ShapeDtypeStruct((S, W), BF16), jax.ShapeDtypeStruct((S, W), BF16)],
        compiler_params=pltpu.CompilerParams(dimension_semantics=("parallel",)),
    )(q_raw, kv, kpe, cos_t, sin_t)


def _mla_post(dq_att, dk_att, dv, cos_t, sin_t, name, tr=256):
    S = dq_att.shape[0]
    W = MLA_HEADS * HEAD_PAD

    def body(dq_ref, dk_ref, dv_ref, cos_ref, sin_ref, dqr_ref, dkv_ref, dkpe_ref):
        cos_v, nsin_v = cos_ref[...], -sin_ref[...]
        kpe = jnp.zeros((tr, 128), F32)
        for h in range(MLA_HEADS):
            lo = h * HEAD_PAD
            dqr_ref[:, lo : lo + 128] = dq_ref[:, lo : lo + 128].astype(BF16)
            dqr_ref[:, lo + 128 : lo + 256] = _rope_tile(dq_ref[:, lo + 128 : lo + 256], cos_v, nsin_v).astype(BF16)
            dkv_ref[:, lo : lo + 128] = dk_ref[:, lo : lo + 128].astype(BF16)
            dkv_ref[:, lo + 128 : lo + 256] = dv_ref[:, h * 128 : (h + 1) * 128].astype(BF16)
            kpe = kpe + dk_ref[:, lo + 128 : lo + 256]
        dkpe_ref[...] = _rope_tile(kpe, cos_v, nsin_v)

    def row(n):
        return pl.BlockSpec((tr, n), lambda i: (i, 0))

    return pl.pallas_call(
        body,
        name=name,
        grid=(S // tr,),
        in_specs=[row(W), row(W), row(MLA_HEADS * V_HEAD), row(128), row(128)],
        out_specs=[row(W), row(W), row(128)],
        out_shape=[jax.ShapeDtypeStruct((S, W), BF16), jax.ShapeDtypeStruct((S, W), BF16), jax.ShapeDtypeStruct((S, 128), F32)],
        compiler_params=pltpu.CompilerParams(dimension_semantics=("parallel",)),
    )(dq_att, dk_att, dv, cos_t, sin_t)


def _lat_bwd(dcqn, dckvn, dkpe, lat, gq, gkv, name, tr=256):
    S = lat.shape[0]

    def body(dcq_ref, dckv_ref, dkpe_ref, lat_ref, gq_ref, gkv_ref, dlat_ref, dgq_ref, dgkv_ref):
        dq, dgq = _rms_bwd_math(dcq_ref[...], lat_ref[:, :Q_LORA], gq_ref[...])
        dkv, dgkv = _rms_bwd_math(dckv_ref[...], lat_ref[:, Q_LORA : Q_LORA + KV_LORA], gkv_ref[...])
        dlat_ref[:, :Q_LORA] = dq.astype(BF16)
        dlat_ref[:, Q_LORA : Q_LORA + KV_LORA] = dkv.astype(BF16)
        dlat_ref[:, Q_LORA + KV_LORA :] = dkpe_ref[...].astype(BF16)

        @pl.when(pl.program_id(0) == 0)
        def _():
            dgq_ref[...] = dgq
            dgkv_ref[...] = dgkv

        @pl.when(pl.program_id(0) != 0)
        def _():
            dgq_ref[...] += dgq
            dgkv_ref[...] += dgkv

    def row(n):
        return pl.BlockSpec((tr, n), lambda i: (i, 0))

    def vec(n):
        return pl.BlockSpec((1, n), lambda i: (0, 0))

    return pl.pallas_call(
        body,
        name=name,
        grid=(S // tr,),
        in_specs=[row(Q_LORA), row(KV_LORA), row(128), row(LAT_PAD), vec(Q_LORA), vec(KV_LORA)],
        out_specs=[row(LAT_PAD), vec(Q_LORA), vec(KV_LORA)],
        out_shape=[
            jax.ShapeDtypeStruct((S, LAT_PAD), BF16),
            jax.ShapeDtypeStruct((1, Q_LORA), F32),
            jax.ShapeDtypeStruct((1, KV_LORA), F32),
        ],
        compiler_params=pltpu.CompilerParams(dimension_semantics=("arbitrary",)),
    )(dcqn, dckvn, dkpe, lat, gq, gkv)


MLA_SCALE = (QK_NOPE + QK_ROPE) ** -0.5
LOG2E = 1.4426950408889634
MLA_C2 = MLA_SCALE * LOG2E
FLASH_T = 1024


def _causal_pairs(n, by_key):
    pairs = [(i, j) for j in range(n) for i in range(j, n)] if by_key else [(i, j) for i in range(n) for j in range(i + 1)]
    return jnp.asarray([p[0] for p in pairs], jnp.int32), jnp.asarray([p[1] for p in pairs], jnp.int32)


def _lanes(x, n):
    return jnp.tile(x, (1, n // 128))


def _flash_grid(npairs, in_specs, out_specs, scratch):
    return pltpu.PrefetchScalarGridSpec(
        num_scalar_prefetch=2, grid=(MLA_HEADS, npairs), in_specs=in_specs, out_specs=out_specs, scratch_shapes=scratch
    )


def _flash2_fwd(q_att, k_att, kv, name, t=FLASH_T):
    S = q_att.shape[0]
    qi_tab, kj_tab = _causal_pairs(S // t, by_key=False)

    def body(qi_ref, kj_ref, q_ref, k_ref, v_ref, o_ref, lse_ref, m_sc, l_sc, acc_sc):
        step = pl.program_id(1)
        qi, kj = qi_ref[step], kj_ref[step]

        @pl.when(kj == 0)
        def _():
            m_sc[...] = jnp.full((t, 128), NEG, F32)
            l_sc[...] = jnp.zeros((t, 128), F32)
            acc_sc[...] = jnp.zeros((t, V_HEAD), F32)

        def update(s):
            m_prev = m_sc[...]
            m_new = jnp.maximum(m_prev, jnp.max(s, axis=1, keepdims=True))
            p = jnp.exp2((s - _lanes(m_new, t)) * MLA_C2)
            alpha = jnp.exp2((m_prev - m_new) * MLA_C2)
            l_sc[...] = alpha * l_sc[...] + jnp.sum(p, axis=1, keepdims=True)
            acc_sc[...] = alpha * acc_sc[...] + _dot(p.astype(BF16), v_ref[...])
            m_sc[...] = m_new

        @pl.when(kj < qi)
        def _():
            update(_dot(q_ref[...], k_ref[...], NT))

        @pl.when(kj == qi)
        def _():
            s = _dot(q_ref[...], k_ref[...], NT)
            rows = lax.broadcasted_iota(jnp.int32, s.shape, 0)
            cols = lax.broadcasted_iota(jnp.int32, s.shape, 1)
            update(jnp.where(cols <= rows, s, NEG))
            l = l_sc[...]
            o_ref[...] = acc_sc[...] / l
            lse_ref[0] = m_sc[...] * MLA_SCALE + jnp.log(l)

    return pl.pallas_call(
        body,
        name=name,
        grid_spec=_flash_grid(
            qi_tab.shape[0],
            [
                pl.BlockSpec((t, HEAD_PAD), lambda h, p, qi, kj: (qi[p], h)),
                pl.BlockSpec((t, HEAD_PAD), lambda h, p, qi, kj: (kj[p], h)),
                pl.BlockSpec((t, V_HEAD), lambda h, p, qi, kj: (kj[p], 2 * h + 1)),
            ],
            [
                pl.BlockSpec((t, V_HEAD), lambda h, p, qi, kj: (qi[p], h)),
                pl.BlockSpec((1, t, 128), lambda h, p, qi, kj: (h, qi[p], 0)),
            ],
            [pltpu.VMEM((t, 128), F32), pltpu.VMEM((t, 128), F32), pltpu.VMEM((t, V_HEAD), F32)],
        ),
        out_shape=[jax.ShapeDtypeStruct((S, MLA_HEADS * V_HEAD), F32), jax.ShapeDtypeStruct((MLA_HEADS, S, 128), F32)],
        compiler_params=pltpu.CompilerParams(dimension_semantics=("parallel", "arbitrary")),
    )(qi_tab, kj_tab, q_att, k_att, kv)


def _flash2_dq(q_att, k_att, kv, o, do, lse, name, t=FLASH_T):
    S = q_att.shape[0]
    qi_tab, kj_tab = _causal_pairs(S // t, by_key=False)

    def body(qi_ref, kj_ref, q_ref, k_ref, v_ref, o_ref, do_ref, lse_ref, dq_ref, dl_ref, acc_sc):
        step = pl.program_id(1)
        qi, kj = qi_ref[step], kj_ref[step]

        @pl.when(kj == 0)
        def _():
            acc_sc[...] = jnp.zeros((t, HEAD_PAD), F32)
            dl = jnp.sum(do_ref[...].astype(F32) * o_ref[...], axis=1, keepdims=True)
            dl_ref[0] = jnp.broadcast_to(dl, (t, 128))

        def update(s):
            k = k_ref[...]
            p = jnp.exp2(s * MLA_C2 - _lanes(lse_ref[0] * LOG2E, t))
            dp = _dot(do_ref[...], v_ref[...], NT)
            ds = p * (dp - _lanes(dl_ref[0], t))
            acc_sc[...] += _dot(ds.astype(BF16), k)

        @pl.when(kj < qi)
        def _():
            update(_dot(q_ref[...], k_ref[...], NT))

        @pl.when(kj == qi)
        def _():
            s = _dot(q_ref[...], k_ref[...], NT)
            rows = lax.broadcasted_iota(jnp.int32, s.shape, 0)
            cols = lax.broadcasted_iota(jnp.int32, s.shape, 1)
            update(jnp.where(cols <= rows, s, NEG))
            dq_ref[...] = acc_sc[...] * MLA_SCALE

    qrow = lambda h, p, qi, kj: (qi[p], h)
    stat = pl.BlockSpec((1, t, 128), lambda h, p, qi, kj: (h, qi[p], 0))
    return pl.pallas_call(
        body,
        name=name,
        grid_spec=_flash_grid(
            qi_tab.shape[0],
            [
                pl.BlockSpec((t, HEAD_PAD), qrow),
                pl.BlockSpec((t, HEAD_PAD), lambda h, p, qi, kj: (kj[p], h)),
                pl.BlockSpec((t, V_HEAD), lambda h, p, qi, kj: (kj[p], 2 * h + 1)),
                pl.BlockSpec((t, V_HEAD), qrow),
                pl.BlockSpec((t, V_HEAD), qrow),
                stat,
            ],
            [pl.BlockSpec((t, HEAD_PAD), qrow), stat],
            [pltpu.VMEM((t, HEAD_PAD), F32)],
        ),
        out_shape=[jax.ShapeDtypeStruct((S, MLA_HEADS * HEAD_PAD), F32), jax.ShapeDtypeStruct((MLA_HEADS, S, 128), F32)],
        compiler_params=pltpu.CompilerParams(dimension_semantics=("parallel", "arbitrary")),
    )(qi_tab, kj_tab, q_att, k_att, kv, o, do, lse)


def _flash2_dkv(q_att, k_att, kv, do, lse_row, delta_row, name, t=FLASH_T):
    S = q_att.shape[0]
    n = S // t
    qi_tab, kj_tab = _causal_pairs(n, by_key=True)

    def body(qi_ref, kj_ref, q_ref, k_ref, v_ref, do_ref, lse_ref, dl_ref, dk_ref, dv_ref, dk_sc, dv_sc):
        step = pl.program_id(1)
        qi, kj = qi_ref[step], kj_ref[step]

        def update(st):
            q, do_v = q_ref[...], do_ref[...]
            pt = jnp.exp2(st * MLA_C2 - lse_ref[0] * LOG2E)
            dv_sc[...] += _dot(pt.astype(BF16), do_v)
            dpt = _dot(v_ref[...], do_v, NT)
            dst = pt * (dpt - dl_ref[0])
            dk_sc[...] += _dot(dst.astype(BF16), q)

        @pl.when(qi == kj)
        def _():
            dk_sc[...] = jnp.zeros((t, HEAD_PAD), F32)
            dv_sc[...] = jnp.zeros((t, V_HEAD), F32)
            st = _dot(k_ref[...], q_ref[...], NT)
            keys = lax.broadcasted_iota(jnp.int32, st.shape, 0)
            qs = lax.broadcasted_iota(jnp.int32, st.shape, 1)
            update(jnp.where(keys <= qs, st, NEG))

        @pl.when(qi > kj)
        def _():
            update(_dot(k_ref[...], q_ref[...], NT))

        @pl.when(qi == n - 1)
        def _():
            dk_ref[...] = dk_sc[...] * MLA_SCALE
            dv_ref[...] = dv_sc[...]

    qrow = lambda h, p, qi, kj: (qi[p], h)
    krow = lambda h, p, qi, kj: (kj[p], h)
    stat = pl.BlockSpec((1, 1, t), lambda h, p, qi, kj: (h, 0, qi[p]))
    return pl.pallas_call(
        body,
        name=name,
        grid_spec=_flash_grid(
            qi_tab.shape[0],
            [
                pl.BlockSpec((t, HEAD_PAD), qrow),
                pl.BlockSpec((t, HEAD_PAD), krow),
                pl.BlockSpec((t, V_HEAD), lambda h, p, qi, kj: (kj[p], 2 * h + 1)),
                pl.BlockSpec((t, V_HEAD), qrow),
                stat,
                stat,
            ],
            [pl.BlockSpec((t, HEAD_PAD), krow), pl.BlockSpec((t, V_HEAD), krow)],
            [pltpu.VMEM((t, HEAD_PAD), F32), pltpu.VMEM((t, V_HEAD), F32)],
        ),
        out_shape=[jax.ShapeDtypeStruct((S, MLA_HEADS * HEAD_PAD), F32), jax.ShapeDtypeStruct((S, MLA_HEADS * V_HEAD), F32)],
        compiler_params=pltpu.CompilerParams(dimension_semantics=("parallel", "arbitrary")),
    )(qi_tab, kj_tab, q_att, k_att, kv, do, lse_row, delta_row)


DIL_SCALE = DIL_HD**-0.5


def _dil_bias():
    slopes = 2.0 ** (-ALIBI_MAX_BIAS * np.arange(1, DIL_HEADS + 1, dtype=np.float64) / DIL_HEADS)
    slopes = slopes.astype(np.float32).reshape(DIL_GROUPS, DIL_HG)
    p = np.arange(DIL_BLK)[:, None]
    kidx = np.arange(2 * DIL_BLK)[None, :]
    j = p + DIL_BLK - kidx
    out = np.zeros((DIL_GROUPS, DIL_HG, DIL_BLK, 2 * DIL_BLK), np.float32)
    for g, (window, dil) in enumerate(DIL_PATTERNS):
        valid = (j >= 0) & (j <= window // dil)
        for h in range(DIL_HG):
            alibi = -slopes[g, h] * (dil * j).astype(np.float32)
            out[g, h] = np.where(valid, alibi, np.float32(NEG))
    return jnp.asarray(out)


DIL_UNROLL = 4


def _unrolled_loop(lo, hi, fn, unroll=DIL_UNROLL):
    groups = (hi - lo) // unroll
    done = lo
    if groups > 1:

        def step(i, carry):
            for u in range(unroll):
                fn(lo + i * unroll + u)
            return carry

        lax.fori_loop(0, groups, step, 0)
        done = lo + groups * unroll
    for n in range(done, hi):
        fn(n)


def _dil_rows(n):
    if isinstance(n, int):
        return slice(n * DIL_BLK, (n + 1) * DIL_BLK), slice(max(n - 1, 0) * DIL_BLK, (n + 1) * DIL_BLK)
    return (
        pl.ds(pl.multiple_of(n * DIL_BLK, DIL_BLK), DIL_BLK),
        pl.ds(pl.multiple_of((n - 1) * DIL_BLK, DIL_BLK), 2 * DIL_BLK),
    )


def _dil_heads_per_step(L):
    hp = DIL_HG if L <= 1024 else 1
    return hp, hp * DIL_HD


def _dil_specs(L, g, hp, w):
    per_seq = 3 * DIL_QKV // w
    spd = DIL_HG // hp

    def col(part):
        return pl.BlockSpec((L, w), lambda i: (0, (i // spd) * per_seq + (part * DIL_QKV + g * DIL_OUT) // w + i % spd))

    bias = pl.BlockSpec((hp, DIL_BLK, 2 * DIL_BLK), lambda i: (i % spd, 0, 0))
    return [bias, col(0), col(1), col(2)], pl.BlockSpec((L, w), lambda i: (0, i))


def _dil_fwd_group(view, bias_g, g, dil, name):
    L = view.shape[0]
    nb = L // DIL_BLK
    hp, w = _dil_heads_per_step(L)

    def body(bias_ref, q_ref, k_ref, v_ref, o_ref, lse_ref):
        def attend(q, kk, vv, b):
            s = _dot(q, kk, NT) * DIL_SCALE + b
            m = jnp.max(s, axis=1, keepdims=True)
            e = jnp.exp(s - m)
            l = jnp.sum(e, axis=1, keepdims=True)
            p = e * (1.0 / l)
            return _dot(p.astype(BF16), vv), m + jnp.log(l)

        def block(n):
            for hh in range(hp):
                cols = slice(hh * 128, (hh + 1) * 128)
                cur, both = _dil_rows(n)
                b = bias_ref[hh][:, DIL_BLK:] if isinstance(n, int) and n == 0 else bias_ref[hh]
                o, l = attend(q_ref[cur, cols], k_ref[both, cols], v_ref[both, cols], b)
                o_ref[cur, cols] = o
                lse_ref[cur, cols] = jnp.broadcast_to(l, (DIL_BLK, 128))

        block(0)
        _unrolled_loop(1, nb, block)

    in_specs, out = _dil_specs(L, g, hp, w)
    return pl.pallas_call(
        body,
        name=name,
        grid=(dil * DIL_HG // hp,),
        in_specs=in_specs,
        out_specs=[out, out],
        out_shape=[jax.ShapeDtypeStruct((L, dil * DIL_OUT), F32), jax.ShapeDtypeStruct((L, dil * DIL_OUT), F32)],
        compiler_params=pltpu.CompilerParams(dimension_semantics=("parallel",)),
    )(bias_g, view, view, view)


def _dil_combine(os_, ls_, name, tr=512):
    S = os_[0].shape[0]

    def body(o0, o1, o2, l0, l1, l2, out_ref, lse_ref):
        a, b, c = l0[...], l1[...], l2[...]
        m = jnp.maximum(jnp.maximum(a, b), c)
        ea, eb, ec = jnp.exp(a - m), jnp.exp(b - m), jnp.exp(c - m)
        den = ea + eb + ec
        inv = 1.0 / den
        out_ref[...] = (ea * inv) * o0[...] + (eb * inv) * o1[...] + (ec * inv) * o2[...]
        lse_ref[...] = m + jnp.log(den)

    row = pl.BlockSpec((tr, DIL_OUT), lambda i: (i, 0))
    return pl.pallas_call(
        body,
        name=name,
        grid=(S // tr,),
        in_specs=[row] * 6,
        out_specs=[row, row],
        out_shape=[jax.ShapeDtypeStruct((S, DIL_OUT), F32)] * 2,
        compiler_params=pltpu.CompilerParams(dimension_semantics=("parallel",)),
    )(*os_, *ls_)


def _dil_rowdot(dod, od, name, tr=512):
    S = dod.shape[0]

    def body(d_ref, o_ref, dd_ref, db_ref):
        db_ref[...] = d_ref[...].astype(BF16)
        for h in range(DIL_HG):
            sl = slice(h * 128, (h + 1) * 128)
            sm = jnp.sum(d_ref[:, sl] * o_ref[:, sl], axis=1, keepdims=True)
            dd_ref[:, sl] = jnp.broadcast_to(sm, (tr, 128))

    row = pl.BlockSpec((tr, DIL_OUT), lambda i: (i, 0))
    return pl.pallas_call(
        body,
        name=name,
        grid=(S // tr,),
        in_specs=[row, row],
        out_specs=[row, row],
        out_shape=[jax.ShapeDtypeStruct((S, DIL_OUT), F32), jax.ShapeDtypeStruct((S, DIL_OUT), BF16)],
        compiler_params=pltpu.CompilerParams(dimension_semantics=("parallel",)),
    )(dod, od)


def _dil_bwd_group(view, bias_g, do_view, dd_view, lse_view, g, dil, name):
    L = view.shape[0]
    nb = L // DIL_BLK
    hp, w = _dil_heads_per_step(L)

    def body(bias_ref, q_ref, k_ref, v_ref, do_ref, dd_ref, lse_ref, dq_ref, dk_ref, dv_ref, dk_sc, dv_sc):
        dk_sc[...] = jnp.zeros((L, w), F32)
        dv_sc[...] = jnp.zeros((L, w), F32)

        def grads(q, kk, vv, do, dd, lse, b):
            s = _dot(q, kk, NT) * DIL_SCALE + b
            p = jnp.exp(s - lse)
            dp = _dot(do, vv, NT)
            ds = ((p * (dp - dd)) * DIL_SCALE).astype(BF16)
            return _dot(ds, kk), _dot(ds, q, TN), _dot(p.astype(BF16), do, TN)

        def block(n):
            for hh in range(hp):
                cols = slice(hh * 128, (hh + 1) * 128)
                stat = slice(hh * 128, hh * 128 + 1)
                cur, both = _dil_rows(n)
                b = bias_ref[hh][:, DIL_BLK:] if isinstance(n, int) and n == 0 else bias_ref[hh]
                dq, dk, dv = grads(q_ref[cur, cols], k_ref[both, cols], v_ref[both, cols], do_ref[cur, cols], dd_ref[cur, stat], lse_ref[cur, stat], b)
                dq_ref[cur, cols] = dq.astype(BF16)
                dk_sc[both, cols] += dk
                dv_sc[both, cols] += dv

        block(0)
        _unrolled_loop(1, nb, block)
        dk_ref[...] = dk_sc[...].astype(BF16)
        dv_ref[...] = dv_sc[...].astype(BF16)

    in_specs, out = _dil_specs(L, g, hp, w)
    return pl.pallas_call(
        body,
        name=name,
        grid=(dil * DIL_HG // hp,),
        in_specs=in_specs + [out, out, out],
        out_specs=[out, out, out],
        out_shape=[jax.ShapeDtypeStruct((L, dil * DIL_OUT), BF16)] * 3,
        scratch_shapes=[pltpu.VMEM((L, w), F32), pltpu.VMEM((L, w), F32)],
        compiler_params=pltpu.CompilerParams(dimension_semantics=("parallel",)),
    )(bias_g, view, view, view, do_view, dd_view, lse_view)


def _merge_fwd(gates, o_a, o_b, name, tr=256):
    S = o_a.shape[0]

    def body(ga_ref, gb_ref, oa_ref, ob_ref, m_ref):
        m_ref[...] = (ga_ref[...] * oa_ref[...] + gb_ref[...] * ob_ref[...]).astype(BF16)

    row = pl.BlockSpec((tr, D_MODEL), lambda i: (i, 0))
    return pl.pallas_call(
        body,
        name=name,
        grid=(S // tr,),
        in_specs=[row, pl.BlockSpec((tr, D_MODEL), lambda i: (i, 1)), row, row],
        out_specs=row,
        out_shape=jax.ShapeDtypeStruct((S, D_MODEL), BF16),
        compiler_params=pltpu.CompilerParams(dimension_semantics=("parallel",)),
    )(gates, gates, o_a, o_b)


def _merge_bwd(dmrg, gates, o_a, o_b, name, tr=256):
    S = o_a.shape[0]

    def body(dm_ref, ga_ref, gb_ref, oa_ref, ob_ref, doa_ref, dob_ref, dga_ref, dgb_ref, dba_ref, dbb_ref):
        dm, ga, gb = dm_ref[...], ga_ref[...], gb_ref[...]
        doa_ref[...] = (dm * ga).astype(BF16)
        dob_ref[...] = (dm * gb).astype(BF16)
        dga = (dm * oa_ref[...]) * (ga * (1.0 - ga))
        dgb = (dm * ob_ref[...]) * (gb * (1.0 - gb))
        dga_ref[...] = dga.astype(BF16)
        dgb_ref[...] = dgb.astype(BF16)
        sa = jnp.sum(dga, axis=0, keepdims=True)
        sb = jnp.sum(dgb, axis=0, keepdims=True)

        @pl.when(pl.program_id(0) == 0)
        def _():
            dba_ref[...] = sa
            dbb_ref[...] = sb

        @pl.when(pl.program_id(0) != 0)
        def _():
            dba_ref[...] += sa
            dbb_ref[...] += sb

    row = pl.BlockSpec((tr, D_MODEL), lambda i: (i, 0))
    row1 = pl.BlockSpec((tr, D_MODEL), lambda i: (i, 1))
    vec = pl.BlockSpec((1, D_MODEL), lambda i: (0, 0))
    outs = pl.pallas_call(
        body,
        name=name,
        grid=(S // tr,),
        in_specs=[row, row, row1, row, row],
        out_specs=[row, row, row, row, vec, vec],
        out_shape=[jax.ShapeDtypeStruct((S, D_MODEL), BF16)] * 4 + [jax.ShapeDtypeStruct((1, D_MODEL), F32)] * 2,
        compiler_params=pltpu.CompilerParams(dimension_semantics=("arbitrary",)),
    )(dmrg, gates, gates, o_a, o_b)
    return outs


CONV_TR = 512
CONV_TC = 512
N_FFC = D_FF_PAD // CONV_TC


def _shift_down(x, edge_rows, k):
    out = pltpu.roll(x, k, 0)
    row = lax.broadcasted_iota(jnp.int32, x.shape, 0)
    for i in range(k):
        out = jnp.where(row == i, edge_rows[i], out)
    return out


def _conv_taps(x, halo_ref, live, w_ref, b_ref):
    h6, h7 = halo_ref[6:7, :] * live, halo_ref[7:8, :] * live
    s1 = _shift_down(x, [h7], 1)
    s2 = _shift_down(x, [h6, h7], 2)
    u = ((b_ref[...] + w_ref[0:1, :] * s2) + w_ref[1:2, :] * s1) + w_ref[2:3, :] * x
    return u, s1, s2


def _prev_halo(tr):
    return lambda i, j: (jnp.maximum(i * (tr // 8) - 1, 0), j)


def _ffn_fwd(u0, cw, cb, name):
    S = u0.shape[0]
    tr, tc = CONV_TR, CONV_TC

    def body(up_ref, gt_ref, hup_ref, hgt_ref, wu_ref, wg_ref, bu_ref, bg_ref, a_ref):
        live = (pl.program_id(0) > 0).astype(F32)
        up, _, _ = _conv_taps(up_ref[...], hup_ref, live, wu_ref, bu_ref)
        gt, _, _ = _conv_taps(gt_ref[...], hgt_ref, live, wg_ref, bg_ref)
        a_ref[...] = ((gt * jax.nn.sigmoid(gt)) * up).astype(BF16)

    return pl.pallas_call(
        body,
        name=name,
        grid=(S // tr, N_FFC),
        in_specs=[
            pl.BlockSpec((tr, tc), lambda i, j: (i, j)),
            pl.BlockSpec((tr, tc), lambda i, j: (i, j + N_FFC)),
            pl.BlockSpec((8, tc), _prev_halo(tr)),
            pl.BlockSpec((8, tc), lambda i, j: (jnp.maximum(i * (tr // 8) - 1, 0), j + N_FFC)),
            pl.BlockSpec((8, tc), lambda i, j: (0, j)),
            pl.BlockSpec((8, tc), lambda i, j: (0, j + N_FFC)),
            pl.BlockSpec((1, tc), lambda i, j: (0, j)),
            pl.BlockSpec((1, tc), lambda i, j: (0, j + N_FFC)),
        ],
        out_specs=pl.BlockSpec((tr, tc), lambda i, j: (i, j)),
        out_shape=jax.ShapeDtypeStruct((S, D_FF_PAD), BF16),
        compiler_params=pltpu.CompilerParams(dimension_semantics=("parallel", "parallel")),
    )(u0, u0, u0, u0, cw, cw, cb, cb)


def _ffn_bwd(u0, da, cw, cb, name):
    S = u0.shape[0]
    tr, tc = CONV_TR, CONV_TC
    nrow, te = S // tr, tr + 8

    def body(up_ref, gt_ref, hup_ref, hgt_ref, nup_ref, ngt_ref, da_ref, nda_ref, wu_ref, wg_ref, bu_ref, bg_ref, du0_ref, dcw_ref, dcb_ref):
        i = pl.program_id(1)
        prev_live = (i > 0).astype(F32)
        next_live = (i < nrow - 1).astype(F32)

        def conv(x_ref, nx_ref, h_ref, w_ref, b_ref):
            x = jnp.concatenate([x_ref[...], nx_ref[...] * next_live], axis=0)
            h6, h7 = h_ref[6:7, :] * prev_live, h_ref[7:8, :] * prev_live
            s1 = _shift_down(x, [h7], 1)
            s2 = _shift_down(x, [h6, h7], 2)
            return ((b_ref[...] + w_ref[0:1, :] * s2) + w_ref[1:2, :] * s1) + w_ref[2:3, :] * x, x, s1, s2

        up, xu0, xu1, xu2 = conv(up_ref, nup_ref, hup_ref, wu_ref, bu_ref)
        gt, xg0, xg1, xg2 = conv(gt_ref, ngt_ref, hgt_ref, wg_ref, bg_ref)
        da_v = jnp.concatenate([da_ref[...], nda_ref[...] * next_live], axis=0)
        sg = jax.nn.sigmoid(gt)
        d_up = da_v * (gt * sg)
        d_gt = (da_v * up) * (sg * (1.0 + gt * (1.0 - sg)))
        tap = lax.broadcasted_iota(jnp.int32, (8, tc), 0)

        def finish(half, du, x0, x1, x2, w_ref):
            n1 = pltpu.roll(du, te - 1, 0)
            n2 = pltpu.roll(du, te - 2, 0)
            du0 = (w_ref[2:3, :] * du + w_ref[1:2, :] * n1) + w_ref[0:1, :] * n2
            du0_ref[half] = du0[:tr].astype(BF16)
            d = du[:tr]
            dcw = jnp.where(
                tap == 0,
                jnp.sum(d * x2[:tr], axis=0, keepdims=True),
                jnp.where(tap == 1, jnp.sum(d * x1[:tr], axis=0, keepdims=True), jnp.where(tap == 2, jnp.sum(d * x0[:tr], axis=0, keepdims=True), 0.0)),
            )
            dcb = jnp.sum(d, axis=0, keepdims=True)

            @pl.when(i == 0)
            def _():
                dcw_ref[half] = dcw
                dcb_ref[half] = dcb

            @pl.when(i != 0)
            def _():
                dcw_ref[half] += dcw
                dcb_ref[half] += dcb

        finish(0, d_up, xu0, xu1, xu2, wu_ref)
        finish(1, d_gt, xg0, xg1, xg2, wg_ref)

    def prev8(off):
        return pl.BlockSpec((8, tc), lambda j, i: (jnp.maximum(i * (tr // 8) - 1, 0), j + off))

    def next8(off):
        return pl.BlockSpec((8, tc), lambda j, i: (jnp.minimum((i + 1) * (tr // 8), S // 8 - 1), j + off))

    return pl.pallas_call(
        body,
        name=name,
        grid=(N_FFC, nrow),
        in_specs=[
            pl.BlockSpec((tr, tc), lambda j, i: (i, j)),
            pl.BlockSpec((tr, tc), lambda j, i: (i, j + N_FFC)),
            prev8(0),
            prev8(N_FFC),
            next8(0),
            next8(N_FFC),
            pl.BlockSpec((tr, tc), lambda j, i: (i, j)),
            next8(0),
            pl.BlockSpec((8, tc), lambda j, i: (0, j)),
            pl.BlockSpec((8, tc), lambda j, i: (0, j + N_FFC)),
            pl.BlockSpec((1, tc), lambda j, i: (0, j)),
            pl.BlockSpec((1, tc), lambda j, i: (0, j + N_FFC)),
        ],
        out_specs=[
            pl.BlockSpec((2, tr, tc), lambda j, i: (0, i, j)),
            pl.BlockSpec((2, 8, tc), lambda j, i: (0, 0, j)),
            pl.BlockSpec((2, 1, tc), lambda j, i: (0, 0, j)),
        ],
        out_shape=[
            jax.ShapeDtypeStruct((2, S, D_FF_PAD), BF16),
            jax.ShapeDtypeStruct((2, 8, D_FF_PAD), F32),
            jax.ShapeDtypeStruct((2, 1, D_FF_PAD), F32),
        ],
        compiler_params=pltpu.CompilerParams(dimension_semantics=("parallel", "arbitrary")),
    )(u0, u0, u0, u0, u0, u0, da, da, cw, cw, cb, cb)


def _adamw(w, g, m, v, name):
    R, C = w.shape
    tr = R
    for cand in (256, 128, 64, 32, 16, 8):
        if R % cand == 0 and R > cand:
            tr = cand
            break

    def body(w_ref, g_ref, m_ref, v_ref, d_ref, nm_ref, nv_ref):
        gv = g_ref[...]
        nm = ADAM_B1 * m_ref[...] + (1.0 - ADAM_B1) * gv
        nv = ADAM_B2 * v_ref[...] + (1.0 - ADAM_B2) * (gv * gv)
        m_hat = nm / (1.0 - ADAM_B1**ADAM_STEP)
        v_hat = nv / (1.0 - ADAM_B2**ADAM_STEP)
        d_ref[...] = -ADAM_LR * (m_hat / (jnp.sqrt(v_hat) + ADAM_EPS) + ADAM_WD * w_ref[...])
        nm_ref[...] = nm
        nv_ref[...] = nv

    blk = pl.BlockSpec((tr, C), lambda i: (i, 0))
    return pl.pallas_call(
        body,
        name=name,
        grid=(R // tr,),
        in_specs=[blk] * 4,
        out_specs=[blk] * 3,
        out_shape=[jax.ShapeDtypeStruct((R, C), F32)] * 3,
        compiler_params=pltpu.CompilerParams(dimension_semantics=("parallel",)),
    )(w, g, m, v)


ANY = pl.BlockSpec(memory_space=pl.ANY)


def _all_gather(blocks, name):
    n = len(blocks)

    def body(*refs):
        x_refs, out_refs = refs[:n], refs[n : 2 * n]
        send_sems, recv_sems, local_sems = refs[2 * n :]
        x, y, c = lax.axis_index("x"), lax.axis_index("y"), lax.axis_index("c")
        me, sibling = (x, y, c), (x, y, 1 - c)
        chips = [(1 - x, y), (x, 1 - y), (1 - x, 1 - y)]

        def slot(a, px, py, pc):
            return out_refs[a].at[4 * px + 2 * py + pc]

        def copy(a, k, blk, to, src=None):
            return pltpu.make_async_remote_copy(
                src_ref=slot(a, *blk) if src is None else src,
                dst_ref=slot(a, *blk),
                send_sem=send_sems.at[7 * a + k],
                recv_sem=recv_sems.at[7 * a + k],
                device_id=to,
                device_id_type=MESH,
            )

        mine = [pltpu.make_async_copy(x_refs[a], slot(a, *me), local_sems.at[a]) for a in range(n)]
        sent = []
        for a in range(n):
            mine[a].start()
            first = [copy(a, 0, me, sibling, src=x_refs[a])]
            first += [copy(a, 1 + j, me, (*chip, c), src=x_refs[a]) for j, chip in enumerate(chips)]
            for cp in first:
                cp.start()
            sent += first
        for a in range(n):
            for j, chip in enumerate(chips):
                copy(a, 1 + j, (*chip, c), me).wait_recv()
                passed = copy(a, 4 + j, (*chip, c), sibling)
                passed.start()
                sent.append(passed)
        for a in range(n):
            copy(a, 0, sibling, me).wait_recv()
            for j, chip in enumerate(chips):
                copy(a, 4 + j, (*chip, 1 - c), me).wait_recv()
        for cp in sent:
            cp.wait_send()
        for cp in mine:
            cp.wait()

    return pl.pallas_call(
        body,
        name=name,
        out_shape=[jax.ShapeDtypeStruct((N_DEV,) + b.shape, b.dtype) for b in blocks],
        in_specs=[ANY] * n,
        out_specs=[ANY] * n,
        scratch_shapes=[pltpu.SemaphoreType.DMA((7 * n,)), pltpu.SemaphoreType.DMA((7 * n,)), pltpu.SemaphoreType.DMA((n,))],
    )(*blocks)


def _pair_exchange(gs, name):
    n = len(gs)

    def body(*refs):
        g_refs, out_refs = refs[:n], refs[n : 2 * n]
        send_sems, recv_sems = refs[2 * n :]
        x, y, c = lax.axis_index("x"), lax.axis_index("y"), lax.axis_index("c")
        copies = [
            pltpu.make_async_remote_copy(
                src_ref=g_refs[a].at[2 * k + (1 - c)],
                dst_ref=out_refs[a].at[k],
                send_sem=send_sems.at[N_CHIP * a + k],
                recv_sem=recv_sems.at[N_CHIP * a + k],
                device_id=(x, y, 1 - c),
                device_id_type=MESH,
            )
            for a in range(n)
            for k in range(N_CHIP)
        ]
        for cp in copies:
            cp.start()
        for cp in copies:
            cp.wait()

    return pl.pallas_call(
        body,
        name=name,
        out_shape=[jax.ShapeDtypeStruct((N_CHIP,) + g.shape[1:], g.dtype) for g in gs],
        in_specs=[ANY] * n,
        out_specs=[ANY] * n,
        scratch_shapes=[pltpu.SemaphoreType.DMA((N_CHIP * n,)), pltpu.SemaphoreType.DMA((N_CHIP * n,))],
    )(*gs)


def _row_tile(rows):
    return max(t for t in range(16, 353, 16) if rows % t == 0)


def _pair_add(g, recv, core, name):
    _, R, C = g.shape
    tr = _row_tile(R)

    def body(core_ref, g_ref, r_ref, o_ref):
        o_ref[...] = (g_ref[...].astype(F32) + r_ref[...].astype(F32)).astype(o_ref.dtype)

    return pl.pallas_call(
        body,
        name=name,
        grid_spec=pltpu.PrefetchScalarGridSpec(
            num_scalar_prefetch=1,
            grid=(N_CHIP, R // tr),
            in_specs=[
                pl.BlockSpec((1, tr, C), lambda k, i, core_ref: (2 * k + core_ref[0], i, 0)),
                pl.BlockSpec((1, tr, C), lambda k, i, core_ref: (k, i, 0)),
            ],
            out_specs=pl.BlockSpec((1, tr, C), lambda k, i, core_ref: (k, i, 0)),
        ),
        out_shape=jax.ShapeDtypeStruct((N_CHIP, R, C), g.dtype),
        compiler_params=pltpu.CompilerParams(dimension_semantics=("parallel", "parallel")),
    )(core, g, recv)


HBM = pl.BlockSpec(memory_space=pltpu.HBM)
SEM = pl.BlockSpec(memory_space=pltpu.SEMAPHORE)
EFFECT = pltpu.SideEffectType.DATAFLOW_SIDE_EFFECTING
RELATIONS = tuple((dx, dy, dc) for dx in (0, 1) for dy in (0, 1) for dc in (0, 1))[1:]


def _related(rel):
    x, y, c = lax.axis_index("x"), lax.axis_index("y"), lax.axis_index("c")
    return (1 - x if rel[0] else x, 1 - y if rel[1] else y, 1 - c if rel[2] else c)


def _dev_index(pos):
    return 4 * pos[0] + 2 * pos[1] + pos[2]


def _peers(chips):
    if chips:
        return [r for r in RELATIONS if not r[2]], N_CHIP, lambda pos: 2 * pos[0] + pos[1]
    return list(RELATIONS), N_DEV, _dev_index


def _exchange_start(srcs, by_slot, after, name, chips=False):
    n = len(srcs)
    rels, slots, slot_of = _peers(chips)
    lands = [lax.empty((slots,) + (s.shape[1:] if by_slot else s.shape), s.dtype) for s in srcs]
    nsem = len(rels) * n

    def body(*refs):
        src_refs, land_refs = refs[:n], refs[n : 2 * n]
        send_sems, recv_sems = refs[2 * n + 1], refs[2 * n + 2]
        token = refs[-1]
        me = slot_of(_related((0, 0, 0)))
        for a in range(n):
            for k, rel in enumerate(rels):
                peer = _related(rel)
                pltpu.make_async_remote_copy(
                    src_ref=src_refs[a].at[slot_of(peer)] if by_slot else src_refs[a],
                    dst_ref=land_refs[a].at[me],
                    send_sem=send_sems.at[len(rels) * a + k],
                    recv_sem=recv_sems.at[len(rels) * a + k],
                    device_id=peer,
                    device_id_type=MESH,
                ).start()
        token[...] = jnp.zeros_like(token)

    def hbm(a):
        return pltpu.HBM(a.shape, a.dtype)

    outs = pl.pallas_call(
        body,
        name=name,
        out_shape=(
            pltpu.SemaphoreType.DMA((nsem,)),
            pltpu.SemaphoreType.DMA((nsem,)),
            *[hbm(s) for s in srcs],
            *[hbm(l) for l in lands],
            jax.ShapeDtypeStruct((8, 128), F32),
        ),
        in_specs=[HBM] * (2 * n) + [ANY],
        out_specs=(SEM, SEM, *[HBM] * (2 * n), pl.BlockSpec(memory_space=pltpu.VMEM)),
        input_output_aliases={i: 2 + i for i in range(2 * n)},
        compiler_params=pltpu.CompilerParams(has_side_effects=EFFECT),
    )(*[pltpu.with_memory_space_constraint(a, pltpu.HBM) for a in list(srcs) + lands], after)
    return (outs[0], outs[1], list(outs[2 : 2 + n]), list(outs[2 + n : 2 + 2 * n])), outs[-1]


def _exchange_wait(handle, by_slot, after, name, chips=False):
    send_sems, recv_sems, srcs, lands = handle
    n = len(srcs)
    rels = _peers(chips)[0]

    def body(*refs):
        src_refs, land_refs = refs[:n], refs[n : 2 * n]
        s_sems, r_sems = refs[2 * n], refs[2 * n + 1]
        for a in range(n):
            for k, rel in enumerate(rels):
                copy = pltpu.make_async_remote_copy(
                    src_ref=src_refs[a].at[0] if by_slot else src_refs[a],
                    dst_ref=land_refs[a].at[0],
                    send_sem=s_sems.at[len(rels) * a + k],
                    recv_sem=r_sems.at[len(rels) * a + k],
                    device_id=_related(rel),
                    device_id_type=MESH,
                )
                copy.wait_send()
                copy.wait_recv()

    outs = pl.pallas_call(
        body,
        name=name,
        out_shape=tuple(pltpu.HBM(a.shape, a.dtype) for a in srcs + lands),
        in_specs=[HBM] * (2 * n) + [SEM, SEM, ANY],
        out_specs=tuple([HBM] * (2 * n)),
        input_output_aliases={i: i for i in range(2 * n)},
        compiler_params=pltpu.CompilerParams(has_side_effects=EFFECT),
    )(*srcs, *lands, send_sems, recv_sems, after)
    return list(outs[:n]), list(outs[n:])


def _slot_sum(parts, name):
    n, R, C = parts.shape
    tr = _row_tile(R) if R % 16 == 0 else R

    def body(p_ref, o_ref):
        acc = p_ref[0].astype(F32)
        for k in range(1, n):
            acc = acc + p_ref[k].astype(F32)
        o_ref[...] = acc

    return pl.pallas_call(
        body,
        name=name,
        grid=(R // tr,),
        in_specs=[pl.BlockSpec((n, tr, C), lambda i: (0, i, 0))],
        out_specs=pl.BlockSpec((tr, C), lambda i: (i, 0)),
        out_shape=jax.ShapeDtypeStruct((R, C), F32),
        compiler_params=pltpu.CompilerParams(dimension_semantics=("parallel",)),
    )(parts)


W_IN_TC = 256
W_IN_BOUNDS = (0, LAT, LAT + 3 * DIL_QKV, LAT + 3 * DIL_QKV + D_MODEL, D_IN)


def _w_in_regroup(slots, name):
    tc = W_IN_TC

    def body(s_ref, lat_ref, dqkv_ref, g_ref, buf):
        for j in range(N_DEV):
            buf[j * IN_ROWS : (j + 1) * IN_ROWS, :] = s_ref[j].astype(F32)[:IN_ROWS, :]
        lat_ref[:LAT, :] = buf[:LAT, :].astype(BF16)
        lat_ref[LAT:, :] = jnp.zeros((LAT_PAD - LAT, tc), BF16)
        dqkv_ref[...] = buf[W_IN_BOUNDS[1] : W_IN_BOUNDS[2], :].astype(BF16)
        g_ref[...] = buf[W_IN_BOUNDS[2] :, :].astype(BF16)

    def col(rows):
        return pl.BlockSpec((rows, tc), lambda k: (0, k))

    return pl.pallas_call(
        body,
        name=name,
        grid=(D_MODEL // tc,),
        in_specs=[pl.BlockSpec((N_DEV, IN_ROWS_PAD, tc), lambda k: (0, 0, k))],
        out_specs=[col(LAT_PAD), col(3 * DIL_QKV), col(2 * D_MODEL)],
        out_shape=[
            jax.ShapeDtypeStruct((LAT_PAD, D_MODEL), BF16),
            jax.ShapeDtypeStruct((3 * DIL_QKV, D_MODEL), BF16),
            jax.ShapeDtypeStruct((2 * D_MODEL, D_MODEL), BF16),
        ],
        scratch_shapes=[pltpu.VMEM((D_IN, tc), F32)],
        compiler_params=pltpu.CompilerParams(dimension_semantics=("parallel",)),
    )(slots)


def _w_in_grad_regroup(g_lat, g_dqkv, g_ga, g_gb, name):
    tc = W_IN_TC

    def body(lat_ref, dqkv_ref, ga_ref, gb_ref, o_ref, buf):
        b = W_IN_BOUNDS
        buf[b[0] : b[1], :] = lat_ref[:LAT, :].astype(F32)
        buf[b[1] : b[2], :] = dqkv_ref[...].astype(F32)
        buf[b[2] : b[3], :] = ga_ref[...].astype(F32)
        buf[b[3] : b[4], :] = gb_ref[...].astype(F32)
        fill = jnp.zeros((IN_ROWS_PAD - IN_ROWS, tc), F32)
        for j in range(N_DEV):
            o_ref[j] = jnp.concatenate([buf[j * IN_ROWS : (j + 1) * IN_ROWS, :], fill], axis=0).astype(BF16)

    def col(rows):
        return pl.BlockSpec((rows, tc), lambda k: (0, k))

    return pl.pallas_call(
        body,
        name=name,
        grid=(D_MODEL // tc,),
        in_specs=[col(LAT_PAD), col(3 * DIL_QKV), col(D_MODEL), col(D_MODEL)],
        out_specs=pl.BlockSpec((N_DEV, IN_ROWS_PAD, tc), lambda k: (0, 0, k)),
        out_shape=jax.ShapeDtypeStruct((N_DEV, IN_ROWS_PAD, D_MODEL), BF16),
        scratch_shapes=[pltpu.VMEM((D_IN, tc), F32)],
        compiler_params=pltpu.CompilerParams(dimension_semantics=("parallel",)),
    )(g_lat, g_dqkv, g_ga, g_gb)


def _ffn_pad(a, axis):
    a = jnp.moveaxis(a, axis, -1)
    g = a.reshape(a.shape[:-1] + (2 * N_DEV, FF_GROUP))
    g = jnp.pad(g, [(0, 0)] * (g.ndim - 1) + [(0, FF_GROUP_PAD - FF_GROUP)])
    return jnp.moveaxis(g.reshape(a.shape[:-1] + (2 * D_FF_PAD,)), -1, axis)


def _ffn_unpad(a, axis):
    a = jnp.moveaxis(a, axis, -1)
    g = a.reshape(a.shape[:-1] + (2 * N_DEV, FF_GROUP_PAD))[..., :FF_GROUP]
    return jnp.moveaxis(g.reshape(a.shape[:-1] + (2 * D_FF,)), -1, axis)


MISC = (("w_o_mla", (256, 1024)), ("w_o_dil", (256, 512)), ("w_uq", (192, 512)), ("w_ukv", (256, 256)))


def _exchange_blocks(w):
    def t(a):
        return a.astype(BF16).T

    up = t(w["w_up"]).reshape(2, FF_GROUP, D_MODEL)
    return [
        jnp.pad(t(w["w_in"]), ((0, IN_ROWS_PAD - IN_ROWS), (0, 0))),
        jnp.pad(up, ((0, 0), (0, FF_GROUP_PAD - FF_GROUP), (0, 0))).reshape(2 * FF_GROUP_PAD, D_MODEL),
        jnp.pad(w["w_down"].astype(BF16), ((0, FF_GROUP_PAD - FF_GROUP), (0, 0))),
        w["w_out"].astype(BF16),
        jnp.concatenate([t(w[n]).reshape(-1, D_MODEL) for n, _ in MISC], axis=0),
    ]


def _misc_split(misc):
    out, off = {}, 0
    for n, (r, c) in MISC:
        rows = r * c // D_MODEL
        out[n] = misc[..., off : off + rows, :].reshape(misc.shape[:-2] + (r, c))
        off += rows
    return out


def _small_matrices(g_misc):
    misc = _misc_split(g_misc)
    uq_t = jnp.pad(misc["w_uq"], ((0, 0), (0, HEAD_PAD - QK_NOPE - QK_ROPE), (0, 0)))
    return {
        "uq_t": uq_t.reshape(MLA_HEADS * HEAD_PAD, Q_LORA),
        "ukv_t": misc["w_ukv"].reshape(MLA_HEADS * HEAD_PAD, KV_LORA),
        "o_mla_t": misc["w_o_mla"].reshape(D_MODEL, MLA_HEADS * V_HEAD),
        "o_dil_t": misc["w_o_dil"].reshape(D_MODEL, DIL_OUT),
    }


def _small_grad_blocks(g):
    uq_t = g["uq_t"].reshape(MLA_HEADS, HEAD_PAD, Q_LORA)[:, : QK_NOPE + QK_ROPE]
    misc = {"w_o_mla": g["o_mla_t"], "w_o_dil": g["o_dil_t"], "w_uq": uq_t, "w_ukv": g["ukv_t"]}
    return [
        g["w_out"].reshape(N_DEV, -1, D_MODEL),
        jnp.concatenate([misc[n].reshape(N_DEV, -1, D_MODEL) for n, _ in MISC], axis=1),
    ]


def _grad_shards(sums):
    s_in, s_out, s_misc, s_up, s_down = sums
    out = {
        "w_in": s_in[:IN_ROWS].T,
        "w_up": s_up.reshape(2, FF_GROUP_PAD, D_MODEL)[:, :FF_GROUP].reshape(2 * FF_GROUP, D_MODEL).T,
        "w_down": s_down[:FF_GROUP],
        "w_out": s_out,
    }
    out.update({n: v.T for n, v in _misc_split(s_misc).items()})
    return out


def _local_step(x, tgt, wt, conv_w, small, small_matrices, ffn_weights, send_ffn_grads, send_small_grads, send_w_in_grads, start_token):
    S = x.shape[0]
    lat_t, dqkv_t, g_t = wt
    cw = jnp.pad(_ffn_pad(conv_w, 1), ((0, 5), (0, 0)))
    cb = _ffn_pad(small["conv_b"], 1)
    cos_t, sin_t = _rope_tables(S)
    bias = _dil_bias()
    g1, g2, g3 = small["attn_norm_g"], small["ffn_norm_g"], small["final_norm_g"]
    gq, gkv = small["q_norm_g"], small["kv_norm_g"]

    h = _rms_fwd(x, g1 + start_token, "rms_attn")
    lat = _mm(h, lat_t, "nt", F32, 1024, LAT_PAD, D_MODEL, "proj_lat")
    dqkv = _mm(h, dqkv_t, "nt", BF16, 1024, 512, D_MODEL, "proj_dqkv")
    gates = _mm(h, g_t, "nt", F32, 1024, 512, D_MODEL, "proj_gates", bias=small["b_gate"], act="sigmoid")
    sm = small_matrices(lat)
    uq_t, ukv_t, o_mla_t, o_dil_t = sm["uq_t"], sm["ukv_t"], sm["o_mla_t"], sm["o_dil_t"]
    cqn, ckvn, kpe = _mla_prep1(lat, gq, gkv, cos_t, sin_t, "mla_prep1")
    q_raw = _mm(cqn, uq_t, "nt", F32, 1024, 1024, Q_LORA, "mla_uq")
    kv = _mm(ckvn, ukv_t, "nt", BF16, 1024, 1024, KV_LORA, "mla_ukv")
    q_att, k_att = _mla_prep2(q_raw, kv, kpe, cos_t, sin_t, "mla_prep2")
    o, lse = _flash2_fwd(q_att, k_att, kv, "mla_flash_fwd")
    o_a = _mm(o, o_mla_t, "nt", F32, 1024, 1024, MLA_HEADS * V_HEAD, "mla_out")

    d_os, d_ls = [], []
    for g, (_, dil) in enumerate(DIL_PATTERNS):
        og, lg = _dil_fwd_group(dqkv.reshape(S // dil, dil * 3 * DIL_QKV), bias[g], g, dil, f"dil_fwd_{g}")
        d_os.append(og.reshape(S, DIL_OUT))
        d_ls.append(lg.reshape(S, DIL_OUT))
    od, dil_lse = _dil_combine(d_os, d_ls, "dil_combine")
    o_b = _mm(od, o_dil_t, "nt", F32, 1024, 1024, DIL_OUT, "dil_out")

    mrg = _merge_fwd(gates, o_a, o_b, "merge_fwd")
    w_out, up_t, w_down = ffn_weights(mrg)
    x1 = _mm(mrg, w_out, "nn", F32, 1024, 1024, D_MODEL, "mix_out", res=x)
    h2 = _rms_fwd(x1, g2, "rms_ffn")
    u0 = _mm(h2, up_t, "nt", F32, 1024, 512, D_MODEL, "ffn_up")
    a = _ffn_fwd(u0, cw, cb, "ffn_conv_fwd")
    x2 = _mm(a, w_down, "nn", F32, 1024, 512, D_FF_PAD // 2, "ffn_down", res=x1)
    loss_part, dx2, dx2b, dg3 = _final_loss(x2, g3, tgt, "final_loss")

    da = _mm(dx2b, w_down, "nt", F32, 1024, 512, D_MODEL, "ffn_down_dx")
    gw_down = _mm(a, dx2b, "tn", BF16, 512, 1024, S, "ffn_down_dw")
    du0, dcw, dcb = _ffn_bwd(u0, da, cw, cb, "ffn_conv_bwd")
    du0 = du0.reshape(2 * S, D_FF_PAD)
    gw_up_t = _mm(du0, h2, "tn", BF16, 512, 1024, S, "ffn_up_dw", a_halves=2)
    sent = send_ffn_grads(gw_up_t, gw_down)
    dh2 = _mm(du0, up_t, "nn", F32, 1024, 1024, D_FF_PAD // 2, "ffn_up_dx", a_halves=2)
    dx1, dx1b, dg2 = _rms_bwd(dh2, x1, g2 + sent, dx2, "rms_ffn_bwd")

    dmrg = _mm(dx1b, w_out, "nt", F32, 1024, 1024, D_MODEL, "mix_out_dx")
    gw_out = _mm(mrg, dx1b, "tn", BF16, 512, 1024, S, "mix_out_dw")
    do_a, do_b, dga, dgb, dba, dbb = _merge_bwd(dmrg, gates, o_a, o_b, "merge_bwd")

    do = _mm(do_a, o_mla_t, "nn", BF16, 1024, 1024, D_MODEL, "mla_out_dx")
    gw_o_mla_t = _mm(do_a, o, "tn", BF16, 1024, 1024, 1024, "mla_out_dw")
    dod = _mm(do_b, o_dil_t, "nn", F32, 1024, DIL_OUT, D_MODEL, "dil_out_dx")
    gw_o_dil_t = _mm(do_b, od, "tn", BF16, 1024, DIL_OUT, 1024, "dil_out_dw")

    dq_att, delta = _flash2_dq(q_att, k_att, kv, o, do, lse, "mla_flash_dq")
    lse_row = lse[:, :, 0][:, None, :]
    delta_row = delta[:, :, 0][:, None, :]
    dk_att, dv = _flash2_dkv(q_att, k_att, kv, do, lse_row, delta_row, "mla_flash_dkv")
    dq_raw, dkv, dkpe = _mla_post(dq_att, dk_att, dv, cos_t, sin_t, "mla_post")
    dcqn = _mm(dq_raw, uq_t, "nn", F32, 1024, Q_LORA, MLA_HEADS * HEAD_PAD, "mla_uq_dx")
    gw_uq_t = _mm(dq_raw, cqn, "tn", BF16, 1024, Q_LORA, 1024, "mla_uq_dw")
    dckvn = _mm(dkv, ukv_t, "nn", F32, 1024, KV_LORA, MLA_HEADS * HEAD_PAD, "mla_ukv_dx")
    gw_ukv_t = _mm(dkv, ckvn, "tn", BF16, 1024, KV_LORA, 1024, "mla_ukv_dw")
    sent = send_small_grads({"uq_t": gw_uq_t, "ukv_t": gw_ukv_t, "o_mla_t": gw_o_mla_t, "o_dil_t": gw_o_dil_t, "w_out": gw_out})
    dlat, dgq, dgkv = _lat_bwd(dcqn, dckvn, dkpe, lat, gq + sent, gkv, "lat_bwd")

    dd, dodb = _dil_rowdot(dod, od, "dil_rowdot")
    dparts = [[None] * DIL_GROUPS for _ in range(3)]
    for g, (_, dil) in enumerate(DIL_PATTERNS):
        L = S // dil
        outs = _dil_bwd_group(
            dqkv.reshape(L, dil * 3 * DIL_QKV),
            bias[g],
            dodb.reshape(L, dil * DIL_OUT),
            dd.reshape(L, dil * DIL_OUT),
            dil_lse.reshape(L, dil * DIL_OUT),
            g,
            dil,
            f"dil_bwd_{g}",
        )
        for t in range(3):
            dparts[t][g] = outs[t].reshape(S, DIL_OUT)
    ddqkv = jnp.concatenate([p for row in dparts for p in row], axis=1)

    gw_lat_t = _mm(dlat, h, "tn", BF16, LAT_PAD, 1024, S, "proj_lat_dw")
    gw_dqkv_t = _mm(ddqkv, h, "tn", BF16, 512, 1024, S, "proj_dqkv_dw")
    gw_ga_t = _mm(dga, h, "tn", BF16, 512, 1024, S, "proj_ga_dw")
    gw_gb_t = _mm(dgb, h, "tn", BF16, 512, 1024, S, "proj_gb_dw")
    sent = send_w_in_grads(gw_lat_t, gw_dqkv_t, gw_ga_t, gw_gb_t)
    dh = _mm(dlat + sent.astype(BF16), lat_t, "nn", F32, 1024, 1024, LAT_PAD, "proj_lat_dx")
    dh = _mm(ddqkv, dqkv_t, "nn", F32, 1024, 1024, 2304, "proj_dqkv_dx", res=dh)
    dh = _mm(dga, g_t, "nn", F32, 1024, 1024, D_MODEL, "proj_ga_dx", res=dh)
    dh = _mm(dgb, g_t, "nn", F32, 1024, 1024, D_MODEL, "proj_gb_dx", res=dh, b_koff=1)
    grad_x, _, dg1 = _rms_bwd(dh, x, g1, dx1, "rms_attn_bwd")

    small_grads = {
        "attn_norm_g": dg1,
        "b_gate": jnp.concatenate([dba, dbb], axis=1),
        "q_norm_g": dgq,
        "kv_norm_g": dgkv,
        "ffn_norm_g": dg2,
        "conv_b": _ffn_unpad(jnp.concatenate([dcb[0], dcb[1]], axis=1), 1),
        "final_norm_g": dg3,
        "conv_w": _ffn_unpad(jnp.concatenate([dcw[0, :3], dcw[1, :3]], axis=1), 1),
    }
    return loss_part, grad_x, small_grads


SMALL_ORDER = ("attn_norm_g", "b_gate", "q_norm_g", "kv_norm_g", "ffn_norm_g", "conv_b", "final_norm_g", "conv_w")
WEIGHT_ORDER = (
    "attn_norm_g", "w_in", "b_gate", "q_norm_g", "w_uq", "kv_norm_g", "w_ukv", "w_o_mla", "w_o_dil", "w_out",
    "ffn_norm_g", "w_up", "conv_w", "conv_b", "w_down", "final_norm_g",
)


def kernel(x, attn_norm_g, w_in, b_gate, q_norm_g, w_uq, kv_norm_g, w_ukv, w_o_mla, w_o_dil, w_out, ffn_norm_g, w_up, conv_w, conv_b, w_down, final_norm_g, loss_target, m_attn_norm_g, m_w_in, m_b_gate, m_q_norm_g, m_w_uq, m_kv_norm_g, m_w_ukv, m_w_o_mla, m_w_o_dil, m_w_out, m_ffn_norm_g, m_w_up, m_conv_w, m_conv_b, m_w_down, m_final_norm_g, v_attn_norm_g, v_w_in, v_b_gate, v_q_norm_g, v_w_uq, v_kv_norm_g, v_w_ukv, v_w_o_mla, v_w_o_dil, v_w_out, v_ffn_norm_g, v_w_up, v_conv_w, v_conv_b, v_w_down, v_final_norm_g):
    env = dict(locals())
    dev = 4 * lax.axis_index("x") + 2 * lax.axis_index("y") + lax.axis_index("c")
    core = lax.axis_index("c").astype(jnp.int32).reshape(1)

    def two_d(a):
        return a.reshape(-1, a.shape[-1])

    w = {n: two_d(env[n]) for n in WEIGHT_ORDER}
    m = {n: two_d(env["m_" + n]) for n in WEIGHT_ORDER}
    v = {n: two_d(env["v_" + n]) for n in WEIGHT_ORDER}

    chip = 2 * lax.axis_index("x") + lax.axis_index("y")

    def own_slot_in(lands, own, slot=dev):
        return [lax.dynamic_update_slice(l, o[None], (slot, 0, 0)) for l, o in zip(lands, own)]

    b_in, b_up, b_down, b_out, b_misc = _exchange_blocks(w)
    r, c = CONV_SHARD
    conv = jnp.pad(w["conv_w"].reshape(-1), (0, 8 * SMALL_COLS - r * c)).reshape(8, SMALL_COLS)
    g_in, conv = _all_gather([b_in, conv], "ag_w_in")
    misc_gather, started = _exchange_start([b_misc], False, conv, "ag_small_start")
    ffn_gather, started2 = _exchange_start([b_up, b_down, b_out], False, started, "ag_ffn_start")
    wt = _w_in_regroup(g_in, "w_in_regroup")
    conv = conv.reshape(N_DEV, 8 * SMALL_COLS)[:, : r * c].reshape(N_DEV, r, c)
    conv_w_full = conv.transpose(1, 0, 2).reshape(r, N_DEV * c)
    small = {n: w[n] for n in SMALL_ORDER if n != "conv_w"}

    def small_matrices(after):
        own, lands = _exchange_wait(misc_gather, False, after, "ag_small_wait")
        return _small_matrices(own_slot_in(lands, own)[0])

    def ffn_weights(after):
        own, lands = _exchange_wait(ffn_gather, False, after, "ag_ffn_wait")
        g_up, g_down, g_out = own_slot_in(lands, own)
        return g_out.reshape(D_MODEL, D_MODEL), g_up.reshape(2 * D_FF_PAD, D_MODEL), g_down.reshape(D_FF_PAD, D_MODEL)

    reduces = {}

    def send_ffn_grads(gw_up_t, gw_down):
        blocks = [gw_up_t.reshape(N_DEV, 2 * FF_GROUP_PAD, D_MODEL), gw_down.reshape(N_DEV, FF_GROUP_PAD, D_MODEL)]
        reduces["ffn"], token = _exchange_start(blocks, True, gw_down, "rs_ffn_start")
        return token[0, 0]

    def send_small_grads(g):
        reduces["small"], token = _exchange_start(_small_grad_blocks(g), True, g["w_out"], "rs_small_start")
        return token[0, 0]

    def send_w_in_grads(g_lat, g_dqkv, g_ga, g_gb):
        e_in = _w_in_grad_regroup(g_lat, g_dqkv, g_ga, g_gb, "w_in_grad_regroup")
        pair = _pair_add(e_in, _pair_exchange([e_in], "rs_w_in_pair_exchange")[0], core, "rs_w_in_pair_add")
        reduces["w_in"], token = _exchange_start([pair], True, pair, "rs_w_in_start", chips=True)
        return token[0, 0]

    loss_part, grad_x, small_grads = _local_step(
        x[0], loss_target[0], wt, conv_w_full, small, small_matrices, ffn_weights,
        send_ffn_grads, send_small_grads, send_w_in_grads, started2[0, 0],
    )
    loss = lax.psum(loss_part[0, 0], AXES)

    def finish(key, by_chip, name):
        sent, lands = _exchange_wait(reduces[key], True, grad_x, name + "_wait", chips=by_chip)
        slot = chip if by_chip else dev
        own = [lax.dynamic_index_in_dim(s, slot, 0, keepdims=False) for s in sent]
        return [_slot_sum(p, f"{name}_sum_{i}") for i, p in enumerate(own_slot_in(lands, own, slot))]

    (s_in,) = finish("w_in", True, "rs_w_in")
    s_out, s_misc = finish("small", False, "rs_small")
    s_up, s_down = finish("ffn", False, "rs_ffn")
    gshard = _grad_shards([s_in, s_out, s_misc, s_up, s_down])

    sflat = jnp.concatenate([small_grads[n].reshape(-1) for n in SMALL_ORDER])
    sflat = jnp.pad(sflat, (0, SMALL_ROWS * SMALL_COLS - sflat.shape[0])).reshape(SMALL_ROWS, SMALL_COLS)
    ssum = _slot_sum(_all_gather([sflat], "ag_small_grads")[0], "small_sum").reshape(-1)
    gsmall, off = {}, 0
    for n in SMALL_ORDER:
        shape = (3, 2 * D_FF) if n == "conv_w" else w[n].shape
        size = shape[0] * shape[1]
        gsmall[n] = ssum[off : off + size].reshape(shape)
        off += size
    gsmall["conv_w"] = lax.dynamic_slice_in_dim(gsmall["conv_w"], dev * CONV_SHARD[1], CONV_SHARD[1], axis=1)

    g_all = {**gshard, **gsmall}
    out_g, out_d, out_m, out_v = [], [], [], []
    for n in WEIGHT_ORDER:
        d, nm, nv = _adamw(w[n], g_all[n], m[n], v[n], "adamw_" + n)
        shape = env[n].shape
        out_g.append(g_all[n].reshape(shape))
        out_d.append(d.reshape(shape))
        out_m.append(nm.reshape(shape))
        out_v.append(nv.reshape(shape))
    return (loss, grad_x[None], *out_g, *out_d, *out_m, *out_v)
```

```python
import functools

import jax
import jax.numpy as jnp
import numpy as np
from jax import lax
from jax.experimental import pallas as pl
from jax.experimental.pallas import tpu as pltpu

F32 = jnp.float32
BF16 = jnp.bfloat16

N_DEV = 8
N_CHIP = 4
AXES = ("x", "y", "c")
MESH = pl.DeviceIdType.MESH

D_MODEL = 2048
MLA_HEADS = 8
QK_NOPE = 128
QK_ROPE = 64
V_HEAD = 128
Q_LORA = 512
KV_LORA = 256
ROPE_THETA = 10000.0
HEAD_PAD = 256
DIL_PATTERNS = ((128, 1), (512, 4), (2048, 16))
DIL_GROUPS = 3
DIL_HG = 4
DIL_HEADS = 12
DIL_HD = 128
DIL_BLK = 128
DIL_QKV = DIL_HEADS * DIL_HD
DIL_OUT = DIL_HG * DIL_HD
ALIBI_MAX_BIAS = 8.0
D_FF = 5504
D_FF_PAD = 5632
NORM_EPS = 1e-6
LAT = Q_LORA + KV_LORA + QK_ROPE
LAT_PAD = 896
D_IN = LAT + 3 * DIL_QKV + 2 * D_MODEL
NEG = -1e30

ADAM_LR = 0.001
ADAM_B1 = 0.9
ADAM_B2 = 0.999
ADAM_EPS = 1e-08
ADAM_WD = 0.01
ADAM_STEP = 10

SMALL_ROWS = 56
SMALL_COLS = 1024

IN_ROWS = 1192
IN_ROWS_PAD = 1200
FF_GROUP = D_FF // N_DEV
FF_GROUP_PAD = D_FF_PAD // N_DEV
CONV_SHARD = (3, 1376)

NT = (((1,), (1,)), ((), ()))
TN = (((0,), (0,)), ((), ()))


def _dot(a, b, dims=(((1,), (0,)), ((), ()))):
    return lax.dot_general(a, b, dims, preferred_element_type=F32)


def _mm(a, b, mode, out_dtype, tm, tn, tk, name, bias=None, act=None, res=None, b_koff=0, a_halves=1):
    H = a_halves
    if mode == "nn":
        (M, K), (K2, N) = (a.shape[0] // H, a.shape[1] * H), b.shape
        assert (b_koff + 1) * K <= K2, (name, a.shape, b.shape)
        koff, K2 = b_koff * (K // tk), K
        kper, mrows = a.shape[1] // tk, M // tm
        a_spec = pl.BlockSpec((tm, tk), lambda i, j, k: (i + (k // kper) * mrows, k % kper))
        b_spec = pl.BlockSpec((tk, tn), lambda i, j, k: (k + koff, j))
        dims = (((1,), (0,)), ((), ()))
    elif mode == "nt":
        (M, K), (N, K2) = a.shape, b.shape
        a_spec = pl.BlockSpec((tm, tk), lambda i, j, k: (i, k))
        b_spec = pl.BlockSpec((tn, tk), lambda i, j, k: (j, k))
        dims = NT
    else:
        (K, M), (K2, N) = (a.shape[0] // H, a.shape[1] * H), b.shape
        mper, krows = a.shape[1] // tm, K // tk
        a_spec = pl.BlockSpec((tk, tm), lambda i, j, k: (k + (i // mper) * krows, i % mper))
        b_spec = pl.BlockSpec((tk, tn), lambda i, j, k: (k, j))
        dims = TN
    assert K == K2 and M % tm == 0 and N % tn == 0 and K % tk == 0, (name, a.shape, b.shape)
    nk = K // tk
    has_bias, has_res = bias is not None, res is not None

    def body(*refs):
        refs = list(refs)
        a_ref, b_ref = refs[0], refs[1]
        pos = 2
        bias_ref = res_ref = None
        if has_bias:
            bias_ref = refs[pos]
            pos += 1
        if has_res:
            res_ref = refs[pos]
            pos += 1
        o_ref = refs[pos]
        p = _dot(a_ref[...].astype(BF16), b_ref[...].astype(BF16), dims)

        def finish(acc):
            if has_bias:
                acc = acc + bias_ref[...]
            if act == "sigmoid":
                acc = jax.nn.sigmoid(acc)
            if has_res:
                acc = res_ref[...] + acc
            o_ref[...] = acc.astype(o_ref.dtype)

        if nk == 1:
            finish(p)
        else:
            acc_ref = refs[pos + 1]
            k = pl.program_id(2)

            @pl.when(k == 0)
            def _():
                acc_ref[...] = p

            @pl.when(k != 0)
            def _():
                acc_ref[...] += p

            @pl.when(k == nk - 1)
            def _():
                finish(acc_ref[...])

    in_specs = [a_spec, b_spec]
    args = [a, b]
    if has_bias:
        in_specs.append(pl.BlockSpec((1, tn), lambda i, j, k: (0, j)))
        args.append(bias)
    if has_res:
        in_specs.append(pl.BlockSpec((tm, tn), lambda i, j, k: (i, j)))
        args.append(res)
    return pl.pallas_call(
        body,
        name=name,
        grid=(M // tm, N // tn, nk),
        in_specs=in_specs,
        out_specs=pl.BlockSpec((tm, tn), lambda i, j, k: (i, j)),
        out_shape=jax.ShapeDtypeStruct((M, N), out_dtype),
        scratch_shapes=[pltpu.VMEM((tm, tn), F32)] if nk > 1 else [],
        compiler_params=pltpu.CompilerParams(dimension_semantics=("parallel", "parallel", "arbitrary")),
    )(*args)


def _rstd(x):
    return lax.rsqrt(jnp.mean(x * x, axis=-1, keepdims=True) + NORM_EPS)


def _rms_bwd_math(dy, x, g):
    r = _rstd(x)
    xh = x * r
    dg = jnp.sum(dy * xh, axis=0, keepdims=True)
    dxh = dy * g
    dx = r * (dxh - xh * jnp.mean(dxh * xh, axis=-1, keepdims=True))
    return dx, dg


def _rms_fwd(x, g, name, tr=256):
    S, D = x.shape

    def body(x_ref, g_ref, o_ref):
        xv = x_ref[...]
        o_ref[...] = ((xv * _rstd(xv)) * g_ref[...]).astype(o_ref.dtype)

    return pl.pallas_call(
        body,
        name=name,
        grid=(S // tr,),
        in_specs=[pl.BlockSpec((tr, D), lambda i: (i, 0)), pl.BlockSpec((1, D), lambda i: (0, 0))],
        out_specs=pl.BlockSpec((tr, D), lambda i: (i, 0)),
        out_shape=jax.ShapeDtypeStruct((S, D), BF16),
        compiler_params=pltpu.CompilerParams(dimension_semantics=("parallel",)),
    )(x, g)


def _rms_bwd(dy, x, g, res, name, tr=256):
    S, D = x.shape

    def body(dy_ref, x_ref, g_ref, res_ref, dx_ref, dxb_ref, dg_ref):
        dx, dg = _rms_bwd_math(dy_ref[...], x_ref[...], g_ref[...])
        dx = dx + res_ref[...]
        dx_ref[...] = dx
        dxb_ref[...] = dx.astype(BF16)

        @pl.when(pl.program_id(0) == 0)
        def _():
            dg_ref[...] = dg

        @pl.when(pl.program_id(0) != 0)
        def _():
            dg_ref[...] += dg

    row = pl.BlockSpec((tr, D), lambda i: (i, 0))
    vec = pl.BlockSpec((1, D), lambda i: (0, 0))
    return pl.pallas_call(
        body,
        name=name,
        grid=(S // tr,),
        in_specs=[row, row, vec, row],
        out_specs=[row, row, vec],
        out_shape=[jax.ShapeDtypeStruct((S, D), F32), jax.ShapeDtypeStruct((S, D), BF16), jax.ShapeDtypeStruct((1, D), F32)],
        compiler_params=pltpu.CompilerParams(dimension_semantics=("arbitrary",)),
    )(dy, x, g, res)


def _final_loss(x2, g, tgt, name, tr=256):
    S, D = x2.shape

    def body(x_ref, g_ref, t_ref, loss_ref, dx_ref, dxb_ref, dg_ref):
        xv, gv = x_ref[...], g_ref[...]
        y = (xv * _rstd(xv)) * gv
        e = y - t_ref[...]
        part = 0.5 * jnp.sum(jnp.mean(e * e, axis=-1, keepdims=True), axis=0, keepdims=True)
        dx, dg = _rms_bwd_math(e * (1.0 / D), xv, gv)
        dx_ref[...] = dx
        dxb_ref[...] = dx.astype(BF16)
        part = jnp.broadcast_to(part, (1, 128))

        @pl.when(pl.program_id(0) == 0)
        def _():
            dg_ref[...] = dg
            loss_ref[...] = part

        @pl.when(pl.program_id(0) != 0)
        def _():
            dg_ref[...] += dg
            loss_ref[...] += part

    row = pl.BlockSpec((tr, D), lambda i: (i, 0))
    vec = pl.BlockSpec((1, D), lambda i: (0, 0))
    return pl.pallas_call(
        body,
        name=name,
        grid=(S // tr,),
        in_specs=[row, vec, row],
        out_specs=[pl.BlockSpec((1, 128), lambda i: (0, 0)), row, row, vec],
        out_shape=[
            jax.ShapeDtypeStruct((1, 128), F32),
            jax.ShapeDtypeStruct((S, D), F32),
            jax.ShapeDtypeStruct((S, D), BF16),
            jax.ShapeDtypeStruct((1, D), F32),
        ],
        compiler_params=pltpu.CompilerParams(dimension_semantics=("arbitrary",)),
    )(x2, g, tgt)


def _rope_tables(S):
    pos = jnp.arange(S, dtype=F32)
    inv_freq = ROPE_THETA ** (-jnp.arange(0, QK_ROPE, 2, dtype=F32) / QK_ROPE)
    ang = pos[:, None] * inv_freq[None, :]
    cos, sin = jnp.cos(ang), jnp.sin(ang)
    zero = jnp.zeros((S, 128 - QK_ROPE), F32)
    return jnp.concatenate([cos, cos, zero], axis=1), jnp.concatenate([-sin, sin, zero], axis=1)


def _rope_tile(x, cos_t, sin_t):
    lane = lax.broadcasted_iota(jnp.int32, x.shape, 1)
    partner = jnp.where(lane < QK_ROPE // 2, pltpu.roll(x, 128 - QK_ROPE // 2, 1), pltpu.roll(x, QK_ROPE // 2, 1))
    return x * cos_t + partner * sin_t


def _mla_prep1(lat, gq, gkv, cos_t, sin_t, name, tr=256):
    S = lat.shape[0]

    def body(lat_ref, gq_ref, gkv_ref, cos_ref, sin_ref, cq_ref, ckv_ref, kpe_ref):
        cq = lat_ref[:, :Q_LORA]
        ckv = lat_ref[:, Q_LORA : Q_LORA + KV_LORA]
        cq_ref[...] = ((cq * _rstd(cq)) * gq_ref[...]).astype(BF16)
        ckv_ref[...] = ((ckv * _rstd(ckv)) * gkv_ref[...]).astype(BF16)
        kpe_ref[...] = _rope_tile(lat_ref[:, Q_LORA + KV_LORA :], cos_ref[...], sin_ref[...]).astype(BF16)

    def row(n):
        return pl.BlockSpec((tr, n), lambda i: (i, 0))

    def vec(n):
        return pl.BlockSpec((1, n), lambda i: (0, 0))

    return pl.pallas_call(
        body,
        name=name,
        grid=(S // tr,),
        in_specs=[row(LAT_PAD), vec(Q_LORA), vec(KV_LORA), row(128), row(128)],
        out_specs=[row(Q_LORA), row(KV_LORA), row(128)],
        out_shape=[
            jax.ShapeDtypeStruct((S, Q_LORA), BF16),
            jax.ShapeDtypeStruct((S, KV_LORA), BF16),
            jax.ShapeDtypeStruct((S, 128), BF16),
        ],
        compiler_params=pltpu.CompilerParams(dimension_semantics=("parallel",)),
    )(lat, gq, gkv, cos_t, sin_t)


def _mla_prep2(q_raw, kv, kpe, cos_t, sin_t, name, tr=256):
    S = q_raw.shape[0]
    W = MLA_HEADS * HEAD_PAD

    def body(q_ref, kv_ref, kpe_ref, cos_ref, sin_ref, qa_ref, ka_ref):
        cos_v, sin_v, kpe_v = cos_ref[...], sin_ref[...], kpe_ref[...]
        for h in range(MLA_HEADS):
            lo = h * HEAD_PAD
            qa_ref[:, lo : lo + 128] = q_ref[:, lo : lo + 128].astype(BF16)
            qa_ref[:, lo + 128 : lo + 256] = _rope_tile(q_ref[:, lo + 128 : lo + 256], cos_v, sin_v).astype(BF16)
            ka_ref[:, lo : lo + 128] = kv_ref[:, lo : lo + 128]
            ka_ref[:, lo + 128 : lo + 256] = kpe_v

    def row(n):
        return pl.BlockSpec((tr, n), lambda i: (i, 0))

    return pl.pallas_call(
        body,
        name=name,
        grid=(S // tr,),
        in_specs=[row(W), row(W), row(128), row(128), row(128)],
        out_specs=[row(W), row(W)],
        out_shape=[jax.ShapeDtypeStruct((S, W), BF16), jax.ShapeDtypeStruct((S, W), BF16)],
        compiler_params=pltpu.CompilerParams(dimension_semantics=("parallel",)),
    )(q_raw, kv, kpe, cos_t, sin_t)


def _mla_post(dq_att, dk_att, dv, cos_t, sin_t, name, tr=256):
    S = dq_att.shape[0]
    W = MLA_HEADS * HEAD_PAD

    def body(dq_ref, dk_ref, dv_ref, cos_ref, sin_ref, dqr_ref, dkv_ref, dkpe_ref):
        cos_v, nsin_v = cos_ref[...], -sin_ref[...]
        kpe = jnp.zeros((tr, 128), F32)
        for h in range(MLA_HEADS):
            lo = h * HEAD_PAD
            dqr_ref[:, lo : lo + 128] = dq_ref[:, lo : lo + 128].astype(BF16)
            dqr_ref[:, lo + 128 : lo + 256] = _rope_tile(dq_ref[:, lo + 128 : lo + 256], cos_v, nsin_v).astype(BF16)
            dkv_ref[:, lo : lo + 128] = dk_ref[:, lo : lo + 128].astype(BF16)
            dkv_ref[:, lo + 128 : lo + 256] = dv_ref[:, h * 128 : (h + 1) * 128].astype(BF16)
            kpe = kpe + dk_ref[:, lo + 128 : lo + 256]
        dkpe_ref[...] = _rope_tile(kpe, cos_v, nsin_v)

    def row(n):
        return pl.BlockSpec((tr, n), lambda i: (i, 0))

    return pl.pallas_call(
        body,
        name=name,
        grid=(S // tr,),
        in_specs=[row(W), row(W), row(MLA_HEADS * V_HEAD), row(128), row(128)],
        out_specs=[row(W), row(W), row(128)],
        out_shape=[jax.ShapeDtypeStruct((S, W), BF16), jax.ShapeDtypeStruct((S, W), BF16), jax.ShapeDtypeStruct((S, 128), F32)],
        compiler_params=pltpu.CompilerParams(dimension_semantics=("parallel",)),
    )(dq_att, dk_att, dv, cos_t, sin_t)


def _lat_bwd(dcqn, dckvn, dkpe, lat, gq, gkv, name, tr=256):
    S = lat.shape[0]

    def body(dcq_ref, dckv_ref, dkpe_ref, lat_ref, gq_ref, gkv_ref, dlat_ref, dgq_ref, dgkv_ref):
        dq, dgq = _rms_bwd_math(dcq_ref[...], lat_ref[:, :Q_LORA], gq_ref[...])
        dkv, dgkv = _rms_bwd_math(dckv_ref[...], lat_ref[:, Q_LORA : Q_LORA + KV_LORA], gkv_ref[...])
        dlat_ref[:, :Q_LORA] = dq.astype(BF16)
        dlat_ref[:, Q_LORA : Q_LORA + KV_LORA] = dkv.astype(BF16)
        dlat_ref[:, Q_LORA + KV_LORA :] = dkpe_ref[...].astype(BF16)

        @pl.when(pl.program_id(0) == 0)
        def _():
            dgq_ref[...] = dgq
            dgkv_ref[...] = dgkv

        @pl.when(pl.program_id(0) != 0)
        def _():
            dgq_ref[...] += dgq
            dgkv_ref[...] += dgkv

    def row(n):
        return pl.BlockSpec((tr, n), lambda i: (i, 0))

    def vec(n):
        return pl.BlockSpec((1, n), lambda i: (0, 0))

    return pl.pallas_call(
        body,
        name=name,
        grid=(S // tr,),
        in_specs=[row(Q_LORA), row(KV_LORA), row(128), row(LAT_PAD), vec(Q_LORA), vec(KV_LORA)],
        out_specs=[row(LAT_PAD), vec(Q_LORA), vec(KV_LORA)],
        out_shape=[
            jax.ShapeDtypeStruct((S, LAT_PAD), BF16),
            jax.ShapeDtypeStruct((1, Q_LORA), F32),
            jax.ShapeDtypeStruct((1, KV_LORA), F32),
        ],
        compiler_params=pltpu.CompilerParams(dimension_semantics=("arbitrary",)),
    )(dcqn, dckvn, dkpe, lat, gq, gkv)


MLA_SCALE = (QK_NOPE + QK_ROPE) ** -0.5
LOG2E = 1.4426950408889634
MLA_C2 = MLA_SCALE * LOG2E
FLASH_T = 1024


def _causal_pairs(n, by_key):
    pairs = [(i, j) for j in range(n) for i in range(j, n)] if by_key else [(i, j) for i in range(n) for j in range(i + 1)]
    return jnp.asarray([p[0] for p in pairs], jnp.int32), jnp.asarray([p[1] for p in pairs], jnp.int32)


def _lanes(x, n):
    return jnp.tile(x, (1, n // 128))


def _flash_grid(npairs, in_specs, out_specs, scratch):
    return pltpu.PrefetchScalarGridSpec(
        num_scalar_prefetch=2, grid=(MLA_HEADS, npairs), in_specs=in_specs, out_specs=out_specs, scratch_shapes=scratch
    )


def _flash2_fwd(q_att, k_att, kv, name, t=FLASH_T):
    S = q_att.shape[0]
    qi_tab, kj_tab = _causal_pairs(S // t, by_key=False)

    def body(qi_ref, kj_ref, q_ref, k_ref, v_ref, o_ref, lse_ref, m_sc, l_sc, acc_sc):
        step = pl.program_id(1)
        qi, kj = qi_ref[step], kj_ref[step]

        @pl.when(kj == 0)
        def _():
            m_sc[...] = jnp.full((t, 128), NEG, F32)
            l_sc[...] = jnp.zeros((t, 128), F32)
            acc_sc[...] = jnp.zeros((t, V_HEAD), F32)

        def update(s):
            m_prev = m_sc[...]
            m_new = jnp.maximum(m_prev, jnp.max(s, axis=1, keepdims=True))
            p = jnp.exp2((s - _lanes(m_new, t)) * MLA_C2)
            alpha = jnp.exp2((m_prev - m_new) * MLA_C2)
            l_sc[...] = alpha * l_sc[...] + jnp.sum(p, axis=1, keepdims=True)
            acc_sc[...] = alpha * acc_sc[...] + _dot(p.astype(BF16), v_ref[...])
            m_sc[...] = m_new

        @pl.when(kj < qi)
        def _():
            update(_dot(q_ref[...], k_ref[...], NT))

        @pl.when(kj == qi)
        def _():
            s = _dot(q_ref[...], k_ref[...], NT)
            rows = lax.broadcasted_iota(jnp.int32, s.shape, 0)
            cols = lax.broadcasted_iota(jnp.int32, s.shape, 1)
            update(jnp.where(cols <= rows, s, NEG))
            l = l_sc[...]
            o_ref[...] = acc_sc[...] / l
            lse_ref[0] = m_sc[...] * MLA_SCALE + jnp.log(l)

    return pl.pallas_call(
        body,
        name=name,
        grid_spec=_flash_grid(
            qi_tab.shape[0],
            [
                pl.BlockSpec((t, HEAD_PAD), lambda h, p, qi, kj: (qi[p], h)),
                pl.BlockSpec((t, HEAD_PAD), lambda h, p, qi, kj: (kj[p], h)),
                pl.BlockSpec((t, V_HEAD), lambda h, p, qi, kj: (kj[p], 2 * h + 1)),
            ],
            [
                pl.BlockSpec((t, V_HEAD), lambda h, p, qi, kj: (qi[p], h)),
                pl.BlockSpec((1, t, 128), lambda h, p, qi, kj: (h, qi[p], 0)),
            ],
            [pltpu.VMEM((t, 128), F32), pltpu.VMEM((t, 128), F32), pltpu.VMEM((t, V_HEAD), F32)],
        ),
        out_shape=[jax.ShapeDtypeStruct((S, MLA_HEADS * V_HEAD), F32), jax.ShapeDtypeStruct((MLA_HEADS, S, 128), F32)],
        compiler_params=pltpu.CompilerParams(dimension_semantics=("parallel", "arbitrary")),
    )(qi_tab, kj_tab, q_att, k_att, kv)


def _flash2_dq(q_att, k_att, kv, o, do, lse, name, t=FLASH_T):
    S = q_att.shape[0]
    qi_tab, kj_tab = _causal_pairs(S // t, by_key=False)

    def body(qi_ref, kj_ref, q_ref, k_ref, v_ref, o_ref, do_ref, lse_ref, dq_ref, dl_ref, acc_sc):
        step = pl.program_id(1)
        qi, kj = qi_ref[step], kj_ref[step]

        @pl.when(kj == 0)
        def _():
            acc_sc[...] = jnp.zeros((t, HEAD_PAD), F32)
            dl = jnp.sum(do_ref[...].astype(F32) * o_ref[...], axis=1, keepdims=True)
            dl_ref[0] = jnp.broadcast_to(dl, (t, 128))

        def update(s):
            k = k_ref[...]
            p = jnp.exp2(s * MLA_C2 - _lanes(lse_ref[0] * LOG2E, t))
            dp = _dot(do_ref[...], v_ref[...], NT)
            ds = p * (dp - _lanes(dl_ref[0], t))
            acc_sc[...] += _dot(ds.astype(BF16), k)

        @pl.when(kj < qi)
        def _():
            update(_dot(q_ref[...], k_ref[...], NT))

        @pl.when(kj == qi)
        def _():
            s = _dot(q_ref[...], k_ref[...], NT)
            rows = lax.broadcasted_iota(jnp.int32, s.shape, 0)
            cols = lax.broadcasted_iota(jnp.int32, s.shape, 1)
            update(jnp.where(cols <= rows, s, NEG))
            dq_ref[...] = acc_sc[...] * MLA_SCALE

    qrow = lambda h, p, qi, kj: (qi[p], h)
    stat = pl.BlockSpec((1, t, 128), lambda h, p, qi, kj: (h, qi[p], 0))
    return pl.pallas_call(
        body,
        name=name,
        grid_spec=_flash_grid(
            qi_tab.shape[0],
            [
                pl.BlockSpec((t, HEAD_PAD), qrow),
                pl.BlockSpec((t, HEAD_PAD), lambda h, p, qi, kj: (kj[p], h)),
                pl.BlockSpec((t, V_HEAD), lambda h, p, qi, kj: (kj[p], 2 * h + 1)),
                pl.BlockSpec((t, V_HEAD), qrow),
                pl.BlockSpec((t, V_HEAD), qrow),
                stat,
            ],
            [pl.BlockSpec((t, HEAD_PAD), qrow), stat],
            [pltpu.VMEM((t, HEAD_PAD), F32)],
        ),
        out_shape=[jax.ShapeDtypeStruct((S, MLA_HEADS * HEAD_PAD), F32), jax.ShapeDtypeStruct((MLA_HEADS, S, 128), F32)],
        compiler_params=pltpu.CompilerParams(dimension_semantics=("parallel", "arbitrary")),
    )(qi_tab, kj_tab, q_att, k_att, kv, o, do, lse)


def _flash2_dkv(q_att, k_att, kv, do, lse_row, delta_row, name, t=FLASH_T):
    S = q_att.shape[0]
    n = S // t
    qi_tab, kj_tab = _causal_pairs(n, by_key=True)

    def body(qi_ref, kj_ref, q_ref, k_ref, v_ref, do_ref, lse_ref, dl_ref, dk_ref, dv_ref, dk_sc, dv_sc):
        step = pl.program_id(1)
        qi, kj = qi_ref[step], kj_ref[step]

        def update(st):
            q, do_v = q_ref[...], do_ref[...]
            pt = jnp.exp2(st * MLA_C2 - lse_ref[0] * LOG2E)
            dv_sc[...] += _dot(pt.astype(BF16), do_v)
            dpt = _dot(v_ref[...], do_v, NT)
            dst = pt * (dpt - dl_ref[0])
            dk_sc[...] += _dot(dst.astype(BF16), q)

        @pl.when(qi == kj)
        def _():
            dk_sc[...] = jnp.zeros((t, HEAD_PAD), F32)
            dv_sc[...] = jnp.zeros((t, V_HEAD), F32)
            st = _dot(k_ref[...], q_ref[...], NT)
            keys = lax.broadcasted_iota(jnp.int32, st.shape, 0)
            qs = lax.broadcasted_iota(jnp.int32, st.shape, 1)
            update(jnp.where(keys <= qs, st, NEG))

        @pl.when(qi > kj)
        def _():
            update(_dot(k_ref[...], q_ref[...], NT))

        @pl.when(qi == n - 1)
        def _():
            dk_ref[...] = dk_sc[...] * MLA_SCALE
            dv_ref[...] = dv_sc[...]

    qrow = lambda h, p, qi, kj: (qi[p], h)
    krow = lambda h, p, qi, kj: (kj[p], h)
    stat = pl.BlockSpec((1, 1, t), lambda h, p, qi, kj: (h, 0, qi[p]))
    return pl.pallas_call(
        body,
        name=name,
        grid_spec=_flash_grid(
            qi_tab.shape[0],
            [
                pl.BlockSpec((t, HEAD_PAD), qrow),
                pl.BlockSpec((t, HEAD_PAD), krow),
                pl.BlockSpec((t, V_HEAD), lambda h, p, qi, kj: (kj[p], 2 * h + 1)),
                pl.BlockSpec((t, V_HEAD), qrow),
                stat,
                stat,
            ],
            [pl.BlockSpec((t, HEAD_PAD), krow), pl.BlockSpec((t, V_HEAD), krow)],
            [pltpu.VMEM((t, HEAD_PAD), F32), pltpu.VMEM((t, V_HEAD), F32)],
        ),
        out_shape=[jax.ShapeDtypeStruct((S, MLA_HEADS * HEAD_PAD), F32), jax.ShapeDtypeStruct((S, MLA_HEADS * V_HEAD), F32)],
        compiler_params=pltpu.CompilerParams(dimension_semantics=("parallel", "arbitrary")),
    )(qi_tab, kj_tab, q_att, k_att, kv, do, lse_row, delta_row)


DIL_SCALE = DIL_HD**-0.5


def _dil_bias():
    slopes = 2.0 ** (-ALIBI_MAX_BIAS * np.arange(1, DIL_HEADS + 1, dtype=np.float64) / DIL_HEADS)
    slopes = slopes.astype(np.float32).reshape(DIL_GROUPS, DIL_HG)
    p = np.arange(DIL_BLK)[:, None]
    kidx = np.arange(2 * DIL_BLK)[None, :]
    j = p + DIL_BLK - kidx
    out = np.zeros((DIL_GROUPS, DIL_HG, DIL_BLK, 2 * DIL_BLK), np.float32)
    for g, (window, dil) in enumerate(DIL_PATTERNS):
        valid = (j >= 0) & (j <= window // dil)
        for h in range(DIL_HG):
            alibi = -slopes[g, h] * (dil * j).astype(np.float32)
            out[g, h] = np.where(valid, alibi, np.float32(NEG))
    return jnp.asarray(out)


DIL_UNROLL = 4


def _unrolled_loop(lo, hi, fn, unroll=DIL_UNROLL):
    groups = (hi - lo) // unroll
    done = lo
    if groups > 1:

        def step(i, carry):
            for u in range(unroll):
                fn(lo + i * unroll + u)
            return carry

        lax.fori_loop(0, groups, step, 0)
        done = lo + groups * unroll
    for n in range(done, hi):
        fn(n)


def _dil_rows(r, n, count, dil):
    if dil == 1:
        if isinstance(n, int):
            return slice(n * DIL_BLK, (n + count) * DIL_BLK)
        return pl.ds(pl.multiple_of(n * DIL_BLK, DIL_BLK), count * DIL_BLK)
    return pl.ds(n * DIL_BLK * dil + r, count * DIL_BLK, stride=dil)


def _dil_each_block(S, dil, block):
    nb = S // dil // DIL_BLK
    if dil == 1:
        block(0, 0, True)
        _unrolled_loop(1, nb, lambda n: block(0, n, False))
    else:
        for r in range(dil):
            for n in range(nb):
                block(r, n, n == 0)


def _dil_col(g, part, h):
    return (g * 3 + part) * DIL_HG + h


def _dil_fwd_group(dqkv, bias_g, g, dil, name):
    S = dqkv.shape[0]

    def body(bias_ref, q_ref, k_ref, v_ref, o_ref, lse_ref):
        def block(r, n, first):
            cur = _dil_rows(r, n, 1, dil)
            both = cur if first else _dil_rows(r, n - 1, 2, dil)
            b = bias_ref[0][:, DIL_BLK:] if first else bias_ref[0]
            q, kk, vv = q_ref[cur, :].astype(BF16), k_ref[both, :].astype(BF16), v_ref[both, :].astype(BF16)
            s = _dot(q, kk, NT) * DIL_SCALE + b
            m = jnp.max(s, axis=1, keepdims=True)
            e = jnp.exp(s - m)
            l = jnp.sum(e, axis=1, keepdims=True)
            p = e * (1.0 / l)
            o_ref[cur, :] = _dot(p.astype(BF16), vv)
            lse_ref[cur, :] = jnp.broadcast_to(m + jnp.log(l), (DIL_BLK, 128))

        _dil_each_block(S, dil, block)

    def col(part):
        return pl.BlockSpec((S, DIL_HD), lambda h: (0, _dil_col(g, part, h)))

    out = pl.BlockSpec((S, DIL_HD), lambda h: (0, h))
    return pl.pallas_call(
        body,
        name=name,
        grid=(DIL_HG,),
        in_specs=[pl.BlockSpec((1, DIL_BLK, 2 * DIL_BLK), lambda h: (h, 0, 0)), col(0), col(1), col(2)],
        out_specs=[out, out],
        out_shape=[jax.ShapeDtypeStruct((S, DIL_OUT), F32), jax.ShapeDtypeStruct((S, DIL_OUT), F32)],
        compiler_params=pltpu.CompilerParams(dimension_semantics=("parallel",)),
    )(bias_g, dqkv, dqkv, dqkv)


def _dil_combine(os_, ls_, name, tr=512):
    S = os_[0].shape[0]

    def body(o0, o1, o2, l0, l1, l2, out_ref, lse_ref):
        a, b, c = l0[...], l1[...], l2[...]
        m = jnp.maximum(jnp.maximum(a, b), c)
        ea, eb, ec = jnp.exp(a - m), jnp.exp(b - m), jnp.exp(c - m)
        den = ea + eb + ec
        inv = 1.0 / den
        out_ref[...] = (ea * inv) * o0[...] + (eb * inv) * o1[...] + (ec * inv) * o2[...]
        lse_ref[...] = m + jnp.log(den)

    row = pl.BlockSpec((tr, DIL_OUT), lambda i: (i, 0))
    return pl.pallas_call(
        body,
        name=name,
        grid=(S // tr,),
        in_specs=[row] * 6,
        out_specs=[row, row],
        out_shape=[jax.ShapeDtypeStruct((S, DIL_OUT), F32)] * 2,
        compiler_params=pltpu.CompilerParams(dimension_semantics=("parallel",)),
    )(*os_, *ls_)


def _dil_rowdot(dod, od, name, tr=512):
    S = dod.shape[0]

    def body(d_ref, o_ref, dd_ref):
        for h in range(DIL_HG):
            sl = slice(h * 128, (h + 1) * 128)
            sm = jnp.sum(d_ref[:, sl] * o_ref[:, sl], axis=1, keepdims=True)
            dd_ref[:, sl] = jnp.broadcast_to(sm, (tr, 128))

    row = pl.BlockSpec((tr, DIL_OUT), lambda i: (i, 0))
    return pl.pallas_call(
        body,
        name=name,
        grid=(S // tr,),
        in_specs=[row, row],
        out_specs=row,
        out_shape=jax.ShapeDtypeStruct((S, DIL_OUT), F32),
        compiler_params=pltpu.CompilerParams(dimension_semantics=("parallel",)),
    )(dod, od)


def _dil_bwd_group(dqkv, bias_g, dod, dd, lse, grads, g, dil, name):
    S = dqkv.shape[0]

    def body(bias_ref, q_ref, k_ref, v_ref, do_ref, dd_ref, lse_ref, _, out_ref):
        out_ref[1] = jnp.zeros((S, DIL_HD), F32)
        out_ref[2] = jnp.zeros((S, DIL_HD), F32)

        def block(r, n, first):
            cur = _dil_rows(r, n, 1, dil)
            both = cur if first else _dil_rows(r, n - 1, 2, dil)
            b = bias_ref[0][:, DIL_BLK:] if first else bias_ref[0]
            q, kk, vv = q_ref[cur, :].astype(BF16), k_ref[both, :].astype(BF16), v_ref[both, :].astype(BF16)
            do = do_ref[cur, :].astype(BF16)
            s = _dot(q, kk, NT) * DIL_SCALE + b
            p = jnp.exp(s - lse_ref[cur, 0:1])
            dp = _dot(do, vv, NT)
            ds = ((p * (dp - dd_ref[cur, 0:1])) * DIL_SCALE).astype(BF16)
            out_ref[0, cur, :] = _dot(ds, kk)
            out_ref[1, both, :] += _dot(ds, q, TN)
            out_ref[2, both, :] += _dot(p.astype(BF16), do, TN)

        _dil_each_block(S, dil, block)

    def col(part):
        return pl.BlockSpec((S, DIL_HD), lambda h: (0, _dil_col(g, part, h)))

    nat = pl.BlockSpec((S, DIL_HD), lambda h: (0, h))
    return pl.pallas_call(
        body,
        name=name,
        grid=(DIL_HG,),
        in_specs=[pl.BlockSpec((1, DIL_BLK, 2 * DIL_BLK), lambda h: (h, 0, 0)), col(0), col(1), col(2), nat, nat, nat, ANY],
        out_specs=pl.BlockSpec((3, S, DIL_HD), lambda h: (g, 0, h)),
        out_shape=jax.ShapeDtypeStruct(grads.shape, F32),
        input_output_aliases={7: 0},
        compiler_params=pltpu.CompilerParams(dimension_semantics=("parallel",)),
    )(bias_g, dqkv, dqkv, dqkv, dod, dd, lse, grads)


def _merge_fwd(gates, o_a, o_b, name, tr=256):
    S = o_a.shape[0]

    def body(ga_ref, gb_ref, oa_ref, ob_ref, m_ref):
        m_ref[...] = (ga_ref[...] * oa_ref[...] + gb_ref[...] * ob_ref[...]).astype(BF16)

    row = pl.BlockSpec((tr, D_MODEL), lambda i: (i, 0))
    return pl.pallas_call(
        body,
        name=name,
        grid=(S // tr,),
        in_specs=[row, pl.BlockSpec((tr, D_MODEL), lambda i: (i, 1)), row, row],
        out_specs=row,
        out_shape=jax.ShapeDtypeStruct((S, D_MODEL), BF16),
        compiler_params=pltpu.CompilerParams(dimension_semantics=("parallel",)),
    )(gates, gates, o_a, o_b)


def _merge_bwd(dmrg, gates, o_a, o_b, name, tr=256):
    S = o_a.shape[0]

    def body(dm_ref, ga_ref, gb_ref, oa_ref, ob_ref, doa_ref, dob_ref, dga_ref, dgb_ref, dba_ref, dbb_ref):
        dm, ga, gb = dm_ref[...], ga_ref[...], gb_ref[...]
        doa_ref[...] = (dm * ga).astype(BF16)
        dob_ref[...] = (dm * gb).astype(BF16)
        dga = (dm * oa_ref[...]) * (ga * (1.0 - ga))
        dgb = (dm * ob_ref[...]) * (gb * (1.0 - gb))
        dga_ref[...] = dga.astype(BF16)
        dgb_ref[...] = dgb.astype(BF16)
        sa = jnp.sum(dga, axis=0, keepdims=True)
        sb = jnp.sum(dgb, axis=0, keepdims=True)

        @pl.when(pl.program_id(0) == 0)
        def _():
            dba_ref[...] = sa
            dbb_ref[...] = sb

        @pl.when(pl.program_id(0) != 0)
        def _():
            dba_ref[...] += sa
            dbb_ref[...] += sb

    row = pl.BlockSpec((tr, D_MODEL), lambda i: (i, 0))
    row1 = pl.BlockSpec((tr, D_MODEL), lambda i: (i, 1))
    vec = pl.BlockSpec((1, D_MODEL), lambda i: (0, 0))
    outs = pl.pallas_call(
        body,
        name=name,
        grid=(S // tr,),
        in_specs=[row, row, row1, row, row],
        out_specs=[row, row, row, row, vec, vec],
        out_shape=[jax.ShapeDtypeStruct((S, D_MODEL), BF16)] * 4 + [jax.ShapeDtypeStruct((1, D_MODEL), F32)] * 2,
        compiler_params=pltpu.CompilerParams(dimension_semantics=("arbitrary",)),
    )(dmrg, gates, gates, o_a, o_b)
    return outs


CONV_TR = 512
CONV_TC = 512
N_FFC = D_FF_PAD // CONV_TC


def _shift_down(x, edge_rows, k):
    out = pltpu.roll(x, k, 0)
    row = lax.broadcasted_iota(jnp.int32, x.shape, 0)
    for i in range(k):
        out = jnp.where(row == i, edge_rows[i], out)
    return out


def _conv_taps(x, halo_ref, live, w_ref, b_ref):
    h6, h7 = halo_ref[6:7, :] * live, halo_ref[7:8, :] * live
    s1 = _shift_down(x, [h7], 1)
    s2 = _shift_down(x, [h6, h7], 2)
    u = ((b_ref[...] + w_ref[0:1, :] * s2) + w_ref[1:2, :] * s1) + w_ref[2:3, :] * x
    return u, s1, s2


def _prev_halo(tr):
    return lambda i, j: (jnp.maximum(i * (tr // 8) - 1, 0), j)


def _ffn_fwd(u0, cw, cb, name):
    S = u0.shape[0]
    tr, tc = CONV_TR, CONV_TC

    def body(up_ref, gt_ref, hup_ref, hgt_ref, wu_ref, wg_ref, bu_ref, bg_ref, a_ref):
        live = (pl.program_id(0) > 0).astype(F32)
        up, _, _ = _conv_taps(up_ref[...], hup_ref, live, wu_ref, bu_ref)
        gt, _, _ = _conv_taps(gt_ref[...], hgt_ref, live, wg_ref, bg_ref)
        a_ref[...] = ((gt * jax.nn.sigmoid(gt)) * up).astype(BF16)

    return pl.pallas_call(
        body,
        name=name,
        grid=(S // tr, N_FFC),
        in_specs=[
            pl.BlockSpec((tr, tc), lambda i, j: (i, j)),
            pl.BlockSpec((tr, tc), lambda i, j: (i, j + N_FFC)),
            pl.BlockSpec((8, tc), _prev_halo(tr)),
            pl.BlockSpec((8, tc), lambda i, j: (jnp.maximum(i * (tr // 8) - 1, 0), j + N_FFC)),
            pl.BlockSpec((8, tc), lambda i, j: (0, j)),
            pl.BlockSpec((8, tc), lambda i, j: (0, j + N_FFC)),
            pl.BlockSpec((1, tc), lambda i, j: (0, j)),
            pl.BlockSpec((1, tc), lambda i, j: (0, j + N_FFC)),
        ],
        out_specs=pl.BlockSpec((tr, tc), lambda i, j: (i, j)),
        out_shape=jax.ShapeDtypeStruct((S, D_FF_PAD), BF16),
        compiler_params=pltpu.CompilerParams(dimension_semantics=("parallel", "parallel")),
    )(u0, u0, u0, u0, cw, cw, cb, cb)


def _ffn_bwd(u0, da, cw, cb, name):
    S = u0.shape[0]
    tr, tc = CONV_TR, CONV_TC
    nrow, te = S // tr, tr + 8

    def body(up_ref, gt_ref, hup_ref, hgt_ref, nup_ref, ngt_ref, da_ref, nda_ref, wu_ref, wg_ref, bu_ref, bg_ref, du0_ref, dcw_ref, dcb_ref):
        i = pl.program_id(1)
        prev_live = (i > 0).astype(F32)
        next_live = (i < nrow - 1).astype(F32)

        def conv(x_ref, nx_ref, h_ref, w_ref, b_ref):
            x = jnp.concatenate([x_ref[...], nx_ref[...] * next_live], axis=0)
            h6, h7 = h_ref[6:7, :] * prev_live, h_ref[7:8, :] * prev_live
            s1 = _shift_down(x, [h7], 1)
            s2 = _shift_down(x, [h6, h7], 2)
            return ((b_ref[...] + w_ref[0:1, :] * s2) + w_ref[1:2, :] * s1) + w_ref[2:3, :] * x, x, s1, s2

        up, xu0, xu1, xu2 = conv(up_ref, nup_ref, hup_ref, wu_ref, bu_ref)
        gt, xg0, xg1, xg2 = conv(gt_ref, ngt_ref, hgt_ref, wg_ref, bg_ref)
        da_v = jnp.concatenate([da_ref[...], nda_ref[...] * next_live], axis=0)
        sg = jax.nn.sigmoid(gt)
        d_up = da_v * (gt * sg)
        d_gt = (da_v * up) * (sg * (1.0 + gt * (1.0 - sg)))
        tap = lax.broadcasted_iota(jnp.int32, (8, tc), 0)

        def finish(half, du, x0, x1, x2, w_ref):
            n1 = pltpu.roll(du, te - 1, 0)
            n2 = pltpu.roll(du, te - 2, 0)
            du0 = (w_ref[2:3, :] * du + w_ref[1:2, :] * n1) + w_ref[0:1, :] * n2
            du0_ref[half] = du0[:tr].astype(BF16)
            d = du[:tr]
            dcw = jnp.where(
                tap == 0,
                jnp.sum(d * x2[:tr], axis=0, keepdims=True),
                jnp.where(tap == 1, jnp.sum(d * x1[:tr], axis=0, keepdims=True), jnp.where(tap == 2, jnp.sum(d * x0[:tr], axis=0, keepdims=True), 0.0)),
            )
            dcb = jnp.sum(d, axis=0, keepdims=True)

            @pl.when(i == 0)
            def _():
                dcw_ref[half] = dcw
                dcb_ref[half] = dcb

            @pl.when(i != 0)
            def _():
                dcw_ref[half] += dcw
                dcb_ref[half] += dcb

        finish(0, d_up, xu0, xu1, xu2, wu_ref)
        finish(1, d_gt, xg0, xg1, xg2, wg_ref)

    def prev8(off):
        return pl.BlockSpec((8, tc), lambda j, i: (jnp.maximum(i * (tr // 8) - 1, 0), j + off))

    def next8(off):
        return pl.BlockSpec((8, tc), lambda j, i: (jnp.minimum((i + 1) * (tr // 8), S // 8 - 1), j + off))

    return pl.pallas_call(
        body,
        name=name,
        grid=(N_FFC, nrow),
        in_specs=[
            pl.BlockSpec((tr, tc), lambda j, i: (i, j)),
            pl.BlockSpec((tr, tc), lambda j, i: (i, j + N_FFC)),
            prev8(0),
            prev8(N_FFC),
            next8(0),
            next8(N_FFC),
            pl.BlockSpec((tr, tc), lambda j, i: (i, j)),
            next8(0),
            pl.BlockSpec((8, tc), lambda j, i: (0, j)),
            pl.BlockSpec((8, tc), lambda j, i: (0, j + N_FFC)),
            pl.BlockSpec((1, tc), lambda j, i: (0, j)),
            pl.BlockSpec((1, tc), lambda j, i: (0, j + N_FFC)),
        ],
        out_specs=[
            pl.BlockSpec((2, tr, tc), lambda j, i: (0, i, j)),
            pl.BlockSpec((2, 8, tc), lambda j, i: (0, 0, j)),
            pl.BlockSpec((2, 1, tc), lambda j, i: (0, 0, j)),
        ],
        out_shape=[
            jax.ShapeDtypeStruct((2, S, D_FF_PAD), BF16),
            jax.ShapeDtypeStruct((2, 8, D_FF_PAD), F32),
            jax.ShapeDtypeStruct((2, 1, D_FF_PAD), F32),
        ],
        compiler_params=pltpu.CompilerParams(dimension_semantics=("parallel", "arbitrary")),
    )(u0, u0, u0, u0, u0, u0, da, da, cw, cw, cb, cb)


def _adamw(w, g, m, v, name):
    R, C = w.shape
    tr = R
    for cand in (256, 128, 64, 32, 16, 8):
        if R % cand == 0 and R > cand:
            tr = cand
            break

    def body(w_ref, g_ref, m_ref, v_ref, d_ref, nm_ref, nv_ref):
        gv = g_ref[...]
        nm = ADAM_B1 * m_ref[...] + (1.0 - ADAM_B1) * gv
        nv = ADAM_B2 * v_ref[...] + (1.0 - ADAM_B2) * (gv * gv)
        m_hat = nm / (1.0 - ADAM_B1**ADAM_STEP)
        v_hat = nv / (1.0 - ADAM_B2**ADAM_STEP)
        d_ref[...] = -ADAM_LR * (m_hat / (jnp.sqrt(v_hat) + ADAM_EPS) + ADAM_WD * w_ref[...])
        nm_ref[...] = nm
        nv_ref[...] = nv

    blk = pl.BlockSpec((tr, C), lambda i: (i, 0))
    return pl.pallas_call(
        body,
        name=name,
        grid=(R // tr,),
        in_specs=[blk] * 4,
        out_specs=[blk] * 3,
        out_shape=[jax.ShapeDtypeStruct((R, C), F32)] * 3,
        compiler_params=pltpu.CompilerParams(dimension_semantics=("parallel",)),
    )(w, g, m, v)


ANY = pl.BlockSpec(memory_space=pl.ANY)


def _all_gather(blocks, name):
    n = len(blocks)

    def body(*refs):
        x_refs, out_refs = refs[:n], refs[n : 2 * n]
        send_sems, recv_sems, local_sems = refs[2 * n :]
        x, y, c = lax.axis_index("x"), lax.axis_index("y"), lax.axis_index("c")
        me, sibling = (x, y, c), (x, y, 1 - c)
        chips = [(1 - x, y), (x, 1 - y), (1 - x, 1 - y)]

        def slot(a, px, py, pc):
            return out_refs[a].at[4 * px + 2 * py + pc]

        def copy(a, k, blk, to, src=None):
            return pltpu.make_async_remote_copy(
                src_ref=slot(a, *blk) if src is None else src,
                dst_ref=slot(a, *blk),
                send_sem=send_sems.at[7 * a + k],
                recv_sem=recv_sems.at[7 * a + k],
                device_id=to,
                device_id_type=MESH,
            )

        mine = [pltpu.make_async_copy(x_refs[a], slot(a, *me), local_sems.at[a]) for a in range(n)]
        sent = []
        for a in range(n):
            mine[a].start()
            first = [copy(a, 0, me, sibling, src=x_refs[a])]
            first += [copy(a, 1 + j, me, (*chip, c), src=x_refs[a]) for j, chip in enumerate(chips)]
            for cp in first:
                cp.start()
            sent += first
        for a in range(n):
            for j, chip in enumerate(chips):
                copy(a, 1 + j, (*chip, c), me).wait_recv()
                passed = copy(a, 4 + j, (*chip, c), sibling)
                passed.start()
                sent.append(passed)
        for a in range(n):
            copy(a, 0, sibling, me).wait_recv()
            for j, chip in enumerate(chips):
                copy(a, 4 + j, (*chip, 1 - c), me).wait_recv()
        for cp in sent:
            cp.wait_send()
        for cp in mine:
            cp.wait()

    return pl.pallas_call(
        body,
        name=name,
        out_shape=[jax.ShapeDtypeStruct((N_DEV,) + b.shape, b.dtype) for b in blocks],
        in_specs=[ANY] * n,
        out_specs=[ANY] * n,
        scratch_shapes=[pltpu.SemaphoreType.DMA((7 * n,)), pltpu.SemaphoreType.DMA((7 * n,)), pltpu.SemaphoreType.DMA((n,))],
    )(*blocks)


def _pair_exchange(gs, name):
    n = len(gs)

    def body(*refs):
        g_refs, out_refs = refs[:n], refs[n : 2 * n]
        send_sems, recv_sems = refs[2 * n :]
        x, y, c = lax.axis_index("x"), lax.axis_index("y"), lax.axis_index("c")
        copies = [
            pltpu.make_async_remote_copy(
                src_ref=g_refs[a].at[2 * k + (1 - c)],
                dst_ref=out_refs[a].at[k],
                send_sem=send_sems.at[N_CHIP * a + k],
                recv_sem=recv_sems.at[N_CHIP * a + k],
                device_id=(x, y, 1 - c),
                device_id_type=MESH,
            )
            for a in range(n)
            for k in range(N_CHIP)
        ]
        for cp in copies:
            cp.start()
        for cp in copies:
            cp.wait()

    return pl.pallas_call(
        body,
        name=name,
        out_shape=[jax.ShapeDtypeStruct((N_CHIP,) + g.shape[1:], g.dtype) for g in gs],
        in_specs=[ANY] * n,
        out_specs=[ANY] * n,
        scratch_shapes=[pltpu.SemaphoreType.DMA((N_CHIP * n,)), pltpu.SemaphoreType.DMA((N_CHIP * n,))],
    )(*gs)


def _row_tile(rows):
    return max(t for t in range(16, 353, 16) if rows % t == 0)


def _pair_add(g, recv, core, name):
    _, R, C = g.shape
    tr = _row_tile(R)

    def body(core_ref, g_ref, r_ref, o_ref):
        o_ref[...] = (g_ref[...].astype(F32) + r_ref[...].astype(F32)).astype(o_ref.dtype)

    return pl.pallas_call(
        body,
        name=name,
        grid_spec=pltpu.PrefetchScalarGridSpec(
            num_scalar_prefetch=1,
            grid=(N_CHIP, R // tr),
            in_specs=[
                pl.BlockSpec((1, tr, C), lambda k, i, core_ref: (2 * k + core_ref[0], i, 0)),
                pl.BlockSpec((1, tr, C), lambda k, i, core_ref: (k, i, 0)),
            ],
            out_specs=pl.BlockSpec((1, tr, C), lambda k, i, core_ref: (k, i, 0)),
        ),
        out_shape=jax.ShapeDtypeStruct((N_CHIP, R, C), g.dtype),
        compiler_params=pltpu.CompilerParams(dimension_semantics=("parallel", "parallel")),
    )(core, g, recv)


HBM = pl.BlockSpec(memory_space=pltpu.HBM)
SEM = pl.BlockSpec(memory_space=pltpu.SEMAPHORE)
EFFECT = pltpu.SideEffectType.DATAFLOW_SIDE_EFFECTING
RELATIONS = tuple((dx, dy, dc) for dx in (0, 1) for dy in (0, 1) for dc in (0, 1))[1:]


def _related(rel):
    x, y, c = lax.axis_index("x"), lax.axis_index("y"), lax.axis_index("c")
    return (1 - x if rel[0] else x, 1 - y if rel[1] else y, 1 - c if rel[2] else c)


def _dev_index(pos):
    return 4 * pos[0] + 2 * pos[1] + pos[2]


def _peers(chips):
    if chips:
        return [r for r in RELATIONS if not r[2]], N_CHIP, lambda pos: 2 * pos[0] + pos[1]
    return list(RELATIONS), N_DEV, _dev_index


def _exchange_start(srcs, by_slot, after, name, chips=False):
    n = len(srcs)
    rels, slots, slot_of = _peers(chips)
    lands = [lax.empty((slots,) + (s.shape[1:] if by_slot else s.shape), s.dtype) for s in srcs]
    nsem = len(rels) * n

    def body(*refs):
        src_refs, land_refs = refs[:n], refs[n : 2 * n]
        send_sems, recv_sems = refs[2 * n + 1], refs[2 * n + 2]
        token = refs[-1]
        me = slot_of(_related((0, 0, 0)))
        for a in range(n):
            for k, rel in enumerate(rels):
                peer = _related(rel)
                pltpu.make_async_remote_copy(
                    src_ref=src_refs[a].at[slot_of(peer)] if by_slot else src_refs[a],
                    dst_ref=land_refs[a].at[me],
                    send_sem=send_sems.at[len(rels) * a + k],
                    recv_sem=recv_sems.at[len(rels) * a + k],
                    device_id=peer,
                    device_id_type=MESH,
                ).start()
        token[...] = jnp.zeros_like(token)

    def hbm(a):
        return pltpu.HBM(a.shape, a.dtype)

    outs = pl.pallas_call(
        body,
        name=name,
        out_shape=(
            pltpu.SemaphoreType.DMA((nsem,)),
            pltpu.SemaphoreType.DMA((nsem,)),
            *[hbm(s) for s in srcs],
            *[hbm(l) for l in lands],
            jax.ShapeDtypeStruct((8, 128), F32),
        ),
        in_specs=[HBM] * (2 * n) + [ANY],
        out_specs=(SEM, SEM, *[HBM] * (2 * n), pl.BlockSpec(memory_space=pltpu.VMEM)),
        input_output_aliases={i: 2 + i for i in range(2 * n)},
        compiler_params=pltpu.CompilerParams(has_side_effects=EFFECT),
    )(*[pltpu.with_memory_space_constraint(a, pltpu.HBM) for a in list(srcs) + lands], after)
    return (outs[0], outs[1], list(outs[2 : 2 + n]), list(outs[2 + n : 2 + 2 * n])), outs[-1]


def _exchange_wait(handle, by_slot, after, name, chips=False):
    send_sems, recv_sems, srcs, lands = handle
    n = len(srcs)
    rels = _peers(chips)[0]

    def body(*refs):
        src_refs, land_refs = refs[:n], refs[n : 2 * n]
        s_sems, r_sems = refs[2 * n], refs[2 * n + 1]
        for a in range(n):
            for k, rel in enumerate(rels):
                copy = pltpu.make_async_remote_copy(
                    src_ref=src_refs[a].at[0] if by_slot else src_refs[a],
                    dst_ref=land_refs[a].at[0],
                    send_sem=s_sems.at[len(rels) * a + k],
                    recv_sem=r_sems.at[len(rels) * a + k],
                    device_id=_related(rel),
                    device_id_type=MESH,
                )
                copy.wait_send()
                copy.wait_recv()

    outs = pl.pallas_call(
        body,
        name=name,
        out_shape=tuple(pltpu.HBM(a.shape, a.dtype) for a in srcs + lands),
        in_specs=[HBM] * (2 * n) + [SEM, SEM, ANY],
        out_specs=tuple([HBM] * (2 * n)),
        input_output_aliases={i: i for i in range(2 * n)},
        compiler_params=pltpu.CompilerParams(has_side_effects=EFFECT),
    )(*srcs, *lands, send_sems, recv_sems, after)
    return list(outs[:n]), list(outs[n:])


def _slot_sum(parts, name):
    n, R, C = parts.shape
    tr = _row_tile(R) if R % 16 == 0 else R

    def body(p_ref, o_ref):
        acc = p_ref[0].astype(F32)
        for k in range(1, n):
            acc = acc + p_ref[k].astype(F32)
        o_ref[...] = acc

    return pl.pallas_call(
        body,
        name=name,
        grid=(R // tr,),
        in_specs=[pl.BlockSpec((n, tr, C), lambda i: (0, i, 0))],
        out_specs=pl.BlockSpec((tr, C), lambda i: (i, 0)),
        out_shape=jax.ShapeDtypeStruct((R, C), F32),
        compiler_params=pltpu.CompilerParams(dimension_semantics=("parallel",)),
    )(parts)


W_IN_TC = 256
W_IN_BOUNDS = (0, LAT, LAT + 3 * DIL_QKV, LAT + 3 * DIL_QKV + D_MODEL, D_IN)


def _dqkv_chunks():
    return [((g * 3 + part) * DIL_OUT, LAT + part * DIL_QKV + g * DIL_OUT) for g in range(DIL_GROUPS) for part in range(3)]


def _w_in_regroup(slots, name):
    tc = W_IN_TC

    def body(s_ref, lat_ref, dqkv_ref, g_ref, buf):
        for j in range(N_DEV):
            buf[j * IN_ROWS : (j + 1) * IN_ROWS, :] = s_ref[j].astype(F32)[:IN_ROWS, :]
        lat_ref[:LAT, :] = buf[:LAT, :].astype(BF16)
        lat_ref[LAT:, :] = jnp.zeros((LAT_PAD - LAT, tc), BF16)
        for dst, src in _dqkv_chunks():
            dqkv_ref[dst : dst + DIL_OUT, :] = buf[src : src + DIL_OUT, :].astype(BF16)
        g_ref[...] = buf[W_IN_BOUNDS[2] :, :].astype(BF16)

    def col(rows):
        return pl.BlockSpec((rows, tc), lambda k: (0, k))

    return pl.pallas_call(
        body,
        name=name,
        grid=(D_MODEL // tc,),
        in_specs=[pl.BlockSpec((N_DEV, IN_ROWS_PAD, tc), lambda k: (0, 0, k))],
        out_specs=[col(LAT_PAD), col(3 * DIL_QKV), col(2 * D_MODEL)],
        out_shape=[
            jax.ShapeDtypeStruct((LAT_PAD, D_MODEL), BF16),
            jax.ShapeDtypeStruct((3 * DIL_QKV, D_MODEL), BF16),
            jax.ShapeDtypeStruct((2 * D_MODEL, D_MODEL), BF16),
        ],
        scratch_shapes=[pltpu.VMEM((D_IN, tc), F32)],
        compiler_params=pltpu.CompilerParams(dimension_semantics=("parallel",)),
    )(slots)


def _w_in_grad_regroup(g_lat, g_dqkv, g_ga, g_gb, name):
    tc = W_IN_TC

    def body(lat_ref, dqkv_ref, ga_ref, gb_ref, o_ref, buf):
        b = W_IN_BOUNDS
        buf[b[0] : b[1], :] = lat_ref[:LAT, :].astype(F32)
        for dst, src in _dqkv_chunks():
            buf[src : src + DIL_OUT, :] = dqkv_ref[dst : dst + DIL_OUT, :].astype(F32)
        buf[b[2] : b[3], :] = ga_ref[...].astype(F32)
        buf[b[3] : b[4], :] = gb_ref[...].astype(F32)
        fill = jnp.zeros((IN_ROWS_PAD - IN_ROWS, tc), F32)
        for j in range(N_DEV):
            o_ref[j] = jnp.concatenate([buf[j * IN_ROWS : (j + 1) * IN_ROWS, :], fill], axis=0).astype(BF16)

    def col(rows):
        return pl.BlockSpec((rows, tc), lambda k: (0, k))

    return pl.pallas_call(
        body,
        name=name,
        grid=(D_MODEL // tc,),
        in_specs=[col(LAT_PAD), col(3 * DIL_QKV), col(D_MODEL), col(D_MODEL)],
        out_specs=pl.BlockSpec((N_DEV, IN_ROWS_PAD, tc), lambda k: (0, 0, k)),
        out_shape=jax.ShapeDtypeStruct((N_DEV, IN_ROWS_PAD, D_MODEL), BF16),
        scratch_shapes=[pltpu.VMEM((D_IN, tc), F32)],
        compiler_params=pltpu.CompilerParams(dimension_semantics=("parallel",)),
    )(g_lat, g_dqkv, g_ga, g_gb)


def _ffn_pad(a, axis):
    a = jnp.moveaxis(a, axis, -1)
    g = a.reshape(a.shape[:-1] + (2 * N_DEV, FF_GROUP))
    g = jnp.pad(g, [(0, 0)] * (g.ndim - 1) + [(0, FF_GROUP_PAD - FF_GROUP)])
    return jnp.moveaxis(g.reshape(a.shape[:-1] + (2 * D_FF_PAD,)), -1, axis)


def _ffn_unpad(a, axis):
    a = jnp.moveaxis(a, axis, -1)
    g = a.reshape(a.shape[:-1] + (2 * N_DEV, FF_GROUP_PAD))[..., :FF_GROUP]
    return jnp.moveaxis(g.reshape(a.shape[:-1] + (2 * D_FF,)), -1, axis)


MISC = (("w_o_mla", (256, 1024)), ("w_o_dil", (256, 512)), ("w_uq", (192, 512)), ("w_ukv", (256, 256)))


def _exchange_blocks(w):
    def t(a):
        return a.astype(BF16).T

    up = t(w["w_up"]).reshape(2, FF_GROUP, D_MODEL)
    return [
        jnp.pad(t(w["w_in"]), ((0, IN_ROWS_PAD - IN_ROWS), (0, 0))),
        jnp.pad(up, ((0, 0), (0, FF_GROUP_PAD - FF_GROUP), (0, 0))).reshape(2 * FF_GROUP_PAD, D_MODEL),
        jnp.pad(w["w_down"].astype(BF16), ((0, FF_GROUP_PAD - FF_GROUP), (0, 0))),
        w["w_out"].astype(BF16),
        jnp.concatenate([t(w[n]).reshape(-1, D_MODEL) for n, _ in MISC], axis=0),
    ]


def _misc_split(misc):
    out, off = {}, 0
    for n, (r, c) in MISC:
        rows = r * c // D_MODEL
        out[n] = misc[..., off : off + rows, :].reshape(misc.shape[:-2] + (r, c))
        off += rows
    return out


def _small_matrices(g_misc):
    misc = _misc_split(g_misc)
    uq_t = jnp.pad(misc["w_uq"], ((0, 0), (0, HEAD_PAD - QK_NOPE - QK_ROPE), (0, 0)))
    return {
        "uq_t": uq_t.reshape(MLA_HEADS * HEAD_PAD, Q_LORA),
        "ukv_t": misc["w_ukv"].reshape(MLA_HEADS * HEAD_PAD, KV_LORA),
        "o_mla_t": misc["w_o_mla"].reshape(D_MODEL, MLA_HEADS * V_HEAD),
        "o_dil_t": misc["w_o_dil"].reshape(D_MODEL, DIL_OUT),
    }


def _small_grad_blocks(g):
    uq_t = g["uq_t"].reshape(MLA_HEADS, HEAD_PAD, Q_LORA)[:, : QK_NOPE + QK_ROPE]
    misc = {"w_o_mla": g["o_mla_t"], "w_o_dil": g["o_dil_t"], "w_uq": uq_t, "w_ukv": g["ukv_t"]}
    return [
        g["w_out"].reshape(N_DEV, -1, D_MODEL),
        jnp.concatenate([misc[n].reshape(N_DEV, -1, D_MODEL) for n, _ in MISC], axis=1),
    ]


def _grad_shards(sums):
    s_in, s_out, s_misc, s_up, s_down = sums
    out = {
        "w_in": s_in[:IN_ROWS].T,
        "w_up": s_up.reshape(2, FF_GROUP_PAD, D_MODEL)[:, :FF_GROUP].reshape(2 * FF_GROUP, D_MODEL).T,
        "w_down": s_down[:FF_GROUP],
        "w_out": s_out,
    }
    out.update({n: v.T for n, v in _misc_split(s_misc).items()})
    return out


def _local_step(x, tgt, wt, conv_w, small, small_matrices, ffn_weights, send_ffn_grads, send_small_grads, send_w_in_grads, start_token):
    S = x.shape[0]
    lat_t, dqkv_t, g_t = wt
    cw = jnp.pad(_ffn_pad(conv_w, 1), ((0, 5), (0, 0)))
    cb = _ffn_pad(small["conv_b"], 1)
    cos_t, sin_t = _rope_tables(S)
    bias = _dil_bias()
    g1, g2, g3 = small["attn_norm_g"], small["ffn_norm_g"], small["final_norm_g"]
    gq, gkv = small["q_norm_g"], small["kv_norm_g"]

    h = _rms_fwd(x, g1 + start_token, "rms_attn")
    lat = _mm(h, lat_t, "nt", F32, 1024, LAT_PAD, D_MODEL, "proj_lat")
    dqkv = _mm(h, dqkv_t, "nt", F32, 1024, 512, D_MODEL, "proj_dqkv")
    gates = _mm(h, g_t, "nt", F32, 1024, 512, D_MODEL, "proj_gates", bias=small["b_gate"], act="sigmoid")
    sm = small_matrices(lat)
    uq_t, ukv_t, o_mla_t, o_dil_t = sm["uq_t"], sm["ukv_t"], sm["o_mla_t"], sm["o_dil_t"]
    cqn, ckvn, kpe = _mla_prep1(lat, gq, gkv, cos_t, sin_t, "mla_prep1")
    q_raw = _mm(cqn, uq_t, "nt", F32, 1024, 1024, Q_LORA, "mla_uq")
    kv = _mm(ckvn, ukv_t, "nt", BF16, 1024, 1024, KV_LORA, "mla_ukv")
    q_att, k_att = _mla_prep2(q_raw, kv, kpe, cos_t, sin_t, "mla_prep2")
    o, lse = _flash2_fwd(q_att, k_att, kv, "mla_flash_fwd")
    o_a = _mm(o, o_mla_t, "nt", F32, 1024, 1024, MLA_HEADS * V_HEAD, "mla_out")

    d_os, d_ls = [], []
    for g, (_, dil) in enumerate(DIL_PATTERNS):
        og, lg = _dil_fwd_group(dqkv, bias[g], g, dil, f"dil_fwd_{g}")
        d_os.append(og)
        d_ls.append(lg)
    od, dil_lse = _dil_combine(d_os, d_ls, "dil_combine")
    o_b = _mm(od, o_dil_t, "nt", F32, 1024, 1024, DIL_OUT, "dil_out")

    mrg = _merge_fwd(gates, o_a, o_b, "merge_fwd")
    w_out, up_t, w_down = ffn_weights(mrg)
    x1 = _mm(mrg, w_out, "nn", F32, 1024, 1024, D_MODEL, "mix_out", res=x)
    h2 = _rms_fwd(x1, g2, "rms_ffn")
    u0 = _mm(h2, up_t, "nt", F32, 1024, 512, D_MODEL, "ffn_up")
    a = _ffn_fwd(u0, cw, cb, "ffn_conv_fwd")
    x2 = _mm(a, w_down, "nn", F32, 1024, 512, D_FF_PAD // 2, "ffn_down", res=x1)
    loss_part, dx2, dx2b, dg3 = _final_loss(x2, g3, tgt, "final_loss")

    da = _mm(dx2b, w_down, "nt", F32, 1024, 512, D_MODEL, "ffn_down_dx")
    gw_down = _mm(a, dx2b, "tn", BF16, 512, 1024, S, "ffn_down_dw")
    du0, dcw, dcb = _ffn_bwd(u0, da, cw, cb, "ffn_conv_bwd")
    du0 = du0.reshape(2 * S, D_FF_PAD)
    gw_up_t = _mm(du0, h2, "tn", BF16, 512, 1024, S, "ffn_up_dw", a_halves=2)
    sent = send_ffn_grads(gw_up_t, gw_down)
    dh2 = _mm(du0, up_t, "nn", F32, 1024, 1024, D_FF_PAD // 2, "ffn_up_dx", a_halves=2)
    dx1, dx1b, dg2 = _rms_bwd(dh2, x1, g2 + sent, dx2, "rms_ffn_bwd")

    dmrg = _mm(dx1b, w_out, "nt", F32, 1024, 1024, D_MODEL, "mix_out_dx")
    gw_out = _mm(mrg, dx1b, "tn", BF16, 512, 1024, S, "mix_out_dw")
    do_a, do_b, dga, dgb, dba, dbb = _merge_bwd(dmrg, gates, o_a, o_b, "merge_bwd")

    do = _mm(do_a, o_mla_t, "nn", BF16, 1024, 1024, D_MODEL, "mla_out_dx")
    gw_o_mla_t = _mm(do_a, o, "tn", BF16, 1024, 1024, 1024, "mla_out_dw")
    dod = _mm(do_b, o_dil_t, "nn", F32, 1024, DIL_OUT, D_MODEL, "dil_out_dx")
    gw_o_dil_t = _mm(do_b, od, "tn", BF16, 1024, DIL_OUT, 1024, "dil_out_dw")

    dq_att, delta = _flash2_dq(q_att, k_att, kv, o, do, lse, "mla_flash_dq")
    lse_row = lse[:, :, 0][:, None, :]
    delta_row = delta[:, :, 0][:, None, :]
    dk_att, dv = _flash2_dkv(q_att, k_att, kv, do, lse_row, delta_row, "mla_flash_dkv")
    dq_raw, dkv, dkpe = _mla_post(dq_att, dk_att, dv, cos_t, sin_t, "mla_post")
    dcqn = _mm(dq_raw, uq_t, "nn", F32, 1024, Q_LORA, MLA_HEADS * HEAD_PAD, "mla_uq_dx")
    gw_uq_t = _mm(dq_raw, cqn, "tn", BF16, 1024, Q_LORA, 1024, "mla_uq_dw")
    dckvn = _mm(dkv, ukv_t, "nn", F32, 1024, KV_LORA, MLA_HEADS * HEAD_PAD, "mla_ukv_dx")
    gw_ukv_t = _mm(dkv, ckvn, "tn", BF16, 1024, KV_LORA, 1024, "mla_ukv_dw")
    sent = send_small_grads({"uq_t": gw_uq_t, "ukv_t": gw_ukv_t, "o_mla_t": gw_o_mla_t, "o_dil_t": gw_o_dil_t, "w_out": gw_out})
    dlat, dgq, dgkv = _lat_bwd(dcqn, dckvn, dkpe, lat, gq + sent, gkv, "lat_bwd")

    dd = _dil_rowdot(dod, od, "dil_rowdot")
    ddqkv = lax.empty((3 * DIL_GROUPS, S, DIL_OUT), F32)
    for g, (_, dil) in enumerate(DIL_PATTERNS):
        ddqkv = _dil_bwd_group(dqkv, bias[g], dod, dd, dil_lse, ddqkv, g, dil, f"dil_bwd_{g}")
    ddqkv = ddqkv.reshape(3 * DIL_GROUPS * S, DIL_OUT)

    gw_lat_t = _mm(dlat, h, "tn", BF16, LAT_PAD, 1024, S, "proj_lat_dw")
    gw_dqkv_t = _mm(ddqkv, h, "tn", BF16, 512, 1024, S, "proj_dqkv_dw", a_halves=3 * DIL_GROUPS)
    gw_ga_t = _mm(dga, h, "tn", BF16, 512, 1024, S, "proj_ga_dw")
    gw_gb_t = _mm(dgb, h, "tn", BF16, 512, 1024, S, "proj_gb_dw")
    sent = send_w_in_grads(gw_lat_t, gw_dqkv_t, gw_ga_t, gw_gb_t)
    dh = _mm(dlat + sent.astype(BF16), lat_t, "nn", F32, 1024, 1024, LAT_PAD, "proj_lat_dx")
    dh = _mm(ddqkv, dqkv_t, "nn", F32, 1024, 1024, DIL_OUT, "proj_dqkv_dx", res=dh, a_halves=3 * DIL_GROUPS)
    dh = _mm(dga, g_t, "nn", F32, 1024, 1024, D_MODEL, "proj_ga_dx", res=dh)
    dh = _mm(dgb, g_t, "nn", F32, 1024, 1024, D_MODEL, "proj_gb_dx", res=dh, b_koff=1)
    grad_x, _, dg1 = _rms_bwd(dh, x, g1, dx1, "rms_attn_bwd")

    small_grads = {
        "attn_norm_g": dg1,
        "b_gate": jnp.concatenate([dba, dbb], axis=1),
        "q_norm_g": dgq,
        "kv_norm_g": dgkv,
        "ffn_norm_g": dg2,
        "conv_b": _ffn_unpad(jnp.concatenate([dcb[0], dcb[1]], axis=1), 1),
        "final_norm_g": dg3,
        "conv_w": _ffn_unpad(jnp.concatenate([dcw[0, :3], dcw[1, :3]], axis=1), 1),
    }
    return loss_part, grad_x, small_grads


SMALL_ORDER = ("attn_norm_g", "b_gate", "q_norm_g", "kv_norm_g", "ffn_norm_g", "conv_b", "final_norm_g", "conv_w")
WEIGHT_ORDER = (
    "attn_norm_g", "w_in", "b_gate", "q_norm_g", "w_uq", "kv_norm_g", "w_ukv", "w_o_mla", "w_o_dil", "w_out",
    "ffn_norm_g", "w_up", "conv_w", "conv_b", "w_down", "final_norm_g",
)


def kernel(x, attn_norm_g, w_in, b_gate, q_norm_g, w_uq, kv_norm_g, w_ukv, w_o_mla, w_o_dil, w_out, ffn_norm_g, w_up, conv_w, conv_b, w_down, final_norm_g, loss_target, m_attn_norm_g, m_w_in, m_b_gate, m_q_norm_g, m_w_uq, m_kv_norm_g, m_w_ukv, m_w_o_mla, m_w_o_dil, m_w_out, m_ffn_norm_g, m_w_up, m_conv_w, m_conv_b, m_w_down, m_final_norm_g, v_attn_norm_g, v_w_in, v_b_gate, v_q_norm_g, v_w_uq, v_kv_norm_g, v_w_ukv, v_w_o_mla, v_w_o_dil, v_w_out, v_ffn_norm_g, v_w_up, v_conv_w, v_conv_b, v_w_down, v_final_norm_g):
    env = dict(locals())
    dev = 4 * lax.axis_index("x") + 2 * lax.axis_index("y") + lax.axis_index("c")
    core = lax.axis_index("c").astype(jnp.int32).reshape(1)

    def two_d(a):
        return a.reshape(-1, a.shape[-1])

    w = {n: two_d(env[n]) for n in WEIGHT_ORDER}
    m = {n: two_d(env["m_" + n]) for n in WEIGHT_ORDER}
    v = {n: two_d(env["v_" + n]) for n in WEIGHT_ORDER}

    chip = 2 * lax.axis_index("x") + lax.axis_index("y")

    def own_slot_in(lands, own, slot=dev):
        return [lax.dynamic_update_slice(l, o[None], (slot, 0, 0)) for l, o in zip(lands, own)]

    b_in, b_up, b_down, b_out, b_misc = _exchange_blocks(w)
    r, c = CONV_SHARD
    conv = jnp.pad(w["conv_w"].reshape(-1), (0, 8 * SMALL_COLS - r * c)).reshape(8, SMALL_COLS)
    g_in, conv = _all_gather([b_in, conv], "ag_w_in")
    misc_gather, started = _exchange_start([b_misc], False, conv, "ag_small_start")
    ffn_gather, started2 = _exchange_start([b_up, b_down, b_out], False, started, "ag_ffn_start")
    wt = _w_in_regroup(g_in, "w_in_regroup")
    conv = conv.reshape(N_DEV, 8 * SMALL_COLS)[:, : r * c].reshape(N_DEV, r, c)
    conv_w_full = conv.transpose(1, 0, 2).reshape(r, N_DEV * c)
    small = {n: w[n] for n in SMALL_ORDER if n != "conv_w"}

    def small_matrices(after):
        own, lands = _exchange_wait(misc_gather, False, after, "ag_small_wait")
        return _small_matrices(own_slot_in(lands, own)[0])

    def ffn_weights(after):
        own, lands = _exchange_wait(ffn_gather, False, after, "ag_ffn_wait")
        g_up, g_down, g_out = own_slot_in(lands, own)
        return g_out.reshape(D_MODEL, D_MODEL), g_up.reshape(2 * D_FF_PAD, D_MODEL), g_down.reshape(D_FF_PAD, D_MODEL)

    reduces = {}

    def send_ffn_grads(gw_up_t, gw_down):
        blocks = [gw_up_t.reshape(N_DEV, 2 * FF_GROUP_PAD, D_MODEL), gw_down.reshape(N_DEV, FF_GROUP_PAD, D_MODEL)]
        reduces["ffn"], token = _exchange_start(blocks, True, gw_down, "rs_ffn_start")
        return token[0, 0]

    def send_small_grads(g):
        reduces["small"], token = _exchange_start(_small_grad_blocks(g), True, g["w_out"], "rs_small_start")
        return token[0, 0]

    def send_w_in_grads(g_lat, g_dqkv, g_ga, g_gb):
        e_in = _w_in_grad_regroup(g_lat, g_dqkv, g_ga, g_gb, "w_in_grad_regroup")
        pair = _pair_add(e_in, _pair_exchange([e_in], "rs_w_in_pair_exchange")[0], core, "rs_w_in_pair_add")
        reduces["w_in"], token = _exchange_start([pair], True, pair, "rs_w_in_start", chips=True)
        return token[0, 0]

    loss_part, grad_x, small_grads = _local_step(
        x[0], loss_target[0], wt, conv_w_full, small, small_matrices, ffn_weights,
        send_ffn_grads, send_small_grads, send_w_in_grads, started2[0, 0],
    )
    loss = lax.psum(loss_part[0, 0], AXES)

    def finish(key, by_chip, name):
        sent, lands = _exchange_wait(reduces[key], True, grad_x, name + "_wait", chips=by_chip)
        slot = chip if by_chip else dev
        own = [lax.dynamic_index_in_dim(s, slot, 0, keepdims=False) for s in sent]
        return [_slot_sum(p, f"{name}_sum_{i}") for i, p in enumerate(own_slot_in(lands, own, slot))]

    (s_in,) = finish("w_in", True, "rs_w_in")
    s_out, s_misc = finish("small", False, "rs_small")
    s_up, s_down = finish("ffn", False, "rs_ffn")
    gshard = _grad_shards([s_in, s_out, s_misc, s_up, s_down])

    sflat = jnp.concatenate([small_grads[n].reshape(-1) for n in SMALL_ORDER])
    sflat = jnp.pad(sflat, (0, SMALL_ROWS * SMALL_COLS - sflat.shape[0])).reshape(SMALL_ROWS, SMALL_COLS)
    ssum = _slot_sum(_all_gather([sflat], "ag_small_grads")[0], "small_sum").reshape(-1)
    gsmall, off = {}, 0
    for n in SMALL_ORDER:
        shape = (3, 2 * D_FF) if n == "conv_w" else w[n].shape
        size = shape[0] * shape[1]
        gsmall[n] = ssum[off : off + size].reshape(shape)
        off += size
    gsmall["conv_w"] = lax.dynamic_slice_in_dim(gsmall["conv_w"], dev * CONV_SHARD[1], CONV_SHARD[1], axis=1)

    g_all = {**gshard, **gsmall}
    out_g, out_d, out_m, out_v = [], [], [], []
    for n in WEIGHT_ORDER:
        d, nm, nv = _adamw(w[n], g_all[n], m[n], v[n], "adamw_" + n)
        shape = env[n].shape
        out_g.append(g_all[n].reshape(shape))
        out_d.append(d.reshape(shape))
        out_m.append(nm.reshape(shape))
        out_v.append(nv.reshape(shape))
    return (loss, grad_x[None], *out_g, *out_d, *out_m, *out_v)
```

```python
import functools

import jax
import jax.numpy as jnp
import numpy as np
from jax import lax
from jax.experimental import pallas as pl
from jax.experimental.pallas import tpu as pltpu

F32 = jnp.float32
BF16 = jnp.bfloat16

N_DEV = 8
N_CHIP = 4
AXES = ("x", "y", "c")
MESH = pl.DeviceIdType.MESH

D_MODEL = 2048
MLA_HEADS = 8
QK_NOPE = 128
QK_ROPE = 64
V_HEAD = 128
Q_LORA = 512
KV_LORA = 256
ROPE_THETA = 10000.0
HEAD_PAD = 256
DIL_PATTERNS = ((128, 1), (512, 4), (2048, 16))
DIL_GROUPS = 3
DIL_HG = 4
DIL_HEADS = 12
DIL_HD = 128
DIL_BLK = 128
DIL_QKV = DIL_HEADS * DIL_HD
DIL_OUT = DIL_HG * DIL_HD
ALIBI_MAX_BIAS = 8.0
D_FF = 5504
D_FF_PAD = 5632
NORM_EPS = 1e-6
LAT = Q_LORA + KV_LORA + QK_ROPE
LAT_PAD = 896
D_IN = LAT + 3 * DIL_QKV + 2 * D_MODEL
NEG = -1e30

ADAM_LR = 0.001
ADAM_B1 = 0.9
ADAM_B2 = 0.999
ADAM_EPS = 1e-08
ADAM_WD = 0.01
ADAM_STEP = 10

SMALL_ROWS = 56
SMALL_COLS = 1024

IN_ROWS = 1192
IN_ROWS_PAD = 1200
FF_GROUP = D_FF // N_DEV
FF_GROUP_PAD = D_FF_PAD // N_DEV
CONV_SHARD = (3, 1376)

NT = (((1,), (1,)), ((), ()))
TN = (((0,), (0,)), ((), ()))


def _dot(a, b, dims=(((1,), (0,)), ((), ()))):
    return lax.dot_general(a, b, dims, preferred_element_type=F32)


def _mm(a, b, mode, out_dtype, tm, tn, tk, name, bias=None, act=None, res=None, b_koff=0, a_halves=1):
    H = a_halves
    if mode == "nn":
        (M, K), (K2, N) = (a.shape[0] // H, a.shape[1] * H), b.shape
        assert (b_koff + 1) * K <= K2, (name, a.shape, b.shape)
        koff, K2 = b_koff * (K // tk), K
        kper, mrows = a.shape[1] // tk, M // tm
        a_spec = pl.BlockSpec((tm, tk), lambda i, j, k: (i + (k // kper) * mrows, k % kper))
        b_spec = pl.BlockSpec((tk, tn), lambda i, j, k: (k + koff, j))
        dims = (((1,), (0,)), ((), ()))
    elif mode == "nt":
        (M, K), (N, K2) = a.shape, b.shape
        a_spec = pl.BlockSpec((tm, tk), lambda i, j, k: (i, k))
        b_spec = pl.BlockSpec((tn, tk), lambda i, j, k: (j, k))
        dims = NT
    else:
        (K, M), (K2, N) = (a.shape[0] // H, a.shape[1] * H), b.shape
        mper, krows = a.shape[1] // tm, K // tk
        a_spec = pl.BlockSpec((tk, tm), lambda i, j, k: (k + (i // mper) * krows, i % mper))
        b_spec = pl.BlockSpec((tk, tn), lambda i, j, k: (k, j))
        dims = TN
    assert K == K2 and M % tm == 0 and N % tn == 0 and K % tk == 0, (name, a.shape, b.shape)
    nk = K // tk
    has_bias, has_res = bias is not None, res is not None

    def body(*refs):
        refs = list(refs)
        a_ref, b_ref = refs[0], refs[1]
        pos = 2
        bias_ref = res_ref = None
        if has_bias:
            bias_ref = refs[pos]
            pos += 1
        if has_res:
            res_ref = refs[pos]
            pos += 1
        o_ref = refs[pos]
        p = _dot(a_ref[...].astype(BF16), b_ref[...].astype(BF16), dims)

        def finish(acc):
            if has_bias:
                acc = acc + bias_ref[...]
            if act == "sigmoid":
                acc = jax.nn.sigmoid(acc)
            if has_res:
                acc = res_ref[...] + acc
            o_ref[...] = acc.astype(o_ref.dtype)

        if nk == 1:
            finish(p)
        else:
            acc_ref = refs[pos + 1]
            k = pl.program_id(2)

            @pl.when(k == 0)
            def _():
                acc_ref[...] = p

            @pl.when(k != 0)
            def _():
                acc_ref[...] += p

            @pl.when(k == nk - 1)
            def _():
                finish(acc_ref[...])

    in_specs = [a_spec, b_spec]
    args = [a, b]
    if has_bias:
        in_specs.append(pl.BlockSpec((1, tn), lambda i, j, k: (0, j)))
        args.append(bias)
    if has_res:
        in_specs.append(pl.BlockSpec((tm, tn), lambda i, j, k: (i, j)))
        args.append(res)
    return pl.pallas_call(
        body,
        name=name,
        grid=(M // tm, N // tn, nk),
        in_specs=in_specs,
        out_specs=pl.BlockSpec((tm, tn), lambda i, j, k: (i, j)),
        out_shape=jax.ShapeDtypeStruct((M, N), out_dtype),
        scratch_shapes=[pltpu.VMEM((tm, tn), F32)] if nk > 1 else [],
        compiler_params=pltpu.CompilerParams(dimension_semantics=("parallel", "parallel", "arbitrary")),
    )(*args)


def _rstd(x):
    return lax.rsqrt(jnp.mean(x * x, axis=-1, keepdims=True) + NORM_EPS)


def _rms_bwd_math(dy, x, g):
    r = _rstd(x)
    xh = x * r
    dg = jnp.sum(dy * xh, axis=0, keepdims=True)
    dxh = dy * g
    dx = r * (dxh - xh * jnp.mean(dxh * xh, axis=-1, keepdims=True))
    return dx, dg


def _rms_fwd(x, g, name, tr=256):
    S, D = x.shape

    def body(x_ref, g_ref, o_ref):
        xv = x_ref[...]
        o_ref[...] = ((xv * _rstd(xv)) * g_ref[...]).astype(o_ref.dtype)

    return pl.pallas_call(
        body,
        name=name,
        grid=(S // tr,),
        in_specs=[pl.BlockSpec((tr, D), lambda i: (i, 0)), pl.BlockSpec((1, D), lambda i: (0, 0))],
        out_specs=pl.BlockSpec((tr, D), lambda i: (i, 0)),
        out_shape=jax.ShapeDtypeStruct((S, D), BF16),
        compiler_params=pltpu.CompilerParams(dimension_semantics=("parallel",)),
    )(x, g)


def _rms_bwd(dy, x, g, res, name, tr=256):
    S, D = x.shape

    def body(dy_ref, x_ref, g_ref, res_ref, dx_ref, dxb_ref, dg_ref):
        dx, dg = _rms_bwd_math(dy_ref[...], x_ref[...], g_ref[...])
        dx = dx + res_ref[...]
        dx_ref[...] = dx
        dxb_ref[...] = dx.astype(BF16)

        @pl.when(pl.program_id(0) == 0)
        def _():
            dg_ref[...] = dg

        @pl.when(pl.program_id(0) != 0)
        def _():
            dg_ref[...] += dg

    row = pl.BlockSpec((tr, D), lambda i: (i, 0))
    vec = pl.BlockSpec((1, D), lambda i: (0, 0))
    return pl.pallas_call(
        body,
        name=name,
        grid=(S // tr,),
        in_specs=[row, row, vec, row],
        out_specs=[row, row, vec],
        out_shape=[jax.ShapeDtypeStruct((S, D), F32), jax.ShapeDtypeStruct((S, D), BF16), jax.ShapeDtypeStruct((1, D), F32)],
        compiler_params=pltpu.CompilerParams(dimension_semantics=("arbitrary",)),
    )(dy, x, g, res)


def _final_loss(x2, g, tgt, name, tr=256):
    S, D = x2.shape

    def body(x_ref, g_ref, t_ref, loss_ref, dx_ref, dxb_ref, dg_ref):
        xv, gv = x_ref[...], g_ref[...]
        y = (xv * _rstd(xv)) * gv
        e = y - t_ref[...]
        part = 0.5 * jnp.sum(jnp.mean(e * e, axis=-1, keepdims=True), axis=0, keepdims=True)
        dx, dg = _rms_bwd_math(e * (1.0 / D), xv, gv)
        dx_ref[...] = dx
        dxb_ref[...] = dx.astype(BF16)
        part = jnp.broadcast_to(part, (1, 128))

        @pl.when(pl.program_id(0) == 0)
        def _():
            dg_ref[...] = dg
            loss_ref[...] = part

        @pl.when(pl.program_id(0) != 0)
        def _():
            dg_ref[...] += dg
            loss_ref[...] += part

    row = pl.BlockSpec((tr, D), lambda i: (i, 0))
    vec = pl.BlockSpec((1, D), lambda i: (0, 0))
    return pl.pallas_call(
        body,
        name=name,
        grid=(S // tr,),
        in_specs=[row, vec, row],
        out_specs=[pl.BlockSpec((1, 128), lambda i: (0, 0)), row, row, vec],
        out_shape=[
            jax.ShapeDtypeStruct((1, 128), F32),
            jax.ShapeDtypeStruct((S, D), F32),
            jax.ShapeDtypeStruct((S, D), BF16),
            jax.ShapeDtypeStruct((1, D), F32),
        ],
        compiler_params=pltpu.CompilerParams(dimension_semantics=("arbitrary",)),
    )(x2, g, tgt)


def _rope_tables(S):
    pos = jnp.arange(S, dtype=F32)
    inv_freq = ROPE_THETA ** (-jnp.arange(0, QK_ROPE, 2, dtype=F32) / QK_ROPE)
    ang = pos[:, None] * inv_freq[None, :]
    cos, sin = jnp.cos(ang), jnp.sin(ang)
    zero = jnp.zeros((S, 128 - QK_ROPE), F32)
    return jnp.concatenate([cos, cos, zero], axis=1), jnp.concatenate([-sin, sin, zero], axis=1)


def _rope_tile(x, cos_t, sin_t):
    lane = lax.broadcasted_iota(jnp.int32, x.shape, 1)
    partner = jnp.where(lane < QK_ROPE // 2, pltpu.roll(x, 128 - QK_ROPE // 2, 1), pltpu.roll(x, QK_ROPE // 2, 1))
    return x * cos_t + partner * sin_t


def _mla_prep1(lat, gq, gkv, cos_t, sin_t, name, tr=256):
    S = lat.shape[0]

    def body(lat_ref, gq_ref, gkv_ref, cos_ref, sin_ref, cq_ref, ckv_ref, kpe_ref):
        cq = lat_ref[:, :Q_LORA]
        ckv = lat_ref[:, Q_LORA : Q_LORA + KV_LORA]
        cq_ref[...] = ((cq * _rstd(cq)) * gq_ref[...]).astype(BF16)
        ckv_ref[...] = ((ckv * _rstd(ckv)) * gkv_ref[...]).astype(BF16)
        kpe_ref[...] = _rope_tile(lat_ref[:, Q_LORA + KV_LORA :], cos_ref[...], sin_ref[...]).astype(BF16)

    def row(n):
        return pl.BlockSpec((tr, n), lambda i: (i, 0))

    def vec(n):
        return pl.BlockSpec((1, n), lambda i: (0, 0))

    return pl.pallas_call(
        body,
        name=name,
        grid=(S // tr,),
        in_specs=[row(LAT_PAD), vec(Q_LORA), vec(KV_LORA), row(128), row(128)],
        out_specs=[row(Q_LORA), row(KV_LORA), row(128)],
        out_shape=[
            jax.ShapeDtypeStruct((S, Q_LORA), BF16),
            jax.ShapeDtypeStruct((S, KV_LORA), BF16),
            jax.ShapeDtypeStruct((S, 128), BF16),
        ],
        compiler_params=pltpu.CompilerParams(dimension_semantics=("parallel",)),
    )(lat, gq, gkv, cos_t, sin_t)


def _mla_prep2(q_raw, kv, kpe, cos_t, sin_t, name, tr=256):
    S = q_raw.shape[0]
    W = MLA_HEADS * HEAD_PAD

    def body(q_ref, kv_ref, kpe_ref, cos_ref, sin_ref, qa_ref, ka_ref):
        cos_v, sin_v, kpe_v = cos_ref[...], sin_ref[...], kpe_ref[...]
        for h in range(MLA_HEADS):
            lo = h * HEAD_PAD
            qa_ref[:, lo : lo + 128] = q_ref[:, lo : lo + 128].astype(BF16)
            qa_ref[:, lo + 128 : lo + 256] = _rope_tile(q_ref[:, lo + 128 : lo + 256], cos_v, sin_v).astype(BF16)
            ka_ref[:, lo : lo + 128] = kv_ref[:, lo : lo + 128]
            ka_ref[:, lo + 128 : lo + 256] = kpe_v

    def row(n):
        return pl.BlockSpec((tr, n), lambda i: (i, 0))

    return pl.pallas_call(
        body,
        name=name,
        grid=(S // tr,),
        in_specs=[row(W), row(W), row(128), row(128), row(128)],
        out_specs=[row(W), row(W)],
        out_shape=[jax.ShapeDtypeStruct((S, W), BF16), jax.ShapeDtypeStruct((S, W), BF16)],
        compiler_params=pltpu.CompilerParams(dimension_semantics=("parallel",)),
    )(q_raw, kv, kpe, cos_t, sin_t)


def _mla_post(dq_att, dk_att, dv, cos_t, sin_t, name, tr=256):
    S = dq_att.shape[0]
    W = MLA_HEADS * HEAD_PAD

    def body(dq_ref, dk_ref, dv_ref, cos_ref, sin_ref, dqr_ref, dkv_ref, dkpe_ref):
        cos_v, nsin_v = cos_ref[...], -sin_ref[...]
        kpe = jnp.zeros((tr, 128), F32)
        for h in range(MLA_HEADS):
            lo = h * HEAD_PAD
            dqr_ref[:, lo : lo + 128] = dq_ref[:, lo : lo + 128].astype(BF16)
            dqr_ref[:, lo + 128 : lo + 256] = _rope_tile(dq_ref[:, lo + 128 : lo + 256], cos_v, nsin_v).astype(BF16)
            dkv_ref[:, lo : lo + 128] = dk_ref[:, lo : lo + 128].astype(BF16)
            dkv_ref[:, lo + 128 : lo + 256] = dv_ref[:, h * 128 : (h + 1) * 128].astype(BF16)
            kpe = kpe + dk_ref[:, lo + 128 : lo + 256]
        dkpe_ref[...] = _rope_tile(kpe, cos_v, nsin_v)

    def row(n):
        return pl.BlockSpec((tr, n), lambda i: (i, 0))

    return pl.pallas_call(
        body,
        name=name,
        grid=(S // tr,),
        in_specs=[row(W), row(W), row(MLA_HEADS * V_HEAD), row(128), row(128)],
        out_specs=[row(W), row(W), row(128)],
        out_shape=[jax.ShapeDtypeStruct((S, W), BF16), jax.ShapeDtypeStruct((S, W), BF16), jax.ShapeDtypeStruct((S, 128), F32)],
        compiler_params=pltpu.CompilerParams(dimension_semantics=("parallel",)),
    )(dq_att, dk_att, dv, cos_t, sin_t)


def _lat_bwd(dcqn, dckvn, dkpe, lat, gq, gkv, name, tr=256):
    S = lat.shape[0]

    def body(dcq_ref, dckv_ref, dkpe_ref, lat_ref, gq_ref, gkv_ref, dlat_ref, dgq_ref, dgkv_ref):
        dq, dgq = _rms_bwd_math(dcq_ref[...], lat_ref[:, :Q_LORA], gq_ref[...])
        dkv, dgkv = _rms_bwd_math(dckv_ref[...], lat_ref[:, Q_LORA : Q_LORA + KV_LORA], gkv_ref[...])
        dlat_ref[:, :Q_LORA] = dq.astype(BF16)
        dlat_ref[:, Q_LORA : Q_LORA + KV_LORA] = dkv.astype(BF16)
        dlat_ref[:, Q_LORA + KV_LORA :] = dkpe_ref[...].astype(BF16)

        @pl.when(pl.program_id(0) == 0)
        def _():
            dgq_ref[...] = dgq
            dgkv_ref[...] = dgkv

        @pl.when(pl.program_id(0) != 0)
        def _():
            dgq_ref[...] += dgq
            dgkv_ref[...] += dgkv

    def row(n):
        return pl.BlockSpec((tr, n), lambda i: (i, 0))

    def vec(n):
        return pl.BlockSpec((1, n), lambda i: (0, 0))

    return pl.pallas_call(
        body,
        name=name,
        grid=(S // tr,),
        in_specs=[row(Q_LORA), row(KV_LORA), row(128), row(LAT_PAD), vec(Q_LORA), vec(KV_LORA)],
        out_specs=[row(LAT_PAD), vec(Q_LORA), vec(KV_LORA)],
        out_shape=[
            jax.ShapeDtypeStruct((S, LAT_PAD), BF16),
            jax.ShapeDtypeStruct((1, Q_LORA), F32),
            jax.ShapeDtypeStruct((1, KV_LORA), F32),
        ],
        compiler_params=pltpu.CompilerParams(dimension_semantics=("arbitrary",)),
    )(dcqn, dckvn, dkpe, lat, gq, gkv)


MLA_SCALE = (QK_NOPE + QK_ROPE) ** -0.5
LOG2E = 1.4426950408889634
MLA_C2 = MLA_SCALE * LOG2E
FLASH_T = 1024


def _causal_pairs(n, by_key):
    pairs = [(i, j) for j in range(n) for i in range(j, n)] if by_key else [(i, j) for i in range(n) for j in range(i + 1)]
    return jnp.asarray([p[0] for p in pairs], jnp.int32), jnp.asarray([p[1] for p in pairs], jnp.int32)


def _lanes(x, n):
    return jnp.tile(x, (1, n // 128))


def _flash_grid(npairs, in_specs, out_specs, scratch):
    return pltpu.PrefetchScalarGridSpec(
        num_scalar_prefetch=2, grid=(MLA_HEADS, npairs), in_specs=in_specs, out_specs=out_specs, scratch_shapes=scratch
    )


def _flash2_fwd(q_att, k_att, kv, name, t=FLASH_T):
    S = q_att.shape[0]
    qi_tab, kj_tab = _causal_pairs(S // t, by_key=False)

    def body(qi_ref, kj_ref, q_ref, k_ref, v_ref, o_ref, lse_ref, m_sc, l_sc, acc_sc):
        step = pl.program_id(1)
        qi, kj = qi_ref[step], kj_ref[step]

        @pl.when(kj == 0)
        def _():
            m_sc[...] = jnp.full((t, 128), NEG, F32)
            l_sc[...] = jnp.zeros((t, 128), F32)
            acc_sc[...] = jnp.zeros((t, V_HEAD), F32)

        def update(s):
            m_prev = m_sc[...]
            m_new = jnp.maximum(m_prev, jnp.max(s, axis=1, keepdims=True))
            p = jnp.exp2((s - _lanes(m_new, t)) * MLA_C2)
            alpha = jnp.exp2((m_prev - m_new) * MLA_C2)
            l_sc[...] = alpha * l_sc[...] + jnp.sum(p, axis=1, keepdims=True)
            acc_sc[...] = alpha * acc_sc[...] + _dot(p.astype(BF16), v_ref[...])
            m_sc[...] = m_new

        @pl.when(kj < qi)
        def _():
            update(_dot(q_ref[...], k_ref[...], NT))

        @pl.when(kj == qi)
        def _():
            s = _dot(q_ref[...], k_ref[...], NT)
            rows = lax.broadcasted_iota(jnp.int32, s.shape, 0)
            cols = lax.broadcasted_iota(jnp.int32, s.shape, 1)
            update(jnp.where(cols <= rows, s, NEG))
            l = l_sc[...]
            o_ref[...] = acc_sc[...] / l
            lse_ref[0] = m_sc[...] * MLA_SCALE + jnp.log(l)

    return pl.pallas_call(
        body,
        name=name,
        grid_spec=_flash_grid(
            qi_tab.shape[0],
            [
                pl.BlockSpec((t, HEAD_PAD), lambda h, p, qi, kj: (qi[p], h)),
                pl.BlockSpec((t, HEAD_PAD), lambda h, p, qi, kj: (kj[p], h)),
                pl.BlockSpec((t, V_HEAD), lambda h, p, qi, kj: (kj[p], 2 * h + 1)),
            ],
            [
                pl.BlockSpec((t, V_HEAD), lambda h, p, qi, kj: (qi[p], h)),
                pl.BlockSpec((1, t, 128), lambda h, p, qi, kj: (h, qi[p], 0)),
            ],
            [pltpu.VMEM((t, 128), F32), pltpu.VMEM((t, 128), F32), pltpu.VMEM((t, V_HEAD), F32)],
        ),
        out_shape=[jax.ShapeDtypeStruct((S, MLA_HEADS * V_HEAD), F32), jax.ShapeDtypeStruct((MLA_HEADS, S, 128), F32)],
        compiler_params=pltpu.CompilerParams(dimension_semantics=("parallel", "arbitrary")),
    )(qi_tab, kj_tab, q_att, k_att, kv)


def _flash2_dq(q_att, k_att, kv, o, do, lse, name, t=FLASH_T):
    S = q_att.shape[0]
    qi_tab, kj_tab = _causal_pairs(S // t, by_key=False)

    def body(qi_ref, kj_ref, q_ref, k_ref, v_ref, o_ref, do_ref, lse_ref, dq_ref, dl_ref, acc_sc):
        step = pl.program_id(1)
        qi, kj = qi_ref[step], kj_ref[step]

        @pl.when(kj == 0)
        def _():
            acc_sc[...] = jnp.zeros((t, HEAD_PAD), F32)
            dl = jnp.sum(do_ref[...].astype(F32) * o_ref[...], axis=1, keepdims=True)
            dl_ref[0] = jnp.broadcast_to(dl, (t, 128))

        def update(s):
            k = k_ref[...]
            p = jnp.exp2(s * MLA_C2 - _lanes(lse_ref[0] * LOG2E, t))
            dp = _dot(do_ref[...], v_ref[...], NT)
            ds = p * (dp - _lanes(dl_ref[0], t))
            acc_sc[...] += _dot(ds.astype(BF16), k)

        @pl.when(kj < qi)
        def _():
            update(_dot(q_ref[...], k_ref[...], NT))

        @pl.when(kj == qi)
        def _():
            s = _dot(q_ref[...], k_ref[...], NT)
            rows = lax.broadcasted_iota(jnp.int32, s.shape, 0)
            cols = lax.broadcasted_iota(jnp.int32, s.shape, 1)
            update(jnp.where(cols <= rows, s, NEG))
            dq_ref[...] = acc_sc[...] * MLA_SCALE

    qrow = lambda h, p, qi, kj: (qi[p], h)
    stat = pl.BlockSpec((1, t, 128), lambda h, p, qi, kj: (h, qi[p], 0))
    return pl.pallas_call(
        body,
        name=name,
        grid_spec=_flash_grid(
            qi_tab.shape[0],
            [
                pl.BlockSpec((t, HEAD_PAD), qrow),
                pl.BlockSpec((t, HEAD_PAD), lambda h, p, qi, kj: (kj[p], h)),
                pl.BlockSpec((t, V_HEAD), lambda h, p, qi, kj: (kj[p], 2 * h + 1)),
                pl.BlockSpec((t, V_HEAD), qrow),
                pl.BlockSpec((t, V_HEAD), qrow),
                stat,
            ],
            [pl.BlockSpec((t, HEAD_PAD), qrow), stat],
            [pltpu.VMEM((t, HEAD_PAD), F32)],
        ),
        out_shape=[jax.ShapeDtypeStruct((S, MLA_HEADS * HEAD_PAD), F32), jax.ShapeDtypeStruct((MLA_HEADS, S, 128), F32)],
        compiler_params=pltpu.CompilerParams(dimension_semantics=("parallel", "arbitrary")),
    )(qi_tab, kj_tab, q_att, k_att, kv, o, do, lse)


def _flash2_dkv(q_att, k_att, kv, do, lse_row, delta_row, name, t=FLASH_T):
    S = q_att.shape[0]
    n = S // t
    qi_tab, kj_tab = _causal_pairs(n, by_key=True)

    def body(qi_ref, kj_ref, q_ref, k_ref, v_ref, do_ref, lse_ref, dl_ref, dk_ref, dv_ref, dk_sc, dv_sc):
        step = pl.program_id(1)
        qi, kj = qi_ref[step], kj_ref[step]

        def update(st):
            q, do_v = q_ref[...], do_ref[...]
            pt = jnp.exp2(st * MLA_C2 - lse_ref[0] * LOG2E)
            dv_sc[...] += _dot(pt.astype(BF16), do_v)
            dpt = _dot(v_ref[...], do_v, NT)
            dst = pt * (dpt - dl_ref[0])
            dk_sc[...] += _dot(dst.astype(BF16), q)

        @pl.when(qi == kj)
        def _():
            dk_sc[...] = jnp.zeros((t, HEAD_PAD), F32)
            dv_sc[...] = jnp.zeros((t, V_HEAD), F32)
            st = _dot(k_ref[...], q_ref[...], NT)
            keys = lax.broadcasted_iota(jnp.int32, st.shape, 0)
            qs = lax.broadcasted_iota(jnp.int32, st.shape, 1)
            update(jnp.where(keys <= qs, st, NEG))

        @pl.when(qi > kj)
        def _():
            update(_dot(k_ref[...], q_ref[...], NT))

        @pl.when(qi == n - 1)
        def _():
            dk_ref[...] = dk_sc[...] * MLA_SCALE
            dv_ref[...] = dv_sc[...]

    qrow = lambda h, p, qi, kj: (qi[p], h)
    krow = lambda h, p, qi, kj: (kj[p], h)
    stat = pl.BlockSpec((1, 1, t), lambda h, p, qi, kj: (h, 0, qi[p]))
    return pl.pallas_call(
        body,
        name=name,
        grid_spec=_flash_grid(
            qi_tab.shape[0],
            [
                pl.BlockSpec((t, HEAD_PAD), qrow),
                pl.BlockSpec((t, HEAD_PAD), krow),
                pl.BlockSpec((t, V_HEAD), lambda h, p, qi, kj: (kj[p], 2 * h + 1)),
                pl.BlockSpec((t, V_HEAD), qrow),
                stat,
                stat,
            ],
            [pl.BlockSpec((t, HEAD_PAD), krow), pl.BlockSpec((t, V_HEAD), krow)],
            [pltpu.VMEM((t, HEAD_PAD), F32), pltpu.VMEM((t, V_HEAD), F32)],
        ),
        out_shape=[jax.ShapeDtypeStruct((S, MLA_HEADS * HEAD_PAD), F32), jax.ShapeDtypeStruct((S, MLA_HEADS * V_HEAD), F32)],
        compiler_params=pltpu.CompilerParams(dimension_semantics=("parallel", "arbitrary")),
    )(qi_tab, kj_tab, q_att, k_att, kv, do, lse_row, delta_row)


DIL_SCALE = DIL_HD**-0.5


def _dil_bias():
    slopes = 2.0 ** (-ALIBI_MAX_BIAS * np.arange(1, DIL_HEADS + 1, dtype=np.float64) / DIL_HEADS)
    slopes = slopes.astype(np.float32).reshape(DIL_GROUPS, DIL_HG)
    p = np.arange(DIL_BLK)[:, None]
    kidx = np.arange(2 * DIL_BLK)[None, :]
    j = p + DIL_BLK - kidx
    out = np.zeros((DIL_GROUPS, DIL_HG, DIL_BLK, 2 * DIL_BLK), np.float32)
    for g, (window, dil) in enumerate(DIL_PATTERNS):
        valid = (j >= 0) & (j <= window // dil)
        for h in range(DIL_HG):
            alibi = -slopes[g, h] * (dil * j).astype(np.float32)
            out[g, h] = np.where(valid, alibi, np.float32(NEG))
    return jnp.asarray(out)


DIL_UNROLL = 4


def _unrolled_loop(lo, hi, fn, unroll=DIL_UNROLL):
    groups = (hi - lo) // unroll
    done = lo
    if groups > 1:

        def step(i, carry):
            for u in range(unroll):
                fn(lo + i * unroll + u)
            return carry

        lax.fori_loop(0, groups, step, 0)
        done = lo + groups * unroll
    for n in range(done, hi):
        fn(n)


def _dil_rows(r, n, count, dil):
    if dil == 1:
        if isinstance(n, int):
            return slice(n * DIL_BLK, (n + count) * DIL_BLK)
        return pl.ds(pl.multiple_of(n * DIL_BLK, DIL_BLK), count * DIL_BLK)
    return pl.ds(n * DIL_BLK * dil + r, count * DIL_BLK, stride=dil)


def _dil_each_block(S, dil, block):
    nb = S // dil // DIL_BLK
    if dil == 1:
        block(0, 0, True)
        _unrolled_loop(1, nb, lambda n: block(0, n, False))
    else:
        for r in range(dil):
            for n in range(nb):
                block(r, n, n == 0)


def _dil_col(g, part, h):
    return (g * 3 + part) * DIL_HG + h


def _dil_fwd_group(dqkv, bias_g, g, dil, name):
    S = dqkv.shape[0]

    def body(bias_ref, q_ref, k_ref, v_ref, o_ref, lse_ref):
        def block(r, n, first):
            cur = _dil_rows(r, n, 1, dil)
            both = cur if first else _dil_rows(r, n - 1, 2, dil)
            b = bias_ref[0][:, DIL_BLK:] if first else bias_ref[0]
            q, kk, vv = q_ref[cur, :].astype(BF16), k_ref[both, :].astype(BF16), v_ref[both, :].astype(BF16)
            s = _dot(q, kk, NT) * DIL_SCALE + b
            m = jnp.max(s, axis=1, keepdims=True)
            e = jnp.exp(s - m)
            l = jnp.sum(e, axis=1, keepdims=True)
            p = e * (1.0 / l)
            o_ref[cur, :] = _dot(p.astype(BF16), vv)
            lse_ref[cur, :] = jnp.broadcast_to(m + jnp.log(l), (DIL_BLK, 128))

        _dil_each_block(S, dil, block)

    def col(part):
        return pl.BlockSpec((S, DIL_HD), lambda h: (0, _dil_col(g, part, h)))

    out = pl.BlockSpec((S, DIL_HD), lambda h: (0, h))
    return pl.pallas_call(
        body,
        name=name,
        grid=(DIL_HG,),
        in_specs=[pl.BlockSpec((1, DIL_BLK, 2 * DIL_BLK), lambda h: (h, 0, 0)), col(0), col(1), col(2)],
        out_specs=[out, out],
        out_shape=[jax.ShapeDtypeStruct((S, DIL_OUT), F32), jax.ShapeDtypeStruct((S, DIL_OUT), F32)],
        compiler_params=pltpu.CompilerParams(dimension_semantics=("parallel",)),
    )(bias_g, dqkv, dqkv, dqkv)


def _dil_combine(os_, ls_, name, tr=512):
    S = os_[0].shape[0]

    def body(o0, o1, o2, l0, l1, l2, out_ref, lse_ref):
        a, b, c = l0[...], l1[...], l2[...]
        m = jnp.maximum(jnp.maximum(a, b), c)
        ea, eb, ec = jnp.exp(a - m), jnp.exp(b - m), jnp.exp(c - m)
        den = ea + eb + ec
        inv = 1.0 / den
        out_ref[...] = (ea * inv) * o0[...] + (eb * inv) * o1[...] + (ec * inv) * o2[...]
        lse_ref[...] = m + jnp.log(den)

    row = pl.BlockSpec((tr, DIL_OUT), lambda i: (i, 0))
    return pl.pallas_call(
        body,
        name=name,
        grid=(S // tr,),
        in_specs=[row] * 6,
        out_specs=[row, row],
        out_shape=[jax.ShapeDtypeStruct((S, DIL_OUT), F32)] * 2,
        compiler_params=pltpu.CompilerParams(dimension_semantics=("parallel",)),
    )(*os_, *ls_)


def _dil_rowdot(dod, od, name, tr=512):
    S = dod.shape[0]

    def body(d_ref, o_ref, dd_ref):
        for h in range(DIL_HG):
            sl = slice(h * 128, (h + 1) * 128)
            sm = jnp.sum(d_ref[:, sl] * o_ref[:, sl], axis=1, keepdims=True)
            dd_ref[:, sl] = jnp.broadcast_to(sm, (tr, 128))

    row = pl.BlockSpec((tr, DIL_OUT), lambda i: (i, 0))
    return pl.pallas_call(
        body,
        name=name,
        grid=(S // tr,),
        in_specs=[row, row],
        out_specs=row,
        out_shape=jax.ShapeDtypeStruct((S, DIL_OUT), F32),
        compiler_params=pltpu.CompilerParams(dimension_semantics=("parallel",)),
    )(dod, od)


def _dil_bwd_group(dqkv, bias_g, dod, dd, lse, grads, g, dil, name):
    S = dqkv.shape[0]

    def body(bias_ref, q_ref, k_ref, v_ref, do_ref, dd_ref, lse_ref, _, out_ref):
        out_ref[1] = jnp.zeros((S, DIL_HD), F32)
        out_ref[2] = jnp.zeros((S, DIL_HD), F32)

        def block(r, n, first):
            cur = _dil_rows(r, n, 1, dil)
            both = cur if first else _dil_rows(r, n - 1, 2, dil)
            b = bias_ref[0][:, DIL_BLK:] if first else bias_ref[0]
            q, kk, vv = q_ref[cur, :].astype(BF16), k_ref[both, :].astype(BF16), v_ref[both, :].astype(BF16)
            do = do_ref[cur, :].astype(BF16)
            s = _dot(q, kk, NT) * DIL_SCALE + b
            p = jnp.exp(s - lse_ref[cur, 0:1])
            dp = _dot(do, vv, NT)
            ds = ((p * (dp - dd_ref[cur, 0:1])) * DIL_SCALE).astype(BF16)
            out_ref[0, cur, :] = _dot(ds, kk)
            out_ref[1, both, :] += _dot(ds, q, TN)
            out_ref[2, both, :] += _dot(p.astype(BF16), do, TN)

        _dil_each_block(S, dil, block)

    def col(part):
        return pl.BlockSpec((S, DIL_HD), lambda h: (0, _dil_col(g, part, h)))

    nat = pl.BlockSpec((S, DIL_HD), lambda h: (0, h))
    return pl.pallas_call(
        body,
        name=name,
        grid=(DIL_HG,),
        in_specs=[pl.BlockSpec((1, DIL_BLK, 2 * DIL_BLK), lambda h: (h, 0, 0)), col(0), col(1), col(2), nat, nat, nat, ANY],
        out_specs=pl.BlockSpec((3, S, DIL_HD), lambda h: (g, 0, h)),
        out_shape=jax.ShapeDtypeStruct(grads.shape, F32),
        input_output_aliases={7: 0},
        compiler_params=pltpu.CompilerParams(dimension_semantics=("parallel",)),
    )(bias_g, dqkv, dqkv, dqkv, dod, dd, lse, grads)


def _merge_fwd(gates, o_a, o_b, name, tr=256):
    S = o_a.shape[0]

    def body(ga_ref, gb_ref, oa_ref, ob_ref, m_ref):
        m_ref[...] = (ga_ref[...] * oa_ref[...] + gb_ref[...] * ob_ref[...]).astype(BF16)

    row = pl.BlockSpec((tr, D_MODEL), lambda i: (i, 0))
    return pl.pallas_call(
        body,
        name=name,
        grid=(S // tr,),
        in_specs=[row, pl.BlockSpec((tr, D_MODEL), lambda i: (i, 1)), row, row],
        out_specs=row,
        out_shape=jax.ShapeDtypeStruct((S, D_MODEL), BF16),
        compiler_params=pltpu.CompilerParams(dimension_semantics=("parallel",)),
    )(gates, gates, o_a, o_b)


def _merge_bwd(dmrg, gates, o_a, o_b, name, tr=256):
    S = o_a.shape[0]

    def body(dm_ref, ga_ref, gb_ref, oa_ref, ob_ref, doa_ref, dob_ref, dga_ref, dgb_ref, dba_ref, dbb_ref):
        dm, ga, gb = dm_ref[...], ga_ref[...], gb_ref[...]
        doa_ref[...] = (dm * ga).astype(BF16)
        dob_ref[...] = (dm * gb).astype(BF16)
        dga = (dm * oa_ref[...]) * (ga * (1.0 - ga))
        dgb = (dm * ob_ref[...]) * (gb * (1.0 - gb))
        dga_ref[...] = dga.astype(BF16)
        dgb_ref[...] = dgb.astype(BF16)
        sa = jnp.sum(dga, axis=0, keepdims=True)
        sb = jnp.sum(dgb, axis=0, keepdims=True)

        @pl.when(pl.program_id(0) == 0)
        def _():
            dba_ref[...] = sa
            dbb_ref[...] = sb

        @pl.when(pl.program_id(0) != 0)
        def _():
            dba_ref[...] += sa
            dbb_ref[...] += sb

    row = pl.BlockSpec((tr, D_MODEL), lambda i: (i, 0))
    row1 = pl.BlockSpec((tr, D_MODEL), lambda i: (i, 1))
    vec = pl.BlockSpec((1, D_MODEL), lambda i: (0, 0))
    outs = pl.pallas_call(
        body,
        name=name,
        grid=(S // tr,),
        in_specs=[row, row, row1, row, row],
        out_specs=[row, row, row, row, vec, vec],
        out_shape=[jax.ShapeDtypeStruct((S, D_MODEL), BF16)] * 4 + [jax.ShapeDtypeStruct((1, D_MODEL), F32)] * 2,
        compiler_params=pltpu.CompilerParams(dimension_semantics=("arbitrary",)),
    )(dmrg, gates, gates, o_a, o_b)
    return outs


CONV_TR = 512
CONV_TC = 512
N_FFC = D_FF_PAD // CONV_TC


def _conv_taps(x, before, w_ref, b_ref):
    x0 = jnp.concatenate([before, x], axis=0)
    x1 = pltpu.roll(x0, 1, 0)
    x2 = pltpu.roll(x0, 2, 0)
    u = ((b_ref[...] + w_ref[0:1, :] * x2) + w_ref[1:2, :] * x1) + w_ref[2:3, :] * x0
    return u, x0, x1, x2


def _prev_halo(tr):
    return lambda i, j: (jnp.maximum(i * (tr // 8) - 1, 0), j)


def _ffn_fwd(u0, cw, cb, name):
    S = u0.shape[0]
    tr, tc = CONV_TR, CONV_TC

    def body(up_ref, gt_ref, hup_ref, hgt_ref, wu_ref, wg_ref, bu_ref, bg_ref, a_ref):
        live = (pl.program_id(0) > 0).astype(F32)
        up = _conv_taps(up_ref[...], hup_ref[...] * live, wu_ref, bu_ref)[0][8:]
        gt = _conv_taps(gt_ref[...], hgt_ref[...] * live, wg_ref, bg_ref)[0][8:]
        a_ref[...] = ((gt * jax.nn.sigmoid(gt)) * up).astype(BF16)

    return pl.pallas_call(
        body,
        name=name,
        grid=(S // tr, N_FFC),
        in_specs=[
            pl.BlockSpec((tr, tc), lambda i, j: (i, j)),
            pl.BlockSpec((tr, tc), lambda i, j: (i, j + N_FFC)),
            pl.BlockSpec((8, tc), _prev_halo(tr)),
            pl.BlockSpec((8, tc), lambda i, j: (jnp.maximum(i * (tr // 8) - 1, 0), j + N_FFC)),
            pl.BlockSpec((8, tc), lambda i, j: (0, j)),
            pl.BlockSpec((8, tc), lambda i, j: (0, j + N_FFC)),
            pl.BlockSpec((1, tc), lambda i, j: (0, j)),
            pl.BlockSpec((1, tc), lambda i, j: (0, j + N_FFC)),
        ],
        out_specs=pl.BlockSpec((tr, tc), lambda i, j: (i, j)),
        out_shape=jax.ShapeDtypeStruct((S, D_FF_PAD), BF16),
        compiler_params=pltpu.CompilerParams(dimension_semantics=("parallel", "parallel")),
    )(u0, u0, u0, u0, cw, cw, cb, cb)


def _ffn_bwd(u0, da, cw, cb, name):
    S = u0.shape[0]
    tr, tc = CONV_TR, CONV_TC
    nrow, te = S // tr, tr + 8

    def body(up_ref, gt_ref, hup_ref, hgt_ref, nup_ref, ngt_ref, da_ref, nda_ref, wu_ref, wg_ref, bu_ref, bg_ref, du0_ref, dcw_ref, dcb_ref):
        i = pl.program_id(1)
        prev_live = (i > 0).astype(F32)
        next_live = (i < nrow - 1).astype(F32)

        def conv(x_ref, nx_ref, h_ref, w_ref, b_ref):
            x = jnp.concatenate([x_ref[...], nx_ref[...] * next_live], axis=0)
            return [t[8:] for t in _conv_taps(x, h_ref[...] * prev_live, w_ref, b_ref)]

        up, xu0, xu1, xu2 = conv(up_ref, nup_ref, hup_ref, wu_ref, bu_ref)
        gt, xg0, xg1, xg2 = conv(gt_ref, ngt_ref, hgt_ref, wg_ref, bg_ref)
        da_v = jnp.concatenate([da_ref[...], nda_ref[...] * next_live], axis=0)
        sg = jax.nn.sigmoid(gt)
        d_up = da_v * (gt * sg)
        d_gt = (da_v * up) * (sg * (1.0 + gt * (1.0 - sg)))
        tap = lax.broadcasted_iota(jnp.int32, (8, tc), 0)

        def finish(half, du, x0, x1, x2, w_ref):
            n1 = pltpu.roll(du, te - 1, 0)
            n2 = pltpu.roll(du, te - 2, 0)
            du0 = (w_ref[2:3, :] * du + w_ref[1:2, :] * n1) + w_ref[0:1, :] * n2
            du0_ref[half] = du0[:tr].astype(BF16)
            d = du[:tr]
            dcw = jnp.where(
                tap == 0,
                jnp.sum(d * x2[:tr], axis=0, keepdims=True),
                jnp.where(tap == 1, jnp.sum(d * x1[:tr], axis=0, keepdims=True), jnp.where(tap == 2, jnp.sum(d * x0[:tr], axis=0, keepdims=True), 0.0)),
            )
            dcb = jnp.sum(d, axis=0, keepdims=True)

            @pl.when(i == 0)
            def _():
                dcw_ref[half] = dcw
                dcb_ref[half] = dcb

            @pl.when(i != 0)
            def _():
                dcw_ref[half] += dcw
                dcb_ref[half] += dcb

        finish(0, d_up, xu0, xu1, xu2, wu_ref)
        finish(1, d_gt, xg0, xg1, xg2, wg_ref)

    def prev8(off):
        return pl.BlockSpec((8, tc), lambda j, i: (jnp.maximum(i * (tr // 8) - 1, 0), j + off))

    def next8(off):
        return pl.BlockSpec((8, tc), lambda j, i: (jnp.minimum((i + 1) * (tr // 8), S // 8 - 1), j + off))

    return pl.pallas_call(
        body,
        name=name,
        grid=(N_FFC, nrow),
        in_specs=[
            pl.BlockSpec((tr, tc), lambda j, i: (i, j)),
            pl.BlockSpec((tr, tc), lambda j, i: (i, j + N_FFC)),
            prev8(0),
            prev8(N_FFC),
            next8(0),
            next8(N_FFC),
            pl.BlockSpec((tr, tc), lambda j, i: (i, j)),
            next8(0),
            pl.BlockSpec((8, tc), lambda j, i: (0, j)),
            pl.BlockSpec((8, tc), lambda j, i: (0, j + N_FFC)),
            pl.BlockSpec((1, tc), lambda j, i: (0, j)),
            pl.BlockSpec((1, tc), lambda j, i: (0, j + N_FFC)),
        ],
        out_specs=[
            pl.BlockSpec((2, tr, tc), lambda j, i: (0, i, j)),
            pl.BlockSpec((2, 8, tc), lambda j, i: (0, 0, j)),
            pl.BlockSpec((2, 1, tc), lambda j, i: (0, 0, j)),
        ],
        out_shape=[
            jax.ShapeDtypeStruct((2, S, D_FF_PAD), BF16),
            jax.ShapeDtypeStruct((2, 8, D_FF_PAD), F32),
            jax.ShapeDtypeStruct((2, 1, D_FF_PAD), F32),
        ],
        compiler_params=pltpu.CompilerParams(dimension_semantics=("parallel", "arbitrary")),
    )(u0, u0, u0, u0, u0, u0, da, da, cw, cw, cb, cb)


def _adamw(w, g, m, v, name):
    R, C = w.shape
    tr = R
    for cand in (256, 128, 64, 32, 16, 8):
        if R % cand == 0 and R > cand:
            tr = cand
            break

    def body(w_ref, g_ref, m_ref, v_ref, d_ref, nm_ref, nv_ref):
        gv = g_ref[...]
        nm = ADAM_B1 * m_ref[...] + (1.0 - ADAM_B1) * gv
        nv = ADAM_B2 * v_ref[...] + (1.0 - ADAM_B2) * (gv * gv)
        m_hat = nm / (1.0 - ADAM_B1**ADAM_STEP)
        v_hat = nv / (1.0 - ADAM_B2**ADAM_STEP)
        d_ref[...] = -ADAM_LR * (m_hat / (jnp.sqrt(v_hat) + ADAM_EPS) + ADAM_WD * w_ref[...])
        nm_ref[...] = nm
        nv_ref[...] = nv

    blk = pl.BlockSpec((tr, C), lambda i: (i, 0))
    return pl.pallas_call(
        body,
        name=name,
        grid=(R // tr,),
        in_specs=[blk] * 4,
        out_specs=[blk] * 3,
        out_shape=[jax.ShapeDtypeStruct((R, C), F32)] * 3,
        compiler_params=pltpu.CompilerParams(dimension_semantics=("parallel",)),
    )(w, g, m, v)


ANY = pl.BlockSpec(memory_space=pl.ANY)


def _all_gather(blocks, name):
    n = len(blocks)

    def body(*refs):
        x_refs, out_refs = refs[:n], refs[n : 2 * n]
        send_sems, recv_sems, local_sems = refs[2 * n :]
        x, y, c = lax.axis_index("x"), lax.axis_index("y"), lax.axis_index("c")
        me, sibling = (x, y, c), (x, y, 1 - c)
        chips = [(1 - x, y), (x, 1 - y), (1 - x, 1 - y)]

        def slot(a, px, py, pc):
            return out_refs[a].at[4 * px + 2 * py + pc]

        def copy(a, k, blk, to, src=None):
            return pltpu.make_async_remote_copy(
                src_ref=slot(a, *blk) if src is None else src,
                dst_ref=slot(a, *blk),
                send_sem=send_sems.at[7 * a + k],
                recv_sem=recv_sems.at[7 * a + k],
                device_id=to,
                device_id_type=MESH,
            )

        mine = [pltpu.make_async_copy(x_refs[a], slot(a, *me), local_sems.at[a]) for a in range(n)]
        sent = []
        for a in range(n):
            mine[a].start()
            first = [copy(a, 0, me, sibling, src=x_refs[a])]
            first += [copy(a, 1 + j, me, (*chip, c), src=x_refs[a]) for j, chip in enumerate(chips)]
            for cp in first:
                cp.start()
            sent += first
        for a in range(n):
            for j, chip in enumerate(chips):
                copy(a, 1 + j, (*chip, c), me).wait_recv()
                passed = copy(a, 4 + j, (*chip, c), sibling)
                passed.start()
                sent.append(passed)
        for a in range(n):
            copy(a, 0, sibling, me).wait_recv()
            for j, chip in enumerate(chips):
                copy(a, 4 + j, (*chip, 1 - c), me).wait_recv()
        for cp in sent:
            cp.wait_send()
        for cp in mine:
            cp.wait()

    return pl.pallas_call(
        body,
        name=name,
        out_shape=[jax.ShapeDtypeStruct((N_DEV,) + b.shape, b.dtype) for b in blocks],
        in_specs=[ANY] * n,
        out_specs=[ANY] * n,
        scratch_shapes=[pltpu.SemaphoreType.DMA((7 * n,)), pltpu.SemaphoreType.DMA((7 * n,)), pltpu.SemaphoreType.DMA((n,))],
    )(*blocks)


def _pair_exchange(gs, name):
    n = len(gs)

    def body(*refs):
        g_refs, out_refs = refs[:n], refs[n : 2 * n]
        send_sems, recv_sems = refs[2 * n :]
        x, y, c = lax.axis_index("x"), lax.axis_index("y"), lax.axis_index("c")
        copies = [
            pltpu.make_async_remote_copy(
                src_ref=g_refs[a].at[2 * k + (1 - c)],
                dst_ref=out_refs[a].at[k],
                send_sem=send_sems.at[N_CHIP * a + k],
                recv_sem=recv_sems.at[N_CHIP * a + k],
                device_id=(x, y, 1 - c),
                device_id_type=MESH,
            )
            for a in range(n)
            for k in range(N_CHIP)
        ]
        for cp in copies:
            cp.start()
        for cp in copies:
            cp.wait()

    return pl.pallas_call(
        body,
        name=name,
        out_shape=[jax.ShapeDtypeStruct((N_CHIP,) + g.shape[1:], g.dtype) for g in gs],
        in_specs=[ANY] * n,
        out_specs=[ANY] * n,
        scratch_shapes=[pltpu.SemaphoreType.DMA((N_CHIP * n,)), pltpu.SemaphoreType.DMA((N_CHIP * n,))],
    )(*gs)


def _row_tile(rows):
    return max(t for t in range(16, 353, 16) if rows % t == 0)


def _pair_add(g, recv, core, name):
    _, R, C = g.shape
    tr = _row_tile(R)

    def body(core_ref, g_ref, r_ref, o_ref):
        o_ref[...] = (g_ref[...].astype(F32) + r_ref[...].astype(F32)).astype(o_ref.dtype)

    return pl.pallas_call(
        body,
        name=name,
        grid_spec=pltpu.PrefetchScalarGridSpec(
            num_scalar_prefetch=1,
            grid=(N_CHIP, R // tr),
            in_specs=[
                pl.BlockSpec((1, tr, C), lambda k, i, core_ref: (2 * k + core_ref[0], i, 0)),
                pl.BlockSpec((1, tr, C), lambda k, i, core_ref: (k, i, 0)),
            ],
            out_specs=pl.BlockSpec((1, tr, C), lambda k, i, core_ref: (k, i, 0)),
        ),
        out_shape=jax.ShapeDtypeStruct((N_CHIP, R, C), g.dtype),
        compiler_params=pltpu.CompilerParams(dimension_semantics=("parallel", "parallel")),
    )(core, g, recv)


HBM = pl.BlockSpec(memory_space=pltpu.HBM)
SEM = pl.BlockSpec(memory_space=pltpu.SEMAPHORE)
EFFECT = pltpu.SideEffectType.DATAFLOW_SIDE_EFFECTING
RELATIONS = tuple((dx, dy, dc) for dx in (0, 1) for dy in (0, 1) for dc in (0, 1))[1:]


def _related(rel):
    x, y, c = lax.axis_index("x"), lax.axis_index("y"), lax.axis_index("c")
    return (1 - x if rel[0] else x, 1 - y if rel[1] else y, 1 - c if rel[2] else c)


def _dev_index(pos):
    return 4 * pos[0] + 2 * pos[1] + pos[2]


def _peers(chips):
    if chips:
        return [r for r in RELATIONS if not r[2]], N_CHIP, lambda pos: 2 * pos[0] + pos[1]
    return list(RELATIONS), N_DEV, _dev_index


def _exchange_start(srcs, by_slot, after, name, chips=False):
    n = len(srcs)
    rels, slots, slot_of = _peers(chips)
    lands = [lax.empty((slots,) + (s.shape[1:] if by_slot else s.shape), s.dtype) for s in srcs]
    nsem = len(rels) * n

    def body(*refs):
        src_refs, land_refs = refs[:n], refs[n : 2 * n]
        send_sems, recv_sems = refs[2 * n + 1], refs[2 * n + 2]
        token = refs[-1]
        me = slot_of(_related((0, 0, 0)))
        for a in range(n):
            for k, rel in enumerate(rels):
                peer = _related(rel)
                pltpu.make_async_remote_copy(
                    src_ref=src_refs[a].at[slot_of(peer)] if by_slot else src_refs[a],
                    dst_ref=land_refs[a].at[me],
                    send_sem=send_sems.at[len(rels) * a + k],
                    recv_sem=recv_sems.at[len(rels) * a + k],
                    device_id=peer,
                    device_id_type=MESH,
                ).start()
        token[...] = jnp.zeros_like(token)

    def hbm(a):
        return pltpu.HBM(a.shape, a.dtype)

    outs = pl.pallas_call(
        body,
        name=name,
        out_shape=(
            pltpu.SemaphoreType.DMA((nsem,)),
            pltpu.SemaphoreType.DMA((nsem,)),
            *[hbm(s) for s in srcs],
            *[hbm(l) for l in lands],
            jax.ShapeDtypeStruct((8, 128), F32),
        ),
        in_specs=[HBM] * (2 * n) + [ANY],
        out_specs=(SEM, SEM, *[HBM] * (2 * n), pl.BlockSpec(memory_space=pltpu.VMEM)),
        input_output_aliases={i: 2 + i for i in range(2 * n)},
        compiler_params=pltpu.CompilerParams(has_side_effects=EFFECT),
    )(*[pltpu.with_memory_space_constraint(a, pltpu.HBM) for a in list(srcs) + lands], after)
    return (outs[0], outs[1], list(outs[2 : 2 + n]), list(outs[2 + n : 2 + 2 * n])), outs[-1]


def _exchange_wait(handle, by_slot, after, name, chips=False):
    send_sems, recv_sems, srcs, lands = handle
    n = len(srcs)
    rels = _peers(chips)[0]

    def body(*refs):
        src_refs, land_refs = refs[:n], refs[n : 2 * n]
        s_sems, r_sems = refs[2 * n], refs[2 * n + 1]
        for a in range(n):
            for k, rel in enumerate(rels):
                copy = pltpu.make_async_remote_copy(
                    src_ref=src_refs[a].at[0] if by_slot else src_refs[a],
                    dst_ref=land_refs[a].at[0],
                    send_sem=s_sems.at[len(rels) * a + k],
                    recv_sem=r_sems.at[len(rels) * a + k],
                    device_id=_related(rel),
                    device_id_type=MESH,
                )
                copy.wait_send()
                copy.wait_recv()

    outs = pl.pallas_call(
        body,
        name=name,
        out_shape=tuple(pltpu.HBM(a.shape, a.dtype) for a in srcs + lands),
        in_specs=[HBM] * (2 * n) + [SEM, SEM, ANY],
        out_specs=tuple([HBM] * (2 * n)),
        input_output_aliases={i: i for i in range(2 * n)},
        compiler_params=pltpu.CompilerParams(has_side_effects=EFFECT),
    )(*srcs, *lands, send_sems, recv_sems, after)
    return list(outs[:n]), list(outs[n:])


def _slot_sum(parts, name):
    n, R, C = parts.shape
    tr = _row_tile(R) if R % 16 == 0 else R

    def body(p_ref, o_ref):
        acc = p_ref[0].astype(F32)
        for k in range(1, n):
            acc = acc + p_ref[k].astype(F32)
        o_ref[...] = acc

    return pl.pallas_call(
        body,
        name=name,
        grid=(R // tr,),
        in_specs=[pl.BlockSpec((n, tr, C), lambda i: (0, i, 0))],
        out_specs=pl.BlockSpec((tr, C), lambda i: (i, 0)),
        out_shape=jax.ShapeDtypeStruct((R, C), F32),
        compiler_params=pltpu.CompilerParams(dimension_semantics=("parallel",)),
    )(parts)


W_IN_TC = 256
W_IN_BOUNDS = (0, LAT, LAT + 3 * DIL_QKV, LAT + 3 * DIL_QKV + D_MODEL, D_IN)


def _dqkv_chunks():
    return [((g * 3 + part) * DIL_OUT, LAT + part * DIL_QKV + g * DIL_OUT) for g in range(DIL_GROUPS) for part in range(3)]


def _w_in_regroup(slots, name):
    tc = W_IN_TC

    def body(s_ref, lat_ref, dqkv_ref, g_ref, buf):
        for j in range(N_DEV):
            buf[j * IN_ROWS : (j + 1) * IN_ROWS, :] = s_ref[j].astype(F32)[:IN_ROWS, :]
        lat_ref[:LAT, :] = buf[:LAT, :].astype(BF16)
        lat_ref[LAT:, :] = jnp.zeros((LAT_PAD - LAT, tc), BF16)
        for dst, src in _dqkv_chunks():
            dqkv_ref[dst : dst + DIL_OUT, :] = buf[src : src + DIL_OUT, :].astype(BF16)
        g_ref[...] = buf[W_IN_BOUNDS[2] :, :].astype(BF16)

    def col(rows):
        return pl.BlockSpec((rows, tc), lambda k: (0, k))

    return pl.pallas_call(
        body,
        name=name,
        grid=(D_MODEL // tc,),
        in_specs=[pl.BlockSpec((N_DEV, IN_ROWS_PAD, tc), lambda k: (0, 0, k))],
        out_specs=[col(LAT_PAD), col(3 * DIL_QKV), col(2 * D_MODEL)],
        out_shape=[
            jax.ShapeDtypeStruct((LAT_PAD, D_MODEL), BF16),
            jax.ShapeDtypeStruct((3 * DIL_QKV, D_MODEL), BF16),
            jax.ShapeDtypeStruct((2 * D_MODEL, D_MODEL), BF16),
        ],
        scratch_shapes=[pltpu.VMEM((D_IN, tc), F32)],
        compiler_params=pltpu.CompilerParams(dimension_semantics=("parallel",)),
    )(slots)


def _w_in_grad_regroup(g_lat, g_dqkv, g_ga, g_gb, name):
    tc = W_IN_TC

    def body(lat_ref, dqkv_ref, ga_ref, gb_ref, o_ref, buf):
        b = W_IN_BOUNDS
        buf[b[0] : b[1], :] = lat_ref[:LAT, :].astype(F32)
        for dst, src in _dqkv_chunks():
            buf[src : src + DIL_OUT, :] = dqkv_ref[dst : dst + DIL_OUT, :].astype(F32)
        buf[b[2] : b[3], :] = ga_ref[...].astype(F32)
        buf[b[3] : b[4], :] = gb_ref[...].astype(F32)
        fill = jnp.zeros((IN_ROWS_PAD - IN_ROWS, tc), F32)
        for j in range(N_DEV):
            o_ref[j] = jnp.concatenate([buf[j * IN_ROWS : (j + 1) * IN_ROWS, :], fill], axis=0).astype(BF16)

    def col(rows):
        return pl.BlockSpec((rows, tc), lambda k: (0, k))

    return pl.pallas_call(
        body,
        name=name,
        grid=(D_MODEL // tc,),
        in_specs=[col(LAT_PAD), col(3 * DIL_QKV), col(D_MODEL), col(D_MODEL)],
        out_specs=pl.BlockSpec((N_DEV, IN_ROWS_PAD, tc), lambda k: (0, 0, k)),
        out_shape=jax.ShapeDtypeStruct((N_DEV, IN_ROWS_PAD, D_MODEL), BF16),
        scratch_shapes=[pltpu.VMEM((D_IN, tc), F32)],
        compiler_params=pltpu.CompilerParams(dimension_semantics=("parallel",)),
    )(g_lat, g_dqkv, g_ga, g_gb)


def _ffn_pad(a, axis):
    a = jnp.moveaxis(a, axis, -1)
    g = a.reshape(a.shape[:-1] + (2 * N_DEV, FF_GROUP))
    g = jnp.pad(g, [(0, 0)] * (g.ndim - 1) + [(0, FF_GROUP_PAD - FF_GROUP)])
    return jnp.moveaxis(g.reshape(a.shape[:-1] + (2 * D_FF_PAD,)), -1, axis)


def _ffn_unpad(a, axis):
    a = jnp.moveaxis(a, axis, -1)
    g = a.reshape(a.shape[:-1] + (2 * N_DEV, FF_GROUP_PAD))[..., :FF_GROUP]
    return jnp.moveaxis(g.reshape(a.shape[:-1] + (2 * D_FF,)), -1, axis)


MISC = (("w_o_mla", (256, 1024)), ("w_o_dil", (256, 512)), ("w_uq", (192, 512)), ("w_ukv", (256, 256)))


def _exchange_blocks(w):
    def t(a):
        return a.astype(BF16).T

    up = t(w["w_up"]).reshape(2, FF_GROUP, D_MODEL)
    return [
        jnp.pad(t(w["w_in"]), ((0, IN_ROWS_PAD - IN_ROWS), (0, 0))),
        jnp.pad(up, ((0, 0), (0, FF_GROUP_PAD - FF_GROUP), (0, 0))).reshape(2 * FF_GROUP_PAD, D_MODEL),
        jnp.pad(w["w_down"].astype(BF16), ((0, FF_GROUP_PAD - FF_GROUP), (0, 0))),
        w["w_out"].astype(BF16),
        jnp.concatenate([t(w[n]).reshape(-1, D_MODEL) for n, _ in MISC], axis=0),
    ]


def _misc_split(misc):
    out, off = {}, 0
    for n, (r, c) in MISC:
        rows = r * c // D_MODEL
        out[n] = misc[..., off : off + rows, :].reshape(misc.shape[:-2] + (r, c))
        off += rows
    return out


def _small_matrices(g_misc):
    misc = _misc_split(g_misc)
    uq_t = jnp.pad(misc["w_uq"], ((0, 0), (0, HEAD_PAD - QK_NOPE - QK_ROPE), (0, 0)))
    return {
        "uq_t": uq_t.reshape(MLA_HEADS * HEAD_PAD, Q_LORA),
        "ukv_t": misc["w_ukv"].reshape(MLA_HEADS * HEAD_PAD, KV_LORA),
        "o_mla_t": misc["w_o_mla"].reshape(D_MODEL, MLA_HEADS * V_HEAD),
        "o_dil_t": misc["w_o_dil"].reshape(D_MODEL, DIL_OUT),
    }


def _small_grad_blocks(g):
    uq_t = g["uq_t"].reshape(MLA_HEADS, HEAD_PAD, Q_LORA)[:, : QK_NOPE + QK_ROPE]
    misc = {"w_o_mla": g["o_mla_t"], "w_o_dil": g["o_dil_t"], "w_uq": uq_t, "w_ukv": g["ukv_t"]}
    return [
        g["w_out"].reshape(N_DEV, -1, D_MODEL),
        jnp.concatenate([misc[n].reshape(N_DEV, -1, D_MODEL) for n, _ in MISC], axis=1),
    ]


def _grad_shards(sums):
    s_in, s_out, s_misc, s_up, s_down = sums
    out = {
        "w_in": s_in[:IN_ROWS].T,
        "w_up": s_up.reshape(2, FF_GROUP_PAD, D_MODEL)[:, :FF_GROUP].reshape(2 * FF_GROUP, D_MODEL).T,
        "w_down": s_down[:FF_GROUP],
        "w_out": s_out,
    }
    out.update({n: v.T for n, v in _misc_split(s_misc).items()})
    return out


def _local_step(x, tgt, wt, conv_w, small, small_matrices, ffn_weight, send_ffn_grads, send_small_grads, send_w_in_grads, start_token):
    S = x.shape[0]
    lat_t, dqkv_t, g_t = wt
    cw = jnp.pad(_ffn_pad(conv_w, 1), ((0, 5), (0, 0)))
    cb = _ffn_pad(small["conv_b"], 1)
    cos_t, sin_t = _rope_tables(S)
    bias = _dil_bias()
    g1, g2, g3 = small["attn_norm_g"], small["ffn_norm_g"], small["final_norm_g"]
    gq, gkv = small["q_norm_g"], small["kv_norm_g"]

    h = _rms_fwd(x, g1 + start_token, "rms_attn")
    lat = _mm(h, lat_t, "nt", F32, 1024, LAT_PAD, D_MODEL, "proj_lat")
    dqkv = _mm(h, dqkv_t, "nt", F32, 1024, 512, D_MODEL, "proj_dqkv")
    gates = _mm(h, g_t, "nt", F32, 1024, 512, D_MODEL, "proj_gates", bias=small["b_gate"], act="sigmoid")
    sm = small_matrices(lat)
    uq_t, ukv_t, o_mla_t, o_dil_t = sm["uq_t"], sm["ukv_t"], sm["o_mla_t"], sm["o_dil_t"]
    cqn, ckvn, kpe = _mla_prep1(lat, gq, gkv, cos_t, sin_t, "mla_prep1")
    q_raw = _mm(cqn, uq_t, "nt", F32, 1024, 1024, Q_LORA, "mla_uq")
    kv = _mm(ckvn, ukv_t, "nt", BF16, 1024, 1024, KV_LORA, "mla_ukv")
    q_att, k_att = _mla_prep2(q_raw, kv, kpe, cos_t, sin_t, "mla_prep2")
    o, lse = _flash2_fwd(q_att, k_att, kv, "mla_flash_fwd")
    o_a = _mm(o, o_mla_t, "nt", F32, 1024, 1024, MLA_HEADS * V_HEAD, "mla_out")

    d_os, d_ls = [], []
    for g, (_, dil) in enumerate(DIL_PATTERNS):
        og, lg = _dil_fwd_group(dqkv, bias[g], g, dil, f"dil_fwd_{g}")
        d_os.append(og)
        d_ls.append(lg)
    od, dil_lse = _dil_combine(d_os, d_ls, "dil_combine")
    o_b = _mm(od, o_dil_t, "nt", F32, 1024, 1024, DIL_OUT, "dil_out")

    mrg = _merge_fwd(gates, o_a, o_b, "merge_fwd")
    w_out = ffn_weight("w_out", mrg)
    x1 = _mm(mrg, w_out, "nn", F32, 1024, 1024, D_MODEL, "mix_out", res=x)
    h2 = _rms_fwd(x1, g2, "rms_ffn")
    up_t = ffn_weight("up_t", h2)
    u0 = _mm(h2, up_t, "nt", F32, 1024, 512, D_MODEL, "ffn_up")
    a = _ffn_fwd(u0, cw, cb, "ffn_conv_fwd")
    w_down = ffn_weight("w_down", a)
    x2 = _mm(a, w_down, "nn", F32, 1024, 512, D_FF_PAD // 2, "ffn_down", res=x1)
    loss_part, dx2, dx2b, dg3 = _final_loss(x2, g3, tgt, "final_loss")

    da = _mm(dx2b, w_down, "nt", F32, 1024, 512, D_MODEL, "ffn_down_dx")
    gw_down = _mm(a, dx2b, "tn", BF16, 512, 1024, S, "ffn_down_dw")
    du0, dcw, dcb = _ffn_bwd(u0, da, cw, cb, "ffn_conv_bwd")
    du0 = du0.reshape(2 * S, D_FF_PAD)
    gw_up_t = _mm(du0, h2, "tn", BF16, 512, 1024, S, "ffn_up_dw", a_halves=2)
    sent = send_ffn_grads(gw_up_t, gw_down)
    dh2 = _mm(du0, up_t, "nn", F32, 1024, 1024, D_FF_PAD // 2, "ffn_up_dx", a_halves=2)
    dx1, dx1b, dg2 = _rms_bwd(dh2, x1, g2 + sent, dx2, "rms_ffn_bwd")

    dmrg = _mm(dx1b, w_out, "nt", F32, 1024, 1024, D_MODEL, "mix_out_dx")
    gw_out = _mm(mrg, dx1b, "tn", BF16, 512, 1024, S, "mix_out_dw")
    do_a, do_b, dga, dgb, dba, dbb = _merge_bwd(dmrg, gates, o_a, o_b, "merge_bwd")

    do = _mm(do_a, o_mla_t, "nn", BF16, 1024, 1024, D_MODEL, "mla_out_dx")
    gw_o_mla_t = _mm(do_a, o, "tn", BF16, 1024, 1024, 1024, "mla_out_dw")
    dod = _mm(do_b, o_dil_t, "nn", F32, 1024, DIL_OUT, D_MODEL, "dil_out_dx")
    gw_o_dil_t = _mm(do_b, od, "tn", BF16, 1024, DIL_OUT, 1024, "dil_out_dw")

    dq_att, delta = _flash2_dq(q_att, k_att, kv, o, do, lse, "mla_flash_dq")
    lse_row = lse[:, :, 0][:, None, :]
    delta_row = delta[:, :, 0][:, None, :]
    dk_att, dv = _flash2_dkv(q_att, k_att, kv, do, lse_row, delta_row, "mla_flash_dkv")
    dq_raw, dkv, dkpe = _mla_post(dq_att, dk_att, dv, cos_t, sin_t, "mla_post")
    dcqn = _mm(dq_raw, uq_t, "nn", F32, 1024, Q_LORA, MLA_HEADS * HEAD_PAD, "mla_uq_dx")
    gw_uq_t = _mm(dq_raw, cqn, "tn", BF16, 1024, Q_LORA, 1024, "mla_uq_dw")
    dckvn = _mm(dkv, ukv_t, "nn", F32, 1024, KV_LORA, MLA_HEADS * HEAD_PAD, "mla_ukv_dx")
    gw_ukv_t = _mm(dkv, ckvn, "tn", BF16, 1024, KV_LORA, 1024, "mla_ukv_dw")
    sent = send_small_grads({"uq_t": gw_uq_t, "ukv_t": gw_ukv_t, "o_mla_t": gw_o_mla_t, "o_dil_t": gw_o_dil_t, "w_out": gw_out})
    dlat, dgq, dgkv = _lat_bwd(dcqn, dckvn, dkpe, lat, gq + sent, gkv, "lat_bwd")

    dd = _dil_rowdot(dod, od, "dil_rowdot")
    ddqkv = lax.empty((3 * DIL_GROUPS, S, DIL_OUT), F32)
    for g, (_, dil) in enumerate(DIL_PATTERNS):
        ddqkv = _dil_bwd_group(dqkv, bias[g], dod, dd, dil_lse, ddqkv, g, dil, f"dil_bwd_{g}")
    ddqkv = ddqkv.reshape(3 * DIL_GROUPS * S, DIL_OUT)

    gw_lat_t = _mm(dlat, h, "tn", BF16, LAT_PAD, 1024, S, "proj_lat_dw")
    gw_dqkv_t = _mm(ddqkv, h, "tn", BF16, 512, 1024, S, "proj_dqkv_dw", a_halves=3 * DIL_GROUPS)
    gw_ga_t = _mm(dga, h, "tn", BF16, 512, 1024, S, "proj_ga_dw")
    gw_gb_t = _mm(dgb, h, "tn", BF16, 512, 1024, S, "proj_gb_dw")
    sent = send_w_in_grads(gw_lat_t, gw_dqkv_t, gw_ga_t, gw_gb_t)
    dh = _mm(dlat + sent.astype(BF16), lat_t, "nn", F32, 1024, 1024, LAT_PAD, "proj_lat_dx")
    dh = _mm(ddqkv, dqkv_t, "nn", F32, 512, D_MODEL, DIL_OUT, "proj_dqkv_dx", res=dh, a_halves=3 * DIL_GROUPS)
    dh = _mm(dga, g_t, "nn", F32, 1024, 1024, D_MODEL, "proj_ga_dx", res=dh)
    dh = _mm(dgb, g_t, "nn", F32, 1024, 1024, D_MODEL, "proj_gb_dx", res=dh, b_koff=1)
    grad_x, _, dg1 = _rms_bwd(dh, x, g1, dx1, "rms_attn_bwd")

    small_grads = {
        "attn_norm_g": dg1,
        "b_gate": jnp.concatenate([dba, dbb], axis=1),
        "q_norm_g": dgq,
        "kv_norm_g": dgkv,
        "ffn_norm_g": dg2,
        "conv_b": _ffn_unpad(jnp.concatenate([dcb[0], dcb[1]], axis=1), 1),
        "final_norm_g": dg3,
        "conv_w": _ffn_unpad(jnp.concatenate([dcw[0, :3], dcw[1, :3]], axis=1), 1),
    }
    return loss_part, grad_x, small_grads


SMALL_ORDER = ("attn_norm_g", "b_gate", "q_norm_g", "kv_norm_g", "ffn_norm_g", "conv_b", "final_norm_g", "conv_w")
WEIGHT_ORDER = (
    "attn_norm_g", "w_in", "b_gate", "q_norm_g", "w_uq", "kv_norm_g", "w_ukv", "w_o_mla", "w_o_dil", "w_out",
    "ffn_norm_g", "w_up", "conv_w", "conv_b", "w_down", "final_norm_g",
)


def kernel(x, attn_norm_g, w_in, b_gate, q_norm_g, w_uq, kv_norm_g, w_ukv, w_o_mla, w_o_dil, w_out, ffn_norm_g, w_up, conv_w, conv_b, w_down, final_norm_g, loss_target, m_attn_norm_g, m_w_in, m_b_gate, m_q_norm_g, m_w_uq, m_kv_norm_g, m_w_ukv, m_w_o_mla, m_w_o_dil, m_w_out, m_ffn_norm_g, m_w_up, m_conv_w, m_conv_b, m_w_down, m_final_norm_g, v_attn_norm_g, v_w_in, v_b_gate, v_q_norm_g, v_w_uq, v_kv_norm_g, v_w_ukv, v_w_o_mla, v_w_o_dil, v_w_out, v_ffn_norm_g, v_w_up, v_conv_w, v_conv_b, v_w_down, v_final_norm_g):
    env = dict(locals())
    dev = 4 * lax.axis_index("x") + 2 * lax.axis_index("y") + lax.axis_index("c")
    core = lax.axis_index("c").astype(jnp.int32).reshape(1)

    def two_d(a):
        return a.reshape(-1, a.shape[-1])

    w = {n: two_d(env[n]) for n in WEIGHT_ORDER}
    m = {n: two_d(env["m_" + n]) for n in WEIGHT_ORDER}
    v = {n: two_d(env["v_" + n]) for n in WEIGHT_ORDER}

    chip = 2 * lax.axis_index("x") + lax.axis_index("y")

    def own_slot_in(lands, own, slot=dev):
        return [lax.dynamic_update_slice(l, o[None], (slot, 0, 0)) for l, o in zip(lands, own)]

    b_in, b_up, b_down, b_out, b_misc = _exchange_blocks(w)
    r, c = CONV_SHARD
    conv = jnp.pad(w["conv_w"].reshape(-1), (0, 8 * SMALL_COLS - r * c)).reshape(8, SMALL_COLS)
    g_in, conv = _all_gather([b_in, conv], "ag_w_in")
    misc_gather, started = _exchange_start([b_misc], False, conv, "ag_small_start")
    ffn_gathers, started2 = {}, started
    for key, block in (("w_out", b_out), ("up_t", b_up), ("w_down", b_down)):
        ffn_gathers[key], started2 = _exchange_start([block], False, started2, f"ag_{key}_start")
    wt = _w_in_regroup(g_in, "w_in_regroup")
    conv = conv.reshape(N_DEV, 8 * SMALL_COLS)[:, : r * c].reshape(N_DEV, r, c)
    conv_w_full = conv.transpose(1, 0, 2).reshape(r, N_DEV * c)
    small = {n: w[n] for n in SMALL_ORDER if n != "conv_w"}

    def small_matrices(after):
        own, lands = _exchange_wait(misc_gather, False, after, "ag_small_wait")
        return _small_matrices(own_slot_in(lands, own)[0])

    def ffn_weight(key, after):
        own, lands = _exchange_wait(ffn_gathers[key], False, after, f"ag_{key}_wait")
        return own_slot_in(lands, own)[0].reshape(-1, D_MODEL)

    reduces = {}

    def send_ffn_grads(gw_up_t, gw_down):
        blocks = [gw_up_t.reshape(N_DEV, 2 * FF_GROUP_PAD, D_MODEL), gw_down.reshape(N_DEV, FF_GROUP_PAD, D_MODEL)]
        reduces["ffn"], token = _exchange_start(blocks, True, gw_down, "rs_ffn_start")
        return token[0, 0]

    def send_small_grads(g):
        reduces["small"], token = _exchange_start(_small_grad_blocks(g), True, g["w_out"], "rs_small_start")
        return token[0, 0]

    def send_w_in_grads(g_lat, g_dqkv, g_ga, g_gb):
        e_in = _w_in_grad_regroup(g_lat, g_dqkv, g_ga, g_gb, "w_in_grad_regroup")
        pair = _pair_add(e_in, _pair_exchange([e_in], "rs_w_in_pair_exchange")[0], core, "rs_w_in_pair_add")
        reduces["w_in"], token = _exchange_start([pair], True, pair, "rs_w_in_start", chips=True)
        return token[0, 0]

    loss_part, grad_x, small_grads = _local_step(
        x[0], loss_target[0], wt, conv_w_full, small, small_matrices, ffn_weight,
        send_ffn_grads, send_small_grads, send_w_in_grads, started2[0, 0],
    )
    loss = lax.psum(loss_part[0, 0], AXES)

    def finish(key, by_chip, name):
        sent, lands = _exchange_wait(reduces[key], True, grad_x, name + "_wait", chips=by_chip)
        slot = chip if by_chip else dev
        own = [lax.dynamic_index_in_dim(s, slot, 0, keepdims=False) for s in sent]
        return [_slot_sum(p, f"{name}_sum_{i}") for i, p in enumerate(own_slot_in(lands, own, slot))]

    (s_in,) = finish("w_in", True, "rs_w_in")
    s_out, s_misc = finish("small", False, "rs_small")
    s_up, s_down = finish("ffn", False, "rs_ffn")
    gshard = _grad_shards([s_in, s_out, s_misc, s_up, s_down])

    sflat = jnp.concatenate([small_grads[n].reshape(-1) for n in SMALL_ORDER])
    sflat = jnp.pad(sflat, (0, SMALL_ROWS * SMALL_COLS - sflat.shape[0])).reshape(SMALL_ROWS, SMALL_COLS)
    ssum = _slot_sum(_all_gather([sflat], "ag_small_grads")[0], "small_sum").reshape(-1)
    gsmall, off = {}, 0
    for n in SMALL_ORDER:
        shape = (3, 2 * D_FF) if n == "conv_w" else w[n].shape
        size = shape[0] * shape[1]
        gsmall[n] = ssum[off : off + size].reshape(shape)
        off += size
    gsmall["conv_w"] = lax.dynamic_slice_in_dim(gsmall["conv_w"], dev * CONV_SHARD[1], CONV_SHARD[1], axis=1)

    g_all = {**gshard, **gsmall}
    out_g, out_d, out_m, out_v = [], [], [], []
    for n in WEIGHT_ORDER:
        d, nm, nv = _adamw(w[n], g_all[n], m[n], v[n], "adamw_" + n)
        shape = env[n].shape
        out_g.append(g_all[n].reshape(shape))
        out_d.append(d.reshape(shape))
        out_m.append(nm.reshape(shape))
        out_v.append(nv.reshape(shape))
    return (loss, grad_x[None], *out_g, *out_d, *out_m, *out_v)
```

```python
import functools

import jax
import jax.numpy as jnp
import numpy as np
from jax import lax
from jax.experimental import pallas as pl
from jax.experimental.pallas import tpu as pltpu

F32 = jnp.float32
BF16 = jnp.bfloat16

N_DEV = 8
N_CHIP = 4
AXES = ("x", "y", "c")
MESH = pl.DeviceIdType.MESH

D_MODEL = 2048
MLA_HEADS = 8
QK_NOPE = 128
QK_ROPE = 64
V_HEAD = 128
Q_LORA = 512
KV_LORA = 256
ROPE_THETA = 10000.0
HEAD_PAD = 256
DIL_PATTERNS = ((128, 1), (512, 4), (2048, 16))
DIL_GROUPS = 3
DIL_HG = 4
DIL_HEADS = 12
DIL_HD = 128
DIL_BLK = 128
DIL_QKV = DIL_HEADS * DIL_HD
DIL_OUT = DIL_HG * DIL_HD
ALIBI_MAX_BIAS = 8.0
D_FF = 5504
D_FF_PAD = 5632
NORM_EPS = 1e-6
LAT = Q_LORA + KV_LORA + QK_ROPE
LAT_PAD = 896
D_IN = LAT + 3 * DIL_QKV + 2 * D_MODEL
NEG = -1e30

ADAM_LR = 0.001
ADAM_B1 = 0.9
ADAM_B2 = 0.999
ADAM_EPS = 1e-08
ADAM_WD = 0.01
ADAM_STEP = 10

SMALL_ROWS = 56
SMALL_COLS = 1024

IN_ROWS = 1192
IN_ROWS_PAD = 1200
FF_GROUP = D_FF // N_DEV
FF_GROUP_PAD = D_FF_PAD // N_DEV
CONV_SHARD = (3, 1376)

NT = (((1,), (1,)), ((), ()))
TN = (((0,), (0,)), ((), ()))


def _dot(a, b, dims=(((1,), (0,)), ((), ()))):
    return lax.dot_general(a, b, dims, preferred_element_type=F32)


def _mm(a, b, mode, out_dtype, tm, tn, tk, name, bias=None, act=None, res=None, b_koff=0, a_halves=1):
    H = a_halves
    if mode == "nn":
        (M, K), (K2, N) = (a.shape[0] // H, a.shape[1] * H), b.shape
        assert (b_koff + 1) * K <= K2, (name, a.shape, b.shape)
        koff, K2 = b_koff * (K // tk), K
        kper, mrows = a.shape[1] // tk, M // tm
        a_spec = pl.BlockSpec((tm, tk), lambda i, j, k: (i + (k // kper) * mrows, k % kper))
        b_spec = pl.BlockSpec((tk, tn), lambda i, j, k: (k + koff, j))
        dims = (((1,), (0,)), ((), ()))
    elif mode == "nt":
        (M, K), (N, K2) = a.shape, b.shape
        a_spec = pl.BlockSpec((tm, tk), lambda i, j, k: (i, k))
        b_spec = pl.BlockSpec((tn, tk), lambda i, j, k: (j, k))
        dims = NT
    else:
        (K, M), (K2, N) = (a.shape[0] // H, a.shape[1] * H), b.shape
        mper, krows = a.shape[1] // tm, K // tk
        a_spec = pl.BlockSpec((tk, tm), lambda i, j, k: (k + (i // mper) * krows, i % mper))
        b_spec = pl.BlockSpec((tk, tn), lambda i, j, k: (k, j))
        dims = TN
    assert K == K2 and M % tm == 0 and N % tn == 0 and K % tk == 0, (name, a.shape, b.shape)
    nk = K // tk
    has_bias, has_res = bias is not None, res is not None

    def body(*refs):
        refs = list(refs)
        a_ref, b_ref = refs[0], refs[1]
        pos = 2
        bias_ref = res_ref = None
        if has_bias:
            bias_ref = refs[pos]
            pos += 1
        if has_res:
            res_ref = refs[pos]
            pos += 1
        o_ref = refs[pos]
        p = _dot(a_ref[...].astype(BF16), b_ref[...].astype(BF16), dims)

        def finish(acc):
            if has_bias:
                acc = acc + bias_ref[...]
            if act == "sigmoid":
                acc = jax.nn.sigmoid(acc)
            if has_res:
                acc = res_ref[...] + acc
            o_ref[...] = acc.astype(o_ref.dtype)

        if nk == 1:
            finish(p)
        else:
            acc_ref = refs[pos + 1]
            k = pl.program_id(2)

            @pl.when(k == 0)
            def _():
                acc_ref[...] = p

            @pl.when(k != 0)
            def _():
                acc_ref[...] += p

            @pl.when(k == nk - 1)
            def _():
                finish(acc_ref[...])

    in_specs = [a_spec, b_spec]
    args = [a, b]
    if has_bias:
        in_specs.append(pl.BlockSpec((1, tn), lambda i, j, k: (0, j)))
        args.append(bias)
    if has_res:
        in_specs.append(pl.BlockSpec((tm, tn), lambda i, j, k: (i, j)))
        args.append(res)
    return pl.pallas_call(
        body,
        name=name,
        grid=(M // tm, N // tn, nk),
        in_specs=in_specs,
        out_specs=pl.BlockSpec((tm, tn), lambda i, j, k: (i, j)),
        out_shape=jax.ShapeDtypeStruct((M, N), out_dtype),
        scratch_shapes=[pltpu.VMEM((tm, tn), F32)] if nk > 1 else [],
        compiler_params=pltpu.CompilerParams(dimension_semantics=("parallel", "parallel", "arbitrary")),
    )(*args)


def _stacked_mm(pieces, w_t, res, name, tm=512, tn=1024):
    P, M, W = pieces.shape
    N = w_t.shape[1]

    def body(a_ref, b_ref, r_ref, o_ref):
        acc = r_ref[...]
        for p in range(P):
            acc = acc + _dot(a_ref[p].astype(BF16), b_ref[p * W : (p + 1) * W, :])
        o_ref[...] = acc

    tile = pl.BlockSpec((tm, tn), lambda i, j: (i, j))
    return pl.pallas_call(
        body,
        name=name,
        grid=(M // tm, N // tn),
        in_specs=[pl.BlockSpec((P, tm, W), lambda i, j: (0, i, 0)), pl.BlockSpec((P * W, tn), lambda i, j: (0, j)), tile],
        out_specs=tile,
        out_shape=jax.ShapeDtypeStruct((M, N), F32),
        compiler_params=pltpu.CompilerParams(dimension_semantics=("parallel", "parallel")),
    )(pieces, w_t, res)


def _rstd(x):
    return lax.rsqrt(jnp.mean(x * x, axis=-1, keepdims=True) + NORM_EPS)


def _rms_bwd_math(dy, x, g):
    r = _rstd(x)
    xh = x * r
    dg = jnp.sum(dy * xh, axis=0, keepdims=True)
    dxh = dy * g
    dx = r * (dxh - xh * jnp.mean(dxh * xh, axis=-1, keepdims=True))
    return dx, dg


def _rms_fwd(x, g, name, tr=256):
    S, D = x.shape

    def body(x_ref, g_ref, o_ref):
        xv = x_ref[...]
        o_ref[...] = ((xv * _rstd(xv)) * g_ref[...]).astype(o_ref.dtype)

    return pl.pallas_call(
        body,
        name=name,
        grid=(S // tr,),
        in_specs=[pl.BlockSpec((tr, D), lambda i: (i, 0)), pl.BlockSpec((1, D), lambda i: (0, 0))],
        out_specs=pl.BlockSpec((tr, D), lambda i: (i, 0)),
        out_shape=jax.ShapeDtypeStruct((S, D), BF16),
        compiler_params=pltpu.CompilerParams(dimension_semantics=("parallel",)),
    )(x, g)


def _rms_bwd(dy, x, g, res, name, tr=256):
    S, D = x.shape

    def body(dy_ref, x_ref, g_ref, res_ref, dx_ref, dxb_ref, dg_ref):
        dx, dg = _rms_bwd_math(dy_ref[...], x_ref[...], g_ref[...])
        dx = dx + res_ref[...]
        dx_ref[...] = dx
        dxb_ref[...] = dx.astype(BF16)

        @pl.when(pl.program_id(0) == 0)
        def _():
            dg_ref[...] = dg

        @pl.when(pl.program_id(0) != 0)
        def _():
            dg_ref[...] += dg

    row = pl.BlockSpec((tr, D), lambda i: (i, 0))
    vec = pl.BlockSpec((1, D), lambda i: (0, 0))
    return pl.pallas_call(
        body,
        name=name,
        grid=(S // tr,),
        in_specs=[row, row, vec, row],
        out_specs=[row, row, vec],
        out_shape=[jax.ShapeDtypeStruct((S, D), F32), jax.ShapeDtypeStruct((S, D), BF16), jax.ShapeDtypeStruct((1, D), F32)],
        compiler_params=pltpu.CompilerParams(dimension_semantics=("arbitrary",)),
    )(dy, x, g, res)


def _final_loss(x2, g, tgt, name, tr=256):
    S, D = x2.shape

    def body(x_ref, g_ref, t_ref, loss_ref, dx_ref, dxb_ref, dg_ref):
        xv, gv = x_ref[...], g_ref[...]
        y = (xv * _rstd(xv)) * gv
        e = y - t_ref[...]
        part = 0.5 * jnp.sum(jnp.mean(e * e, axis=-1, keepdims=True), axis=0, keepdims=True)
        dx, dg = _rms_bwd_math(e * (1.0 / D), xv, gv)
        dx_ref[...] = dx
        dxb_ref[...] = dx.astype(BF16)
        part = jnp.broadcast_to(part, (1, 128))

        @pl.when(pl.program_id(0) == 0)
        def _():
            dg_ref[...] = dg
            loss_ref[...] = part

        @pl.when(pl.program_id(0) != 0)
        def _():
            dg_ref[...] += dg
            loss_ref[...] += part

    row = pl.BlockSpec((tr, D), lambda i: (i, 0))
    vec = pl.BlockSpec((1, D), lambda i: (0, 0))
    return pl.pallas_call(
        body,
        name=name,
        grid=(S // tr,),
        in_specs=[row, vec, row],
        out_specs=[pl.BlockSpec((1, 128), lambda i: (0, 0)), row, row, vec],
        out_shape=[
            jax.ShapeDtypeStruct((1, 128), F32),
            jax.ShapeDtypeStruct((S, D), F32),
            jax.ShapeDtypeStruct((S, D), BF16),
            jax.ShapeDtypeStruct((1, D), F32),
        ],
        compiler_params=pltpu.CompilerParams(dimension_semantics=("arbitrary",)),
    )(x2, g, tgt)


def _rope_tables(S):
    pos = jnp.arange(S, dtype=F32)
    inv_freq = ROPE_THETA ** (-jnp.arange(0, QK_ROPE, 2, dtype=F32) / QK_ROPE)
    ang = pos[:, None] * inv_freq[None, :]
    cos, sin = jnp.cos(ang), jnp.sin(ang)
    zero = jnp.zeros((S, 128 - QK_ROPE), F32)
    return jnp.concatenate([cos, cos, zero], axis=1), jnp.concatenate([-sin, sin, zero], axis=1)


def _rope_tile(x, cos_t, sin_t):
    lane = lax.broadcasted_iota(jnp.int32, x.shape, 1)
    partner = jnp.where(lane < QK_ROPE // 2, pltpu.roll(x, 128 - QK_ROPE // 2, 1), pltpu.roll(x, QK_ROPE // 2, 1))
    return x * cos_t + partner * sin_t


def _mla_prep1(lat, gq, gkv, cos_t, sin_t, name, tr=256):
    S = lat.shape[0]

    def body(lat_ref, gq_ref, gkv_ref, cos_ref, sin_ref, cq_ref, ckv_ref, kpe_ref):
        cq = lat_ref[:, :Q_LORA]
        ckv = lat_ref[:, Q_LORA : Q_LORA + KV_LORA]
        cq_ref[...] = ((cq * _rstd(cq)) * gq_ref[...]).astype(BF16)
        ckv_ref[...] = ((ckv * _rstd(ckv)) * gkv_ref[...]).astype(BF16)
        kpe_ref[...] = _rope_tile(lat_ref[:, Q_LORA + KV_LORA :], cos_ref[...], sin_ref[...]).astype(BF16)

    def row(n):
        return pl.BlockSpec((tr, n), lambda i: (i, 0))

    def vec(n):
        return pl.BlockSpec((1, n), lambda i: (0, 0))

    return pl.pallas_call(
        body,
        name=name,
        grid=(S // tr,),
        in_specs=[row(LAT_PAD), vec(Q_LORA), vec(KV_LORA), row(128), row(128)],
        out_specs=[row(Q_LORA), row(KV_LORA), row(128)],
        out_shape=[
            jax.ShapeDtypeStruct((S, Q_LORA), BF16),
            jax.ShapeDtypeStruct((S, KV_LORA), BF16),
            jax.ShapeDtypeStruct((S, 128), BF16),
        ],
        compiler_params=pltpu.CompilerParams(dimension_semantics=("parallel",)),
    )(lat, gq, gkv, cos_t, sin_t)


def _mla_prep2(q_raw, kv, kpe, cos_t, sin_t, name, tr=256):
    S = q_raw.shape[0]
    W = MLA_HEADS * HEAD_PAD

    def body(q_ref, kv_ref, kpe_ref, cos_ref, sin_ref, qa_ref, ka_ref):
        cos_v, sin_v, kpe_v = cos_ref[...], sin_ref[...], kpe_ref[...]
        for h in range(MLA_HEADS):
            lo = h * HEAD_PAD
            qa_ref[:, lo : lo + 128] = q_ref[:, lo : lo + 128].astype(BF16)
            qa_ref[:, lo + 128 : lo + 256] = _rope_tile(q_ref[:, lo + 128 : lo + 256], cos_v, sin_v).astype(BF16)
            ka_ref[:, lo : lo + 128] = kv_ref[:, lo : lo + 128]
            ka_ref[:, lo + 128 : lo + 256] = kpe_v

    def row(n):
        return pl.BlockSpec((tr, n), lambda i: (i, 0))

    return pl.pallas_call(
        body,
        name=name,
        grid=(S // tr,),
        in_specs=[row(W), row(W), row(128), row(128), row(128)],
        out_specs=[row(W), row(W)],
        out_shape=[jax.ShapeDtypeStruct((S, W), BF16), jax.ShapeDtypeStruct((S, W), BF16)],
        compiler_params=pltpu.CompilerParams(dimension_semantics=("parallel",)),
    )(q_raw, kv, kpe, cos_t, sin_t)


def _mla_post(dq_att, dk_att, dv, cos_t, sin_t, name, tr=256):
    S = dq_att.shape[0]
    W = MLA_HEADS * HEAD_PAD

    def body(dq_ref, dk_ref, dv_ref, cos_ref, sin_ref, dqr_ref, dkv_ref, dkpe_ref):
        cos_v, nsin_v = cos_ref[...], -sin_ref[...]
        kpe = jnp.zeros((tr, 128), F32)
        for h in range(MLA_HEADS):
            lo = h * HEAD_PAD
            dqr_ref[:, lo : lo + 128] = dq_ref[:, lo : lo + 128].astype(BF16)
            dqr_ref[:, lo + 128 : lo + 256] = _rope_tile(dq_ref[:, lo + 128 : lo + 256], cos_v, nsin_v).astype(BF16)
            dkv_ref[:, lo : lo + 128] = dk_ref[:, lo : lo + 128].astype(BF16)
            dkv_ref[:, lo + 128 : lo + 256] = dv_ref[:, h * 128 : (h + 1) * 128].astype(BF16)
            kpe = kpe + dk_ref[:, lo + 128 : lo + 256]
        dkpe_ref[...] = _rope_tile(kpe, cos_v, nsin_v)

    def row(n):
        return pl.BlockSpec((tr, n), lambda i: (i, 0))

    return pl.pallas_call(
        body,
        name=name,
        grid=(S // tr,),
        in_specs=[row(W), row(W), row(MLA_HEADS * V_HEAD), row(128), row(128)],
        out_specs=[row(W), row(W), row(128)],
        out_shape=[jax.ShapeDtypeStruct((S, W), BF16), jax.ShapeDtypeStruct((S, W), BF16), jax.ShapeDtypeStruct((S, 128), F32)],
        compiler_params=pltpu.CompilerParams(dimension_semantics=("parallel",)),
    )(dq_att, dk_att, dv, cos_t, sin_t)


def _lat_bwd(dcqn, dckvn, dkpe, lat, gq, gkv, name, tr=256):
    S = lat.shape[0]

    def body(dcq_ref, dckv_ref, dkpe_ref, lat_ref, gq_ref, gkv_ref, dlat_ref, dgq_ref, dgkv_ref):
        dq, dgq = _rms_bwd_math(dcq_ref[...], lat_ref[:, :Q_LORA], gq_ref[...])
        dkv, dgkv = _rms_bwd_math(dckv_ref[...], lat_ref[:, Q_LORA : Q_LORA + KV_LORA], gkv_ref[...])
        dlat_ref[:, :Q_LORA] = dq.astype(BF16)
        dlat_ref[:, Q_LORA : Q_LORA + KV_LORA] = dkv.astype(BF16)
        dlat_ref[:, Q_LORA + KV_LORA :] = dkpe_ref[...].astype(BF16)

        @pl.when(pl.program_id(0) == 0)
        def _():
            dgq_ref[...] = dgq
            dgkv_ref[...] = dgkv

        @pl.when(pl.program_id(0) != 0)
        def _():
            dgq_ref[...] += dgq
            dgkv_ref[...] += dgkv

    def row(n):
        return pl.BlockSpec((tr, n), lambda i: (i, 0))

    def vec(n):
        return pl.BlockSpec((1, n), lambda i: (0, 0))

    return pl.pallas_call(
        body,
        name=name,
        grid=(S // tr,),
        in_specs=[row(Q_LORA), row(KV_LORA), row(128), row(LAT_PAD), vec(Q_LORA), vec(KV_LORA)],
        out_specs=[row(LAT_PAD), vec(Q_LORA), vec(KV_LORA)],
        out_shape=[
            jax.ShapeDtypeStruct((S, LAT_PAD), BF16),
            jax.ShapeDtypeStruct((1, Q_LORA), F32),
            jax.ShapeDtypeStruct((1, KV_LORA), F32),
        ],
        compiler_params=pltpu.CompilerParams(dimension_semantics=("arbitrary",)),
    )(dcqn, dckvn, dkpe, lat, gq, gkv)


MLA_SCALE = (QK_NOPE + QK_ROPE) ** -0.5
LOG2E = 1.4426950408889634
MLA_C2 = MLA_SCALE * LOG2E
FLASH_T = 1024


def _causal_pairs(n, by_key):
    pairs = [(i, j) for j in range(n) for i in range(j, n)] if by_key else [(i, j) for i in range(n) for j in range(i + 1)]
    return jnp.asarray([p[0] for p in pairs], jnp.int32), jnp.asarray([p[1] for p in pairs], jnp.int32)


def _lanes(x, n):
    return jnp.tile(x, (1, n // 128))


def _flash_grid(npairs, in_specs, out_specs, scratch):
    return pltpu.PrefetchScalarGridSpec(
        num_scalar_prefetch=2, grid=(MLA_HEADS, npairs), in_specs=in_specs, out_specs=out_specs, scratch_shapes=scratch
    )


def _flash2_fwd(q_att, k_att, kv, name, t=FLASH_T):
    S = q_att.shape[0]
    qi_tab, kj_tab = _causal_pairs(S // t, by_key=False)

    def body(qi_ref, kj_ref, q_ref, k_ref, v_ref, o_ref, lse_ref, m_sc, l_sc, acc_sc):
        step = pl.program_id(1)
        qi, kj = qi_ref[step], kj_ref[step]

        @pl.when(kj == 0)
        def _():
            m_sc[...] = jnp.full((t, 128), NEG, F32)
            l_sc[...] = jnp.zeros((t, 128), F32)
            acc_sc[...] = jnp.zeros((t, V_HEAD), F32)

        def update(s):
            m_prev = m_sc[...]
            m_new = jnp.maximum(m_prev, jnp.max(s, axis=1, keepdims=True))
            p = jnp.exp2((s - _lanes(m_new, t)) * MLA_C2)
            alpha = jnp.exp2((m_prev - m_new) * MLA_C2)
            l_sc[...] = alpha * l_sc[...] + jnp.sum(p, axis=1, keepdims=True)
            acc_sc[...] = alpha * acc_sc[...] + _dot(p.astype(BF16), v_ref[...])
            m_sc[...] = m_new

        @pl.when(kj < qi)
        def _():
            update(_dot(q_ref[...], k_ref[...], NT))

        @pl.when(kj == qi)
        def _():
            s = _dot(q_ref[...], k_ref[...], NT)
            rows = lax.broadcasted_iota(jnp.int32, s.shape, 0)
            cols = lax.broadcasted_iota(jnp.int32, s.shape, 1)
            update(jnp.where(cols <= rows, s, NEG))
            l = l_sc[...]
            o_ref[...] = acc_sc[...] / l
            lse_ref[0] = m_sc[...] * MLA_SCALE + jnp.log(l)

    return pl.pallas_call(
        body,
        name=name,
        grid_spec=_flash_grid(
            qi_tab.shape[0],
            [
                pl.BlockSpec((t, HEAD_PAD), lambda h, p, qi, kj: (qi[p], h)),
                pl.BlockSpec((t, HEAD_PAD), lambda h, p, qi, kj: (kj[p], h)),
                pl.BlockSpec((t, V_HEAD), lambda h, p, qi, kj: (kj[p], 2 * h + 1)),
            ],
            [
                pl.BlockSpec((t, V_HEAD), lambda h, p, qi, kj: (qi[p], h)),
                pl.BlockSpec((1, t, 128), lambda h, p, qi, kj: (h, qi[p], 0)),
            ],
            [pltpu.VMEM((t, 128), F32), pltpu.VMEM((t, 128), F32), pltpu.VMEM((t, V_HEAD), F32)],
        ),
        out_shape=[jax.ShapeDtypeStruct((S, MLA_HEADS * V_HEAD), F32), jax.ShapeDtypeStruct((MLA_HEADS, S, 128), F32)],
        compiler_params=pltpu.CompilerParams(dimension_semantics=("parallel", "arbitrary")),
    )(qi_tab, kj_tab, q_att, k_att, kv)


def _flash_delta(do, o, name, tr=512):
    S = o.shape[0]

    def body(do_ref, o_ref, d_ref):
        lane = lax.broadcasted_iota(jnp.int32, (tr, 128), 1)
        acc = jnp.zeros((tr, 128), F32)
        for h in range(MLA_HEADS):
            sl = slice(h * V_HEAD, (h + 1) * V_HEAD)
            acc = jnp.where(lane == h, jnp.sum(do_ref[:, sl].astype(F32) * o_ref[:, sl], axis=1, keepdims=True), acc)
        d_ref[...] = acc

    row = pl.BlockSpec((tr, MLA_HEADS * V_HEAD), lambda i: (i, 0))
    return pl.pallas_call(
        body,
        name=name,
        grid=(S // tr,),
        in_specs=[row, row],
        out_specs=pl.BlockSpec((tr, 128), lambda i: (i, 0)),
        out_shape=jax.ShapeDtypeStruct((S, 128), F32),
        compiler_params=pltpu.CompilerParams(dimension_semantics=("parallel",)),
    )(do, o)


def _flash2_bwd(q_att, k_att, kv, do, lse_row, delta_row, name, t=FLASH_T):
    S = q_att.shape[0]
    n = S // t
    qi_tab, kj_tab = _causal_pairs(n, by_key=True)
    last = qi_tab.shape[0] - 1

    def body(qi_ref, kj_ref, q_ref, k_ref, v_ref, do_ref, lse_ref, dl_ref, dq_ref, dk_ref, dv_ref, dk_sc, dv_sc):
        step = pl.program_id(1)
        qi, kj = qi_ref[step], kj_ref[step]

        @pl.when(step == 0)
        def _():
            dq_ref[...] = jnp.zeros((S, HEAD_PAD), F32)

        def update(st):
            q, do_v = q_ref[...], do_ref[...]
            pt = jnp.exp2(st * MLA_C2 - lse_ref[0] * LOG2E)
            dv_sc[...] += _dot(pt.astype(BF16), do_v)
            dpt = _dot(v_ref[...], do_v, NT)
            dst = (pt * (dpt - dl_ref[0])).astype(BF16)
            dk_sc[...] += _dot(dst, q)
            rows = pl.ds(pl.multiple_of(qi * t, t), t)
            dq_ref[rows, :] += _dot(dst, k_ref[...], TN)

        @pl.when(qi == kj)
        def _():
            dk_sc[...] = jnp.zeros((t, HEAD_PAD), F32)
            dv_sc[...] = jnp.zeros((t, V_HEAD), F32)
            st = _dot(k_ref[...], q_ref[...], NT)
            keys = lax.broadcasted_iota(jnp.int32, st.shape, 0)
            qs = lax.broadcasted_iota(jnp.int32, st.shape, 1)
            update(jnp.where(keys <= qs, st, NEG))

        @pl.when(qi > kj)
        def _():
            update(_dot(k_ref[...], q_ref[...], NT))

        @pl.when(qi == n - 1)
        def _():
            dk_ref[...] = dk_sc[...] * MLA_SCALE
            dv_ref[...] = dv_sc[...]

        @pl.when(step == last)
        def _():
            dq_ref[...] = dq_ref[...] * MLA_SCALE

    qrow = lambda h, p, qi, kj: (qi[p], h)
    krow = lambda h, p, qi, kj: (kj[p], h)
    stat = pl.BlockSpec((1, 1, t), lambda h, p, qi, kj: (h, 0, qi[p]))
    return pl.pallas_call(
        body,
        name=name,
        grid_spec=_flash_grid(
            qi_tab.shape[0],
            [
                pl.BlockSpec((t, HEAD_PAD), qrow),
                pl.BlockSpec((t, HEAD_PAD), krow),
                pl.BlockSpec((t, V_HEAD), lambda h, p, qi, kj: (kj[p], 2 * h + 1)),
                pl.BlockSpec((t, V_HEAD), qrow),
                stat,
                stat,
            ],
            [
                pl.BlockSpec((S, HEAD_PAD), lambda h, p, qi, kj: (0, h)),
                pl.BlockSpec((t, HEAD_PAD), krow),
                pl.BlockSpec((t, V_HEAD), krow),
            ],
            [pltpu.VMEM((t, HEAD_PAD), F32), pltpu.VMEM((t, V_HEAD), F32)],
        ),
        out_shape=[
            jax.ShapeDtypeStruct((S, MLA_HEADS * HEAD_PAD), F32),
            jax.ShapeDtypeStruct((S, MLA_HEADS * HEAD_PAD), F32),
            jax.ShapeDtypeStruct((S, MLA_HEADS * V_HEAD), F32),
        ],
        compiler_params=pltpu.CompilerParams(dimension_semantics=("parallel", "arbitrary")),
    )(qi_tab, kj_tab, q_att, k_att, kv, do, lse_row, delta_row)


DIL_SCALE = DIL_HD**-0.5


def _dil_bias():
    slopes = 2.0 ** (-ALIBI_MAX_BIAS * np.arange(1, DIL_HEADS + 1, dtype=np.float64) / DIL_HEADS)
    slopes = slopes.astype(np.float32).reshape(DIL_GROUPS, DIL_HG)
    p = np.arange(DIL_BLK)[:, None]
    kidx = np.arange(2 * DIL_BLK)[None, :]
    j = p + DIL_BLK - kidx
    out = np.zeros((DIL_GROUPS, DIL_HG, DIL_BLK, 2 * DIL_BLK), np.float32)
    for g, (window, dil) in enumerate(DIL_PATTERNS):
        valid = (j >= 0) & (j <= window // dil)
        for h in range(DIL_HG):
            alibi = -slopes[g, h] * (dil * j).astype(np.float32)
            out[g, h] = np.where(valid, alibi, np.float32(NEG))
    return jnp.asarray(out)


DIL_UNROLL = 4


def _unrolled_loop(lo, hi, fn, unroll=DIL_UNROLL):
    groups = (hi - lo) // unroll
    done = lo
    if groups > 1:

        def step(i, carry):
            for u in range(unroll):
                fn(lo + i * unroll + u)
            return carry

        lax.fori_loop(0, groups, step, 0)
        done = lo + groups * unroll
    for n in range(done, hi):
        fn(n)


def _dil_rows(r, n, count, dil):
    if dil == 1:
        if isinstance(n, int):
            return slice(n * DIL_BLK, (n + count) * DIL_BLK)
        return pl.ds(pl.multiple_of(n * DIL_BLK, DIL_BLK), count * DIL_BLK)
    return pl.ds(n * DIL_BLK * dil + r, count * DIL_BLK, stride=dil)


def _dil_each_block(S, dil, block):
    nb = S // dil // DIL_BLK
    if dil == 1:
        block(0, 0, True)
        _unrolled_loop(1, nb, lambda n: block(0, n, False))
    else:
        for r in range(dil):
            for n in range(nb):
                block(r, n, n == 0)


def _dil_col(g, part, h):
    return (g * 3 + part) * DIL_HG + h


def _dil_fwd_group(dqkv, bias_g, g, dil, name):
    S = dqkv.shape[0]

    def body(bias_ref, q_ref, k_ref, v_ref, o_ref, lse_ref):
        def block(r, n, first):
            cur = _dil_rows(r, n, 1, dil)
            both = cur if first else _dil_rows(r, n - 1, 2, dil)
            b = bias_ref[0][:, DIL_BLK:] if first else bias_ref[0]
            q, kk, vv = q_ref[cur, :].astype(BF16), k_ref[both, :].astype(BF16), v_ref[both, :].astype(BF16)
            s = _dot(q, kk, NT) * DIL_SCALE + b
            m = jnp.max(s, axis=1, keepdims=True)
            e = jnp.exp(s - m)
            l = jnp.sum(e, axis=1, keepdims=True)
            p = e * (1.0 / l)
            o_ref[cur, :] = _dot(p.astype(BF16), vv)
            lse_ref[cur, :] = jnp.broadcast_to(m + jnp.log(l), (DIL_BLK, 128))

        _dil_each_block(S, dil, block)

    def col(part):
        return pl.BlockSpec((S, DIL_HD), lambda h: (0, _dil_col(g, part, h)))

    out = pl.BlockSpec((S, DIL_HD), lambda h: (0, h))
    return pl.pallas_call(
        body,
        name=name,
        grid=(DIL_HG,),
        in_specs=[pl.BlockSpec((1, DIL_BLK, 2 * DIL_BLK), lambda h: (h, 0, 0)), col(0), col(1), col(2)],
        out_specs=[out, out],
        out_shape=[jax.ShapeDtypeStruct((S, DIL_OUT), F32), jax.ShapeDtypeStruct((S, DIL_OUT), F32)],
        compiler_params=pltpu.CompilerParams(dimension_semantics=("parallel",)),
    )(bias_g, dqkv, dqkv, dqkv)


def _dil_combine(os_, ls_, name, tr=512):
    S = os_[0].shape[0]

    def body(o0, o1, o2, l0, l1, l2, out_ref, lse_ref):
        a, b, c = l0[...], l1[...], l2[...]
        m = jnp.maximum(jnp.maximum(a, b), c)
        ea, eb, ec = jnp.exp(a - m), jnp.exp(b - m), jnp.exp(c - m)
        den = ea + eb + ec
        inv = 1.0 / den
        out_ref[...] = (ea * inv) * o0[...] + (eb * inv) * o1[...] + (ec * inv) * o2[...]
        lse_ref[...] = m + jnp.log(den)

    row = pl.BlockSpec((tr, DIL_OUT), lambda i: (i, 0))
    return pl.pallas_call(
        body,
        name=name,
        grid=(S // tr,),
        in_specs=[row] * 6,
        out_specs=[row, row],
        out_shape=[jax.ShapeDtypeStruct((S, DIL_OUT), F32)] * 2,
        compiler_params=pltpu.CompilerParams(dimension_semantics=("parallel",)),
    )(*os_, *ls_)


def _dil_rowdot(dod, od, name, tr=512):
    S = dod.shape[0]

    def body(d_ref, o_ref, dd_ref):
        for h in range(DIL_HG):
            sl = slice(h * 128, (h + 1) * 128)
            sm = jnp.sum(d_ref[:, sl] * o_ref[:, sl], axis=1, keepdims=True)
            dd_ref[:, sl] = jnp.broadcast_to(sm, (tr, 128))

    row = pl.BlockSpec((tr, DIL_OUT), lambda i: (i, 0))
    return pl.pallas_call(
        body,
        name=name,
        grid=(S // tr,),
        in_specs=[row, row],
        out_specs=row,
        out_shape=jax.ShapeDtypeStruct((S, DIL_OUT), F32),
        compiler_params=pltpu.CompilerParams(dimension_semantics=("parallel",)),
    )(dod, od)


def _dil_bwd_group(dqkv, bias_g, dod, dd, lse, grads, g, dil, name):
    S = dqkv.shape[0]

    def body(bias_ref, q_ref, k_ref, v_ref, do_ref, dd_ref, lse_ref, _, out_ref):
        out_ref[1] = jnp.zeros((S, DIL_HD), F32)
        out_ref[2] = jnp.zeros((S, DIL_HD), F32)

        def block(r, n, first):
            cur = _dil_rows(r, n, 1, dil)
            both = cur if first else _dil_rows(r, n - 1, 2, dil)
            b = bias_ref[0][:, DIL_BLK:] if first else bias_ref[0]
            q, kk, vv = q_ref[cur, :].astype(BF16), k_ref[both, :].astype(BF16), v_ref[both, :].astype(BF16)
            do = do_ref[cur, :].astype(BF16)
            s = _dot(q, kk, NT) * DIL_SCALE + b
            p = jnp.exp(s - lse_ref[cur, 0:1])
            dp = _dot(do, vv, NT)
            ds = ((p * (dp - dd_ref[cur, 0:1])) * DIL_SCALE).astype(BF16)
            out_ref[0, cur, :] = _dot(ds, kk)
            out_ref[1, both, :] += _dot(ds, q, TN)
            out_ref[2, both, :] += _dot(p.astype(BF16), do, TN)

        _dil_each_block(S, dil, block)

    def col(part):
        return pl.BlockSpec((S, DIL_HD), lambda h: (0, _dil_col(g, part, h)))

    nat = pl.BlockSpec((S, DIL_HD), lambda h: (0, h))
    return pl.pallas_call(
        body,
        name=name,
        grid=(DIL_HG,),
        in_specs=[pl.BlockSpec((1, DIL_BLK, 2 * DIL_BLK), lambda h: (h, 0, 0)), col(0), col(1), col(2), nat, nat, nat, ANY],
        out_specs=pl.BlockSpec((3, S, DIL_HD), lambda h: (g, 0, h)),
        out_shape=jax.ShapeDtypeStruct(grads.shape, F32),
        input_output_aliases={7: 0},
        compiler_params=pltpu.CompilerParams(dimension_semantics=("parallel",)),
    )(bias_g, dqkv, dqkv, dqkv, dod, dd, lse, grads)


def _merge_fwd(gates, o_a, o_b, name, tr=256):
    S = o_a.shape[0]

    def body(ga_ref, gb_ref, oa_ref, ob_ref, m_ref):
        m_ref[...] = (ga_ref[...] * oa_ref[...] + gb_ref[...] * ob_ref[...]).astype(BF16)

    row = pl.BlockSpec((tr, D_MODEL), lambda i: (i, 0))
    return pl.pallas_call(
        body,
        name=name,
        grid=(S // tr,),
        in_specs=[row, pl.BlockSpec((tr, D_MODEL), lambda i: (i, 1)), row, row],
        out_specs=row,
        out_shape=jax.ShapeDtypeStruct((S, D_MODEL), BF16),
        compiler_params=pltpu.CompilerParams(dimension_semantics=("parallel",)),
    )(gates, gates, o_a, o_b)


def _merge_bwd(dmrg, gates, o_a, o_b, name, tr=256):
    S = o_a.shape[0]

    def body(dm_ref, ga_ref, gb_ref, oa_ref, ob_ref, doa_ref, dob_ref, dga_ref, dgb_ref, dba_ref, dbb_ref):
        dm, ga, gb = dm_ref[...], ga_ref[...], gb_ref[...]
        doa_ref[...] = (dm * ga).astype(BF16)
        dob_ref[...] = (dm * gb).astype(BF16)
        dga = (dm * oa_ref[...]) * (ga * (1.0 - ga))
        dgb = (dm * ob_ref[...]) * (gb * (1.0 - gb))
        dga_ref[...] = dga.astype(BF16)
        dgb_ref[...] = dgb.astype(BF16)
        sa = jnp.sum(dga, axis=0, keepdims=True)
        sb = jnp.sum(dgb, axis=0, keepdims=True)

        @pl.when(pl.program_id(0) == 0)
        def _():
            dba_ref[...] = sa
            dbb_ref[...] = sb

        @pl.when(pl.program_id(0) != 0)
        def _():
            dba_ref[...] += sa
            dbb_ref[...] += sb

    row = pl.BlockSpec((tr, D_MODEL), lambda i: (i, 0))
    row1 = pl.BlockSpec((tr, D_MODEL), lambda i: (i, 1))
    vec = pl.BlockSpec((1, D_MODEL), lambda i: (0, 0))
    outs = pl.pallas_call(
        body,
        name=name,
        grid=(S // tr,),
        in_specs=[row, row, row1, row, row],
        out_specs=[row, row, row, row, vec, vec],
        out_shape=[jax.ShapeDtypeStruct((S, D_MODEL), BF16)] * 4 + [jax.ShapeDtypeStruct((1, D_MODEL), F32)] * 2,
        compiler_params=pltpu.CompilerParams(dimension_semantics=("arbitrary",)),
    )(dmrg, gates, gates, o_a, o_b)
    return outs


CONV_TR = 512
CONV_TC = 512
N_FFC = D_FF_PAD // CONV_TC


def _conv_taps(x, before, w_ref, b_ref):
    x0 = jnp.concatenate([before, x], axis=0)
    x1 = pltpu.roll(x0, 1, 0)
    x2 = pltpu.roll(x0, 2, 0)
    u = ((b_ref[...] + w_ref[0:1, :] * x2) + w_ref[1:2, :] * x1) + w_ref[2:3, :] * x0
    return u, x0, x1, x2


def _prev_halo(tr):
    return lambda i, j: (jnp.maximum(i * (tr // 8) - 1, 0), j)


def _ffn_fwd(u0, cw, cb, name):
    S = u0.shape[0]
    tr, tc = CONV_TR, CONV_TC

    def body(up_ref, gt_ref, hup_ref, hgt_ref, wu_ref, wg_ref, bu_ref, bg_ref, a_ref):
        live = (pl.program_id(0) > 0).astype(F32)
        up = _conv_taps(up_ref[...], hup_ref[...] * live, wu_ref, bu_ref)[0][8:]
        gt = _conv_taps(gt_ref[...], hgt_ref[...] * live, wg_ref, bg_ref)[0][8:]
        a_ref[...] = ((gt * jax.nn.sigmoid(gt)) * up).astype(BF16)

    return pl.pallas_call(
        body,
        name=name,
        grid=(S // tr, N_FFC),
        in_specs=[
            pl.BlockSpec((tr, tc), lambda i, j: (i, j)),
            pl.BlockSpec((tr, tc), lambda i, j: (i, j + N_FFC)),
            pl.BlockSpec((8, tc), _prev_halo(tr)),
            pl.BlockSpec((8, tc), lambda i, j: (jnp.maximum(i * (tr // 8) - 1, 0), j + N_FFC)),
            pl.BlockSpec((8, tc), lambda i, j: (0, j)),
            pl.BlockSpec((8, tc), lambda i, j: (0, j + N_FFC)),
            pl.BlockSpec((1, tc), lambda i, j: (0, j)),
            pl.BlockSpec((1, tc), lambda i, j: (0, j + N_FFC)),
        ],
        out_specs=pl.BlockSpec((tr, tc), lambda i, j: (i, j)),
        out_shape=jax.ShapeDtypeStruct((S, D_FF_PAD), BF16),
        compiler_params=pltpu.CompilerParams(dimension_semantics=("parallel", "parallel")),
    )(u0, u0, u0, u0, cw, cw, cb, cb)


def _ffn_bwd(u0, da, cw, cb, name):
    S = u0.shape[0]
    tr, tc = CONV_TR, CONV_TC
    nrow, te = S // tr, tr + 8

    def body(up_ref, gt_ref, hup_ref, hgt_ref, nup_ref, ngt_ref, da_ref, nda_ref, wu_ref, wg_ref, bu_ref, bg_ref, du0_ref, dcw_ref, dcb_ref):
        i = pl.program_id(1)
        prev_live = (i > 0).astype(F32)
        next_live = (i < nrow - 1).astype(F32)

        def conv(x_ref, nx_ref, h_ref, w_ref, b_ref):
            x = jnp.concatenate([x_ref[...], nx_ref[...] * next_live], axis=0)
            return [t[8:] for t in _conv_taps(x, h_ref[...] * prev_live, w_ref, b_ref)]

        up, xu0, xu1, xu2 = conv(up_ref, nup_ref, hup_ref, wu_ref, bu_ref)
        gt, xg0, xg1, xg2 = conv(gt_ref, ngt_ref, hgt_ref, wg_ref, bg_ref)
        da_v = jnp.concatenate([da_ref[...], nda_ref[...] * next_live], axis=0)
        sg = jax.nn.sigmoid(gt)
        d_up = da_v * (gt * sg)
        d_gt = (da_v * up) * (sg * (1.0 + gt * (1.0 - sg)))
        tap = lax.broadcasted_iota(jnp.int32, (8, tc), 0)

        def finish(half, du, x0, x1, x2, w_ref):
            n1 = pltpu.roll(du, te - 1, 0)
            n2 = pltpu.roll(du, te - 2, 0)
            du0 = (w_ref[2:3, :] * du + w_ref[1:2, :] * n1) + w_ref[0:1, :] * n2
            du0_ref[half] = du0[:tr].astype(BF16)
            d = du[:tr]
            dcw = jnp.where(
                tap == 0,
                jnp.sum(d * x2[:tr], axis=0, keepdims=True),
                jnp.where(tap == 1, jnp.sum(d * x1[:tr], axis=0, keepdims=True), jnp.where(tap == 2, jnp.sum(d * x0[:tr], axis=0, keepdims=True), 0.0)),
            )
            dcb = jnp.sum(d, axis=0, keepdims=True)

            @pl.when(i == 0)
            def _():
                dcw_ref[half] = dcw
                dcb_ref[half] = dcb

            @pl.when(i != 0)
            def _():
                dcw_ref[half] += dcw
                dcb_ref[half] += dcb

        finish(0, d_up, xu0, xu1, xu2, wu_ref)
        finish(1, d_gt, xg0, xg1, xg2, wg_ref)

    def prev8(off):
        return pl.BlockSpec((8, tc), lambda j, i: (jnp.maximum(i * (tr // 8) - 1, 0), j + off))

    def next8(off):
        return pl.BlockSpec((8, tc), lambda j, i: (jnp.minimum((i + 1) * (tr // 8), S // 8 - 1), j + off))

    return pl.pallas_call(
        body,
        name=name,
        grid=(N_FFC, nrow),
        in_specs=[
            pl.BlockSpec((tr, tc), lambda j, i: (i, j)),
            pl.BlockSpec((tr, tc), lambda j, i: (i, j + N_FFC)),
            prev8(0),
            prev8(N_FFC),
            next8(0),
            next8(N_FFC),
            pl.BlockSpec((tr, tc), lambda j, i: (i, j)),
            next8(0),
            pl.BlockSpec((8, tc), lambda j, i: (0, j)),
            pl.BlockSpec((8, tc), lambda j, i: (0, j + N_FFC)),
            pl.BlockSpec((1, tc), lambda j, i: (0, j)),
            pl.BlockSpec((1, tc), lambda j, i: (0, j + N_FFC)),
        ],
        out_specs=[
            pl.BlockSpec((2, tr, tc), lambda j, i: (0, i, j)),
            pl.BlockSpec((2, 8, tc), lambda j, i: (0, 0, j)),
            pl.BlockSpec((2, 1, tc), lambda j, i: (0, 0, j)),
        ],
        out_shape=[
            jax.ShapeDtypeStruct((2, S, D_FF_PAD), BF16),
            jax.ShapeDtypeStruct((2, 8, D_FF_PAD), F32),
            jax.ShapeDtypeStruct((2, 1, D_FF_PAD), F32),
        ],
        compiler_params=pltpu.CompilerParams(dimension_semantics=("parallel", "arbitrary")),
    )(u0, u0, u0, u0, u0, u0, da, da, cw, cw, cb, cb)


def _adamw(w, g, m, v, name):
    R, C = w.shape
    tr = R
    for cand in (256, 128, 64, 32, 16, 8):
        if R % cand == 0 and R > cand:
            tr = cand
            break

    def body(w_ref, g_ref, m_ref, v_ref, d_ref, nm_ref, nv_ref):
        gv = g_ref[...]
        nm = ADAM_B1 * m_ref[...] + (1.0 - ADAM_B1) * gv
        nv = ADAM_B2 * v_ref[...] + (1.0 - ADAM_B2) * (gv * gv)
        m_hat = nm / (1.0 - ADAM_B1**ADAM_STEP)
        v_hat = nv / (1.0 - ADAM_B2**ADAM_STEP)
        d_ref[...] = -ADAM_LR * (m_hat / (jnp.sqrt(v_hat) + ADAM_EPS) + ADAM_WD * w_ref[...])
        nm_ref[...] = nm
        nv_ref[...] = nv

    blk = pl.BlockSpec((tr, C), lambda i: (i, 0))
    return pl.pallas_call(
        body,
        name=name,
        grid=(R // tr,),
        in_specs=[blk] * 4,
        out_specs=[blk] * 3,
        out_shape=[jax.ShapeDtypeStruct((R, C), F32)] * 3,
        compiler_params=pltpu.CompilerParams(dimension_semantics=("parallel",)),
    )(w, g, m, v)


ANY = pl.BlockSpec(memory_space=pl.ANY)


def _all_gather(blocks, name):
    n = len(blocks)

    def body(*refs):
        x_refs, out_refs = refs[:n], refs[n : 2 * n]
        send_sems, recv_sems, local_sems = refs[2 * n :]
        x, y, c = lax.axis_index("x"), lax.axis_index("y"), lax.axis_index("c")
        me, sibling = (x, y, c), (x, y, 1 - c)
        chips = [(1 - x, y), (x, 1 - y), (1 - x, 1 - y)]

        def slot(a, px, py, pc):
            return out_refs[a].at[4 * px + 2 * py + pc]

        def copy(a, k, blk, to, src=None):
            return pltpu.make_async_remote_copy(
                src_ref=slot(a, *blk) if src is None else src,
                dst_ref=slot(a, *blk),
                send_sem=send_sems.at[7 * a + k],
                recv_sem=recv_sems.at[7 * a + k],
                device_id=to,
                device_id_type=MESH,
            )

        mine = [pltpu.make_async_copy(x_refs[a], slot(a, *me), local_sems.at[a]) for a in range(n)]
        sent = []
        for a in range(n):
            mine[a].start()
            first = [copy(a, 0, me, sibling, src=x_refs[a])]
            first += [copy(a, 1 + j, me, (*chip, c), src=x_refs[a]) for j, chip in enumerate(chips)]
            for cp in first:
                cp.start()
            sent += first
        for a in range(n):
            for j, chip in enumerate(chips):
                copy(a, 1 + j, (*chip, c), me).wait_recv()
                passed = copy(a, 4 + j, (*chip, c), sibling)
                passed.start()
                sent.append(passed)
        for a in range(n):
            copy(a, 0, sibling, me).wait_recv()
            for j, chip in enumerate(chips):
                copy(a, 4 + j, (*chip, 1 - c), me).wait_recv()
        for cp in sent:
            cp.wait_send()
        for cp in mine:
            cp.wait()

    return pl.pallas_call(
        body,
        name=name,
        out_shape=[jax.ShapeDtypeStruct((N_DEV,) + b.shape, b.dtype) for b in blocks],
        in_specs=[ANY] * n,
        out_specs=[ANY] * n,
        scratch_shapes=[pltpu.SemaphoreType.DMA((7 * n,)), pltpu.SemaphoreType.DMA((7 * n,)), pltpu.SemaphoreType.DMA((n,))],
    )(*blocks)


def _pair_exchange(gs, name):
    n = len(gs)

    def body(*refs):
        g_refs, out_refs = refs[:n], refs[n : 2 * n]
        send_sems, recv_sems = refs[2 * n :]
        x, y, c = lax.axis_index("x"), lax.axis_index("y"), lax.axis_index("c")
        copies = [
            pltpu.make_async_remote_copy(
                src_ref=g_refs[a].at[2 * k + (1 - c)],
                dst_ref=out_refs[a].at[k],
                send_sem=send_sems.at[N_CHIP * a + k],
                recv_sem=recv_sems.at[N_CHIP * a + k],
                device_id=(x, y, 1 - c),
                device_id_type=MESH,
            )
            for a in range(n)
            for k in range(N_CHIP)
        ]
        for cp in copies:
            cp.start()
        for cp in copies:
            cp.wait()

    return pl.pallas_call(
        body,
        name=name,
        out_shape=[jax.ShapeDtypeStruct((N_CHIP,) + g.shape[1:], g.dtype) for g in gs],
        in_specs=[ANY] * n,
        out_specs=[ANY] * n,
        scratch_shapes=[pltpu.SemaphoreType.DMA((N_CHIP * n,)), pltpu.SemaphoreType.DMA((N_CHIP * n,))],
    )(*gs)


def _row_tile(rows):
    return max(t for t in range(16, 353, 16) if rows % t == 0)


def _pair_add(g, recv, core, name):
    _, R, C = g.shape
    tr = _row_tile(R)

    def body(core_ref, g_ref, r_ref, o_ref):
        o_ref[...] = (g_ref[...].astype(F32) + r_ref[...].astype(F32)).astype(o_ref.dtype)

    return pl.pallas_call(
        body,
        name=name,
        grid_spec=pltpu.PrefetchScalarGridSpec(
            num_scalar_prefetch=1,
            grid=(N_CHIP, R // tr),
            in_specs=[
                pl.BlockSpec((1, tr, C), lambda k, i, core_ref: (2 * k + core_ref[0], i, 0)),
                pl.BlockSpec((1, tr, C), lambda k, i, core_ref: (k, i, 0)),
            ],
            out_specs=pl.BlockSpec((1, tr, C), lambda k, i, core_ref: (k, i, 0)),
        ),
        out_shape=jax.ShapeDtypeStruct((N_CHIP, R, C), g.dtype),
        compiler_params=pltpu.CompilerParams(dimension_semantics=("parallel", "parallel")),
    )(core, g, recv)


HBM = pl.BlockSpec(memory_space=pltpu.HBM)
SEM = pl.BlockSpec(memory_space=pltpu.SEMAPHORE)
EFFECT = pltpu.SideEffectType.DATAFLOW_SIDE_EFFECTING
RELATIONS = tuple((dx, dy, dc) for dx in (0, 1) for dy in (0, 1) for dc in (0, 1))[1:]


def _related(rel):
    x, y, c = lax.axis_index("x"), lax.axis_index("y"), lax.axis_index("c")
    return (1 - x if rel[0] else x, 1 - y if rel[1] else y, 1 - c if rel[2] else c)


def _dev_index(pos):
    return 4 * pos[0] + 2 * pos[1] + pos[2]


def _peers(chips):
    if chips:
        return [r for r in RELATIONS if not r[2]], N_CHIP, lambda pos: 2 * pos[0] + pos[1]
    return list(RELATIONS), N_DEV, _dev_index


def _exchange_start(srcs, by_slot, after, name, chips=False):
    n = len(srcs)
    rels, slots, slot_of = _peers(chips)
    lands = [lax.empty((slots,) + (s.shape[1:] if by_slot else s.shape), s.dtype) for s in srcs]
    nsem = len(rels) * n

    def body(*refs):
        src_refs, land_refs = refs[:n], refs[n : 2 * n]
        send_sems, recv_sems = refs[2 * n + 1], refs[2 * n + 2]
        token = refs[-1]
        me = slot_of(_related((0, 0, 0)))
        for a in range(n):
            for k, rel in enumerate(rels):
                peer = _related(rel)
                pltpu.make_async_remote_copy(
                    src_ref=src_refs[a].at[slot_of(peer)] if by_slot else src_refs[a],
                    dst_ref=land_refs[a].at[me],
                    send_sem=send_sems.at[len(rels) * a + k],
                    recv_sem=recv_sems.at[len(rels) * a + k],
                    device_id=peer,
                    device_id_type=MESH,
                ).start()
        token[...] = jnp.zeros_like(token)

    def hbm(a):
        return pltpu.HBM(a.shape, a.dtype)

    outs = pl.pallas_call(
        body,
        name=name,
        out_shape=(
            pltpu.SemaphoreType.DMA((nsem,)),
            pltpu.SemaphoreType.DMA((nsem,)),
            *[hbm(s) for s in srcs],
            *[hbm(l) for l in lands],
            jax.ShapeDtypeStruct((8, 128), F32),
        ),
        in_specs=[HBM] * (2 * n) + [ANY],
        out_specs=(SEM, SEM, *[HBM] * (2 * n), pl.BlockSpec(memory_space=pltpu.VMEM)),
        input_output_aliases={i: 2 + i for i in range(2 * n)},
        compiler_params=pltpu.CompilerParams(has_side_effects=EFFECT),
    )(*[pltpu.with_memory_space_constraint(a, pltpu.HBM) for a in list(srcs) + lands], after)
    return (outs[0], outs[1], list(outs[2 : 2 + n]), list(outs[2 + n : 2 + 2 * n])), outs[-1]


def _exchange_wait(handle, by_slot, after, name, chips=False):
    send_sems, recv_sems, srcs, lands = handle
    n = len(srcs)
    rels = _peers(chips)[0]

    def body(*refs):
        src_refs, land_refs = refs[:n], refs[n : 2 * n]
        s_sems, r_sems = refs[2 * n], refs[2 * n + 1]
        for a in range(n):
            for k, rel in enumerate(rels):
                copy = pltpu.make_async_remote_copy(
                    src_ref=src_refs[a].at[0] if by_slot else src_refs[a],
                    dst_ref=land_refs[a].at[0],
                    send_sem=s_sems.at[len(rels) * a + k],
                    recv_sem=r_sems.at[len(rels) * a + k],
                    device_id=_related(rel),
                    device_id_type=MESH,
                )
                copy.wait_send()
                copy.wait_recv()

    outs = pl.pallas_call(
        body,
        name=name,
        out_shape=tuple(pltpu.HBM(a.shape, a.dtype) for a in srcs + lands),
        in_specs=[HBM] * (2 * n) + [SEM, SEM, ANY],
        out_specs=tuple([HBM] * (2 * n)),
        input_output_aliases={i: i for i in range(2 * n)},
        compiler_params=pltpu.CompilerParams(has_side_effects=EFFECT),
    )(*srcs, *lands, send_sems, recv_sems, after)
    return list(outs[:n]), list(outs[n:])


def _slot_sum(parts, name):
    n, R, C = parts.shape
    tr = _row_tile(R) if R % 16 == 0 else R

    def body(p_ref, o_ref):
        acc = p_ref[0].astype(F32)
        for k in range(1, n):
            acc = acc + p_ref[k].astype(F32)
        o_ref[...] = acc

    return pl.pallas_call(
        body,
        name=name,
        grid=(R // tr,),
        in_specs=[pl.BlockSpec((n, tr, C), lambda i: (0, i, 0))],
        out_specs=pl.BlockSpec((tr, C), lambda i: (i, 0)),
        out_shape=jax.ShapeDtypeStruct((R, C), F32),
        compiler_params=pltpu.CompilerParams(dimension_semantics=("parallel",)),
    )(parts)


W_IN_TC = 256
W_IN_BOUNDS = (0, LAT, LAT + 3 * DIL_QKV, LAT + 3 * DIL_QKV + D_MODEL, D_IN)


def _dqkv_chunks():
    return [((g * 3 + part) * DIL_OUT, LAT + part * DIL_QKV + g * DIL_OUT) for g in range(DIL_GROUPS) for part in range(3)]


def _w_in_regroup(slots, name):
    tc = W_IN_TC

    def body(s_ref, lat_ref, dqkv_ref, g_ref, buf):
        for j in range(N_DEV):
            buf[j * IN_ROWS : (j + 1) * IN_ROWS, :] = s_ref[j].astype(F32)[:IN_ROWS, :]
        lat_ref[:LAT, :] = buf[:LAT, :].astype(BF16)
        lat_ref[LAT:, :] = jnp.zeros((LAT_PAD - LAT, tc), BF16)
        for dst, src in _dqkv_chunks():
            dqkv_ref[dst : dst + DIL_OUT, :] = buf[src : src + DIL_OUT, :].astype(BF16)
        g_ref[...] = buf[W_IN_BOUNDS[2] :, :].astype(BF16)

    def col(rows):
        return pl.BlockSpec((rows, tc), lambda k: (0, k))

    return pl.pallas_call(
        body,
        name=name,
        grid=(D_MODEL // tc,),
        in_specs=[pl.BlockSpec((N_DEV, IN_ROWS_PAD, tc), lambda k: (0, 0, k))],
        out_specs=[col(LAT_PAD), col(3 * DIL_QKV), col(2 * D_MODEL)],
        out_shape=[
            jax.ShapeDtypeStruct((LAT_PAD, D_MODEL), BF16),
            jax.ShapeDtypeStruct((3 * DIL_QKV, D_MODEL), BF16),
            jax.ShapeDtypeStruct((2 * D_MODEL, D_MODEL), BF16),
        ],
        scratch_shapes=[pltpu.VMEM((D_IN, tc), F32)],
        compiler_params=pltpu.CompilerParams(dimension_semantics=("parallel",)),
    )(slots)


def _w_in_grad_regroup(g_lat, g_dqkv, g_ga, g_gb, name):
    tc = W_IN_TC

    def body(lat_ref, dqkv_ref, ga_ref, gb_ref, o_ref, buf):
        b = W_IN_BOUNDS
        buf[b[0] : b[1], :] = lat_ref[:LAT, :].astype(F32)
        for dst, src in _dqkv_chunks():
            buf[src : src + DIL_OUT, :] = dqkv_ref[dst : dst + DIL_OUT, :].astype(F32)
        buf[b[2] : b[3], :] = ga_ref[...].astype(F32)
        buf[b[3] : b[4], :] = gb_ref[...].astype(F32)
        fill = jnp.zeros((IN_ROWS_PAD - IN_ROWS, tc), F32)
        for j in range(N_DEV):
            o_ref[j] = jnp.concatenate([buf[j * IN_ROWS : (j + 1) * IN_ROWS, :], fill], axis=0).astype(BF16)

    def col(rows):
        return pl.BlockSpec((rows, tc), lambda k: (0, k))

    return pl.pallas_call(
        body,
        name=name,
        grid=(D_MODEL // tc,),
        in_specs=[col(LAT_PAD), col(3 * DIL_QKV), col(D_MODEL), col(D_MODEL)],
        out_specs=pl.BlockSpec((N_DEV, IN_ROWS_PAD, tc), lambda k: (0, 0, k)),
        out_shape=jax.ShapeDtypeStruct((N_DEV, IN_ROWS_PAD, D_MODEL), BF16),
        scratch_shapes=[pltpu.VMEM((D_IN, tc), F32)],
        compiler_params=pltpu.CompilerParams(dimension_semantics=("parallel",)),
    )(g_lat, g_dqkv, g_ga, g_gb)


def _ffn_pad(a, axis):
    a = jnp.moveaxis(a, axis, -1)
    g = a.reshape(a.shape[:-1] + (2 * N_DEV, FF_GROUP))
    g = jnp.pad(g, [(0, 0)] * (g.ndim - 1) + [(0, FF_GROUP_PAD - FF_GROUP)])
    return jnp.moveaxis(g.reshape(a.shape[:-1] + (2 * D_FF_PAD,)), -1, axis)


def _ffn_unpad(a, axis):
    a = jnp.moveaxis(a, axis, -1)
    g = a.reshape(a.shape[:-1] + (2 * N_DEV, FF_GROUP_PAD))[..., :FF_GROUP]
    return jnp.moveaxis(g.reshape(a.shape[:-1] + (2 * D_FF,)), -1, axis)


MISC = (("w_o_mla", (256, 1024)), ("w_o_dil", (256, 512)), ("w_uq", (192, 512)), ("w_ukv", (256, 256)))


def _exchange_blocks(w):
    def t(a):
        return a.astype(BF16).T

    up = t(w["w_up"]).reshape(2, FF_GROUP, D_MODEL)
    return [
        jnp.pad(t(w["w_in"]), ((0, IN_ROWS_PAD - IN_ROWS), (0, 0))),
        jnp.pad(up, ((0, 0), (0, FF_GROUP_PAD - FF_GROUP), (0, 0))).reshape(2 * FF_GROUP_PAD, D_MODEL),
        jnp.pad(w["w_down"].astype(BF16), ((0, FF_GROUP_PAD - FF_GROUP), (0, 0))),
        w["w_out"].astype(BF16),
        jnp.concatenate([t(w[n]).reshape(-1, D_MODEL) for n, _ in MISC], axis=0),
    ]


def _misc_split(misc):
    out, off = {}, 0
    for n, (r, c) in MISC:
        rows = r * c // D_MODEL
        out[n] = misc[..., off : off + rows, :].reshape(misc.shape[:-2] + (r, c))
        off += rows
    return out


def _small_matrices(g_misc):
    misc = _misc_split(g_misc)
    uq_t = jnp.pad(misc["w_uq"], ((0, 0), (0, HEAD_PAD - QK_NOPE - QK_ROPE), (0, 0)))
    return {
        "uq_t": uq_t.reshape(MLA_HEADS * HEAD_PAD, Q_LORA),
        "ukv_t": misc["w_ukv"].reshape(MLA_HEADS * HEAD_PAD, KV_LORA),
        "o_mla_t": misc["w_o_mla"].reshape(D_MODEL, MLA_HEADS * V_HEAD),
        "o_dil_t": misc["w_o_dil"].reshape(D_MODEL, DIL_OUT),
    }


def _small_grad_blocks(g):
    uq_t = g["uq_t"].reshape(MLA_HEADS, HEAD_PAD, Q_LORA)[:, : QK_NOPE + QK_ROPE]
    misc = {"w_o_mla": g["o_mla_t"], "w_o_dil": g["o_dil_t"], "w_uq": uq_t, "w_ukv": g["ukv_t"]}
    return [
        g["w_out"].reshape(N_DEV, -1, D_MODEL),
        jnp.concatenate([misc[n].reshape(N_DEV, -1, D_MODEL) for n, _ in MISC], axis=1),
    ]


def _grad_shards(sums):
    s_in, s_out, s_misc, s_up, s_down = sums
    out = {
        "w_in": s_in[:IN_ROWS].T,
        "w_up": s_up.reshape(2, FF_GROUP_PAD, D_MODEL)[:, :FF_GROUP].reshape(2 * FF_GROUP, D_MODEL).T,
        "w_down": s_down[:FF_GROUP],
        "w_out": s_out,
    }
    out.update({n: v.T for n, v in _misc_split(s_misc).items()})
    return out


def _local_step(x, tgt, wt, conv_w, small, small_matrices, ffn_weight, send_ffn_grads, send_small_grads, send_w_in_grads, start_token):
    S = x.shape[0]
    lat_t, dqkv_t, g_t = wt
    cw = jnp.pad(_ffn_pad(conv_w, 1), ((0, 5), (0, 0)))
    cb = _ffn_pad(small["conv_b"], 1)
    cos_t, sin_t = _rope_tables(S)
    bias = _dil_bias()
    g1, g2, g3 = small["attn_norm_g"], small["ffn_norm_g"], small["final_norm_g"]
    gq, gkv = small["q_norm_g"], small["kv_norm_g"]

    h = _rms_fwd(x, g1 + start_token, "rms_attn")
    lat = _mm(h, lat_t, "nt", F32, 1024, LAT_PAD, D_MODEL, "proj_lat")
    dqkv = _mm(h, dqkv_t, "nt", F32, 1024, 512, D_MODEL, "proj_dqkv")
    gates = _mm(h, g_t, "nt", F32, 1024, 512, D_MODEL, "proj_gates", bias=small["b_gate"], act="sigmoid")
    sm = small_matrices(lat)
    uq_t, ukv_t, o_mla_t, o_dil_t = sm["uq_t"], sm["ukv_t"], sm["o_mla_t"], sm["o_dil_t"]
    cqn, ckvn, kpe = _mla_prep1(lat, gq, gkv, cos_t, sin_t, "mla_prep1")
    q_raw = _mm(cqn, uq_t, "nt", F32, 1024, 1024, Q_LORA, "mla_uq")
    kv = _mm(ckvn, ukv_t, "nt", BF16, 1024, 1024, KV_LORA, "mla_ukv")
    q_att, k_att = _mla_prep2(q_raw, kv, kpe, cos_t, sin_t, "mla_prep2")
    o, lse = _flash2_fwd(q_att, k_att, kv, "mla_flash_fwd")
    o_a = _mm(o, o_mla_t, "nt", F32, 1024, 1024, MLA_HEADS * V_HEAD, "mla_out")

    d_os, d_ls = [], []
    for g, (_, dil) in enumerate(DIL_PATTERNS):
        og, lg = _dil_fwd_group(dqkv, bias[g], g, dil, f"dil_fwd_{g}")
        d_os.append(og)
        d_ls.append(lg)
    od, dil_lse = _dil_combine(d_os, d_ls, "dil_combine")
    o_b = _mm(od, o_dil_t, "nt", F32, 1024, 1024, DIL_OUT, "dil_out")

    mrg = _merge_fwd(gates, o_a, o_b, "merge_fwd")
    w_out = ffn_weight("w_out", mrg)
    x1 = _mm(mrg, w_out, "nn", F32, 1024, 1024, D_MODEL, "mix_out", res=x)
    h2 = _rms_fwd(x1, g2, "rms_ffn")
    up_t = ffn_weight("up_t", h2)
    u0 = _mm(h2, up_t, "nt", F32, 1024, 512, D_MODEL, "ffn_up")
    a = _ffn_fwd(u0, cw, cb, "ffn_conv_fwd")
    w_down = ffn_weight("w_down", a)
    x2 = _mm(a, w_down, "nn", F32, 1024, 512, D_FF_PAD // 2, "ffn_down", res=x1)
    loss_part, dx2, dx2b, dg3 = _final_loss(x2, g3, tgt, "final_loss")

    da = _mm(dx2b, w_down, "nt", F32, 1024, 512, D_MODEL, "ffn_down_dx")
    gw_down = _mm(a, dx2b, "tn", BF16, 512, 1024, S, "ffn_down_dw")
    du0, dcw, dcb = _ffn_bwd(u0, da, cw, cb, "ffn_conv_bwd")
    du0 = du0.reshape(2 * S, D_FF_PAD)
    gw_up_t = _mm(du0, h2, "tn", BF16, 512, 1024, S, "ffn_up_dw", a_halves=2)
    sent = send_ffn_grads(gw_up_t, gw_down)
    dh2 = _mm(du0, up_t, "nn", F32, 1024, 1024, D_FF_PAD // 2, "ffn_up_dx", a_halves=2)
    dx1, dx1b, dg2 = _rms_bwd(dh2, x1, g2 + sent, dx2, "rms_ffn_bwd")

    dmrg = _mm(dx1b, w_out, "nt", F32, 1024, 1024, D_MODEL, "mix_out_dx")
    gw_out = _mm(mrg, dx1b, "tn", BF16, 512, 1024, S, "mix_out_dw")
    do_a, do_b, dga, dgb, dba, dbb = _merge_bwd(dmrg, gates, o_a, o_b, "merge_bwd")

    do = _mm(do_a, o_mla_t, "nn", BF16, 1024, 1024, D_MODEL, "mla_out_dx")
    gw_o_mla_t = _mm(do_a, o, "tn", BF16, 1024, 1024, 1024, "mla_out_dw")
    dod = _mm(do_b, o_dil_t, "nn", F32, 1024, DIL_OUT, D_MODEL, "dil_out_dx")
    gw_o_dil_t = _mm(do_b, od, "tn", BF16, 1024, DIL_OUT, 1024, "dil_out_dw")

    delta = _flash_delta(do, o, "mla_flash_delta")
    lse_row = lse[:, :, 0][:, None, :]
    delta_row = delta[:, :MLA_HEADS].T[:, None, :]
    dq_att, dk_att, dv = _flash2_bwd(q_att, k_att, kv, do, lse_row, delta_row, "mla_flash_bwd")
    dq_raw, dkv, dkpe = _mla_post(dq_att, dk_att, dv, cos_t, sin_t, "mla_post")
    dcqn = _mm(dq_raw, uq_t, "nn", F32, 1024, Q_LORA, MLA_HEADS * HEAD_PAD, "mla_uq_dx")
    gw_uq_t = _mm(dq_raw, cqn, "tn", BF16, 1024, Q_LORA, 1024, "mla_uq_dw")
    dckvn = _mm(dkv, ukv_t, "nn", F32, 1024, KV_LORA, MLA_HEADS * HEAD_PAD, "mla_ukv_dx")
    gw_ukv_t = _mm(dkv, ckvn, "tn", BF16, 1024, KV_LORA, 1024, "mla_ukv_dw")
    sent = send_small_grads({"uq_t": gw_uq_t, "ukv_t": gw_ukv_t, "o_mla_t": gw_o_mla_t, "o_dil_t": gw_o_dil_t, "w_out": gw_out})
    dlat, dgq, dgkv = _lat_bwd(dcqn, dckvn, dkpe, lat, gq + sent, gkv, "lat_bwd")

    dd = _dil_rowdot(dod, od, "dil_rowdot")
    ddqkv = lax.empty((3 * DIL_GROUPS, S, DIL_OUT), F32)
    for g, (_, dil) in enumerate(DIL_PATTERNS):
        ddqkv = _dil_bwd_group(dqkv, bias[g], dod, dd, dil_lse, ddqkv, g, dil, f"dil_bwd_{g}")
    gw_lat_t = _mm(dlat, h, "tn", BF16, LAT_PAD, 1024, S, "proj_lat_dw")
    gw_dqkv_t = _mm(ddqkv.reshape(3 * DIL_GROUPS * S, DIL_OUT), h, "tn", BF16, 512, 1024, S, "proj_dqkv_dw", a_halves=3 * DIL_GROUPS)
    gw_ga_t = _mm(dga, h, "tn", BF16, 512, 1024, S, "proj_ga_dw")
    gw_gb_t = _mm(dgb, h, "tn", BF16, 512, 1024, S, "proj_gb_dw")
    sent = send_w_in_grads(gw_lat_t, gw_dqkv_t, gw_ga_t, gw_gb_t)
    dh = _mm(dlat + sent.astype(BF16), lat_t, "nn", F32, 1024, 1024, LAT_PAD, "proj_lat_dx")
    dh = _stacked_mm(ddqkv, dqkv_t, dh, "proj_dqkv_dx")
    dh = _mm(dga, g_t, "nn", F32, 1024, 1024, D_MODEL, "proj_ga_dx", res=dh)
    dh = _mm(dgb, g_t, "nn", F32, 1024, 1024, D_MODEL, "proj_gb_dx", res=dh, b_koff=1)
    grad_x, _, dg1 = _rms_bwd(dh, x, g1, dx1, "rms_attn_bwd")

    small_grads = {
        "attn_norm_g": dg1,
        "b_gate": jnp.concatenate([dba, dbb], axis=1),
        "q_norm_g": dgq,
        "kv_norm_g": dgkv,
        "ffn_norm_g": dg2,
        "conv_b": _ffn_unpad(jnp.concatenate([dcb[0], dcb[1]], axis=1), 1),
        "final_norm_g": dg3,
        "conv_w": _ffn_unpad(jnp.concatenate([dcw[0, :3], dcw[1, :3]], axis=1), 1),
    }
    return loss_part, grad_x, small_grads


SMALL_ORDER = ("attn_norm_g", "b_gate", "q_norm_g", "kv_norm_g", "ffn_norm_g", "conv_b", "final_norm_g", "conv_w")
WEIGHT_ORDER = (
    "attn_norm_g", "w_in", "b_gate", "q_norm_g", "w_uq", "kv_norm_g", "w_ukv", "w_o_mla", "w_o_dil", "w_out",
    "ffn_norm_g", "w_up", "conv_w", "conv_b", "w_down", "final_norm_g",
)


def kernel(x, attn_norm_g, w_in, b_gate, q_norm_g, w_uq, kv_norm_g, w_ukv, w_o_mla, w_o_dil, w_out, ffn_norm_g, w_up, conv_w, conv_b, w_down, final_norm_g, loss_target, m_attn_norm_g, m_w_in, m_b_gate, m_q_norm_g, m_w_uq, m_kv_norm_g, m_w_ukv, m_w_o_mla, m_w_o_dil, m_w_out, m_ffn_norm_g, m_w_up, m_conv_w, m_conv_b, m_w_down, m_final_norm_g, v_attn_norm_g, v_w_in, v_b_gate, v_q_norm_g, v_w_uq, v_kv_norm_g, v_w_ukv, v_w_o_mla, v_w_o_dil, v_w_out, v_ffn_norm_g, v_w_up, v_conv_w, v_conv_b, v_w_down, v_final_norm_g):
    env = dict(locals())
    dev = 4 * lax.axis_index("x") + 2 * lax.axis_index("y") + lax.axis_index("c")
    core = lax.axis_index("c").astype(jnp.int32).reshape(1)

    def two_d(a):
        return a.reshape(-1, a.shape[-1])

    w = {n: two_d(env[n]) for n in WEIGHT_ORDER}
    m = {n: two_d(env["m_" + n]) for n in WEIGHT_ORDER}
    v = {n: two_d(env["v_" + n]) for n in WEIGHT_ORDER}

    chip = 2 * lax.axis_index("x") + lax.axis_index("y")

    def own_slot_in(lands, own, slot=dev):
        return [lax.dynamic_update_slice(l, o[None], (slot, 0, 0)) for l, o in zip(lands, own)]

    b_in, b_up, b_down, b_out, b_misc = _exchange_blocks(w)
    r, c = CONV_SHARD
    conv = jnp.pad(w["conv_w"].reshape(-1), (0, 8 * SMALL_COLS - r * c)).reshape(8, SMALL_COLS)
    g_in, conv = _all_gather([b_in, conv], "ag_w_in")
    misc_gather, started = _exchange_start([b_misc], False, conv, "ag_small_start")
    ffn_gathers, started2 = {}, started
    for key, block in (("w_out", b_out), ("up_t", b_up), ("w_down", b_down)):
        ffn_gathers[key], started2 = _exchange_start([block], False, started2, f"ag_{key}_start")
    wt = _w_in_regroup(g_in, "w_in_regroup")
    conv = conv.reshape(N_DEV, 8 * SMALL_COLS)[:, : r * c].reshape(N_DEV, r, c)
    conv_w_full = conv.transpose(1, 0, 2).reshape(r, N_DEV * c)
    small = {n: w[n] for n in SMALL_ORDER if n != "conv_w"}

    def small_matrices(after):
        own, lands = _exchange_wait(misc_gather, False, after, "ag_small_wait")
        return _small_matrices(own_slot_in(lands, own)[0])

    def ffn_weight(key, after):
        own, lands = _exchange_wait(ffn_gathers[key], False, after, f"ag_{key}_wait")
        return own_slot_in(lands, own)[0].reshape(-1, D_MODEL)

    reduces = {}

    def send_ffn_grads(gw_up_t, gw_down):
        blocks = [gw_up_t.reshape(N_DEV, 2 * FF_GROUP_PAD, D_MODEL), gw_down.reshape(N_DEV, FF_GROUP_PAD, D_MODEL)]
        reduces["ffn"], token = _exchange_start(blocks, True, gw_down, "rs_ffn_start")
        return token[0, 0]

    def send_small_grads(g):
        reduces["small"], token = _exchange_start(_small_grad_blocks(g), True, g["w_out"], "rs_small_start")
        return token[0, 0]

    def send_w_in_grads(g_lat, g_dqkv, g_ga, g_gb):
        e_in = _w_in_grad_regroup(g_lat, g_dqkv, g_ga, g_gb, "w_in_grad_regroup")
        pair = _pair_add(e_in, _pair_exchange([e_in], "rs_w_in_pair_exchange")[0], core, "rs_w_in_pair_add")
        reduces["w_in"], token = _exchange_start([pair], True, pair, "rs_w_in_start", chips=True)
        return token[0, 0]

    loss_part, grad_x, small_grads = _local_step(
        x[0], loss_target[0], wt, conv_w_full, small, small_matrices, ffn_weight,
        send_ffn_grads, send_small_grads, send_w_in_grads, started2[0, 0],
    )
    loss = lax.psum(loss_part[0, 0], AXES)

    def finish(key, by_chip, name):
        sent, lands = _exchange_wait(reduces[key], True, grad_x, name + "_wait", chips=by_chip)
        slot = chip if by_chip else dev
        own = [lax.dynamic_index_in_dim(s, slot, 0, keepdims=False) for s in sent]
        return [_slot_sum(p, f"{name}_sum_{i}") for i, p in enumerate(own_slot_in(lands, own, slot))]

    (s_in,) = finish("w_in", True, "rs_w_in")
    s_out, s_misc = finish("small", False, "rs_small")
    s_up, s_down = finish("ffn", False, "rs_ffn")
    gshard = _grad_shards([s_in, s_out, s_misc, s_up, s_down])

    sflat = jnp.concatenate([small_grads[n].reshape(-1) for n in SMALL_ORDER])
    sflat = jnp.pad(sflat, (0, SMALL_ROWS * SMALL_COLS - sflat.shape[0])).reshape(SMALL_ROWS, SMALL_COLS)
    ssum = _slot_sum(_all_gather([sflat], "ag_small_grads")[0], "small_sum").reshape(-1)
    gsmall, off = {}, 0
    for n in SMALL_ORDER:
        shape = (3, 2 * D_FF) if n == "conv_w" else w[n].shape
        size = shape[0] * shape[1]
        gsmall[n] = ssum[off : off + size].reshape(shape)
        off += size
    gsmall["conv_w"] = lax.dynamic_slice_in_dim(gsmall["conv_w"], dev * CONV_SHARD[1], CONV_SHARD[1], axis=1)

    g_all = {**gshard, **gsmall}
    out_g, out_d, out_m, out_v = [], [], [], []
    for n in WEIGHT_ORDER:
        d, nm, nv = _adamw(w[n], g_all[n], m[n], v[n], "adamw_" + n)
        shape = env[n].shape
        out_g.append(g_all[n].reshape(shape))
        out_d.append(d.reshape(shape))
        out_m.append(nm.reshape(shape))
        out_v.append(nv.reshape(shape))
    return (loss, grad_x[None], *out_g, *out_d, *out_m, *out_v)
```

```python
import functools

import jax
import jax.numpy as jnp
import numpy as np
from jax import lax
from jax.experimental import pallas as pl
from jax.experimental.pallas import tpu as pltpu

F32 = jnp.float32
BF16 = jnp.bfloat16

N_DEV = 8
N_CHIP = 4
AXES = ("x", "y", "c")
MESH = pl.DeviceIdType.MESH

D_MODEL = 2048
MLA_HEADS = 8
QK_NOPE = 128
QK_ROPE = 64
V_HEAD = 128
Q_LORA = 512
KV_LORA = 256
ROPE_THETA = 10000.0
HEAD_PAD = 256
DIL_PATTERNS = ((128, 1), (512, 4), (2048, 16))
DIL_GROUPS = 3
DIL_HG = 4
DIL_HEADS = 12
DIL_HD = 128
DIL_BLK = 128
DIL_QKV = DIL_HEADS * DIL_HD
DIL_OUT = DIL_HG * DIL_HD
ALIBI_MAX_BIAS = 8.0
D_FF = 5504
D_FF_PAD = 5632
NORM_EPS = 1e-6
LAT = Q_LORA + KV_LORA + QK_ROPE
LAT_PAD = 896
D_IN = LAT + 3 * DIL_QKV + 2 * D_MODEL
NEG = -1e30

ADAM_LR = 0.001
ADAM_B1 = 0.9
ADAM_B2 = 0.999
ADAM_EPS = 1e-08
ADAM_WD = 0.01
ADAM_STEP = 10

SMALL_ROWS = 56
SMALL_COLS = 1024

IN_ROWS = 1192
IN_ROWS_PAD = 1200
FF_GROUP = D_FF // N_DEV
FF_GROUP_PAD = D_FF_PAD // N_DEV
CONV_SHARD = (3, 1376)

NT = (((1,), (1,)), ((), ()))
TN = (((0,), (0,)), ((), ()))


def _dot(a, b, dims=(((1,), (0,)), ((), ()))):
    return lax.dot_general(a, b, dims, preferred_element_type=F32)


def _mm(a, b, mode, out_dtype, tm, tn, tk, name, bias=None, act=None, res=None, b_koff=0, a_halves=1):
    H = a_halves
    if mode == "nn":
        (M, K), (K2, N) = (a.shape[0] // H, a.shape[1] * H), b.shape
        assert (b_koff + 1) * K <= K2, (name, a.shape, b.shape)
        koff, K2 = b_koff * (K // tk), K
        kper, mrows = a.shape[1] // tk, M // tm
        a_spec = pl.BlockSpec((tm, tk), lambda i, j, k: (i + (k // kper) * mrows, k % kper))
        b_spec = pl.BlockSpec((tk, tn), lambda i, j, k: (k + koff, j))
        dims = (((1,), (0,)), ((), ()))
    elif mode == "nt":
        (M, K), (N, K2) = a.shape, b.shape
        a_spec = pl.BlockSpec((tm, tk), lambda i, j, k: (i, k))
        b_spec = pl.BlockSpec((tn, tk), lambda i, j, k: (j, k))
        dims = NT
    else:
        (K, M), (K2, N) = (a.shape[0] // H, a.shape[1] * H), b.shape
        mper, krows = a.shape[1] // tm, K // tk
        a_spec = pl.BlockSpec((tk, tm), lambda i, j, k: (k + (i // mper) * krows, i % mper))
        b_spec = pl.BlockSpec((tk, tn), lambda i, j, k: (k, j))
        dims = TN
    assert K == K2 and M % tm == 0 and N % tn == 0 and K % tk == 0, (name, a.shape, b.shape)
    nk = K // tk
    has_bias, has_res = bias is not None, res is not None

    def body(*refs):
        refs = list(refs)
        a_ref, b_ref = refs[0], refs[1]
        pos = 2
        bias_ref = res_ref = None
        if has_bias:
            bias_ref = refs[pos]
            pos += 1
        if has_res:
            res_ref = refs[pos]
            pos += 1
        o_ref = refs[pos]
        p = _dot(a_ref[...].astype(BF16), b_ref[...].astype(BF16), dims)

        def finish(acc):
            if has_bias:
                acc = acc + bias_ref[...]
            if act == "sigmoid":
                acc = jax.nn.sigmoid(acc)
            if has_res:
                acc = res_ref[...] + acc
            o_ref[...] = acc.astype(o_ref.dtype)

        if nk == 1:
            finish(p)
        else:
            acc_ref = refs[pos + 1]
            k = pl.program_id(2)

            @pl.when(k == 0)
            def _():
                acc_ref[...] = p

            @pl.when(k != 0)
            def _():
                acc_ref[...] += p

            @pl.when(k == nk - 1)
            def _():
                finish(acc_ref[...])

    in_specs = [a_spec, b_spec]
    args = [a, b]
    if has_bias:
        in_specs.append(pl.BlockSpec((1, tn), lambda i, j, k: (0, j)))
        args.append(bias)
    if has_res:
        in_specs.append(pl.BlockSpec((tm, tn), lambda i, j, k: (i, j)))
        args.append(res)
    return pl.pallas_call(
        body,
        name=name,
        grid=(M // tm, N // tn, nk),
        in_specs=in_specs,
        out_specs=pl.BlockSpec((tm, tn), lambda i, j, k: (i, j)),
        out_shape=jax.ShapeDtypeStruct((M, N), out_dtype),
        scratch_shapes=[pltpu.VMEM((tm, tn), F32)] if nk > 1 else [],
        compiler_params=pltpu.CompilerParams(dimension_semantics=("parallel", "parallel", "arbitrary")),
    )(*args)


def _stacked_mm(pieces, w_t, res, name, tm=512, tn=1024):
    P, M, W = pieces.shape
    N = w_t.shape[1]

    def body(a_ref, b_ref, r_ref, o_ref):
        acc = r_ref[...]
        for p in range(P):
            acc = acc + _dot(a_ref[p].astype(BF16), b_ref[p * W : (p + 1) * W, :])
        o_ref[...] = acc

    tile = pl.BlockSpec((tm, tn), lambda i, j: (i, j))
    return pl.pallas_call(
        body,
        name=name,
        grid=(M // tm, N // tn),
        in_specs=[pl.BlockSpec((P, tm, W), lambda i, j: (0, i, 0)), pl.BlockSpec((P * W, tn), lambda i, j: (0, j)), tile],
        out_specs=tile,
        out_shape=jax.ShapeDtypeStruct((M, N), F32),
        compiler_params=pltpu.CompilerParams(dimension_semantics=("parallel", "parallel")),
    )(pieces, w_t, res)


def _rstd(x):
    return lax.rsqrt(jnp.mean(x * x, axis=-1, keepdims=True) + NORM_EPS)


def _rms_bwd_math(dy, x, g):
    r = _rstd(x)
    xh = x * r
    dg = jnp.sum(dy * xh, axis=0, keepdims=True)
    dxh = dy * g
    dx = r * (dxh - xh * jnp.mean(dxh * xh, axis=-1, keepdims=True))
    return dx, dg


def _rms_fwd(x, g, name, tr=256):
    S, D = x.shape

    def body(x_ref, g_ref, o_ref):
        xv = x_ref[...]
        o_ref[...] = ((xv * _rstd(xv)) * g_ref[...]).astype(o_ref.dtype)

    return pl.pallas_call(
        body,
        name=name,
        grid=(S // tr,),
        in_specs=[pl.BlockSpec((tr, D), lambda i: (i, 0)), pl.BlockSpec((1, D), lambda i: (0, 0))],
        out_specs=pl.BlockSpec((tr, D), lambda i: (i, 0)),
        out_shape=jax.ShapeDtypeStruct((S, D), BF16),
        compiler_params=pltpu.CompilerParams(dimension_semantics=("parallel",)),
    )(x, g)


def _rms_bwd(dy, x, g, res, name, tr=256):
    S, D = x.shape

    def body(dy_ref, x_ref, g_ref, res_ref, dx_ref, dxb_ref, dg_ref):
        dx, dg = _rms_bwd_math(dy_ref[...], x_ref[...], g_ref[...])
        dx = dx + res_ref[...]
        dx_ref[...] = dx
        dxb_ref[...] = dx.astype(BF16)

        @pl.when(pl.program_id(0) == 0)
        def _():
            dg_ref[...] = dg

        @pl.when(pl.program_id(0) != 0)
        def _():
            dg_ref[...] += dg

    row = pl.BlockSpec((tr, D), lambda i: (i, 0))
    vec = pl.BlockSpec((1, D), lambda i: (0, 0))
    return pl.pallas_call(
        body,
        name=name,
        grid=(S // tr,),
        in_specs=[row, row, vec, row],
        out_specs=[row, row, vec],
        out_shape=[jax.ShapeDtypeStruct((S, D), F32), jax.ShapeDtypeStruct((S, D), BF16), jax.ShapeDtypeStruct((1, D), F32)],
        compiler_params=pltpu.CompilerParams(dimension_semantics=("arbitrary",)),
    )(dy, x, g, res)


def _mm_res_rms(a, b, res, g, name, tm=256):
    M, K = a.shape
    D = b.shape[1]

    def body(a_ref, b_ref, res_ref, g_ref, y_ref, h_ref):
        y = res_ref[...] + _dot(a_ref[...], b_ref[...])
        y_ref[...] = y
        h_ref[...] = ((y * _rstd(y)) * g_ref[...]).astype(BF16)

    row = pl.BlockSpec((tm, D), lambda i: (i, 0))
    return pl.pallas_call(
        body,
        name=name,
        grid=(M // tm,),
        in_specs=[pl.BlockSpec((tm, K), lambda i: (i, 0)), pl.BlockSpec((K, D), lambda i: (0, 0)), row, pl.BlockSpec((1, D), lambda i: (0, 0))],
        out_specs=[row, row],
        out_shape=[jax.ShapeDtypeStruct((M, D), F32), jax.ShapeDtypeStruct((M, D), BF16)],
        compiler_params=pltpu.CompilerParams(dimension_semantics=("parallel",)),
    )(a, b, res, g)


def _mm_rms_bwd(a, b, b_koff, dy_part, x, g, res, name, tm=256):
    M, K = a.shape
    D = x.shape[1]

    def body(a_ref, b_ref, dyp_ref, x_ref, g_ref, res_ref, dx_ref, dg_ref):
        dy = dyp_ref[...] + _dot(a_ref[...], b_ref[...])
        dx, dg = _rms_bwd_math(dy, x_ref[...], g_ref[...])
        dx_ref[...] = dx + res_ref[...]

        @pl.when(pl.program_id(0) == 0)
        def _():
            dg_ref[...] = dg

        @pl.when(pl.program_id(0) != 0)
        def _():
            dg_ref[...] += dg

    row = pl.BlockSpec((tm, D), lambda i: (i, 0))
    vec = pl.BlockSpec((1, D), lambda i: (0, 0))
    return pl.pallas_call(
        body,
        name=name,
        grid=(M // tm,),
        in_specs=[pl.BlockSpec((tm, K), lambda i: (i, 0)), pl.BlockSpec((K, D), lambda i: (b_koff, 0)), row, row, vec, row],
        out_specs=[row, vec],
        out_shape=[jax.ShapeDtypeStruct((M, D), F32), jax.ShapeDtypeStruct((1, D), F32)],
        compiler_params=pltpu.CompilerParams(dimension_semantics=("arbitrary",)),
    )(a, b, dy_part, x, g, res)


def _final_loss(x2, g, tgt, name, tr=256):
    S, D = x2.shape

    def body(x_ref, g_ref, t_ref, loss_ref, dx_ref, dxb_ref, dg_ref):
        xv, gv = x_ref[...], g_ref[...]
        y = (xv * _rstd(xv)) * gv
        e = y - t_ref[...]
        part = 0.5 * jnp.sum(jnp.mean(e * e, axis=-1, keepdims=True), axis=0, keepdims=True)
        dx, dg = _rms_bwd_math(e * (1.0 / D), xv, gv)
        dx_ref[...] = dx
        dxb_ref[...] = dx.astype(BF16)
        part = jnp.broadcast_to(part, (1, 128))

        @pl.when(pl.program_id(0) == 0)
        def _():
            dg_ref[...] = dg
            loss_ref[...] = part

        @pl.when(pl.program_id(0) != 0)
        def _():
            dg_ref[...] += dg
            loss_ref[...] += part

    row = pl.BlockSpec((tr, D), lambda i: (i, 0))
    vec = pl.BlockSpec((1, D), lambda i: (0, 0))
    return pl.pallas_call(
        body,
        name=name,
        grid=(S // tr,),
        in_specs=[row, vec, row],
        out_specs=[pl.BlockSpec((1, 128), lambda i: (0, 0)), row, row, vec],
        out_shape=[
            jax.ShapeDtypeStruct((1, 128), F32),
            jax.ShapeDtypeStruct((S, D), F32),
            jax.ShapeDtypeStruct((S, D), BF16),
            jax.ShapeDtypeStruct((1, D), F32),
        ],
        compiler_params=pltpu.CompilerParams(dimension_semantics=("arbitrary",)),
    )(x2, g, tgt)


def _rope_tables(S):
    pos = jnp.arange(S, dtype=F32)
    inv_freq = ROPE_THETA ** (-jnp.arange(0, QK_ROPE, 2, dtype=F32) / QK_ROPE)
    ang = pos[:, None] * inv_freq[None, :]
    cos, sin = jnp.cos(ang), jnp.sin(ang)
    zero = jnp.zeros((S, 128 - QK_ROPE), F32)
    return jnp.concatenate([cos, cos, zero], axis=1), jnp.concatenate([-sin, sin, zero], axis=1)


def _rope_tile(x, cos_t, sin_t):
    lane = lax.broadcasted_iota(jnp.int32, x.shape, 1)
    partner = jnp.where(lane < QK_ROPE // 2, pltpu.roll(x, 128 - QK_ROPE // 2, 1), pltpu.roll(x, QK_ROPE // 2, 1))
    return x * cos_t + partner * sin_t


def _mla_prep1(lat, gq, gkv, cos_t, sin_t, name, tr=256):
    S = lat.shape[0]

    def body(lat_ref, gq_ref, gkv_ref, cos_ref, sin_ref, cq_ref, ckv_ref, kpe_ref):
        cq = lat_ref[:, :Q_LORA]
        ckv = lat_ref[:, Q_LORA : Q_LORA + KV_LORA]
        cq_ref[...] = ((cq * _rstd(cq)) * gq_ref[...]).astype(BF16)
        ckv_ref[...] = ((ckv * _rstd(ckv)) * gkv_ref[...]).astype(BF16)
        kpe_ref[...] = _rope_tile(lat_ref[:, Q_LORA + KV_LORA :], cos_ref[...], sin_ref[...]).astype(BF16)

    def row(n):
        return pl.BlockSpec((tr, n), lambda i: (i, 0))

    def vec(n):
        return pl.BlockSpec((1, n), lambda i: (0, 0))

    return pl.pallas_call(
        body,
        name=name,
        grid=(S // tr,),
        in_specs=[row(LAT_PAD), vec(Q_LORA), vec(KV_LORA), row(128), row(128)],
        out_specs=[row(Q_LORA), row(KV_LORA), row(128)],
        out_shape=[
            jax.ShapeDtypeStruct((S, Q_LORA), BF16),
            jax.ShapeDtypeStruct((S, KV_LORA), BF16),
            jax.ShapeDtypeStruct((S, 128), BF16),
        ],
        compiler_params=pltpu.CompilerParams(dimension_semantics=("parallel",)),
    )(lat, gq, gkv, cos_t, sin_t)


def _mla_prep2(q_raw, kv, kpe, cos_t, sin_t, name, tr=256):
    S = q_raw.shape[0]
    W = MLA_HEADS * HEAD_PAD

    def body(q_ref, kv_ref, kpe_ref, cos_ref, sin_ref, qa_ref, ka_ref):
        cos_v, sin_v, kpe_v = cos_ref[...], sin_ref[...], kpe_ref[...]
        for h in range(MLA_HEADS):
            lo = h * HEAD_PAD
            qa_ref[:, lo : lo + 128] = q_ref[:, lo : lo + 128].astype(BF16)
            qa_ref[:, lo + 128 : lo + 256] = _rope_tile(q_ref[:, lo + 128 : lo + 256], cos_v, sin_v).astype(BF16)
            ka_ref[:, lo : lo + 128] = kv_ref[:, lo : lo + 128]
            ka_ref[:, lo + 128 : lo + 256] = kpe_v

    def row(n):
        return pl.BlockSpec((tr, n), lambda i: (i, 0))

    return pl.pallas_call(
        body,
        name=name,
        grid=(S // tr,),
        in_specs=[row(W), row(W), row(128), row(128), row(128)],
        out_specs=[row(W), row(W)],
        out_shape=[jax.ShapeDtypeStruct((S, W), BF16), jax.ShapeDtypeStruct((S, W), BF16)],
        compiler_params=pltpu.CompilerParams(dimension_semantics=("parallel",)),
    )(q_raw, kv, kpe, cos_t, sin_t)


def _mla_post(dq_att, dk_att, dv, cos_t, sin_t, name, tr=256):
    S = dq_att.shape[0]
    W = MLA_HEADS * HEAD_PAD

    def body(dq_ref, dk_ref, dv_ref, cos_ref, sin_ref, dqr_ref, dkv_ref, dkpe_ref):
        cos_v, nsin_v = cos_ref[...], -sin_ref[...]
        kpe = jnp.zeros((tr, 128), F32)
        for h in range(MLA_HEADS):
            lo = h * HEAD_PAD
            dqr_ref[:, lo : lo + 128] = dq_ref[:, lo : lo + 128].astype(BF16)
            dqr_ref[:, lo + 128 : lo + 256] = _rope_tile(dq_ref[:, lo + 128 : lo + 256], cos_v, nsin_v).astype(BF16)
            dkv_ref[:, lo : lo + 128] = dk_ref[:, lo : lo + 128].astype(BF16)
            dkv_ref[:, lo + 128 : lo + 256] = dv_ref[:, h * 128 : (h + 1) * 128].astype(BF16)
            kpe = kpe + dk_ref[:, lo + 128 : lo + 256]
        dkpe_ref[...] = _rope_tile(kpe, cos_v, nsin_v)

    def row(n):
        return pl.BlockSpec((tr, n), lambda i: (i, 0))

    return pl.pallas_call(
        body,
        name=name,
        grid=(S // tr,),
        in_specs=[row(W), row(W), row(MLA_HEADS * V_HEAD), row(128), row(128)],
        out_specs=[row(W), row(W), row(128)],
        out_shape=[jax.ShapeDtypeStruct((S, W), BF16), jax.ShapeDtypeStruct((S, W), BF16), jax.ShapeDtypeStruct((S, 128), F32)],
        compiler_params=pltpu.CompilerParams(dimension_semantics=("parallel",)),
    )(dq_att, dk_att, dv, cos_t, sin_t)


def _lat_bwd(dcqn, dckvn, dkpe, lat, gq, gkv, name, tr=256):
    S = lat.shape[0]

    def body(dcq_ref, dckv_ref, dkpe_ref, lat_ref, gq_ref, gkv_ref, dlat_ref, dgq_ref, dgkv_ref):
        dq, dgq = _rms_bwd_math(dcq_ref[...], lat_ref[:, :Q_LORA], gq_ref[...])
        dkv, dgkv = _rms_bwd_math(dckv_ref[...], lat_ref[:, Q_LORA : Q_LORA + KV_LORA], gkv_ref[...])
        dlat_ref[:, :Q_LORA] = dq.astype(BF16)
        dlat_ref[:, Q_LORA : Q_LORA + KV_LORA] = dkv.astype(BF16)
        dlat_ref[:, Q_LORA + KV_LORA :] = dkpe_ref[...].astype(BF16)

        @pl.when(pl.program_id(0) == 0)
        def _():
            dgq_ref[...] = dgq
            dgkv_ref[...] = dgkv

        @pl.when(pl.program_id(0) != 0)
        def _():
            dgq_ref[...] += dgq
            dgkv_ref[...] += dgkv

    def row(n):
        return pl.BlockSpec((tr, n), lambda i: (i, 0))

    def vec(n):
        return pl.BlockSpec((1, n), lambda i: (0, 0))

    return pl.pallas_call(
        body,
        name=name,
        grid=(S // tr,),
        in_specs=[row(Q_LORA), row(KV_LORA), row(128), row(LAT_PAD), vec(Q_LORA), vec(KV_LORA)],
        out_specs=[row(LAT_PAD), vec(Q_LORA), vec(KV_LORA)],
        out_shape=[
            jax.ShapeDtypeStruct((S, LAT_PAD), BF16),
            jax.ShapeDtypeStruct((1, Q_LORA), F32),
            jax.ShapeDtypeStruct((1, KV_LORA), F32),
        ],
        compiler_params=pltpu.CompilerParams(dimension_semantics=("arbitrary",)),
    )(dcqn, dckvn, dkpe, lat, gq, gkv)


MLA_SCALE = (QK_NOPE + QK_ROPE) ** -0.5
LOG2E = 1.4426950408889634
MLA_C2 = MLA_SCALE * LOG2E
FLASH_T = 1024


def _causal_pairs(n, by_key):
    pairs = [(i, j) for j in range(n) for i in range(j, n)] if by_key else [(i, j) for i in range(n) for j in range(i + 1)]
    return jnp.asarray([p[0] for p in pairs], jnp.int32), jnp.asarray([p[1] for p in pairs], jnp.int32)


def _lanes(x, n):
    return jnp.tile(x, (1, n // 128))


def _flash_grid(npairs, in_specs, out_specs, scratch):
    return pltpu.PrefetchScalarGridSpec(
        num_scalar_prefetch=2, grid=(MLA_HEADS, npairs), in_specs=in_specs, out_specs=out_specs, scratch_shapes=scratch
    )


def _flash2_fwd(q_att, k_att, kv, name, t=FLASH_T):
    S = q_att.shape[0]
    qi_tab, kj_tab = _causal_pairs(S // t, by_key=False)

    def body(qi_ref, kj_ref, q_ref, k_ref, v_ref, o_ref, lse_ref, m_sc, l_sc, acc_sc):
        step = pl.program_id(1)
        qi, kj = qi_ref[step], kj_ref[step]

        @pl.when(kj == 0)
        def _():
            m_sc[...] = jnp.full((t, 128), NEG, F32)
            l_sc[...] = jnp.zeros((t, 128), F32)
            acc_sc[...] = jnp.zeros((t, V_HEAD), F32)

        def update(s):
            m_prev = m_sc[...]
            m_new = jnp.maximum(m_prev, jnp.max(s, axis=1, keepdims=True))
            p = jnp.exp2((s - _lanes(m_new, t)) * MLA_C2)
            alpha = jnp.exp2((m_prev - m_new) * MLA_C2)
            l_sc[...] = alpha * l_sc[...] + jnp.sum(p, axis=1, keepdims=True)
            acc_sc[...] = alpha * acc_sc[...] + _dot(p.astype(BF16), v_ref[...])
            m_sc[...] = m_new

        @pl.when(kj < qi)
        def _():
            update(_dot(q_ref[...], k_ref[...], NT))

        @pl.when(kj == qi)
        def _():
            s = _dot(q_ref[...], k_ref[...], NT)
            rows = lax.broadcasted_iota(jnp.int32, s.shape, 0)
            cols = lax.broadcasted_iota(jnp.int32, s.shape, 1)
            update(jnp.where(cols <= rows, s, NEG))
            l = l_sc[...]
            o_ref[...] = acc_sc[...] / l
            lse_ref[0] = m_sc[...] * MLA_SCALE + jnp.log(l)

    return pl.pallas_call(
        body,
        name=name,
        grid_spec=_flash_grid(
            qi_tab.shape[0],
            [
                pl.BlockSpec((t, HEAD_PAD), lambda h, p, qi, kj: (qi[p], h)),
                pl.BlockSpec((t, HEAD_PAD), lambda h, p, qi, kj: (kj[p], h)),
                pl.BlockSpec((t, V_HEAD), lambda h, p, qi, kj: (kj[p], 2 * h + 1)),
            ],
            [
                pl.BlockSpec((t, V_HEAD), lambda h, p, qi, kj: (qi[p], h)),
                pl.BlockSpec((1, t, 128), lambda h, p, qi, kj: (h, qi[p], 0)),
            ],
            [pltpu.VMEM((t, 128), F32), pltpu.VMEM((t, 128), F32), pltpu.VMEM((t, V_HEAD), F32)],
        ),
        out_shape=[jax.ShapeDtypeStruct((S, MLA_HEADS * V_HEAD), F32), jax.ShapeDtypeStruct((MLA_HEADS, S, 128), F32)],
        compiler_params=pltpu.CompilerParams(dimension_semantics=("parallel", "arbitrary")),
    )(qi_tab, kj_tab, q_att, k_att, kv)


def _flash_delta(do, o, name, tr=512):
    S = o.shape[0]

    def body(do_ref, o_ref, d_ref):
        lane = lax.broadcasted_iota(jnp.int32, (tr, 128), 1)
        acc = jnp.zeros((tr, 128), F32)
        for h in range(MLA_HEADS):
            sl = slice(h * V_HEAD, (h + 1) * V_HEAD)
            acc = jnp.where(lane == h, jnp.sum(do_ref[:, sl].astype(F32) * o_ref[:, sl], axis=1, keepdims=True), acc)
        d_ref[...] = acc

    row = pl.BlockSpec((tr, MLA_HEADS * V_HEAD), lambda i: (i, 0))
    return pl.pallas_call(
        body,
        name=name,
        grid=(S // tr,),
        in_specs=[row, row],
        out_specs=pl.BlockSpec((tr, 128), lambda i: (i, 0)),
        out_shape=jax.ShapeDtypeStruct((S, 128), F32),
        compiler_params=pltpu.CompilerParams(dimension_semantics=("parallel",)),
    )(do, o)


def _flash2_bwd(q_att, k_att, kv, do, lse_row, delta_row, name, t=FLASH_T):
    S = q_att.shape[0]
    n = S // t
    qi_tab, kj_tab = _causal_pairs(n, by_key=True)
    last = qi_tab.shape[0] - 1

    def body(qi_ref, kj_ref, q_ref, k_ref, v_ref, do_ref, lse_ref, dl_ref, dq_ref, dk_ref, dv_ref, dk_sc, dv_sc):
        step = pl.program_id(1)
        qi, kj = qi_ref[step], kj_ref[step]

        @pl.when(step == 0)
        def _():
            dq_ref[...] = jnp.zeros((S, HEAD_PAD), F32)

        def update(st):
            q, do_v = q_ref[...], do_ref[...]
            pt = jnp.exp2(st * MLA_C2 - lse_ref[0] * LOG2E)
            dv_sc[...] += _dot(pt.astype(BF16), do_v)
            dpt = _dot(v_ref[...], do_v, NT)
            dst = (pt * (dpt - dl_ref[0])).astype(BF16)
            dk_sc[...] += _dot(dst, q)
            rows = pl.ds(pl.multiple_of(qi * t, t), t)
            dq_ref[rows, :] += _dot(dst, k_ref[...], TN)

        @pl.when(qi == kj)
        def _():
            dk_sc[...] = jnp.zeros((t, HEAD_PAD), F32)
            dv_sc[...] = jnp.zeros((t, V_HEAD), F32)
            st = _dot(k_ref[...], q_ref[...], NT)
            keys = lax.broadcasted_iota(jnp.int32, st.shape, 0)
            qs = lax.broadcasted_iota(jnp.int32, st.shape, 1)
            update(jnp.where(keys <= qs, st, NEG))

        @pl.when(qi > kj)
        def _():
            update(_dot(k_ref[...], q_ref[...], NT))

        @pl.when(qi == n - 1)
        def _():
            dk_ref[...] = dk_sc[...] * MLA_SCALE
            dv_ref[...] = dv_sc[...]

        @pl.when(step == last)
        def _():
            dq_ref[...] = dq_ref[...] * MLA_SCALE

    qrow = lambda h, p, qi, kj: (qi[p], h)
    krow = lambda h, p, qi, kj: (kj[p], h)
    stat = pl.BlockSpec((1, 1, t), lambda h, p, qi, kj: (h, 0, qi[p]))
    return pl.pallas_call(
        body,
        name=name,
        grid_spec=_flash_grid(
            qi_tab.shape[0],
            [
                pl.BlockSpec((t, HEAD_PAD), qrow),
                pl.BlockSpec((t, HEAD_PAD), krow),
                pl.BlockSpec((t, V_HEAD), lambda h, p, qi, kj: (kj[p], 2 * h + 1)),
                pl.BlockSpec((t, V_HEAD), qrow),
                stat,
                stat,
            ],
            [
                pl.BlockSpec((S, HEAD_PAD), lambda h, p, qi, kj: (0, h)),
                pl.BlockSpec((t, HEAD_PAD), krow),
                pl.BlockSpec((t, V_HEAD), krow),
            ],
            [pltpu.VMEM((t, HEAD_PAD), F32), pltpu.VMEM((t, V_HEAD), F32)],
        ),
        out_shape=[
            jax.ShapeDtypeStruct((S, MLA_HEADS * HEAD_PAD), F32),
            jax.ShapeDtypeStruct((S, MLA_HEADS * HEAD_PAD), F32),
            jax.ShapeDtypeStruct((S, MLA_HEADS * V_HEAD), F32),
        ],
        compiler_params=pltpu.CompilerParams(dimension_semantics=("parallel", "arbitrary")),
    )(qi_tab, kj_tab, q_att, k_att, kv, do, lse_row, delta_row)


DIL_SCALE = DIL_HD**-0.5


def _dil_bias():
    slopes = 2.0 ** (-ALIBI_MAX_BIAS * np.arange(1, DIL_HEADS + 1, dtype=np.float64) / DIL_HEADS)
    slopes = slopes.astype(np.float32).reshape(DIL_GROUPS, DIL_HG)
    p = np.arange(DIL_BLK)[:, None]
    kidx = np.arange(2 * DIL_BLK)[None, :]
    j = p + DIL_BLK - kidx
    out = np.zeros((DIL_GROUPS, DIL_HG, DIL_BLK, 2 * DIL_BLK), np.float32)
    for g, (window, dil) in enumerate(DIL_PATTERNS):
        valid = (j >= 0) & (j <= window // dil)
        for h in range(DIL_HG):
            alibi = -slopes[g, h] * (dil * j).astype(np.float32)
            out[g, h] = np.where(valid, alibi, np.float32(NEG))
    return jnp.asarray(out)


DIL_UNROLL = 4


def _unrolled_loop(lo, hi, fn, unroll=DIL_UNROLL):
    groups = (hi - lo) // unroll
    done = lo
    if groups > 1:

        def step(i, carry):
            for u in range(unroll):
                fn(lo + i * unroll + u)
            return carry

        lax.fori_loop(0, groups, step, 0)
        done = lo + groups * unroll
    for n in range(done, hi):
        fn(n)


def _dil_rows(r, n, count, dil):
    if dil == 1:
        if isinstance(n, int):
            return slice(n * DIL_BLK, (n + count) * DIL_BLK)
        return pl.ds(pl.multiple_of(n * DIL_BLK, DIL_BLK), count * DIL_BLK)
    return pl.ds(n * DIL_BLK * dil + r, count * DIL_BLK, stride=dil)


def _dil_each_block(S, dil, block):
    nb = S // dil // DIL_BLK
    if dil == 1:
        block(0, 0, True)
        _unrolled_loop(1, nb, lambda n: block(0, n, False))
    else:
        for r in range(dil):
            for n in range(nb):
                block(r, n, n == 0)


def _dil_col(g, part, h):
    return (g * 3 + part) * DIL_HG + h


def _dil_fwd_group(dqkv, bias_g, g, dil, name):
    S = dqkv.shape[0]

    def body(bias_ref, q_ref, k_ref, v_ref, o_ref, lse_ref):
        def block(r, n, first):
            cur = _dil_rows(r, n, 1, dil)
            both = cur if first else _dil_rows(r, n - 1, 2, dil)
            b = bias_ref[0][:, DIL_BLK:] if first else bias_ref[0]
            q, kk, vv = q_ref[cur, :].astype(BF16), k_ref[both, :].astype(BF16), v_ref[both, :].astype(BF16)
            s = _dot(q, kk, NT) * DIL_SCALE + b
            m = jnp.max(s, axis=1, keepdims=True)
            e = jnp.exp(s - m)
            l = jnp.sum(e, axis=1, keepdims=True)
            p = e * (1.0 / l)
            o_ref[cur, :] = _dot(p.astype(BF16), vv)
            lse_ref[cur, :] = jnp.broadcast_to(m + jnp.log(l), (DIL_BLK, 128))

        _dil_each_block(S, dil, block)

    def col(part):
        return pl.BlockSpec((S, DIL_HD), lambda h: (0, _dil_col(g, part, h)))

    out = pl.BlockSpec((S, DIL_HD), lambda h: (0, h))
    return pl.pallas_call(
        body,
        name=name,
        grid=(DIL_HG,),
        in_specs=[pl.BlockSpec((1, DIL_BLK, 2 * DIL_BLK), lambda h: (h, 0, 0)), col(0), col(1), col(2)],
        out_specs=[out, out],
        out_shape=[jax.ShapeDtypeStruct((S, DIL_OUT), F32), jax.ShapeDtypeStruct((S, DIL_OUT), F32)],
        compiler_params=pltpu.CompilerParams(dimension_semantics=("parallel",)),
    )(bias_g, dqkv, dqkv, dqkv)


def _dil_combine(os_, ls_, name, tr=512):
    S = os_[0].shape[0]

    def body(o0, o1, o2, l0, l1, l2, out_ref, lse_ref):
        a, b, c = l0[...], l1[...], l2[...]
        m = jnp.maximum(jnp.maximum(a, b), c)
        ea, eb, ec = jnp.exp(a - m), jnp.exp(b - m), jnp.exp(c - m)
        den = ea + eb + ec
        inv = 1.0 / den
        out_ref[...] = (ea * inv) * o0[...] + (eb * inv) * o1[...] + (ec * inv) * o2[...]
        lse_ref[...] = m + jnp.log(den)

    row = pl.BlockSpec((tr, DIL_OUT), lambda i: (i, 0))
    return pl.pallas_call(
        body,
        name=name,
        grid=(S // tr,),
        in_specs=[row] * 6,
        out_specs=[row, row],
        out_shape=[jax.ShapeDtypeStruct((S, DIL_OUT), F32)] * 2,
        compiler_params=pltpu.CompilerParams(dimension_semantics=("parallel",)),
    )(*os_, *ls_)


def _dil_rowdot(dod, od, name, tr=512):
    S = dod.shape[0]

    def body(d_ref, o_ref, dd_ref):
        for h in range(DIL_HG):
            sl = slice(h * 128, (h + 1) * 128)
            sm = jnp.sum(d_ref[:, sl] * o_ref[:, sl], axis=1, keepdims=True)
            dd_ref[:, sl] = jnp.broadcast_to(sm, (tr, 128))

    row = pl.BlockSpec((tr, DIL_OUT), lambda i: (i, 0))
    return pl.pallas_call(
        body,
        name=name,
        grid=(S // tr,),
        in_specs=[row, row],
        out_specs=row,
        out_shape=jax.ShapeDtypeStruct((S, DIL_OUT), F32),
        compiler_params=pltpu.CompilerParams(dimension_semantics=("parallel",)),
    )(dod, od)


def _dil_bwd_group(dqkv, bias_g, dod, dd, lse, grads, g, dil, name):
    S = dqkv.shape[0]

    def body(bias_ref, q_ref, k_ref, v_ref, do_ref, dd_ref, lse_ref, _, out_ref):
        out_ref[1] = jnp.zeros((S, DIL_HD), F32)
        out_ref[2] = jnp.zeros((S, DIL_HD), F32)

        def block(r, n, first):
            cur = _dil_rows(r, n, 1, dil)
            both = cur if first else _dil_rows(r, n - 1, 2, dil)
            b = bias_ref[0][:, DIL_BLK:] if first else bias_ref[0]
            q, kk, vv = q_ref[cur, :].astype(BF16), k_ref[both, :].astype(BF16), v_ref[both, :].astype(BF16)
            do = do_ref[cur, :].astype(BF16)
            s = _dot(q, kk, NT) * DIL_SCALE + b
            p = jnp.exp(s - lse_ref[cur, 0:1])
            dp = _dot(do, vv, NT)
            ds = ((p * (dp - dd_ref[cur, 0:1])) * DIL_SCALE).astype(BF16)
            out_ref[0, cur, :] = _dot(ds, kk)
            out_ref[1, both, :] += _dot(ds, q, TN)
            out_ref[2, both, :] += _dot(p.astype(BF16), do, TN)

        _dil_each_block(S, dil, block)

    def col(part):
        return pl.BlockSpec((S, DIL_HD), lambda h: (0, _dil_col(g, part, h)))

    nat = pl.BlockSpec((S, DIL_HD), lambda h: (0, h))
    return pl.pallas_call(
        body,
        name=name,
        grid=(DIL_HG,),
        in_specs=[pl.BlockSpec((1, DIL_BLK, 2 * DIL_BLK), lambda h: (h, 0, 0)), col(0), col(1), col(2), nat, nat, nat, ANY],
        out_specs=pl.BlockSpec((3, S, DIL_HD), lambda h: (g, 0, h)),
        out_shape=jax.ShapeDtypeStruct(grads.shape, F32),
        input_output_aliases={7: 0},
        compiler_params=pltpu.CompilerParams(dimension_semantics=("parallel",)),
    )(bias_g, dqkv, dqkv, dqkv, dod, dd, lse, grads)


def _merge_fwd(gates, o_a, o_b, name, tr=256):
    S = o_a.shape[0]

    def body(ga_ref, gb_ref, oa_ref, ob_ref, m_ref):
        m_ref[...] = (ga_ref[...] * oa_ref[...] + gb_ref[...] * ob_ref[...]).astype(BF16)

    row = pl.BlockSpec((tr, D_MODEL), lambda i: (i, 0))
    return pl.pallas_call(
        body,
        name=name,
        grid=(S // tr,),
        in_specs=[row, pl.BlockSpec((tr, D_MODEL), lambda i: (i, 1)), row, row],
        out_specs=row,
        out_shape=jax.ShapeDtypeStruct((S, D_MODEL), BF16),
        compiler_params=pltpu.CompilerParams(dimension_semantics=("parallel",)),
    )(gates, gates, o_a, o_b)


def _merge_bwd(dmrg, gates, o_a, o_b, name, tr=256):
    S = o_a.shape[0]

    def body(dm_ref, ga_ref, gb_ref, oa_ref, ob_ref, doa_ref, dob_ref, dga_ref, dgb_ref, dba_ref, dbb_ref):
        dm, ga, gb = dm_ref[...], ga_ref[...], gb_ref[...]
        doa_ref[...] = (dm * ga).astype(BF16)
        dob_ref[...] = (dm * gb).astype(BF16)
        dga = (dm * oa_ref[...]) * (ga * (1.0 - ga))
        dgb = (dm * ob_ref[...]) * (gb * (1.0 - gb))
        dga_ref[...] = dga.astype(BF16)
        dgb_ref[...] = dgb.astype(BF16)
        sa = jnp.sum(dga, axis=0, keepdims=True)
        sb = jnp.sum(dgb, axis=0, keepdims=True)

        @pl.when(pl.program_id(0) == 0)
        def _():
            dba_ref[...] = sa
            dbb_ref[...] = sb

        @pl.when(pl.program_id(0) != 0)
        def _():
            dba_ref[...] += sa
            dbb_ref[...] += sb

    row = pl.BlockSpec((tr, D_MODEL), lambda i: (i, 0))
    row1 = pl.BlockSpec((tr, D_MODEL), lambda i: (i, 1))
    vec = pl.BlockSpec((1, D_MODEL), lambda i: (0, 0))
    outs = pl.pallas_call(
        body,
        name=name,
        grid=(S // tr,),
        in_specs=[row, row, row1, row, row],
        out_specs=[row, row, row, row, vec, vec],
        out_shape=[jax.ShapeDtypeStruct((S, D_MODEL), BF16)] * 4 + [jax.ShapeDtypeStruct((1, D_MODEL), F32)] * 2,
        compiler_params=pltpu.CompilerParams(dimension_semantics=("arbitrary",)),
    )(dmrg, gates, gates, o_a, o_b)
    return outs


CONV_TR = 512
CONV_TC = 512
N_FFC = D_FF_PAD // CONV_TC


def _conv_taps(x, before, w_ref, b_ref):
    x0 = jnp.concatenate([before, x], axis=0)
    x1 = pltpu.roll(x0, 1, 0)
    x2 = pltpu.roll(x0, 2, 0)
    u = ((b_ref[...] + w_ref[0:1, :] * x2) + w_ref[1:2, :] * x1) + w_ref[2:3, :] * x0
    return u, x0, x1, x2


def _prev_halo(tr):
    return lambda i, j: (jnp.maximum(i * (tr // 8) - 1, 0), j)


def _ffn_fwd(u0, cw, cb, name):
    S = u0.shape[0]
    tr, tc = CONV_TR, CONV_TC

    def body(up_ref, gt_ref, hup_ref, hgt_ref, wu_ref, wg_ref, bu_ref, bg_ref, a_ref):
        live = (pl.program_id(0) > 0).astype(F32)
        up = _conv_taps(up_ref[...], hup_ref[...] * live, wu_ref, bu_ref)[0][8:]
        gt = _conv_taps(gt_ref[...], hgt_ref[...] * live, wg_ref, bg_ref)[0][8:]
        a_ref[...] = ((gt * jax.nn.sigmoid(gt)) * up).astype(BF16)

    return pl.pallas_call(
        body,
        name=name,
        grid=(S // tr, N_FFC),
        in_specs=[
            pl.BlockSpec((tr, tc), lambda i, j: (i, j)),
            pl.BlockSpec((tr, tc), lambda i, j: (i, j + N_FFC)),
            pl.BlockSpec((8, tc), _prev_halo(tr)),
            pl.BlockSpec((8, tc), lambda i, j: (jnp.maximum(i * (tr // 8) - 1, 0), j + N_FFC)),
            pl.BlockSpec((8, tc), lambda i, j: (0, j)),
            pl.BlockSpec((8, tc), lambda i, j: (0, j + N_FFC)),
            pl.BlockSpec((1, tc), lambda i, j: (0, j)),
            pl.BlockSpec((1, tc), lambda i, j: (0, j + N_FFC)),
        ],
        out_specs=pl.BlockSpec((tr, tc), lambda i, j: (i, j)),
        out_shape=jax.ShapeDtypeStruct((S, D_FF_PAD), BF16),
        compiler_params=pltpu.CompilerParams(dimension_semantics=("parallel", "parallel")),
    )(u0, u0, u0, u0, cw, cw, cb, cb)


def _ffn_bwd(u0, da, cw, cb, name):
    S = u0.shape[0]
    tr, tc = CONV_TR, CONV_TC
    nrow, te = S // tr, tr + 8

    def body(up_ref, gt_ref, hup_ref, hgt_ref, nup_ref, ngt_ref, da_ref, nda_ref, wu_ref, wg_ref, bu_ref, bg_ref, du0_ref, dcw_ref, dcb_ref):
        i = pl.program_id(1)
        prev_live = (i > 0).astype(F32)
        next_live = (i < nrow - 1).astype(F32)

        def conv(x_ref, nx_ref, h_ref, w_ref, b_ref):
            x = jnp.concatenate([x_ref[...], nx_ref[...] * next_live], axis=0)
            return [t[8:] for t in _conv_taps(x, h_ref[...] * prev_live, w_ref, b_ref)]

        up, xu0, xu1, xu2 = conv(up_ref, nup_ref, hup_ref, wu_ref, bu_ref)
        gt, xg0, xg1, xg2 = conv(gt_ref, ngt_ref, hgt_ref, wg_ref, bg_ref)
        da_v = jnp.concatenate([da_ref[...], nda_ref[...] * next_live], axis=0)
        sg = jax.nn.sigmoid(gt)
        d_up = da_v * (gt * sg)
        d_gt = (da_v * up) * (sg * (1.0 + gt * (1.0 - sg)))
        tap = lax.broadcasted_iota(jnp.int32, (8, tc), 0)

        def finish(half, du, x0, x1, x2, w_ref):
            n1 = pltpu.roll(du, te - 1, 0)
            n2 = pltpu.roll(du, te - 2, 0)
            du0 = (w_ref[2:3, :] * du + w_ref[1:2, :] * n1) + w_ref[0:1, :] * n2
            du0_ref[half] = du0[:tr].astype(BF16)
            d = du[:tr]
            dcw = jnp.where(
                tap == 0,
                jnp.sum(d * x2[:tr], axis=0, keepdims=True),
                jnp.where(tap == 1, jnp.sum(d * x1[:tr], axis=0, keepdims=True), jnp.where(tap == 2, jnp.sum(d * x0[:tr], axis=0, keepdims=True), 0.0)),
            )
            dcb = jnp.sum(d, axis=0, keepdims=True)

            @pl.when(i == 0)
            def _():
                dcw_ref[half] = dcw
                dcb_ref[half] = dcb

            @pl.when(i != 0)
            def _():
                dcw_ref[half] += dcw
                dcb_ref[half] += dcb

        finish(0, d_up, xu0, xu1, xu2, wu_ref)
        finish(1, d_gt, xg0, xg1, xg2, wg_ref)

    def prev8(off):
        return pl.BlockSpec((8, tc), lambda j, i: (jnp.maximum(i * (tr // 8) - 1, 0), j + off))

    def next8(off):
        return pl.BlockSpec((8, tc), lambda j, i: (jnp.minimum((i + 1) * (tr // 8), S // 8 - 1), j + off))

    return pl.pallas_call(
        body,
        name=name,
        grid=(N_FFC, nrow),
        in_specs=[
            pl.BlockSpec((tr, tc), lambda j, i: (i, j)),
            pl.BlockSpec((tr, tc), lambda j, i: (i, j + N_FFC)),
            prev8(0),
            prev8(N_FFC),
            next8(0),
            next8(N_FFC),
            pl.BlockSpec((tr, tc), lambda j, i: (i, j)),
            next8(0),
            pl.BlockSpec((8, tc), lambda j, i: (0, j)),
            pl.BlockSpec((8, tc), lambda j, i: (0, j + N_FFC)),
            pl.BlockSpec((1, tc), lambda j, i: (0, j)),
            pl.BlockSpec((1, tc), lambda j, i: (0, j + N_FFC)),
        ],
        out_specs=[
            pl.BlockSpec((2, tr, tc), lambda j, i: (0, i, j)),
            pl.BlockSpec((2, 8, tc), lambda j, i: (0, 0, j)),
            pl.BlockSpec((2, 1, tc), lambda j, i: (0, 0, j)),
        ],
        out_shape=[
            jax.ShapeDtypeStruct((2, S, D_FF_PAD), BF16),
            jax.ShapeDtypeStruct((2, 8, D_FF_PAD), F32),
            jax.ShapeDtypeStruct((2, 1, D_FF_PAD), F32),
        ],
        compiler_params=pltpu.CompilerParams(dimension_semantics=("parallel", "arbitrary")),
    )(u0, u0, u0, u0, u0, u0, da, da, cw, cw, cb, cb)


ADAMW_BLOCK_BYTES = 3 << 20


def _adamw(w, g, m, v, name):
    R, C = w.shape
    fits = [t for t in range(8, R + 1, 8) if R % t == 0 and t * C * 4 <= ADAMW_BLOCK_BYTES]
    tr = max(fits) if fits else R

    def body(w_ref, g_ref, m_ref, v_ref, d_ref, nm_ref, nv_ref):
        gv = g_ref[...]
        nm = ADAM_B1 * m_ref[...] + (1.0 - ADAM_B1) * gv
        nv = ADAM_B2 * v_ref[...] + (1.0 - ADAM_B2) * (gv * gv)
        m_hat = nm / (1.0 - ADAM_B1**ADAM_STEP)
        v_hat = nv / (1.0 - ADAM_B2**ADAM_STEP)
        d_ref[...] = -ADAM_LR * (m_hat / (jnp.sqrt(v_hat) + ADAM_EPS) + ADAM_WD * w_ref[...])
        nm_ref[...] = nm
        nv_ref[...] = nv

    blk = pl.BlockSpec((tr, C), lambda i: (i, 0))
    return pl.pallas_call(
        body,
        name=name,
        grid=(R // tr,),
        in_specs=[blk] * 4,
        out_specs=[blk] * 3,
        out_shape=[jax.ShapeDtypeStruct((R, C), F32)] * 3,
        compiler_params=pltpu.CompilerParams(dimension_semantics=("parallel",)),
    )(w, g, m, v)


ANY = pl.BlockSpec(memory_space=pl.ANY)


def _all_gather(blocks, name):
    n = len(blocks)

    def body(*refs):
        x_refs, out_refs = refs[:n], refs[n : 2 * n]
        send_sems, recv_sems, local_sems = refs[2 * n :]
        x, y, c = lax.axis_index("x"), lax.axis_index("y"), lax.axis_index("c")
        me, sibling = (x, y, c), (x, y, 1 - c)
        chips = [(1 - x, y), (x, 1 - y), (1 - x, 1 - y)]

        def slot(a, px, py, pc):
            return out_refs[a].at[4 * px + 2 * py + pc]

        def copy(a, k, blk, to, src=None):
            return pltpu.make_async_remote_copy(
                src_ref=slot(a, *blk) if src is None else src,
                dst_ref=slot(a, *blk),
                send_sem=send_sems.at[7 * a + k],
                recv_sem=recv_sems.at[7 * a + k],
                device_id=to,
                device_id_type=MESH,
            )

        mine = [pltpu.make_async_copy(x_refs[a], slot(a, *me), local_sems.at[a]) for a in range(n)]
        sent = []
        for a in range(n):
            mine[a].start()
            first = [copy(a, 0, me, sibling, src=x_refs[a])]
            first += [copy(a, 1 + j, me, (*chip, c), src=x_refs[a]) for j, chip in enumerate(chips)]
            for cp in first:
                cp.start()
            sent += first
        for a in range(n):
            for j, chip in enumerate(chips):
                copy(a, 1 + j, (*chip, c), me).wait_recv()
                passed = copy(a, 4 + j, (*chip, c), sibling)
                passed.start()
                sent.append(passed)
        for a in range(n):
            copy(a, 0, sibling, me).wait_recv()
            for j, chip in enumerate(chips):
                copy(a, 4 + j, (*chip, 1 - c), me).wait_recv()
        for cp in sent:
            cp.wait_send()
        for cp in mine:
            cp.wait()

    return pl.pallas_call(
        body,
        name=name,
        out_shape=[jax.ShapeDtypeStruct((N_DEV,) + b.shape, b.dtype) for b in blocks],
        in_specs=[ANY] * n,
        out_specs=[ANY] * n,
        scratch_shapes=[pltpu.SemaphoreType.DMA((7 * n,)), pltpu.SemaphoreType.DMA((7 * n,)), pltpu.SemaphoreType.DMA((n,))],
    )(*blocks)


def _pair_exchange(gs, name):
    n = len(gs)

    def body(*refs):
        g_refs, out_refs = refs[:n], refs[n : 2 * n]
        send_sems, recv_sems = refs[2 * n :]
        x, y, c = lax.axis_index("x"), lax.axis_index("y"), lax.axis_index("c")
        copies = [
            pltpu.make_async_remote_copy(
                src_ref=g_refs[a].at[2 * k + (1 - c)],
                dst_ref=out_refs[a].at[k],
                send_sem=send_sems.at[N_CHIP * a + k],
                recv_sem=recv_sems.at[N_CHIP * a + k],
                device_id=(x, y, 1 - c),
                device_id_type=MESH,
            )
            for a in range(n)
            for k in range(N_CHIP)
        ]
        for cp in copies:
            cp.start()
        for cp in copies:
            cp.wait()

    return pl.pallas_call(
        body,
        name=name,
        out_shape=[jax.ShapeDtypeStruct((N_CHIP,) + g.shape[1:], g.dtype) for g in gs],
        in_specs=[ANY] * n,
        out_specs=[ANY] * n,
        scratch_shapes=[pltpu.SemaphoreType.DMA((N_CHIP * n,)), pltpu.SemaphoreType.DMA((N_CHIP * n,))],
    )(*gs)


def _row_tile(rows):
    return max(t for t in range(16, 353, 16) if rows % t == 0)


def _pair_add(g, recv, core, name):
    _, R, C = g.shape
    tr = _row_tile(R)

    def body(core_ref, g_ref, r_ref, o_ref):
        o_ref[...] = (g_ref[...].astype(F32) + r_ref[...].astype(F32)).astype(o_ref.dtype)

    return pl.pallas_call(
        body,
        name=name,
        grid_spec=pltpu.PrefetchScalarGridSpec(
            num_scalar_prefetch=1,
            grid=(N_CHIP, R // tr),
            in_specs=[
                pl.BlockSpec((1, tr, C), lambda k, i, core_ref: (2 * k + core_ref[0], i, 0)),
                pl.BlockSpec((1, tr, C), lambda k, i, core_ref: (k, i, 0)),
            ],
            out_specs=pl.BlockSpec((1, tr, C), lambda k, i, core_ref: (k, i, 0)),
        ),
        out_shape=jax.ShapeDtypeStruct((N_CHIP, R, C), g.dtype),
        compiler_params=pltpu.CompilerParams(dimension_semantics=("parallel", "parallel")),
    )(core, g, recv)


HBM = pl.BlockSpec(memory_space=pltpu.HBM)
SEM = pl.BlockSpec(memory_space=pltpu.SEMAPHORE)
EFFECT = pltpu.SideEffectType.DATAFLOW_SIDE_EFFECTING
RELATIONS = tuple((dx, dy, dc) for dx in (0, 1) for dy in (0, 1) for dc in (0, 1))[1:]


def _related(rel):
    x, y, c = lax.axis_index("x"), lax.axis_index("y"), lax.axis_index("c")
    return (1 - x if rel[0] else x, 1 - y if rel[1] else y, 1 - c if rel[2] else c)


def _dev_index(pos):
    return 4 * pos[0] + 2 * pos[1] + pos[2]


def _peers(chips):
    if chips:
        return [r for r in RELATIONS if not r[2]], N_CHIP, lambda pos: 2 * pos[0] + pos[1]
    return list(RELATIONS), N_DEV, _dev_index


def _exchange_start(srcs, by_slot, after, name, chips=False):
    n = len(srcs)
    rels, slots, slot_of = _peers(chips)
    lands = [lax.empty((slots,) + (s.shape[1:] if by_slot else s.shape), s.dtype) for s in srcs]
    nsem = len(rels) * n

    def body(*refs):
        src_refs, land_refs = refs[:n], refs[n : 2 * n]
        send_sems, recv_sems = refs[2 * n + 1], refs[2 * n + 2]
        token = refs[-1]
        me = slot_of(_related((0, 0, 0)))
        for a in range(n):
            for k, rel in enumerate(rels):
                peer = _related(rel)
                pltpu.make_async_remote_copy(
                    src_ref=src_refs[a].at[slot_of(peer)] if by_slot else src_refs[a],
                    dst_ref=land_refs[a].at[me],
                    send_sem=send_sems.at[len(rels) * a + k],
                    recv_sem=recv_sems.at[len(rels) * a + k],
                    device_id=peer,
                    device_id_type=MESH,
                ).start()
        token[...] = jnp.zeros_like(token)

    def hbm(a):
        return pltpu.HBM(a.shape, a.dtype)

    outs = pl.pallas_call(
        body,
        name=name,
        out_shape=(
            pltpu.SemaphoreType.DMA((nsem,)),
            pltpu.SemaphoreType.DMA((nsem,)),
            *[hbm(s) for s in srcs],
            *[hbm(l) for l in lands],
            jax.ShapeDtypeStruct((8, 128), F32),
        ),
        in_specs=[HBM] * (2 * n) + [ANY],
        out_specs=(SEM, SEM, *[HBM] * (2 * n), pl.BlockSpec(memory_space=pltpu.VMEM)),
        input_output_aliases={i: 2 + i for i in range(2 * n)},
        compiler_params=pltpu.CompilerParams(has_side_effects=EFFECT),
    )(*[pltpu.with_memory_space_constraint(a, pltpu.HBM) for a in list(srcs) + lands], after)
    return (outs[0], outs[1], list(outs[2 : 2 + n]), list(outs[2 + n : 2 + 2 * n])), outs[-1]


def _exchange_wait(handle, by_slot, after, name, chips=False):
    send_sems, recv_sems, srcs, lands = handle
    n = len(srcs)
    rels = _peers(chips)[0]

    def body(*refs):
        src_refs, land_refs = refs[:n], refs[n : 2 * n]
        s_sems, r_sems = refs[2 * n], refs[2 * n + 1]
        for a in range(n):
            for k, rel in enumerate(rels):
                copy = pltpu.make_async_remote_copy(
                    src_ref=src_refs[a].at[0] if by_slot else src_refs[a],
                    dst_ref=land_refs[a].at[0],
                    send_sem=s_sems.at[len(rels) * a + k],
                    recv_sem=r_sems.at[len(rels) * a + k],
                    device_id=_related(rel),
                    device_id_type=MESH,
                )
                copy.wait_send()
                copy.wait_recv()

    outs = pl.pallas_call(
        body,
        name=name,
        out_shape=tuple(pltpu.HBM(a.shape, a.dtype) for a in srcs + lands),
        in_specs=[HBM] * (2 * n) + [SEM, SEM, ANY],
        out_specs=tuple([HBM] * (2 * n)),
        input_output_aliases={i: i for i in range(2 * n)},
        compiler_params=pltpu.CompilerParams(has_side_effects=EFFECT),
    )(*srcs, *lands, send_sems, recv_sems, after)
    return list(outs[:n]), list(outs[n:])


NEAR = ((0, 0, 1), (1, 0, 0), (0, 1, 0), (1, 1, 0))


def _gather2_start(blocks, name):
    n = len(blocks)
    lands = [lax.empty((N_DEV,) + b.shape, b.dtype) for b in blocks]

    def body(*refs):
        src_refs, land_refs = refs[:n], refs[n : 2 * n]
        send_sems, recv_sems, token = refs[2 * n], refs[2 * n + 1], refs[-1]
        me = _dev_index(_related((0, 0, 0)))
        for a in range(n):
            for k, rel in enumerate(NEAR):
                pltpu.make_async_remote_copy(
                    src_ref=src_refs[a],
                    dst_ref=land_refs[a].at[me],
                    send_sem=send_sems.at[len(NEAR) * a + k],
                    recv_sem=recv_sems.at[len(NEAR) * a + k],
                    device_id=_related(rel),
                    device_id_type=MESH,
                ).start()
        token[...] = jnp.zeros_like(token)

    nsem = len(NEAR) * n
    outs = pl.pallas_call(
        body,
        name=name,
        out_shape=(
            pltpu.SemaphoreType.DMA((nsem,)),
            pltpu.SemaphoreType.DMA((nsem,)),
            *[pltpu.HBM(a.shape, a.dtype) for a in list(blocks) + lands],
            jax.ShapeDtypeStruct((8, 128), F32),
        ),
        in_specs=[HBM] * (2 * n),
        out_specs=(SEM, SEM, *[HBM] * (2 * n), pl.BlockSpec(memory_space=pltpu.VMEM)),
        input_output_aliases={i: 2 + i for i in range(2 * n)},
        compiler_params=pltpu.CompilerParams(has_side_effects=EFFECT),
    )(*[pltpu.with_memory_space_constraint(a, pltpu.HBM) for a in list(blocks) + lands])
    return (outs[0], outs[1], list(outs[2 : 2 + n]), list(outs[2 + n : 2 + 2 * n])), outs[-1]


def _gather2_forward(handle, after, name):
    send1, recv1, srcs, lands = handle
    n = len(srcs)

    def body(*refs):
        src_refs, land_refs = refs[:n], refs[n : 2 * n]
        s1, r1 = refs[2 * n], refs[2 * n + 1]
        s2, r2 = refs[-2], refs[-1]
        sibling = _related(NEAR[0])
        for a in range(n):
            for k, rel in enumerate(NEAR):
                first = pltpu.make_async_remote_copy(
                    src_ref=src_refs[a],
                    dst_ref=land_refs[a].at[0],
                    send_sem=s1.at[len(NEAR) * a + k],
                    recv_sem=r1.at[len(NEAR) * a + k],
                    device_id=_related(rel),
                    device_id_type=MESH,
                )
                first.wait_send()
                first.wait_recv()
                if k:
                    slot = land_refs[a].at[_dev_index(_related(rel))]
                    pltpu.make_async_remote_copy(
                        src_ref=slot,
                        dst_ref=slot,
                        send_sem=s2.at[3 * a + k - 1],
                        recv_sem=r2.at[3 * a + k - 1],
                        device_id=sibling,
                        device_id_type=MESH,
                    ).start()

    outs = pl.pallas_call(
        body,
        name=name,
        out_shape=(
            *[pltpu.HBM(a.shape, a.dtype) for a in srcs + lands],
            pltpu.SemaphoreType.DMA((3 * n,)),
            pltpu.SemaphoreType.DMA((3 * n,)),
        ),
        in_specs=[HBM] * (2 * n) + [SEM, SEM, ANY],
        out_specs=(*[HBM] * (2 * n), SEM, SEM),
        input_output_aliases={i: i for i in range(2 * n)},
        compiler_params=pltpu.CompilerParams(has_side_effects=EFFECT),
    )(*srcs, *lands, send1, recv1, after)
    return outs[-2], outs[-1], list(outs[:n]), list(outs[n : 2 * n])


def _gather2_wait(handle, name):
    send2, recv2, srcs, lands = handle
    n = len(srcs)

    def body(*refs):
        land_refs = refs[n : 2 * n]
        s2, r2 = refs[2 * n], refs[2 * n + 1]
        for a in range(n):
            for j in range(3):
                passed = pltpu.make_async_remote_copy(
                    src_ref=land_refs[a].at[0],
                    dst_ref=land_refs[a].at[0],
                    send_sem=s2.at[3 * a + j],
                    recv_sem=r2.at[3 * a + j],
                    device_id=_related(NEAR[0]),
                    device_id_type=MESH,
                )
                passed.wait_send()
                passed.wait_recv()

    outs = pl.pallas_call(
        body,
        name=name,
        out_shape=tuple(pltpu.HBM(a.shape, a.dtype) for a in srcs + lands),
        in_specs=[HBM] * (2 * n) + [SEM, SEM],
        out_specs=tuple([HBM] * (2 * n)),
        input_output_aliases={i: i for i in range(2 * n)},
        compiler_params=pltpu.CompilerParams(has_side_effects=EFFECT),
    )(*srcs, *lands, send2, recv2)
    return list(outs[:n]), list(outs[n:])


def _slot_sum(parts, name):
    n, R, C = parts.shape
    tr = _row_tile(R) if R % 16 == 0 else R

    def body(p_ref, o_ref):
        acc = p_ref[0].astype(F32)
        for k in range(1, n):
            acc = acc + p_ref[k].astype(F32)
        o_ref[...] = acc

    return pl.pallas_call(
        body,
        name=name,
        grid=(R // tr,),
        in_specs=[pl.BlockSpec((n, tr, C), lambda i: (0, i, 0))],
        out_specs=pl.BlockSpec((tr, C), lambda i: (i, 0)),
        out_shape=jax.ShapeDtypeStruct((R, C), F32),
        compiler_params=pltpu.CompilerParams(dimension_semantics=("parallel",)),
    )(parts)


W_IN_TC = 256
W_IN_BOUNDS = (0, LAT, LAT + 3 * DIL_QKV, LAT + 3 * DIL_QKV + D_MODEL, D_IN)


def _dqkv_chunks():
    return [((g * 3 + part) * DIL_OUT, LAT + part * DIL_QKV + g * DIL_OUT) for g in range(DIL_GROUPS) for part in range(3)]


def _w_in_regroup(slots, name):
    tc = W_IN_TC

    def body(s_ref, lat_ref, dqkv_ref, g_ref, buf):
        for j in range(N_DEV):
            buf[j * IN_ROWS : (j + 1) * IN_ROWS, :] = s_ref[j].astype(F32)[:IN_ROWS, :]
        lat_ref[:LAT, :] = buf[:LAT, :].astype(BF16)
        lat_ref[LAT:, :] = jnp.zeros((LAT_PAD - LAT, tc), BF16)
        for dst, src in _dqkv_chunks():
            dqkv_ref[dst : dst + DIL_OUT, :] = buf[src : src + DIL_OUT, :].astype(BF16)
        g_ref[...] = buf[W_IN_BOUNDS[2] :, :].astype(BF16)

    def col(rows):
        return pl.BlockSpec((rows, tc), lambda k: (0, k))

    return pl.pallas_call(
        body,
        name=name,
        grid=(D_MODEL // tc,),
        in_specs=[pl.BlockSpec((N_DEV, IN_ROWS_PAD, tc), lambda k: (0, 0, k))],
        out_specs=[col(LAT_PAD), col(3 * DIL_QKV), col(2 * D_MODEL)],
        out_shape=[
            jax.ShapeDtypeStruct((LAT_PAD, D_MODEL), BF16),
            jax.ShapeDtypeStruct((3 * DIL_QKV, D_MODEL), BF16),
            jax.ShapeDtypeStruct((2 * D_MODEL, D_MODEL), BF16),
        ],
        scratch_shapes=[pltpu.VMEM((D_IN, tc), F32)],
        compiler_params=pltpu.CompilerParams(dimension_semantics=("parallel",)),
    )(slots)


def _w_in_grad_regroup(g_lat, g_dqkv, g_ga, g_gb, name):
    tc = W_IN_TC

    def body(lat_ref, dqkv_ref, ga_ref, gb_ref, o_ref, buf):
        b = W_IN_BOUNDS
        buf[b[0] : b[1], :] = lat_ref[:LAT, :].astype(F32)
        for dst, src in _dqkv_chunks():
            buf[src : src + DIL_OUT, :] = dqkv_ref[dst : dst + DIL_OUT, :].astype(F32)
        buf[b[2] : b[3], :] = ga_ref[...].astype(F32)
        buf[b[3] : b[4], :] = gb_ref[...].astype(F32)
        fill = jnp.zeros((IN_ROWS_PAD - IN_ROWS, tc), F32)
        for j in range(N_DEV):
            o_ref[j] = jnp.concatenate([buf[j * IN_ROWS : (j + 1) * IN_ROWS, :], fill], axis=0).astype(BF16)

    def col(rows):
        return pl.BlockSpec((rows, tc), lambda k: (0, k))

    return pl.pallas_call(
        body,
        name=name,
        grid=(D_MODEL // tc,),
        in_specs=[col(LAT_PAD), col(3 * DIL_QKV), col(D_MODEL), col(D_MODEL)],
        out_specs=pl.BlockSpec((N_DEV, IN_ROWS_PAD, tc), lambda k: (0, 0, k)),
        out_shape=jax.ShapeDtypeStruct((N_DEV, IN_ROWS_PAD, D_MODEL), BF16),
        scratch_shapes=[pltpu.VMEM((D_IN, tc), F32)],
        compiler_params=pltpu.CompilerParams(dimension_semantics=("parallel",)),
    )(g_lat, g_dqkv, g_ga, g_gb)


def _ffn_pad(a, axis):
    a = jnp.moveaxis(a, axis, -1)
    g = a.reshape(a.shape[:-1] + (2 * N_DEV, FF_GROUP))
    g = jnp.pad(g, [(0, 0)] * (g.ndim - 1) + [(0, FF_GROUP_PAD - FF_GROUP)])
    return jnp.moveaxis(g.reshape(a.shape[:-1] + (2 * D_FF_PAD,)), -1, axis)


def _ffn_unpad(a, axis):
    a = jnp.moveaxis(a, axis, -1)
    g = a.reshape(a.shape[:-1] + (2 * N_DEV, FF_GROUP_PAD))[..., :FF_GROUP]
    return jnp.moveaxis(g.reshape(a.shape[:-1] + (2 * D_FF,)), -1, axis)


MISC = (("w_o_mla", (256, 1024)), ("w_o_dil", (256, 512)), ("w_uq", (192, 512)), ("w_ukv", (256, 256)))
BIG_WEIGHTS = ("w_in", "w_up", "w_down", "w_out") + tuple(n for n, _ in MISC)


def _exchange_blocks(w):
    def t(a):
        return a.astype(BF16).T

    up = t(w["w_up"]).reshape(2, FF_GROUP, D_MODEL)
    return [
        jnp.pad(t(w["w_in"]), ((0, IN_ROWS_PAD - IN_ROWS), (0, 0))),
        jnp.pad(up, ((0, 0), (0, FF_GROUP_PAD - FF_GROUP), (0, 0))).reshape(2 * FF_GROUP_PAD, D_MODEL),
        jnp.pad(w["w_down"].astype(BF16), ((0, FF_GROUP_PAD - FF_GROUP), (0, 0))),
        w["w_out"].astype(BF16),
        jnp.concatenate([t(w[n]).reshape(-1, D_MODEL) for n, _ in MISC], axis=0),
    ]


def _misc_split(misc):
    out, off = {}, 0
    for n, (r, c) in MISC:
        rows = r * c // D_MODEL
        out[n] = misc[..., off : off + rows, :].reshape(misc.shape[:-2] + (r, c))
        off += rows
    return out


def _small_matrices(g_misc):
    misc = _misc_split(g_misc)
    uq_t = jnp.pad(misc["w_uq"], ((0, 0), (0, HEAD_PAD - QK_NOPE - QK_ROPE), (0, 0)))
    return {
        "uq_t": uq_t.reshape(MLA_HEADS * HEAD_PAD, Q_LORA),
        "ukv_t": misc["w_ukv"].reshape(MLA_HEADS * HEAD_PAD, KV_LORA),
        "o_mla_t": misc["w_o_mla"].reshape(D_MODEL, MLA_HEADS * V_HEAD),
        "o_dil_t": misc["w_o_dil"].reshape(D_MODEL, DIL_OUT),
    }


def _small_grad_blocks(g):
    uq_t = g["uq_t"].reshape(MLA_HEADS, HEAD_PAD, Q_LORA)[:, : QK_NOPE + QK_ROPE]
    misc = {"w_o_mla": g["o_mla_t"], "w_o_dil": g["o_dil_t"], "w_uq": uq_t, "w_ukv": g["ukv_t"]}
    return [
        g["w_out"].reshape(N_DEV, -1, D_MODEL),
        jnp.concatenate([misc[n].reshape(N_DEV, -1, D_MODEL) for n, _ in MISC], axis=1),
    ]


def _grad_shards(sums):
    s_in, s_out, s_misc, s_up, s_down = sums
    out = {
        "w_in": s_in[:IN_ROWS].T,
        "w_up": s_up.reshape(2, FF_GROUP_PAD, D_MODEL)[:, :FF_GROUP].reshape(2 * FF_GROUP, D_MODEL).T,
        "w_down": s_down[:FF_GROUP],
        "w_out": s_out,
    }
    out.update({n: v.T for n, v in _misc_split(s_misc).items()})
    return out


def _local_step(x, tgt, wt, conv_w, small, small_matrices, ffn_weight, send_ffn_grads, send_small_grads, send_w_in_grads, start_token):
    S = x.shape[0]
    lat_t, dqkv_t, g_t = wt
    cw = jnp.pad(_ffn_pad(conv_w, 1), ((0, 5), (0, 0)))
    cb = _ffn_pad(small["conv_b"], 1)
    cos_t, sin_t = _rope_tables(S)
    bias = _dil_bias()
    g1, g2, g3 = small["attn_norm_g"], small["ffn_norm_g"], small["final_norm_g"]
    gq, gkv = small["q_norm_g"], small["kv_norm_g"]

    h = _rms_fwd(x, g1 + start_token, "rms_attn")
    lat = _mm(h, lat_t, "nt", F32, 1024, LAT_PAD, D_MODEL, "proj_lat")
    dqkv = _mm(h, dqkv_t, "nt", F32, 1024, 512, D_MODEL, "proj_dqkv")
    gates = _mm(h, g_t, "nt", F32, 1024, 512, D_MODEL, "proj_gates", bias=small["b_gate"], act="sigmoid")
    sm = small_matrices(gates)
    uq_t, ukv_t, o_mla_t, o_dil_t = sm["uq_t"], sm["ukv_t"], sm["o_mla_t"], sm["o_dil_t"]
    cqn, ckvn, kpe = _mla_prep1(lat, gq, gkv, cos_t, sin_t, "mla_prep1")
    q_raw = _mm(cqn, uq_t, "nt", F32, 1024, 1024, Q_LORA, "mla_uq")
    kv = _mm(ckvn, ukv_t, "nt", BF16, 1024, 1024, KV_LORA, "mla_ukv")
    q_att, k_att = _mla_prep2(q_raw, kv, kpe, cos_t, sin_t, "mla_prep2")
    o, lse = _flash2_fwd(q_att, k_att, kv, "mla_flash_fwd")
    o_a = _mm(o, o_mla_t, "nt", F32, 1024, 1024, MLA_HEADS * V_HEAD, "mla_out")

    d_os, d_ls = [], []
    for g, (_, dil) in enumerate(DIL_PATTERNS):
        og, lg = _dil_fwd_group(dqkv, bias[g], g, dil, f"dil_fwd_{g}")
        d_os.append(og)
        d_ls.append(lg)
    od, dil_lse = _dil_combine(d_os, d_ls, "dil_combine")
    o_b = _mm(od, o_dil_t, "nt", F32, 1024, 1024, DIL_OUT, "dil_out")

    mrg = _merge_fwd(gates, o_a, o_b, "merge_fwd")
    w_out = ffn_weight("w_out", mrg)
    x1, h2 = _mm_res_rms(mrg, w_out, x, g2, "mix_out")
    up_t = ffn_weight("up_t", h2)
    u0 = _mm(h2, up_t, "nt", F32, 1024, 512, D_MODEL, "ffn_up")
    a = _ffn_fwd(u0, cw, cb, "ffn_conv_fwd")
    w_down = ffn_weight("w_down", a)
    x2 = _mm(a, w_down, "nn", F32, 1024, 512, D_FF_PAD // 2, "ffn_down", res=x1)
    loss_part, dx2, dx2b, dg3 = _final_loss(x2, g3, tgt, "final_loss")

    da = _mm(dx2b, w_down, "nt", F32, 1024, 512, D_MODEL, "ffn_down_dx")
    gw_down = _mm(a, dx2b, "tn", BF16, 512, 1024, S, "ffn_down_dw")
    du0, dcw, dcb = _ffn_bwd(u0, da, cw, cb, "ffn_conv_bwd")
    du0 = du0.reshape(2 * S, D_FF_PAD)
    gw_up_t = _mm(du0, h2, "tn", BF16, 512, 1024, S, "ffn_up_dw", a_halves=2)
    sent = send_ffn_grads(gw_up_t, gw_down)
    dh2 = _mm(du0, up_t, "nn", F32, 1024, 1024, D_FF_PAD // 2, "ffn_up_dx", a_halves=2)
    dx1, dx1b, dg2 = _rms_bwd(dh2, x1, g2 + sent, dx2, "rms_ffn_bwd")

    dmrg = _mm(dx1b, w_out, "nt", F32, 1024, 1024, D_MODEL, "mix_out_dx")
    gw_out = _mm(mrg, dx1b, "tn", BF16, 512, 1024, S, "mix_out_dw")
    do_a, do_b, dga, dgb, dba, dbb = _merge_bwd(dmrg, gates, o_a, o_b, "merge_bwd")

    do = _mm(do_a, o_mla_t, "nn", BF16, 1024, 1024, D_MODEL, "mla_out_dx")
    gw_o_mla_t = _mm(do_a, o, "tn", BF16, 1024, 1024, 1024, "mla_out_dw")
    dod = _mm(do_b, o_dil_t, "nn", F32, 1024, DIL_OUT, D_MODEL, "dil_out_dx")
    gw_o_dil_t = _mm(do_b, od, "tn", BF16, 1024, DIL_OUT, 1024, "dil_out_dw")

    delta = _flash_delta(do, o, "mla_flash_delta")
    lse_row = lse[:, :, 0][:, None, :]
    delta_row = delta[:, :MLA_HEADS].T[:, None, :]
    dq_att, dk_att, dv = _flash2_bwd(q_att, k_att, kv, do, lse_row, delta_row, "mla_flash_bwd")
    dq_raw, dkv, dkpe = _mla_post(dq_att, dk_att, dv, cos_t, sin_t, "mla_post")
    dcqn = _mm(dq_raw, uq_t, "nn", F32, 1024, Q_LORA, MLA_HEADS * HEAD_PAD, "mla_uq_dx")
    gw_uq_t = _mm(dq_raw, cqn, "tn", BF16, 1024, Q_LORA, 1024, "mla_uq_dw")
    dckvn = _mm(dkv, ukv_t, "nn", F32, 1024, KV_LORA, MLA_HEADS * HEAD_PAD, "mla_ukv_dx")
    gw_ukv_t = _mm(dkv, ckvn, "tn", BF16, 1024, KV_LORA, 1024, "mla_ukv_dw")
    sent = send_small_grads({"uq_t": gw_uq_t, "ukv_t": gw_ukv_t, "o_mla_t": gw_o_mla_t, "o_dil_t": gw_o_dil_t, "w_out": gw_out})
    dlat, dgq, dgkv = _lat_bwd(dcqn, dckvn, dkpe, lat, gq + sent, gkv, "lat_bwd")

    dd = _dil_rowdot(dod, od, "dil_rowdot")
    ddqkv = lax.empty((3 * DIL_GROUPS, S, DIL_OUT), F32)
    for g, (_, dil) in enumerate(DIL_PATTERNS):
        ddqkv = _dil_bwd_group(dqkv, bias[g], dod, dd, dil_lse, ddqkv, g, dil, f"dil_bwd_{g}")
    gw_lat_t = _mm(dlat, h, "tn", BF16, LAT_PAD, 1024, S, "proj_lat_dw")
    gw_dqkv_t = _mm(ddqkv.reshape(3 * DIL_GROUPS * S, DIL_OUT), h, "tn", BF16, 512, 1024, S, "proj_dqkv_dw", a_halves=3 * DIL_GROUPS)
    gw_ga_t = _mm(dga, h, "tn", BF16, 512, 1024, S, "proj_ga_dw")
    gw_gb_t = _mm(dgb, h, "tn", BF16, 512, 1024, S, "proj_gb_dw")
    sent = send_w_in_grads(gw_lat_t, gw_dqkv_t, gw_ga_t, gw_gb_t)
    dh = _mm(dlat + sent.astype(BF16), lat_t, "nn", F32, 1024, 1024, LAT_PAD, "proj_lat_dx")
    dh = _stacked_mm(ddqkv, dqkv_t, dh, "proj_dqkv_dx")
    dh = _mm(dga, g_t, "nn", F32, 1024, 1024, D_MODEL, "proj_ga_dx", res=dh)
    grad_x, dg1 = _mm_rms_bwd(dgb, g_t, 1, dh, x, g1, dx1, "proj_gb_dx_rms_attn_bwd")

    small_grads = {
        "attn_norm_g": dg1,
        "b_gate": jnp.concatenate([dba, dbb], axis=1),
        "q_norm_g": dgq,
        "kv_norm_g": dgkv,
        "ffn_norm_g": dg2,
        "conv_b": _ffn_unpad(jnp.concatenate([dcb[0], dcb[1]], axis=1), 1),
        "final_norm_g": dg3,
        "conv_w": _ffn_unpad(jnp.concatenate([dcw[0, :3], dcw[1, :3]], axis=1), 1),
    }
    return loss_part, grad_x, small_grads


SMALL_ORDER = ("attn_norm_g", "b_gate", "q_norm_g", "kv_norm_g", "ffn_norm_g", "conv_b", "final_norm_g", "conv_w")
WEIGHT_ORDER = (
    "attn_norm_g", "w_in", "b_gate", "q_norm_g", "w_uq", "kv_norm_g", "w_ukv", "w_o_mla", "w_o_dil", "w_out",
    "ffn_norm_g", "w_up", "conv_w", "conv_b", "w_down", "final_norm_g",
)


def kernel(x, attn_norm_g, w_in, b_gate, q_norm_g, w_uq, kv_norm_g, w_ukv, w_o_mla, w_o_dil, w_out, ffn_norm_g, w_up, conv_w, conv_b, w_down, final_norm_g, loss_target, m_attn_norm_g, m_w_in, m_b_gate, m_q_norm_g, m_w_uq, m_kv_norm_g, m_w_ukv, m_w_o_mla, m_w_o_dil, m_w_out, m_ffn_norm_g, m_w_up, m_conv_w, m_conv_b, m_w_down, m_final_norm_g, v_attn_norm_g, v_w_in, v_b_gate, v_q_norm_g, v_w_uq, v_kv_norm_g, v_w_ukv, v_w_o_mla, v_w_o_dil, v_w_out, v_ffn_norm_g, v_w_up, v_conv_w, v_conv_b, v_w_down, v_final_norm_g):
    env = dict(locals())
    dev = 4 * lax.axis_index("x") + 2 * lax.axis_index("y") + lax.axis_index("c")
    core = lax.axis_index("c").astype(jnp.int32).reshape(1)

    def two_d(a):
        return a.reshape(-1, a.shape[-1])

    w = {n: two_d(env[n]) for n in WEIGHT_ORDER}
    m = {n: two_d(env["m_" + n]) for n in WEIGHT_ORDER}
    v = {n: two_d(env["v_" + n]) for n in WEIGHT_ORDER}

    chip = 2 * lax.axis_index("x") + lax.axis_index("y")

    def own_slot_in(lands, own, slot=dev):
        return [lax.dynamic_update_slice(l, o[None], (slot, 0, 0)) for l, o in zip(lands, own)]

    b_in = _exchange_blocks(w)[0]
    r, c = CONV_SHARD
    conv = jnp.pad(w["conv_w"].reshape(-1), (0, 8 * SMALL_COLS - r * c)).reshape(8, SMALL_COLS)
    first_level, token = _gather2_start([b_in, conv], "ag_w_in_start")
    tied = {n: w[n] + token[0, 0] for n in BIG_WEIGHTS}
    _, b_up, b_down, b_out, b_misc = _exchange_blocks(tied)
    prepared = b_up[:1, :1] + b_down[:1, :1] + b_out[:1, :1] + b_misc[:1, :1]
    own, lands = _gather2_wait(_gather2_forward(first_level, prepared, "ag_w_in_forward"), "ag_w_in_wait")
    g_in, conv = own_slot_in(lands, own)
    misc_gather, started = _exchange_start([b_misc], False, conv, "ag_small_start")
    ffn_gathers, started2 = {}, started
    for key, block in (("w_out", b_out), ("up_t", b_up), ("w_down", b_down)):
        ffn_gathers[key], started2 = _exchange_start([block], False, started2, f"ag_{key}_start")
    wt = _w_in_regroup(g_in, "w_in_regroup")
    conv = conv.reshape(N_DEV, 8 * SMALL_COLS)[:, : r * c].reshape(N_DEV, r, c)
    conv_w_full = conv.transpose(1, 0, 2).reshape(r, N_DEV * c)
    small = {n: w[n] for n in SMALL_ORDER if n != "conv_w"}

    def small_matrices(after):
        own, lands = _exchange_wait(misc_gather, False, after, "ag_small_wait")
        return _small_matrices(own_slot_in(lands, own)[0])

    def ffn_weight(key, after):
        own, lands = _exchange_wait(ffn_gathers[key], False, after, f"ag_{key}_wait")
        return own_slot_in(lands, own)[0].reshape(-1, D_MODEL)

    reduces = {}

    def send_ffn_grads(gw_up_t, gw_down):
        blocks = [gw_up_t.reshape(N_DEV, 2 * FF_GROUP_PAD, D_MODEL), gw_down.reshape(N_DEV, FF_GROUP_PAD, D_MODEL)]
        reduces["ffn"], token = _exchange_start(blocks, True, gw_down, "rs_ffn_start")
        return token[0, 0]

    def send_small_grads(g):
        reduces["small"], token = _exchange_start(_small_grad_blocks(g), True, g["w_out"], "rs_small_start")
        return token[0, 0]

    def send_w_in_grads(g_lat, g_dqkv, g_ga, g_gb):
        e_in = _w_in_grad_regroup(g_lat, g_dqkv, g_ga, g_gb, "w_in_grad_regroup")
        pair = _pair_add(e_in, _pair_exchange([e_in], "rs_w_in_pair_exchange")[0], core, "rs_w_in_pair_add")
        reduces["w_in"], token = _exchange_start([pair], True, pair, "rs_w_in_start", chips=True)
        return token[0, 0]

    loss_part, grad_x, small_grads = _local_step(
        x[0], loss_target[0], wt, conv_w_full, small, small_matrices, ffn_weight,
        send_ffn_grads, send_small_grads, send_w_in_grads, started2[0, 0],
    )
    loss = lax.psum(loss_part[0, 0], AXES)

    def finish(key, by_chip, name):
        sent, lands = _exchange_wait(reduces[key], True, grad_x, name + "_wait", chips=by_chip)
        slot = chip if by_chip else dev
        own = [lax.dynamic_index_in_dim(s, slot, 0, keepdims=False) for s in sent]
        return [_slot_sum(p, f"{name}_sum_{i}") for i, p in enumerate(own_slot_in(lands, own, slot))]

    (s_in,) = finish("w_in", True, "rs_w_in")
    s_out, s_misc = finish("small", False, "rs_small")
    s_up, s_down = finish("ffn", False, "rs_ffn")
    gshard = _grad_shards([s_in, s_out, s_misc, s_up, s_down])

    sflat = jnp.concatenate([small_grads[n].reshape(-1) for n in SMALL_ORDER])
    sflat = jnp.pad(sflat, (0, SMALL_ROWS * SMALL_COLS - sflat.shape[0])).reshape(SMALL_ROWS, SMALL_COLS)
    ssum = _slot_sum(_all_gather([sflat], "ag_small_grads")[0], "small_sum").reshape(-1)
    gsmall, off = {}, 0
    for n in SMALL_ORDER:
        shape = (3, 2 * D_FF) if n == "conv_w" else w[n].shape
        size = shape[0] * shape[1]
        gsmall[n] = ssum[off : off + size].reshape(shape)
        off += size
    gsmall["conv_w"] = lax.dynamic_slice_in_dim(gsmall["conv_w"], dev * CONV_SHARD[1], CONV_SHARD[1], axis=1)

    g_all = {**gshard, **gsmall}
    out_g, out_d, out_m, out_v = [], [], [], []
    for n in WEIGHT_ORDER:
        d, nm, nv = _adamw(w[n], g_all[n], m[n], v[n], "adamw_" + n)
        shape = env[n].shape
        out_g.append(g_all[n].reshape(shape))
        out_d.append(d.reshape(shape))
        out_m.append(nm.reshape(shape))
        out_v.append(nv.reshape(shape))
    return (loss, grad_x[None], *out_g, *out_d, *out_m, *out_v)
```

```python
import functools

import jax
import jax.numpy as jnp
import numpy as np
from jax import lax
from jax.experimental import pallas as pl
from jax.experimental.pallas import tpu as pltpu

F32 = jnp.float32
BF16 = jnp.bfloat16

N_DEV = 8
N_CHIP = 4
AXES = ("x", "y", "c")
MESH = pl.DeviceIdType.MESH

D_MODEL = 2048
MLA_HEADS = 8
QK_NOPE = 128
QK_ROPE = 64
V_HEAD = 128
Q_LORA = 512
KV_LORA = 256
ROPE_THETA = 10000.0
HEAD_PAD = 256
DIL_PATTERNS = ((128, 1), (512, 4), (2048, 16))
DIL_GROUPS = 3
DIL_HG = 4
DIL_HEADS = 12
DIL_HD = 128
DIL_BLK = 128
DIL_QKV = DIL_HEADS * DIL_HD
DIL_OUT = DIL_HG * DIL_HD
ALIBI_MAX_BIAS = 8.0
D_FF = 5504
D_FF_PAD = 5632
NORM_EPS = 1e-6
LAT = Q_LORA + KV_LORA + QK_ROPE
LAT_PAD = 896
D_IN = LAT + 3 * DIL_QKV + 2 * D_MODEL
NEG = -1e30

ADAM_LR = 0.001
ADAM_B1 = 0.9
ADAM_B2 = 0.999
ADAM_EPS = 1e-08
ADAM_WD = 0.01
ADAM_STEP = 10

SMALL_ROWS = 56
SMALL_COLS = 1024

IN_ROWS = 1192
IN_ROWS_PAD = 1200
FF_GROUP = D_FF // N_DEV
FF_GROUP_PAD = D_FF_PAD // N_DEV
CONV_SHARD = (3, 1376)

NT = (((1,), (1,)), ((), ()))
TN = (((0,), (0,)), ((), ()))


def _dot(a, b, dims=(((1,), (0,)), ((), ()))):
    return lax.dot_general(a, b, dims, preferred_element_type=F32)


def _mm(a, b, mode, out_dtype, tm, tn, tk, name, bias=None, act=None, res=None, b_koff=0, a_halves=1):
    H = a_halves
    if mode == "nn":
        (M, K), (K2, N) = (a.shape[0] // H, a.shape[1] * H), b.shape
        assert (b_koff + 1) * K <= K2, (name, a.shape, b.shape)
        koff, K2 = b_koff * (K // tk), K
        kper, mrows = a.shape[1] // tk, M // tm
        a_spec = pl.BlockSpec((tm, tk), lambda i, j, k: (i + (k // kper) * mrows, k % kper))
        b_spec = pl.BlockSpec((tk, tn), lambda i, j, k: (k + koff, j))
        dims = (((1,), (0,)), ((), ()))
    elif mode == "nt":
        (M, K), (N, K2) = a.shape, b.shape
        a_spec = pl.BlockSpec((tm, tk), lambda i, j, k: (i, k))
        b_spec = pl.BlockSpec((tn, tk), lambda i, j, k: (j, k))
        dims = NT
    else:
        (K, M), (K2, N) = (a.shape[0] // H, a.shape[1] * H), b.shape
        mper, krows = a.shape[1] // tm, K // tk
        a_spec = pl.BlockSpec((tk, tm), lambda i, j, k: (k + (i // mper) * krows, i % mper))
        b_spec = pl.BlockSpec((tk, tn), lambda i, j, k: (k, j))
        dims = TN
    assert K == K2 and M % tm == 0 and N % tn == 0 and K % tk == 0, (name, a.shape, b.shape)
    nk = K // tk
    has_bias, has_res = bias is not None, res is not None

    def body(*refs):
        refs = list(refs)
        a_ref, b_ref = refs[0], refs[1]
        pos = 2
        bias_ref = res_ref = None
        if has_bias:
            bias_ref = refs[pos]
            pos += 1
        if has_res:
            res_ref = refs[pos]
            pos += 1
        o_ref = refs[pos]
        p = _dot(a_ref[...].astype(BF16), b_ref[...].astype(BF16), dims)

        def finish(acc):
            if has_bias:
                acc = acc + bias_ref[...]
            if act == "sigmoid":
                acc = jax.nn.sigmoid(acc)
            if has_res:
                acc = res_ref[...] + acc
            o_ref[...] = acc.astype(o_ref.dtype)

        if nk == 1:
            finish(p)
        else:
            acc_ref = refs[pos + 1]
            k = pl.program_id(2)

            @pl.when(k == 0)
            def _():
                acc_ref[...] = p

            @pl.when(k != 0)
            def _():
                acc_ref[...] += p

            @pl.when(k == nk - 1)
            def _():
                finish(acc_ref[...])

    in_specs = [a_spec, b_spec]
    args = [a, b]
    if has_bias:
        in_specs.append(pl.BlockSpec((1, tn), lambda i, j, k: (0, j)))
        args.append(bias)
    if has_res:
        in_specs.append(pl.BlockSpec((tm, tn), lambda i, j, k: (i, j)))
        args.append(res)
    return pl.pallas_call(
        body,
        name=name,
        grid=(M // tm, N // tn, nk),
        in_specs=in_specs,
        out_specs=pl.BlockSpec((tm, tn), lambda i, j, k: (i, j)),
        out_shape=jax.ShapeDtypeStruct((M, N), out_dtype),
        scratch_shapes=[pltpu.VMEM((tm, tn), F32)] if nk > 1 else [],
        compiler_params=pltpu.CompilerParams(dimension_semantics=("parallel", "parallel", "arbitrary")),
    )(*args)


def _stacked_mm(pieces, w_t, res, name, tm=512, tn=1024):
    P, M, W = pieces.shape
    N = w_t.shape[1]

    def body(a_ref, b_ref, r_ref, o_ref):
        acc = r_ref[...]
        for p in range(P):
            acc = acc + _dot(a_ref[p].astype(BF16), b_ref[p * W : (p + 1) * W, :])
        o_ref[...] = acc

    tile = pl.BlockSpec((tm, tn), lambda i, j: (i, j))
    return pl.pallas_call(
        body,
        name=name,
        grid=(M // tm, N // tn),
        in_specs=[pl.BlockSpec((P, tm, W), lambda i, j: (0, i, 0)), pl.BlockSpec((P * W, tn), lambda i, j: (0, j)), tile],
        out_specs=tile,
        out_shape=jax.ShapeDtypeStruct((M, N), F32),
        compiler_params=pltpu.CompilerParams(dimension_semantics=("parallel", "parallel")),
    )(pieces, w_t, res)


def _rstd(x):
    return lax.rsqrt(jnp.mean(x * x, axis=-1, keepdims=True) + NORM_EPS)


def _rms_bwd_math(dy, x, g):
    r = _rstd(x)
    xh = x * r
    dg = jnp.sum(dy * xh, axis=0, keepdims=True)
    dxh = dy * g
    dx = r * (dxh - xh * jnp.mean(dxh * xh, axis=-1, keepdims=True))
    return dx, dg


def _rms_fwd(x, g, name, tr=256):
    S, D = x.shape

    def body(x_ref, g_ref, o_ref):
        xv = x_ref[...]
        o_ref[...] = ((xv * _rstd(xv)) * g_ref[...]).astype(o_ref.dtype)

    return pl.pallas_call(
        body,
        name=name,
        grid=(S // tr,),
        in_specs=[pl.BlockSpec((tr, D), lambda i: (i, 0)), pl.BlockSpec((1, D), lambda i: (0, 0))],
        out_specs=pl.BlockSpec((tr, D), lambda i: (i, 0)),
        out_shape=jax.ShapeDtypeStruct((S, D), BF16),
        compiler_params=pltpu.CompilerParams(dimension_semantics=("parallel",)),
    )(x, g)


def _rms_bwd(dy, x, g, res, name, tr=256):
    S, D = x.shape

    def body(dy_ref, x_ref, g_ref, res_ref, dx_ref, dxb_ref, dg_ref):
        dx, dg = _rms_bwd_math(dy_ref[...], x_ref[...], g_ref[...])
        dx = dx + res_ref[...]
        dx_ref[...] = dx
        dxb_ref[...] = dx.astype(BF16)

        @pl.when(pl.program_id(0) == 0)
        def _():
            dg_ref[...] = dg

        @pl.when(pl.program_id(0) != 0)
        def _():
            dg_ref[...] += dg

    row = pl.BlockSpec((tr, D), lambda i: (i, 0))
    vec = pl.BlockSpec((1, D), lambda i: (0, 0))
    return pl.pallas_call(
        body,
        name=name,
        grid=(S // tr,),
        in_specs=[row, row, vec, row],
        out_specs=[row, row, vec],
        out_shape=[jax.ShapeDtypeStruct((S, D), F32), jax.ShapeDtypeStruct((S, D), BF16), jax.ShapeDtypeStruct((1, D), F32)],
        compiler_params=pltpu.CompilerParams(dimension_semantics=("arbitrary",)),
    )(dy, x, g, res)


def _mm_res_rms(a, b, res, g, name, tm=256):
    M, K = a.shape
    D = b.shape[1]

    def body(a_ref, b_ref, res_ref, g_ref, y_ref, h_ref):
        y = res_ref[...] + _dot(a_ref[...], b_ref[...])
        y_ref[...] = y
        h_ref[...] = ((y * _rstd(y)) * g_ref[...]).astype(BF16)

    row = pl.BlockSpec((tm, D), lambda i: (i, 0))
    return pl.pallas_call(
        body,
        name=name,
        grid=(M // tm,),
        in_specs=[pl.BlockSpec((tm, K), lambda i: (i, 0)), pl.BlockSpec((K, D), lambda i: (0, 0)), row, pl.BlockSpec((1, D), lambda i: (0, 0))],
        out_specs=[row, row],
        out_shape=[jax.ShapeDtypeStruct((M, D), F32), jax.ShapeDtypeStruct((M, D), BF16)],
        compiler_params=pltpu.CompilerParams(dimension_semantics=("parallel",)),
    )(a, b, res, g)


def _mm_rms_bwd(a, b, b_koff, dy_part, x, g, res, name, tm=256):
    M, K = a.shape
    D = x.shape[1]

    def body(a_ref, b_ref, dyp_ref, x_ref, g_ref, res_ref, dx_ref, dg_ref):
        dy = dyp_ref[...] + _dot(a_ref[...], b_ref[...])
        dx, dg = _rms_bwd_math(dy, x_ref[...], g_ref[...])
        dx_ref[...] = dx + res_ref[...]

        @pl.when(pl.program_id(0) == 0)
        def _():
            dg_ref[...] = dg

        @pl.when(pl.program_id(0) != 0)
        def _():
            dg_ref[...] += dg

    row = pl.BlockSpec((tm, D), lambda i: (i, 0))
    vec = pl.BlockSpec((1, D), lambda i: (0, 0))
    return pl.pallas_call(
        body,
        name=name,
        grid=(M // tm,),
        in_specs=[pl.BlockSpec((tm, K), lambda i: (i, 0)), pl.BlockSpec((K, D), lambda i: (b_koff, 0)), row, row, vec, row],
        out_specs=[row, vec],
        out_shape=[jax.ShapeDtypeStruct((M, D), F32), jax.ShapeDtypeStruct((1, D), F32)],
        compiler_params=pltpu.CompilerParams(dimension_semantics=("arbitrary",)),
    )(a, b, dy_part, x, g, res)


def _final_loss(x2, g, tgt, name, tr=256):
    S, D = x2.shape

    def body(x_ref, g_ref, t_ref, loss_ref, dx_ref, dxb_ref, dg_ref):
        xv, gv = x_ref[...], g_ref[...]
        y = (xv * _rstd(xv)) * gv
        e = y - t_ref[...]
        part = 0.5 * jnp.sum(jnp.mean(e * e, axis=-1, keepdims=True), axis=0, keepdims=True)
        dx, dg = _rms_bwd_math(e * (1.0 / D), xv, gv)
        dx_ref[...] = dx
        dxb_ref[...] = dx.astype(BF16)
        part = jnp.broadcast_to(part, (1, 128))

        @pl.when(pl.program_id(0) == 0)
        def _():
            dg_ref[...] = dg
            loss_ref[...] = part

        @pl.when(pl.program_id(0) != 0)
        def _():
            dg_ref[...] += dg
            loss_ref[...] += part

    row = pl.BlockSpec((tr, D), lambda i: (i, 0))
    vec = pl.BlockSpec((1, D), lambda i: (0, 0))
    return pl.pallas_call(
        body,
        name=name,
        grid=(S // tr,),
        in_specs=[row, vec, row],
        out_specs=[pl.BlockSpec((1, 128), lambda i: (0, 0)), row, row, vec],
        out_shape=[
            jax.ShapeDtypeStruct((1, 128), F32),
            jax.ShapeDtypeStruct((S, D), F32),
            jax.ShapeDtypeStruct((S, D), BF16),
            jax.ShapeDtypeStruct((1, D), F32),
        ],
        compiler_params=pltpu.CompilerParams(dimension_semantics=("arbitrary",)),
    )(x2, g, tgt)


def _rope_tables(S):
    pos = jnp.arange(S, dtype=F32)
    inv_freq = ROPE_THETA ** (-jnp.arange(0, QK_ROPE, 2, dtype=F32) / QK_ROPE)
    ang = pos[:, None] * inv_freq[None, :]
    cos, sin = jnp.cos(ang), jnp.sin(ang)
    zero = jnp.zeros((S, 128 - QK_ROPE), F32)
    return jnp.concatenate([cos, cos, zero], axis=1), jnp.concatenate([-sin, sin, zero], axis=1)


def _rope_tile(x, cos_t, sin_t):
    lane = lax.broadcasted_iota(jnp.int32, x.shape, 1)
    partner = jnp.where(lane < QK_ROPE // 2, pltpu.roll(x, 128 - QK_ROPE // 2, 1), pltpu.roll(x, QK_ROPE // 2, 1))
    return x * cos_t + partner * sin_t


def _mla_prep1(lat, gq, gkv, cos_t, sin_t, name, tr=256):
    S = lat.shape[0]

    def body(lat_ref, gq_ref, gkv_ref, cos_ref, sin_ref, cq_ref, ckv_ref, kpe_ref):
        cq = lat_ref[:, :Q_LORA]
        ckv = lat_ref[:, Q_LORA : Q_LORA + KV_LORA]
        cq_ref[...] = ((cq * _rstd(cq)) * gq_ref[...]).astype(BF16)
        ckv_ref[...] = ((ckv * _rstd(ckv)) * gkv_ref[...]).astype(BF16)
        kpe_ref[...] = _rope_tile(lat_ref[:, Q_LORA + KV_LORA :], cos_ref[...], sin_ref[...]).astype(BF16)

    def row(n):
        return pl.BlockSpec((tr, n), lambda i: (i, 0))

    def vec(n):
        return pl.BlockSpec((1, n), lambda i: (0, 0))

    return pl.pallas_call(
        body,
        name=name,
        grid=(S // tr,),
        in_specs=[row(LAT_PAD), vec(Q_LORA), vec(KV_LORA), row(128), row(128)],
        out_specs=[row(Q_LORA), row(KV_LORA), row(128)],
        out_shape=[
            jax.ShapeDtypeStruct((S, Q_LORA), BF16),
            jax.ShapeDtypeStruct((S, KV_LORA), BF16),
            jax.ShapeDtypeStruct((S, 128), BF16),
        ],
        compiler_params=pltpu.CompilerParams(dimension_semantics=("parallel",)),
    )(lat, gq, gkv, cos_t, sin_t)


def _mla_prep2(q_raw, kv, kpe, cos_t, sin_t, name, tr=256):
    S = q_raw.shape[0]
    W = MLA_HEADS * HEAD_PAD

    def body(q_ref, kv_ref, kpe_ref, cos_ref, sin_ref, qa_ref, ka_ref):
        cos_v, sin_v, kpe_v = cos_ref[...], sin_ref[...], kpe_ref[...]
        for h in range(MLA_HEADS):
            lo = h * HEAD_PAD
            qa_ref[:, lo : lo + 128] = q_ref[:, lo : lo + 128].astype(BF16)
            qa_ref[:, lo + 128 : lo + 256] = _rope_tile(q_ref[:, lo + 128 : lo + 256], cos_v, sin_v).astype(BF16)
            ka_ref[:, lo : lo + 128] = kv_ref[:, lo : lo + 128]
            ka_ref[:, lo + 128 : lo + 256] = kpe_v

    def row(n):
        return pl.BlockSpec((tr, n), lambda i: (i, 0))

    return pl.pallas_call(
        body,
        name=name,
        grid=(S // tr,),
        in_specs=[row(W), row(W), row(128), row(128), row(128)],
        out_specs=[row(W), row(W)],
        out_shape=[jax.ShapeDtypeStruct((S, W), BF16), jax.ShapeDtypeStruct((S, W), BF16)],
        compiler_params=pltpu.CompilerParams(dimension_semantics=("parallel",)),
    )(q_raw, kv, kpe, cos_t, sin_t)


def _mla_post(dq_att, dk_att, dv, cos_t, sin_t, name, tr=256):
    S = dq_att.shape[0]
    W = MLA_HEADS * HEAD_PAD

    def body(dq_ref, dk_ref, dv_ref, cos_ref, sin_ref, dqr_ref, dkv_ref, dkpe_ref):
        cos_v, nsin_v = cos_ref[...], -sin_ref[...]
        kpe = jnp.zeros((tr, 128), F32)
        for h in range(MLA_HEADS):
            lo = h * HEAD_PAD
            dqr_ref[:, lo : lo + 128] = dq_ref[:, lo : lo + 128].astype(BF16)
            dqr_ref[:, lo + 128 : lo + 256] = _rope_tile(dq_ref[:, lo + 128 : lo + 256], cos_v, nsin_v).astype(BF16)
            dkv_ref[:, lo : lo + 128] = dk_ref[:, lo : lo + 128].astype(BF16)
            dkv_ref[:, lo + 128 : lo + 256] = dv_ref[:, h * 128 : (h + 1) * 128].astype(BF16)
            kpe = kpe + dk_ref[:, lo + 128 : lo + 256]
        dkpe_ref[...] = _rope_tile(kpe, cos_v, nsin_v)

    def row(n):
        return pl.BlockSpec((tr, n), lambda i: (i, 0))

    return pl.pallas_call(
        body,
        name=name,
        grid=(S // tr,),
        in_specs=[row(W), row(W), row(MLA_HEADS * V_HEAD), row(128), row(128)],
        out_specs=[row(W), row(W), row(128)],
        out_shape=[jax.ShapeDtypeStruct((S, W), BF16), jax.ShapeDtypeStruct((S, W), BF16), jax.ShapeDtypeStruct((S, 128), F32)],
        compiler_params=pltpu.CompilerParams(dimension_semantics=("parallel",)),
    )(dq_att, dk_att, dv, cos_t, sin_t)


def _lat_bwd(dcqn, dckvn, dkpe, lat, gq, gkv, name, tr=256):
    S = lat.shape[0]

    def body(dcq_ref, dckv_ref, dkpe_ref, lat_ref, gq_ref, gkv_ref, dlat_ref, dgq_ref, dgkv_ref):
        dq, dgq = _rms_bwd_math(dcq_ref[...], lat_ref[:, :Q_LORA], gq_ref[...])
        dkv, dgkv = _rms_bwd_math(dckv_ref[...], lat_ref[:, Q_LORA : Q_LORA + KV_LORA], gkv_ref[...])
        dlat_ref[:, :Q_LORA] = dq.astype(BF16)
        dlat_ref[:, Q_LORA : Q_LORA + KV_LORA] = dkv.astype(BF16)
        dlat_ref[:, Q_LORA + KV_LORA :] = dkpe_ref[...].astype(BF16)

        @pl.when(pl.program_id(0) == 0)
        def _():
            dgq_ref[...] = dgq
            dgkv_ref[...] = dgkv

        @pl.when(pl.program_id(0) != 0)
        def _():
            dgq_ref[...] += dgq
            dgkv_ref[...] += dgkv

    def row(n):
        return pl.BlockSpec((tr, n), lambda i: (i, 0))

    def vec(n):
        return pl.BlockSpec((1, n), lambda i: (0, 0))

    return pl.pallas_call(
        body,
        name=name,
        grid=(S // tr,),
        in_specs=[row(Q_LORA), row(KV_LORA), row(128), row(LAT_PAD), vec(Q_LORA), vec(KV_LORA)],
        out_specs=[row(LAT_PAD), vec(Q_LORA), vec(KV_LORA)],
        out_shape=[
            jax.ShapeDtypeStruct((S, LAT_PAD), BF16),
            jax.ShapeDtypeStruct((1, Q_LORA), F32),
            jax.ShapeDtypeStruct((1, KV_LORA), F32),
        ],
        compiler_params=pltpu.CompilerParams(dimension_semantics=("arbitrary",)),
    )(dcqn, dckvn, dkpe, lat, gq, gkv)


MLA_SCALE = (QK_NOPE + QK_ROPE) ** -0.5
LOG2E = 1.4426950408889634
MLA_C2 = MLA_SCALE * LOG2E
FLASH_T = 1024


def _causal_pairs(n, by_key):
    pairs = [(i, j) for j in range(n) for i in range(j, n)] if by_key else [(i, j) for i in range(n) for j in range(i + 1)]
    return jnp.asarray([p[0] for p in pairs], jnp.int32), jnp.asarray([p[1] for p in pairs], jnp.int32)


def _lanes(x, n):
    return jnp.tile(x, (1, n // 128))


def _flash_grid(npairs, in_specs, out_specs, scratch):
    return pltpu.PrefetchScalarGridSpec(
        num_scalar_prefetch=2, grid=(MLA_HEADS, npairs), in_specs=in_specs, out_specs=out_specs, scratch_shapes=scratch
    )


def _flash2_fwd(q_att, k_att, kv, name, t=FLASH_T):
    S = q_att.shape[0]
    qi_tab, kj_tab = _causal_pairs(S // t, by_key=False)

    def body(qi_ref, kj_ref, q_ref, k_ref, v_ref, o_ref, lse_ref, m_sc, l_sc, acc_sc):
        step = pl.program_id(1)
        qi, kj = qi_ref[step], kj_ref[step]

        @pl.when(kj == 0)
        def _():
            m_sc[...] = jnp.full((t, 128), NEG, F32)
            l_sc[...] = jnp.zeros((t, 128), F32)
            acc_sc[...] = jnp.zeros((t, V_HEAD), F32)

        def update(s):
            m_prev = m_sc[...]
            m_new = jnp.maximum(m_prev, jnp.max(s, axis=1, keepdims=True))
            p = jnp.exp2((s - _lanes(m_new, t)) * MLA_C2)
            alpha = jnp.exp2((m_prev - m_new) * MLA_C2)
            l_sc[...] = alpha * l_sc[...] + jnp.sum(p, axis=1, keepdims=True)
            acc_sc[...] = alpha * acc_sc[...] + _dot(p.astype(BF16), v_ref[...])
            m_sc[...] = m_new

        @pl.when(kj < qi)
        def _():
            update(_dot(q_ref[...], k_ref[...], NT))

        @pl.when(kj == qi)
        def _():
            s = _dot(q_ref[...], k_ref[...], NT)
            rows = lax.broadcasted_iota(jnp.int32, s.shape, 0)
            cols = lax.broadcasted_iota(jnp.int32, s.shape, 1)
            update(jnp.where(cols <= rows, s, NEG))
            l = l_sc[...]
            o_ref[...] = acc_sc[...] / l
            lse_ref[0] = m_sc[...] * MLA_SCALE + jnp.log(l)

    return pl.pallas_call(
        body,
        name=name,
        grid_spec=_flash_grid(
            qi_tab.shape[0],
            [
                pl.BlockSpec((t, HEAD_PAD), lambda h, p, qi, kj: (qi[p], h)),
                pl.BlockSpec((t, HEAD_PAD), lambda h, p, qi, kj: (kj[p], h)),
                pl.BlockSpec((t, V_HEAD), lambda h, p, qi, kj: (kj[p], 2 * h + 1)),
            ],
            [
                pl.BlockSpec((t, V_HEAD), lambda h, p, qi, kj: (qi[p], h)),
                pl.BlockSpec((1, t, 128), lambda h, p, qi, kj: (h, qi[p], 0)),
            ],
            [pltpu.VMEM((t, 128), F32), pltpu.VMEM((t, 128), F32), pltpu.VMEM((t, V_HEAD), F32)],
        ),
        out_shape=[jax.ShapeDtypeStruct((S, MLA_HEADS * V_HEAD), F32), jax.ShapeDtypeStruct((MLA_HEADS, S, 128), F32)],
        compiler_params=pltpu.CompilerParams(dimension_semantics=("parallel", "arbitrary")),
    )(qi_tab, kj_tab, q_att, k_att, kv)


def _flash_delta(do, o, name, tr=512):
    S = o.shape[0]

    def body(do_ref, o_ref, d_ref):
        lane = lax.broadcasted_iota(jnp.int32, (tr, 128), 1)
        acc = jnp.zeros((tr, 128), F32)
        for h in range(MLA_HEADS):
            sl = slice(h * V_HEAD, (h + 1) * V_HEAD)
            acc = jnp.where(lane == h, jnp.sum(do_ref[:, sl].astype(F32) * o_ref[:, sl], axis=1, keepdims=True), acc)
        d_ref[...] = acc

    row = pl.BlockSpec((tr, MLA_HEADS * V_HEAD), lambda i: (i, 0))
    return pl.pallas_call(
        body,
        name=name,
        grid=(S // tr,),
        in_specs=[row, row],
        out_specs=pl.BlockSpec((tr, 128), lambda i: (i, 0)),
        out_shape=jax.ShapeDtypeStruct((S, 128), F32),
        compiler_params=pltpu.CompilerParams(dimension_semantics=("parallel",)),
    )(do, o)


def _flash2_bwd(q_att, k_att, kv, do, lse_row, delta_row, name, t=FLASH_T):
    S = q_att.shape[0]
    n = S // t
    qi_tab, kj_tab = _causal_pairs(n, by_key=True)
    last = qi_tab.shape[0] - 1

    def body(qi_ref, kj_ref, q_ref, k_ref, v_ref, do_ref, lse_ref, dl_ref, dq_ref, dk_ref, dv_ref, dk_sc, dv_sc):
        step = pl.program_id(1)
        qi, kj = qi_ref[step], kj_ref[step]

        @pl.when(step == 0)
        def _():
            dq_ref[...] = jnp.zeros((S, HEAD_PAD), F32)

        def update(st):
            q, do_v = q_ref[...], do_ref[...]
            pt = jnp.exp2(st * MLA_C2 - lse_ref[0] * LOG2E)
            dv_sc[...] += _dot(pt.astype(BF16), do_v)
            dpt = _dot(v_ref[...], do_v, NT)
            dst = (pt * (dpt - dl_ref[0])).astype(BF16)
            dk_sc[...] += _dot(dst, q)
            rows = pl.ds(pl.multiple_of(qi * t, t), t)
            dq_ref[rows, :] += _dot(dst, k_ref[...], TN)

        @pl.when(qi == kj)
        def _():
            dk_sc[...] = jnp.zeros((t, HEAD_PAD), F32)
            dv_sc[...] = jnp.zeros((t, V_HEAD), F32)
            st = _dot(k_ref[...], q_ref[...], NT)
            keys = lax.broadcasted_iota(jnp.int32, st.shape, 0)
            qs = lax.broadcasted_iota(jnp.int32, st.shape, 1)
            update(jnp.where(keys <= qs, st, NEG))

        @pl.when(qi > kj)
        def _():
            update(_dot(k_ref[...], q_ref[...], NT))

        @pl.when(qi == n - 1)
        def _():
            dk_ref[...] = dk_sc[...] * MLA_SCALE
            dv_ref[...] = dv_sc[...]

        @pl.when(step == last)
        def _():
            dq_ref[...] = dq_ref[...] * MLA_SCALE

    qrow = lambda h, p, qi, kj: (qi[p], h)
    krow = lambda h, p, qi, kj: (kj[p], h)
    stat = pl.BlockSpec((1, 1, t), lambda h, p, qi, kj: (h, 0, qi[p]))
    return pl.pallas_call(
        body,
        name=name,
        grid_spec=_flash_grid(
            qi_tab.shape[0],
            [
                pl.BlockSpec((t, HEAD_PAD), qrow),
                pl.BlockSpec((t, HEAD_PAD), krow),
                pl.BlockSpec((t, V_HEAD), lambda h, p, qi, kj: (kj[p], 2 * h + 1)),
                pl.BlockSpec((t, V_HEAD), qrow),
                stat,
                stat,
            ],
            [
                pl.BlockSpec((S, HEAD_PAD), lambda h, p, qi, kj: (0, h)),
                pl.BlockSpec((t, HEAD_PAD), krow),
                pl.BlockSpec((t, V_HEAD), krow),
            ],
            [pltpu.VMEM((t, HEAD_PAD), F32), pltpu.VMEM((t, V_HEAD), F32)],
        ),
        out_shape=[
            jax.ShapeDtypeStruct((S, MLA_HEADS * HEAD_PAD), F32),
            jax.ShapeDtypeStruct((S, MLA_HEADS * HEAD_PAD), F32),
            jax.ShapeDtypeStruct((S, MLA_HEADS * V_HEAD), F32),
        ],
        compiler_params=pltpu.CompilerParams(dimension_semantics=("parallel", "arbitrary")),
    )(qi_tab, kj_tab, q_att, k_att, kv, do, lse_row, delta_row)


DIL_SCALE = DIL_HD**-0.5


def _dil_bias():
    slopes = 2.0 ** (-ALIBI_MAX_BIAS * np.arange(1, DIL_HEADS + 1, dtype=np.float64) / DIL_HEADS)
    slopes = slopes.astype(np.float32).reshape(DIL_GROUPS, DIL_HG)
    p = np.arange(DIL_BLK)[:, None]
    kidx = np.arange(2 * DIL_BLK)[None, :]
    j = p + DIL_BLK - kidx
    out = np.zeros((DIL_GROUPS, DIL_HG, DIL_BLK, 2 * DIL_BLK), np.float32)
    for g, (window, dil) in enumerate(DIL_PATTERNS):
        valid = (j >= 0) & (j <= window // dil)
        for h in range(DIL_HG):
            alibi = -slopes[g, h] * (dil * j).astype(np.float32)
            out[g, h] = np.where(valid, alibi, np.float32(NEG))
    return jnp.asarray(out)


DIL_UNROLL = 4


def _unrolled_loop(lo, hi, fn, unroll=DIL_UNROLL):
    groups = (hi - lo) // unroll
    done = lo
    if groups > 1:

        def step(i, carry):
            for u in range(unroll):
                fn(lo + i * unroll + u)
            return carry

        lax.fori_loop(0, groups, step, 0)
        done = lo + groups * unroll
    for n in range(done, hi):
        fn(n)


def _dil_rows(r, n, count, dil):
    if dil == 1:
        if isinstance(n, int):
            return slice(n * DIL_BLK, (n + count) * DIL_BLK)
        return pl.ds(pl.multiple_of(n * DIL_BLK, DIL_BLK), count * DIL_BLK)
    return pl.ds(n * DIL_BLK * dil + r, count * DIL_BLK, stride=dil)


def _dil_each_block(S, dil, block):
    nb = S // dil // DIL_BLK
    if dil == 1:
        block(0, 0, True)
        _unrolled_loop(1, nb, lambda n: block(0, n, False))
    else:
        for r in range(dil):
            for n in range(nb):
                block(r, n, n == 0)


def _dil_col(g, part, h):
    return (g * 3 + part) * DIL_HG + h


def _dil_fwd_group(dqkv, bias_g, g, dil, name):
    S = dqkv.shape[0]

    def body(bias_ref, q_ref, k_ref, v_ref, o_ref, lse_ref):
        def block(r, n, first):
            cur = _dil_rows(r, n, 1, dil)
            both = cur if first else _dil_rows(r, n - 1, 2, dil)
            b = bias_ref[0][:, DIL_BLK:] if first else bias_ref[0]
            q, kk, vv = q_ref[cur, :].astype(BF16), k_ref[both, :].astype(BF16), v_ref[both, :].astype(BF16)
            s = _dot(q, kk, NT) * DIL_SCALE + b
            m = jnp.max(s, axis=1, keepdims=True)
            e = jnp.exp(s - m)
            l = jnp.sum(e, axis=1, keepdims=True)
            p = e * (1.0 / l)
            o_ref[cur, :] = _dot(p.astype(BF16), vv)
            lse_ref[cur, :] = jnp.broadcast_to(m + jnp.log(l), (DIL_BLK, 128))

        _dil_each_block(S, dil, block)

    def col(part):
        return pl.BlockSpec((S, DIL_HD), lambda h: (0, _dil_col(g, part, h)))

    out = pl.BlockSpec((S, DIL_HD), lambda h: (0, h))
    return pl.pallas_call(
        body,
        name=name,
        grid=(DIL_HG,),
        in_specs=[pl.BlockSpec((1, DIL_BLK, 2 * DIL_BLK), lambda h: (h, 0, 0)), col(0), col(1), col(2)],
        out_specs=[out, out],
        out_shape=[jax.ShapeDtypeStruct((S, DIL_OUT), F32), jax.ShapeDtypeStruct((S, DIL_OUT), F32)],
        compiler_params=pltpu.CompilerParams(dimension_semantics=("parallel",)),
    )(bias_g, dqkv, dqkv, dqkv)


def _dil_combine(os_, ls_, name, tr=512):
    S = os_[0].shape[0]

    def body(o0, o1, o2, l0, l1, l2, out_ref, lse_ref):
        a, b, c = l0[...], l1[...], l2[...]
        m = jnp.maximum(jnp.maximum(a, b), c)
        ea, eb, ec = jnp.exp(a - m), jnp.exp(b - m), jnp.exp(c - m)
        den = ea + eb + ec
        inv = 1.0 / den
        out_ref[...] = (ea * inv) * o0[...] + (eb * inv) * o1[...] + (ec * inv) * o2[...]
        lse_ref[...] = m + jnp.log(den)

    row = pl.BlockSpec((tr, DIL_OUT), lambda i: (i, 0))
    return pl.pallas_call(
        body,
        name=name,
        grid=(S // tr,),
        in_specs=[row] * 6,
        out_specs=[row, row],
        out_shape=[jax.ShapeDtypeStruct((S, DIL_OUT), F32)] * 2,
        compiler_params=pltpu.CompilerParams(dimension_semantics=("parallel",)),
    )(*os_, *ls_)


def _dil_rowdot(dod, od, name, tr=512):
    S = dod.shape[0]

    def body(d_ref, o_ref, dd_ref):
        for h in range(DIL_HG):
            sl = slice(h * 128, (h + 1) * 128)
            sm = jnp.sum(d_ref[:, sl] * o_ref[:, sl], axis=1, keepdims=True)
            dd_ref[:, sl] = jnp.broadcast_to(sm, (tr, 128))

    row = pl.BlockSpec((tr, DIL_OUT), lambda i: (i, 0))
    return pl.pallas_call(
        body,
        name=name,
        grid=(S // tr,),
        in_specs=[row, row],
        out_specs=row,
        out_shape=jax.ShapeDtypeStruct((S, DIL_OUT), F32),
        compiler_params=pltpu.CompilerParams(dimension_semantics=("parallel",)),
    )(dod, od)


def _dil_bwd_group(dqkv, bias_g, dod, dd, lse, grads, g, dil, name):
    S = dqkv.shape[0]

    def body(bias_ref, q_ref, k_ref, v_ref, do_ref, dd_ref, lse_ref, _, out_ref):
        out_ref[1] = jnp.zeros((S, DIL_HD), F32)
        out_ref[2] = jnp.zeros((S, DIL_HD), F32)

        def block(r, n, first):
            cur = _dil_rows(r, n, 1, dil)
            both = cur if first else _dil_rows(r, n - 1, 2, dil)
            b = bias_ref[0][:, DIL_BLK:] if first else bias_ref[0]
            q, kk, vv = q_ref[cur, :].astype(BF16), k_ref[both, :].astype(BF16), v_ref[both, :].astype(BF16)
            do = do_ref[cur, :].astype(BF16)
            s = _dot(q, kk, NT) * DIL_SCALE + b
            p = jnp.exp(s - lse_ref[cur, 0:1])
            dp = _dot(do, vv, NT)
            ds = ((p * (dp - dd_ref[cur, 0:1])) * DIL_SCALE).astype(BF16)
            out_ref[0, cur, :] = _dot(ds, kk)
            out_ref[1, both, :] += _dot(ds, q, TN)
            out_ref[2, both, :] += _dot(p.astype(BF16), do, TN)

        _dil_each_block(S, dil, block)

    def col(part):
        return pl.BlockSpec((S, DIL_HD), lambda h: (0, _dil_col(g, part, h)))

    nat = pl.BlockSpec((S, DIL_HD), lambda h: (0, h))
    return pl.pallas_call(
        body,
        name=name,
        grid=(DIL_HG,),
        in_specs=[pl.BlockSpec((1, DIL_BLK, 2 * DIL_BLK), lambda h: (h, 0, 0)), col(0), col(1), col(2), nat, nat, nat, ANY],
        out_specs=pl.BlockSpec((3, S, DIL_HD), lambda h: (g, 0, h)),
        out_shape=jax.ShapeDtypeStruct(grads.shape, F32),
        input_output_aliases={7: 0},
        compiler_params=pltpu.CompilerParams(dimension_semantics=("parallel",)),
    )(bias_g, dqkv, dqkv, dqkv, dod, dd, lse, grads)


def _merge_fwd(gates, o_a, o_b, name, tr=256):
    S = o_a.shape[0]

    def body(ga_ref, gb_ref, oa_ref, ob_ref, m_ref):
        m_ref[...] = (ga_ref[...] * oa_ref[...] + gb_ref[...] * ob_ref[...]).astype(BF16)

    row = pl.BlockSpec((tr, D_MODEL), lambda i: (i, 0))
    return pl.pallas_call(
        body,
        name=name,
        grid=(S // tr,),
        in_specs=[row, pl.BlockSpec((tr, D_MODEL), lambda i: (i, 1)), row, row],
        out_specs=row,
        out_shape=jax.ShapeDtypeStruct((S, D_MODEL), BF16),
        compiler_params=pltpu.CompilerParams(dimension_semantics=("parallel",)),
    )(gates, gates, o_a, o_b)


def _merge_bwd(dmrg, gates, o_a, o_b, name, tr=256):
    S = o_a.shape[0]

    def body(dm_ref, ga_ref, gb_ref, oa_ref, ob_ref, doa_ref, dob_ref, dga_ref, dgb_ref, dba_ref, dbb_ref):
        dm, ga, gb = dm_ref[...], ga_ref[...], gb_ref[...]
        doa_ref[...] = (dm * ga).astype(BF16)
        dob_ref[...] = (dm * gb).astype(BF16)
        dga = (dm * oa_ref[...]) * (ga * (1.0 - ga))
        dgb = (dm * ob_ref[...]) * (gb * (1.0 - gb))
        dga_ref[...] = dga.astype(BF16)
        dgb_ref[...] = dgb.astype(BF16)
        sa = jnp.sum(dga, axis=0, keepdims=True)
        sb = jnp.sum(dgb, axis=0, keepdims=True)

        @pl.when(pl.program_id(0) == 0)
        def _():
            dba_ref[...] = sa
            dbb_ref[...] = sb

        @pl.when(pl.program_id(0) != 0)
        def _():
            dba_ref[...] += sa
            dbb_ref[...] += sb

    row = pl.BlockSpec((tr, D_MODEL), lambda i: (i, 0))
    row1 = pl.BlockSpec((tr, D_MODEL), lambda i: (i, 1))
    vec = pl.BlockSpec((1, D_MODEL), lambda i: (0, 0))
    outs = pl.pallas_call(
        body,
        name=name,
        grid=(S // tr,),
        in_specs=[row, row, row1, row, row],
        out_specs=[row, row, row, row, vec, vec],
        out_shape=[jax.ShapeDtypeStruct((S, D_MODEL), BF16)] * 4 + [jax.ShapeDtypeStruct((1, D_MODEL), F32)] * 2,
        compiler_params=pltpu.CompilerParams(dimension_semantics=("arbitrary",)),
    )(dmrg, gates, gates, o_a, o_b)
    return outs


CONV_TR = 512
CONV_TC = 512
N_FFC = D_FF_PAD // CONV_TC


def _conv_taps(x, before, w_ref, b_ref):
    x0 = jnp.concatenate([before, x], axis=0)
    x1 = pltpu.roll(x0, 1, 0)
    x2 = pltpu.roll(x0, 2, 0)
    u = ((b_ref[...] + w_ref[0:1, :] * x2) + w_ref[1:2, :] * x1) + w_ref[2:3, :] * x0
    return u, x0, x1, x2


def _prev_halo(tr):
    return lambda i, j: (jnp.maximum(i * (tr // 8) - 1, 0), j)


def _ffn_fwd(u0, cw, cb, name):
    S = u0.shape[0]
    tr, tc = CONV_TR, CONV_TC

    def body(up_ref, gt_ref, hup_ref, hgt_ref, wu_ref, wg_ref, bu_ref, bg_ref, a_ref):
        live = (pl.program_id(0) > 0).astype(F32)
        up = _conv_taps(up_ref[...], hup_ref[...] * live, wu_ref, bu_ref)[0][8:]
        gt = _conv_taps(gt_ref[...], hgt_ref[...] * live, wg_ref, bg_ref)[0][8:]
        a_ref[...] = ((gt * jax.nn.sigmoid(gt)) * up).astype(BF16)

    return pl.pallas_call(
        body,
        name=name,
        grid=(S // tr, N_FFC),
        in_specs=[
            pl.BlockSpec((tr, tc), lambda i, j: (i, j)),
            pl.BlockSpec((tr, tc), lambda i, j: (i, j + N_FFC)),
            pl.BlockSpec((8, tc), _prev_halo(tr)),
            pl.BlockSpec((8, tc), lambda i, j: (jnp.maximum(i * (tr // 8) - 1, 0), j + N_FFC)),
            pl.BlockSpec((8, tc), lambda i, j: (0, j)),
            pl.BlockSpec((8, tc), lambda i, j: (0, j + N_FFC)),
            pl.BlockSpec((1, tc), lambda i, j: (0, j)),
            pl.BlockSpec((1, tc), lambda i, j: (0, j + N_FFC)),
        ],
        out_specs=pl.BlockSpec((tr, tc), lambda i, j: (i, j)),
        out_shape=jax.ShapeDtypeStruct((S, D_FF_PAD), BF16),
        compiler_params=pltpu.CompilerParams(dimension_semantics=("parallel", "parallel")),
    )(u0, u0, u0, u0, cw, cw, cb, cb)


def _ffn_bwd(u0, da, cw, cb, name):
    S = u0.shape[0]
    tr, tc = CONV_TR, CONV_TC
    nrow, te = S // tr, tr + 8

    def body(up_ref, gt_ref, hup_ref, hgt_ref, nup_ref, ngt_ref, da_ref, nda_ref, wu_ref, wg_ref, bu_ref, bg_ref, du0_ref, dcw_ref, dcb_ref):
        i = pl.program_id(1)
        prev_live = (i > 0).astype(F32)
        next_live = (i < nrow - 1).astype(F32)

        def conv(x_ref, nx_ref, h_ref, w_ref, b_ref):
            x = jnp.concatenate([x_ref[...], nx_ref[...] * next_live], axis=0)
            return [t[8:] for t in _conv_taps(x, h_ref[...] * prev_live, w_ref, b_ref)]

        up, xu0, xu1, xu2 = conv(up_ref, nup_ref, hup_ref, wu_ref, bu_ref)
        gt, xg0, xg1, xg2 = conv(gt_ref, ngt_ref, hgt_ref, wg_ref, bg_ref)
        da_v = jnp.concatenate([da_ref[...], nda_ref[...] * next_live], axis=0)
        sg = jax.nn.sigmoid(gt)
        d_up = da_v * (gt * sg)
        d_gt = (da_v * up) * (sg * (1.0 + gt * (1.0 - sg)))
        tap = lax.broadcasted_iota(jnp.int32, (8, tc), 0)

        def finish(half, du, x0, x1, x2, w_ref):
            n1 = pltpu.roll(du, te - 1, 0)
            n2 = pltpu.roll(du, te - 2, 0)
            du0 = (w_ref[2:3, :] * du + w_ref[1:2, :] * n1) + w_ref[0:1, :] * n2
            du0_ref[half] = du0[:tr].astype(BF16)
            d = du[:tr]
            dcw = jnp.where(
                tap == 0,
                jnp.sum(d * x2[:tr], axis=0, keepdims=True),
                jnp.where(tap == 1, jnp.sum(d * x1[:tr], axis=0, keepdims=True), jnp.where(tap == 2, jnp.sum(d * x0[:tr], axis=0, keepdims=True), 0.0)),
            )
            dcb = jnp.sum(d, axis=0, keepdims=True)

            @pl.when(i == 0)
            def _():
                dcw_ref[half] = dcw
                dcb_ref[half] = dcb

            @pl.when(i != 0)
            def _():
                dcw_ref[half] += dcw
                dcb_ref[half] += dcb

        finish(0, d_up, xu0, xu1, xu2, wu_ref)
        finish(1, d_gt, xg0, xg1, xg2, wg_ref)

    def prev8(off):
        return pl.BlockSpec((8, tc), lambda j, i: (jnp.maximum(i * (tr // 8) - 1, 0), j + off))

    def next8(off):
        return pl.BlockSpec((8, tc), lambda j, i: (jnp.minimum((i + 1) * (tr // 8), S // 8 - 1), j + off))

    return pl.pallas_call(
        body,
        name=name,
        grid=(N_FFC, nrow),
        in_specs=[
            pl.BlockSpec((tr, tc), lambda j, i: (i, j)),
            pl.BlockSpec((tr, tc), lambda j, i: (i, j + N_FFC)),
            prev8(0),
            prev8(N_FFC),
            next8(0),
            next8(N_FFC),
            pl.BlockSpec((tr, tc), lambda j, i: (i, j)),
            next8(0),
            pl.BlockSpec((8, tc), lambda j, i: (0, j)),
            pl.BlockSpec((8, tc), lambda j, i: (0, j + N_FFC)),
            pl.BlockSpec((1, tc), lambda j, i: (0, j)),
            pl.BlockSpec((1, tc), lambda j, i: (0, j + N_FFC)),
        ],
        out_specs=[
            pl.BlockSpec((2, tr, tc), lambda j, i: (0, i, j)),
            pl.BlockSpec((2, 8, tc), lambda j, i: (0, 0, j)),
            pl.BlockSpec((2, 1, tc), lambda j, i: (0, 0, j)),
        ],
        out_shape=[
            jax.ShapeDtypeStruct((2, S, D_FF_PAD), BF16),
            jax.ShapeDtypeStruct((2, 8, D_FF_PAD), F32),
            jax.ShapeDtypeStruct((2, 1, D_FF_PAD), F32),
        ],
        compiler_params=pltpu.CompilerParams(dimension_semantics=("parallel", "arbitrary")),
    )(u0, u0, u0, u0, u0, u0, da, da, cw, cw, cb, cb)


ADAMW_BLOCK_BYTES = 3 << 20


def _adamw(w, g, m, v, name):
    R, C = w.shape
    fits = [t for t in range(8, R + 1, 8) if R % t == 0 and t * C * 4 <= ADAMW_BLOCK_BYTES]
    tr = max(fits) if fits else R

    def body(w_ref, g_ref, m_ref, v_ref, d_ref, nm_ref, nv_ref):
        gv = g_ref[...]
        nm = ADAM_B1 * m_ref[...] + (1.0 - ADAM_B1) * gv
        nv = ADAM_B2 * v_ref[...] + (1.0 - ADAM_B2) * (gv * gv)
        m_hat = nm / (1.0 - ADAM_B1**ADAM_STEP)
        v_hat = nv / (1.0 - ADAM_B2**ADAM_STEP)
        d_ref[...] = -ADAM_LR * (m_hat / (jnp.sqrt(v_hat) + ADAM_EPS) + ADAM_WD * w_ref[...])
        nm_ref[...] = nm
        nv_ref[...] = nv

    blk = pl.BlockSpec((tr, C), lambda i: (i, 0))
    return pl.pallas_call(
        body,
        name=name,
        grid=(R // tr,),
        in_specs=[blk] * 4,
        out_specs=[blk] * 3,
        out_shape=[jax.ShapeDtypeStruct((R, C), F32)] * 3,
        compiler_params=pltpu.CompilerParams(dimension_semantics=("parallel",)),
    )(w, g, m, v)


ANY = pl.BlockSpec(memory_space=pl.ANY)


def _pair_exchange(gs, name):
    n = len(gs)

    def body(*refs):
        g_refs, out_refs = refs[:n], refs[n : 2 * n]
        send_sems, recv_sems = refs[2 * n :]
        x, y, c = lax.axis_index("x"), lax.axis_index("y"), lax.axis_index("c")
        copies = [
            pltpu.make_async_remote_copy(
                src_ref=g_refs[a].at[2 * k + (1 - c)],
                dst_ref=out_refs[a].at[k],
                send_sem=send_sems.at[N_CHIP * a + k],
                recv_sem=recv_sems.at[N_CHIP * a + k],
                device_id=(x, y, 1 - c),
                device_id_type=MESH,
            )
            for a in range(n)
            for k in range(N_CHIP)
        ]
        for cp in copies:
            cp.start()
        for cp in copies:
            cp.wait()

    return pl.pallas_call(
        body,
        name=name,
        out_shape=[jax.ShapeDtypeStruct((N_CHIP,) + g.shape[1:], g.dtype) for g in gs],
        in_specs=[ANY] * n,
        out_specs=[ANY] * n,
        scratch_shapes=[pltpu.SemaphoreType.DMA((N_CHIP * n,)), pltpu.SemaphoreType.DMA((N_CHIP * n,))],
    )(*gs)


def _row_tile(rows):
    return max(t for t in range(16, 353, 16) if rows % t == 0)


def _pair_add(g, recv, core, name):
    _, R, C = g.shape
    tr = _row_tile(R)

    def body(core_ref, g_ref, r_ref, o_ref):
        o_ref[...] = (g_ref[...].astype(F32) + r_ref[...].astype(F32)).astype(o_ref.dtype)

    return pl.pallas_call(
        body,
        name=name,
        grid_spec=pltpu.PrefetchScalarGridSpec(
            num_scalar_prefetch=1,
            grid=(N_CHIP, R // tr),
            in_specs=[
                pl.BlockSpec((1, tr, C), lambda k, i, core_ref: (2 * k + core_ref[0], i, 0)),
                pl.BlockSpec((1, tr, C), lambda k, i, core_ref: (k, i, 0)),
            ],
            out_specs=pl.BlockSpec((1, tr, C), lambda k, i, core_ref: (k, i, 0)),
        ),
        out_shape=jax.ShapeDtypeStruct((N_CHIP, R, C), g.dtype),
        compiler_params=pltpu.CompilerParams(dimension_semantics=("parallel", "parallel")),
    )(core, g, recv)


HBM = pl.BlockSpec(memory_space=pltpu.HBM)
SEM = pl.BlockSpec(memory_space=pltpu.SEMAPHORE)
EFFECT = pltpu.SideEffectType.DATAFLOW_SIDE_EFFECTING
RELATIONS = tuple((dx, dy, dc) for dx in (0, 1) for dy in (0, 1) for dc in (0, 1))[1:]


def _related(rel):
    x, y, c = lax.axis_index("x"), lax.axis_index("y"), lax.axis_index("c")
    return (1 - x if rel[0] else x, 1 - y if rel[1] else y, 1 - c if rel[2] else c)


def _dev_index(pos):
    return 4 * pos[0] + 2 * pos[1] + pos[2]


def _peers(chips):
    if chips:
        return [r for r in RELATIONS if not r[2]], N_CHIP, lambda pos: 2 * pos[0] + pos[1]
    return list(RELATIONS), N_DEV, _dev_index


def _exchange_start(srcs, by_slot, after, name, chips=False):
    n = len(srcs)
    rels, slots, slot_of = _peers(chips)
    lands = [lax.empty((slots,) + (s.shape[1:] if by_slot else s.shape), s.dtype) for s in srcs]
    nsem = len(rels) * n

    def body(*refs):
        src_refs, land_refs = refs[:n], refs[n : 2 * n]
        send_sems, recv_sems = refs[2 * n + 1], refs[2 * n + 2]
        token = refs[-1]
        me = slot_of(_related((0, 0, 0)))
        for a in range(n):
            for k, rel in enumerate(rels):
                peer = _related(rel)
                pltpu.make_async_remote_copy(
                    src_ref=src_refs[a].at[slot_of(peer)] if by_slot else src_refs[a],
                    dst_ref=land_refs[a].at[me],
                    send_sem=send_sems.at[len(rels) * a + k],
                    recv_sem=recv_sems.at[len(rels) * a + k],
                    device_id=peer,
                    device_id_type=MESH,
                ).start()
        token[...] = jnp.zeros_like(token)

    def hbm(a):
        return pltpu.HBM(a.shape, a.dtype)

    outs = pl.pallas_call(
        body,
        name=name,
        out_shape=(
            pltpu.SemaphoreType.DMA((nsem,)),
            pltpu.SemaphoreType.DMA((nsem,)),
            *[hbm(s) for s in srcs],
            *[hbm(l) for l in lands],
            jax.ShapeDtypeStruct((8, 128), F32),
        ),
        in_specs=[HBM] * (2 * n) + [ANY],
        out_specs=(SEM, SEM, *[HBM] * (2 * n), pl.BlockSpec(memory_space=pltpu.VMEM)),
        input_output_aliases={i: 2 + i for i in range(2 * n)},
        compiler_params=pltpu.CompilerParams(has_side_effects=EFFECT),
    )(*[pltpu.with_memory_space_constraint(a, pltpu.HBM) for a in list(srcs) + lands], after)
    return (outs[0], outs[1], list(outs[2 : 2 + n]), list(outs[2 + n : 2 + 2 * n])), outs[-1]


def _exchange_wait(handle, by_slot, after, name, chips=False):
    send_sems, recv_sems, srcs, lands = handle
    n = len(srcs)
    rels = _peers(chips)[0]

    def body(*refs):
        src_refs, land_refs = refs[:n], refs[n : 2 * n]
        s_sems, r_sems = refs[2 * n], refs[2 * n + 1]
        for a in range(n):
            for k, rel in enumerate(rels):
                copy = pltpu.make_async_remote_copy(
                    src_ref=src_refs[a].at[0] if by_slot else src_refs[a],
                    dst_ref=land_refs[a].at[0],
                    send_sem=s_sems.at[len(rels) * a + k],
                    recv_sem=r_sems.at[len(rels) * a + k],
                    device_id=_related(rel),
                    device_id_type=MESH,
                )
                copy.wait_send()
                copy.wait_recv()

    outs = pl.pallas_call(
        body,
        name=name,
        out_shape=tuple(pltpu.HBM(a.shape, a.dtype) for a in srcs + lands),
        in_specs=[HBM] * (2 * n) + [SEM, SEM, ANY],
        out_specs=tuple([HBM] * (2 * n)),
        input_output_aliases={i: i for i in range(2 * n)},
        compiler_params=pltpu.CompilerParams(has_side_effects=EFFECT),
    )(*srcs, *lands, send_sems, recv_sems, after)
    return list(outs[:n]), list(outs[n:])


NEAR = ((0, 0, 1), (1, 0, 0), (0, 1, 0), (1, 1, 0))


def _gather2_start(blocks, name):
    n = len(blocks)
    lands = [lax.empty((N_DEV,) + b.shape, b.dtype) for b in blocks]

    def body(*refs):
        src_refs, land_refs = refs[:n], refs[n : 2 * n]
        send_sems, recv_sems, token = refs[2 * n], refs[2 * n + 1], refs[-1]
        me = _dev_index(_related((0, 0, 0)))
        for a in range(n):
            for k, rel in enumerate(NEAR):
                pltpu.make_async_remote_copy(
                    src_ref=src_refs[a],
                    dst_ref=land_refs[a].at[me],
                    send_sem=send_sems.at[len(NEAR) * a + k],
                    recv_sem=recv_sems.at[len(NEAR) * a + k],
                    device_id=_related(rel),
                    device_id_type=MESH,
                ).start()
        token[...] = jnp.zeros_like(token)

    nsem = len(NEAR) * n
    outs = pl.pallas_call(
        body,
        name=name,
        out_shape=(
            pltpu.SemaphoreType.DMA((nsem,)),
            pltpu.SemaphoreType.DMA((nsem,)),
            *[pltpu.HBM(a.shape, a.dtype) for a in list(blocks) + lands],
            jax.ShapeDtypeStruct((8, 128), F32),
        ),
        in_specs=[HBM] * (2 * n),
        out_specs=(SEM, SEM, *[HBM] * (2 * n), pl.BlockSpec(memory_space=pltpu.VMEM)),
        input_output_aliases={i: 2 + i for i in range(2 * n)},
        compiler_params=pltpu.CompilerParams(has_side_effects=EFFECT),
    )(*[pltpu.with_memory_space_constraint(a, pltpu.HBM) for a in list(blocks) + lands])
    return (outs[0], outs[1], list(outs[2 : 2 + n]), list(outs[2 + n : 2 + 2 * n])), outs[-1]


def _gather2_forward(handle, after, name):
    send1, recv1, srcs, lands = handle
    n = len(srcs)

    def body(*refs):
        src_refs, land_refs = refs[:n], refs[n : 2 * n]
        s1, r1 = refs[2 * n], refs[2 * n + 1]
        s2, r2 = refs[-2], refs[-1]
        sibling = _related(NEAR[0])
        for a in range(n):
            for k, rel in enumerate(NEAR):
                first = pltpu.make_async_remote_copy(
                    src_ref=src_refs[a],
                    dst_ref=land_refs[a].at[0],
                    send_sem=s1.at[len(NEAR) * a + k],
                    recv_sem=r1.at[len(NEAR) * a + k],
                    device_id=_related(rel),
                    device_id_type=MESH,
                )
                first.wait_send()
                first.wait_recv()
                if k:
                    slot = land_refs[a].at[_dev_index(_related(rel))]
                    pltpu.make_async_remote_copy(
                        src_ref=slot,
                        dst_ref=slot,
                        send_sem=s2.at[3 * a + k - 1],
                        recv_sem=r2.at[3 * a + k - 1],
                        device_id=sibling,
                        device_id_type=MESH,
                    ).start()

    outs = pl.pallas_call(
        body,
        name=name,
        out_shape=(
            *[pltpu.HBM(a.shape, a.dtype) for a in srcs + lands],
            pltpu.SemaphoreType.DMA((3 * n,)),
            pltpu.SemaphoreType.DMA((3 * n,)),
        ),
        in_specs=[HBM] * (2 * n) + [SEM, SEM, ANY],
        out_specs=(*[HBM] * (2 * n), SEM, SEM),
        input_output_aliases={i: i for i in range(2 * n)},
        compiler_params=pltpu.CompilerParams(has_side_effects=EFFECT),
    )(*srcs, *lands, send1, recv1, after)
    return outs[-2], outs[-1], list(outs[:n]), list(outs[n : 2 * n])


def _gather2_wait(handle, name):
    send2, recv2, srcs, lands = handle
    n = len(srcs)

    def body(*refs):
        land_refs = refs[n : 2 * n]
        s2, r2 = refs[2 * n], refs[2 * n + 1]
        for a in range(n):
            for j in range(3):
                passed = pltpu.make_async_remote_copy(
                    src_ref=land_refs[a].at[0],
                    dst_ref=land_refs[a].at[0],
                    send_sem=s2.at[3 * a + j],
                    recv_sem=r2.at[3 * a + j],
                    device_id=_related(NEAR[0]),
                    device_id_type=MESH,
                )
                passed.wait_send()
                passed.wait_recv()

    outs = pl.pallas_call(
        body,
        name=name,
        out_shape=tuple(pltpu.HBM(a.shape, a.dtype) for a in srcs + lands),
        in_specs=[HBM] * (2 * n) + [SEM, SEM],
        out_specs=tuple([HBM] * (2 * n)),
        input_output_aliases={i: i for i in range(2 * n)},
        compiler_params=pltpu.CompilerParams(has_side_effects=EFFECT),
    )(*srcs, *lands, send2, recv2)
    return list(outs[:n]), list(outs[n:])


def _slot_sum(parts, name):
    n, R, C = parts.shape
    tr = _row_tile(R) if R % 16 == 0 else R

    def body(p_ref, o_ref):
        acc = p_ref[0].astype(F32)
        for k in range(1, n):
            acc = acc + p_ref[k].astype(F32)
        o_ref[...] = acc

    return pl.pallas_call(
        body,
        name=name,
        grid=(R // tr,),
        in_specs=[pl.BlockSpec((n, tr, C), lambda i: (0, i, 0))],
        out_specs=pl.BlockSpec((tr, C), lambda i: (i, 0)),
        out_shape=jax.ShapeDtypeStruct((R, C), F32),
        compiler_params=pltpu.CompilerParams(dimension_semantics=("parallel",)),
    )(parts)


W_IN_TC = 256
W_IN_BOUNDS = (0, LAT, LAT + 3 * DIL_QKV, LAT + 3 * DIL_QKV + D_MODEL, D_IN)


def _dqkv_chunks():
    return [((g * 3 + part) * DIL_OUT, LAT + part * DIL_QKV + g * DIL_OUT) for g in range(DIL_GROUPS) for part in range(3)]


def _w_in_regroup(slots, name):
    tc = W_IN_TC

    def body(s_ref, lat_ref, dqkv_ref, g_ref, buf):
        for j in range(N_DEV):
            buf[j * IN_ROWS : (j + 1) * IN_ROWS, :] = s_ref[j].astype(F32)[:IN_ROWS, :]
        lat_ref[:LAT, :] = buf[:LAT, :].astype(BF16)
        lat_ref[LAT:, :] = jnp.zeros((LAT_PAD - LAT, tc), BF16)
        for dst, src in _dqkv_chunks():
            dqkv_ref[dst : dst + DIL_OUT, :] = buf[src : src + DIL_OUT, :].astype(BF16)
        g_ref[...] = buf[W_IN_BOUNDS[2] :, :].astype(BF16)

    def col(rows):
        return pl.BlockSpec((rows, tc), lambda k: (0, k))

    return pl.pallas_call(
        body,
        name=name,
        grid=(D_MODEL // tc,),
        in_specs=[pl.BlockSpec((N_DEV, IN_ROWS_PAD, tc), lambda k: (0, 0, k))],
        out_specs=[col(LAT_PAD), col(3 * DIL_QKV), col(2 * D_MODEL)],
        out_shape=[
            jax.ShapeDtypeStruct((LAT_PAD, D_MODEL), BF16),
            jax.ShapeDtypeStruct((3 * DIL_QKV, D_MODEL), BF16),
            jax.ShapeDtypeStruct((2 * D_MODEL, D_MODEL), BF16),
        ],
        scratch_shapes=[pltpu.VMEM((D_IN, tc), F32)],
        compiler_params=pltpu.CompilerParams(dimension_semantics=("parallel",)),
    )(slots)


def _w_in_grad_regroup(g_lat, g_dqkv, g_ga, g_gb, name):
    tc = W_IN_TC

    def body(lat_ref, dqkv_ref, ga_ref, gb_ref, o_ref, buf):
        b = W_IN_BOUNDS
        buf[b[0] : b[1], :] = lat_ref[:LAT, :].astype(F32)
        for dst, src in _dqkv_chunks():
            buf[src : src + DIL_OUT, :] = dqkv_ref[dst : dst + DIL_OUT, :].astype(F32)
        buf[b[2] : b[3], :] = ga_ref[...].astype(F32)
        buf[b[3] : b[4], :] = gb_ref[...].astype(F32)
        fill = jnp.zeros((IN_ROWS_PAD - IN_ROWS, tc), F32)
        for j in range(N_DEV):
            o_ref[j] = jnp.concatenate([buf[j * IN_ROWS : (j + 1) * IN_ROWS, :], fill], axis=0).astype(BF16)

    def col(rows):
        return pl.BlockSpec((rows, tc), lambda k: (0, k))

    return pl.pallas_call(
        body,
        name=name,
        grid=(D_MODEL // tc,),
        in_specs=[col(LAT_PAD), col(3 * DIL_QKV), col(D_MODEL), col(D_MODEL)],
        out_specs=pl.BlockSpec((N_DEV, IN_ROWS_PAD, tc), lambda k: (0, 0, k)),
        out_shape=jax.ShapeDtypeStruct((N_DEV, IN_ROWS_PAD, D_MODEL), BF16),
        scratch_shapes=[pltpu.VMEM((D_IN, tc), F32)],
        compiler_params=pltpu.CompilerParams(dimension_semantics=("parallel",)),
    )(g_lat, g_dqkv, g_ga, g_gb)


def _ffn_pad(a, axis):
    a = jnp.moveaxis(a, axis, -1)
    g = a.reshape(a.shape[:-1] + (2 * N_DEV, FF_GROUP))
    g = jnp.pad(g, [(0, 0)] * (g.ndim - 1) + [(0, FF_GROUP_PAD - FF_GROUP)])
    return jnp.moveaxis(g.reshape(a.shape[:-1] + (2 * D_FF_PAD,)), -1, axis)


def _ffn_unpad(a, axis):
    a = jnp.moveaxis(a, axis, -1)
    g = a.reshape(a.shape[:-1] + (2 * N_DEV, FF_GROUP_PAD))[..., :FF_GROUP]
    return jnp.moveaxis(g.reshape(a.shape[:-1] + (2 * D_FF,)), -1, axis)


MISC = (("w_o_mla", (256, 1024)), ("w_o_dil", (256, 512)), ("w_uq", (192, 512)), ("w_ukv", (256, 256)))
BIG_WEIGHTS = ("w_in", "w_up", "w_down", "w_out") + tuple(n for n, _ in MISC)


def _exchange_blocks(w):
    def t(a):
        return a.astype(BF16).T

    up = t(w["w_up"]).reshape(2, FF_GROUP, D_MODEL)
    return [
        jnp.pad(t(w["w_in"]), ((0, IN_ROWS_PAD - IN_ROWS), (0, 0))),
        jnp.pad(up, ((0, 0), (0, FF_GROUP_PAD - FF_GROUP), (0, 0))).reshape(2 * FF_GROUP_PAD, D_MODEL),
        jnp.pad(w["w_down"].astype(BF16), ((0, FF_GROUP_PAD - FF_GROUP), (0, 0))),
        w["w_out"].astype(BF16),
        jnp.concatenate([t(w[n]).reshape(-1, D_MODEL) for n, _ in MISC], axis=0),
    ]


def _misc_split(misc):
    out, off = {}, 0
    for n, (r, c) in MISC:
        rows = r * c // D_MODEL
        out[n] = misc[..., off : off + rows, :].reshape(misc.shape[:-2] + (r, c))
        off += rows
    return out


def _small_matrices(g_misc):
    misc = _misc_split(g_misc)
    uq_t = jnp.pad(misc["w_uq"], ((0, 0), (0, HEAD_PAD - QK_NOPE - QK_ROPE), (0, 0)))
    return {
        "uq_t": uq_t.reshape(MLA_HEADS * HEAD_PAD, Q_LORA),
        "ukv_t": misc["w_ukv"].reshape(MLA_HEADS * HEAD_PAD, KV_LORA),
        "o_mla_t": misc["w_o_mla"].reshape(D_MODEL, MLA_HEADS * V_HEAD),
        "o_dil_t": misc["w_o_dil"].reshape(D_MODEL, DIL_OUT),
    }


def _small_grad_blocks(g):
    uq_t = g["uq_t"].reshape(MLA_HEADS, HEAD_PAD, Q_LORA)[:, : QK_NOPE + QK_ROPE]
    misc = {"w_o_mla": g["o_mla_t"], "w_o_dil": g["o_dil_t"], "w_uq": uq_t, "w_ukv": g["ukv_t"]}
    return [
        g["w_out"].reshape(N_DEV, -1, D_MODEL),
        jnp.concatenate([misc[n].reshape(N_DEV, -1, D_MODEL) for n, _ in MISC], axis=1),
    ]


def _grad_shards(sums):
    s_in, s_out, s_misc, s_up, s_down = sums
    out = {
        "w_in": s_in[:IN_ROWS].T,
        "w_up": s_up.reshape(2, FF_GROUP_PAD, D_MODEL)[:, :FF_GROUP].reshape(2 * FF_GROUP, D_MODEL).T,
        "w_down": s_down[:FF_GROUP],
        "w_out": s_out,
    }
    out.update({n: v.T for n, v in _misc_split(s_misc).items()})
    return out


def _local_step(x, h, tgt, wt, conv_w, small, small_matrices, ffn_weight, send_ffn_grads, send_small_grads, send_w_in_grads, start_token):
    S = x.shape[0]
    lat_t, dqkv_t, g_t = wt
    cw = jnp.pad(_ffn_pad(conv_w, 1), ((0, 5), (0, 0)))
    cb = _ffn_pad(small["conv_b"], 1)
    cos_t, sin_t = _rope_tables(S)
    bias = _dil_bias()
    g1, g2, g3 = small["attn_norm_g"], small["ffn_norm_g"], small["final_norm_g"]
    gq, gkv = small["q_norm_g"], small["kv_norm_g"]

    lat = _mm(h, lat_t + start_token.astype(BF16), "nt", F32, 1024, LAT_PAD, D_MODEL, "proj_lat")
    dqkv = _mm(h, dqkv_t, "nt", F32, 1024, 1536, D_MODEL, "proj_dqkv")
    gates = _mm(h, g_t, "nt", F32, 1024, 1024, D_MODEL, "proj_gates", bias=small["b_gate"], act="sigmoid")
    sm = small_matrices(gates)
    uq_t, ukv_t, o_mla_t, o_dil_t = sm["uq_t"], sm["ukv_t"], sm["o_mla_t"], sm["o_dil_t"]
    cqn, ckvn, kpe = _mla_prep1(lat, gq, gkv, cos_t, sin_t, "mla_prep1")
    q_raw = _mm(cqn, uq_t, "nt", F32, 1024, 1024, Q_LORA, "mla_uq")
    kv = _mm(ckvn, ukv_t, "nt", BF16, 1024, 1024, KV_LORA, "mla_ukv")
    q_att, k_att = _mla_prep2(q_raw, kv, kpe, cos_t, sin_t, "mla_prep2")
    o, lse = _flash2_fwd(q_att, k_att, kv, "mla_flash_fwd")
    o_a = _mm(o, o_mla_t, "nt", F32, 1024, 1024, MLA_HEADS * V_HEAD, "mla_out")

    d_os, d_ls = [], []
    for g, (_, dil) in enumerate(DIL_PATTERNS):
        og, lg = _dil_fwd_group(dqkv, bias[g], g, dil, f"dil_fwd_{g}")
        d_os.append(og)
        d_ls.append(lg)
    od, dil_lse = _dil_combine(d_os, d_ls, "dil_combine")
    o_b = _mm(od, o_dil_t, "nt", F32, 1024, 1024, DIL_OUT, "dil_out")

    mrg = _merge_fwd(gates, o_a, o_b, "merge_fwd")
    w_out = ffn_weight("w_out", mrg)
    x1, h2 = _mm_res_rms(mrg, w_out, x, g2, "mix_out")
    up_t = ffn_weight("up_t", h2)
    u0 = _mm(h2, up_t, "nt", F32, 1024, 1024, D_MODEL, "ffn_up")
    a = _ffn_fwd(u0, cw, cb, "ffn_conv_fwd")
    w_down = ffn_weight("w_down", a)
    x2 = _mm(a, w_down, "nn", F32, 1024, 512, D_FF_PAD // 2, "ffn_down", res=x1)
    loss_part, dx2, dx2b, dg3 = _final_loss(x2, g3, tgt, "final_loss")

    da = _mm(dx2b, w_down, "nt", F32, 1024, 512, D_MODEL, "ffn_down_dx")
    gw_down = _mm(a, dx2b, "tn", BF16, 512, 1024, S, "ffn_down_dw")
    du0, dcw, dcb = _ffn_bwd(u0, da, cw, cb, "ffn_conv_bwd")
    du0 = du0.reshape(2 * S, D_FF_PAD)
    gw_up_t = _mm(du0, h2, "tn", BF16, 512, 1024, S, "ffn_up_dw", a_halves=2)
    sent = send_ffn_grads(gw_up_t, gw_down)
    dh2 = _mm(du0, up_t, "nn", F32, 1024, 1024, D_FF_PAD // 2, "ffn_up_dx", a_halves=2)
    dx1, dx1b, dg2 = _rms_bwd(dh2, x1, g2 + sent, dx2, "rms_ffn_bwd")

    dmrg = _mm(dx1b, w_out, "nt", F32, 1024, 1024, D_MODEL, "mix_out_dx")
    gw_out = _mm(mrg, dx1b, "tn", BF16, 512, 1024, S, "mix_out_dw")
    do_a, do_b, dga, dgb, dba, dbb = _merge_bwd(dmrg, gates, o_a, o_b, "merge_bwd")

    do = _mm(do_a, o_mla_t, "nn", BF16, 1024, 1024, D_MODEL, "mla_out_dx")
    gw_o_mla_t = _mm(do_a, o, "tn", BF16, 1024, 1024, 1024, "mla_out_dw")
    dod = _mm(do_b, o_dil_t, "nn", F32, 1024, DIL_OUT, D_MODEL, "dil_out_dx")
    gw_o_dil_t = _mm(do_b, od, "tn", BF16, 1024, DIL_OUT, 1024, "dil_out_dw")

    delta = _flash_delta(do, o, "mla_flash_delta")
    lse_row = lse[:, :, 0][:, None, :]
    delta_row = delta[:, :MLA_HEADS].T[:, None, :]
    dq_att, dk_att, dv = _flash2_bwd(q_att, k_att, kv, do, lse_row, delta_row, "mla_flash_bwd")
    dq_raw, dkv, dkpe = _mla_post(dq_att, dk_att, dv, cos_t, sin_t, "mla_post")
    dcqn = _mm(dq_raw, uq_t, "nn", F32, 1024, Q_LORA, MLA_HEADS * HEAD_PAD, "mla_uq_dx")
    gw_uq_t = _mm(dq_raw, cqn, "tn", BF16, 1024, Q_LORA, 1024, "mla_uq_dw")
    dckvn = _mm(dkv, ukv_t, "nn", F32, 1024, KV_LORA, MLA_HEADS * HEAD_PAD, "mla_ukv_dx")
    gw_ukv_t = _mm(dkv, ckvn, "tn", BF16, 1024, KV_LORA, 1024, "mla_ukv_dw")
    sent = send_small_grads({"uq_t": gw_uq_t, "ukv_t": gw_ukv_t, "o_mla_t": gw_o_mla_t, "o_dil_t": gw_o_dil_t, "w_out": gw_out})
    dlat, dgq, dgkv = _lat_bwd(dcqn, dckvn, dkpe, lat, gq + sent, gkv, "lat_bwd")

    dd = _dil_rowdot(dod, od, "dil_rowdot")
    ddqkv = lax.empty((3 * DIL_GROUPS, S, DIL_OUT), F32)
    for g, (_, dil) in enumerate(DIL_PATTERNS):
        ddqkv = _dil_bwd_group(dqkv, bias[g], dod, dd, dil_lse, ddqkv, g, dil, f"dil_bwd_{g}")
    gw_lat_t = _mm(dlat, h, "tn", BF16, LAT_PAD, 1024, S, "proj_lat_dw")
    gw_dqkv_t = _mm(ddqkv.reshape(3 * DIL_GROUPS * S, DIL_OUT), h, "tn", BF16, 512, 1024, S, "proj_dqkv_dw", a_halves=3 * DIL_GROUPS)
    gw_ga_t = _mm(dga, h, "tn", BF16, 512, 1024, S, "proj_ga_dw")
    gw_gb_t = _mm(dgb, h, "tn", BF16, 512, 1024, S, "proj_gb_dw")
    sent = send_w_in_grads(gw_lat_t, gw_dqkv_t, gw_ga_t, gw_gb_t)
    dh = _mm(dlat + sent.astype(BF16), lat_t, "nn", F32, 1024, 1024, LAT_PAD, "proj_lat_dx")
    dh = _stacked_mm(ddqkv, dqkv_t, dh, "proj_dqkv_dx")
    dh = _mm(dga, g_t, "nn", F32, 1024, 1024, D_MODEL, "proj_ga_dx", res=dh)
    grad_x, dg1 = _mm_rms_bwd(dgb, g_t, 1, dh, x, g1, dx1, "proj_gb_dx_rms_attn_bwd")

    small_grads = {
        "attn_norm_g": dg1,
        "b_gate": jnp.concatenate([dba, dbb], axis=1),
        "q_norm_g": dgq,
        "kv_norm_g": dgkv,
        "ffn_norm_g": dg2,
        "conv_b": _ffn_unpad(jnp.concatenate([dcb[0], dcb[1]], axis=1), 1),
        "final_norm_g": dg3,
        "conv_w": _ffn_unpad(jnp.concatenate([dcw[0, :3], dcw[1, :3]], axis=1), 1),
    }
    return loss_part, grad_x, small_grads


SMALL_ORDER = ("attn_norm_g", "b_gate", "q_norm_g", "kv_norm_g", "ffn_norm_g", "conv_b", "final_norm_g", "conv_w")
WEIGHT_ORDER = (
    "attn_norm_g", "w_in", "b_gate", "q_norm_g", "w_uq", "kv_norm_g", "w_ukv", "w_o_mla", "w_o_dil", "w_out",
    "ffn_norm_g", "w_up", "conv_w", "conv_b", "w_down", "final_norm_g",
)


def kernel(x, attn_norm_g, w_in, b_gate, q_norm_g, w_uq, kv_norm_g, w_ukv, w_o_mla, w_o_dil, w_out, ffn_norm_g, w_up, conv_w, conv_b, w_down, final_norm_g, loss_target, m_attn_norm_g, m_w_in, m_b_gate, m_q_norm_g, m_w_uq, m_kv_norm_g, m_w_ukv, m_w_o_mla, m_w_o_dil, m_w_out, m_ffn_norm_g, m_w_up, m_conv_w, m_conv_b, m_w_down, m_final_norm_g, v_attn_norm_g, v_w_in, v_b_gate, v_q_norm_g, v_w_uq, v_kv_norm_g, v_w_ukv, v_w_o_mla, v_w_o_dil, v_w_out, v_ffn_norm_g, v_w_up, v_conv_w, v_conv_b, v_w_down, v_final_norm_g):
    env = dict(locals())
    dev = 4 * lax.axis_index("x") + 2 * lax.axis_index("y") + lax.axis_index("c")
    core = lax.axis_index("c").astype(jnp.int32).reshape(1)

    def two_d(a):
        return a.reshape(-1, a.shape[-1])

    w = {n: two_d(env[n]) for n in WEIGHT_ORDER}
    m = {n: two_d(env["m_" + n]) for n in WEIGHT_ORDER}
    v = {n: two_d(env["v_" + n]) for n in WEIGHT_ORDER}

    chip = 2 * lax.axis_index("x") + lax.axis_index("y")

    def own_slot_in(lands, own, slot=dev):
        return [lax.dynamic_update_slice(l, o[None], (slot, 0, 0)) for l, o in zip(lands, own)]

    b_in = _exchange_blocks(w)[0]
    r, c = CONV_SHARD
    conv = jnp.pad(w["conv_w"].reshape(-1), (0, 8 * SMALL_COLS - r * c)).reshape(8, SMALL_COLS)
    first_level, token = _gather2_start([b_in, conv], "ag_w_in_start")
    tied = {n: w[n] + token[0, 0] for n in BIG_WEIGHTS}
    _, b_up, b_down, b_out, b_misc = _exchange_blocks(tied)
    h = _rms_fwd(x[0], w["attn_norm_g"] + token[0, 0], "rms_attn")
    prepared = b_up[:1, :1] + b_down[:1, :1] + b_out[:1, :1] + b_misc[:1, :1] + h[:1, :1]
    own, lands = _gather2_wait(_gather2_forward(first_level, prepared, "ag_w_in_forward"), "ag_w_in_wait")
    g_in, conv = own_slot_in(lands, own)
    misc_gather, started = _exchange_start([b_misc], False, conv, "ag_small_start")
    ffn_gathers, started2 = {}, started
    for key, block in (("w_out", b_out), ("up_t", b_up), ("w_down", b_down)):
        ffn_gathers[key], started2 = _exchange_start([block], False, started2, f"ag_{key}_start")
    wt = _w_in_regroup(g_in, "w_in_regroup")
    conv = conv.reshape(N_DEV, 8 * SMALL_COLS)[:, : r * c].reshape(N_DEV, r, c)
    conv_w_full = conv.transpose(1, 0, 2).reshape(r, N_DEV * c)
    small = {n: w[n] for n in SMALL_ORDER if n != "conv_w"}

    def small_matrices(after):
        own, lands = _exchange_wait(misc_gather, False, after, "ag_small_wait")
        return _small_matrices(own_slot_in(lands, own)[0])

    def ffn_weight(key, after):
        own, lands = _exchange_wait(ffn_gathers[key], False, after, f"ag_{key}_wait")
        return own_slot_in(lands, own)[0].reshape(-1, D_MODEL)

    reduces = {}

    def send_ffn_grads(gw_up_t, gw_down):
        blocks = [gw_up_t.reshape(N_DEV, 2 * FF_GROUP_PAD, D_MODEL), gw_down.reshape(N_DEV, FF_GROUP_PAD, D_MODEL)]
        reduces["ffn"], token = _exchange_start(blocks, True, gw_down, "rs_ffn_start")
        return token[0, 0]

    def send_small_grads(g):
        reduces["small"], token = _exchange_start(_small_grad_blocks(g), True, g["w_out"], "rs_small_start")
        return token[0, 0]

    def send_w_in_grads(g_lat, g_dqkv, g_ga, g_gb):
        e_in = _w_in_grad_regroup(g_lat, g_dqkv, g_ga, g_gb, "w_in_grad_regroup")
        pair = _pair_add(e_in, _pair_exchange([e_in], "rs_w_in_pair_exchange")[0], core, "rs_w_in_pair_add")
        reduces["w_in"], token = _exchange_start([pair], True, pair, "rs_w_in_start", chips=True)
        return token[0, 0]

    loss_part, grad_x, small_grads = _local_step(
        x[0], h, loss_target[0], wt, conv_w_full, small, small_matrices, ffn_weight,
        send_ffn_grads, send_small_grads, send_w_in_grads, started2[0, 0],
    )
    loss = lax.psum(loss_part[0, 0], AXES)
    sflat = jnp.concatenate([small_grads[n].reshape(-1) for n in SMALL_ORDER])
    sflat = jnp.pad(sflat, (0, SMALL_ROWS * SMALL_COLS - sflat.shape[0])).reshape(SMALL_ROWS, SMALL_COLS)
    vec_gather, _ = _exchange_start([sflat], False, sflat, "rs_vec_start")

    def finish(key, by_chip, name):
        sent, lands = _exchange_wait(reduces[key], True, grad_x, name + "_wait", chips=by_chip)
        slot = chip if by_chip else dev
        own = [lax.dynamic_index_in_dim(s, slot, 0, keepdims=False) for s in sent]
        return [_slot_sum(p, f"{name}_sum_{i}") for i, p in enumerate(own_slot_in(lands, own, slot))]

    (s_in,) = finish("w_in", True, "rs_w_in")
    s_out, s_misc = finish("small", False, "rs_small")
    s_up, s_down = finish("ffn", False, "rs_ffn")
    gshard = _grad_shards([s_in, s_out, s_misc, s_up, s_down])

    updates = {n: _adamw(w[n], gshard[n], m[n], v[n], "adamw_" + n) for n in BIG_WEIGHTS}

    own, lands = _exchange_wait(vec_gather, False, updates["w_ukv"][0], "rs_vec_wait")
    ssum = _slot_sum(own_slot_in(lands, own)[0], "small_sum").reshape(-1)
    gsmall, off = {}, 0
    for n in SMALL_ORDER:
        shape = (3, 2 * D_FF) if n == "conv_w" else w[n].shape
        size = shape[0] * shape[1]
        gsmall[n] = ssum[off : off + size].reshape(shape)
        off += size
    gsmall["conv_w"] = lax.dynamic_slice_in_dim(gsmall["conv_w"], dev * CONV_SHARD[1], CONV_SHARD[1], axis=1)
    updates.update({n: _adamw(w[n], gsmall[n], m[n], v[n], "adamw_" + n) for n in SMALL_ORDER})

    g_all = {**gshard, **gsmall}
    out_g, out_d, out_m, out_v = [], [], [], []
    for n in WEIGHT_ORDER:
        d, nm, nv = updates[n]
        shape = env[n].shape
        out_g.append(g_all[n].reshape(shape))
        out_d.append(d.reshape(shape))
        out_m.append(nm.reshape(shape))
        out_v.append(nv.reshape(shape))
    return (loss, grad_x[None], *out_g, *out_d, *out_m, *out_v)
```

```python
import functools

import jax
import jax.numpy as jnp
import numpy as np
from jax import lax
from jax.experimental import pallas as pl
from jax.experimental.pallas import tpu as pltpu

F32 = jnp.float32
BF16 = jnp.bfloat16

N_DEV = 8
N_CHIP = 4
AXES = ("x", "y", "c")
MESH = pl.DeviceIdType.MESH

D_MODEL = 2048
MLA_HEADS = 8
QK_NOPE = 128
QK_ROPE = 64
V_HEAD = 128
Q_LORA = 512
KV_LORA = 256
ROPE_THETA = 10000.0
HEAD_PAD = 256
DIL_PATTERNS = ((128, 1), (512, 4), (2048, 16))
DIL_GROUPS = 3
DIL_HG = 4
DIL_HEADS = 12
DIL_HD = 128
DIL_BLK = 128
DIL_QKV = DIL_HEADS * DIL_HD
DIL_OUT = DIL_HG * DIL_HD
ALIBI_MAX_BIAS = 8.0
D_FF = 5504
D_FF_PAD = 5632
NORM_EPS = 1e-6
LAT = Q_LORA + KV_LORA + QK_ROPE
LAT_PAD = 896
D_IN = LAT + 3 * DIL_QKV + 2 * D_MODEL
NEG = -1e30

ADAM_LR = 0.001
ADAM_B1 = 0.9
ADAM_B2 = 0.999
ADAM_EPS = 1e-08
ADAM_WD = 0.01
ADAM_STEP = 10

SMALL_ROWS = 56
SMALL_COLS = 1024

IN_ROWS = 1192
IN_ROWS_PAD = 1200
FF_GROUP = D_FF // N_DEV
FF_GROUP_PAD = D_FF_PAD // N_DEV
CONV_SHARD = (3, 1376)

NT = (((1,), (1,)), ((), ()))
TN = (((0,), (0,)), ((), ()))


def _dot(a, b, dims=(((1,), (0,)), ((), ()))):
    return lax.dot_general(a, b, dims, preferred_element_type=F32)


def _mm(a, b, mode, out_dtype, tm, tn, tk, name, bias=None, act=None, res=None, b_koff=0, a_halves=1):
    H = a_halves
    if mode == "nn":
        (M, K), (K2, N) = (a.shape[0] // H, a.shape[1] * H), b.shape
        assert (b_koff + 1) * K <= K2, (name, a.shape, b.shape)
        koff, K2 = b_koff * (K // tk), K
        kper, mrows = a.shape[1] // tk, M // tm
        a_spec = pl.BlockSpec((tm, tk), lambda i, j, k: (i + (k // kper) * mrows, k % kper))
        b_spec = pl.BlockSpec((tk, tn), lambda i, j, k: (k + koff, j))
        dims = (((1,), (0,)), ((), ()))
    elif mode == "nt":
        (M, K), (N, K2) = a.shape, b.shape
        a_spec = pl.BlockSpec((tm, tk), lambda i, j, k: (i, k))
        b_spec = pl.BlockSpec((tn, tk), lambda i, j, k: (j, k))
        dims = NT
    else:
        (K, M), (K2, N) = (a.shape[0] // H, a.shape[1] * H), b.shape
        mper, krows = a.shape[1] // tm, K // tk
        a_spec = pl.BlockSpec((tk, tm), lambda i, j, k: (k + (i // mper) * krows, i % mper))
        b_spec = pl.BlockSpec((tk, tn), lambda i, j, k: (k, j))
        dims = TN
    assert K == K2 and M % tm == 0 and N % tn == 0 and K % tk == 0, (name, a.shape, b.shape)
    nk = K // tk
    has_bias, has_res = bias is not None, res is not None

    def body(*refs):
        refs = list(refs)
        a_ref, b_ref = refs[0], refs[1]
        pos = 2
        bias_ref = res_ref = None
        if has_bias:
            bias_ref = refs[pos]
            pos += 1
        if has_res:
            res_ref = refs[pos]
            pos += 1
        o_ref = refs[pos]
        p = _dot(a_ref[...].astype(BF16), b_ref[...].astype(BF16), dims)

        def finish(acc):
            if has_bias:
                acc = acc + bias_ref[...]
            if act == "sigmoid":
                acc = jax.nn.sigmoid(acc)
            if has_res:
                acc = res_ref[...] + acc
            o_ref[...] = acc.astype(o_ref.dtype)

        if nk == 1:
            finish(p)
        else:
            acc_ref = refs[pos + 1]
            k = pl.program_id(2)

            @pl.when(k == 0)
            def _():
                acc_ref[...] = p

            @pl.when(k != 0)
            def _():
                acc_ref[...] += p

            @pl.when(k == nk - 1)
            def _():
                finish(acc_ref[...])

    in_specs = [a_spec, b_spec]
    args = [a, b]
    if has_bias:
        in_specs.append(pl.BlockSpec((1, tn), lambda i, j, k: (0, j)))
        args.append(bias)
    if has_res:
        in_specs.append(pl.BlockSpec((tm, tn), lambda i, j, k: (i, j)))
        args.append(res)
    return pl.pallas_call(
        body,
        name=name,
        grid=(M // tm, N // tn, nk),
        in_specs=in_specs,
        out_specs=pl.BlockSpec((tm, tn), lambda i, j, k: (i, j)),
        out_shape=jax.ShapeDtypeStruct((M, N), out_dtype),
        scratch_shapes=[pltpu.VMEM((tm, tn), F32)] if nk > 1 else [],
        compiler_params=pltpu.CompilerParams(dimension_semantics=("parallel", "parallel", "arbitrary")),
    )(*args)


def _stacked_mm(pieces, w_t, res, name, tm=512, tn=1024):
    P, M, W = pieces.shape
    N = w_t.shape[1]

    def body(a_ref, b_ref, r_ref, o_ref):
        acc = r_ref[...]
        for p in range(P):
            acc = acc + _dot(a_ref[p].astype(BF16), b_ref[p * W : (p + 1) * W, :])
        o_ref[...] = acc

    tile = pl.BlockSpec((tm, tn), lambda i, j: (i, j))
    return pl.pallas_call(
        body,
        name=name,
        grid=(M // tm, N // tn),
        in_specs=[pl.BlockSpec((P, tm, W), lambda i, j: (0, i, 0)), pl.BlockSpec((P * W, tn), lambda i, j: (0, j)), tile],
        out_specs=tile,
        out_shape=jax.ShapeDtypeStruct((M, N), F32),
        compiler_params=pltpu.CompilerParams(dimension_semantics=("parallel", "parallel")),
    )(pieces, w_t, res)


def _rstd(x):
    return lax.rsqrt(jnp.mean(x * x, axis=-1, keepdims=True) + NORM_EPS)


def _rms_bwd_math(dy, x, g):
    r = _rstd(x)
    xh = x * r
    dg = jnp.sum(dy * xh, axis=0, keepdims=True)
    dxh = dy * g
    dx = r * (dxh - xh * jnp.mean(dxh * xh, axis=-1, keepdims=True))
    return dx, dg


def _rms_fwd(x, g, name, tr=256):
    S, D = x.shape

    def body(x_ref, g_ref, o_ref):
        xv = x_ref[...]
        o_ref[...] = ((xv * _rstd(xv)) * g_ref[...]).astype(o_ref.dtype)

    return pl.pallas_call(
        body,
        name=name,
        grid=(S // tr,),
        in_specs=[pl.BlockSpec((tr, D), lambda i: (i, 0)), pl.BlockSpec((1, D), lambda i: (0, 0))],
        out_specs=pl.BlockSpec((tr, D), lambda i: (i, 0)),
        out_shape=jax.ShapeDtypeStruct((S, D), BF16),
        compiler_params=pltpu.CompilerParams(dimension_semantics=("parallel",)),
    )(x, g)


def _rms_bwd(dy, x, g, res, name, tr=256):
    S, D = x.shape

    def body(dy_ref, x_ref, g_ref, res_ref, dx_ref, dxb_ref, dg_ref):
        dx, dg = _rms_bwd_math(dy_ref[...], x_ref[...], g_ref[...])
        dx = dx + res_ref[...]
        dx_ref[...] = dx
        dxb_ref[...] = dx.astype(BF16)

        @pl.when(pl.program_id(0) == 0)
        def _():
            dg_ref[...] = dg

        @pl.when(pl.program_id(0) != 0)
        def _():
            dg_ref[...] += dg

    row = pl.BlockSpec((tr, D), lambda i: (i, 0))
    vec = pl.BlockSpec((1, D), lambda i: (0, 0))
    return pl.pallas_call(
        body,
        name=name,
        grid=(S // tr,),
        in_specs=[row, row, vec, row],
        out_specs=[row, row, vec],
        out_shape=[jax.ShapeDtypeStruct((S, D), F32), jax.ShapeDtypeStruct((S, D), BF16), jax.ShapeDtypeStruct((1, D), F32)],
        compiler_params=pltpu.CompilerParams(dimension_semantics=("arbitrary",)),
    )(dy, x, g, res)


def _mm_res_rms(a, b, res, g, name, tm=256):
    M, K = a.shape
    D = b.shape[1]

    def body(a_ref, b_ref, res_ref, g_ref, y_ref, h_ref):
        y = res_ref[...] + _dot(a_ref[...], b_ref[...])
        y_ref[...] = y
        h_ref[...] = ((y * _rstd(y)) * g_ref[...]).astype(BF16)

    row = pl.BlockSpec((tm, D), lambda i: (i, 0))
    return pl.pallas_call(
        body,
        name=name,
        grid=(M // tm,),
        in_specs=[pl.BlockSpec((tm, K), lambda i: (i, 0)), pl.BlockSpec((K, D), lambda i: (0, 0)), row, pl.BlockSpec((1, D), lambda i: (0, 0))],
        out_specs=[row, row],
        out_shape=[jax.ShapeDtypeStruct((M, D), F32), jax.ShapeDtypeStruct((M, D), BF16)],
        compiler_params=pltpu.CompilerParams(dimension_semantics=("parallel",)),
    )(a, b, res, g)


def _mm_rms_bwd(a, b, b_koff, dy_part, x, g, res, name, tm=256):
    M, K = a.shape
    D = x.shape[1]

    def body(a_ref, b_ref, dyp_ref, x_ref, g_ref, res_ref, dx_ref, dg_ref):
        dy = dyp_ref[...] + _dot(a_ref[...], b_ref[...])
        dx, dg = _rms_bwd_math(dy, x_ref[...], g_ref[...])
        dx_ref[...] = dx + res_ref[...]

        @pl.when(pl.program_id(0) == 0)
        def _():
            dg_ref[...] = dg

        @pl.when(pl.program_id(0) != 0)
        def _():
            dg_ref[...] += dg

    row = pl.BlockSpec((tm, D), lambda i: (i, 0))
    vec = pl.BlockSpec((1, D), lambda i: (0, 0))
    return pl.pallas_call(
        body,
        name=name,
        grid=(M // tm,),
        in_specs=[pl.BlockSpec((tm, K), lambda i: (i, 0)), pl.BlockSpec((K, D), lambda i: (b_koff, 0)), row, row, vec, row],
        out_specs=[row, vec],
        out_shape=[jax.ShapeDtypeStruct((M, D), F32), jax.ShapeDtypeStruct((1, D), F32)],
        compiler_params=pltpu.CompilerParams(dimension_semantics=("arbitrary",)),
    )(a, b, dy_part, x, g, res)


def _final_loss(x2, g, tgt, name, tr=256):
    S, D = x2.shape

    def body(x_ref, g_ref, t_ref, loss_ref, dx_ref, dxb_ref, dg_ref):
        xv, gv = x_ref[...], g_ref[...]
        y = (xv * _rstd(xv)) * gv
        e = y - t_ref[...]
        part = 0.5 * jnp.sum(jnp.mean(e * e, axis=-1, keepdims=True), axis=0, keepdims=True)
        dx, dg = _rms_bwd_math(e * (1.0 / D), xv, gv)
        dx_ref[...] = dx
        dxb_ref[...] = dx.astype(BF16)
        part = jnp.broadcast_to(part, (1, 128))

        @pl.when(pl.program_id(0) == 0)
        def _():
            dg_ref[...] = dg
            loss_ref[...] = part

        @pl.when(pl.program_id(0) != 0)
        def _():
            dg_ref[...] += dg
            loss_ref[...] += part

    row = pl.BlockSpec((tr, D), lambda i: (i, 0))
    vec = pl.BlockSpec((1, D), lambda i: (0, 0))
    return pl.pallas_call(
        body,
        name=name,
        grid=(S // tr,),
        in_specs=[row, vec, row],
        out_specs=[pl.BlockSpec((1, 128), lambda i: (0, 0)), row, row, vec],
        out_shape=[
            jax.ShapeDtypeStruct((1, 128), F32),
            jax.ShapeDtypeStruct((S, D), F32),
            jax.ShapeDtypeStruct((S, D), BF16),
            jax.ShapeDtypeStruct((1, D), F32),
        ],
        compiler_params=pltpu.CompilerParams(dimension_semantics=("arbitrary",)),
    )(x2, g, tgt)


def _rope_tables(S):
    pos = jnp.arange(S, dtype=F32)
    inv_freq = ROPE_THETA ** (-jnp.arange(0, QK_ROPE, 2, dtype=F32) / QK_ROPE)
    ang = pos[:, None] * inv_freq[None, :]
    cos, sin = jnp.cos(ang), jnp.sin(ang)
    zero = jnp.zeros((S, 128 - QK_ROPE), F32)
    return jnp.concatenate([cos, cos, zero], axis=1), jnp.concatenate([-sin, sin, zero], axis=1)


def _rope_tile(x, cos_t, sin_t):
    lane = lax.broadcasted_iota(jnp.int32, x.shape, 1)
    partner = jnp.where(lane < QK_ROPE // 2, pltpu.roll(x, 128 - QK_ROPE // 2, 1), pltpu.roll(x, QK_ROPE // 2, 1))
    return x * cos_t + partner * sin_t


def _mla_prep1(lat, gq, gkv, cos_t, sin_t, name, tr=256):
    S = lat.shape[0]

    def body(lat_ref, gq_ref, gkv_ref, cos_ref, sin_ref, cq_ref, ckv_ref, kpe_ref):
        cq = lat_ref[:, :Q_LORA]
        ckv = lat_ref[:, Q_LORA : Q_LORA + KV_LORA]
        cq_ref[...] = ((cq * _rstd(cq)) * gq_ref[...]).astype(BF16)
        ckv_ref[...] = ((ckv * _rstd(ckv)) * gkv_ref[...]).astype(BF16)
        kpe_ref[...] = _rope_tile(lat_ref[:, Q_LORA + KV_LORA :], cos_ref[...], sin_ref[...]).astype(BF16)

    def row(n):
        return pl.BlockSpec((tr, n), lambda i: (i, 0))

    def vec(n):
        return pl.BlockSpec((1, n), lambda i: (0, 0))

    return pl.pallas_call(
        body,
        name=name,
        grid=(S // tr,),
        in_specs=[row(LAT_PAD), vec(Q_LORA), vec(KV_LORA), row(128), row(128)],
        out_specs=[row(Q_LORA), row(KV_LORA), row(128)],
        out_shape=[
            jax.ShapeDtypeStruct((S, Q_LORA), BF16),
            jax.ShapeDtypeStruct((S, KV_LORA), BF16),
            jax.ShapeDtypeStruct((S, 128), BF16),
        ],
        compiler_params=pltpu.CompilerParams(dimension_semantics=("parallel",)),
    )(lat, gq, gkv, cos_t, sin_t)


def _mla_prep2(q_raw, kv, kpe, cos_t, sin_t, name, tr=256):
    S = q_raw.shape[0]
    W = MLA_HEADS * HEAD_PAD

    def body(q_ref, kv_ref, kpe_ref, cos_ref, sin_ref, qa_ref, ka_ref):
        cos_v, sin_v, kpe_v = cos_ref[...], sin_ref[...], kpe_ref[...]
        for h in range(MLA_HEADS):
            lo = h * HEAD_PAD
            qa_ref[:, lo : lo + 128] = q_ref[:, lo : lo + 128].astype(BF16)
            qa_ref[:, lo + 128 : lo + 256] = _rope_tile(q_ref[:, lo + 128 : lo + 256], cos_v, sin_v).astype(BF16)
            ka_ref[:, lo : lo + 128] = kv_ref[:, lo : lo + 128]
            ka_ref[:, lo + 128 : lo + 256] = kpe_v

    def row(n):
        return pl.BlockSpec((tr, n), lambda i: (i, 0))

    return pl.pallas_call(
        body,
        name=name,
        grid=(S // tr,),
        in_specs=[row(W), row(W), row(128), row(128), row(128)],
        out_specs=[row(W), row(W)],
        out_shape=[jax.ShapeDtypeStruct((S, W), BF16), jax.ShapeDtypeStruct((S, W), BF16)],
        compiler_params=pltpu.CompilerParams(dimension_semantics=("parallel",)),
    )(q_raw, kv, kpe, cos_t, sin_t)


def _mla_post(dq_att, dk_att, dv, cos_t, sin_t, name, tr=256):
    S = dq_att.shape[0]
    W = MLA_HEADS * HEAD_PAD

    def body(dq_ref, dk_ref, dv_ref, cos_ref, sin_ref, dqr_ref, dkv_ref, dkpe_ref):
        cos_v, nsin_v = cos_ref[...], -sin_ref[...]
        kpe = jnp.zeros((tr, 128), F32)
        for h in range(MLA_HEADS):
            lo = h * HEAD_PAD
            dqr_ref[:, lo : lo + 128] = dq_ref[:, lo : lo + 128].astype(BF16)
            dqr_ref[:, lo + 128 : lo + 256] = _rope_tile(dq_ref[:, lo + 128 : lo + 256], cos_v, nsin_v).astype(BF16)
            dkv_ref[:, lo : lo + 128] = dk_ref[:, lo : lo + 128].astype(BF16)
            dkv_ref[:, lo + 128 : lo + 256] = dv_ref[:, h * 128 : (h + 1) * 128].astype(BF16)
            kpe = kpe + dk_ref[:, lo + 128 : lo + 256]
        dkpe_ref[...] = _rope_tile(kpe, cos_v, nsin_v)

    def row(n):
        return pl.BlockSpec((tr, n), lambda i: (i, 0))

    return pl.pallas_call(
        body,
        name=name,
        grid=(S // tr,),
        in_specs=[row(W), row(W), row(MLA_HEADS * V_HEAD), row(128), row(128)],
        out_specs=[row(W), row(W), row(128)],
        out_shape=[jax.ShapeDtypeStruct((S, W), BF16), jax.ShapeDtypeStruct((S, W), BF16), jax.ShapeDtypeStruct((S, 128), F32)],
        compiler_params=pltpu.CompilerParams(dimension_semantics=("parallel",)),
    )(dq_att, dk_att, dv, cos_t, sin_t)


def _lat_bwd(dcqn, dckvn, dkpe, lat, gq, gkv, name, tr=256):
    S = lat.shape[0]

    def body(dcq_ref, dckv_ref, dkpe_ref, lat_ref, gq_ref, gkv_ref, dlat_ref, dgq_ref, dgkv_ref):
        dq, dgq = _rms_bwd_math(dcq_ref[...], lat_ref[:, :Q_LORA], gq_ref[...])
        dkv, dgkv = _rms_bwd_math(dckv_ref[...], lat_ref[:, Q_LORA : Q_LORA + KV_LORA], gkv_ref[...])
        dlat_ref[:, :Q_LORA] = dq.astype(BF16)
        dlat_ref[:, Q_LORA : Q_LORA + KV_LORA] = dkv.astype(BF16)
        dlat_ref[:, Q_LORA + KV_LORA :] = dkpe_ref[...].astype(BF16)

        @pl.when(pl.program_id(0) == 0)
        def _():
            dgq_ref[...] = dgq
            dgkv_ref[...] = dgkv

        @pl.when(pl.program_id(0) != 0)
        def _():
            dgq_ref[...] += dgq
            dgkv_ref[...] += dgkv

    def row(n):
        return pl.BlockSpec((tr, n), lambda i: (i, 0))

    def vec(n):
        return pl.BlockSpec((1, n), lambda i: (0, 0))

    return pl.pallas_call(
        body,
        name=name,
        grid=(S // tr,),
        in_specs=[row(Q_LORA), row(KV_LORA), row(128), row(LAT_PAD), vec(Q_LORA), vec(KV_LORA)],
        out_specs=[row(LAT_PAD), vec(Q_LORA), vec(KV_LORA)],
        out_shape=[
            jax.ShapeDtypeStruct((S, LAT_PAD), BF16),
            jax.ShapeDtypeStruct((1, Q_LORA), F32),
            jax.ShapeDtypeStruct((1, KV_LORA), F32),
        ],
        compiler_params=pltpu.CompilerParams(dimension_semantics=("arbitrary",)),
    )(dcqn, dckvn, dkpe, lat, gq, gkv)


MLA_SCALE = (QK_NOPE + QK_ROPE) ** -0.5
LOG2E = 1.4426950408889634
MLA_C2 = MLA_SCALE * LOG2E
FLASH_T = 1024


def _causal_pairs(n, by_key):
    pairs = [(i, j) for j in range(n) for i in range(j, n)] if by_key else [(i, j) for i in range(n) for j in range(i + 1)]
    return jnp.asarray([p[0] for p in pairs], jnp.int32), jnp.asarray([p[1] for p in pairs], jnp.int32)


def _lanes(x, n):
    return jnp.tile(x, (1, n // 128))


def _flash_grid(npairs, in_specs, out_specs, scratch):
    return pltpu.PrefetchScalarGridSpec(
        num_scalar_prefetch=2, grid=(MLA_HEADS, npairs), in_specs=in_specs, out_specs=out_specs, scratch_shapes=scratch
    )


def _flash2_fwd(q_att, k_att, kv, name, t=FLASH_T):
    S = q_att.shape[0]
    qi_tab, kj_tab = _causal_pairs(S // t, by_key=False)

    def body(qi_ref, kj_ref, q_ref, k_ref, v_ref, o_ref, lse_ref, m_sc, l_sc, acc_sc):
        step = pl.program_id(1)
        qi, kj = qi_ref[step], kj_ref[step]

        @pl.when(kj == 0)
        def _():
            m_sc[...] = jnp.full((t, 128), NEG, F32)
            l_sc[...] = jnp.zeros((t, 128), F32)
            acc_sc[...] = jnp.zeros((t, V_HEAD), F32)

        def update(s):
            m_prev = m_sc[...]
            m_new = jnp.maximum(m_prev, jnp.max(s, axis=1, keepdims=True))
            p = jnp.exp2((s - _lanes(m_new, t)) * MLA_C2)
            alpha = jnp.exp2((m_prev - m_new) * MLA_C2)
            l_sc[...] = alpha * l_sc[...] + jnp.sum(p, axis=1, keepdims=True)
            acc_sc[...] = alpha * acc_sc[...] + _dot(p.astype(BF16), v_ref[...])
            m_sc[...] = m_new

        @pl.when(kj < qi)
        def _():
            update(_dot(q_ref[...], k_ref[...], NT))

        @pl.when(kj == qi)
        def _():
            s = _dot(q_ref[...], k_ref[...], NT)
            rows = lax.broadcasted_iota(jnp.int32, s.shape, 0)
            cols = lax.broadcasted_iota(jnp.int32, s.shape, 1)
            update(jnp.where(cols <= rows, s, NEG))
            l = l_sc[...]
            o_ref[...] = acc_sc[...] / l
            lse_ref[0] = m_sc[...] * MLA_SCALE + jnp.log(l)

    return pl.pallas_call(
        body,
        name=name,
        grid_spec=_flash_grid(
            qi_tab.shape[0],
            [
                pl.BlockSpec((t, HEAD_PAD), lambda h, p, qi, kj: (qi[p], h)),
                pl.BlockSpec((t, HEAD_PAD), lambda h, p, qi, kj: (kj[p], h)),
                pl.BlockSpec((t, V_HEAD), lambda h, p, qi, kj: (kj[p], 2 * h + 1)),
            ],
            [
                pl.BlockSpec((t, V_HEAD), lambda h, p, qi, kj: (qi[p], h)),
                pl.BlockSpec((1, t, 128), lambda h, p, qi, kj: (h, qi[p], 0)),
            ],
            [pltpu.VMEM((t, 128), F32), pltpu.VMEM((t, 128), F32), pltpu.VMEM((t, V_HEAD), F32)],
        ),
        out_shape=[jax.ShapeDtypeStruct((S, MLA_HEADS * V_HEAD), F32), jax.ShapeDtypeStruct((MLA_HEADS, S, 128), F32)],
        compiler_params=pltpu.CompilerParams(dimension_semantics=("parallel", "arbitrary")),
    )(qi_tab, kj_tab, q_att, k_att, kv)


def _flash_delta(do, o, name, tr=512):
    S = o.shape[0]

    def body(do_ref, o_ref, d_ref):
        lane = lax.broadcasted_iota(jnp.int32, (tr, 128), 1)
        acc = jnp.zeros((tr, 128), F32)
        for h in range(MLA_HEADS):
            sl = slice(h * V_HEAD, (h + 1) * V_HEAD)
            acc = jnp.where(lane == h, jnp.sum(do_ref[:, sl].astype(F32) * o_ref[:, sl], axis=1, keepdims=True), acc)
        d_ref[...] = acc

    row = pl.BlockSpec((tr, MLA_HEADS * V_HEAD), lambda i: (i, 0))
    return pl.pallas_call(
        body,
        name=name,
        grid=(S // tr,),
        in_specs=[row, row],
        out_specs=pl.BlockSpec((tr, 128), lambda i: (i, 0)),
        out_shape=jax.ShapeDtypeStruct((S, 128), F32),
        compiler_params=pltpu.CompilerParams(dimension_semantics=("parallel",)),
    )(do, o)


def _flash2_bwd(q_att, k_att, kv, do, lse_row, delta_row, name, t=FLASH_T):
    S = q_att.shape[0]
    n = S // t
    qi_tab, kj_tab = _causal_pairs(n, by_key=True)
    last = qi_tab.shape[0] - 1

    def body(qi_ref, kj_ref, q_ref, k_ref, v_ref, do_ref, lse_ref, dl_ref, dq_ref, dk_ref, dv_ref, dk_sc, dv_sc):
        step = pl.program_id(1)
        qi, kj = qi_ref[step], kj_ref[step]

        @pl.when(step == 0)
        def _():
            dq_ref[...] = jnp.zeros((S, HEAD_PAD), F32)

        def update(st):
            q, do_v = q_ref[...], do_ref[...]
            pt = jnp.exp2(st * MLA_C2 - lse_ref[0] * LOG2E)
            dv_sc[...] += _dot(pt.astype(BF16), do_v)
            dpt = _dot(v_ref[...], do_v, NT)
            dst = (pt * (dpt - dl_ref[0])).astype(BF16)
            dk_sc[...] += _dot(dst, q)
            rows = pl.ds(pl.multiple_of(qi * t, t), t)
            dq_ref[rows, :] += _dot(dst, k_ref[...], TN)

        @pl.when(qi == kj)
        def _():
            dk_sc[...] = jnp.zeros((t, HEAD_PAD), F32)
            dv_sc[...] = jnp.zeros((t, V_HEAD), F32)
            st = _dot(k_ref[...], q_ref[...], NT)
            keys = lax.broadcasted_iota(jnp.int32, st.shape, 0)
            qs = lax.broadcasted_iota(jnp.int32, st.shape, 1)
            update(jnp.where(keys <= qs, st, NEG))

        @pl.when(qi > kj)
        def _():
            update(_dot(k_ref[...], q_ref[...], NT))

        @pl.when(qi == n - 1)
        def _():
            dk_ref[...] = dk_sc[...] * MLA_SCALE
            dv_ref[...] = dv_sc[...]

        @pl.when(step == last)
        def _():
            dq_ref[...] = dq_ref[...] * MLA_SCALE

    qrow = lambda h, p, qi, kj: (qi[p], h)
    krow = lambda h, p, qi, kj: (kj[p], h)
    stat = pl.BlockSpec((1, 1, t), lambda h, p, qi, kj: (h, 0, qi[p]))
    return pl.pallas_call(
        body,
        name=name,
        grid_spec=_flash_grid(
            qi_tab.shape[0],
            [
                pl.BlockSpec((t, HEAD_PAD), qrow),
                pl.BlockSpec((t, HEAD_PAD), krow),
                pl.BlockSpec((t, V_HEAD), lambda h, p, qi, kj: (kj[p], 2 * h + 1)),
                pl.BlockSpec((t, V_HEAD), qrow),
                stat,
                stat,
            ],
            [
                pl.BlockSpec((S, HEAD_PAD), lambda h, p, qi, kj: (0, h)),
                pl.BlockSpec((t, HEAD_PAD), krow),
                pl.BlockSpec((t, V_HEAD), krow),
            ],
            [pltpu.VMEM((t, HEAD_PAD), F32), pltpu.VMEM((t, V_HEAD), F32)],
        ),
        out_shape=[
            jax.ShapeDtypeStruct((S, MLA_HEADS * HEAD_PAD), F32),
            jax.ShapeDtypeStruct((S, MLA_HEADS * HEAD_PAD), F32),
            jax.ShapeDtypeStruct((S, MLA_HEADS * V_HEAD), F32),
        ],
        compiler_params=pltpu.CompilerParams(dimension_semantics=("parallel", "arbitrary")),
    )(qi_tab, kj_tab, q_att, k_att, kv, do, lse_row, delta_row)


DIL_SCALE = DIL_HD**-0.5


def _dil_bias():
    slopes = 2.0 ** (-ALIBI_MAX_BIAS * np.arange(1, DIL_HEADS + 1, dtype=np.float64) / DIL_HEADS)
    slopes = slopes.astype(np.float32).reshape(DIL_GROUPS, DIL_HG)
    p = np.arange(DIL_BLK)[:, None]
    kidx = np.arange(2 * DIL_BLK)[None, :]
    j = p + DIL_BLK - kidx
    out = np.zeros((DIL_GROUPS, DIL_HG, DIL_BLK, 2 * DIL_BLK), np.float32)
    for g, (window, dil) in enumerate(DIL_PATTERNS):
        valid = (j >= 0) & (j <= window // dil)
        for h in range(DIL_HG):
            alibi = -slopes[g, h] * (dil * j).astype(np.float32)
            out[g, h] = np.where(valid, alibi, np.float32(NEG))
    return jnp.asarray(out)


DIL_UNROLL = 4


def _unrolled_loop(lo, hi, fn, unroll=DIL_UNROLL):
    groups = (hi - lo) // unroll
    done = lo
    if groups > 1:

        def step(i, carry):
            for u in range(unroll):
                fn(lo + i * unroll + u)
            return carry

        lax.fori_loop(0, groups, step, 0)
        done = lo + groups * unroll
    for n in range(done, hi):
        fn(n)


def _dil_rows(r, n, count, dil):
    if dil == 1:
        if isinstance(n, int):
            return slice(n * DIL_BLK, (n + count) * DIL_BLK)
        return pl.ds(pl.multiple_of(n * DIL_BLK, DIL_BLK), count * DIL_BLK)
    return pl.ds(n * DIL_BLK * dil + r, count * DIL_BLK, stride=dil)


def _dil_each_block(S, dil, block):
    nb = S // dil // DIL_BLK
    if dil == 1:
        block(0, 0, True)
        _unrolled_loop(1, nb, lambda n: block(0, n, False))
    else:
        for r in range(dil):
            for n in range(nb):
                block(r, n, n == 0)


def _dil_col(g, part, h):
    return (g * 3 + part) * DIL_HG + h


def _dil_fwd_group(dqkv, bias_g, g, dil, name):
    S = dqkv.shape[0]

    def body(bias_ref, q_ref, k_ref, v_ref, o_ref, lse_ref):
        def block(r, n, first):
            cur = _dil_rows(r, n, 1, dil)
            both = cur if first else _dil_rows(r, n - 1, 2, dil)
            b = bias_ref[0][:, DIL_BLK:] if first else bias_ref[0]
            q, kk, vv = q_ref[cur, :].astype(BF16), k_ref[both, :].astype(BF16), v_ref[both, :].astype(BF16)
            s = _dot(q, kk, NT) * DIL_SCALE + b
            m = jnp.max(s, axis=1, keepdims=True)
            e = jnp.exp(s - m)
            l = jnp.sum(e, axis=1, keepdims=True)
            p = e * (1.0 / l)
            o_ref[cur, :] = _dot(p.astype(BF16), vv)
            lse_ref[cur, :] = jnp.broadcast_to(m + jnp.log(l), (DIL_BLK, 128))

        _dil_each_block(S, dil, block)

    def col(part):
        return pl.BlockSpec((S, DIL_HD), lambda h: (0, _dil_col(g, part, h)))

    out = pl.BlockSpec((S, DIL_HD), lambda h: (0, h))
    return pl.pallas_call(
        body,
        name=name,
        grid=(DIL_HG,),
        in_specs=[pl.BlockSpec((1, DIL_BLK, 2 * DIL_BLK), lambda h: (h, 0, 0)), col(0), col(1), col(2)],
        out_specs=[out, out],
        out_shape=[jax.ShapeDtypeStruct((S, DIL_OUT), F32), jax.ShapeDtypeStruct((S, DIL_OUT), F32)],
        compiler_params=pltpu.CompilerParams(dimension_semantics=("parallel",)),
    )(bias_g, dqkv, dqkv, dqkv)


def _dil_combine(os_, ls_, name, tr=512):
    S = os_[0].shape[0]

    def body(o0, o1, o2, l0, l1, l2, out_ref, lse_ref):
        a, b, c = l0[...], l1[...], l2[...]
        m = jnp.maximum(jnp.maximum(a, b), c)
        ea, eb, ec = jnp.exp(a - m), jnp.exp(b - m), jnp.exp(c - m)
        den = ea + eb + ec
        inv = 1.0 / den
        out_ref[...] = (ea * inv) * o0[...] + (eb * inv) * o1[...] + (ec * inv) * o2[...]
        lse_ref[...] = m + jnp.log(den)

    row = pl.BlockSpec((tr, DIL_OUT), lambda i: (i, 0))
    return pl.pallas_call(
        body,
        name=name,
        grid=(S // tr,),
        in_specs=[row] * 6,
        out_specs=[row, row],
        out_shape=[jax.ShapeDtypeStruct((S, DIL_OUT), F32)] * 2,
        compiler_params=pltpu.CompilerParams(dimension_semantics=("parallel",)),
    )(*os_, *ls_)


def _dil_rowdot(dod, od, name, tr=512):
    S = dod.shape[0]

    def body(d_ref, o_ref, dd_ref):
        for h in range(DIL_HG):
            sl = slice(h * 128, (h + 1) * 128)
            sm = jnp.sum(d_ref[:, sl] * o_ref[:, sl], axis=1, keepdims=True)
            dd_ref[:, sl] = jnp.broadcast_to(sm, (tr, 128))

    row = pl.BlockSpec((tr, DIL_OUT), lambda i: (i, 0))
    return pl.pallas_call(
        body,
        name=name,
        grid=(S // tr,),
        in_specs=[row, row],
        out_specs=row,
        out_shape=jax.ShapeDtypeStruct((S, DIL_OUT), F32),
        compiler_params=pltpu.CompilerParams(dimension_semantics=("parallel",)),
    )(dod, od)


def _dil_bwd_group(dqkv, bias_g, dod, dd, lse, grads, g, dil, name):
    S = dqkv.shape[0]

    def body(bias_ref, q_ref, k_ref, v_ref, do_ref, dd_ref, lse_ref, _, out_ref):
        out_ref[1] = jnp.zeros((S, DIL_HD), F32)
        out_ref[2] = jnp.zeros((S, DIL_HD), F32)

        def block(r, n, first):
            cur = _dil_rows(r, n, 1, dil)
            both = cur if first else _dil_rows(r, n - 1, 2, dil)
            b = bias_ref[0][:, DIL_BLK:] if first else bias_ref[0]
            q, kk, vv = q_ref[cur, :].astype(BF16), k_ref[both, :].astype(BF16), v_ref[both, :].astype(BF16)
            do = do_ref[cur, :].astype(BF16)
            s = _dot(q, kk, NT) * DIL_SCALE + b
            p = jnp.exp(s - lse_ref[cur, 0:1])
            dp = _dot(do, vv, NT)
            ds = ((p * (dp - dd_ref[cur, 0:1])) * DIL_SCALE).astype(BF16)
            out_ref[0, cur, :] = _dot(ds, kk)
            out_ref[1, both, :] += _dot(ds, q, TN)
            out_ref[2, both, :] += _dot(p.astype(BF16), do, TN)

        _dil_each_block(S, dil, block)

    def col(part):
        return pl.BlockSpec((S, DIL_HD), lambda h: (0, _dil_col(g, part, h)))

    nat = pl.BlockSpec((S, DIL_HD), lambda h: (0, h))
    return pl.pallas_call(
        body,
        name=name,
        grid=(DIL_HG,),
        in_specs=[pl.BlockSpec((1, DIL_BLK, 2 * DIL_BLK), lambda h: (h, 0, 0)), col(0), col(1), col(2), nat, nat, nat, ANY],
        out_specs=pl.BlockSpec((3, S, DIL_HD), lambda h: (g, 0, h)),
        out_shape=jax.ShapeDtypeStruct(grads.shape, F32),
        input_output_aliases={7: 0},
        compiler_params=pltpu.CompilerParams(dimension_semantics=("parallel",)),
    )(bias_g, dqkv, dqkv, dqkv, dod, dd, lse, grads)


def _merge_fwd(gates, o_a, o_b, name, tr=256):
    S = o_a.shape[0]

    def body(ga_ref, gb_ref, oa_ref, ob_ref, m_ref):
        m_ref[...] = (ga_ref[...] * oa_ref[...] + gb_ref[...] * ob_ref[...]).astype(BF16)

    row = pl.BlockSpec((tr, D_MODEL), lambda i: (i, 0))
    return pl.pallas_call(
        body,
        name=name,
        grid=(S // tr,),
        in_specs=[row, pl.BlockSpec((tr, D_MODEL), lambda i: (i, 1)), row, row],
        out_specs=row,
        out_shape=jax.ShapeDtypeStruct((S, D_MODEL), BF16),
        compiler_params=pltpu.CompilerParams(dimension_semantics=("parallel",)),
    )(gates, gates, o_a, o_b)


def _merge_bwd(dmrg, gates, o_a, o_b, name, tr=256):
    S = o_a.shape[0]

    def body(dm_ref, ga_ref, gb_ref, oa_ref, ob_ref, doa_ref, dob_ref, dga_ref, dgb_ref, dba_ref, dbb_ref):
        dm, ga, gb = dm_ref[...], ga_ref[...], gb_ref[...]
        doa_ref[...] = (dm * ga).astype(BF16)
        dob_ref[...] = (dm * gb).astype(BF16)
        dga = (dm * oa_ref[...]) * (ga * (1.0 - ga))
        dgb = (dm * ob_ref[...]) * (gb * (1.0 - gb))
        dga_ref[...] = dga.astype(BF16)
        dgb_ref[...] = dgb.astype(BF16)
        sa = jnp.sum(dga, axis=0, keepdims=True)
        sb = jnp.sum(dgb, axis=0, keepdims=True)

        @pl.when(pl.program_id(0) == 0)
        def _():
            dba_ref[...] = sa
            dbb_ref[...] = sb

        @pl.when(pl.program_id(0) != 0)
        def _():
            dba_ref[...] += sa
            dbb_ref[...] += sb

    row = pl.BlockSpec((tr, D_MODEL), lambda i: (i, 0))
    row1 = pl.BlockSpec((tr, D_MODEL), lambda i: (i, 1))
    vec = pl.BlockSpec((1, D_MODEL), lambda i: (0, 0))
    outs = pl.pallas_call(
        body,
        name=name,
        grid=(S // tr,),
        in_specs=[row, row, row1, row, row],
        out_specs=[row, row, row, row, vec, vec],
        out_shape=[jax.ShapeDtypeStruct((S, D_MODEL), BF16)] * 4 + [jax.ShapeDtypeStruct((1, D_MODEL), F32)] * 2,
        compiler_params=pltpu.CompilerParams(dimension_semantics=("arbitrary",)),
    )(dmrg, gates, gates, o_a, o_b)
    return outs


CONV_TR = 512
CONV_TC = 512
N_FFC = D_FF_PAD // CONV_TC


def _conv_taps(x, before, w_ref, b_ref):
    x0 = jnp.concatenate([before, x], axis=0)
    x1 = pltpu.roll(x0, 1, 0)
    x2 = pltpu.roll(x0, 2, 0)
    u = ((b_ref[...] + w_ref[0:1, :] * x2) + w_ref[1:2, :] * x1) + w_ref[2:3, :] * x0
    return u, x0, x1, x2


def _prev_halo(tr):
    return lambda i, j: (jnp.maximum(i * (tr // 8) - 1, 0), j)


def _ffn_fwd(u0, cw, cb, name):
    S = u0.shape[0]
    tr, tc = CONV_TR, CONV_TC

    def body(up_ref, gt_ref, hup_ref, hgt_ref, wu_ref, wg_ref, bu_ref, bg_ref, a_ref):
        live = (pl.program_id(0) > 0).astype(F32)
        up = _conv_taps(up_ref[...], hup_ref[...] * live, wu_ref, bu_ref)[0][8:]
        gt = _conv_taps(gt_ref[...], hgt_ref[...] * live, wg_ref, bg_ref)[0][8:]
        a_ref[...] = ((gt * jax.nn.sigmoid(gt)) * up).astype(BF16)

    return pl.pallas_call(
        body,
        name=name,
        grid=(S // tr, N_FFC),
        in_specs=[
            pl.BlockSpec((tr, tc), lambda i, j: (i, j)),
            pl.BlockSpec((tr, tc), lambda i, j: (i, j + N_FFC)),
            pl.BlockSpec((8, tc), _prev_halo(tr)),
            pl.BlockSpec((8, tc), lambda i, j: (jnp.maximum(i * (tr // 8) - 1, 0), j + N_FFC)),
            pl.BlockSpec((8, tc), lambda i, j: (0, j)),
            pl.BlockSpec((8, tc), lambda i, j: (0, j + N_FFC)),
            pl.BlockSpec((1, tc), lambda i, j: (0, j)),
            pl.BlockSpec((1, tc), lambda i, j: (0, j + N_FFC)),
        ],
        out_specs=pl.BlockSpec((tr, tc), lambda i, j: (i, j)),
        out_shape=jax.ShapeDtypeStruct((S, D_FF_PAD), BF16),
        compiler_params=pltpu.CompilerParams(dimension_semantics=("parallel", "parallel")),
    )(u0, u0, u0, u0, cw, cw, cb, cb)


def _ffn_bwd(u0, da, cw, cb, name):
    S = u0.shape[0]
    tr, tc = CONV_TR, CONV_TC
    nrow, te = S // tr, tr + 8

    def body(up_ref, gt_ref, hup_ref, hgt_ref, nup_ref, ngt_ref, da_ref, nda_ref, wu_ref, wg_ref, bu_ref, bg_ref, du0_ref, dcw_ref, dcb_ref):
        i = pl.program_id(1)
        prev_live = (i > 0).astype(F32)
        next_live = (i < nrow - 1).astype(F32)

        def conv(x_ref, nx_ref, h_ref, w_ref, b_ref):
            x = jnp.concatenate([x_ref[...], nx_ref[...] * next_live], axis=0)
            return [t[8:] for t in _conv_taps(x, h_ref[...] * prev_live, w_ref, b_ref)]

        up, xu0, xu1, xu2 = conv(up_ref, nup_ref, hup_ref, wu_ref, bu_ref)
        gt, xg0, xg1, xg2 = conv(gt_ref, ngt_ref, hgt_ref, wg_ref, bg_ref)
        da_v = jnp.concatenate([da_ref[...], nda_ref[...] * next_live], axis=0)
        sg = jax.nn.sigmoid(gt)
        d_up = da_v * (gt * sg)
        d_gt = (da_v * up) * (sg * (1.0 + gt * (1.0 - sg)))
        tap = lax.broadcasted_iota(jnp.int32, (8, tc), 0)

        def finish(half, du, x0, x1, x2, w_ref):
            n1 = pltpu.roll(du, te - 1, 0)
            n2 = pltpu.roll(du, te - 2, 0)
            du0 = (w_ref[2:3, :] * du + w_ref[1:2, :] * n1) + w_ref[0:1, :] * n2
            du0_ref[half] = du0[:tr].astype(BF16)
            d = du[:tr]
            dcw = jnp.where(
                tap == 0,
                jnp.sum(d * x2[:tr], axis=0, keepdims=True),
                jnp.where(tap == 1, jnp.sum(d * x1[:tr], axis=0, keepdims=True), jnp.where(tap == 2, jnp.sum(d * x0[:tr], axis=0, keepdims=True), 0.0)),
            )
            dcb = jnp.sum(d, axis=0, keepdims=True)

            @pl.when(i == 0)
            def _():
                dcw_ref[half] = dcw
                dcb_ref[half] = dcb

            @pl.when(i != 0)
            def _():
                dcw_ref[half] += dcw
                dcb_ref[half] += dcb

        finish(0, d_up, xu0, xu1, xu2, wu_ref)
        finish(1, d_gt, xg0, xg1, xg2, wg_ref)

    def prev8(off):
        return pl.BlockSpec((8, tc), lambda j, i: (jnp.maximum(i * (tr // 8) - 1, 0), j + off))

    def next8(off):
        return pl.BlockSpec((8, tc), lambda j, i: (jnp.minimum((i + 1) * (tr // 8), S // 8 - 1), j + off))

    return pl.pallas_call(
        body,
        name=name,
        grid=(N_FFC, nrow),
        in_specs=[
            pl.BlockSpec((tr, tc), lambda j, i: (i, j)),
            pl.BlockSpec((tr, tc), lambda j, i: (i, j + N_FFC)),
            prev8(0),
            prev8(N_FFC),
            next8(0),
            next8(N_FFC),
            pl.BlockSpec((tr, tc), lambda j, i: (i, j)),
            next8(0),
            pl.BlockSpec((8, tc), lambda j, i: (0, j)),
            pl.BlockSpec((8, tc), lambda j, i: (0, j + N_FFC)),
            pl.BlockSpec((1, tc), lambda j, i: (0, j)),
            pl.BlockSpec((1, tc), lambda j, i: (0, j + N_FFC)),
        ],
        out_specs=[
            pl.BlockSpec((2, tr, tc), lambda j, i: (0, i, j)),
            pl.BlockSpec((2, 8, tc), lambda j, i: (0, 0, j)),
            pl.BlockSpec((2, 1, tc), lambda j, i: (0, 0, j)),
        ],
        out_shape=[
            jax.ShapeDtypeStruct((2, S, D_FF_PAD), BF16),
            jax.ShapeDtypeStruct((2, 8, D_FF_PAD), F32),
            jax.ShapeDtypeStruct((2, 1, D_FF_PAD), F32),
        ],
        compiler_params=pltpu.CompilerParams(dimension_semantics=("parallel", "arbitrary")),
    )(u0, u0, u0, u0, u0, u0, da, da, cw, cw, cb, cb)


ADAMW_BLOCK_BYTES = 3 << 20


def _adamw(w, g, m, v, name):
    R, C = w.shape
    fits = [t for t in range(8, R + 1, 8) if R % t == 0 and t * C * 4 <= ADAMW_BLOCK_BYTES]
    tr = max(fits) if fits else R

    def body(w_ref, g_ref, m_ref, v_ref, d_ref, nm_ref, nv_ref):
        gv = g_ref[...]
        nm = ADAM_B1 * m_ref[...] + (1.0 - ADAM_B1) * gv
        nv = ADAM_B2 * v_ref[...] + (1.0 - ADAM_B2) * (gv * gv)
        m_hat = nm / (1.0 - ADAM_B1**ADAM_STEP)
        v_hat = nv / (1.0 - ADAM_B2**ADAM_STEP)
        d_ref[...] = -ADAM_LR * (m_hat / (jnp.sqrt(v_hat) + ADAM_EPS) + ADAM_WD * w_ref[...])
        nm_ref[...] = nm
        nv_ref[...] = nv

    blk = pl.BlockSpec((tr, C), lambda i: (i, 0))
    return pl.pallas_call(
        body,
        name=name,
        grid=(R // tr,),
        in_specs=[blk] * 4,
        out_specs=[blk] * 3,
        out_shape=[jax.ShapeDtypeStruct((R, C), F32)] * 3,
        compiler_params=pltpu.CompilerParams(dimension_semantics=("parallel",)),
    )(w, g, m, v)


ANY = pl.BlockSpec(memory_space=pl.ANY)


def _pair_exchange(gs, name):
    n = len(gs)

    def body(*refs):
        g_refs, out_refs = refs[:n], refs[n : 2 * n]
        send_sems, recv_sems = refs[2 * n :]
        x, y, c = lax.axis_index("x"), lax.axis_index("y"), lax.axis_index("c")
        copies = [
            pltpu.make_async_remote_copy(
                src_ref=g_refs[a].at[2 * k + (1 - c)],
                dst_ref=out_refs[a].at[k],
                send_sem=send_sems.at[N_CHIP * a + k],
                recv_sem=recv_sems.at[N_CHIP * a + k],
                device_id=(x, y, 1 - c),
                device_id_type=MESH,
            )
            for a in range(n)
            for k in range(N_CHIP)
        ]
        for cp in copies:
            cp.start()
        for cp in copies:
            cp.wait()

    return pl.pallas_call(
        body,
        name=name,
        out_shape=[jax.ShapeDtypeStruct((N_CHIP,) + g.shape[1:], g.dtype) for g in gs],
        in_specs=[ANY] * n,
        out_specs=[ANY] * n,
        scratch_shapes=[pltpu.SemaphoreType.DMA((N_CHIP * n,)), pltpu.SemaphoreType.DMA((N_CHIP * n,))],
    )(*gs)


def _row_tile(rows):
    return max(t for t in range(16, 353, 16) if rows % t == 0)


def _pair_add(g, recv, core, name):
    _, R, C = g.shape
    tr = _row_tile(R)

    def body(core_ref, g_ref, r_ref, o_ref):
        o_ref[...] = (g_ref[...].astype(F32) + r_ref[...].astype(F32)).astype(o_ref.dtype)

    return pl.pallas_call(
        body,
        name=name,
        grid_spec=pltpu.PrefetchScalarGridSpec(
            num_scalar_prefetch=1,
            grid=(N_CHIP, R // tr),
            in_specs=[
                pl.BlockSpec((1, tr, C), lambda k, i, core_ref: (2 * k + core_ref[0], i, 0)),
                pl.BlockSpec((1, tr, C), lambda k, i, core_ref: (k, i, 0)),
            ],
            out_specs=pl.BlockSpec((1, tr, C), lambda k, i, core_ref: (k, i, 0)),
        ),
        out_shape=jax.ShapeDtypeStruct((N_CHIP, R, C), g.dtype),
        compiler_params=pltpu.CompilerParams(dimension_semantics=("parallel", "parallel")),
    )(core, g, recv)


HBM = pl.BlockSpec(memory_space=pltpu.HBM)
SEM = pl.BlockSpec(memory_space=pltpu.SEMAPHORE)
EFFECT = pltpu.SideEffectType.DATAFLOW_SIDE_EFFECTING
RELATIONS = tuple((dx, dy, dc) for dx in (0, 1) for dy in (0, 1) for dc in (0, 1))[1:]


def _related(rel):
    x, y, c = lax.axis_index("x"), lax.axis_index("y"), lax.axis_index("c")
    return (1 - x if rel[0] else x, 1 - y if rel[1] else y, 1 - c if rel[2] else c)


def _dev_index(pos):
    return 4 * pos[0] + 2 * pos[1] + pos[2]


def _peers(chips):
    if chips:
        return [r for r in RELATIONS if not r[2]], N_CHIP, lambda pos: 2 * pos[0] + pos[1]
    return list(RELATIONS), N_DEV, _dev_index


def _exchange_start(srcs, by_slot, after, name, chips=False):
    n = len(srcs)
    rels, slots, slot_of = _peers(chips)
    lands = [lax.empty((slots,) + (s.shape[1:] if by_slot else s.shape), s.dtype) for s in srcs]
    nsem = len(rels) * n

    def body(*refs):
        src_refs, land_refs = refs[:n], refs[n : 2 * n]
        send_sems, recv_sems = refs[2 * n + 1], refs[2 * n + 2]
        token = refs[-1]
        me = slot_of(_related((0, 0, 0)))
        for a in range(n):
            for k, rel in enumerate(rels):
                peer = _related(rel)
                pltpu.make_async_remote_copy(
                    src_ref=src_refs[a].at[slot_of(peer)] if by_slot else src_refs[a],
                    dst_ref=land_refs[a].at[me],
                    send_sem=send_sems.at[len(rels) * a + k],
                    recv_sem=recv_sems.at[len(rels) * a + k],
                    device_id=peer,
                    device_id_type=MESH,
                ).start()
        token[...] = jnp.zeros_like(token)

    def hbm(a):
        return pltpu.HBM(a.shape, a.dtype)

    outs = pl.pallas_call(
        body,
        name=name,
        out_shape=(
            pltpu.SemaphoreType.DMA((nsem,)),
            pltpu.SemaphoreType.DMA((nsem,)),
            *[hbm(s) for s in srcs],
            *[hbm(l) for l in lands],
            jax.ShapeDtypeStruct((8, 128), F32),
        ),
        in_specs=[HBM] * (2 * n) + [ANY],
        out_specs=(SEM, SEM, *[HBM] * (2 * n), pl.BlockSpec(memory_space=pltpu.VMEM)),
        input_output_aliases={i: 2 + i for i in range(2 * n)},
        compiler_params=pltpu.CompilerParams(has_side_effects=EFFECT),
    )(*[pltpu.with_memory_space_constraint(a, pltpu.HBM) for a in list(srcs) + lands], after)
    return (outs[0], outs[1], list(outs[2 : 2 + n]), list(outs[2 + n : 2 + 2 * n])), outs[-1]


def _exchange_wait(handle, by_slot, after, name, chips=False):
    send_sems, recv_sems, srcs, lands = handle
    n = len(srcs)
    rels = _peers(chips)[0]

    def body(*refs):
        src_refs, land_refs = refs[:n], refs[n : 2 * n]
        s_sems, r_sems = refs[2 * n], refs[2 * n + 1]
        for a in range(n):
            for k, rel in enumerate(rels):
                copy = pltpu.make_async_remote_copy(
                    src_ref=src_refs[a].at[0] if by_slot else src_refs[a],
                    dst_ref=land_refs[a].at[0],
                    send_sem=s_sems.at[len(rels) * a + k],
                    recv_sem=r_sems.at[len(rels) * a + k],
                    device_id=_related(rel),
                    device_id_type=MESH,
                )
                copy.wait_send()
                copy.wait_recv()

    outs = pl.pallas_call(
        body,
        name=name,
        out_shape=tuple(pltpu.HBM(a.shape, a.dtype) for a in srcs + lands),
        in_specs=[HBM] * (2 * n) + [SEM, SEM, ANY],
        out_specs=tuple([HBM] * (2 * n)),
        input_output_aliases={i: i for i in range(2 * n)},
        compiler_params=pltpu.CompilerParams(has_side_effects=EFFECT),
    )(*srcs, *lands, send_sems, recv_sems, after)
    return list(outs[:n]), list(outs[n:])


NEAR = ((0, 0, 1), (1, 0, 0), (0, 1, 0), (1, 1, 0))


def _gather2_start(blocks, name):
    n = len(blocks)
    lands = [lax.empty((N_DEV,) + b.shape, b.dtype) for b in blocks]

    def body(*refs):
        src_refs, land_refs = refs[:n], refs[n : 2 * n]
        send_sems, recv_sems, token = refs[2 * n], refs[2 * n + 1], refs[-1]
        me = _dev_index(_related((0, 0, 0)))
        for a in range(n):
            for k, rel in enumerate(NEAR):
                pltpu.make_async_remote_copy(
                    src_ref=src_refs[a],
                    dst_ref=land_refs[a].at[me],
                    send_sem=send_sems.at[len(NEAR) * a + k],
                    recv_sem=recv_sems.at[len(NEAR) * a + k],
                    device_id=_related(rel),
                    device_id_type=MESH,
                ).start()
        token[...] = jnp.zeros_like(token)

    nsem = len(NEAR) * n
    outs = pl.pallas_call(
        body,
        name=name,
        out_shape=(
            pltpu.SemaphoreType.DMA((nsem,)),
            pltpu.SemaphoreType.DMA((nsem,)),
            *[pltpu.HBM(a.shape, a.dtype) for a in list(blocks) + lands],
            jax.ShapeDtypeStruct((8, 128), F32),
        ),
        in_specs=[HBM] * (2 * n),
        out_specs=(SEM, SEM, *[HBM] * (2 * n), pl.BlockSpec(memory_space=pltpu.VMEM)),
        input_output_aliases={i: 2 + i for i in range(2 * n)},
        compiler_params=pltpu.CompilerParams(has_side_effects=EFFECT),
    )(*[pltpu.with_memory_space_constraint(a, pltpu.HBM) for a in list(blocks) + lands])
    return (outs[0], outs[1], list(outs[2 : 2 + n]), list(outs[2 + n : 2 + 2 * n])), outs[-1]


def _gather2_forward(handle, after, name):
    send1, recv1, srcs, lands = handle
    n = len(srcs)

    def body(*refs):
        src_refs, land_refs = refs[:n], refs[n : 2 * n]
        s1, r1 = refs[2 * n], refs[2 * n + 1]
        s2, r2 = refs[-2], refs[-1]
        sibling = _related(NEAR[0])
        for a in range(n):
            for k, rel in enumerate(NEAR):
                first = pltpu.make_async_remote_copy(
                    src_ref=src_refs[a],
                    dst_ref=land_refs[a].at[0],
                    send_sem=s1.at[len(NEAR) * a + k],
                    recv_sem=r1.at[len(NEAR) * a + k],
                    device_id=_related(rel),
                    device_id_type=MESH,
                )
                first.wait_send()
                first.wait_recv()
                if k:
                    slot = land_refs[a].at[_dev_index(_related(rel))]
                    pltpu.make_async_remote_copy(
                        src_ref=slot,
                        dst_ref=slot,
                        send_sem=s2.at[3 * a + k - 1],
                        recv_sem=r2.at[3 * a + k - 1],
                        device_id=sibling,
                        device_id_type=MESH,
                    ).start()

    outs = pl.pallas_call(
        body,
        name=name,
        out_shape=(
            *[pltpu.HBM(a.shape, a.dtype) for a in srcs + lands],
            pltpu.SemaphoreType.DMA((3 * n,)),
            pltpu.SemaphoreType.DMA((3 * n,)),
        ),
        in_specs=[HBM] * (2 * n) + [SEM, SEM, ANY],
        out_specs=(*[HBM] * (2 * n), SEM, SEM),
        input_output_aliases={i: i for i in range(2 * n)},
        compiler_params=pltpu.CompilerParams(has_side_effects=EFFECT),
    )(*srcs, *lands, send1, recv1, after)
    return outs[-2], outs[-1], list(outs[:n]), list(outs[n : 2 * n])


def _gather2_wait(handle, name):
    send2, recv2, srcs, lands = handle
    n = len(srcs)

    def body(*refs):
        land_refs = refs[n : 2 * n]
        s2, r2 = refs[2 * n], refs[2 * n + 1]
        for a in range(n):
            for j in range(3):
                passed = pltpu.make_async_remote_copy(
                    src_ref=land_refs[a].at[0],
                    dst_ref=land_refs[a].at[0],
                    send_sem=s2.at[3 * a + j],
                    recv_sem=r2.at[3 * a + j],
                    device_id=_related(NEAR[0]),
                    device_id_type=MESH,
                )
                passed.wait_send()
                passed.wait_recv()

    outs = pl.pallas_call(
        body,
        name=name,
        out_shape=tuple(pltpu.HBM(a.shape, a.dtype) for a in srcs + lands),
        in_specs=[HBM] * (2 * n) + [SEM, SEM],
        out_specs=tuple([HBM] * (2 * n)),
        input_output_aliases={i: i for i in range(2 * n)},
        compiler_params=pltpu.CompilerParams(has_side_effects=EFFECT),
    )(*srcs, *lands, send2, recv2)
    return list(outs[:n]), list(outs[n:])


def _slot_sum(parts, name):
    n, R, C = parts.shape
    tr = _row_tile(R) if R % 16 == 0 else R

    def body(p_ref, o_ref):
        acc = p_ref[0].astype(F32)
        for k in range(1, n):
            acc = acc + p_ref[k].astype(F32)
        o_ref[...] = acc

    return pl.pallas_call(
        body,
        name=name,
        grid=(R // tr,),
        in_specs=[pl.BlockSpec((n, tr, C), lambda i: (0, i, 0))],
        out_specs=pl.BlockSpec((tr, C), lambda i: (i, 0)),
        out_shape=jax.ShapeDtypeStruct((R, C), F32),
        compiler_params=pltpu.CompilerParams(dimension_semantics=("parallel",)),
    )(parts)


W_IN_TC = 256
W_IN_BOUNDS = (0, LAT, LAT + 3 * DIL_QKV, LAT + 3 * DIL_QKV + D_MODEL, D_IN)


def _dqkv_chunks():
    return [((g * 3 + part) * DIL_OUT, LAT + part * DIL_QKV + g * DIL_OUT) for g in range(DIL_GROUPS) for part in range(3)]


def _w_in_regroup(slots, after, name):
    tc = W_IN_TC

    def body(s_ref, _, lat_ref, dqkv_ref, g_ref, buf):
        for j in range(N_DEV):
            buf[j * IN_ROWS : (j + 1) * IN_ROWS, :] = s_ref[j].astype(F32)[:IN_ROWS, :]
        lat_ref[:LAT, :] = buf[:LAT, :].astype(BF16)
        lat_ref[LAT:, :] = jnp.zeros((LAT_PAD - LAT, tc), BF16)
        for dst, src in _dqkv_chunks():
            dqkv_ref[dst : dst + DIL_OUT, :] = buf[src : src + DIL_OUT, :].astype(BF16)
        g_ref[...] = buf[W_IN_BOUNDS[2] :, :].astype(BF16)

    def col(rows):
        return pl.BlockSpec((rows, tc), lambda k: (0, k))

    return pl.pallas_call(
        body,
        name=name,
        grid=(D_MODEL // tc,),
        in_specs=[pl.BlockSpec((N_DEV, IN_ROWS_PAD, tc), lambda k: (0, 0, k)), pl.BlockSpec((8, 128), lambda k: (0, 0))],
        out_specs=[col(LAT_PAD), col(3 * DIL_QKV), col(2 * D_MODEL)],
        out_shape=[
            jax.ShapeDtypeStruct((LAT_PAD, D_MODEL), BF16),
            jax.ShapeDtypeStruct((3 * DIL_QKV, D_MODEL), BF16),
            jax.ShapeDtypeStruct((2 * D_MODEL, D_MODEL), BF16),
        ],
        scratch_shapes=[pltpu.VMEM((D_IN, tc), F32)],
        compiler_params=pltpu.CompilerParams(dimension_semantics=("parallel",)),
    )(slots, after)


def _w_in_grad_regroup(g_lat, g_dqkv, g_ga, g_gb, name):
    tc = W_IN_TC

    def body(lat_ref, dqkv_ref, ga_ref, gb_ref, o_ref, buf):
        b = W_IN_BOUNDS
        buf[b[0] : b[1], :] = lat_ref[:LAT, :].astype(F32)
        for dst, src in _dqkv_chunks():
            buf[src : src + DIL_OUT, :] = dqkv_ref[dst : dst + DIL_OUT, :].astype(F32)
        buf[b[2] : b[3], :] = ga_ref[...].astype(F32)
        buf[b[3] : b[4], :] = gb_ref[...].astype(F32)
        fill = jnp.zeros((IN_ROWS_PAD - IN_ROWS, tc), F32)
        for j in range(N_DEV):
            o_ref[j] = jnp.concatenate([buf[j * IN_ROWS : (j + 1) * IN_ROWS, :], fill], axis=0).astype(BF16)

    def col(rows):
        return pl.BlockSpec((rows, tc), lambda k: (0, k))

    return pl.pallas_call(
        body,
        name=name,
        grid=(D_MODEL // tc,),
        in_specs=[col(LAT_PAD), col(3 * DIL_QKV), col(D_MODEL), col(D_MODEL)],
        out_specs=pl.BlockSpec((N_DEV, IN_ROWS_PAD, tc), lambda k: (0, 0, k)),
        out_shape=jax.ShapeDtypeStruct((N_DEV, IN_ROWS_PAD, D_MODEL), BF16),
        scratch_shapes=[pltpu.VMEM((D_IN, tc), F32)],
        compiler_params=pltpu.CompilerParams(dimension_semantics=("parallel",)),
    )(g_lat, g_dqkv, g_ga, g_gb)


def _ffn_pad(a, axis):
    a = jnp.moveaxis(a, axis, -1)
    g = a.reshape(a.shape[:-1] + (2 * N_DEV, FF_GROUP))
    g = jnp.pad(g, [(0, 0)] * (g.ndim - 1) + [(0, FF_GROUP_PAD - FF_GROUP)])
    return jnp.moveaxis(g.reshape(a.shape[:-1] + (2 * D_FF_PAD,)), -1, axis)


def _ffn_unpad(a, axis):
    a = jnp.moveaxis(a, axis, -1)
    g = a.reshape(a.shape[:-1] + (2 * N_DEV, FF_GROUP_PAD))[..., :FF_GROUP]
    return jnp.moveaxis(g.reshape(a.shape[:-1] + (2 * D_FF,)), -1, axis)


MISC = (("w_o_mla", (256, 1024)), ("w_o_dil", (256, 512)), ("w_uq", (192, 512)), ("w_ukv", (256, 256)))
BIG_WEIGHTS = ("w_in", "w_up", "w_down", "w_out") + tuple(n for n, _ in MISC)


def _exchange_blocks(w):
    def t(a):
        return a.astype(BF16).T

    up = t(w["w_up"]).reshape(2, FF_GROUP, D_MODEL)
    return [
        jnp.pad(t(w["w_in"]), ((0, IN_ROWS_PAD - IN_ROWS), (0, 0))),
        jnp.pad(up, ((0, 0), (0, FF_GROUP_PAD - FF_GROUP), (0, 0))).reshape(2 * FF_GROUP_PAD, D_MODEL),
        jnp.pad(w["w_down"].astype(BF16), ((0, FF_GROUP_PAD - FF_GROUP), (0, 0))),
        w["w_out"].astype(BF16),
        jnp.concatenate([t(w[n]).reshape(-1, D_MODEL) for n, _ in MISC], axis=0),
    ]


def _misc_split(misc):
    out, off = {}, 0
    for n, (r, c) in MISC:
        rows = r * c // D_MODEL
        out[n] = misc[..., off : off + rows, :].reshape(misc.shape[:-2] + (r, c))
        off += rows
    return out


def _small_matrices(g_misc):
    misc = _misc_split(g_misc)
    uq_t = jnp.pad(misc["w_uq"], ((0, 0), (0, HEAD_PAD - QK_NOPE - QK_ROPE), (0, 0)))
    return {
        "uq_t": uq_t.reshape(MLA_HEADS * HEAD_PAD, Q_LORA),
        "ukv_t": misc["w_ukv"].reshape(MLA_HEADS * HEAD_PAD, KV_LORA),
        "o_mla_t": misc["w_o_mla"].reshape(D_MODEL, MLA_HEADS * V_HEAD),
        "o_dil_t": misc["w_o_dil"].reshape(D_MODEL, DIL_OUT),
    }


def _small_grad_blocks(g):
    uq_t = g["uq_t"].reshape(MLA_HEADS, HEAD_PAD, Q_LORA)[:, : QK_NOPE + QK_ROPE]
    misc = {"w_o_mla": g["o_mla_t"], "w_o_dil": g["o_dil_t"], "w_uq": uq_t, "w_ukv": g["ukv_t"]}
    return [
        g["w_out"].reshape(N_DEV, -1, D_MODEL),
        jnp.concatenate([misc[n].reshape(N_DEV, -1, D_MODEL) for n, _ in MISC], axis=1),
    ]


def _grad_shards(sums):
    s_in, s_out, s_misc, s_up, s_down = sums
    out = {
        "w_in": s_in[:IN_ROWS].T,
        "w_up": s_up.reshape(2, FF_GROUP_PAD, D_MODEL)[:, :FF_GROUP].reshape(2 * FF_GROUP, D_MODEL).T,
        "w_down": s_down[:FF_GROUP],
        "w_out": s_out,
    }
    out.update({n: v.T for n, v in _misc_split(s_misc).items()})
    return out


def _local_step(x, h, tgt, wt, conv_w, small, small_matrices, ffn_weight, send_ffn_grads, send_small_grads, send_w_in_grads):
    S = x.shape[0]
    lat_t, dqkv_t, g_t = wt
    cw = jnp.pad(_ffn_pad(conv_w, 1), ((0, 5), (0, 0)))
    cb = _ffn_pad(small["conv_b"], 1)
    cos_t, sin_t = _rope_tables(S)
    bias = _dil_bias()
    g1, g2, g3 = small["attn_norm_g"], small["ffn_norm_g"], small["final_norm_g"]
    gq, gkv = small["q_norm_g"], small["kv_norm_g"]

    lat = _mm(h, lat_t, "nt", F32, 1024, LAT_PAD, D_MODEL, "proj_lat")
    dqkv = _mm(h, dqkv_t, "nt", F32, 1024, 1536, D_MODEL, "proj_dqkv")
    gates = _mm(h, g_t, "nt", F32, 1024, 1024, D_MODEL, "proj_gates", bias=small["b_gate"], act="sigmoid")
    sm = small_matrices(gates)
    uq_t, ukv_t, o_mla_t, o_dil_t = sm["uq_t"], sm["ukv_t"], sm["o_mla_t"], sm["o_dil_t"]
    cqn, ckvn, kpe = _mla_prep1(lat, gq, gkv, cos_t, sin_t, "mla_prep1")
    q_raw = _mm(cqn, uq_t, "nt", F32, 1024, 1024, Q_LORA, "mla_uq")
    kv = _mm(ckvn, ukv_t, "nt", BF16, 1024, 1024, KV_LORA, "mla_ukv")
    q_att, k_att = _mla_prep2(q_raw, kv, kpe, cos_t, sin_t, "mla_prep2")
    o, lse = _flash2_fwd(q_att, k_att, kv, "mla_flash_fwd")
    o_a = _mm(o, o_mla_t, "nt", F32, 1024, 1024, MLA_HEADS * V_HEAD, "mla_out")

    d_os, d_ls = [], []
    for g, (_, dil) in enumerate(DIL_PATTERNS):
        og, lg = _dil_fwd_group(dqkv, bias[g], g, dil, f"dil_fwd_{g}")
        d_os.append(og)
        d_ls.append(lg)
    od, dil_lse = _dil_combine(d_os, d_ls, "dil_combine")
    o_b = _mm(od, o_dil_t, "nt", F32, 1024, 1024, DIL_OUT, "dil_out")

    mrg = _merge_fwd(gates, o_a, o_b, "merge_fwd")
    w_out = ffn_weight("w_out", mrg)
    x1, h2 = _mm_res_rms(mrg, w_out, x, g2, "mix_out")
    up_t = ffn_weight("up_t", h2)
    u0 = _mm(h2, up_t, "nt", F32, 1024, 1024, D_MODEL, "ffn_up")
    a = _ffn_fwd(u0, cw, cb, "ffn_conv_fwd")
    w_down = ffn_weight("w_down", a)
    x2 = _mm(a, w_down, "nn", F32, 1024, 512, D_FF_PAD // 2, "ffn_down", res=x1)
    loss_part, dx2, dx2b, dg3 = _final_loss(x2, g3, tgt, "final_loss")

    da = _mm(dx2b, w_down, "nt", F32, 1024, 512, D_MODEL, "ffn_down_dx")
    gw_down = _mm(a, dx2b, "tn", BF16, 512, 1024, S, "ffn_down_dw")
    du0, dcw, dcb = _ffn_bwd(u0, da, cw, cb, "ffn_conv_bwd")
    du0 = du0.reshape(2 * S, D_FF_PAD)
    gw_up_t = _mm(du0, h2, "tn", BF16, 512, 1024, S, "ffn_up_dw", a_halves=2)
    sent = send_ffn_grads(gw_up_t, gw_down)
    dh2 = _mm(du0, up_t, "nn", F32, 1024, 1024, D_FF_PAD // 2, "ffn_up_dx", a_halves=2)
    dx1, dx1b, dg2 = _rms_bwd(dh2, x1, g2 + sent, dx2, "rms_ffn_bwd")

    dmrg = _mm(dx1b, w_out, "nt", F32, 1024, 1024, D_MODEL, "mix_out_dx")
    gw_out = _mm(mrg, dx1b, "tn", BF16, 512, 1024, S, "mix_out_dw")
    do_a, do_b, dga, dgb, dba, dbb = _merge_bwd(dmrg, gates, o_a, o_b, "merge_bwd")

    do = _mm(do_a, o_mla_t, "nn", BF16, 1024, 1024, D_MODEL, "mla_out_dx")
    gw_o_mla_t = _mm(do_a, o, "tn", BF16, 1024, 1024, 1024, "mla_out_dw")
    dod = _mm(do_b, o_dil_t, "nn", F32, 1024, DIL_OUT, D_MODEL, "dil_out_dx")
    gw_o_dil_t = _mm(do_b, od, "tn", BF16, 1024, DIL_OUT, 1024, "dil_out_dw")

    delta = _flash_delta(do, o, "mla_flash_delta")
    lse_row = lse[:, :, 0][:, None, :]
    delta_row = delta[:, :MLA_HEADS].T[:, None, :]
    dq_att, dk_att, dv = _flash2_bwd(q_att, k_att, kv, do, lse_row, delta_row, "mla_flash_bwd")
    dq_raw, dkv, dkpe = _mla_post(dq_att, dk_att, dv, cos_t, sin_t, "mla_post")
    dcqn = _mm(dq_raw, uq_t, "nn", F32, 1024, Q_LORA, MLA_HEADS * HEAD_PAD, "mla_uq_dx")
    gw_uq_t = _mm(dq_raw, cqn, "tn", BF16, 1024, Q_LORA, 1024, "mla_uq_dw")
    dckvn = _mm(dkv, ukv_t, "nn", F32, 1024, KV_LORA, MLA_HEADS * HEAD_PAD, "mla_ukv_dx")
    gw_ukv_t = _mm(dkv, ckvn, "tn", BF16, 1024, KV_LORA, 1024, "mla_ukv_dw")
    sent = send_small_grads({"uq_t": gw_uq_t, "ukv_t": gw_ukv_t, "o_mla_t": gw_o_mla_t, "o_dil_t": gw_o_dil_t, "w_out": gw_out})
    dlat, dgq, dgkv = _lat_bwd(dcqn, dckvn, dkpe, lat, gq + sent, gkv, "lat_bwd")

    dd = _dil_rowdot(dod, od, "dil_rowdot")
    ddqkv = lax.empty((3 * DIL_GROUPS, S, DIL_OUT), F32)
    for g, (_, dil) in enumerate(DIL_PATTERNS):
        ddqkv = _dil_bwd_group(dqkv, bias[g], dod, dd, dil_lse, ddqkv, g, dil, f"dil_bwd_{g}")
    gw_lat_t = _mm(dlat, h, "tn", BF16, LAT_PAD, 1024, S, "proj_lat_dw")
    gw_dqkv_t = _mm(ddqkv.reshape(3 * DIL_GROUPS * S, DIL_OUT), h, "tn", BF16, 512, 1024, S, "proj_dqkv_dw", a_halves=3 * DIL_GROUPS)
    gw_ga_t = _mm(dga, h, "tn", BF16, 512, 1024, S, "proj_ga_dw")
    gw_gb_t = _mm(dgb, h, "tn", BF16, 512, 1024, S, "proj_gb_dw")
    sent = send_w_in_grads(gw_lat_t, gw_dqkv_t, gw_ga_t, gw_gb_t)
    dh = _mm(dlat + sent.astype(BF16), lat_t, "nn", F32, 1024, 1024, LAT_PAD, "proj_lat_dx")
    dh = _stacked_mm(ddqkv, dqkv_t, dh, "proj_dqkv_dx")
    dh = _mm(dga, g_t, "nn", F32, 1024, 1024, D_MODEL, "proj_ga_dx", res=dh)
    grad_x, dg1 = _mm_rms_bwd(dgb, g_t, 1, dh, x, g1, dx1, "proj_gb_dx_rms_attn_bwd")

    small_grads = {
        "attn_norm_g": dg1,
        "b_gate": jnp.concatenate([dba, dbb], axis=1),
        "q_norm_g": dgq,
        "kv_norm_g": dgkv,
        "ffn_norm_g": dg2,
        "conv_b": _ffn_unpad(jnp.concatenate([dcb[0], dcb[1]], axis=1), 1),
        "final_norm_g": dg3,
        "conv_w": _ffn_unpad(jnp.concatenate([dcw[0, :3], dcw[1, :3]], axis=1), 1),
    }
    return loss_part, grad_x, small_grads


SMALL_ORDER = ("attn_norm_g", "b_gate", "q_norm_g", "kv_norm_g", "ffn_norm_g", "conv_b", "final_norm_g", "conv_w")
WEIGHT_ORDER = (
    "attn_norm_g", "w_in", "b_gate", "q_norm_g", "w_uq", "kv_norm_g", "w_ukv", "w_o_mla", "w_o_dil", "w_out",
    "ffn_norm_g", "w_up", "conv_w", "conv_b", "w_down", "final_norm_g",
)


def kernel(x, attn_norm_g, w_in, b_gate, q_norm_g, w_uq, kv_norm_g, w_ukv, w_o_mla, w_o_dil, w_out, ffn_norm_g, w_up, conv_w, conv_b, w_down, final_norm_g, loss_target, m_attn_norm_g, m_w_in, m_b_gate, m_q_norm_g, m_w_uq, m_kv_norm_g, m_w_ukv, m_w_o_mla, m_w_o_dil, m_w_out, m_ffn_norm_g, m_w_up, m_conv_w, m_conv_b, m_w_down, m_final_norm_g, v_attn_norm_g, v_w_in, v_b_gate, v_q_norm_g, v_w_uq, v_kv_norm_g, v_w_ukv, v_w_o_mla, v_w_o_dil, v_w_out, v_ffn_norm_g, v_w_up, v_conv_w, v_conv_b, v_w_down, v_final_norm_g):
    env = dict(locals())
    dev = 4 * lax.axis_index("x") + 2 * lax.axis_index("y") + lax.axis_index("c")
    core = lax.axis_index("c").astype(jnp.int32).reshape(1)

    def two_d(a):
        return a.reshape(-1, a.shape[-1])

    w = {n: two_d(env[n]) for n in WEIGHT_ORDER}
    m = {n: two_d(env["m_" + n]) for n in WEIGHT_ORDER}
    v = {n: two_d(env["v_" + n]) for n in WEIGHT_ORDER}

    chip = 2 * lax.axis_index("x") + lax.axis_index("y")

    def own_slot_in(lands, own, slot=dev):
        return [lax.dynamic_update_slice(l, o[None], (slot, 0, 0)) for l, o in zip(lands, own)]

    b_in = _exchange_blocks(w)[0]
    r, c = CONV_SHARD
    conv = jnp.pad(w["conv_w"].reshape(-1), (0, 8 * SMALL_COLS - r * c)).reshape(8, SMALL_COLS)
    first_level, token = _gather2_start([b_in, conv], "ag_w_in_start")
    tied = {n: w[n] + token[0, 0] for n in BIG_WEIGHTS}
    _, b_up, b_down, b_out, b_misc = _exchange_blocks(tied)
    h = _rms_fwd(x[0], w["attn_norm_g"] + token[0, 0], "rms_attn")
    prepared = b_up[:1, :1] + b_down[:1, :1] + b_out[:1, :1] + b_misc[:1, :1] + h[:1, :1]
    own, lands = _gather2_wait(_gather2_forward(first_level, prepared, "ag_w_in_forward"), "ag_w_in_wait")
    g_in, conv = own_slot_in(lands, own)
    misc_gather, started = _exchange_start([b_misc], False, conv, "ag_small_start")
    ffn_gathers, started2 = {}, started
    for key, block in (("w_out", b_out), ("up_t", b_up), ("w_down", b_down)):
        ffn_gathers[key], started2 = _exchange_start([block], False, started2, f"ag_{key}_start")
    wt = _w_in_regroup(g_in, started2, "w_in_regroup")
    conv = conv.reshape(N_DEV, 8 * SMALL_COLS)[:, : r * c].reshape(N_DEV, r, c)
    conv_w_full = conv.transpose(1, 0, 2).reshape(r, N_DEV * c)
    small = {n: w[n] for n in SMALL_ORDER if n != "conv_w"}

    def small_matrices(after):
        own, lands = _exchange_wait(misc_gather, False, after, "ag_small_wait")
        return _small_matrices(own_slot_in(lands, own)[0])

    def ffn_weight(key, after):
        own, lands = _exchange_wait(ffn_gathers[key], False, after, f"ag_{key}_wait")
        return own_slot_in(lands, own)[0].reshape(-1, D_MODEL)

    reduces = {}

    def send_ffn_grads(gw_up_t, gw_down):
        blocks = [gw_up_t.reshape(N_DEV, 2 * FF_GROUP_PAD, D_MODEL), gw_down.reshape(N_DEV, FF_GROUP_PAD, D_MODEL)]
        reduces["ffn"], token = _exchange_start(blocks, True, gw_down, "rs_ffn_start")
        return token[0, 0]

    def send_small_grads(g):
        reduces["small"], token = _exchange_start(_small_grad_blocks(g), True, g["w_out"], "rs_small_start")
        return token[0, 0]

    def send_w_in_grads(g_lat, g_dqkv, g_ga, g_gb):
        e_in = _w_in_grad_regroup(g_lat, g_dqkv, g_ga, g_gb, "w_in_grad_regroup")
        pair = _pair_add(e_in, _pair_exchange([e_in], "rs_w_in_pair_exchange")[0], core, "rs_w_in_pair_add")
        reduces["w_in"], token = _exchange_start([pair], True, pair, "rs_w_in_start", chips=True)
        return token[0, 0]

    loss_part, grad_x, small_grads = _local_step(
        x[0], h, loss_target[0], wt, conv_w_full, small, small_matrices, ffn_weight,
        send_ffn_grads, send_small_grads, send_w_in_grads,
    )
    loss = lax.psum(loss_part[0, 0], AXES)
    sflat = jnp.concatenate([small_grads[n].reshape(-1) for n in SMALL_ORDER])
    sflat = jnp.pad(sflat, (0, SMALL_ROWS * SMALL_COLS - sflat.shape[0])).reshape(SMALL_ROWS, SMALL_COLS)
    vec_gather, _ = _exchange_start([sflat], False, sflat, "rs_vec_start")

    def finish(key, by_chip, name):
        sent, lands = _exchange_wait(reduces[key], True, grad_x, name + "_wait", chips=by_chip)
        slot = chip if by_chip else dev
        own = [lax.dynamic_index_in_dim(s, slot, 0, keepdims=False) for s in sent]
        return [_slot_sum(p, f"{name}_sum_{i}") for i, p in enumerate(own_slot_in(lands, own, slot))]

    (s_in,) = finish("w_in", True, "rs_w_in")
    s_out, s_misc = finish("small", False, "rs_small")
    s_up, s_down = finish("ffn", False, "rs_ffn")
    gshard = _grad_shards([s_in, s_out, s_misc, s_up, s_down])

    updates = {n: _adamw(w[n], gshard[n], m[n], v[n], "adamw_" + n) for n in BIG_WEIGHTS}

    own, lands = _exchange_wait(vec_gather, False, updates["w_ukv"][0], "rs_vec_wait")
    ssum = _slot_sum(own_slot_in(lands, own)[0], "small_sum").reshape(-1)
    gsmall, off = {}, 0
    for n in SMALL_ORDER:
        shape = (3, 2 * D_FF) if n == "conv_w" else w[n].shape
        size = shape[0] * shape[1]
        gsmall[n] = ssum[off : off + size].reshape(shape)
        off += size
    gsmall["conv_w"] = lax.dynamic_slice_in_dim(gsmall["conv_w"], dev * CONV_SHARD[1], CONV_SHARD[1], axis=1)
    updates.update({n: _adamw(w[n], gsmall[n], m[n], v[n], "adamw_" + n) for n in SMALL_ORDER})

    g_all = {**gshard, **gsmall}
    out_g, out_d, out_m, out_v = [], [], [], []
    for n in WEIGHT_ORDER:
        d, nm, nv = updates[n]
        shape = env[n].shape
        out_g.append(g_all[n].reshape(shape))
        out_d.append(d.reshape(shape))
        out_m.append(nm.reshape(shape))
        out_v.append(nv.reshape(shape))
    return (loss, grad_x[None], *out_g, *out_d, *out_m, *out_v)
```

```python
import functools

import jax
import jax.numpy as jnp
import numpy as np
from jax import lax
from jax.experimental import pallas as pl
from jax.experimental.pallas import tpu as pltpu

F32 = jnp.float32
BF16 = jnp.bfloat16

N_DEV = 8
N_CHIP = 4
AXES = ("x", "y", "c")
MESH = pl.DeviceIdType.MESH

D_MODEL = 2048
MLA_HEADS = 8
QK_NOPE = 128
QK_ROPE = 64
V_HEAD = 128
Q_LORA = 512
KV_LORA = 256
ROPE_THETA = 10000.0
HEAD_PAD = 256
DIL_PATTERNS = ((128, 1), (512, 4), (2048, 16))
DIL_GROUPS = 3
DIL_HG = 4
DIL_HEADS = 12
DIL_HD = 128
DIL_BLK = 128
DIL_QKV = DIL_HEADS * DIL_HD
DIL_OUT = DIL_HG * DIL_HD
ALIBI_MAX_BIAS = 8.0
D_FF = 5504
D_FF_PAD = 5632
NORM_EPS = 1e-6
LAT = Q_LORA + KV_LORA + QK_ROPE
LAT_PAD = 896
D_IN = LAT + 3 * DIL_QKV + 2 * D_MODEL
NEG = -1e30

ADAM_LR = 0.001
ADAM_B1 = 0.9
ADAM_B2 = 0.999
ADAM_EPS = 1e-08
ADAM_WD = 0.01
ADAM_STEP = 10

SMALL_ROWS = 56
SMALL_COLS = 1024

IN_ROWS = 1192
IN_ROWS_PAD = 1200
FF_GROUP = D_FF // N_DEV
FF_GROUP_PAD = D_FF_PAD // N_DEV
CONV_SHARD = (3, 1376)

NT = (((1,), (1,)), ((), ()))
TN = (((0,), (0,)), ((), ()))


def _dot(a, b, dims=(((1,), (0,)), ((), ()))):
    return lax.dot_general(a, b, dims, preferred_element_type=F32)


def _mm(a, b, mode, out_dtype, tm, tn, tk, name, bias=None, act=None, res=None, b_koff=0, a_halves=1):
    H = a_halves
    if mode == "nn":
        (M, K), (K2, N) = (a.shape[0] // H, a.shape[1] * H), b.shape
        assert (b_koff + 1) * K <= K2, (name, a.shape, b.shape)
        koff, K2 = b_koff * (K // tk), K
        kper, mrows = a.shape[1] // tk, M // tm
        a_spec = pl.BlockSpec((tm, tk), lambda i, j, k: (i + (k // kper) * mrows, k % kper))
        b_spec = pl.BlockSpec((tk, tn), lambda i, j, k: (k + koff, j))
        dims = (((1,), (0,)), ((), ()))
    elif mode == "nt":
        (M, K), (N, K2) = a.shape, b.shape
        a_spec = pl.BlockSpec((tm, tk), lambda i, j, k: (i, k))
        b_spec = pl.BlockSpec((tn, tk), lambda i, j, k: (j, k))
        dims = NT
    else:
        (K, M), (K2, N) = (a.shape[0] // H, a.shape[1] * H), b.shape
        mper, krows = a.shape[1] // tm, K // tk
        a_spec = pl.BlockSpec((tk, tm), lambda i, j, k: (k + (i // mper) * krows, i % mper))
        b_spec = pl.BlockSpec((tk, tn), lambda i, j, k: (k, j))
        dims = TN
    assert K == K2 and M % tm == 0 and N % tn == 0 and K % tk == 0, (name, a.shape, b.shape)
    nk = K // tk
    has_bias, has_res = bias is not None, res is not None

    def body(*refs):
        refs = list(refs)
        a_ref, b_ref = refs[0], refs[1]
        pos = 2
        bias_ref = res_ref = None
        if has_bias:
            bias_ref = refs[pos]
            pos += 1
        if has_res:
            res_ref = refs[pos]
            pos += 1
        o_ref = refs[pos]
        p = _dot(a_ref[...].astype(BF16), b_ref[...].astype(BF16), dims)

        def finish(acc):
            if has_bias:
                acc = acc + bias_ref[...]
            if act == "sigmoid":
                acc = jax.nn.sigmoid(acc)
            if has_res:
                acc = res_ref[...] + acc
            o_ref[...] = acc.astype(o_ref.dtype)

        if nk == 1:
            finish(p)
        else:
            acc_ref = refs[pos + 1]
            k = pl.program_id(2)

            @pl.when(k == 0)
            def _():
                acc_ref[...] = p

            @pl.when(k != 0)
            def _():
                acc_ref[...] += p

            @pl.when(k == nk - 1)
            def _():
                finish(acc_ref[...])

    in_specs = [a_spec, b_spec]
    args = [a, b]
    if has_bias:
        in_specs.append(pl.BlockSpec((1, tn), lambda i, j, k: (0, j)))
        args.append(bias)
    if has_res:
        in_specs.append(pl.BlockSpec((tm, tn), lambda i, j, k: (i, j)))
        args.append(res)
    return pl.pallas_call(
        body,
        name=name,
        grid=(M // tm, N // tn, nk),
        in_specs=in_specs,
        out_specs=pl.BlockSpec((tm, tn), lambda i, j, k: (i, j)),
        out_shape=jax.ShapeDtypeStruct((M, N), out_dtype),
        scratch_shapes=[pltpu.VMEM((tm, tn), F32)] if nk > 1 else [],
        compiler_params=pltpu.CompilerParams(dimension_semantics=("parallel", "parallel", "arbitrary")),
    )(*args)


def _stacked_mm(pieces, w_t, res, name, tm=512, tn=1024):
    P, M, W = pieces.shape
    N = w_t.shape[1]

    def body(a_ref, b_ref, r_ref, o_ref):
        acc = r_ref[...]
        for p in range(P):
            acc = acc + _dot(a_ref[p].astype(BF16), b_ref[p * W : (p + 1) * W, :])
        o_ref[...] = acc

    tile = pl.BlockSpec((tm, tn), lambda i, j: (i, j))
    return pl.pallas_call(
        body,
        name=name,
        grid=(M // tm, N // tn),
        in_specs=[pl.BlockSpec((P, tm, W), lambda i, j: (0, i, 0)), pl.BlockSpec((P * W, tn), lambda i, j: (0, j)), tile],
        out_specs=tile,
        out_shape=jax.ShapeDtypeStruct((M, N), F32),
        compiler_params=pltpu.CompilerParams(dimension_semantics=("parallel", "parallel")),
    )(pieces, w_t, res)


def _rstd(x):
    return lax.rsqrt(jnp.mean(x * x, axis=-1, keepdims=True) + NORM_EPS)


def _rms_bwd_math(dy, x, g):
    r = _rstd(x)
    xh = x * r
    dg = jnp.sum(dy * xh, axis=0, keepdims=True)
    dxh = dy * g
    dx = r * (dxh - xh * jnp.mean(dxh * xh, axis=-1, keepdims=True))
    return dx, dg


def _rms_fwd(x, g, name, tr=256):
    S, D = x.shape

    def body(x_ref, g_ref, o_ref):
        xv = x_ref[...]
        o_ref[...] = ((xv * _rstd(xv)) * g_ref[...]).astype(o_ref.dtype)

    return pl.pallas_call(
        body,
        name=name,
        grid=(S // tr,),
        in_specs=[pl.BlockSpec((tr, D), lambda i: (i, 0)), pl.BlockSpec((1, D), lambda i: (0, 0))],
        out_specs=pl.BlockSpec((tr, D), lambda i: (i, 0)),
        out_shape=jax.ShapeDtypeStruct((S, D), BF16),
        compiler_params=pltpu.CompilerParams(dimension_semantics=("parallel",)),
    )(x, g)


def _rms_bwd(dy, x, g, res, name, tr=256):
    S, D = x.shape

    def body(dy_ref, x_ref, g_ref, res_ref, dx_ref, dxb_ref, dg_ref):
        dx, dg = _rms_bwd_math(dy_ref[...], x_ref[...], g_ref[...])
        dx = dx + res_ref[...]
        dx_ref[...] = dx
        dxb_ref[...] = dx.astype(BF16)

        @pl.when(pl.program_id(0) == 0)
        def _():
            dg_ref[...] = dg

        @pl.when(pl.program_id(0) != 0)
        def _():
            dg_ref[...] += dg

    row = pl.BlockSpec((tr, D), lambda i: (i, 0))
    vec = pl.BlockSpec((1, D), lambda i: (0, 0))
    return pl.pallas_call(
        body,
        name=name,
        grid=(S // tr,),
        in_specs=[row, row, vec, row],
        out_specs=[row, row, vec],
        out_shape=[jax.ShapeDtypeStruct((S, D), F32), jax.ShapeDtypeStruct((S, D), BF16), jax.ShapeDtypeStruct((1, D), F32)],
        compiler_params=pltpu.CompilerParams(dimension_semantics=("arbitrary",)),
    )(dy, x, g, res)


def _mm_res_rms(a, b, res, g, name, tm=256):
    M, K = a.shape
    D = b.shape[1]

    def body(a_ref, b_ref, res_ref, g_ref, y_ref, h_ref):
        y = res_ref[...] + _dot(a_ref[...], b_ref[...])
        y_ref[...] = y
        h_ref[...] = ((y * _rstd(y)) * g_ref[...]).astype(BF16)

    row = pl.BlockSpec((tm, D), lambda i: (i, 0))
    return pl.pallas_call(
        body,
        name=name,
        grid=(M // tm,),
        in_specs=[pl.BlockSpec((tm, K), lambda i: (i, 0)), pl.BlockSpec((K, D), lambda i: (0, 0)), row, pl.BlockSpec((1, D), lambda i: (0, 0))],
        out_specs=[row, row],
        out_shape=[jax.ShapeDtypeStruct((M, D), F32), jax.ShapeDtypeStruct((M, D), BF16)],
        compiler_params=pltpu.CompilerParams(dimension_semantics=("parallel",)),
    )(a, b, res, g)


def _mm_rms_bwd(a, b, b_koff, dy_part, x, g, res, name, tm=256):
    M, K = a.shape
    D = x.shape[1]

    def body(a_ref, b_ref, dyp_ref, x_ref, g_ref, res_ref, dx_ref, dg_ref):
        dy = dyp_ref[...] + _dot(a_ref[...], b_ref[...])
        dx, dg = _rms_bwd_math(dy, x_ref[...], g_ref[...])
        dx_ref[...] = dx + res_ref[...]

        @pl.when(pl.program_id(0) == 0)
        def _():
            dg_ref[...] = dg

        @pl.when(pl.program_id(0) != 0)
        def _():
            dg_ref[...] += dg

    row = pl.BlockSpec((tm, D), lambda i: (i, 0))
    vec = pl.BlockSpec((1, D), lambda i: (0, 0))
    return pl.pallas_call(
        body,
        name=name,
        grid=(M // tm,),
        in_specs=[pl.BlockSpec((tm, K), lambda i: (i, 0)), pl.BlockSpec((K, D), lambda i: (b_koff, 0)), row, row, vec, row],
        out_specs=[row, vec],
        out_shape=[jax.ShapeDtypeStruct((M, D), F32), jax.ShapeDtypeStruct((1, D), F32)],
        compiler_params=pltpu.CompilerParams(dimension_semantics=("arbitrary",)),
    )(a, b, dy_part, x, g, res)


def _final_loss(x2, g, tgt, name, tr=256):
    S, D = x2.shape

    def body(x_ref, g_ref, t_ref, loss_ref, dx_ref, dxb_ref, dg_ref):
        xv, gv = x_ref[...], g_ref[...]
        y = (xv * _rstd(xv)) * gv
        e = y - t_ref[...]
        part = 0.5 * jnp.sum(jnp.mean(e * e, axis=-1, keepdims=True), axis=0, keepdims=True)
        dx, dg = _rms_bwd_math(e * (1.0 / D), xv, gv)
        dx_ref[...] = dx
        dxb_ref[...] = dx.astype(BF16)
        part = jnp.broadcast_to(part, (1, 128))

        @pl.when(pl.program_id(0) == 0)
        def _():
            dg_ref[...] = dg
            loss_ref[...] = part

        @pl.when(pl.program_id(0) != 0)
        def _():
            dg_ref[...] += dg
            loss_ref[...] += part

    row = pl.BlockSpec((tr, D), lambda i: (i, 0))
    vec = pl.BlockSpec((1, D), lambda i: (0, 0))
    return pl.pallas_call(
        body,
        name=name,
        grid=(S // tr,),
        in_specs=[row, vec, row],
        out_specs=[pl.BlockSpec((1, 128), lambda i: (0, 0)), row, row, vec],
        out_shape=[
            jax.ShapeDtypeStruct((1, 128), F32),
            jax.ShapeDtypeStruct((S, D), F32),
            jax.ShapeDtypeStruct((S, D), BF16),
            jax.ShapeDtypeStruct((1, D), F32),
        ],
        compiler_params=pltpu.CompilerParams(dimension_semantics=("arbitrary",)),
    )(x2, g, tgt)


def _rope_tables(S):
    pos = jnp.arange(S, dtype=F32)
    inv_freq = ROPE_THETA ** (-jnp.arange(0, QK_ROPE, 2, dtype=F32) / QK_ROPE)
    ang = pos[:, None] * inv_freq[None, :]
    cos, sin = jnp.cos(ang), jnp.sin(ang)
    zero = jnp.zeros((S, 128 - QK_ROPE), F32)
    return jnp.concatenate([cos, cos, zero], axis=1), jnp.concatenate([-sin, sin, zero], axis=1)


def _rope_tile(x, cos_t, sin_t):
    lane = lax.broadcasted_iota(jnp.int32, x.shape, 1)
    partner = jnp.where(lane < QK_ROPE // 2, pltpu.roll(x, 128 - QK_ROPE // 2, 1), pltpu.roll(x, QK_ROPE // 2, 1))
    return x * cos_t + partner * sin_t


def _mla_prep1(lat, gq, gkv, cos_t, sin_t, name, tr=256):
    S = lat.shape[0]

    def body(lat_ref, gq_ref, gkv_ref, cos_ref, sin_ref, cq_ref, ckv_ref, kpe_ref):
        cq = lat_ref[:, :Q_LORA]
        ckv = lat_ref[:, Q_LORA : Q_LORA + KV_LORA]
        cq_ref[...] = ((cq * _rstd(cq)) * gq_ref[...]).astype(BF16)
        ckv_ref[...] = ((ckv * _rstd(ckv)) * gkv_ref[...]).astype(BF16)
        kpe_ref[...] = _rope_tile(lat_ref[:, Q_LORA + KV_LORA :], cos_ref[...], sin_ref[...]).astype(BF16)

    def row(n):
        return pl.BlockSpec((tr, n), lambda i: (i, 0))

    def vec(n):
        return pl.BlockSpec((1, n), lambda i: (0, 0))

    return pl.pallas_call(
        body,
        name=name,
        grid=(S // tr,),
        in_specs=[row(LAT_PAD), vec(Q_LORA), vec(KV_LORA), row(128), row(128)],
        out_specs=[row(Q_LORA), row(KV_LORA), row(128)],
        out_shape=[
            jax.ShapeDtypeStruct((S, Q_LORA), BF16),
            jax.ShapeDtypeStruct((S, KV_LORA), BF16),
            jax.ShapeDtypeStruct((S, 128), BF16),
        ],
        compiler_params=pltpu.CompilerParams(dimension_semantics=("parallel",)),
    )(lat, gq, gkv, cos_t, sin_t)


def _mla_prep2(q_raw, kv, kpe, cos_t, sin_t, name, tr=256):
    S = q_raw.shape[0]
    W = MLA_HEADS * HEAD_PAD

    def body(q_ref, kv_ref, kpe_ref, cos_ref, sin_ref, qa_ref, ka_ref):
        cos_v, sin_v, kpe_v = cos_ref[...], sin_ref[...], kpe_ref[...]
        for h in range(MLA_HEADS):
            lo = h * HEAD_PAD
            qa_ref[:, lo : lo + 128] = q_ref[:, lo : lo + 128].astype(BF16)
            qa_ref[:, lo + 128 : lo + 256] = _rope_tile(q_ref[:, lo + 128 : lo + 256], cos_v, sin_v).astype(BF16)
            ka_ref[:, lo : lo + 128] = kv_ref[:, lo : lo + 128]
            ka_ref[:, lo + 128 : lo + 256] = kpe_v

    def row(n):
        return pl.BlockSpec((tr, n), lambda i: (i, 0))

    return pl.pallas_call(
        body,
        name=name,
        grid=(S // tr,),
        in_specs=[row(W), row(W), row(128), row(128), row(128)],
        out_specs=[row(W), row(W)],
        out_shape=[jax.ShapeDtypeStruct((S, W), BF16), jax.ShapeDtypeStruct((S, W), BF16)],
        compiler_params=pltpu.CompilerParams(dimension_semantics=("parallel",)),
    )(q_raw, kv, kpe, cos_t, sin_t)


def _mla_post(dq_att, dk_att, dv, cos_t, sin_t, name, tr=256):
    S = dq_att.shape[0]
    W = MLA_HEADS * HEAD_PAD

    def body(dq_ref, dk_ref, dv_ref, cos_ref, sin_ref, dqr_ref, dkv_ref, dkpe_ref):
        cos_v, nsin_v = cos_ref[...], -sin_ref[...]
        kpe = jnp.zeros((tr, 128), F32)
        for h in range(MLA_HEADS):
            lo = h * HEAD_PAD
            dqr_ref[:, lo : lo + 128] = dq_ref[:, lo : lo + 128].astype(BF16)
            dqr_ref[:, lo + 128 : lo + 256] = _rope_tile(dq_ref[:, lo + 128 : lo + 256], cos_v, nsin_v).astype(BF16)
            dkv_ref[:, lo : lo + 128] = dk_ref[:, lo : lo + 128].astype(BF16)
            dkv_ref[:, lo + 128 : lo + 256] = dv_ref[:, h * 128 : (h + 1) * 128].astype(BF16)
            kpe = kpe + dk_ref[:, lo + 128 : lo + 256]
        dkpe_ref[...] = _rope_tile(kpe, cos_v, nsin_v)

    def row(n):
        return pl.BlockSpec((tr, n), lambda i: (i, 0))

    return pl.pallas_call(
        body,
        name=name,
        grid=(S // tr,),
        in_specs=[row(W), row(W), row(MLA_HEADS * V_HEAD), row(128), row(128)],
        out_specs=[row(W), row(W), row(128)],
        out_shape=[jax.ShapeDtypeStruct((S, W), BF16), jax.ShapeDtypeStruct((S, W), BF16), jax.ShapeDtypeStruct((S, 128), F32)],
        compiler_params=pltpu.CompilerParams(dimension_semantics=("parallel",)),
    )(dq_att, dk_att, dv, cos_t, sin_t)


def _lat_bwd(dcqn, dckvn, dkpe, lat, gq, gkv, name, tr=256):
    S = lat.shape[0]

    def body(dcq_ref, dckv_ref, dkpe_ref, lat_ref, gq_ref, gkv_ref, dlat_ref, dgq_ref, dgkv_ref):
        dq, dgq = _rms_bwd_math(dcq_ref[...], lat_ref[:, :Q_LORA], gq_ref[...])
        dkv, dgkv = _rms_bwd_math(dckv_ref[...], lat_ref[:, Q_LORA : Q_LORA + KV_LORA], gkv_ref[...])
        dlat_ref[:, :Q_LORA] = dq.astype(BF16)
        dlat_ref[:, Q_LORA : Q_LORA + KV_LORA] = dkv.astype(BF16)
        dlat_ref[:, Q_LORA + KV_LORA :] = dkpe_ref[...].astype(BF16)

        @pl.when(pl.program_id(0) == 0)
        def _():
            dgq_ref[...] = dgq
            dgkv_ref[...] = dgkv

        @pl.when(pl.program_id(0) != 0)
        def _():
            dgq_ref[...] += dgq
            dgkv_ref[...] += dgkv

    def row(n):
        return pl.BlockSpec((tr, n), lambda i: (i, 0))

    def vec(n):
        return pl.BlockSpec((1, n), lambda i: (0, 0))

    return pl.pallas_call(
        body,
        name=name,
        grid=(S // tr,),
        in_specs=[row(Q_LORA), row(KV_LORA), row(128), row(LAT_PAD), vec(Q_LORA), vec(KV_LORA)],
        out_specs=[row(LAT_PAD), vec(Q_LORA), vec(KV_LORA)],
        out_shape=[
            jax.ShapeDtypeStruct((S, LAT_PAD), BF16),
            jax.ShapeDtypeStruct((1, Q_LORA), F32),
            jax.ShapeDtypeStruct((1, KV_LORA), F32),
        ],
        compiler_params=pltpu.CompilerParams(dimension_semantics=("arbitrary",)),
    )(dcqn, dckvn, dkpe, lat, gq, gkv)


MLA_SCALE = (QK_NOPE + QK_ROPE) ** -0.5
LOG2E = 1.4426950408889634
MLA_C2 = MLA_SCALE * LOG2E
FLASH_T = 1024


def _causal_pairs(n, by_key):
    pairs = [(i, j) for j in range(n) for i in range(j, n)] if by_key else [(i, j) for i in range(n) for j in range(i + 1)]
    return jnp.asarray([p[0] for p in pairs], jnp.int32), jnp.asarray([p[1] for p in pairs], jnp.int32)


def _lanes(x, n):
    return jnp.tile(x, (1, n // 128))


def _flash_grid(npairs, in_specs, out_specs, scratch):
    return pltpu.PrefetchScalarGridSpec(
        num_scalar_prefetch=2, grid=(MLA_HEADS, npairs), in_specs=in_specs, out_specs=out_specs, scratch_shapes=scratch
    )


def _flash2_fwd(q_att, k_att, kv, name, t=FLASH_T):
    S = q_att.shape[0]
    qi_tab, kj_tab = _causal_pairs(S // t, by_key=False)

    def body(qi_ref, kj_ref, q_ref, k_ref, v_ref, o_ref, lse_ref, m_sc, l_sc, acc_sc):
        step = pl.program_id(1)
        qi, kj = qi_ref[step], kj_ref[step]

        @pl.when(kj == 0)
        def _():
            m_sc[...] = jnp.full((t, 128), NEG, F32)
            l_sc[...] = jnp.zeros((t, 128), F32)
            acc_sc[...] = jnp.zeros((t, V_HEAD), F32)

        def update(s):
            m_prev = m_sc[...]
            m_new = jnp.maximum(m_prev, jnp.max(s, axis=1, keepdims=True))
            p = jnp.exp2((s - _lanes(m_new, t)) * MLA_C2)
            alpha = jnp.exp2((m_prev - m_new) * MLA_C2)
            l_sc[...] = alpha * l_sc[...] + jnp.sum(p, axis=1, keepdims=True)
            acc_sc[...] = alpha * acc_sc[...] + _dot(p.astype(BF16), v_ref[...])
            m_sc[...] = m_new

        @pl.when(kj < qi)
        def _():
            update(_dot(q_ref[...], k_ref[...], NT))

        @pl.when(kj == qi)
        def _():
            s = _dot(q_ref[...], k_ref[...], NT)
            rows = lax.broadcasted_iota(jnp.int32, s.shape, 0)
            cols = lax.broadcasted_iota(jnp.int32, s.shape, 1)
            update(jnp.where(cols <= rows, s, NEG))
            l = l_sc[...]
            o_ref[...] = acc_sc[...] / l
            lse_ref[0] = m_sc[...] * MLA_SCALE + jnp.log(l)

    return pl.pallas_call(
        body,
        name=name,
        grid_spec=_flash_grid(
            qi_tab.shape[0],
            [
                pl.BlockSpec((t, HEAD_PAD), lambda h, p, qi, kj: (qi[p], h)),
                pl.BlockSpec((t, HEAD_PAD), lambda h, p, qi, kj: (kj[p], h)),
                pl.BlockSpec((t, V_HEAD), lambda h, p, qi, kj: (kj[p], 2 * h + 1)),
            ],
            [
                pl.BlockSpec((t, V_HEAD), lambda h, p, qi, kj: (qi[p], h)),
                pl.BlockSpec((1, t, 128), lambda h, p, qi, kj: (h, qi[p], 0)),
            ],
            [pltpu.VMEM((t, 128), F32), pltpu.VMEM((t, 128), F32), pltpu.VMEM((t, V_HEAD), F32)],
        ),
        out_shape=[jax.ShapeDtypeStruct((S, MLA_HEADS * V_HEAD), F32), jax.ShapeDtypeStruct((MLA_HEADS, S, 128), F32)],
        compiler_params=pltpu.CompilerParams(dimension_semantics=("parallel", "arbitrary")),
    )(qi_tab, kj_tab, q_att, k_att, kv)


def _flash_delta(do, o, name, tr=512):
    S = o.shape[0]

    def body(do_ref, o_ref, d_ref):
        lane = lax.broadcasted_iota(jnp.int32, (tr, 128), 1)
        acc = jnp.zeros((tr, 128), F32)
        for h in range(MLA_HEADS):
            sl = slice(h * V_HEAD, (h + 1) * V_HEAD)
            acc = jnp.where(lane == h, jnp.sum(do_ref[:, sl].astype(F32) * o_ref[:, sl], axis=1, keepdims=True), acc)
        d_ref[...] = acc

    row = pl.BlockSpec((tr, MLA_HEADS * V_HEAD), lambda i: (i, 0))
    return pl.pallas_call(
        body,
        name=name,
        grid=(S // tr,),
        in_specs=[row, row],
        out_specs=pl.BlockSpec((tr, 128), lambda i: (i, 0)),
        out_shape=jax.ShapeDtypeStruct((S, 128), F32),
        compiler_params=pltpu.CompilerParams(dimension_semantics=("parallel",)),
    )(do, o)


def _flash2_bwd(q_att, k_att, kv, do, lse_row, delta_row, name, t=FLASH_T):
    S = q_att.shape[0]
    n = S // t
    qi_tab, kj_tab = _causal_pairs(n, by_key=True)
    last = qi_tab.shape[0] - 1

    def body(qi_ref, kj_ref, q_ref, k_ref, v_ref, do_ref, lse_ref, dl_ref, dq_ref, dk_ref, dv_ref, dk_sc, dv_sc):
        step = pl.program_id(1)
        qi, kj = qi_ref[step], kj_ref[step]

        @pl.when(step == 0)
        def _():
            dq_ref[...] = jnp.zeros((S, HEAD_PAD), F32)

        def update(st):
            q, do_v = q_ref[...], do_ref[...]
            pt = jnp.exp2(st * MLA_C2 - lse_ref[0] * LOG2E)
            dv_sc[...] += _dot(pt.astype(BF16), do_v)
            dpt = _dot(v_ref[...], do_v, NT)
            dst = (pt * (dpt - dl_ref[0])).astype(BF16)
            dk_sc[...] += _dot(dst, q)
            rows = pl.ds(pl.multiple_of(qi * t, t), t)
            dq_ref[rows, :] += _dot(dst, k_ref[...], TN)

        @pl.when(qi == kj)
        def _():
            dk_sc[...] = jnp.zeros((t, HEAD_PAD), F32)
            dv_sc[...] = jnp.zeros((t, V_HEAD), F32)
            st = _dot(k_ref[...], q_ref[...], NT)
            keys = lax.broadcasted_iota(jnp.int32, st.shape, 0)
            qs = lax.broadcasted_iota(jnp.int32, st.shape, 1)
            update(jnp.where(keys <= qs, st, NEG))

        @pl.when(qi > kj)
        def _():
            update(_dot(k_ref[...], q_ref[...], NT))

        @pl.when(qi == n - 1)
        def _():
            dk_ref[...] = dk_sc[...] * MLA_SCALE
            dv_ref[...] = dv_sc[...]

        @pl.when(step == last)
        def _():
            dq_ref[...] = dq_ref[...] * MLA_SCALE

    qrow = lambda h, p, qi, kj: (qi[p], h)
    krow = lambda h, p, qi, kj: (kj[p], h)
    stat = pl.BlockSpec((1, 1, t), lambda h, p, qi, kj: (h, 0, qi[p]))
    return pl.pallas_call(
        body,
        name=name,
        grid_spec=_flash_grid(
            qi_tab.shape[0],
            [
                pl.BlockSpec((t, HEAD_PAD), qrow),
                pl.BlockSpec((t, HEAD_PAD), krow),
                pl.BlockSpec((t, V_HEAD), lambda h, p, qi, kj: (kj[p], 2 * h + 1)),
                pl.BlockSpec((t, V_HEAD), qrow),
                stat,
                stat,
            ],
            [
                pl.BlockSpec((S, HEAD_PAD), lambda h, p, qi, kj: (0, h)),
                pl.BlockSpec((t, HEAD_PAD), krow),
                pl.BlockSpec((t, V_HEAD), krow),
            ],
            [pltpu.VMEM((t, HEAD_PAD), F32), pltpu.VMEM((t, V_HEAD), F32)],
        ),
        out_shape=[
            jax.ShapeDtypeStruct((S, MLA_HEADS * HEAD_PAD), F32),
            jax.ShapeDtypeStruct((S, MLA_HEADS * HEAD_PAD), F32),
            jax.ShapeDtypeStruct((S, MLA_HEADS * V_HEAD), F32),
        ],
        compiler_params=pltpu.CompilerParams(dimension_semantics=("parallel", "arbitrary")),
    )(qi_tab, kj_tab, q_att, k_att, kv, do, lse_row, delta_row)


DIL_SCALE = DIL_HD**-0.5


def _dil_bias():
    slopes = 2.0 ** (-ALIBI_MAX_BIAS * np.arange(1, DIL_HEADS + 1, dtype=np.float64) / DIL_HEADS)
    slopes = slopes.astype(np.float32).reshape(DIL_GROUPS, DIL_HG)
    p = np.arange(DIL_BLK)[:, None]
    kidx = np.arange(2 * DIL_BLK)[None, :]
    j = p + DIL_BLK - kidx
    out = np.zeros((DIL_GROUPS, DIL_HG, DIL_BLK, 2 * DIL_BLK), np.float32)
    for g, (window, dil) in enumerate(DIL_PATTERNS):
        valid = (j >= 0) & (j <= window // dil)
        for h in range(DIL_HG):
            alibi = -slopes[g, h] * (dil * j).astype(np.float32)
            out[g, h] = np.where(valid, alibi, np.float32(NEG))
    return jnp.asarray(out)


DIL_UNROLL = 4


def _unrolled_loop(lo, hi, fn, unroll=DIL_UNROLL):
    groups = (hi - lo) // unroll
    done = lo
    if groups > 1:

        def step(i, carry):
            for u in range(unroll):
                fn(lo + i * unroll + u)
            return carry

        lax.fori_loop(0, groups, step, 0)
        done = lo + groups * unroll
    for n in range(done, hi):
        fn(n)


def _dil_rows(r, n, count, dil):
    if dil == 1:
        if isinstance(n, int):
            return slice(n * DIL_BLK, (n + count) * DIL_BLK)
        return pl.ds(pl.multiple_of(n * DIL_BLK, DIL_BLK), count * DIL_BLK)
    return pl.ds(n * DIL_BLK * dil + r, count * DIL_BLK, stride=dil)


def _dil_each_block(S, dil, block):
    nb = S // dil // DIL_BLK
    if dil == 1:
        block(0, 0, True)
        _unrolled_loop(1, nb, lambda n: block(0, n, False))
    else:
        for r in range(dil):
            for n in range(nb):
                block(r, n, n == 0)


def _dil_col(g, part, h):
    return (g * 3 + part) * DIL_HG + h


def _dil_fwd_group(dqkv, bias_g, g, dil, name):
    S = dqkv.shape[0]

    def body(bias_ref, q_ref, k_ref, v_ref, o_ref, lse_ref):
        def block(r, n, first):
            cur = _dil_rows(r, n, 1, dil)
            both = cur if first else _dil_rows(r, n - 1, 2, dil)
            b = bias_ref[0][:, DIL_BLK:] if first else bias_ref[0]
            q, kk, vv = q_ref[cur, :].astype(BF16), k_ref[both, :].astype(BF16), v_ref[both, :].astype(BF16)
            s = _dot(q, kk, NT) * DIL_SCALE + b
            m = jnp.max(s, axis=1, keepdims=True)
            e = jnp.exp(s - m)
            l = jnp.sum(e, axis=1, keepdims=True)
            p = e * (1.0 / l)
            o_ref[cur, :] = _dot(p.astype(BF16), vv)
            lse_ref[cur, :] = jnp.broadcast_to(m + jnp.log(l), (DIL_BLK, 128))

        _dil_each_block(S, dil, block)

    def col(part):
        return pl.BlockSpec((S, DIL_HD), lambda h: (0, _dil_col(g, part, h)))

    out = pl.BlockSpec((S, DIL_HD), lambda h: (0, h))
    return pl.pallas_call(
        body,
        name=name,
        grid=(DIL_HG,),
        in_specs=[pl.BlockSpec((1, DIL_BLK, 2 * DIL_BLK), lambda h: (h, 0, 0)), col(0), col(1), col(2)],
        out_specs=[out, out],
        out_shape=[jax.ShapeDtypeStruct((S, DIL_OUT), F32), jax.ShapeDtypeStruct((S, DIL_OUT), F32)],
        compiler_params=pltpu.CompilerParams(dimension_semantics=("parallel",)),
    )(bias_g, dqkv, dqkv, dqkv)


def _dil_combine(os_, ls_, name, tr=512):
    S = os_[0].shape[0]

    def body(o0, o1, o2, l0, l1, l2, out_ref, lse_ref):
        a, b, c = l0[...], l1[...], l2[...]
        m = jnp.maximum(jnp.maximum(a, b), c)
        ea, eb, ec = jnp.exp(a - m), jnp.exp(b - m), jnp.exp(c - m)
        den = ea + eb + ec
        inv = 1.0 / den
        out_ref[...] = (ea * inv) * o0[...] + (eb * inv) * o1[...] + (ec * inv) * o2[...]
        lse_ref[...] = m + jnp.log(den)

    row = pl.BlockSpec((tr, DIL_OUT), lambda i: (i, 0))
    return pl.pallas_call(
        body,
        name=name,
        grid=(S // tr,),
        in_specs=[row] * 6,
        out_specs=[row, row],
        out_shape=[jax.ShapeDtypeStruct((S, DIL_OUT), F32)] * 2,
        compiler_params=pltpu.CompilerParams(dimension_semantics=("parallel",)),
    )(*os_, *ls_)


def _dil_rowdot(dod, od, name, tr=512):
    S = dod.shape[0]

    def body(d_ref, o_ref, dd_ref):
        for h in range(DIL_HG):
            sl = slice(h * 128, (h + 1) * 128)
            sm = jnp.sum(d_ref[:, sl] * o_ref[:, sl], axis=1, keepdims=True)
            dd_ref[:, sl] = jnp.broadcast_to(sm, (tr, 128))

    row = pl.BlockSpec((tr, DIL_OUT), lambda i: (i, 0))
    return pl.pallas_call(
        body,
        name=name,
        grid=(S // tr,),
        in_specs=[row, row],
        out_specs=row,
        out_shape=jax.ShapeDtypeStruct((S, DIL_OUT), F32),
        compiler_params=pltpu.CompilerParams(dimension_semantics=("parallel",)),
    )(dod, od)


def _dil_bwd_group(dqkv, bias_g, dod, dd, lse, grads, g, dil, name):
    S = dqkv.shape[0]

    def body(bias_ref, q_ref, k_ref, v_ref, do_ref, dd_ref, lse_ref, _, out_ref):
        out_ref[1] = jnp.zeros((S, DIL_HD), F32)
        out_ref[2] = jnp.zeros((S, DIL_HD), F32)

        def block(r, n, first):
            cur = _dil_rows(r, n, 1, dil)
            both = cur if first else _dil_rows(r, n - 1, 2, dil)
            b = bias_ref[0][:, DIL_BLK:] if first else bias_ref[0]
            q, kk, vv = q_ref[cur, :].astype(BF16), k_ref[both, :].astype(BF16), v_ref[both, :].astype(BF16)
            do = do_ref[cur, :].astype(BF16)
            s = _dot(q, kk, NT) * DIL_SCALE + b
            p = jnp.exp(s - lse_ref[cur, 0:1])
            dp = _dot(do, vv, NT)
            ds = ((p * (dp - dd_ref[cur, 0:1])) * DIL_SCALE).astype(BF16)
            out_ref[0, cur, :] = _dot(ds, kk)
            out_ref[1, both, :] += _dot(ds, q, TN)
            out_ref[2, both, :] += _dot(p.astype(BF16), do, TN)

        _dil_each_block(S, dil, block)

    def col(part):
        return pl.BlockSpec((S, DIL_HD), lambda h: (0, _dil_col(g, part, h)))

    nat = pl.BlockSpec((S, DIL_HD), lambda h: (0, h))
    return pl.pallas_call(
        body,
        name=name,
        grid=(DIL_HG,),
        in_specs=[pl.BlockSpec((1, DIL_BLK, 2 * DIL_BLK), lambda h: (h, 0, 0)), col(0), col(1), col(2), nat, nat, nat, ANY],
        out_specs=pl.BlockSpec((3, S, DIL_HD), lambda h: (g, 0, h)),
        out_shape=jax.ShapeDtypeStruct(grads.shape, F32),
        input_output_aliases={7: 0},
        compiler_params=pltpu.CompilerParams(dimension_semantics=("parallel",)),
    )(bias_g, dqkv, dqkv, dqkv, dod, dd, lse, grads)


def _merge_fwd(gates, o_a, o_b, name, tr=256):
    S = o_a.shape[0]

    def body(ga_ref, gb_ref, oa_ref, ob_ref, m_ref):
        m_ref[...] = (ga_ref[...] * oa_ref[...] + gb_ref[...] * ob_ref[...]).astype(BF16)

    row = pl.BlockSpec((tr, D_MODEL), lambda i: (i, 0))
    return pl.pallas_call(
        body,
        name=name,
        grid=(S // tr,),
        in_specs=[row, pl.BlockSpec((tr, D_MODEL), lambda i: (i, 1)), row, row],
        out_specs=row,
        out_shape=jax.ShapeDtypeStruct((S, D_MODEL), BF16),
        compiler_params=pltpu.CompilerParams(dimension_semantics=("parallel",)),
    )(gates, gates, o_a, o_b)


def _merge_bwd(dmrg, gates, o_a, o_b, name, tr=256):
    S = o_a.shape[0]

    def body(dm_ref, ga_ref, gb_ref, oa_ref, ob_ref, doa_ref, dob_ref, dga_ref, dgb_ref, dba_ref, dbb_ref):
        dm, ga, gb = dm_ref[...], ga_ref[...], gb_ref[...]
        doa_ref[...] = (dm * ga).astype(BF16)
        dob_ref[...] = (dm * gb).astype(BF16)
        dga = (dm * oa_ref[...]) * (ga * (1.0 - ga))
        dgb = (dm * ob_ref[...]) * (gb * (1.0 - gb))
        dga_ref[...] = dga.astype(BF16)
        dgb_ref[...] = dgb.astype(BF16)
        sa = jnp.sum(dga, axis=0, keepdims=True)
        sb = jnp.sum(dgb, axis=0, keepdims=True)

        @pl.when(pl.program_id(0) == 0)
        def _():
            dba_ref[...] = sa
            dbb_ref[...] = sb

        @pl.when(pl.program_id(0) != 0)
        def _():
            dba_ref[...] += sa
            dbb_ref[...] += sb

    row = pl.BlockSpec((tr, D_MODEL), lambda i: (i, 0))
    row1 = pl.BlockSpec((tr, D_MODEL), lambda i: (i, 1))
    vec = pl.BlockSpec((1, D_MODEL), lambda i: (0, 0))
    outs = pl.pallas_call(
        body,
        name=name,
        grid=(S // tr,),
        in_specs=[row, row, row1, row, row],
        out_specs=[row, row, row, row, vec, vec],
        out_shape=[jax.ShapeDtypeStruct((S, D_MODEL), BF16)] * 4 + [jax.ShapeDtypeStruct((1, D_MODEL), F32)] * 2,
        compiler_params=pltpu.CompilerParams(dimension_semantics=("arbitrary",)),
    )(dmrg, gates, gates, o_a, o_b)
    return outs


CONV_TR = 512
CONV_TC = 512
N_FFC = D_FF_PAD // CONV_TC


def _conv_taps(x, before, w_ref, b_ref):
    x0 = jnp.concatenate([before, x], axis=0)
    x1 = pltpu.roll(x0, 1, 0)
    x2 = pltpu.roll(x0, 2, 0)
    u = ((b_ref[...] + w_ref[0:1, :] * x2) + w_ref[1:2, :] * x1) + w_ref[2:3, :] * x0
    return u, x0, x1, x2


def _prev_halo(tr):
    return lambda i, j: (jnp.maximum(i * (tr // 8) - 1, 0), j)


def _ffn_fwd(u0, cw, cb, name):
    S = u0.shape[0]
    tr, tc = CONV_TR, CONV_TC

    def body(up_ref, gt_ref, hup_ref, hgt_ref, wu_ref, wg_ref, bu_ref, bg_ref, a_ref):
        live = (pl.program_id(0) > 0).astype(F32)
        up = _conv_taps(up_ref[...], hup_ref[...] * live, wu_ref, bu_ref)[0][8:]
        gt = _conv_taps(gt_ref[...], hgt_ref[...] * live, wg_ref, bg_ref)[0][8:]
        a_ref[...] = ((gt * jax.nn.sigmoid(gt)) * up).astype(BF16)

    return pl.pallas_call(
        body,
        name=name,
        grid=(S // tr, N_FFC),
        in_specs=[
            pl.BlockSpec((tr, tc), lambda i, j: (i, j)),
            pl.BlockSpec((tr, tc), lambda i, j: (i, j + N_FFC)),
            pl.BlockSpec((8, tc), _prev_halo(tr)),
            pl.BlockSpec((8, tc), lambda i, j: (jnp.maximum(i * (tr // 8) - 1, 0), j + N_FFC)),
            pl.BlockSpec((8, tc), lambda i, j: (0, j)),
            pl.BlockSpec((8, tc), lambda i, j: (0, j + N_FFC)),
            pl.BlockSpec((1, tc), lambda i, j: (0, j)),
            pl.BlockSpec((1, tc), lambda i, j: (0, j + N_FFC)),
        ],
        out_specs=pl.BlockSpec((tr, tc), lambda i, j: (i, j)),
        out_shape=jax.ShapeDtypeStruct((S, D_FF_PAD), BF16),
        compiler_params=pltpu.CompilerParams(dimension_semantics=("parallel", "parallel")),
    )(u0, u0, u0, u0, cw, cw, cb, cb)


def _ffn_bwd(u0, da, cw, cb, name):
    S = u0.shape[0]
    tr, tc = CONV_TR, CONV_TC
    nrow, te = S // tr, tr + 8

    def body(up_ref, gt_ref, hup_ref, hgt_ref, nup_ref, ngt_ref, da_ref, nda_ref, wu_ref, wg_ref, bu_ref, bg_ref, du0_ref, dcw_ref, dcb_ref):
        i = pl.program_id(1)
        prev_live = (i > 0).astype(F32)
        next_live = (i < nrow - 1).astype(F32)

        def conv(x_ref, nx_ref, h_ref, w_ref, b_ref):
            x = jnp.concatenate([x_ref[...], nx_ref[...] * next_live], axis=0)
            return [t[8:] for t in _conv_taps(x, h_ref[...] * prev_live, w_ref, b_ref)]

        up, xu0, xu1, xu2 = conv(up_ref, nup_ref, hup_ref, wu_ref, bu_ref)
        gt, xg0, xg1, xg2 = conv(gt_ref, ngt_ref, hgt_ref, wg_ref, bg_ref)
        da_v = jnp.concatenate([da_ref[...], nda_ref[...] * next_live], axis=0)
        sg = jax.nn.sigmoid(gt)
        d_up = da_v * (gt * sg)
        d_gt = (da_v * up) * (sg * (1.0 + gt * (1.0 - sg)))
        tap = lax.broadcasted_iota(jnp.int32, (8, tc), 0)

        def finish(half, du, x0, x1, x2, w_ref):
            n1 = pltpu.roll(du, te - 1, 0)
            n2 = pltpu.roll(du, te - 2, 0)
            du0 = (w_ref[2:3, :] * du + w_ref[1:2, :] * n1) + w_ref[0:1, :] * n2
            du0_ref[half] = du0[:tr].astype(BF16)
            d = du[:tr]
            dcw = jnp.where(
                tap == 0,
                jnp.sum(d * x2[:tr], axis=0, keepdims=True),
                jnp.where(tap == 1, jnp.sum(d * x1[:tr], axis=0, keepdims=True), jnp.where(tap == 2, jnp.sum(d * x0[:tr], axis=0, keepdims=True), 0.0)),
            )
            dcb = jnp.sum(d, axis=0, keepdims=True)

            @pl.when(i == 0)
            def _():
                dcw_ref[half] = dcw
                dcb_ref[half] = dcb

            @pl.when(i != 0)
            def _():
                dcw_ref[half] += dcw
                dcb_ref[half] += dcb

        finish(0, d_up, xu0, xu1, xu2, wu_ref)
        finish(1, d_gt, xg0, xg1, xg2, wg_ref)

    def prev8(off):
        return pl.BlockSpec((8, tc), lambda j, i: (jnp.maximum(i * (tr // 8) - 1, 0), j + off))

    def next8(off):
        return pl.BlockSpec((8, tc), lambda j, i: (jnp.minimum((i + 1) * (tr // 8), S // 8 - 1), j + off))

    return pl.pallas_call(
        body,
        name=name,
        grid=(N_FFC, nrow),
        in_specs=[
            pl.BlockSpec((tr, tc), lambda j, i: (i, j)),
            pl.BlockSpec((tr, tc), lambda j, i: (i, j + N_FFC)),
            prev8(0),
            prev8(N_FFC),
            next8(0),
            next8(N_FFC),
            pl.BlockSpec((tr, tc), lambda j, i: (i, j)),
            next8(0),
            pl.BlockSpec((8, tc), lambda j, i: (0, j)),
            pl.BlockSpec((8, tc), lambda j, i: (0, j + N_FFC)),
            pl.BlockSpec((1, tc), lambda j, i: (0, j)),
            pl.BlockSpec((1, tc), lambda j, i: (0, j + N_FFC)),
        ],
        out_specs=[
            pl.BlockSpec((2, tr, tc), lambda j, i: (0, i, j)),
            pl.BlockSpec((2, 8, tc), lambda j, i: (0, 0, j)),
            pl.BlockSpec((2, 1, tc), lambda j, i: (0, 0, j)),
        ],
        out_shape=[
            jax.ShapeDtypeStruct((2, S, D_FF_PAD), BF16),
            jax.ShapeDtypeStruct((2, 8, D_FF_PAD), F32),
            jax.ShapeDtypeStruct((2, 1, D_FF_PAD), F32),
        ],
        compiler_params=pltpu.CompilerParams(dimension_semantics=("parallel", "arbitrary")),
    )(u0, u0, u0, u0, u0, u0, da, da, cw, cw, cb, cb)


ADAMW_BLOCK_BYTES = 3 << 20


def _adamw(w, g, m, v, name):
    R, C = w.shape
    fits = [t for t in range(8, R + 1, 8) if R % t == 0 and t * C * 4 <= ADAMW_BLOCK_BYTES]
    tr = max(fits) if fits else R

    def body(w_ref, g_ref, m_ref, v_ref, d_ref, nm_ref, nv_ref):
        gv = g_ref[...]
        nm = ADAM_B1 * m_ref[...] + (1.0 - ADAM_B1) * gv
        nv = ADAM_B2 * v_ref[...] + (1.0 - ADAM_B2) * (gv * gv)
        m_hat = nm / (1.0 - ADAM_B1**ADAM_STEP)
        v_hat = nv / (1.0 - ADAM_B2**ADAM_STEP)
        d_ref[...] = -ADAM_LR * (m_hat / (jnp.sqrt(v_hat) + ADAM_EPS) + ADAM_WD * w_ref[...])
        nm_ref[...] = nm
        nv_ref[...] = nv

    blk = pl.BlockSpec((tr, C), lambda i: (i, 0))
    return pl.pallas_call(
        body,
        name=name,
        grid=(R // tr,),
        in_specs=[blk] * 4,
        out_specs=[blk] * 3,
        out_shape=[jax.ShapeDtypeStruct((R, C), F32)] * 3,
        compiler_params=pltpu.CompilerParams(dimension_semantics=("parallel",)),
    )(w, g, m, v)


ANY = pl.BlockSpec(memory_space=pl.ANY)


def _pair_exchange(gs, name):
    n = len(gs)

    def body(*refs):
        g_refs, out_refs = refs[:n], refs[n : 2 * n]
        send_sems, recv_sems = refs[2 * n :]
        x, y, c = lax.axis_index("x"), lax.axis_index("y"), lax.axis_index("c")
        copies = [
            pltpu.make_async_remote_copy(
                src_ref=g_refs[a].at[2 * k + (1 - c)],
                dst_ref=out_refs[a].at[k],
                send_sem=send_sems.at[N_CHIP * a + k],
                recv_sem=recv_sems.at[N_CHIP * a + k],
                device_id=(x, y, 1 - c),
                device_id_type=MESH,
            )
            for a in range(n)
            for k in range(N_CHIP)
        ]
        for cp in copies:
            cp.start()
        for cp in copies:
            cp.wait()

    return pl.pallas_call(
        body,
        name=name,
        out_shape=[jax.ShapeDtypeStruct((N_CHIP,) + g.shape[1:], g.dtype) for g in gs],
        in_specs=[ANY] * n,
        out_specs=[ANY] * n,
        scratch_shapes=[pltpu.SemaphoreType.DMA((N_CHIP * n,)), pltpu.SemaphoreType.DMA((N_CHIP * n,))],
    )(*gs)


def _row_tile(rows):
    return max(t for t in range(16, 353, 16) if rows % t == 0)


def _pair_add(g, recv, core, name):
    _, R, C = g.shape
    tr = _row_tile(R)

    def body(core_ref, g_ref, r_ref, o_ref):
        o_ref[...] = (g_ref[...].astype(F32) + r_ref[...].astype(F32)).astype(o_ref.dtype)

    return pl.pallas_call(
        body,
        name=name,
        grid_spec=pltpu.PrefetchScalarGridSpec(
            num_scalar_prefetch=1,
            grid=(N_CHIP, R // tr),
            in_specs=[
                pl.BlockSpec((1, tr, C), lambda k, i, core_ref: (2 * k + core_ref[0], i, 0)),
                pl.BlockSpec((1, tr, C), lambda k, i, core_ref: (k, i, 0)),
            ],
            out_specs=pl.BlockSpec((1, tr, C), lambda k, i, core_ref: (k, i, 0)),
        ),
        out_shape=jax.ShapeDtypeStruct((N_CHIP, R, C), g.dtype),
        compiler_params=pltpu.CompilerParams(dimension_semantics=("parallel", "parallel")),
    )(core, g, recv)


HBM = pl.BlockSpec(memory_space=pltpu.HBM)
SEM = pl.BlockSpec(memory_space=pltpu.SEMAPHORE)
EFFECT = pltpu.SideEffectType.DATAFLOW_SIDE_EFFECTING
RELATIONS = tuple((dx, dy, dc) for dx in (0, 1) for dy in (0, 1) for dc in (0, 1))[1:]


def _related(rel):
    x, y, c = lax.axis_index("x"), lax.axis_index("y"), lax.axis_index("c")
    return (1 - x if rel[0] else x, 1 - y if rel[1] else y, 1 - c if rel[2] else c)


def _dev_index(pos):
    return 4 * pos[0] + 2 * pos[1] + pos[2]


def _peers(chips):
    if chips:
        return [r for r in RELATIONS if not r[2]], N_CHIP, lambda pos: 2 * pos[0] + pos[1]
    return list(RELATIONS), N_DEV, _dev_index


def _exchange_start(srcs, by_slot, after, name, chips=False):
    n = len(srcs)
    rels, slots, slot_of = _peers(chips)
    lands = [lax.empty((slots,) + (s.shape[1:] if by_slot else s.shape), s.dtype) for s in srcs]
    nsem = len(rels) * n

    def body(*refs):
        src_refs, land_refs = refs[:n], refs[n : 2 * n]
        send_sems, recv_sems = refs[2 * n + 1], refs[2 * n + 2]
        token = refs[-1]
        me = slot_of(_related((0, 0, 0)))
        for a in range(n):
            for k, rel in enumerate(rels):
                peer = _related(rel)
                pltpu.make_async_remote_copy(
                    src_ref=src_refs[a].at[slot_of(peer)] if by_slot else src_refs[a],
                    dst_ref=land_refs[a].at[me],
                    send_sem=send_sems.at[len(rels) * a + k],
                    recv_sem=recv_sems.at[len(rels) * a + k],
                    device_id=peer,
                    device_id_type=MESH,
                ).start()
        token[...] = jnp.zeros_like(token)

    def hbm(a):
        return pltpu.HBM(a.shape, a.dtype)

    outs = pl.pallas_call(
        body,
        name=name,
        out_shape=(
            pltpu.SemaphoreType.DMA((nsem,)),
            pltpu.SemaphoreType.DMA((nsem,)),
            *[hbm(s) for s in srcs],
            *[hbm(l) for l in lands],
            jax.ShapeDtypeStruct((8, 128), F32),
        ),
        in_specs=[HBM] * (2 * n) + [ANY],
        out_specs=(SEM, SEM, *[HBM] * (2 * n), pl.BlockSpec(memory_space=pltpu.VMEM)),
        input_output_aliases={i: 2 + i for i in range(2 * n)},
        compiler_params=pltpu.CompilerParams(has_side_effects=EFFECT),
    )(*[pltpu.with_memory_space_constraint(a, pltpu.HBM) for a in list(srcs) + lands], after)
    return (outs[0], outs[1], list(outs[2 : 2 + n]), list(outs[2 + n : 2 + 2 * n])), outs[-1]


def _exchange_wait(handle, by_slot, after, name, chips=False):
    send_sems, recv_sems, srcs, lands = handle
    n = len(srcs)
    rels = _peers(chips)[0]

    def body(*refs):
        src_refs, land_refs = refs[:n], refs[n : 2 * n]
        s_sems, r_sems = refs[2 * n], refs[2 * n + 1]
        for a in range(n):
            for k, rel in enumerate(rels):
                copy = pltpu.make_async_remote_copy(
                    src_ref=src_refs[a].at[0] if by_slot else src_refs[a],
                    dst_ref=land_refs[a].at[0],
                    send_sem=s_sems.at[len(rels) * a + k],
                    recv_sem=r_sems.at[len(rels) * a + k],
                    device_id=_related(rel),
                    device_id_type=MESH,
                )
                copy.wait_send()
                copy.wait_recv()

    outs = pl.pallas_call(
        body,
        name=name,
        out_shape=tuple(pltpu.HBM(a.shape, a.dtype) for a in srcs + lands),
        in_specs=[HBM] * (2 * n) + [SEM, SEM, ANY],
        out_specs=tuple([HBM] * (2 * n)),
        input_output_aliases={i: i for i in range(2 * n)},
        compiler_params=pltpu.CompilerParams(has_side_effects=EFFECT),
    )(*srcs, *lands, send_sems, recv_sems, after)
    return list(outs[:n]), list(outs[n:])


NEAR = ((0, 0, 1), (1, 0, 0), (0, 1, 0), (1, 1, 0))


def _gather2_start(blocks, name):
    n = len(blocks)
    lands = [lax.empty((N_DEV,) + b.shape, b.dtype) for b in blocks]

    def body(*refs):
        src_refs, land_refs = refs[:n], refs[n : 2 * n]
        send_sems, recv_sems, token = refs[2 * n], refs[2 * n + 1], refs[-1]
        me = _dev_index(_related((0, 0, 0)))
        for a in range(n):
            for k, rel in enumerate(NEAR):
                pltpu.make_async_remote_copy(
                    src_ref=src_refs[a],
                    dst_ref=land_refs[a].at[me],
                    send_sem=send_sems.at[len(NEAR) * a + k],
                    recv_sem=recv_sems.at[len(NEAR) * a + k],
                    device_id=_related(rel),
                    device_id_type=MESH,
                ).start()
        token[...] = jnp.zeros_like(token)

    nsem = len(NEAR) * n
    outs = pl.pallas_call(
        body,
        name=name,
        out_shape=(
            pltpu.SemaphoreType.DMA((nsem,)),
            pltpu.SemaphoreType.DMA((nsem,)),
            *[pltpu.HBM(a.shape, a.dtype) for a in list(blocks) + lands],
            jax.ShapeDtypeStruct((8, 128), F32),
        ),
        in_specs=[HBM] * (2 * n),
        out_specs=(SEM, SEM, *[HBM] * (2 * n), pl.BlockSpec(memory_space=pltpu.VMEM)),
        input_output_aliases={i: 2 + i for i in range(2 * n)},
        compiler_params=pltpu.CompilerParams(has_side_effects=EFFECT),
    )(*[pltpu.with_memory_space_constraint(a, pltpu.HBM) for a in list(blocks) + lands])
    return (outs[0], outs[1], list(outs[2 : 2 + n]), list(outs[2 + n : 2 + 2 * n])), outs[-1]


def _gather2_forward(handle, after, name):
    send1, recv1, srcs, lands = handle
    n = len(srcs)

    def body(*refs):
        src_refs, land_refs = refs[:n], refs[n : 2 * n]
        s1, r1 = refs[2 * n], refs[2 * n + 1]
        s2, r2 = refs[-2], refs[-1]
        sibling = _related(NEAR[0])
        for a in range(n):
            for k, rel in enumerate(NEAR):
                first = pltpu.make_async_remote_copy(
                    src_ref=src_refs[a],
                    dst_ref=land_refs[a].at[0],
                    send_sem=s1.at[len(NEAR) * a + k],
                    recv_sem=r1.at[len(NEAR) * a + k],
                    device_id=_related(rel),
                    device_id_type=MESH,
                )
                first.wait_send()
                first.wait_recv()
                if k:
                    slot = land_refs[a].at[_dev_index(_related(rel))]
                    pltpu.make_async_remote_copy(
                        src_ref=slot,
                        dst_ref=slot,
                        send_sem=s2.at[3 * a + k - 1],
                        recv_sem=r2.at[3 * a + k - 1],
                        device_id=sibling,
                        device_id_type=MESH,
                    ).start()

    outs = pl.pallas_call(
        body,
        name=name,
        out_shape=(
            *[pltpu.HBM(a.shape, a.dtype) for a in srcs + lands],
            pltpu.SemaphoreType.DMA((3 * n,)),
            pltpu.SemaphoreType.DMA((3 * n,)),
        ),
        in_specs=[HBM] * (2 * n) + [SEM, SEM, ANY],
        out_specs=(*[HBM] * (2 * n), SEM, SEM),
        input_output_aliases={i: i for i in range(2 * n)},
        compiler_params=pltpu.CompilerParams(has_side_effects=EFFECT),
    )(*srcs, *lands, send1, recv1, after)
    return outs[-2], outs[-1], list(outs[:n]), list(outs[n : 2 * n])


def _gather2_wait(handle, name):
    send2, recv2, srcs, lands = handle
    n = len(srcs)

    def body(*refs):
        land_refs = refs[n : 2 * n]
        s2, r2 = refs[2 * n], refs[2 * n + 1]
        for a in range(n):
            for j in range(3):
                passed = pltpu.make_async_remote_copy(
                    src_ref=land_refs[a].at[0],
                    dst_ref=land_refs[a].at[0],
                    send_sem=s2.at[3 * a + j],
                    recv_sem=r2.at[3 * a + j],
                    device_id=_related(NEAR[0]),
                    device_id_type=MESH,
                )
                passed.wait_send()
                passed.wait_recv()

    outs = pl.pallas_call(
        body,
        name=name,
        out_shape=tuple(pltpu.HBM(a.shape, a.dtype) for a in srcs + lands),
        in_specs=[HBM] * (2 * n) + [SEM, SEM],
        out_specs=tuple([HBM] * (2 * n)),
        input_output_aliases={i: i for i in range(2 * n)},
        compiler_params=pltpu.CompilerParams(has_side_effects=EFFECT),
    )(*srcs, *lands, send2, recv2)
    return list(outs[:n]), list(outs[n:])


def _slot_sum(parts, name):
    n, R, C = parts.shape
    tr = _row_tile(R) if R % 16 == 0 else R

    def body(p_ref, o_ref):
        acc = p_ref[0].astype(F32)
        for k in range(1, n):
            acc = acc + p_ref[k].astype(F32)
        o_ref[...] = acc

    return pl.pallas_call(
        body,
        name=name,
        grid=(R // tr,),
        in_specs=[pl.BlockSpec((n, tr, C), lambda i: (0, i, 0))],
        out_specs=pl.BlockSpec((tr, C), lambda i: (i, 0)),
        out_shape=jax.ShapeDtypeStruct((R, C), F32),
        compiler_params=pltpu.CompilerParams(dimension_semantics=("parallel",)),
    )(parts)


W_IN_TC = 256
W_IN_BOUNDS = (0, LAT, LAT + 3 * DIL_QKV, LAT + 3 * DIL_QKV + D_MODEL, D_IN)


def _dqkv_chunks():
    return [((g * 3 + part) * DIL_OUT, LAT + part * DIL_QKV + g * DIL_OUT) for g in range(DIL_GROUPS) for part in range(3)]


def _w_in_regroup(slots, after, name):
    tc = W_IN_TC

    def body(s_ref, _, lat_ref, dqkv_ref, g_ref, buf):
        for j in range(N_DEV):
            buf[j * IN_ROWS : (j + 1) * IN_ROWS, :] = s_ref[j].astype(F32)[:IN_ROWS, :]
        lat_ref[:LAT, :] = buf[:LAT, :].astype(BF16)
        lat_ref[LAT:, :] = jnp.zeros((LAT_PAD - LAT, tc), BF16)
        for dst, src in _dqkv_chunks():
            dqkv_ref[dst : dst + DIL_OUT, :] = buf[src : src + DIL_OUT, :].astype(BF16)
        g_ref[...] = buf[W_IN_BOUNDS[2] :, :].astype(BF16)

    def col(rows):
        return pl.BlockSpec((rows, tc), lambda k: (0, k))

    return pl.pallas_call(
        body,
        name=name,
        grid=(D_MODEL // tc,),
        in_specs=[pl.BlockSpec((N_DEV, IN_ROWS_PAD, tc), lambda k: (0, 0, k)), pl.BlockSpec((8, 128), lambda k: (0, 0))],
        out_specs=[col(LAT_PAD), col(3 * DIL_QKV), col(2 * D_MODEL)],
        out_shape=[
            jax.ShapeDtypeStruct((LAT_PAD, D_MODEL), BF16),
            jax.ShapeDtypeStruct((3 * DIL_QKV, D_MODEL), BF16),
            jax.ShapeDtypeStruct((2 * D_MODEL, D_MODEL), BF16),
        ],
        scratch_shapes=[pltpu.VMEM((D_IN, tc), F32)],
        compiler_params=pltpu.CompilerParams(dimension_semantics=("parallel",)),
    )(slots, after)


def _w_in_grad_regroup(g_lat, g_dqkv, g_ga, g_gb, name):
    tc = W_IN_TC

    def body(lat_ref, dqkv_ref, ga_ref, gb_ref, o_ref, buf):
        b = W_IN_BOUNDS
        buf[b[0] : b[1], :] = lat_ref[:LAT, :].astype(F32)
        for dst, src in _dqkv_chunks():
            buf[src : src + DIL_OUT, :] = dqkv_ref[dst : dst + DIL_OUT, :].astype(F32)
        buf[b[2] : b[3], :] = ga_ref[...].astype(F32)
        buf[b[3] : b[4], :] = gb_ref[...].astype(F32)
        fill = jnp.zeros((IN_ROWS_PAD - IN_ROWS, tc), F32)
        for j in range(N_DEV):
            o_ref[j] = jnp.concatenate([buf[j * IN_ROWS : (j + 1) * IN_ROWS, :], fill], axis=0).astype(BF16)

    def col(rows):
        return pl.BlockSpec((rows, tc), lambda k: (0, k))

    return pl.pallas_call(
        body,
        name=name,
        grid=(D_MODEL // tc,),
        in_specs=[col(LAT_PAD), col(3 * DIL_QKV), col(D_MODEL), col(D_MODEL)],
        out_specs=pl.BlockSpec((N_DEV, IN_ROWS_PAD, tc), lambda k: (0, 0, k)),
        out_shape=jax.ShapeDtypeStruct((N_DEV, IN_ROWS_PAD, D_MODEL), BF16),
        scratch_shapes=[pltpu.VMEM((D_IN, tc), F32)],
        compiler_params=pltpu.CompilerParams(dimension_semantics=("parallel",)),
    )(g_lat, g_dqkv, g_ga, g_gb)


def _ffn_pad(a, axis):
    a = jnp.moveaxis(a, axis, -1)
    g = a.reshape(a.shape[:-1] + (2 * N_DEV, FF_GROUP))
    g = jnp.pad(g, [(0, 0)] * (g.ndim - 1) + [(0, FF_GROUP_PAD - FF_GROUP)])
    return jnp.moveaxis(g.reshape(a.shape[:-1] + (2 * D_FF_PAD,)), -1, axis)


def _ffn_unpad(a, axis):
    a = jnp.moveaxis(a, axis, -1)
    g = a.reshape(a.shape[:-1] + (2 * N_DEV, FF_GROUP_PAD))[..., :FF_GROUP]
    return jnp.moveaxis(g.reshape(a.shape[:-1] + (2 * D_FF,)), -1, axis)


MISC = (("w_o_mla", (256, 1024)), ("w_o_dil", (256, 512)), ("w_uq", (192, 512)), ("w_ukv", (256, 256)))
BIG_WEIGHTS = ("w_in", "w_up", "w_down", "w_out") + tuple(n for n, _ in MISC)


def _exchange_blocks(w):
    def t(a):
        return a.astype(BF16).T

    up = t(w["w_up"]).reshape(2, FF_GROUP, D_MODEL)
    return [
        jnp.pad(t(w["w_in"]), ((0, IN_ROWS_PAD - IN_ROWS), (0, 0))),
        jnp.pad(up, ((0, 0), (0, FF_GROUP_PAD - FF_GROUP), (0, 0))).reshape(2 * FF_GROUP_PAD, D_MODEL),
        jnp.pad(w["w_down"].astype(BF16), ((0, FF_GROUP_PAD - FF_GROUP), (0, 0))),
        w["w_out"].astype(BF16),
        jnp.concatenate([t(w[n]).reshape(-1, D_MODEL) for n, _ in MISC], axis=0),
    ]


def _misc_split(misc):
    out, off = {}, 0
    for n, (r, c) in MISC:
        rows = r * c // D_MODEL
        out[n] = misc[..., off : off + rows, :].reshape(misc.shape[:-2] + (r, c))
        off += rows
    return out


def _small_matrices(g_misc):
    misc = _misc_split(g_misc)
    uq_t = jnp.pad(misc["w_uq"], ((0, 0), (0, HEAD_PAD - QK_NOPE - QK_ROPE), (0, 0)))
    return {
        "uq_t": uq_t.reshape(MLA_HEADS * HEAD_PAD, Q_LORA),
        "ukv_t": misc["w_ukv"].reshape(MLA_HEADS * HEAD_PAD, KV_LORA),
        "o_mla_t": misc["w_o_mla"].reshape(D_MODEL, MLA_HEADS * V_HEAD),
        "o_dil_t": misc["w_o_dil"].reshape(D_MODEL, DIL_OUT),
    }


def _small_grad_blocks(g):
    uq_t = g["uq_t"].reshape(MLA_HEADS, HEAD_PAD, Q_LORA)[:, : QK_NOPE + QK_ROPE]
    misc = {"w_o_mla": g["o_mla_t"], "w_o_dil": g["o_dil_t"], "w_uq": uq_t, "w_ukv": g["ukv_t"]}
    return [
        g["w_out"].reshape(N_DEV, -1, D_MODEL),
        jnp.concatenate([misc[n].reshape(N_DEV, -1, D_MODEL) for n, _ in MISC], axis=1),
    ]


def _grad_shards(sums):
    s_in, s_out, s_misc, s_up, s_down = sums
    out = {
        "w_in": s_in[:IN_ROWS].T,
        "w_up": s_up.reshape(2, FF_GROUP_PAD, D_MODEL)[:, :FF_GROUP].reshape(2 * FF_GROUP, D_MODEL).T,
        "w_down": s_down[:FF_GROUP],
        "w_out": s_out,
    }
    out.update({n: v.T for n, v in _misc_split(s_misc).items()})
    return out


def _local_step(x, h, tgt, wt, conv_w, small, small_matrices, ffn_weight, send_ffn_grads, send_small_grads, send_w_in_grads):
    S = x.shape[0]
    lat_t, dqkv_t, g_t = wt
    cw = jnp.pad(_ffn_pad(conv_w, 1), ((0, 5), (0, 0)))
    cb = _ffn_pad(small["conv_b"], 1)
    cos_t, sin_t = _rope_tables(S)
    bias = _dil_bias()
    g1, g2, g3 = small["attn_norm_g"], small["ffn_norm_g"], small["final_norm_g"]
    gq, gkv = small["q_norm_g"], small["kv_norm_g"]

    lat = _mm(h, lat_t, "nt", F32, 1024, LAT_PAD, D_MODEL, "proj_lat")
    dqkv = _mm(h, dqkv_t, "nt", F32, 1024, 1536, D_MODEL, "proj_dqkv")
    gates = _mm(h, g_t, "nt", F32, 1024, 1024, D_MODEL, "proj_gates", bias=small["b_gate"], act="sigmoid")
    sm = small_matrices(gates)
    uq_t, ukv_t, o_mla_t, o_dil_t = sm["uq_t"], sm["ukv_t"], sm["o_mla_t"], sm["o_dil_t"]
    cqn, ckvn, kpe = _mla_prep1(lat, gq, gkv, cos_t, sin_t, "mla_prep1")
    q_raw = _mm(cqn, uq_t, "nt", F32, 1024, 1024, Q_LORA, "mla_uq")
    kv = _mm(ckvn, ukv_t, "nt", BF16, 1024, 1024, KV_LORA, "mla_ukv")
    q_att, k_att = _mla_prep2(q_raw, kv, kpe, cos_t, sin_t, "mla_prep2")
    o, lse = _flash2_fwd(q_att, k_att, kv, "mla_flash_fwd")
    o_a = _mm(o, o_mla_t, "nt", F32, 1024, 1024, MLA_HEADS * V_HEAD, "mla_out")

    d_os, d_ls = [], []
    for g, (_, dil) in enumerate(DIL_PATTERNS):
        og, lg = _dil_fwd_group(dqkv, bias[g], g, dil, f"dil_fwd_{g}")
        d_os.append(og)
        d_ls.append(lg)
    od, dil_lse = _dil_combine(d_os, d_ls, "dil_combine")
    o_b = _mm(od, o_dil_t, "nt", F32, 1024, 1024, DIL_OUT, "dil_out")

    mrg = _merge_fwd(gates, o_a, o_b, "merge_fwd")
    w_out = ffn_weight("w_out", mrg)
    x1, h2 = _mm_res_rms(mrg, w_out, x, g2, "mix_out")
    up_t = ffn_weight("up_t", h2)
    u0 = _mm(h2, up_t, "nt", F32, 1024, 1024, D_MODEL, "ffn_up")
    a = _ffn_fwd(u0, cw, cb, "ffn_conv_fwd")
    w_down = ffn_weight("w_down", a)
    x2 = _mm(a, w_down, "nn", F32, 1024, 1024, D_FF_PAD // 2, "ffn_down", res=x1)
    loss_part, dx2, dx2b, dg3 = _final_loss(x2, g3, tgt, "final_loss")

    da = _mm(dx2b, w_down, "nt", F32, 1024, D_FF_PAD // 4, D_MODEL, "ffn_down_dx")
    gw_down = _mm(a, dx2b, "tn", BF16, 512, D_MODEL, S, "ffn_down_dw")
    du0, dcw, dcb = _ffn_bwd(u0, da, cw, cb, "ffn_conv_bwd")
    du0 = du0.reshape(2 * S, D_FF_PAD)
    gw_up_t = _mm(du0, h2, "tn", BF16, 512, D_MODEL, S, "ffn_up_dw", a_halves=2)
    sent = send_ffn_grads(gw_up_t, gw_down)
    dh2 = _mm(du0, up_t, "nn", F32, 1024, 1024, D_FF_PAD // 2, "ffn_up_dx", a_halves=2)
    dx1, dx1b, dg2 = _rms_bwd(dh2, x1, g2 + sent, dx2, "rms_ffn_bwd")

    dmrg = _mm(dx1b, w_out, "nt", F32, 1024, 1024, D_MODEL, "mix_out_dx")
    gw_out = _mm(mrg, dx1b, "tn", BF16, 512, D_MODEL, S, "mix_out_dw")
    do_a, do_b, dga, dgb, dba, dbb = _merge_bwd(dmrg, gates, o_a, o_b, "merge_bwd")

    do = _mm(do_a, o_mla_t, "nn", BF16, 1024, 1024, D_MODEL, "mla_out_dx")
    gw_o_mla_t = _mm(do_a, o, "tn", BF16, 1024, 1024, 1024, "mla_out_dw")
    dod = _mm(do_b, o_dil_t, "nn", F32, 1024, DIL_OUT, D_MODEL, "dil_out_dx")
    gw_o_dil_t = _mm(do_b, od, "tn", BF16, 1024, DIL_OUT, 1024, "dil_out_dw")

    delta = _flash_delta(do, o, "mla_flash_delta")
    lse_row = lse[:, :, 0][:, None, :]
    delta_row = delta[:, :MLA_HEADS].T[:, None, :]
    dq_att, dk_att, dv = _flash2_bwd(q_att, k_att, kv, do, lse_row, delta_row, "mla_flash_bwd")
    dq_raw, dkv, dkpe = _mla_post(dq_att, dk_att, dv, cos_t, sin_t, "mla_post")
    dcqn = _mm(dq_raw, uq_t, "nn", F32, 1024, Q_LORA, MLA_HEADS * HEAD_PAD, "mla_uq_dx")
    gw_uq_t = _mm(dq_raw, cqn, "tn", BF16, 1024, Q_LORA, 1024, "mla_uq_dw")
    dckvn = _mm(dkv, ukv_t, "nn", F32, 1024, KV_LORA, MLA_HEADS * HEAD_PAD, "mla_ukv_dx")
    gw_ukv_t = _mm(dkv, ckvn, "tn", BF16, 1024, KV_LORA, 1024, "mla_ukv_dw")
    sent = send_small_grads({"uq_t": gw_uq_t, "ukv_t": gw_ukv_t, "o_mla_t": gw_o_mla_t, "o_dil_t": gw_o_dil_t, "w_out": gw_out})
    dlat, dgq, dgkv = _lat_bwd(dcqn, dckvn, dkpe, lat, gq + sent, gkv, "lat_bwd")

    dd = _dil_rowdot(dod, od, "dil_rowdot")
    ddqkv = lax.empty((3 * DIL_GROUPS, S, DIL_OUT), F32)
    for g, (_, dil) in enumerate(DIL_PATTERNS):
        ddqkv = _dil_bwd_group(dqkv, bias[g], dod, dd, dil_lse, ddqkv, g, dil, f"dil_bwd_{g}")
    gw_lat_t = _mm(dlat, h, "tn", BF16, LAT_PAD, 1024, S, "proj_lat_dw")
    gw_dqkv_t = _mm(ddqkv.reshape(3 * DIL_GROUPS * S, DIL_OUT), h, "tn", BF16, 512, 1024, S, "proj_dqkv_dw", a_halves=3 * DIL_GROUPS)
    gw_ga_t = _mm(dga, h, "tn", BF16, 512, D_MODEL, S, "proj_ga_dw")
    gw_gb_t = _mm(dgb, h, "tn", BF16, 512, D_MODEL, S, "proj_gb_dw")
    sent = send_w_in_grads(gw_lat_t, gw_dqkv_t, gw_ga_t, gw_gb_t)
    dh = _mm(dlat + sent.astype(BF16), lat_t, "nn", F32, 1024, 1024, LAT_PAD, "proj_lat_dx")
    dh = _stacked_mm(ddqkv, dqkv_t, dh, "proj_dqkv_dx")
    dh = _mm(dga, g_t, "nn", F32, 1024, 1024, D_MODEL, "proj_ga_dx", res=dh)
    grad_x, dg1 = _mm_rms_bwd(dgb, g_t, 1, dh, x, g1, dx1, "proj_gb_dx_rms_attn_bwd")

    small_grads = {
        "attn_norm_g": dg1,
        "b_gate": jnp.concatenate([dba, dbb], axis=1),
        "q_norm_g": dgq,
        "kv_norm_g": dgkv,
        "ffn_norm_g": dg2,
        "conv_b": _ffn_unpad(jnp.concatenate([dcb[0], dcb[1]], axis=1), 1),
        "final_norm_g": dg3,
        "conv_w": _ffn_unpad(jnp.concatenate([dcw[0, :3], dcw[1, :3]], axis=1), 1),
    }
    return loss_part, grad_x, small_grads


SMALL_ORDER = ("attn_norm_g", "b_gate", "q_norm_g", "kv_norm_g", "ffn_norm_g", "conv_b", "final_norm_g", "conv_w")
WEIGHT_ORDER = (
    "attn_norm_g", "w_in", "b_gate", "q_norm_g", "w_uq", "kv_norm_g", "w_ukv", "w_o_mla", "w_o_dil", "w_out",
    "ffn_norm_g", "w_up", "conv_w", "conv_b", "w_down", "final_norm_g",
)


def kernel(x, attn_norm_g, w_in, b_gate, q_norm_g, w_uq, kv_norm_g, w_ukv, w_o_mla, w_o_dil, w_out, ffn_norm_g, w_up, conv_w, conv_b, w_down, final_norm_g, loss_target, m_attn_norm_g, m_w_in, m_b_gate, m_q_norm_g, m_w_uq, m_kv_norm_g, m_w_ukv, m_w_o_mla, m_w_o_dil, m_w_out, m_ffn_norm_g, m_w_up, m_conv_w, m_conv_b, m_w_down, m_final_norm_g, v_attn_norm_g, v_w_in, v_b_gate, v_q_norm_g, v_w_uq, v_kv_norm_g, v_w_ukv, v_w_o_mla, v_w_o_dil, v_w_out, v_ffn_norm_g, v_w_up, v_conv_w, v_conv_b, v_w_down, v_final_norm_g):
    env = dict(locals())
    dev = 4 * lax.axis_index("x") + 2 * lax.axis_index("y") + lax.axis_index("c")
    core = lax.axis_index("c").astype(jnp.int32).reshape(1)

    def two_d(a):
        return a.reshape(-1, a.shape[-1])

    w = {n: two_d(env[n]) for n in WEIGHT_ORDER}
    m = {n: two_d(env["m_" + n]) for n in WEIGHT_ORDER}
    v = {n: two_d(env["v_" + n]) for n in WEIGHT_ORDER}

    chip = 2 * lax.axis_index("x") + lax.axis_index("y")

    def own_slot_in(lands, own, slot=dev):
        return [lax.dynamic_update_slice(l, o[None], (slot, 0, 0)) for l, o in zip(lands, own)]

    b_in = _exchange_blocks(w)[0]
    r, c = CONV_SHARD
    conv = jnp.pad(w["conv_w"].reshape(-1), (0, 8 * SMALL_COLS - r * c)).reshape(8, SMALL_COLS)
    first_level, token = _gather2_start([b_in, conv], "ag_w_in_start")
    tied = {n: w[n] + token[0, 0] for n in BIG_WEIGHTS}
    _, b_up, b_down, b_out, b_misc = _exchange_blocks(tied)
    h = _rms_fwd(x[0], w["attn_norm_g"] + token[0, 0], "rms_attn")
    prepared = b_up[:1, :1] + b_down[:1, :1] + b_out[:1, :1] + b_misc[:1, :1] + h[:1, :1]
    own, lands = _gather2_wait(_gather2_forward(first_level, prepared, "ag_w_in_forward"), "ag_w_in_wait")
    g_in, conv = own_slot_in(lands, own)
    misc_gather, started = _exchange_start([b_misc], False, conv, "ag_small_start")
    ffn_gathers, started2 = {}, started
    for key, block in (("w_out", b_out), ("up_t", b_up), ("w_down", b_down)):
        ffn_gathers[key], started2 = _exchange_start([block], False, started2, f"ag_{key}_start")
    wt = _w_in_regroup(g_in, started2, "w_in_regroup")
    conv = conv.reshape(N_DEV, 8 * SMALL_COLS)[:, : r * c].reshape(N_DEV, r, c)
    conv_w_full = conv.transpose(1, 0, 2).reshape(r, N_DEV * c)
    small = {n: w[n] for n in SMALL_ORDER if n != "conv_w"}

    def small_matrices(after):
        own, lands = _exchange_wait(misc_gather, False, after, "ag_small_wait")
        return _small_matrices(own_slot_in(lands, own)[0])

    def ffn_weight(key, after):
        own, lands = _exchange_wait(ffn_gathers[key], False, after, f"ag_{key}_wait")
        return own_slot_in(lands, own)[0].reshape(-1, D_MODEL)

    reduces = {}

    def send_ffn_grads(gw_up_t, gw_down):
        blocks = [gw_up_t.reshape(N_DEV, 2 * FF_GROUP_PAD, D_MODEL), gw_down.reshape(N_DEV, FF_GROUP_PAD, D_MODEL)]
        reduces["ffn"], token = _exchange_start(blocks, True, gw_down, "rs_ffn_start")
        return token[0, 0]

    def send_small_grads(g):
        reduces["small"], token = _exchange_start(_small_grad_blocks(g), True, g["w_out"], "rs_small_start")
        return token[0, 0]

    def send_w_in_grads(g_lat, g_dqkv, g_ga, g_gb):
        e_in = _w_in_grad_regroup(g_lat, g_dqkv, g_ga, g_gb, "w_in_grad_regroup")
        pair = _pair_add(e_in, _pair_exchange([e_in], "rs_w_in_pair_exchange")[0], core, "rs_w_in_pair_add")
        reduces["w_in"], token = _exchange_start([pair], True, pair, "rs_w_in_start", chips=True)
        return token[0, 0]

    loss_part, grad_x, small_grads = _local_step(
        x[0], h, loss_target[0], wt, conv_w_full, small, small_matrices, ffn_weight,
        send_ffn_grads, send_small_grads, send_w_in_grads,
    )
    loss = lax.psum(loss_part[0, 0], AXES)
    sflat = jnp.concatenate([small_grads[n].reshape(-1) for n in SMALL_ORDER])
    sflat = jnp.pad(sflat, (0, SMALL_ROWS * SMALL_COLS - sflat.shape[0])).reshape(SMALL_ROWS, SMALL_COLS)
    vec_gather, _ = _exchange_start([sflat], False, sflat, "rs_vec_start")

    def finish(key, by_chip, name):
        sent, lands = _exchange_wait(reduces[key], True, grad_x, name + "_wait", chips=by_chip)
        slot = chip if by_chip else dev
        own = [lax.dynamic_index_in_dim(s, slot, 0, keepdims=False) for s in sent]
        return [_slot_sum(p, f"{name}_sum_{i}") for i, p in enumerate(own_slot_in(lands, own, slot))]

    (s_in,) = finish("w_in", True, "rs_w_in")
    s_out, s_misc = finish("small", False, "rs_small")
    s_up, s_down = finish("ffn", False, "rs_ffn")
    gshard = _grad_shards([s_in, s_out, s_misc, s_up, s_down])

    updates = {n: _adamw(w[n], gshard[n], m[n], v[n], "adamw_" + n) for n in BIG_WEIGHTS}

    big_done = sum(updates[n][0][:1, :1] for n in BIG_WEIGHTS)
    own, lands = _exchange_wait(vec_gather, False, big_done, "rs_vec_wait")
    ssum = _slot_sum(own_slot_in(lands, own)[0], "small_sum").reshape(-1)
    gsmall, off = {}, 0
    for n in SMALL_ORDER:
        shape = (3, 2 * D_FF) if n == "conv_w" else w[n].shape
        size = shape[0] * shape[1]
        gsmall[n] = ssum[off : off + size].reshape(shape)
        off += size
    gsmall["conv_w"] = lax.dynamic_slice_in_dim(gsmall["conv_w"], dev * CONV_SHARD[1], CONV_SHARD[1], axis=1)
    updates.update({n: _adamw(w[n], gsmall[n], m[n], v[n], "adamw_" + n) for n in SMALL_ORDER})

    g_all = {**gshard, **gsmall}
    out_g, out_d, out_m, out_v = [], [], [], []
    for n in WEIGHT_ORDER:
        d, nm, nv = updates[n]
        shape = env[n].shape
        out_g.append(g_all[n].reshape(shape))
        out_d.append(d.reshape(shape))
        out_m.append(nm.reshape(shape))
        out_v.append(nv.reshape(shape))
    return (loss, grad_x[None], *out_g, *out_d, *out_m, *out_v)
```

```python
import functools

import jax
import jax.numpy as jnp
import numpy as np
from jax import lax
from jax.experimental import pallas as pl
from jax.experimental.pallas import tpu as pltpu

F32 = jnp.float32
BF16 = jnp.bfloat16

N_DEV = 8
N_CHIP = 4
AXES = ("x", "y", "c")
MESH = pl.DeviceIdType.MESH

D_MODEL = 2048
MLA_HEADS = 8
QK_NOPE = 128
QK_ROPE = 64
V_HEAD = 128
Q_LORA = 512
KV_LORA = 256
ROPE_THETA = 10000.0
HEAD_PAD = 256
DIL_PATTERNS = ((128, 1), (512, 4), (2048, 16))
DIL_GROUPS = 3
DIL_HG = 4
DIL_HEADS = 12
DIL_HD = 128
DIL_BLK = 128
DIL_QKV = DIL_HEADS * DIL_HD
DIL_OUT = DIL_HG * DIL_HD
ALIBI_MAX_BIAS = 8.0
D_FF = 5504
D_FF_PAD = 5632
NORM_EPS = 1e-6
LAT = Q_LORA + KV_LORA + QK_ROPE
LAT_PAD = 896
D_IN = LAT + 3 * DIL_QKV + 2 * D_MODEL
NEG = -1e30

ADAM_LR = 0.001
ADAM_B1 = 0.9
ADAM_B2 = 0.999
ADAM_EPS = 1e-08
ADAM_WD = 0.01
ADAM_STEP = 10

SMALL_ROWS = 56
SMALL_COLS = 1024

IN_ROWS = 1192
IN_ROWS_PAD = 1200
FF_GROUP = D_FF // N_DEV
FF_GROUP_PAD = D_FF_PAD // N_DEV
CONV_SHARD = (3, 1376)

NT = (((1,), (1,)), ((), ()))
TN = (((0,), (0,)), ((), ()))


def _dot(a, b, dims=(((1,), (0,)), ((), ()))):
    return lax.dot_general(a, b, dims, preferred_element_type=F32)


def _mm(a, b, mode, out_dtype, tm, tn, tk, name, bias=None, act=None, res=None, b_koff=0, a_halves=1):
    H = a_halves
    if mode == "nn":
        (M, K), (K2, N) = (a.shape[0] // H, a.shape[1] * H), b.shape
        assert (b_koff + 1) * K <= K2, (name, a.shape, b.shape)
        koff, K2 = b_koff * (K // tk), K
        kper, mrows = a.shape[1] // tk, M // tm
        a_spec = pl.BlockSpec((tm, tk), lambda i, j, k: (i + (k // kper) * mrows, k % kper))
        b_spec = pl.BlockSpec((tk, tn), lambda i, j, k: (k + koff, j))
        dims = (((1,), (0,)), ((), ()))
    elif mode == "nt":
        (M, K), (N, K2) = a.shape, b.shape
        a_spec = pl.BlockSpec((tm, tk), lambda i, j, k: (i, k))
        b_spec = pl.BlockSpec((tn, tk), lambda i, j, k: (j, k))
        dims = NT
    else:
        (K, M), (K2, N) = (a.shape[0] // H, a.shape[1] * H), b.shape
        mper, krows = a.shape[1] // tm, K // tk
        a_spec = pl.BlockSpec((tk, tm), lambda i, j, k: (k + (i // mper) * krows, i % mper))
        b_spec = pl.BlockSpec((tk, tn), lambda i, j, k: (k, j))
        dims = TN
    assert K == K2 and M % tm == 0 and N % tn == 0 and K % tk == 0, (name, a.shape, b.shape)
    nk = K // tk
    has_bias, has_res = bias is not None, res is not None

    def body(*refs):
        refs = list(refs)
        a_ref, b_ref = refs[0], refs[1]
        pos = 2
        bias_ref = res_ref = None
        if has_bias:
            bias_ref = refs[pos]
            pos += 1
        if has_res:
            res_ref = refs[pos]
            pos += 1
        o_ref = refs[pos]
        p = _dot(a_ref[...].astype(BF16), b_ref[...].astype(BF16), dims)

        def finish(acc):
            if has_bias:
                acc = acc + bias_ref[...]
            if act == "sigmoid":
                acc = jax.nn.sigmoid(acc)
            if has_res:
                acc = res_ref[...] + acc
            o_ref[...] = acc.astype(o_ref.dtype)

        if nk == 1:
            finish(p)
        else:
            acc_ref = refs[pos + 1]
            k = pl.program_id(2)

            @pl.when(k == 0)
            def _():
                acc_ref[...] = p

            @pl.when(k != 0)
            def _():
                acc_ref[...] += p

            @pl.when(k == nk - 1)
            def _():
                finish(acc_ref[...])

    in_specs = [a_spec, b_spec]
    args = [a, b]
    if has_bias:
        in_specs.append(pl.BlockSpec((1, tn), lambda i, j, k: (0, j)))
        args.append(bias)
    if has_res:
        in_specs.append(pl.BlockSpec((tm, tn), lambda i, j, k: (i, j)))
        args.append(res)
    return pl.pallas_call(
        body,
        name=name,
        grid=(M // tm, N // tn, nk),
        in_specs=in_specs,
        out_specs=pl.BlockSpec((tm, tn), lambda i, j, k: (i, j)),
        out_shape=jax.ShapeDtypeStruct((M, N), out_dtype),
        scratch_shapes=[pltpu.VMEM((tm, tn), F32)] if nk > 1 else [],
        compiler_params=pltpu.CompilerParams(dimension_semantics=("parallel", "parallel", "arbitrary")),
    )(*args)


def _stacked_mm(pieces, w_t, res, name, tm=512, tn=1024):
    P, M, W = pieces.shape
    N = w_t.shape[1]

    def body(a_ref, b_ref, r_ref, o_ref):
        acc = r_ref[...]
        for p in range(P):
            acc = acc + _dot(a_ref[p].astype(BF16), b_ref[p * W : (p + 1) * W, :])
        o_ref[...] = acc

    tile = pl.BlockSpec((tm, tn), lambda i, j: (i, j))
    return pl.pallas_call(
        body,
        name=name,
        grid=(M // tm, N // tn),
        in_specs=[pl.BlockSpec((P, tm, W), lambda i, j: (0, i, 0)), pl.BlockSpec((P * W, tn), lambda i, j: (0, j)), tile],
        out_specs=tile,
        out_shape=jax.ShapeDtypeStruct((M, N), F32),
        compiler_params=pltpu.CompilerParams(dimension_semantics=("parallel", "parallel")),
    )(pieces, w_t, res)


def _rstd(x):
    return lax.rsqrt(jnp.mean(x * x, axis=-1, keepdims=True) + NORM_EPS)


def _rms_bwd_math(dy, x, g):
    r = _rstd(x)
    xh = x * r
    dg = jnp.sum(dy * xh, axis=0, keepdims=True)
    dxh = dy * g
    dx = r * (dxh - xh * jnp.mean(dxh * xh, axis=-1, keepdims=True))
    return dx, dg


def _rms_fwd(x, g, name, tr=256):
    S, D = x.shape

    def body(x_ref, g_ref, o_ref):
        xv = x_ref[...]
        o_ref[...] = ((xv * _rstd(xv)) * g_ref[...]).astype(o_ref.dtype)

    return pl.pallas_call(
        body,
        name=name,
        grid=(S // tr,),
        in_specs=[pl.BlockSpec((tr, D), lambda i: (i, 0)), pl.BlockSpec((1, D), lambda i: (0, 0))],
        out_specs=pl.BlockSpec((tr, D), lambda i: (i, 0)),
        out_shape=jax.ShapeDtypeStruct((S, D), BF16),
        compiler_params=pltpu.CompilerParams(dimension_semantics=("parallel",)),
    )(x, g)


def _rms_bwd(dy, x, g, res, name, tr=256):
    S, D = x.shape

    def body(dy_ref, x_ref, g_ref, res_ref, dx_ref, dxb_ref, dg_ref):
        dx, dg = _rms_bwd_math(dy_ref[...], x_ref[...], g_ref[...])
        dx = dx + res_ref[...]
        dx_ref[...] = dx
        dxb_ref[...] = dx.astype(BF16)

        @pl.when(pl.program_id(0) == 0)
        def _():
            dg_ref[...] = dg

        @pl.when(pl.program_id(0) != 0)
        def _():
            dg_ref[...] += dg

    row = pl.BlockSpec((tr, D), lambda i: (i, 0))
    vec = pl.BlockSpec((1, D), lambda i: (0, 0))
    return pl.pallas_call(
        body,
        name=name,
        grid=(S // tr,),
        in_specs=[row, row, vec, row],
        out_specs=[row, row, vec],
        out_shape=[jax.ShapeDtypeStruct((S, D), F32), jax.ShapeDtypeStruct((S, D), BF16), jax.ShapeDtypeStruct((1, D), F32)],
        compiler_params=pltpu.CompilerParams(dimension_semantics=("arbitrary",)),
    )(dy, x, g, res)


def _mm_res_rms(a, b, res, g, name, tm=256):
    M, K = a.shape
    D = b.shape[1]

    def body(a_ref, b_ref, res_ref, g_ref, y_ref, h_ref):
        y = res_ref[...] + _dot(a_ref[...], b_ref[...])
        y_ref[...] = y
        h_ref[...] = ((y * _rstd(y)) * g_ref[...]).astype(BF16)

    row = pl.BlockSpec((tm, D), lambda i: (i, 0))
    return pl.pallas_call(
        body,
        name=name,
        grid=(M // tm,),
        in_specs=[pl.BlockSpec((tm, K), lambda i: (i, 0)), pl.BlockSpec((K, D), lambda i: (0, 0)), row, pl.BlockSpec((1, D), lambda i: (0, 0))],
        out_specs=[row, row],
        out_shape=[jax.ShapeDtypeStruct((M, D), F32), jax.ShapeDtypeStruct((M, D), BF16)],
        compiler_params=pltpu.CompilerParams(dimension_semantics=("parallel",)),
    )(a, b, res, g)


def _mm_rms_bwd(a, b, b_koff, dy_part, x, g, res, name, tm=256):
    M, K = a.shape
    D = x.shape[1]

    def body(a_ref, b_ref, dyp_ref, x_ref, g_ref, res_ref, dx_ref, dg_ref):
        dy = dyp_ref[...] + _dot(a_ref[...], b_ref[...])
        dx, dg = _rms_bwd_math(dy, x_ref[...], g_ref[...])
        dx_ref[...] = dx + res_ref[...]

        @pl.when(pl.program_id(0) == 0)
        def _():
            dg_ref[...] = dg

        @pl.when(pl.program_id(0) != 0)
        def _():
            dg_ref[...] += dg

    row = pl.BlockSpec((tm, D), lambda i: (i, 0))
    vec = pl.BlockSpec((1, D), lambda i: (0, 0))
    return pl.pallas_call(
        body,
        name=name,
        grid=(M // tm,),
        in_specs=[pl.BlockSpec((tm, K), lambda i: (i, 0)), pl.BlockSpec((K, D), lambda i: (b_koff, 0)), row, row, vec, row],
        out_specs=[row, vec],
        out_shape=[jax.ShapeDtypeStruct((M, D), F32), jax.ShapeDtypeStruct((1, D), F32)],
        compiler_params=pltpu.CompilerParams(dimension_semantics=("arbitrary",)),
    )(a, b, dy_part, x, g, res)


def _final_loss(x2, g, tgt, name, tr=256):
    S, D = x2.shape

    def body(x_ref, g_ref, t_ref, loss_ref, dx_ref, dxb_ref, dg_ref):
        xv, gv = x_ref[...], g_ref[...]
        y = (xv * _rstd(xv)) * gv
        e = y - t_ref[...]
        part = 0.5 * jnp.sum(jnp.mean(e * e, axis=-1, keepdims=True), axis=0, keepdims=True)
        dx, dg = _rms_bwd_math(e * (1.0 / D), xv, gv)
        dx_ref[...] = dx
        dxb_ref[...] = dx.astype(BF16)
        part = jnp.broadcast_to(part, (1, 128))

        @pl.when(pl.program_id(0) == 0)
        def _():
            dg_ref[...] = dg
            loss_ref[...] = part

        @pl.when(pl.program_id(0) != 0)
        def _():
            dg_ref[...] += dg
            loss_ref[...] += part

    row = pl.BlockSpec((tr, D), lambda i: (i, 0))
    vec = pl.BlockSpec((1, D), lambda i: (0, 0))
    return pl.pallas_call(
        body,
        name=name,
        grid=(S // tr,),
        in_specs=[row, vec, row],
        out_specs=[pl.BlockSpec((1, 128), lambda i: (0, 0)), row, row, vec],
        out_shape=[
            jax.ShapeDtypeStruct((1, 128), F32),
            jax.ShapeDtypeStruct((S, D), F32),
            jax.ShapeDtypeStruct((S, D), BF16),
            jax.ShapeDtypeStruct((1, D), F32),
        ],
        compiler_params=pltpu.CompilerParams(dimension_semantics=("arbitrary",)),
    )(x2, g, tgt)


def _rope_tables(S):
    pos = jnp.arange(S, dtype=F32)
    inv_freq = ROPE_THETA ** (-jnp.arange(0, QK_ROPE, 2, dtype=F32) / QK_ROPE)
    ang = pos[:, None] * inv_freq[None, :]
    cos, sin = jnp.cos(ang), jnp.sin(ang)
    zero = jnp.zeros((S, 128 - QK_ROPE), F32)
    return jnp.concatenate([cos, cos, zero], axis=1), jnp.concatenate([-sin, sin, zero], axis=1)


def _rope_tile(x, cos_t, sin_t):
    lane = lax.broadcasted_iota(jnp.int32, x.shape, 1)
    partner = jnp.where(lane < QK_ROPE // 2, pltpu.roll(x, 128 - QK_ROPE // 2, 1), pltpu.roll(x, QK_ROPE // 2, 1))
    return x * cos_t + partner * sin_t


def _mla_prep1(lat, gq, gkv, cos_t, sin_t, name, tr=256):
    S = lat.shape[0]

    def body(lat_ref, gq_ref, gkv_ref, cos_ref, sin_ref, cq_ref, ckv_ref, kpe_ref):
        cq = lat_ref[:, :Q_LORA]
        ckv = lat_ref[:, Q_LORA : Q_LORA + KV_LORA]
        cq_ref[...] = ((cq * _rstd(cq)) * gq_ref[...]).astype(BF16)
        ckv_ref[...] = ((ckv * _rstd(ckv)) * gkv_ref[...]).astype(BF16)
        kpe_ref[...] = _rope_tile(lat_ref[:, Q_LORA + KV_LORA :], cos_ref[...], sin_ref[...]).astype(BF16)

    def row(n):
        return pl.BlockSpec((tr, n), lambda i: (i, 0))

    def vec(n):
        return pl.BlockSpec((1, n), lambda i: (0, 0))

    return pl.pallas_call(
        body,
        name=name,
        grid=(S // tr,),
        in_specs=[row(LAT_PAD), vec(Q_LORA), vec(KV_LORA), row(128), row(128)],
        out_specs=[row(Q_LORA), row(KV_LORA), row(128)],
        out_shape=[
            jax.ShapeDtypeStruct((S, Q_LORA), BF16),
            jax.ShapeDtypeStruct((S, KV_LORA), BF16),
            jax.ShapeDtypeStruct((S, 128), BF16),
        ],
        compiler_params=pltpu.CompilerParams(dimension_semantics=("parallel",)),
    )(lat, gq, gkv, cos_t, sin_t)


def _mla_prep2(q_raw, kv, kpe, cos_t, sin_t, name, tr=256):
    S = q_raw.shape[0]
    W = MLA_HEADS * HEAD_PAD

    def body(q_ref, kv_ref, kpe_ref, cos_ref, sin_ref, qa_ref, ka_ref):
        cos_v, sin_v, kpe_v = cos_ref[...], sin_ref[...], kpe_ref[...]
        for h in range(MLA_HEADS):
            lo = h * HEAD_PAD
            qa_ref[:, lo : lo + 128] = q_ref[:, lo : lo + 128].astype(BF16)
            qa_ref[:, lo + 128 : lo + 256] = _rope_tile(q_ref[:, lo + 128 : lo + 256], cos_v, sin_v).astype(BF16)
            ka_ref[:, lo : lo + 128] = kv_ref[:, lo : lo + 128]
            ka_ref[:, lo + 128 : lo + 256] = kpe_v

    def row(n):
        return pl.BlockSpec((tr, n), lambda i: (i, 0))

    return pl.pallas_call(
        body,
        name=name,
        grid=(S // tr,),
        in_specs=[row(W), row(W), row(128), row(128), row(128)],
        out_specs=[row(W), row(W)],
        out_shape=[jax.ShapeDtypeStruct((S, W), BF16), jax.ShapeDtypeStruct((S, W), BF16)],
        compiler_params=pltpu.CompilerParams(dimension_semantics=("parallel",)),
    )(q_raw, kv, kpe, cos_t, sin_t)


def _mla_post(dq_att, dk_att, dv, cos_t, sin_t, name, tr=256):
    S = dq_att.shape[0]
    W = MLA_HEADS * HEAD_PAD

    def body(dq_ref, dk_ref, dv_ref, cos_ref, sin_ref, dqr_ref, dkv_ref, dkpe_ref):
        cos_v, nsin_v = cos_ref[...], -sin_ref[...]
        kpe = jnp.zeros((tr, 128), F32)
        for h in range(MLA_HEADS):
            lo = h * HEAD_PAD
            dqr_ref[:, lo : lo + 128] = dq_ref[:, lo : lo + 128].astype(BF16)
            dqr_ref[:, lo + 128 : lo + 256] = _rope_tile(dq_ref[:, lo + 128 : lo + 256], cos_v, nsin_v).astype(BF16)
            dkv_ref[:, lo : lo + 128] = dk_ref[:, lo : lo + 128].astype(BF16)
            dkv_ref[:, lo + 128 : lo + 256] = dv_ref[:, h * 128 : (h + 1) * 128].astype(BF16)
            kpe = kpe + dk_ref[:, lo + 128 : lo + 256]
        dkpe_ref[...] = _rope_tile(kpe, cos_v, nsin_v)

    def row(n):
        return pl.BlockSpec((tr, n), lambda i: (i, 0))

    return pl.pallas_call(
        body,
        name=name,
        grid=(S // tr,),
        in_specs=[row(W), row(W), row(MLA_HEADS * V_HEAD), row(128), row(128)],
        out_specs=[row(W), row(W), row(128)],
        out_shape=[jax.ShapeDtypeStruct((S, W), BF16), jax.ShapeDtypeStruct((S, W), BF16), jax.ShapeDtypeStruct((S, 128), F32)],
        compiler_params=pltpu.CompilerParams(dimension_semantics=("parallel",)),
    )(dq_att, dk_att, dv, cos_t, sin_t)


def _lat_bwd(dcqn, dckvn, dkpe, lat, gq, gkv, name, tr=256):
    S = lat.shape[0]

    def body(dcq_ref, dckv_ref, dkpe_ref, lat_ref, gq_ref, gkv_ref, dlat_ref, dgq_ref, dgkv_ref):
        dq, dgq = _rms_bwd_math(dcq_ref[...], lat_ref[:, :Q_LORA], gq_ref[...])
        dkv, dgkv = _rms_bwd_math(dckv_ref[...], lat_ref[:, Q_LORA : Q_LORA + KV_LORA], gkv_ref[...])
        dlat_ref[:, :Q_LORA] = dq.astype(BF16)
        dlat_ref[:, Q_LORA : Q_LORA + KV_LORA] = dkv.astype(BF16)
        dlat_ref[:, Q_LORA + KV_LORA :] = dkpe_ref[...].astype(BF16)

        @pl.when(pl.program_id(0) == 0)
        def _():
            dgq_ref[...] = dgq
            dgkv_ref[...] = dgkv

        @pl.when(pl.program_id(0) != 0)
        def _():
            dgq_ref[...] += dgq
            dgkv_ref[...] += dgkv

    def row(n):
        return pl.BlockSpec((tr, n), lambda i: (i, 0))

    def vec(n):
        return pl.BlockSpec((1, n), lambda i: (0, 0))

    return pl.pallas_call(
        body,
        name=name,
        grid=(S // tr,),
        in_specs=[row(Q_LORA), row(KV_LORA), row(128), row(LAT_PAD), vec(Q_LORA), vec(KV_LORA)],
        out_specs=[row(LAT_PAD), vec(Q_LORA), vec(KV_LORA)],
        out_shape=[
            jax.ShapeDtypeStruct((S, LAT_PAD), BF16),
            jax.ShapeDtypeStruct((1, Q_LORA), F32),
            jax.ShapeDtypeStruct((1, KV_LORA), F32),
        ],
        compiler_params=pltpu.CompilerParams(dimension_semantics=("arbitrary",)),
    )(dcqn, dckvn, dkpe, lat, gq, gkv)


MLA_SCALE = (QK_NOPE + QK_ROPE) ** -0.5
LOG2E = 1.4426950408889634
MLA_C2 = MLA_SCALE * LOG2E
FLASH_T = 1024


def _causal_pairs(n, by_key):
    pairs = [(i, j) for j in range(n) for i in range(j, n)] if by_key else [(i, j) for i in range(n) for j in range(i + 1)]
    return jnp.asarray([p[0] for p in pairs], jnp.int32), jnp.asarray([p[1] for p in pairs], jnp.int32)


def _causal_mask(shape, shift, keys_first=False):
    q_axis, k_axis = (1, 0) if keys_first else (0, 1)
    return lax.broadcasted_iota(jnp.int32, shape, k_axis) <= lax.broadcasted_iota(jnp.int32, shape, q_axis) + shift


def _lanes(x, n):
    return jnp.tile(x, (1, n // 128))


def _flash_grid(npairs, in_specs, out_specs, scratch):
    return pltpu.PrefetchScalarGridSpec(
        num_scalar_prefetch=2, grid=(MLA_HEADS, npairs), in_specs=in_specs, out_specs=out_specs, scratch_shapes=scratch
    )


def _flash2_fwd(q_att, k_att, kv, name, t=FLASH_T):
    S = q_att.shape[0]
    half = t // 2
    qi_tab, kj_tab = _causal_pairs(S // t, by_key=False)

    def body(qi_ref, kj_ref, q_ref, k_ref, v_ref, o_ref, lse_ref, m_sc, l_sc, acc_sc):
        step = pl.program_id(1)
        qi, kj = qi_ref[step], kj_ref[step]

        @pl.when(kj == 0)
        def _():
            m_sc[...] = jnp.full((t, 128), NEG, F32)
            l_sc[...] = jnp.zeros((t, 128), F32)
            acc_sc[...] = jnp.zeros((t, V_HEAD), F32)

        def update(rows, s, v):
            m_prev = m_sc[rows, :]
            m_new = jnp.maximum(m_prev, jnp.max(s, axis=1, keepdims=True))
            p = jnp.exp2((s - _lanes(m_new, s.shape[1])) * MLA_C2)
            alpha = jnp.exp2((m_prev - m_new) * MLA_C2)
            l_sc[rows, :] = alpha * l_sc[rows, :] + jnp.sum(p, axis=1, keepdims=True)
            acc_sc[rows, :] = alpha * acc_sc[rows, :] + _dot(p.astype(BF16), v)
            m_sc[rows, :] = m_new

        @pl.when(kj < qi)
        def _():
            update(slice(0, t), _dot(q_ref[...], k_ref[...], NT), v_ref[...])

        @pl.when(kj == qi)
        def _():
            top = _dot(q_ref[:half, :], k_ref[:half, :], NT)
            update(slice(0, half), jnp.where(_causal_mask(top.shape, 0), top, NEG), v_ref[:half, :])
            bot = _dot(q_ref[half:, :], k_ref[...], NT)
            update(slice(half, t), jnp.where(_causal_mask(bot.shape, half), bot, NEG), v_ref[...])
            l = l_sc[...]
            o_ref[...] = acc_sc[...] / l
            lse_ref[0] = m_sc[...] * MLA_SCALE + jnp.log(l)

    return pl.pallas_call(
        body,
        name=name,
        grid_spec=_flash_grid(
            qi_tab.shape[0],
            [
                pl.BlockSpec((t, HEAD_PAD), lambda h, p, qi, kj: (qi[p], h)),
                pl.BlockSpec((t, HEAD_PAD), lambda h, p, qi, kj: (kj[p], h)),
                pl.BlockSpec((t, V_HEAD), lambda h, p, qi, kj: (kj[p], 2 * h + 1)),
            ],
            [
                pl.BlockSpec((t, V_HEAD), lambda h, p, qi, kj: (qi[p], h)),
                pl.BlockSpec((1, t, 128), lambda h, p, qi, kj: (h, qi[p], 0)),
            ],
            [pltpu.VMEM((t, 128), F32), pltpu.VMEM((t, 128), F32), pltpu.VMEM((t, V_HEAD), F32)],
        ),
        out_shape=[jax.ShapeDtypeStruct((S, MLA_HEADS * V_HEAD), F32), jax.ShapeDtypeStruct((MLA_HEADS, S, 128), F32)],
        compiler_params=pltpu.CompilerParams(dimension_semantics=("parallel", "arbitrary")),
    )(qi_tab, kj_tab, q_att, k_att, kv)


def _flash_delta(do, o, name, tr=512):
    S = o.shape[0]

    def body(do_ref, o_ref, d_ref):
        lane = lax.broadcasted_iota(jnp.int32, (tr, 128), 1)
        acc = jnp.zeros((tr, 128), F32)
        for h in range(MLA_HEADS):
            sl = slice(h * V_HEAD, (h + 1) * V_HEAD)
            acc = jnp.where(lane == h, jnp.sum(do_ref[:, sl].astype(F32) * o_ref[:, sl], axis=1, keepdims=True), acc)
        d_ref[...] = acc

    row = pl.BlockSpec((tr, MLA_HEADS * V_HEAD), lambda i: (i, 0))
    return pl.pallas_call(
        body,
        name=name,
        grid=(S // tr,),
        in_specs=[row, row],
        out_specs=pl.BlockSpec((tr, 128), lambda i: (i, 0)),
        out_shape=jax.ShapeDtypeStruct((S, 128), F32),
        compiler_params=pltpu.CompilerParams(dimension_semantics=("parallel",)),
    )(do, o)


def _flash2_bwd(q_att, k_att, kv, do, lse_row, delta_row, name, t=FLASH_T):
    S = q_att.shape[0]
    n = S // t
    qi_tab, kj_tab = _causal_pairs(n, by_key=True)
    last = qi_tab.shape[0] - 1
    half = t // 2

    def body(qi_ref, kj_ref, q_ref, k_ref, v_ref, do_ref, lse_ref, dl_ref, dq_ref, dk_ref, dv_ref, dk_sc, dv_sc):
        step = pl.program_id(1)
        qi, kj = qi_ref[step], kj_ref[step]

        @pl.when(step == 0)
        def _():
            dq_ref[...] = jnp.zeros((S, HEAD_PAD), F32)

        def update(k0, q0, st):
            nk, nq = st.shape
            kr, qr = slice(k0, k0 + nk), slice(q0, q0 + nq)
            q, do_v = q_ref[qr, :], do_ref[qr, :]
            pt = jnp.exp2(st * MLA_C2 - lse_ref[0][:, qr] * LOG2E)
            dv_sc[kr, :] += _dot(pt.astype(BF16), do_v)
            dpt = _dot(v_ref[kr, :], do_v, NT)
            dst = (pt * (dpt - dl_ref[0][:, qr])).astype(BF16)
            dk_sc[kr, :] += _dot(dst, q)
            rows = pl.ds(pl.multiple_of(qi * t + q0, half), nq)
            dq_ref[rows, :] += _dot(dst, k_ref[kr, :], TN)

        @pl.when(qi == kj)
        def _():
            dk_sc[...] = jnp.zeros((t, HEAD_PAD), F32)
            dv_sc[...] = jnp.zeros((t, V_HEAD), F32)
            top = _dot(k_ref[:half, :], q_ref[...], NT)
            update(0, 0, jnp.where(_causal_mask(top.shape, 0, keys_first=True), top, NEG))
            bot = _dot(k_ref[half:, :], q_ref[half:, :], NT)
            update(half, half, jnp.where(_causal_mask(bot.shape, 0, keys_first=True), bot, NEG))

        @pl.when(qi > kj)
        def _():
            update(0, 0, _dot(k_ref[...], q_ref[...], NT))

        @pl.when(qi == n - 1)
        def _():
            dk_ref[...] = dk_sc[...] * MLA_SCALE
            dv_ref[...] = dv_sc[...]

        @pl.when(step == last)
        def _():
            dq_ref[...] = dq_ref[...] * MLA_SCALE

    qrow = lambda h, p, qi, kj: (qi[p], h)
    krow = lambda h, p, qi, kj: (kj[p], h)
    stat = pl.BlockSpec((1, 1, t), lambda h, p, qi, kj: (h, 0, qi[p]))
    return pl.pallas_call(
        body,
        name=name,
        grid_spec=_flash_grid(
            qi_tab.shape[0],
            [
                pl.BlockSpec((t, HEAD_PAD), qrow),
                pl.BlockSpec((t, HEAD_PAD), krow),
                pl.BlockSpec((t, V_HEAD), lambda h, p, qi, kj: (kj[p], 2 * h + 1)),
                pl.BlockSpec((t, V_HEAD), qrow),
                stat,
                stat,
            ],
            [
                pl.BlockSpec((S, HEAD_PAD), lambda h, p, qi, kj: (0, h)),
                pl.BlockSpec((t, HEAD_PAD), krow),
                pl.BlockSpec((t, V_HEAD), krow),
            ],
            [pltpu.VMEM((t, HEAD_PAD), F32), pltpu.VMEM((t, V_HEAD), F32)],
        ),
        out_shape=[
            jax.ShapeDtypeStruct((S, MLA_HEADS * HEAD_PAD), F32),
            jax.ShapeDtypeStruct((S, MLA_HEADS * HEAD_PAD), F32),
            jax.ShapeDtypeStruct((S, MLA_HEADS * V_HEAD), F32),
        ],
        compiler_params=pltpu.CompilerParams(dimension_semantics=("parallel", "arbitrary")),
    )(qi_tab, kj_tab, q_att, k_att, kv, do, lse_row, delta_row)


DIL_SCALE = DIL_HD**-0.5


def _dil_bias():
    slopes = 2.0 ** (-ALIBI_MAX_BIAS * np.arange(1, DIL_HEADS + 1, dtype=np.float64) / DIL_HEADS)
    slopes = slopes.astype(np.float32).reshape(DIL_GROUPS, DIL_HG)
    p = np.arange(DIL_BLK)[:, None]
    kidx = np.arange(2 * DIL_BLK)[None, :]
    j = p + DIL_BLK - kidx
    out = np.zeros((DIL_GROUPS, DIL_HG, DIL_BLK, 2 * DIL_BLK), np.float32)
    for g, (window, dil) in enumerate(DIL_PATTERNS):
        valid = (j >= 0) & (j <= window // dil)
        for h in range(DIL_HG):
            alibi = -slopes[g, h] * (dil * j).astype(np.float32)
            out[g, h] = np.where(valid, alibi, np.float32(NEG))
    return jnp.asarray(out)


DIL_UNROLL = 4


def _unrolled_loop(lo, hi, fn, unroll=DIL_UNROLL):
    groups = (hi - lo) // unroll
    done = lo
    if groups > 1:

        def step(i, carry):
            for u in range(unroll):
                fn(lo + i * unroll + u)
            return carry

        lax.fori_loop(0, groups, step, 0)
        done = lo + groups * unroll
    for n in range(done, hi):
        fn(n)


def _dil_rows(r, n, count, dil):
    if dil == 1:
        if isinstance(n, int):
            return slice(n * DIL_BLK, (n + count) * DIL_BLK)
        return pl.ds(pl.multiple_of(n * DIL_BLK, DIL_BLK), count * DIL_BLK)
    return pl.ds(n * DIL_BLK * dil + r, count * DIL_BLK, stride=dil)


def _dil_each_block(S, dil, block):
    nb = S // dil // DIL_BLK
    if dil == 1:
        block(0, 0, True)
        _unrolled_loop(1, nb, lambda n: block(0, n, False))
    else:
        for r in range(dil):
            for n in range(nb):
                block(r, n, n == 0)


def _dil_col(g, part, h):
    return (g * 3 + part) * DIL_HG + h


def _dil_fwd_group(dqkv, bias_g, g, dil, name):
    S = dqkv.shape[0]

    def body(bias_ref, q_ref, k_ref, v_ref, o_ref, lse_ref):
        def block(r, n, first):
            cur = _dil_rows(r, n, 1, dil)
            both = cur if first else _dil_rows(r, n - 1, 2, dil)
            b = bias_ref[0][:, DIL_BLK:] if first else bias_ref[0]
            q, kk, vv = q_ref[cur, :].astype(BF16), k_ref[both, :].astype(BF16), v_ref[both, :].astype(BF16)
            s = _dot(q, kk, NT) * DIL_SCALE + b
            m = jnp.max(s, axis=1, keepdims=True)
            e = jnp.exp(s - m)
            l = jnp.sum(e, axis=1, keepdims=True)
            p = e * (1.0 / l)
            o_ref[cur, :] = _dot(p.astype(BF16), vv)
            lse_ref[cur, :] = jnp.broadcast_to(m + jnp.log(l), (DIL_BLK, 128))

        _dil_each_block(S, dil, block)

    def col(part):
        return pl.BlockSpec((S, DIL_HD), lambda h: (0, _dil_col(g, part, h)))

    out = pl.BlockSpec((S, DIL_HD), lambda h: (0, h))
    return pl.pallas_call(
        body,
        name=name,
        grid=(DIL_HG,),
        in_specs=[pl.BlockSpec((1, DIL_BLK, 2 * DIL_BLK), lambda h: (h, 0, 0)), col(0), col(1), col(2)],
        out_specs=[out, out],
        out_shape=[jax.ShapeDtypeStruct((S, DIL_OUT), F32), jax.ShapeDtypeStruct((S, DIL_OUT), F32)],
        compiler_params=pltpu.CompilerParams(dimension_semantics=("parallel",)),
    )(bias_g, dqkv, dqkv, dqkv)


def _dil_combine(os_, ls_, name, tr=512):
    S = os_[0].shape[0]

    def body(o0, o1, o2, l0, l1, l2, out_ref, lse_ref):
        a, b, c = l0[...], l1[...], l2[...]
        m = jnp.maximum(jnp.maximum(a, b), c)
        ea, eb, ec = jnp.exp(a - m), jnp.exp(b - m), jnp.exp(c - m)
        den = ea + eb + ec
        inv = 1.0 / den
        out_ref[...] = (ea * inv) * o0[...] + (eb * inv) * o1[...] + (ec * inv) * o2[...]
        lse_ref[...] = m + jnp.log(den)

    row = pl.BlockSpec((tr, DIL_OUT), lambda i: (i, 0))
    return pl.pallas_call(
        body,
        name=name,
        grid=(S // tr,),
        in_specs=[row] * 6,
        out_specs=[row, row],
        out_shape=[jax.ShapeDtypeStruct((S, DIL_OUT), F32)] * 2,
        compiler_params=pltpu.CompilerParams(dimension_semantics=("parallel",)),
    )(*os_, *ls_)


def _dil_rowdot(dod, od, name, tr=512):
    S = dod.shape[0]

    def body(d_ref, o_ref, dd_ref):
        for h in range(DIL_HG):
            sl = slice(h * 128, (h + 1) * 128)
            sm = jnp.sum(d_ref[:, sl] * o_ref[:, sl], axis=1, keepdims=True)
            dd_ref[:, sl] = jnp.broadcast_to(sm, (tr, 128))

    row = pl.BlockSpec((tr, DIL_OUT), lambda i: (i, 0))
    return pl.pallas_call(
        body,
        name=name,
        grid=(S // tr,),
        in_specs=[row, row],
        out_specs=row,
        out_shape=jax.ShapeDtypeStruct((S, DIL_OUT), F32),
        compiler_params=pltpu.CompilerParams(dimension_semantics=("parallel",)),
    )(dod, od)


def _dil_bwd_group(dqkv, bias_g, dod, dd, lse, grads, g, dil, name):
    S = dqkv.shape[0]

    def body(bias_ref, q_ref, k_ref, v_ref, do_ref, dd_ref, lse_ref, _, out_ref):
        out_ref[1] = jnp.zeros((S, DIL_HD), F32)
        out_ref[2] = jnp.zeros((S, DIL_HD), F32)

        def block(r, n, first):
            cur = _dil_rows(r, n, 1, dil)
            both = cur if first else _dil_rows(r, n - 1, 2, dil)
            b = bias_ref[0][:, DIL_BLK:] if first else bias_ref[0]
            q, kk, vv = q_ref[cur, :].astype(BF16), k_ref[both, :].astype(BF16), v_ref[both, :].astype(BF16)
            do = do_ref[cur, :].astype(BF16)
            s = _dot(q, kk, NT) * DIL_SCALE + b
            p = jnp.exp(s - lse_ref[cur, 0:1])
            dp = _dot(do, vv, NT)
            ds = ((p * (dp - dd_ref[cur, 0:1])) * DIL_SCALE).astype(BF16)
            out_ref[0, cur, :] = _dot(ds, kk)
            out_ref[1, both, :] += _dot(ds, q, TN)
            out_ref[2, both, :] += _dot(p.astype(BF16), do, TN)

        _dil_each_block(S, dil, block)

    def col(part):
        return pl.BlockSpec((S, DIL_HD), lambda h: (0, _dil_col(g, part, h)))

    nat = pl.BlockSpec((S, DIL_HD), lambda h: (0, h))
    return pl.pallas_call(
        body,
        name=name,
        grid=(DIL_HG,),
        in_specs=[pl.BlockSpec((1, DIL_BLK, 2 * DIL_BLK), lambda h: (h, 0, 0)), col(0), col(1), col(2), nat, nat, nat, ANY],
        out_specs=pl.BlockSpec((3, S, DIL_HD), lambda h: (g, 0, h)),
        out_shape=jax.ShapeDtypeStruct(grads.shape, F32),
        input_output_aliases={7: 0},
        compiler_params=pltpu.CompilerParams(dimension_semantics=("parallel",)),
    )(bias_g, dqkv, dqkv, dqkv, dod, dd, lse, grads)


def _merge_fwd(gates, o_a, o_b, name, tr=256):
    S = o_a.shape[0]

    def body(ga_ref, gb_ref, oa_ref, ob_ref, m_ref):
        m_ref[...] = (ga_ref[...] * oa_ref[...] + gb_ref[...] * ob_ref[...]).astype(BF16)

    row = pl.BlockSpec((tr, D_MODEL), lambda i: (i, 0))
    return pl.pallas_call(
        body,
        name=name,
        grid=(S // tr,),
        in_specs=[row, pl.BlockSpec((tr, D_MODEL), lambda i: (i, 1)), row, row],
        out_specs=row,
        out_shape=jax.ShapeDtypeStruct((S, D_MODEL), BF16),
        compiler_params=pltpu.CompilerParams(dimension_semantics=("parallel",)),
    )(gates, gates, o_a, o_b)


def _merge_bwd(dmrg, gates, o_a, o_b, name, tr=256):
    S = o_a.shape[0]

    def body(dm_ref, ga_ref, gb_ref, oa_ref, ob_ref, doa_ref, dob_ref, dga_ref, dgb_ref, dba_ref, dbb_ref):
        dm, ga, gb = dm_ref[...], ga_ref[...], gb_ref[...]
        doa_ref[...] = (dm * ga).astype(BF16)
        dob_ref[...] = (dm * gb).astype(BF16)
        dga = (dm * oa_ref[...]) * (ga * (1.0 - ga))
        dgb = (dm * ob_ref[...]) * (gb * (1.0 - gb))
        dga_ref[...] = dga.astype(BF16)
        dgb_ref[...] = dgb.astype(BF16)
        sa = jnp.sum(dga, axis=0, keepdims=True)
        sb = jnp.sum(dgb, axis=0, keepdims=True)

        @pl.when(pl.program_id(0) == 0)
        def _():
            dba_ref[...] = sa
            dbb_ref[...] = sb

        @pl.when(pl.program_id(0) != 0)
        def _():
            dba_ref[...] += sa
            dbb_ref[...] += sb

    row = pl.BlockSpec((tr, D_MODEL), lambda i: (i, 0))
    row1 = pl.BlockSpec((tr, D_MODEL), lambda i: (i, 1))
    vec = pl.BlockSpec((1, D_MODEL), lambda i: (0, 0))
    outs = pl.pallas_call(
        body,
        name=name,
        grid=(S // tr,),
        in_specs=[row, row, row1, row, row],
        out_specs=[row, row, row, row, vec, vec],
        out_shape=[jax.ShapeDtypeStruct((S, D_MODEL), BF16)] * 4 + [jax.ShapeDtypeStruct((1, D_MODEL), F32)] * 2,
        compiler_params=pltpu.CompilerParams(dimension_semantics=("arbitrary",)),
    )(dmrg, gates, gates, o_a, o_b)
    return outs


CONV_TR = 512
CONV_TC = 512
N_FFC = D_FF_PAD // CONV_TC


def _conv_taps(x, before, w_ref, b_ref):
    x0 = jnp.concatenate([before, x], axis=0)
    x1 = pltpu.roll(x0, 1, 0)
    x2 = pltpu.roll(x0, 2, 0)
    u = ((b_ref[...] + w_ref[0:1, :] * x2) + w_ref[1:2, :] * x1) + w_ref[2:3, :] * x0
    return u, x0, x1, x2


def _prev_halo(tr):
    return lambda i, j: (jnp.maximum(i * (tr // 8) - 1, 0), j)


def _ffn_fwd(u0, cw, cb, name):
    S = u0.shape[0]
    tr, tc = CONV_TR, CONV_TC

    def body(up_ref, gt_ref, hup_ref, hgt_ref, wu_ref, wg_ref, bu_ref, bg_ref, a_ref):
        live = (pl.program_id(0) > 0).astype(F32)
        up = _conv_taps(up_ref[...], hup_ref[...] * live, wu_ref, bu_ref)[0][8:]
        gt = _conv_taps(gt_ref[...], hgt_ref[...] * live, wg_ref, bg_ref)[0][8:]
        a_ref[...] = ((gt * jax.nn.sigmoid(gt)) * up).astype(BF16)

    return pl.pallas_call(
        body,
        name=name,
        grid=(S // tr, N_FFC),
        in_specs=[
            pl.BlockSpec((tr, tc), lambda i, j: (i, j)),
            pl.BlockSpec((tr, tc), lambda i, j: (i, j + N_FFC)),
            pl.BlockSpec((8, tc), _prev_halo(tr)),
            pl.BlockSpec((8, tc), lambda i, j: (jnp.maximum(i * (tr // 8) - 1, 0), j + N_FFC)),
            pl.BlockSpec((8, tc), lambda i, j: (0, j)),
            pl.BlockSpec((8, tc), lambda i, j: (0, j + N_FFC)),
            pl.BlockSpec((1, tc), lambda i, j: (0, j)),
            pl.BlockSpec((1, tc), lambda i, j: (0, j + N_FFC)),
        ],
        out_specs=pl.BlockSpec((tr, tc), lambda i, j: (i, j)),
        out_shape=jax.ShapeDtypeStruct((S, D_FF_PAD), BF16),
        compiler_params=pltpu.CompilerParams(dimension_semantics=("parallel", "parallel")),
    )(u0, u0, u0, u0, cw, cw, cb, cb)


def _ffn_bwd(u0, da, cw, cb, name):
    S = u0.shape[0]
    tr, tc = CONV_TR, CONV_TC
    nrow, te = S // tr, tr + 8

    def body(up_ref, gt_ref, hup_ref, hgt_ref, nup_ref, ngt_ref, da_ref, nda_ref, wu_ref, wg_ref, bu_ref, bg_ref, du0_ref, dcw_ref, dcb_ref):
        i = pl.program_id(1)
        prev_live = (i > 0).astype(F32)
        next_live = (i < nrow - 1).astype(F32)

        def conv(x_ref, nx_ref, h_ref, w_ref, b_ref):
            x = jnp.concatenate([x_ref[...], nx_ref[...] * next_live], axis=0)
            return [t[8:] for t in _conv_taps(x, h_ref[...] * prev_live, w_ref, b_ref)]

        up, xu0, xu1, xu2 = conv(up_ref, nup_ref, hup_ref, wu_ref, bu_ref)
        gt, xg0, xg1, xg2 = conv(gt_ref, ngt_ref, hgt_ref, wg_ref, bg_ref)
        da_v = jnp.concatenate([da_ref[...], nda_ref[...] * next_live], axis=0)
        sg = jax.nn.sigmoid(gt)
        d_up = da_v * (gt * sg)
        d_gt = (da_v * up) * (sg * (1.0 + gt * (1.0 - sg)))
        tap = lax.broadcasted_iota(jnp.int32, (8, tc), 0)

        def finish(half, du, x0, x1, x2, w_ref):
            n1 = pltpu.roll(du, te - 1, 0)
            n2 = pltpu.roll(du, te - 2, 0)
            du0 = (w_ref[2:3, :] * du + w_ref[1:2, :] * n1) + w_ref[0:1, :] * n2
            du0_ref[half] = du0[:tr].astype(BF16)
            d = du[:tr]
            dcw = jnp.where(
                tap == 0,
                jnp.sum(d * x2[:tr], axis=0, keepdims=True),
                jnp.where(tap == 1, jnp.sum(d * x1[:tr], axis=0, keepdims=True), jnp.where(tap == 2, jnp.sum(d * x0[:tr], axis=0, keepdims=True), 0.0)),
            )
            dcb = jnp.sum(d, axis=0, keepdims=True)

            @pl.when(i == 0)
            def _():
                dcw_ref[half] = dcw
                dcb_ref[half] = dcb

            @pl.when(i != 0)
            def _():
                dcw_ref[half] += dcw
                dcb_ref[half] += dcb

        finish(0, d_up, xu0, xu1, xu2, wu_ref)
        finish(1, d_gt, xg0, xg1, xg2, wg_ref)

    def prev8(off):
        return pl.BlockSpec((8, tc), lambda j, i: (jnp.maximum(i * (tr // 8) - 1, 0), j + off))

    def next8(off):
        return pl.BlockSpec((8, tc), lambda j, i: (jnp.minimum((i + 1) * (tr // 8), S // 8 - 1), j + off))

    return pl.pallas_call(
        body,
        name=name,
        grid=(N_FFC, nrow),
        in_specs=[
            pl.BlockSpec((tr, tc), lambda j, i: (i, j)),
            pl.BlockSpec((tr, tc), lambda j, i: (i, j + N_FFC)),
            prev8(0),
            prev8(N_FFC),
            next8(0),
            next8(N_FFC),
            pl.BlockSpec((tr, tc), lambda j, i: (i, j)),
            next8(0),
            pl.BlockSpec((8, tc), lambda j, i: (0, j)),
            pl.BlockSpec((8, tc), lambda j, i: (0, j + N_FFC)),
            pl.BlockSpec((1, tc), lambda j, i: (0, j)),
            pl.BlockSpec((1, tc), lambda j, i: (0, j + N_FFC)),
        ],
        out_specs=[
            pl.BlockSpec((2, tr, tc), lambda j, i: (0, i, j)),
            pl.BlockSpec((2, 8, tc), lambda j, i: (0, 0, j)),
            pl.BlockSpec((2, 1, tc), lambda j, i: (0, 0, j)),
        ],
        out_shape=[
            jax.ShapeDtypeStruct((2, S, D_FF_PAD), BF16),
            jax.ShapeDtypeStruct((2, 8, D_FF_PAD), F32),
            jax.ShapeDtypeStruct((2, 1, D_FF_PAD), F32),
        ],
        compiler_params=pltpu.CompilerParams(dimension_semantics=("parallel", "arbitrary")),
    )(u0, u0, u0, u0, u0, u0, da, da, cw, cw, cb, cb)


ADAMW_BLOCK_BYTES = 3 << 20


def _adamw(w, g, m, v, name):
    R, C = w.shape
    fits = [t for t in range(8, R + 1, 8) if R % t == 0 and t * C * 4 <= ADAMW_BLOCK_BYTES]
    tr = max(fits) if fits else R

    def body(w_ref, g_ref, m_ref, v_ref, d_ref, nm_ref, nv_ref):
        gv = g_ref[...]
        nm = ADAM_B1 * m_ref[...] + (1.0 - ADAM_B1) * gv
        nv = ADAM_B2 * v_ref[...] + (1.0 - ADAM_B2) * (gv * gv)
        m_hat = nm / (1.0 - ADAM_B1**ADAM_STEP)
        v_hat = nv / (1.0 - ADAM_B2**ADAM_STEP)
        d_ref[...] = -ADAM_LR * (m_hat / (jnp.sqrt(v_hat) + ADAM_EPS) + ADAM_WD * w_ref[...])
        nm_ref[...] = nm
        nv_ref[...] = nv

    blk = pl.BlockSpec((tr, C), lambda i: (i, 0))
    return pl.pallas_call(
        body,
        name=name,
        grid=(R // tr,),
        in_specs=[blk] * 4,
        out_specs=[blk] * 3,
        out_shape=[jax.ShapeDtypeStruct((R, C), F32)] * 3,
        compiler_params=pltpu.CompilerParams(dimension_semantics=("parallel",)),
    )(w, g, m, v)


ANY = pl.BlockSpec(memory_space=pl.ANY)


def _pair_exchange(gs, name):
    n = len(gs)

    def body(*refs):
        g_refs, out_refs = refs[:n], refs[n : 2 * n]
        send_sems, recv_sems = refs[2 * n :]
        x, y, c = lax.axis_index("x"), lax.axis_index("y"), lax.axis_index("c")
        copies = [
            pltpu.make_async_remote_copy(
                src_ref=g_refs[a].at[2 * k + (1 - c)],
                dst_ref=out_refs[a].at[k],
                send_sem=send_sems.at[N_CHIP * a + k],
                recv_sem=recv_sems.at[N_CHIP * a + k],
                device_id=(x, y, 1 - c),
                device_id_type=MESH,
            )
            for a in range(n)
            for k in range(N_CHIP)
        ]
        for cp in copies:
            cp.start()
        for cp in copies:
            cp.wait()

    return pl.pallas_call(
        body,
        name=name,
        out_shape=[jax.ShapeDtypeStruct((N_CHIP,) + g.shape[1:], g.dtype) for g in gs],
        in_specs=[ANY] * n,
        out_specs=[ANY] * n,
        scratch_shapes=[pltpu.SemaphoreType.DMA((N_CHIP * n,)), pltpu.SemaphoreType.DMA((N_CHIP * n,))],
    )(*gs)


def _row_tile(rows):
    return max(t for t in range(16, 353, 16) if rows % t == 0)


def _pair_add(g, recv, core, name):
    _, R, C = g.shape
    tr = _row_tile(R)

    def body(core_ref, g_ref, r_ref, o_ref):
        o_ref[...] = (g_ref[...].astype(F32) + r_ref[...].astype(F32)).astype(o_ref.dtype)

    return pl.pallas_call(
        body,
        name=name,
        grid_spec=pltpu.PrefetchScalarGridSpec(
            num_scalar_prefetch=1,
            grid=(N_CHIP, R // tr),
            in_specs=[
                pl.BlockSpec((1, tr, C), lambda k, i, core_ref: (2 * k + core_ref[0], i, 0)),
                pl.BlockSpec((1, tr, C), lambda k, i, core_ref: (k, i, 0)),
            ],
            out_specs=pl.BlockSpec((1, tr, C), lambda k, i, core_ref: (k, i, 0)),
        ),
        out_shape=jax.ShapeDtypeStruct((N_CHIP, R, C), g.dtype),
        compiler_params=pltpu.CompilerParams(dimension_semantics=("parallel", "parallel")),
    )(core, g, recv)


HBM = pl.BlockSpec(memory_space=pltpu.HBM)
SEM = pl.BlockSpec(memory_space=pltpu.SEMAPHORE)
EFFECT = pltpu.SideEffectType.DATAFLOW_SIDE_EFFECTING
RELATIONS = tuple((dx, dy, dc) for dx in (0, 1) for dy in (0, 1) for dc in (0, 1))[1:]


def _related(rel):
    x, y, c = lax.axis_index("x"), lax.axis_index("y"), lax.axis_index("c")
    return (1 - x if rel[0] else x, 1 - y if rel[1] else y, 1 - c if rel[2] else c)


def _dev_index(pos):
    return 4 * pos[0] + 2 * pos[1] + pos[2]


def _peers(chips):
    if chips:
        return [r for r in RELATIONS if not r[2]], N_CHIP, lambda pos: 2 * pos[0] + pos[1]
    return list(RELATIONS), N_DEV, _dev_index


def _exchange_start(srcs, by_slot, after, name, chips=False):
    n = len(srcs)
    extra = [] if after is None else [after]
    rels, slots, slot_of = _peers(chips)
    lands = [lax.empty((slots,) + (s.shape[1:] if by_slot else s.shape), s.dtype) for s in srcs]
    nsem = len(rels) * n

    def body(*refs):
        src_refs, land_refs = refs[:n], refs[n : 2 * n]
        send_sems, recv_sems = refs[2 * n + len(extra)], refs[2 * n + len(extra) + 1]
        token = refs[-1]
        me = slot_of(_related((0, 0, 0)))
        for a in range(n):
            for k, rel in enumerate(rels):
                peer = _related(rel)
                pltpu.make_async_remote_copy(
                    src_ref=src_refs[a].at[slot_of(peer)] if by_slot else src_refs[a],
                    dst_ref=land_refs[a].at[me],
                    send_sem=send_sems.at[len(rels) * a + k],
                    recv_sem=recv_sems.at[len(rels) * a + k],
                    device_id=peer,
                    device_id_type=MESH,
                ).start()
        token[...] = jnp.zeros_like(token)

    def hbm(a):
        return pltpu.HBM(a.shape, a.dtype)

    outs = pl.pallas_call(
        body,
        name=name,
        out_shape=(
            pltpu.SemaphoreType.DMA((nsem,)),
            pltpu.SemaphoreType.DMA((nsem,)),
            *[hbm(s) for s in srcs],
            *[hbm(l) for l in lands],
            jax.ShapeDtypeStruct((8, 128), F32),
        ),
        in_specs=[HBM] * (2 * n) + [ANY] * len(extra),
        out_specs=(SEM, SEM, *[HBM] * (2 * n), pl.BlockSpec(memory_space=pltpu.VMEM)),
        input_output_aliases={i: 2 + i for i in range(2 * n)},
        compiler_params=pltpu.CompilerParams(has_side_effects=EFFECT),
    )(*[pltpu.with_memory_space_constraint(a, pltpu.HBM) for a in list(srcs) + lands], *extra)
    return (outs[0], outs[1], list(outs[2 : 2 + n]), list(outs[2 + n : 2 + 2 * n])), outs[-1]


def _exchange_wait(handle, by_slot, after, name, chips=False):
    send_sems, recv_sems, srcs, lands = handle
    n = len(srcs)
    rels = _peers(chips)[0]

    def body(*refs):
        src_refs, land_refs = refs[:n], refs[n : 2 * n]
        s_sems, r_sems = refs[2 * n], refs[2 * n + 1]
        for a in range(n):
            for k, rel in enumerate(rels):
                copy = pltpu.make_async_remote_copy(
                    src_ref=src_refs[a].at[0] if by_slot else src_refs[a],
                    dst_ref=land_refs[a].at[0],
                    send_sem=s_sems.at[len(rels) * a + k],
                    recv_sem=r_sems.at[len(rels) * a + k],
                    device_id=_related(rel),
                    device_id_type=MESH,
                )
                copy.wait_send()
                copy.wait_recv()

    outs = pl.pallas_call(
        body,
        name=name,
        out_shape=tuple(pltpu.HBM(a.shape, a.dtype) for a in srcs + lands),
        in_specs=[HBM] * (2 * n) + [SEM, SEM, ANY],
        out_specs=tuple([HBM] * (2 * n)),
        input_output_aliases={i: i for i in range(2 * n)},
        compiler_params=pltpu.CompilerParams(has_side_effects=EFFECT),
    )(*srcs, *lands, send_sems, recv_sems, after)
    return list(outs[:n]), list(outs[n:])


NEAR = ((0, 0, 1), (1, 0, 0), (0, 1, 0), (1, 1, 0))


def _gather2_start(blocks, name):
    n = len(blocks)
    lands = [lax.empty((N_DEV,) + b.shape, b.dtype) for b in blocks]

    def body(*refs):
        src_refs, land_refs = refs[:n], refs[n : 2 * n]
        send_sems, recv_sems, token = refs[2 * n], refs[2 * n + 1], refs[-1]
        me = _dev_index(_related((0, 0, 0)))
        for a in range(n):
            for k, rel in enumerate(NEAR):
                pltpu.make_async_remote_copy(
                    src_ref=src_refs[a],
                    dst_ref=land_refs[a].at[me],
                    send_sem=send_sems.at[len(NEAR) * a + k],
                    recv_sem=recv_sems.at[len(NEAR) * a + k],
                    device_id=_related(rel),
                    device_id_type=MESH,
                ).start()
        token[...] = jnp.zeros_like(token)

    nsem = len(NEAR) * n
    outs = pl.pallas_call(
        body,
        name=name,
        out_shape=(
            pltpu.SemaphoreType.DMA((nsem,)),
            pltpu.SemaphoreType.DMA((nsem,)),
            *[pltpu.HBM(a.shape, a.dtype) for a in list(blocks) + lands],
            jax.ShapeDtypeStruct((8, 128), F32),
        ),
        in_specs=[HBM] * (2 * n),
        out_specs=(SEM, SEM, *[HBM] * (2 * n), pl.BlockSpec(memory_space=pltpu.VMEM)),
        input_output_aliases={i: 2 + i for i in range(2 * n)},
        compiler_params=pltpu.CompilerParams(has_side_effects=EFFECT),
    )(*[pltpu.with_memory_space_constraint(a, pltpu.HBM) for a in list(blocks) + lands])
    return (outs[0], outs[1], list(outs[2 : 2 + n]), list(outs[2 + n : 2 + 2 * n])), outs[-1]


def _gather2_forward(handle, after, name):
    send1, recv1, srcs, lands = handle
    n = len(srcs)

    def body(*refs):
        src_refs, land_refs = refs[:n], refs[n : 2 * n]
        s1, r1 = refs[2 * n], refs[2 * n + 1]
        s2, r2 = refs[-2], refs[-1]
        sibling = _related(NEAR[0])
        for a in range(n):
            for k, rel in enumerate(NEAR):
                first = pltpu.make_async_remote_copy(
                    src_ref=src_refs[a],
                    dst_ref=land_refs[a].at[0],
                    send_sem=s1.at[len(NEAR) * a + k],
                    recv_sem=r1.at[len(NEAR) * a + k],
                    device_id=_related(rel),
                    device_id_type=MESH,
                )
                first.wait_send()
                first.wait_recv()
                if k:
                    slot = land_refs[a].at[_dev_index(_related(rel))]
                    pltpu.make_async_remote_copy(
                        src_ref=slot,
                        dst_ref=slot,
                        send_sem=s2.at[3 * a + k - 1],
                        recv_sem=r2.at[3 * a + k - 1],
                        device_id=sibling,
                        device_id_type=MESH,
                    ).start()

    outs = pl.pallas_call(
        body,
        name=name,
        out_shape=(
            *[pltpu.HBM(a.shape, a.dtype) for a in srcs + lands],
            pltpu.SemaphoreType.DMA((3 * n,)),
            pltpu.SemaphoreType.DMA((3 * n,)),
        ),
        in_specs=[HBM] * (2 * n) + [SEM, SEM, ANY],
        out_specs=(*[HBM] * (2 * n), SEM, SEM),
        input_output_aliases={i: i for i in range(2 * n)},
        compiler_params=pltpu.CompilerParams(has_side_effects=EFFECT),
    )(*srcs, *lands, send1, recv1, after)
    return outs[-2], outs[-1], list(outs[:n]), list(outs[n : 2 * n])


def _gather2_wait(handle, name):
    send2, recv2, srcs, lands = handle
    n = len(srcs)

    def body(*refs):
        land_refs = refs[n : 2 * n]
        s2, r2 = refs[2 * n], refs[2 * n + 1]
        for a in range(n):
            for j in range(3):
                passed = pltpu.make_async_remote_copy(
                    src_ref=land_refs[a].at[0],
                    dst_ref=land_refs[a].at[0],
                    send_sem=s2.at[3 * a + j],
                    recv_sem=r2.at[3 * a + j],
                    device_id=_related(NEAR[0]),
                    device_id_type=MESH,
                )
                passed.wait_send()
                passed.wait_recv()

    outs = pl.pallas_call(
        body,
        name=name,
        out_shape=tuple(pltpu.HBM(a.shape, a.dtype) for a in srcs + lands),
        in_specs=[HBM] * (2 * n) + [SEM, SEM],
        out_specs=tuple([HBM] * (2 * n)),
        input_output_aliases={i: i for i in range(2 * n)},
        compiler_params=pltpu.CompilerParams(has_side_effects=EFFECT),
    )(*srcs, *lands, send2, recv2)
    return list(outs[:n]), list(outs[n:])


def _slot_sum(parts, name):
    n, R, C = parts.shape
    tr = _row_tile(R) if R % 16 == 0 else R

    def body(p_ref, o_ref):
        acc = p_ref[0].astype(F32)
        for k in range(1, n):
            acc = acc + p_ref[k].astype(F32)
        o_ref[...] = acc

    return pl.pallas_call(
        body,
        name=name,
        grid=(R // tr,),
        in_specs=[pl.BlockSpec((n, tr, C), lambda i: (0, i, 0))],
        out_specs=pl.BlockSpec((tr, C), lambda i: (i, 0)),
        out_shape=jax.ShapeDtypeStruct((R, C), F32),
        compiler_params=pltpu.CompilerParams(dimension_semantics=("parallel",)),
    )(parts)


W_IN_TC = 256
W_IN_BOUNDS = (0, LAT, LAT + 3 * DIL_QKV, LAT + 3 * DIL_QKV + D_MODEL, D_IN)


def _dqkv_chunks():
    return [((g * 3 + part) * DIL_OUT, LAT + part * DIL_QKV + g * DIL_OUT) for g in range(DIL_GROUPS) for part in range(3)]


def _w_in_regroup(slots, after, name):
    tc = W_IN_TC

    def body(s_ref, _, lat_ref, dqkv_ref, g_ref, buf):
        for j in range(N_DEV):
            buf[j * IN_ROWS : (j + 1) * IN_ROWS, :] = s_ref[j].astype(F32)[:IN_ROWS, :]
        lat_ref[:LAT, :] = buf[:LAT, :].astype(BF16)
        lat_ref[LAT:, :] = jnp.zeros((LAT_PAD - LAT, tc), BF16)
        for dst, src in _dqkv_chunks():
            dqkv_ref[dst : dst + DIL_OUT, :] = buf[src : src + DIL_OUT, :].astype(BF16)
        g_ref[...] = buf[W_IN_BOUNDS[2] :, :].astype(BF16)

    def col(rows):
        return pl.BlockSpec((rows, tc), lambda k: (0, k))

    return pl.pallas_call(
        body,
        name=name,
        grid=(D_MODEL // tc,),
        in_specs=[pl.BlockSpec((N_DEV, IN_ROWS_PAD, tc), lambda k: (0, 0, k)), pl.BlockSpec((8, 128), lambda k: (0, 0))],
        out_specs=[col(LAT_PAD), col(3 * DIL_QKV), col(2 * D_MODEL)],
        out_shape=[
            jax.ShapeDtypeStruct((LAT_PAD, D_MODEL), BF16),
            jax.ShapeDtypeStruct((3 * DIL_QKV, D_MODEL), BF16),
            jax.ShapeDtypeStruct((2 * D_MODEL, D_MODEL), BF16),
        ],
        scratch_shapes=[pltpu.VMEM((D_IN, tc), F32)],
        compiler_params=pltpu.CompilerParams(dimension_semantics=("parallel",)),
    )(slots, after)


def _w_in_grad_regroup(g_lat, g_dqkv, g_ga, g_gb, name):
    tc = W_IN_TC

    def body(lat_ref, dqkv_ref, ga_ref, gb_ref, o_ref, buf):
        b = W_IN_BOUNDS
        buf[b[0] : b[1], :] = lat_ref[:LAT, :].astype(F32)
        for dst, src in _dqkv_chunks():
            buf[src : src + DIL_OUT, :] = dqkv_ref[dst : dst + DIL_OUT, :].astype(F32)
        buf[b[2] : b[3], :] = ga_ref[...].astype(F32)
        buf[b[3] : b[4], :] = gb_ref[...].astype(F32)
        fill = jnp.zeros((IN_ROWS_PAD - IN_ROWS, tc), F32)
        for j in range(N_DEV):
            o_ref[j] = jnp.concatenate([buf[j * IN_ROWS : (j + 1) * IN_ROWS, :], fill], axis=0).astype(BF16)

    def col(rows):
        return pl.BlockSpec((rows, tc), lambda k: (0, k))

    return pl.pallas_call(
        body,
        name=name,
        grid=(D_MODEL // tc,),
        in_specs=[col(LAT_PAD), col(3 * DIL_QKV), col(D_MODEL), col(D_MODEL)],
        out_specs=pl.BlockSpec((N_DEV, IN_ROWS_PAD, tc), lambda k: (0, 0, k)),
        out_shape=jax.ShapeDtypeStruct((N_DEV, IN_ROWS_PAD, D_MODEL), BF16),
        scratch_shapes=[pltpu.VMEM((D_IN, tc), F32)],
        compiler_params=pltpu.CompilerParams(dimension_semantics=("parallel",)),
    )(g_lat, g_dqkv, g_ga, g_gb)


def _ffn_pad(a, axis):
    a = jnp.moveaxis(a, axis, -1)
    g = a.reshape(a.shape[:-1] + (2 * N_DEV, FF_GROUP))
    g = jnp.pad(g, [(0, 0)] * (g.ndim - 1) + [(0, FF_GROUP_PAD - FF_GROUP)])
    return jnp.moveaxis(g.reshape(a.shape[:-1] + (2 * D_FF_PAD,)), -1, axis)


def _ffn_unpad(a, axis):
    a = jnp.moveaxis(a, axis, -1)
    g = a.reshape(a.shape[:-1] + (2 * N_DEV, FF_GROUP_PAD))[..., :FF_GROUP]
    return jnp.moveaxis(g.reshape(a.shape[:-1] + (2 * D_FF,)), -1, axis)


MISC = (("w_o_mla", (256, 1024)), ("w_o_dil", (256, 512)), ("w_uq", (192, 512)), ("w_ukv", (256, 256)))
BIG_WEIGHTS = ("w_in", "w_up", "w_down", "w_out") + tuple(n for n, _ in MISC)


def _exchange_blocks(w):
    def t(a):
        return a.astype(BF16).T

    up = t(w["w_up"]).reshape(2, FF_GROUP, D_MODEL)
    return [
        jnp.pad(t(w["w_in"]), ((0, IN_ROWS_PAD - IN_ROWS), (0, 0))),
        jnp.pad(up, ((0, 0), (0, FF_GROUP_PAD - FF_GROUP), (0, 0))).reshape(2 * FF_GROUP_PAD, D_MODEL),
        jnp.pad(w["w_down"].astype(BF16), ((0, FF_GROUP_PAD - FF_GROUP), (0, 0))),
        w["w_out"].astype(BF16),
        jnp.concatenate([t(w[n]).reshape(-1, D_MODEL) for n, _ in MISC], axis=0),
    ]


def _misc_split(misc):
    out, off = {}, 0
    for n, (r, c) in MISC:
        rows = r * c // D_MODEL
        out[n] = misc[..., off : off + rows, :].reshape(misc.shape[:-2] + (r, c))
        off += rows
    return out


def _small_matrices(g_misc):
    misc = _misc_split(g_misc)
    uq_t = jnp.pad(misc["w_uq"], ((0, 0), (0, HEAD_PAD - QK_NOPE - QK_ROPE), (0, 0)))
    return {
        "uq_t": uq_t.reshape(MLA_HEADS * HEAD_PAD, Q_LORA),
        "ukv_t": misc["w_ukv"].reshape(MLA_HEADS * HEAD_PAD, KV_LORA),
        "o_mla_t": misc["w_o_mla"].reshape(D_MODEL, MLA_HEADS * V_HEAD),
        "o_dil_t": misc["w_o_dil"].reshape(D_MODEL, DIL_OUT),
    }


def _small_grad_blocks(g):
    uq_t = g["uq_t"].reshape(MLA_HEADS, HEAD_PAD, Q_LORA)[:, : QK_NOPE + QK_ROPE]
    misc = {"w_o_mla": g["o_mla_t"], "w_o_dil": g["o_dil_t"], "w_uq": uq_t, "w_ukv": g["ukv_t"]}
    return [
        g["w_out"].reshape(N_DEV, -1, D_MODEL),
        jnp.concatenate([misc[n].reshape(N_DEV, -1, D_MODEL) for n, _ in MISC], axis=1),
    ]


def _grad_shards(sums):
    s_in, s_out, s_misc, s_up, s_down = sums
    out = {
        "w_in": s_in[:IN_ROWS].T,
        "w_up": s_up.reshape(2, FF_GROUP_PAD, D_MODEL)[:, :FF_GROUP].reshape(2 * FF_GROUP, D_MODEL).T,
        "w_down": s_down[:FF_GROUP],
        "w_out": s_out,
    }
    out.update({n: v.T for n, v in _misc_split(s_misc).items()})
    return out


def _local_step(x, h, tgt, wt, conv_w, small, small_matrices, ffn_weight, send_ffn_grads, send_small_grads, send_w_in_grads):
    S = x.shape[0]
    lat_t, dqkv_t, g_t = wt
    cw = jnp.pad(_ffn_pad(conv_w, 1), ((0, 5), (0, 0)))
    cb = _ffn_pad(small["conv_b"], 1)
    cos_t, sin_t = _rope_tables(S)
    bias = _dil_bias()
    g1, g2, g3 = small["attn_norm_g"], small["ffn_norm_g"], small["final_norm_g"]
    gq, gkv = small["q_norm_g"], small["kv_norm_g"]

    lat = _mm(h, lat_t, "nt", F32, 1024, LAT_PAD, D_MODEL, "proj_lat")
    dqkv = _mm(h, dqkv_t, "nt", F32, 1024, 1536, D_MODEL, "proj_dqkv")
    gates = _mm(h, g_t, "nt", F32, 1024, 1024, D_MODEL, "proj_gates", bias=small["b_gate"], act="sigmoid")
    sm = small_matrices(gates)
    uq_t, ukv_t, o_mla_t, o_dil_t = sm["uq_t"], sm["ukv_t"], sm["o_mla_t"], sm["o_dil_t"]
    cqn, ckvn, kpe = _mla_prep1(lat, gq, gkv, cos_t, sin_t, "mla_prep1")
    q_raw = _mm(cqn, uq_t, "nt", F32, 1024, 1024, Q_LORA, "mla_uq")
    kv = _mm(ckvn, ukv_t, "nt", BF16, 1024, 1024, KV_LORA, "mla_ukv")
    q_att, k_att = _mla_prep2(q_raw, kv, kpe, cos_t, sin_t, "mla_prep2")
    o, lse = _flash2_fwd(q_att, k_att, kv, "mla_flash_fwd")
    o_a = _mm(o, o_mla_t, "nt", F32, 1024, 1024, MLA_HEADS * V_HEAD, "mla_out")

    d_os, d_ls = [], []
    for g, (_, dil) in enumerate(DIL_PATTERNS):
        og, lg = _dil_fwd_group(dqkv, bias[g], g, dil, f"dil_fwd_{g}")
        d_os.append(og)
        d_ls.append(lg)
    od, dil_lse = _dil_combine(d_os, d_ls, "dil_combine")
    o_b = _mm(od, o_dil_t, "nt", F32, 1024, 1024, DIL_OUT, "dil_out")

    mrg = _merge_fwd(gates, o_a, o_b, "merge_fwd")
    w_out = ffn_weight("w_out", mrg)
    x1, h2 = _mm_res_rms(mrg, w_out, x, g2, "mix_out")
    up_t = ffn_weight("up_t", h2)
    u0 = _mm(h2, up_t, "nt", F32, 1024, 1024, D_MODEL, "ffn_up")
    a = _ffn_fwd(u0, cw, cb, "ffn_conv_fwd")
    w_down = ffn_weight("w_down", a)
    x2 = _mm(a, w_down, "nn", F32, 1024, 1024, D_FF_PAD // 2, "ffn_down", res=x1)
    loss_part, dx2, dx2b, dg3 = _final_loss(x2, g3, tgt, "final_loss")

    da = _mm(dx2b, w_down, "nt", F32, 1024, D_FF_PAD // 4, D_MODEL, "ffn_down_dx")
    gw_down = _mm(a, dx2b, "tn", BF16, 512, D_MODEL, S, "ffn_down_dw")
    du0, dcw, dcb = _ffn_bwd(u0, da, cw, cb, "ffn_conv_bwd")
    du0 = du0.reshape(2 * S, D_FF_PAD)
    gw_up_t = _mm(du0, h2, "tn", BF16, 512, D_MODEL, S, "ffn_up_dw", a_halves=2)
    sent = send_ffn_grads(gw_up_t, gw_down)
    dh2 = _mm(du0, up_t, "nn", F32, 1024, 1024, D_FF_PAD // 2, "ffn_up_dx", a_halves=2)
    dx1, dx1b, dg2 = _rms_bwd(dh2, x1, g2 + sent, dx2, "rms_ffn_bwd")

    dmrg = _mm(dx1b, w_out, "nt", F32, 1024, 1024, D_MODEL, "mix_out_dx")
    gw_out = _mm(mrg, dx1b, "tn", BF16, 512, D_MODEL, S, "mix_out_dw")
    do_a, do_b, dga, dgb, dba, dbb = _merge_bwd(dmrg, gates, o_a, o_b, "merge_bwd")

    do = _mm(do_a, o_mla_t, "nn", BF16, 1024, 1024, D_MODEL, "mla_out_dx")
    gw_o_mla_t = _mm(do_a, o, "tn", BF16, 1024, 1024, 1024, "mla_out_dw")
    dod = _mm(do_b, o_dil_t, "nn", F32, 1024, DIL_OUT, D_MODEL, "dil_out_dx")
    gw_o_dil_t = _mm(do_b, od, "tn", BF16, 1024, DIL_OUT, 1024, "dil_out_dw")

    delta = _flash_delta(do, o, "mla_flash_delta")
    lse_row = lse[:, :, 0][:, None, :]
    delta_row = delta[:, :MLA_HEADS].T[:, None, :]
    dq_att, dk_att, dv = _flash2_bwd(q_att, k_att, kv, do, lse_row, delta_row, "mla_flash_bwd")
    dq_raw, dkv, dkpe = _mla_post(dq_att, dk_att, dv, cos_t, sin_t, "mla_post")
    dcqn = _mm(dq_raw, uq_t, "nn", F32, 1024, Q_LORA, MLA_HEADS * HEAD_PAD, "mla_uq_dx")
    gw_uq_t = _mm(dq_raw, cqn, "tn", BF16, 1024, Q_LORA, 1024, "mla_uq_dw")
    dckvn = _mm(dkv, ukv_t, "nn", F32, 1024, KV_LORA, MLA_HEADS * HEAD_PAD, "mla_ukv_dx")
    gw_ukv_t = _mm(dkv, ckvn, "tn", BF16, 1024, KV_LORA, 1024, "mla_ukv_dw")
    sent = send_small_grads({"uq_t": gw_uq_t, "ukv_t": gw_ukv_t, "o_mla_t": gw_o_mla_t, "o_dil_t": gw_o_dil_t, "w_out": gw_out})
    dlat, dgq, dgkv = _lat_bwd(dcqn, dckvn, dkpe, lat, gq + sent, gkv, "lat_bwd")

    dd = _dil_rowdot(dod, od, "dil_rowdot")
    ddqkv = lax.empty((3 * DIL_GROUPS, S, DIL_OUT), F32)
    for g, (_, dil) in enumerate(DIL_PATTERNS):
        ddqkv = _dil_bwd_group(dqkv, bias[g], dod, dd, dil_lse, ddqkv, g, dil, f"dil_bwd_{g}")
    gw_lat_t = _mm(dlat, h, "tn", BF16, LAT_PAD, 1024, S, "proj_lat_dw")
    gw_dqkv_t = _mm(ddqkv.reshape(3 * DIL_GROUPS * S, DIL_OUT), h, "tn", BF16, 512, 1024, S, "proj_dqkv_dw", a_halves=3 * DIL_GROUPS)
    gw_ga_t = _mm(dga, h, "tn", BF16, 512, D_MODEL, S, "proj_ga_dw")
    gw_gb_t = _mm(dgb, h, "tn", BF16, 512, D_MODEL, S, "proj_gb_dw")
    sent = send_w_in_grads(gw_lat_t, gw_dqkv_t, gw_ga_t, gw_gb_t)
    dh = _mm(dlat + sent.astype(BF16), lat_t, "nn", F32, 1024, 1024, LAT_PAD, "proj_lat_dx")
    dh = _stacked_mm(ddqkv, dqkv_t, dh, "proj_dqkv_dx")
    dh = _mm(dga, g_t, "nn", F32, 1024, 1024, D_MODEL, "proj_ga_dx", res=dh)
    grad_x, dg1 = _mm_rms_bwd(dgb, g_t, 1, dh, x, g1, dx1, "proj_gb_dx_rms_attn_bwd")

    small_grads = {
        "attn_norm_g": dg1,
        "b_gate": jnp.concatenate([dba, dbb], axis=1),
        "q_norm_g": dgq,
        "kv_norm_g": dgkv,
        "ffn_norm_g": dg2,
        "conv_b": _ffn_unpad(jnp.concatenate([dcb[0], dcb[1]], axis=1), 1),
        "final_norm_g": dg3,
        "conv_w": _ffn_unpad(jnp.concatenate([dcw[0, :3], dcw[1, :3]], axis=1), 1),
    }
    return loss_part, grad_x, small_grads


SMALL_ORDER = ("attn_norm_g", "b_gate", "q_norm_g", "kv_norm_g", "ffn_norm_g", "conv_b", "final_norm_g", "conv_w")
WEIGHT_ORDER = (
    "attn_norm_g", "w_in", "b_gate", "q_norm_g", "w_uq", "kv_norm_g", "w_ukv", "w_o_mla", "w_o_dil", "w_out",
    "ffn_norm_g", "w_up", "conv_w", "conv_b", "w_down", "final_norm_g",
)


def kernel(x, attn_norm_g, w_in, b_gate, q_norm_g, w_uq, kv_norm_g, w_ukv, w_o_mla, w_o_dil, w_out, ffn_norm_g, w_up, conv_w, conv_b, w_down, final_norm_g, loss_target, m_attn_norm_g, m_w_in, m_b_gate, m_q_norm_g, m_w_uq, m_kv_norm_g, m_w_ukv, m_w_o_mla, m_w_o_dil, m_w_out, m_ffn_norm_g, m_w_up, m_conv_w, m_conv_b, m_w_down, m_final_norm_g, v_attn_norm_g, v_w_in, v_b_gate, v_q_norm_g, v_w_uq, v_kv_norm_g, v_w_ukv, v_w_o_mla, v_w_o_dil, v_w_out, v_ffn_norm_g, v_w_up, v_conv_w, v_conv_b, v_w_down, v_final_norm_g):
    env = dict(locals())
    dev = 4 * lax.axis_index("x") + 2 * lax.axis_index("y") + lax.axis_index("c")
    core = lax.axis_index("c").astype(jnp.int32).reshape(1)

    def two_d(a):
        return a.reshape(-1, a.shape[-1])

    w = {n: two_d(env[n]) for n in WEIGHT_ORDER}
    m = {n: two_d(env["m_" + n]) for n in WEIGHT_ORDER}
    v = {n: two_d(env["v_" + n]) for n in WEIGHT_ORDER}

    chip = 2 * lax.axis_index("x") + lax.axis_index("y")

    def own_slot_in(lands, own, slot=dev):
        return [lax.dynamic_update_slice(l, o[None], (slot, 0, 0)) for l, o in zip(lands, own)]

    b_in = _exchange_blocks(w)[0]
    r, c = CONV_SHARD
    conv = jnp.pad(w["conv_w"].reshape(-1), (0, 8 * SMALL_COLS - r * c)).reshape(8, SMALL_COLS)
    first_level, token = _gather2_start([b_in, conv], "ag_w_in_start")
    tied = {n: w[n] + token[0, 0] for n in BIG_WEIGHTS}
    _, b_up, b_down, b_out, b_misc = _exchange_blocks(tied)
    h = _rms_fwd(x[0], w["attn_norm_g"] + token[0, 0], "rms_attn")
    prepared = b_up[:1, :1] + b_down[:1, :1] + b_out[:1, :1] + b_misc[:1, :1] + h[:1, :1]
    own, lands = _gather2_wait(_gather2_forward(first_level, prepared, "ag_w_in_forward"), "ag_w_in_wait")
    g_in, conv = own_slot_in(lands, own)
    misc_gather, started = _exchange_start([b_misc], False, conv, "ag_small_start")
    ffn_gathers, started2 = {}, started
    for key, block in (("w_out", b_out), ("up_t", b_up), ("w_down", b_down)):
        ffn_gathers[key], started2 = _exchange_start([block], False, started2, f"ag_{key}_start")
    wt = _w_in_regroup(g_in, started2, "w_in_regroup")
    conv = conv.reshape(N_DEV, 8 * SMALL_COLS)[:, : r * c].reshape(N_DEV, r, c)
    conv_w_full = conv.transpose(1, 0, 2).reshape(r, N_DEV * c)
    small = {n: w[n] for n in SMALL_ORDER if n != "conv_w"}

    def small_matrices(after):
        own, lands = _exchange_wait(misc_gather, False, after, "ag_small_wait")
        return _small_matrices(own_slot_in(lands, own)[0])

    def ffn_weight(key, after):
        own, lands = _exchange_wait(ffn_gathers[key], False, after, f"ag_{key}_wait")
        return own_slot_in(lands, own)[0].reshape(-1, D_MODEL)

    reduces = {}

    def send_ffn_grads(gw_up_t, gw_down):
        blocks = [gw_up_t.reshape(N_DEV, 2 * FF_GROUP_PAD, D_MODEL), gw_down.reshape(N_DEV, FF_GROUP_PAD, D_MODEL)]
        reduces["ffn"], token = _exchange_start(blocks, True, None, "rs_ffn_start")
        return token[0, 0]

    def send_small_grads(g):
        reduces["small"], token = _exchange_start(_small_grad_blocks(g), True, None, "rs_small_start")
        return token[0, 0]

    def send_w_in_grads(g_lat, g_dqkv, g_ga, g_gb):
        e_in = _w_in_grad_regroup(g_lat, g_dqkv, g_ga, g_gb, "w_in_grad_regroup")
        pair = _pair_add(e_in, _pair_exchange([e_in], "rs_w_in_pair_exchange")[0], core, "rs_w_in_pair_add")
        reduces["w_in"], token = _exchange_start([pair], True, None, "rs_w_in_start", chips=True)
        return token[0, 0]

    loss_part, grad_x, small_grads = _local_step(
        x[0], h, loss_target[0], wt, conv_w_full, small, small_matrices, ffn_weight,
        send_ffn_grads, send_small_grads, send_w_in_grads,
    )
    loss = lax.psum(loss_part[0, 0], AXES)
    sflat = jnp.concatenate([small_grads[n].reshape(-1) for n in SMALL_ORDER])
    sflat = jnp.pad(sflat, (0, SMALL_ROWS * SMALL_COLS - sflat.shape[0])).reshape(SMALL_ROWS, SMALL_COLS)
    vec_gather, _ = _exchange_start([sflat], False, None, "rs_vec_start")

    def finish(key, by_chip, name):
        sent, lands = _exchange_wait(reduces[key], True, grad_x, name + "_wait", chips=by_chip)
        slot = chip if by_chip else dev
        own = [lax.dynamic_index_in_dim(s, slot, 0, keepdims=False) for s in sent]
        return [_slot_sum(p, f"{name}_sum_{i}") for i, p in enumerate(own_slot_in(lands, own, slot))]

    (s_in,) = finish("w_in", True, "rs_w_in")
    s_out, s_misc = finish("small", False, "rs_small")
    s_up, s_down = finish("ffn", False, "rs_ffn")
    gshard = _grad_shards([s_in, s_out, s_misc, s_up, s_down])

    updates = {n: _adamw(w[n], gshard[n], m[n], v[n], "adamw_" + n) for n in BIG_WEIGHTS}

    big_done = sum(updates[n][0][:1, :1] for n in BIG_WEIGHTS)
    own, lands = _exchange_wait(vec_gather, False, big_done, "rs_vec_wait")
    ssum = _slot_sum(own_slot_in(lands, own)[0], "small_sum").reshape(-1)
    gsmall, off = {}, 0
    for n in SMALL_ORDER:
        shape = (3, 2 * D_FF) if n == "conv_w" else w[n].shape
        size = shape[0] * shape[1]
        gsmall[n] = ssum[off : off + size].reshape(shape)
        off += size
    gsmall["conv_w"] = lax.dynamic_slice_in_dim(gsmall["conv_w"], dev * CONV_SHARD[1], CONV_SHARD[1], axis=1)
    updates.update({n: _adamw(w[n], gsmall[n], m[n], v[n], "adamw_" + n) for n in SMALL_ORDER})

    g_all = {**gshard, **gsmall}
    out_g, out_d, out_m, out_v = [], [], [], []
    for n in WEIGHT_ORDER:
        d, nm, nv = updates[n]
        shape = env[n].shape
        out_g.append(g_all[n].reshape(shape))
        out_d.append(d.reshape(shape))
        out_m.append(nm.reshape(shape))
        out_v.append(nv.reshape(shape))
    return (loss, grad_x[None], *out_g, *out_d, *out_m, *out_v)
```

```python
import functools

import jax
import jax.numpy as jnp
import numpy as np
from jax import lax
from jax.experimental import pallas as pl
from jax.experimental.pallas import tpu as pltpu

F32 = jnp.float32
BF16 = jnp.bfloat16

N_DEV = 8
N_CHIP = 4
AXES = ("x", "y", "c")
MESH = pl.DeviceIdType.MESH

D_MODEL = 2048
MLA_HEADS = 8
QK_NOPE = 128
QK_ROPE = 64
V_HEAD = 128
Q_LORA = 512
KV_LORA = 256
ROPE_THETA = 10000.0
HEAD_PAD = 256
DIL_PATTERNS = ((128, 1), (512, 4), (2048, 16))
DIL_GROUPS = 3
DIL_HG = 4
DIL_HEADS = 12
DIL_HD = 128
DIL_BLK = 128
DIL_QKV = DIL_HEADS * DIL_HD
DIL_OUT = DIL_HG * DIL_HD
ALIBI_MAX_BIAS = 8.0
D_FF = 5504
D_FF_PAD = 5632
NORM_EPS = 1e-6
LAT = Q_LORA + KV_LORA + QK_ROPE
LAT_PAD = 896
D_IN = LAT + 3 * DIL_QKV + 2 * D_MODEL
NEG = -1e30

ADAM_LR = 0.001
ADAM_B1 = 0.9
ADAM_B2 = 0.999
ADAM_EPS = 1e-08
ADAM_WD = 0.01
ADAM_STEP = 10

SMALL_ROWS = 56
SMALL_COLS = 1024

IN_ROWS = 1192
IN_ROWS_PAD = 1200
FF_GROUP = D_FF // N_DEV
FF_GROUP_PAD = D_FF_PAD // N_DEV
CONV_SHARD = (3, 1376)

NT = (((1,), (1,)), ((), ()))
TN = (((0,), (0,)), ((), ()))


def _dot(a, b, dims=(((1,), (0,)), ((), ()))):
    return lax.dot_general(a, b, dims, preferred_element_type=F32)


def _mm(a, b, mode, out_dtype, tm, tn, tk, name, bias=None, act=None, res=None, b_koff=0, a_halves=1):
    H = a_halves
    if mode == "nn":
        (M, K), (K2, N) = (a.shape[0] // H, a.shape[1] * H), b.shape
        assert (b_koff + 1) * K <= K2, (name, a.shape, b.shape)
        koff, K2 = b_koff * (K // tk), K
        kper, mrows = a.shape[1] // tk, M // tm
        a_spec = pl.BlockSpec((tm, tk), lambda i, j, k: (i + (k // kper) * mrows, k % kper))
        b_spec = pl.BlockSpec((tk, tn), lambda i, j, k: (k + koff, j))
        dims = (((1,), (0,)), ((), ()))
    elif mode == "nt":
        (M, K), (N, K2) = a.shape, b.shape
        a_spec = pl.BlockSpec((tm, tk), lambda i, j, k: (i, k))
        b_spec = pl.BlockSpec((tn, tk), lambda i, j, k: (j, k))
        dims = NT
    else:
        (K, M), (K2, N) = (a.shape[0] // H, a.shape[1] * H), b.shape
        mper, krows = a.shape[1] // tm, K // tk
        a_spec = pl.BlockSpec((tk, tm), lambda i, j, k: (k + (i // mper) * krows, i % mper))
        b_spec = pl.BlockSpec((tk, tn), lambda i, j, k: (k, j))
        dims = TN
    assert K == K2 and M % tm == 0 and N % tn == 0 and K % tk == 0, (name, a.shape, b.shape)
    nk = K // tk
    has_bias, has_res = bias is not None, res is not None

    def body(*refs):
        refs = list(refs)
        a_ref, b_ref = refs[0], refs[1]
        pos = 2
        bias_ref = res_ref = None
        if has_bias:
            bias_ref = refs[pos]
            pos += 1
        if has_res:
            res_ref = refs[pos]
            pos += 1
        o_ref = refs[pos]
        p = _dot(a_ref[...].astype(BF16), b_ref[...].astype(BF16), dims)

        def finish(acc):
            if has_bias:
                acc = acc + bias_ref[...]
            if act == "sigmoid":
                acc = jax.nn.sigmoid(acc)
            if has_res:
                acc = res_ref[...] + acc
            o_ref[...] = acc.astype(o_ref.dtype)

        if nk == 1:
            finish(p)
        else:
            acc_ref = refs[pos + 1]
            k = pl.program_id(2)

            @pl.when(k == 0)
            def _():
                acc_ref[...] = p

            @pl.when(k != 0)
            def _():
                acc_ref[...] += p

            @pl.when(k == nk - 1)
            def _():
                finish(acc_ref[...])

    in_specs = [a_spec, b_spec]
    args = [a, b]
    if has_bias:
        in_specs.append(pl.BlockSpec((1, tn), lambda i, j, k: (0, j)))
        args.append(bias)
    if has_res:
        in_specs.append(pl.BlockSpec((tm, tn), lambda i, j, k: (i, j)))
        args.append(res)
    return pl.pallas_call(
        body,
        name=name,
        grid=(M // tm, N // tn, nk),
        in_specs=in_specs,
        out_specs=pl.BlockSpec((tm, tn), lambda i, j, k: (i, j)),
        out_shape=jax.ShapeDtypeStruct((M, N), out_dtype),
        scratch_shapes=[pltpu.VMEM((tm, tn), F32)] if nk > 1 else [],
        compiler_params=pltpu.CompilerParams(dimension_semantics=("parallel", "parallel", "arbitrary")),
    )(*args)


def _stacked_mm(pieces, w_t, res, after, name, tm=512, tn=1024):
    P, M, W = pieces.shape
    N = w_t.shape[1]

    def body(a_ref, b_ref, r_ref, _, o_ref):
        acc = r_ref[...]
        for p in range(P):
            acc = acc + _dot(a_ref[p].astype(BF16), b_ref[p * W : (p + 1) * W, :])
        o_ref[...] = acc

    tile = pl.BlockSpec((tm, tn), lambda i, j: (i, j))
    return pl.pallas_call(
        body,
        name=name,
        grid=(M // tm, N // tn),
        in_specs=[
            pl.BlockSpec((P, tm, W), lambda i, j: (0, i, 0)),
            pl.BlockSpec((P * W, tn), lambda i, j: (0, j)),
            tile,
            pl.BlockSpec((8, 128), lambda i, j: (0, 0)),
        ],
        out_specs=tile,
        out_shape=jax.ShapeDtypeStruct((M, N), F32),
        compiler_params=pltpu.CompilerParams(dimension_semantics=("parallel", "parallel")),
    )(pieces, w_t, res, after)


def _rstd(x):
    return lax.rsqrt(jnp.mean(x * x, axis=-1, keepdims=True) + NORM_EPS)


def _rms_bwd_math(dy, x, g):
    r = _rstd(x)
    xh = x * r
    dg = jnp.sum(dy * xh, axis=0, keepdims=True)
    dxh = dy * g
    dx = r * (dxh - xh * jnp.mean(dxh * xh, axis=-1, keepdims=True))
    return dx, dg


def _rms_fwd(x, g, name, tr=256):
    S, D = x.shape

    def body(x_ref, g_ref, o_ref):
        xv = x_ref[...]
        o_ref[...] = ((xv * _rstd(xv)) * g_ref[...]).astype(o_ref.dtype)

    return pl.pallas_call(
        body,
        name=name,
        grid=(S // tr,),
        in_specs=[pl.BlockSpec((tr, D), lambda i: (i, 0)), pl.BlockSpec((1, D), lambda i: (0, 0))],
        out_specs=pl.BlockSpec((tr, D), lambda i: (i, 0)),
        out_shape=jax.ShapeDtypeStruct((S, D), BF16),
        compiler_params=pltpu.CompilerParams(dimension_semantics=("parallel",)),
    )(x, g)


def _rms_bwd(dy, x, g, res, name, tr=256):
    S, D = x.shape

    def body(dy_ref, x_ref, g_ref, res_ref, dx_ref, dxb_ref, dg_ref):
        dx, dg = _rms_bwd_math(dy_ref[...], x_ref[...], g_ref[...])
        dx = dx + res_ref[...]
        dx_ref[...] = dx
        dxb_ref[...] = dx.astype(BF16)

        @pl.when(pl.program_id(0) == 0)
        def _():
            dg_ref[...] = dg

        @pl.when(pl.program_id(0) != 0)
        def _():
            dg_ref[...] += dg

    row = pl.BlockSpec((tr, D), lambda i: (i, 0))
    vec = pl.BlockSpec((1, D), lambda i: (0, 0))
    return pl.pallas_call(
        body,
        name=name,
        grid=(S // tr,),
        in_specs=[row, row, vec, row],
        out_specs=[row, row, vec],
        out_shape=[jax.ShapeDtypeStruct((S, D), F32), jax.ShapeDtypeStruct((S, D), BF16), jax.ShapeDtypeStruct((1, D), F32)],
        compiler_params=pltpu.CompilerParams(dimension_semantics=("arbitrary",)),
    )(dy, x, g, res)


def _mm_res_rms(a, b, res, g, name, tm=256):
    M, K = a.shape
    D = b.shape[1]

    def body(a_ref, b_ref, res_ref, g_ref, y_ref, h_ref):
        y = res_ref[...] + _dot(a_ref[...], b_ref[...])
        y_ref[...] = y
        h_ref[...] = ((y * _rstd(y)) * g_ref[...]).astype(BF16)

    row = pl.BlockSpec((tm, D), lambda i: (i, 0))
    return pl.pallas_call(
        body,
        name=name,
        grid=(M // tm,),
        in_specs=[pl.BlockSpec((tm, K), lambda i: (i, 0)), pl.BlockSpec((K, D), lambda i: (0, 0)), row, pl.BlockSpec((1, D), lambda i: (0, 0))],
        out_specs=[row, row],
        out_shape=[jax.ShapeDtypeStruct((M, D), F32), jax.ShapeDtypeStruct((M, D), BF16)],
        compiler_params=pltpu.CompilerParams(dimension_semantics=("parallel",)),
    )(a, b, res, g)


def _mm_rms_bwd(a, b, b_koff, dy_part, x, g, res, name, tm=256):
    M, K = a.shape
    D = x.shape[1]

    def body(a_ref, b_ref, dyp_ref, x_ref, g_ref, res_ref, dx_ref, dg_ref):
        dy = dyp_ref[...] + _dot(a_ref[...], b_ref[...])
        dx, dg = _rms_bwd_math(dy, x_ref[...], g_ref[...])
        dx_ref[...] = dx + res_ref[...]

        @pl.when(pl.program_id(0) == 0)
        def _():
            dg_ref[...] = dg

        @pl.when(pl.program_id(0) != 0)
        def _():
            dg_ref[...] += dg

    row = pl.BlockSpec((tm, D), lambda i: (i, 0))
    vec = pl.BlockSpec((1, D), lambda i: (0, 0))
    return pl.pallas_call(
        body,
        name=name,
        grid=(M // tm,),
        in_specs=[pl.BlockSpec((tm, K), lambda i: (i, 0)), pl.BlockSpec((K, D), lambda i: (b_koff, 0)), row, row, vec, row],
        out_specs=[row, vec],
        out_shape=[jax.ShapeDtypeStruct((M, D), F32), jax.ShapeDtypeStruct((1, D), F32)],
        compiler_params=pltpu.CompilerParams(dimension_semantics=("arbitrary",)),
    )(a, b, dy_part, x, g, res)


def _final_loss(x2, g, tgt, name, tr=256):
    S, D = x2.shape

    def body(x_ref, g_ref, t_ref, loss_ref, dx_ref, dxb_ref, dg_ref):
        xv, gv = x_ref[...], g_ref[...]
        y = (xv * _rstd(xv)) * gv
        e = y - t_ref[...]
        part = 0.5 * jnp.sum(jnp.mean(e * e, axis=-1, keepdims=True), axis=0, keepdims=True)
        dx, dg = _rms_bwd_math(e * (1.0 / D), xv, gv)
        dx_ref[...] = dx
        dxb_ref[...] = dx.astype(BF16)
        part = jnp.broadcast_to(part, (1, 128))

        @pl.when(pl.program_id(0) == 0)
        def _():
            dg_ref[...] = dg
            loss_ref[...] = part

        @pl.when(pl.program_id(0) != 0)
        def _():
            dg_ref[...] += dg
            loss_ref[...] += part

    row = pl.BlockSpec((tr, D), lambda i: (i, 0))
    vec = pl.BlockSpec((1, D), lambda i: (0, 0))
    return pl.pallas_call(
        body,
        name=name,
        grid=(S // tr,),
        in_specs=[row, vec, row],
        out_specs=[pl.BlockSpec((1, 128), lambda i: (0, 0)), row, row, vec],
        out_shape=[
            jax.ShapeDtypeStruct((1, 128), F32),
            jax.ShapeDtypeStruct((S, D), F32),
            jax.ShapeDtypeStruct((S, D), BF16),
            jax.ShapeDtypeStruct((1, D), F32),
        ],
        compiler_params=pltpu.CompilerParams(dimension_semantics=("arbitrary",)),
    )(x2, g, tgt)


def _rope_tables(S):
    pos = jnp.arange(S, dtype=F32)
    inv_freq = ROPE_THETA ** (-jnp.arange(0, QK_ROPE, 2, dtype=F32) / QK_ROPE)
    ang = pos[:, None] * inv_freq[None, :]
    cos, sin = jnp.cos(ang), jnp.sin(ang)
    zero = jnp.zeros((S, 128 - QK_ROPE), F32)
    return jnp.concatenate([cos, cos, zero], axis=1), jnp.concatenate([-sin, sin, zero], axis=1)


def _rope_tile(x, cos_t, sin_t):
    lane = lax.broadcasted_iota(jnp.int32, x.shape, 1)
    partner = jnp.where(lane < QK_ROPE // 2, pltpu.roll(x, 128 - QK_ROPE // 2, 1), pltpu.roll(x, QK_ROPE // 2, 1))
    return x * cos_t + partner * sin_t


def _mla_prep1(lat, gq, gkv, cos_t, sin_t, name, tr=256):
    S = lat.shape[0]

    def body(lat_ref, gq_ref, gkv_ref, cos_ref, sin_ref, cq_ref, ckv_ref, kpe_ref):
        cq = lat_ref[:, :Q_LORA]
        ckv = lat_ref[:, Q_LORA : Q_LORA + KV_LORA]
        cq_ref[...] = ((cq * _rstd(cq)) * gq_ref[...]).astype(BF16)
        ckv_ref[...] = ((ckv * _rstd(ckv)) * gkv_ref[...]).astype(BF16)
        kpe_ref[...] = _rope_tile(lat_ref[:, Q_LORA + KV_LORA :], cos_ref[...], sin_ref[...]).astype(BF16)

    def row(n):
        return pl.BlockSpec((tr, n), lambda i: (i, 0))

    def vec(n):
        return pl.BlockSpec((1, n), lambda i: (0, 0))

    return pl.pallas_call(
        body,
        name=name,
        grid=(S // tr,),
        in_specs=[row(LAT_PAD), vec(Q_LORA), vec(KV_LORA), row(128), row(128)],
        out_specs=[row(Q_LORA), row(KV_LORA), row(128)],
        out_shape=[
            jax.ShapeDtypeStruct((S, Q_LORA), BF16),
            jax.ShapeDtypeStruct((S, KV_LORA), BF16),
            jax.ShapeDtypeStruct((S, 128), BF16),
        ],
        compiler_params=pltpu.CompilerParams(dimension_semantics=("parallel",)),
    )(lat, gq, gkv, cos_t, sin_t)


def _mla_prep2(q_raw, kv, kpe, cos_t, sin_t, name, tr=256):
    S = q_raw.shape[0]
    W = MLA_HEADS * HEAD_PAD

    def body(q_ref, kv_ref, kpe_ref, cos_ref, sin_ref, qa_ref, ka_ref):
        cos_v, sin_v, kpe_v = cos_ref[...], sin_ref[...], kpe_ref[...]
        for h in range(MLA_HEADS):
            lo = h * HEAD_PAD
            qa_ref[:, lo : lo + 128] = q_ref[:, lo : lo + 128].astype(BF16)
            qa_ref[:, lo + 128 : lo + 256] = _rope_tile(q_ref[:, lo + 128 : lo + 256], cos_v, sin_v).astype(BF16)
            ka_ref[:, lo : lo + 128] = kv_ref[:, lo : lo + 128]
            ka_ref[:, lo + 128 : lo + 256] = kpe_v

    def row(n):
        return pl.BlockSpec((tr, n), lambda i: (i, 0))

    return pl.pallas_call(
        body,
        name=name,
        grid=(S // tr,),
        in_specs=[row(W), row(W), row(128), row(128), row(128)],
        out_specs=[row(W), row(W)],
        out_shape=[jax.ShapeDtypeStruct((S, W), BF16), jax.ShapeDtypeStruct((S, W), BF16)],
        compiler_params=pltpu.CompilerParams(dimension_semantics=("parallel",)),
    )(q_raw, kv, kpe, cos_t, sin_t)


def _mla_post(dq_att, dk_att, dv, cos_t, sin_t, name, tr=256):
    S = dq_att.shape[0]
    W = MLA_HEADS * HEAD_PAD

    def body(dq_ref, dk_ref, dv_ref, cos_ref, sin_ref, dqr_ref, dkv_ref, dkpe_ref):
        cos_v, nsin_v = cos_ref[...], -sin_ref[...]
        kpe = jnp.zeros((tr, 128), F32)
        for h in range(MLA_HEADS):
            lo = h * HEAD_PAD
            dqr_ref[:, lo : lo + 128] = dq_ref[:, lo : lo + 128].astype(BF16)
            dqr_ref[:, lo + 128 : lo + 256] = _rope_tile(dq_ref[:, lo + 128 : lo + 256], cos_v, nsin_v).astype(BF16)
            dkv_ref[:, lo : lo + 128] = dk_ref[:, lo : lo + 128].astype(BF16)
            dkv_ref[:, lo + 128 : lo + 256] = dv_ref[:, h * 128 : (h + 1) * 128].astype(BF16)
            kpe = kpe + dk_ref[:, lo + 128 : lo + 256]
        dkpe_ref[...] = _rope_tile(kpe, cos_v, nsin_v)

    def row(n):
        return pl.BlockSpec((tr, n), lambda i: (i, 0))

    return pl.pallas_call(
        body,
        name=name,
        grid=(S // tr,),
        in_specs=[row(W), row(W), row(MLA_HEADS * V_HEAD), row(128), row(128)],
        out_specs=[row(W), row(W), row(128)],
        out_shape=[jax.ShapeDtypeStruct((S, W), BF16), jax.ShapeDtypeStruct((S, W), BF16), jax.ShapeDtypeStruct((S, 128), F32)],
        compiler_params=pltpu.CompilerParams(dimension_semantics=("parallel",)),
    )(dq_att, dk_att, dv, cos_t, sin_t)


def _lat_bwd(dcqn, dckvn, dkpe, lat, gq, gkv, name, tr=256):
    S = lat.shape[0]

    def body(dcq_ref, dckv_ref, dkpe_ref, lat_ref, gq_ref, gkv_ref, dlat_ref, dgq_ref, dgkv_ref):
        dq, dgq = _rms_bwd_math(dcq_ref[...], lat_ref[:, :Q_LORA], gq_ref[...])
        dkv, dgkv = _rms_bwd_math(dckv_ref[...], lat_ref[:, Q_LORA : Q_LORA + KV_LORA], gkv_ref[...])
        dlat_ref[:, :Q_LORA] = dq.astype(BF16)
        dlat_ref[:, Q_LORA : Q_LORA + KV_LORA] = dkv.astype(BF16)
        dlat_ref[:, Q_LORA + KV_LORA :] = dkpe_ref[...].astype(BF16)

        @pl.when(pl.program_id(0) == 0)
        def _():
            dgq_ref[...] = dgq
            dgkv_ref[...] = dgkv

        @pl.when(pl.program_id(0) != 0)
        def _():
            dgq_ref[...] += dgq
            dgkv_ref[...] += dgkv

    def row(n):
        return pl.BlockSpec((tr, n), lambda i: (i, 0))

    def vec(n):
        return pl.BlockSpec((1, n), lambda i: (0, 0))

    return pl.pallas_call(
        body,
        name=name,
        grid=(S // tr,),
        in_specs=[row(Q_LORA), row(KV_LORA), row(128), row(LAT_PAD), vec(Q_LORA), vec(KV_LORA)],
        out_specs=[row(LAT_PAD), vec(Q_LORA), vec(KV_LORA)],
        out_shape=[
            jax.ShapeDtypeStruct((S, LAT_PAD), BF16),
            jax.ShapeDtypeStruct((1, Q_LORA), F32),
            jax.ShapeDtypeStruct((1, KV_LORA), F32),
        ],
        compiler_params=pltpu.CompilerParams(dimension_semantics=("arbitrary",)),
    )(dcqn, dckvn, dkpe, lat, gq, gkv)


MLA_SCALE = (QK_NOPE + QK_ROPE) ** -0.5
LOG2E = 1.4426950408889634
MLA_C2 = MLA_SCALE * LOG2E
FLASH_T = 1024


def _causal_pairs(n, by_key):
    pairs = [(i, j) for j in range(n) for i in range(j, n)] if by_key else [(i, j) for i in range(n) for j in range(i + 1)]
    return jnp.asarray([p[0] for p in pairs], jnp.int32), jnp.asarray([p[1] for p in pairs], jnp.int32)


def _causal_mask(shape, shift, keys_first=False):
    q_axis, k_axis = (1, 0) if keys_first else (0, 1)
    return lax.broadcasted_iota(jnp.int32, shape, k_axis) <= lax.broadcasted_iota(jnp.int32, shape, q_axis) + shift


def _lanes(x, n):
    return jnp.tile(x, (1, n // 128))


def _flash_grid(npairs, in_specs, out_specs, scratch):
    return pltpu.PrefetchScalarGridSpec(
        num_scalar_prefetch=2, grid=(MLA_HEADS, npairs), in_specs=in_specs, out_specs=out_specs, scratch_shapes=scratch
    )


def _flash2_fwd(q_att, k_att, kv, name, t=FLASH_T):
    S = q_att.shape[0]
    half = t // 2
    qi_tab, kj_tab = _causal_pairs(S // t, by_key=False)

    def body(qi_ref, kj_ref, q_ref, k_ref, v_ref, o_ref, lse_ref, m_sc, l_sc, acc_sc):
        step = pl.program_id(1)
        qi, kj = qi_ref[step], kj_ref[step]

        @pl.when(kj == 0)
        def _():
            m_sc[...] = jnp.full((t, 128), NEG, F32)
            l_sc[...] = jnp.zeros((t, 128), F32)
            acc_sc[...] = jnp.zeros((t, V_HEAD), F32)

        def update(rows, s, v):
            m_prev = m_sc[rows, :]
            m_new = jnp.maximum(m_prev, jnp.max(s, axis=1, keepdims=True))
            p = jnp.exp2((s - _lanes(m_new, s.shape[1])) * MLA_C2)
            alpha = jnp.exp2((m_prev - m_new) * MLA_C2)
            l_sc[rows, :] = alpha * l_sc[rows, :] + jnp.sum(p, axis=1, keepdims=True)
            acc_sc[rows, :] = alpha * acc_sc[rows, :] + _dot(p.astype(BF16), v)
            m_sc[rows, :] = m_new

        @pl.when(kj < qi)
        def _():
            update(slice(0, t), _dot(q_ref[...], k_ref[...], NT), v_ref[...])

        @pl.when(kj == qi)
        def _():
            top = _dot(q_ref[:half, :], k_ref[:half, :], NT)
            update(slice(0, half), jnp.where(_causal_mask(top.shape, 0), top, NEG), v_ref[:half, :])
            bot = _dot(q_ref[half:, :], k_ref[...], NT)
            update(slice(half, t), jnp.where(_causal_mask(bot.shape, half), bot, NEG), v_ref[...])
            l = l_sc[...]
            o_ref[...] = acc_sc[...] / l
            lse_ref[0] = m_sc[...] * MLA_SCALE + jnp.log(l)

    return pl.pallas_call(
        body,
        name=name,
        grid_spec=_flash_grid(
            qi_tab.shape[0],
            [
                pl.BlockSpec((t, HEAD_PAD), lambda h, p, qi, kj: (qi[p], h)),
                pl.BlockSpec((t, HEAD_PAD), lambda h, p, qi, kj: (kj[p], h)),
                pl.BlockSpec((t, V_HEAD), lambda h, p, qi, kj: (kj[p], 2 * h + 1)),
            ],
            [
                pl.BlockSpec((t, V_HEAD), lambda h, p, qi, kj: (qi[p], h)),
                pl.BlockSpec((1, t, 128), lambda h, p, qi, kj: (h, qi[p], 0)),
            ],
            [pltpu.VMEM((t, 128), F32), pltpu.VMEM((t, 128), F32), pltpu.VMEM((t, V_HEAD), F32)],
        ),
        out_shape=[jax.ShapeDtypeStruct((S, MLA_HEADS * V_HEAD), F32), jax.ShapeDtypeStruct((MLA_HEADS, S, 128), F32)],
        compiler_params=pltpu.CompilerParams(dimension_semantics=("parallel", "arbitrary")),
    )(qi_tab, kj_tab, q_att, k_att, kv)


def _flash_delta(do, o, name, tr=512):
    S = o.shape[0]

    def body(do_ref, o_ref, d_ref):
        lane = lax.broadcasted_iota(jnp.int32, (tr, 128), 1)
        acc = jnp.zeros((tr, 128), F32)
        for h in range(MLA_HEADS):
            sl = slice(h * V_HEAD, (h + 1) * V_HEAD)
            acc = jnp.where(lane == h, jnp.sum(do_ref[:, sl].astype(F32) * o_ref[:, sl], axis=1, keepdims=True), acc)
        d_ref[...] = acc

    row = pl.BlockSpec((tr, MLA_HEADS * V_HEAD), lambda i: (i, 0))
    return pl.pallas_call(
        body,
        name=name,
        grid=(S // tr,),
        in_specs=[row, row],
        out_specs=pl.BlockSpec((tr, 128), lambda i: (i, 0)),
        out_shape=jax.ShapeDtypeStruct((S, 128), F32),
        compiler_params=pltpu.CompilerParams(dimension_semantics=("parallel",)),
    )(do, o)


def _flash2_bwd(q_att, k_att, kv, do, lse_row, delta_row, name, t=FLASH_T):
    S = q_att.shape[0]
    n = S // t
    qi_tab, kj_tab = _causal_pairs(n, by_key=True)
    last = qi_tab.shape[0] - 1
    half = t // 2

    def body(qi_ref, kj_ref, q_ref, k_ref, v_ref, do_ref, lse_ref, dl_ref, dq_ref, dk_ref, dv_ref, dk_sc, dv_sc):
        step = pl.program_id(1)
        qi, kj = qi_ref[step], kj_ref[step]

        @pl.when(step == 0)
        def _():
            dq_ref[...] = jnp.zeros((S, HEAD_PAD), F32)

        def update(k0, q0, st):
            nk, nq = st.shape
            kr, qr = slice(k0, k0 + nk), slice(q0, q0 + nq)
            q, do_v = q_ref[qr, :], do_ref[qr, :]
            pt = jnp.exp2(st * MLA_C2 - lse_ref[0][:, qr] * LOG2E)
            dv_sc[kr, :] += _dot(pt.astype(BF16), do_v)
            dpt = _dot(v_ref[kr, :], do_v, NT)
            dst = (pt * (dpt - dl_ref[0][:, qr])).astype(BF16)
            dk_sc[kr, :] += _dot(dst, q)
            rows = pl.ds(pl.multiple_of(qi * t + q0, half), nq)
            dq_ref[rows, :] += _dot(dst, k_ref[kr, :], TN)

        @pl.when(qi == kj)
        def _():
            dk_sc[...] = jnp.zeros((t, HEAD_PAD), F32)
            dv_sc[...] = jnp.zeros((t, V_HEAD), F32)
            top = _dot(k_ref[:half, :], q_ref[...], NT)
            update(0, 0, jnp.where(_causal_mask(top.shape, 0, keys_first=True), top, NEG))
            bot = _dot(k_ref[half:, :], q_ref[half:, :], NT)
            update(half, half, jnp.where(_causal_mask(bot.shape, 0, keys_first=True), bot, NEG))

        @pl.when(qi > kj)
        def _():
            update(0, 0, _dot(k_ref[...], q_ref[...], NT))

        @pl.when(qi == n - 1)
        def _():
            dk_ref[...] = dk_sc[...] * MLA_SCALE
            dv_ref[...] = dv_sc[...]

        @pl.when(step == last)
        def _():
            dq_ref[...] = dq_ref[...] * MLA_SCALE

    qrow = lambda h, p, qi, kj: (qi[p], h)
    krow = lambda h, p, qi, kj: (kj[p], h)
    stat = pl.BlockSpec((1, 1, t), lambda h, p, qi, kj: (h, 0, qi[p]))
    return pl.pallas_call(
        body,
        name=name,
        grid_spec=_flash_grid(
            qi_tab.shape[0],
            [
                pl.BlockSpec((t, HEAD_PAD), qrow),
                pl.BlockSpec((t, HEAD_PAD), krow),
                pl.BlockSpec((t, V_HEAD), lambda h, p, qi, kj: (kj[p], 2 * h + 1)),
                pl.BlockSpec((t, V_HEAD), qrow),
                stat,
                stat,
            ],
            [
                pl.BlockSpec((S, HEAD_PAD), lambda h, p, qi, kj: (0, h)),
                pl.BlockSpec((t, HEAD_PAD), krow),
                pl.BlockSpec((t, V_HEAD), krow),
            ],
            [pltpu.VMEM((t, HEAD_PAD), F32), pltpu.VMEM((t, V_HEAD), F32)],
        ),
        out_shape=[
            jax.ShapeDtypeStruct((S, MLA_HEADS * HEAD_PAD), F32),
            jax.ShapeDtypeStruct((S, MLA_HEADS * HEAD_PAD), F32),
            jax.ShapeDtypeStruct((S, MLA_HEADS * V_HEAD), F32),
        ],
        compiler_params=pltpu.CompilerParams(dimension_semantics=("parallel", "arbitrary")),
    )(qi_tab, kj_tab, q_att, k_att, kv, do, lse_row, delta_row)


DIL_SCALE = DIL_HD**-0.5


def _dil_bias():
    slopes = 2.0 ** (-ALIBI_MAX_BIAS * np.arange(1, DIL_HEADS + 1, dtype=np.float64) / DIL_HEADS)
    slopes = slopes.astype(np.float32).reshape(DIL_GROUPS, DIL_HG)
    p = np.arange(DIL_BLK)[:, None]
    kidx = np.arange(2 * DIL_BLK)[None, :]
    j = p + DIL_BLK - kidx
    out = np.zeros((DIL_GROUPS, DIL_HG, DIL_BLK, 2 * DIL_BLK), np.float32)
    for g, (window, dil) in enumerate(DIL_PATTERNS):
        valid = (j >= 0) & (j <= window // dil)
        for h in range(DIL_HG):
            alibi = -slopes[g, h] * (dil * j).astype(np.float32)
            out[g, h] = np.where(valid, alibi, np.float32(NEG))
    return jnp.asarray(out)


DIL_UNROLL = 4


def _unrolled_loop(lo, hi, fn, unroll=DIL_UNROLL):
    groups = (hi - lo) // unroll
    done = lo
    if groups > 1:

        def step(i, carry):
            for u in range(unroll):
                fn(lo + i * unroll + u)
            return carry

        lax.fori_loop(0, groups, step, 0)
        done = lo + groups * unroll
    for n in range(done, hi):
        fn(n)


def _dil_rows(r, n, count, dil):
    if dil == 1:
        if isinstance(n, int):
            return slice(n * DIL_BLK, (n + count) * DIL_BLK)
        return pl.ds(pl.multiple_of(n * DIL_BLK, DIL_BLK), count * DIL_BLK)
    return pl.ds(n * DIL_BLK * dil + r, count * DIL_BLK, stride=dil)


def _dil_each_block(S, dil, block):
    nb = S // dil // DIL_BLK
    if dil == 1:
        block(0, 0, True)
        _unrolled_loop(1, nb, lambda n: block(0, n, False))
    else:
        for r in range(dil):
            for n in range(nb):
                block(r, n, n == 0)


def _dil_col(g, part, h):
    return (g * 3 + part) * DIL_HG + h


def _dil_fwd_group(dqkv, bias_g, g, dil, name):
    S = dqkv.shape[0]

    def body(bias_ref, q_ref, k_ref, v_ref, o_ref, lse_ref):
        def block(r, n, first):
            cur = _dil_rows(r, n, 1, dil)
            both = cur if first else _dil_rows(r, n - 1, 2, dil)
            b = bias_ref[0][:, DIL_BLK:] if first else bias_ref[0]
            q, kk, vv = q_ref[cur, :].astype(BF16), k_ref[both, :].astype(BF16), v_ref[both, :].astype(BF16)
            s = _dot(q, kk, NT) * DIL_SCALE + b
            m = jnp.max(s, axis=1, keepdims=True)
            e = jnp.exp(s - m)
            l = jnp.sum(e, axis=1, keepdims=True)
            p = e * (1.0 / l)
            o_ref[cur, :] = _dot(p.astype(BF16), vv)
            lse_ref[cur, :] = jnp.broadcast_to(m + jnp.log(l), (DIL_BLK, 128))

        _dil_each_block(S, dil, block)

    def col(part):
        return pl.BlockSpec((S, DIL_HD), lambda h: (0, _dil_col(g, part, h)))

    out = pl.BlockSpec((S, DIL_HD), lambda h: (0, h))
    return pl.pallas_call(
        body,
        name=name,
        grid=(DIL_HG,),
        in_specs=[pl.BlockSpec((1, DIL_BLK, 2 * DIL_BLK), lambda h: (h, 0, 0)), col(0), col(1), col(2)],
        out_specs=[out, out],
        out_shape=[jax.ShapeDtypeStruct((S, DIL_OUT), F32), jax.ShapeDtypeStruct((S, DIL_OUT), F32)],
        compiler_params=pltpu.CompilerParams(dimension_semantics=("parallel",)),
    )(bias_g, dqkv, dqkv, dqkv)


def _dil_combine(os_, ls_, name, tr=512):
    S = os_[0].shape[0]

    def body(o0, o1, o2, l0, l1, l2, out_ref, lse_ref):
        a, b, c = l0[...], l1[...], l2[...]
        m = jnp.maximum(jnp.maximum(a, b), c)
        ea, eb, ec = jnp.exp(a - m), jnp.exp(b - m), jnp.exp(c - m)
        den = ea + eb + ec
        inv = 1.0 / den
        out_ref[...] = (ea * inv) * o0[...] + (eb * inv) * o1[...] + (ec * inv) * o2[...]
        lse_ref[...] = m + jnp.log(den)

    row = pl.BlockSpec((tr, DIL_OUT), lambda i: (i, 0))
    return pl.pallas_call(
        body,
        name=name,
        grid=(S // tr,),
        in_specs=[row] * 6,
        out_specs=[row, row],
        out_shape=[jax.ShapeDtypeStruct((S, DIL_OUT), F32)] * 2,
        compiler_params=pltpu.CompilerParams(dimension_semantics=("parallel",)),
    )(*os_, *ls_)


def _dil_rowdot(dod, od, name, tr=512):
    S = dod.shape[0]

    def body(d_ref, o_ref, dd_ref):
        for h in range(DIL_HG):
            sl = slice(h * 128, (h + 1) * 128)
            sm = jnp.sum(d_ref[:, sl] * o_ref[:, sl], axis=1, keepdims=True)
            dd_ref[:, sl] = jnp.broadcast_to(sm, (tr, 128))

    row = pl.BlockSpec((tr, DIL_OUT), lambda i: (i, 0))
    return pl.pallas_call(
        body,
        name=name,
        grid=(S // tr,),
        in_specs=[row, row],
        out_specs=row,
        out_shape=jax.ShapeDtypeStruct((S, DIL_OUT), F32),
        compiler_params=pltpu.CompilerParams(dimension_semantics=("parallel",)),
    )(dod, od)


def _dil_bwd_group(dqkv, bias_g, dod, dd, lse, grads, g, dil, name):
    S = dqkv.shape[0]

    def body(bias_ref, q_ref, k_ref, v_ref, do_ref, dd_ref, lse_ref, _, out_ref):
        out_ref[1] = jnp.zeros((S, DIL_HD), F32)
        out_ref[2] = jnp.zeros((S, DIL_HD), F32)

        def block(r, n, first):
            cur = _dil_rows(r, n, 1, dil)
            both = cur if first else _dil_rows(r, n - 1, 2, dil)
            b = bias_ref[0][:, DIL_BLK:] if first else bias_ref[0]
            q, kk, vv = q_ref[cur, :].astype(BF16), k_ref[both, :].astype(BF16), v_ref[both, :].astype(BF16)
            do = do_ref[cur, :].astype(BF16)
            s = _dot(q, kk, NT) * DIL_SCALE + b
            p = jnp.exp(s - lse_ref[cur, 0:1])
            dp = _dot(do, vv, NT)
            ds = ((p * (dp - dd_ref[cur, 0:1])) * DIL_SCALE).astype(BF16)
            out_ref[0, cur, :] = _dot(ds, kk)
            out_ref[1, both, :] += _dot(ds, q, TN)
            out_ref[2, both, :] += _dot(p.astype(BF16), do, TN)

        _dil_each_block(S, dil, block)

    def col(part):
        return pl.BlockSpec((S, DIL_HD), lambda h: (0, _dil_col(g, part, h)))

    nat = pl.BlockSpec((S, DIL_HD), lambda h: (0, h))
    return pl.pallas_call(
        body,
        name=name,
        grid=(DIL_HG,),
        in_specs=[pl.BlockSpec((1, DIL_BLK, 2 * DIL_BLK), lambda h: (h, 0, 0)), col(0), col(1), col(2), nat, nat, nat, ANY],
        out_specs=pl.BlockSpec((3, S, DIL_HD), lambda h: (g, 0, h)),
        out_shape=jax.ShapeDtypeStruct(grads.shape, F32),
        input_output_aliases={7: 0},
        compiler_params=pltpu.CompilerParams(dimension_semantics=("parallel",)),
    )(bias_g, dqkv, dqkv, dqkv, dod, dd, lse, grads)


def _merge_fwd(gates, o_a, o_b, name, tr=256):
    S = o_a.shape[0]

    def body(ga_ref, gb_ref, oa_ref, ob_ref, m_ref):
        m_ref[...] = (ga_ref[...] * oa_ref[...] + gb_ref[...] * ob_ref[...]).astype(BF16)

    row = pl.BlockSpec((tr, D_MODEL), lambda i: (i, 0))
    return pl.pallas_call(
        body,
        name=name,
        grid=(S // tr,),
        in_specs=[row, pl.BlockSpec((tr, D_MODEL), lambda i: (i, 1)), row, row],
        out_specs=row,
        out_shape=jax.ShapeDtypeStruct((S, D_MODEL), BF16),
        compiler_params=pltpu.CompilerParams(dimension_semantics=("parallel",)),
    )(gates, gates, o_a, o_b)


def _merge_bwd(dmrg, gates, o_a, o_b, name, tr=256):
    S = o_a.shape[0]

    def body(dm_ref, ga_ref, gb_ref, oa_ref, ob_ref, doa_ref, dob_ref, dga_ref, dgb_ref, dba_ref, dbb_ref):
        dm, ga, gb = dm_ref[...], ga_ref[...], gb_ref[...]
        doa_ref[...] = (dm * ga).astype(BF16)
        dob_ref[...] = (dm * gb).astype(BF16)
        dga = (dm * oa_ref[...]) * (ga * (1.0 - ga))
        dgb = (dm * ob_ref[...]) * (gb * (1.0 - gb))
        dga_ref[...] = dga.astype(BF16)
        dgb_ref[...] = dgb.astype(BF16)
        sa = jnp.sum(dga, axis=0, keepdims=True)
        sb = jnp.sum(dgb, axis=0, keepdims=True)

        @pl.when(pl.program_id(0) == 0)
        def _():
            dba_ref[...] = sa
            dbb_ref[...] = sb

        @pl.when(pl.program_id(0) != 0)
        def _():
            dba_ref[...] += sa
            dbb_ref[...] += sb

    row = pl.BlockSpec((tr, D_MODEL), lambda i: (i, 0))
    row1 = pl.BlockSpec((tr, D_MODEL), lambda i: (i, 1))
    vec = pl.BlockSpec((1, D_MODEL), lambda i: (0, 0))
    outs = pl.pallas_call(
        body,
        name=name,
        grid=(S // tr,),
        in_specs=[row, row, row1, row, row],
        out_specs=[row, row, row, row, vec, vec],
        out_shape=[jax.ShapeDtypeStruct((S, D_MODEL), BF16)] * 4 + [jax.ShapeDtypeStruct((1, D_MODEL), F32)] * 2,
        compiler_params=pltpu.CompilerParams(dimension_semantics=("arbitrary",)),
    )(dmrg, gates, gates, o_a, o_b)
    return outs


CONV_TR = 512
CONV_TC = 512
N_FFC = D_FF_PAD // CONV_TC


def _conv_taps(x, before, w_ref, b_ref):
    x0 = jnp.concatenate([before, x], axis=0)
    x1 = pltpu.roll(x0, 1, 0)
    x2 = pltpu.roll(x0, 2, 0)
    u = ((b_ref[...] + w_ref[0:1, :] * x2) + w_ref[1:2, :] * x1) + w_ref[2:3, :] * x0
    return u, x0, x1, x2


def _prev_halo(tr):
    return lambda i, j: (jnp.maximum(i * (tr // 8) - 1, 0), j)


def _ffn_fwd(u0, cw, cb, name):
    S = u0.shape[0]
    tr, tc = CONV_TR, CONV_TC

    def body(up_ref, gt_ref, hup_ref, hgt_ref, wu_ref, wg_ref, bu_ref, bg_ref, a_ref):
        live = (pl.program_id(0) > 0).astype(F32)
        up = _conv_taps(up_ref[...], hup_ref[...] * live, wu_ref, bu_ref)[0][8:]
        gt = _conv_taps(gt_ref[...], hgt_ref[...] * live, wg_ref, bg_ref)[0][8:]
        a_ref[...] = ((gt * jax.nn.sigmoid(gt)) * up).astype(BF16)

    return pl.pallas_call(
        body,
        name=name,
        grid=(S // tr, N_FFC),
        in_specs=[
            pl.BlockSpec((tr, tc), lambda i, j: (i, j)),
            pl.BlockSpec((tr, tc), lambda i, j: (i, j + N_FFC)),
            pl.BlockSpec((8, tc), _prev_halo(tr)),
            pl.BlockSpec((8, tc), lambda i, j: (jnp.maximum(i * (tr // 8) - 1, 0), j + N_FFC)),
            pl.BlockSpec((8, tc), lambda i, j: (0, j)),
            pl.BlockSpec((8, tc), lambda i, j: (0, j + N_FFC)),
            pl.BlockSpec((1, tc), lambda i, j: (0, j)),
            pl.BlockSpec((1, tc), lambda i, j: (0, j + N_FFC)),
        ],
        out_specs=pl.BlockSpec((tr, tc), lambda i, j: (i, j)),
        out_shape=jax.ShapeDtypeStruct((S, D_FF_PAD), BF16),
        compiler_params=pltpu.CompilerParams(dimension_semantics=("parallel", "parallel")),
    )(u0, u0, u0, u0, cw, cw, cb, cb)


def _ffn_bwd(u0, da, cw, cb, name):
    S = u0.shape[0]
    tr, tc = CONV_TR, CONV_TC
    nrow, te = S // tr, tr + 8

    def body(up_ref, gt_ref, hup_ref, hgt_ref, nup_ref, ngt_ref, da_ref, nda_ref, wu_ref, wg_ref, bu_ref, bg_ref, du0_ref, dcw_ref, dcb_ref):
        i = pl.program_id(1)
        prev_live = (i > 0).astype(F32)
        next_live = (i < nrow - 1).astype(F32)

        def conv(x_ref, nx_ref, h_ref, w_ref, b_ref):
            x = jnp.concatenate([x_ref[...], nx_ref[...] * next_live], axis=0)
            return [t[8:] for t in _conv_taps(x, h_ref[...] * prev_live, w_ref, b_ref)]

        up, xu0, xu1, xu2 = conv(up_ref, nup_ref, hup_ref, wu_ref, bu_ref)
        gt, xg0, xg1, xg2 = conv(gt_ref, ngt_ref, hgt_ref, wg_ref, bg_ref)
        da_v = jnp.concatenate([da_ref[...], nda_ref[...] * next_live], axis=0)
        sg = jax.nn.sigmoid(gt)
        d_up = da_v * (gt * sg)
        d_gt = (da_v * up) * (sg * (1.0 + gt * (1.0 - sg)))
        tap = lax.broadcasted_iota(jnp.int32, (8, tc), 0)

        def finish(half, du, x0, x1, x2, w_ref):
            n1 = pltpu.roll(du, te - 1, 0)
            n2 = pltpu.roll(du, te - 2, 0)
            du0 = (w_ref[2:3, :] * du + w_ref[1:2, :] * n1) + w_ref[0:1, :] * n2
            du0_ref[half] = du0[:tr].astype(BF16)
            d = du[:tr]
            dcw = jnp.where(
                tap == 0,
                jnp.sum(d * x2[:tr], axis=0, keepdims=True),
                jnp.where(tap == 1, jnp.sum(d * x1[:tr], axis=0, keepdims=True), jnp.where(tap == 2, jnp.sum(d * x0[:tr], axis=0, keepdims=True), 0.0)),
            )
            dcb = jnp.sum(d, axis=0, keepdims=True)

            @pl.when(i == 0)
            def _():
                dcw_ref[half] = dcw
                dcb_ref[half] = dcb

            @pl.when(i != 0)
            def _():
                dcw_ref[half] += dcw
                dcb_ref[half] += dcb

        finish(0, d_up, xu0, xu1, xu2, wu_ref)
        finish(1, d_gt, xg0, xg1, xg2, wg_ref)

    def prev8(off):
        return pl.BlockSpec((8, tc), lambda j, i: (jnp.maximum(i * (tr // 8) - 1, 0), j + off))

    def next8(off):
        return pl.BlockSpec((8, tc), lambda j, i: (jnp.minimum((i + 1) * (tr // 8), S // 8 - 1), j + off))

    return pl.pallas_call(
        body,
        name=name,
        grid=(N_FFC, nrow),
        in_specs=[
            pl.BlockSpec((tr, tc), lambda j, i: (i, j)),
            pl.BlockSpec((tr, tc), lambda j, i: (i, j + N_FFC)),
            prev8(0),
            prev8(N_FFC),
            next8(0),
            next8(N_FFC),
            pl.BlockSpec((tr, tc), lambda j, i: (i, j)),
            next8(0),
            pl.BlockSpec((8, tc), lambda j, i: (0, j)),
            pl.BlockSpec((8, tc), lambda j, i: (0, j + N_FFC)),
            pl.BlockSpec((1, tc), lambda j, i: (0, j)),
            pl.BlockSpec((1, tc), lambda j, i: (0, j + N_FFC)),
        ],
        out_specs=[
            pl.BlockSpec((2, tr, tc), lambda j, i: (0, i, j)),
            pl.BlockSpec((2, 8, tc), lambda j, i: (0, 0, j)),
            pl.BlockSpec((2, 1, tc), lambda j, i: (0, 0, j)),
        ],
        out_shape=[
            jax.ShapeDtypeStruct((2, S, D_FF_PAD), BF16),
            jax.ShapeDtypeStruct((2, 8, D_FF_PAD), F32),
            jax.ShapeDtypeStruct((2, 1, D_FF_PAD), F32),
        ],
        compiler_params=pltpu.CompilerParams(dimension_semantics=("parallel", "arbitrary")),
    )(u0, u0, u0, u0, u0, u0, da, da, cw, cw, cb, cb)


ADAMW_BLOCK_BYTES = 3 << 20


def _adamw(w, g, m, v, name):
    R, C = w.shape
    fits = [t for t in range(8, R + 1, 8) if R % t == 0 and t * C * 4 <= ADAMW_BLOCK_BYTES]
    tr = max(fits) if fits else R

    def body(w_ref, g_ref, m_ref, v_ref, d_ref, nm_ref, nv_ref):
        gv = g_ref[...]
        nm = ADAM_B1 * m_ref[...] + (1.0 - ADAM_B1) * gv
        nv = ADAM_B2 * v_ref[...] + (1.0 - ADAM_B2) * (gv * gv)
        m_hat = nm / (1.0 - ADAM_B1**ADAM_STEP)
        v_hat = nv / (1.0 - ADAM_B2**ADAM_STEP)
        d_ref[...] = -ADAM_LR * (m_hat / (jnp.sqrt(v_hat) + ADAM_EPS) + ADAM_WD * w_ref[...])
        nm_ref[...] = nm
        nv_ref[...] = nv

    blk = pl.BlockSpec((tr, C), lambda i: (i, 0))
    return pl.pallas_call(
        body,
        name=name,
        grid=(R // tr,),
        in_specs=[blk] * 4,
        out_specs=[blk] * 3,
        out_shape=[jax.ShapeDtypeStruct((R, C), F32)] * 3,
        compiler_params=pltpu.CompilerParams(dimension_semantics=("parallel",)),
    )(w, g, m, v)


ANY = pl.BlockSpec(memory_space=pl.ANY)


def _row_tile(rows):
    return max(t for t in range(16, 353, 16) if rows % t == 0)


def _pair_add(g, recv, core, name):
    _, R, C = g.shape
    tr = _row_tile(R)

    def body(core_ref, g_ref, r_ref, o_ref):
        o_ref[...] = (g_ref[...].astype(F32) + r_ref[...].astype(F32)).astype(o_ref.dtype)

    return pl.pallas_call(
        body,
        name=name,
        grid_spec=pltpu.PrefetchScalarGridSpec(
            num_scalar_prefetch=1,
            grid=(N_CHIP, R // tr),
            in_specs=[
                pl.BlockSpec((1, tr, C), lambda k, i, core_ref: (2 * k + core_ref[0], i, 0)),
                pl.BlockSpec((1, tr, C), lambda k, i, core_ref: (k, i, 0)),
            ],
            out_specs=pl.BlockSpec((1, tr, C), lambda k, i, core_ref: (k, i, 0)),
        ),
        out_shape=jax.ShapeDtypeStruct((N_CHIP, R, C), g.dtype),
        compiler_params=pltpu.CompilerParams(dimension_semantics=("parallel", "parallel")),
    )(core, g, recv)


HBM = pl.BlockSpec(memory_space=pltpu.HBM)
SEM = pl.BlockSpec(memory_space=pltpu.SEMAPHORE)
EFFECT = pltpu.SideEffectType.DATAFLOW_SIDE_EFFECTING
RELATIONS = tuple((dx, dy, dc) for dx in (0, 1) for dy in (0, 1) for dc in (0, 1))[1:]


def _related(rel):
    x, y, c = lax.axis_index("x"), lax.axis_index("y"), lax.axis_index("c")
    return (1 - x if rel[0] else x, 1 - y if rel[1] else y, 1 - c if rel[2] else c)


def _dev_index(pos):
    return 4 * pos[0] + 2 * pos[1] + pos[2]


def _peers(chips):
    if chips:
        return [r for r in RELATIONS if not r[2]], N_CHIP, lambda pos: 2 * pos[0] + pos[1]
    return list(RELATIONS), N_DEV, _dev_index


def _exchange_start(srcs, by_slot, after, name, chips=False):
    n = len(srcs)
    extra = [] if after is None else [after]
    rels, slots, slot_of = _peers(chips)
    lands = [lax.empty((slots,) + (s.shape[1:] if by_slot else s.shape), s.dtype) for s in srcs]
    nsem = len(rels) * n

    def body(*refs):
        src_refs, land_refs = refs[:n], refs[n : 2 * n]
        send_sems, recv_sems = refs[2 * n + len(extra)], refs[2 * n + len(extra) + 1]
        token = refs[-1]
        me = slot_of(_related((0, 0, 0)))
        for a in range(n):
            for k, rel in enumerate(rels):
                peer = _related(rel)
                pltpu.make_async_remote_copy(
                    src_ref=src_refs[a].at[slot_of(peer)] if by_slot else src_refs[a],
                    dst_ref=land_refs[a].at[me],
                    send_sem=send_sems.at[len(rels) * a + k],
                    recv_sem=recv_sems.at[len(rels) * a + k],
                    device_id=peer,
                    device_id_type=MESH,
                ).start()
        token[...] = jnp.zeros_like(token)

    def hbm(a):
        return pltpu.HBM(a.shape, a.dtype)

    outs = pl.pallas_call(
        body,
        name=name,
        out_shape=(
            pltpu.SemaphoreType.DMA((nsem,)),
            pltpu.SemaphoreType.DMA((nsem,)),
            *[hbm(s) for s in srcs],
            *[hbm(l) for l in lands],
            jax.ShapeDtypeStruct((8, 128), F32),
        ),
        in_specs=[HBM] * (2 * n) + [ANY] * len(extra),
        out_specs=(SEM, SEM, *[HBM] * (2 * n), pl.BlockSpec(memory_space=pltpu.VMEM)),
        input_output_aliases={i: 2 + i for i in range(2 * n)},
        compiler_params=pltpu.CompilerParams(has_side_effects=EFFECT),
    )(*[pltpu.with_memory_space_constraint(a, pltpu.HBM) for a in list(srcs) + lands], *extra)
    return (outs[0], outs[1], list(outs[2 : 2 + n]), list(outs[2 + n : 2 + 2 * n])), outs[-1]


def _exchange_wait(handle, by_slot, after, name, chips=False):
    send_sems, recv_sems, srcs, lands = handle
    n = len(srcs)
    rels = _peers(chips)[0]

    def body(*refs):
        src_refs, land_refs = refs[:n], refs[n : 2 * n]
        s_sems, r_sems = refs[2 * n], refs[2 * n + 1]
        for a in range(n):
            for k, rel in enumerate(rels):
                copy = pltpu.make_async_remote_copy(
                    src_ref=src_refs[a].at[0] if by_slot else src_refs[a],
                    dst_ref=land_refs[a].at[0],
                    send_sem=s_sems.at[len(rels) * a + k],
                    recv_sem=r_sems.at[len(rels) * a + k],
                    device_id=_related(rel),
                    device_id_type=MESH,
                )
                copy.wait_send()
                copy.wait_recv()

    outs = pl.pallas_call(
        body,
        name=name,
        out_shape=tuple(pltpu.HBM(a.shape, a.dtype) for a in srcs + lands),
        in_specs=[HBM] * (2 * n) + [SEM, SEM, ANY],
        out_specs=tuple([HBM] * (2 * n)),
        input_output_aliases={i: i for i in range(2 * n)},
        compiler_params=pltpu.CompilerParams(has_side_effects=EFFECT),
    )(*srcs, *lands, send_sems, recv_sems, after)
    return list(outs[:n]), list(outs[n:])


def _pair_start(g, name):
    land = lax.empty((N_CHIP,) + g.shape[1:], g.dtype)

    def body(g_ref, land_ref, send_sems, recv_sems, g_thru, land_thru, token):
        c = lax.axis_index("c")
        for k in range(N_CHIP):
            pltpu.make_async_remote_copy(
                src_ref=g_ref.at[2 * k + (1 - c)],
                dst_ref=land_ref.at[k],
                send_sem=send_sems.at[k],
                recv_sem=recv_sems.at[k],
                device_id=_related((0, 0, 1)),
                device_id_type=MESH,
            ).start()
        token[...] = jnp.zeros_like(token)

    outs = pl.pallas_call(
        body,
        name=name,
        out_shape=(
            pltpu.SemaphoreType.DMA((N_CHIP,)),
            pltpu.SemaphoreType.DMA((N_CHIP,)),
            pltpu.HBM(g.shape, g.dtype),
            pltpu.HBM(land.shape, land.dtype),
            jax.ShapeDtypeStruct((8, 128), F32),
        ),
        in_specs=[HBM, HBM],
        out_specs=(SEM, SEM, HBM, HBM, pl.BlockSpec(memory_space=pltpu.VMEM)),
        input_output_aliases={0: 2, 1: 3},
        compiler_params=pltpu.CompilerParams(has_side_effects=EFFECT),
    )(pltpu.with_memory_space_constraint(g, pltpu.HBM), pltpu.with_memory_space_constraint(land, pltpu.HBM))
    return outs[:4], outs[4]


def _pair_wait(handle, after, name):
    send_sems, recv_sems, g, land = handle

    def body(g_ref, land_ref, s_sems, r_sems, _, g_out, land_out):
        for k in range(N_CHIP):
            copy = pltpu.make_async_remote_copy(
                src_ref=g_ref.at[0],
                dst_ref=land_ref.at[0],
                send_sem=s_sems.at[k],
                recv_sem=r_sems.at[k],
                device_id=_related((0, 0, 1)),
                device_id_type=MESH,
            )
            copy.wait_send()
            copy.wait_recv()

    return pl.pallas_call(
        body,
        name=name,
        out_shape=(pltpu.HBM(g.shape, g.dtype), pltpu.HBM(land.shape, land.dtype)),
        in_specs=[HBM, HBM, SEM, SEM, ANY],
        out_specs=(HBM, HBM),
        input_output_aliases={0: 0, 1: 1},
        compiler_params=pltpu.CompilerParams(has_side_effects=EFFECT),
    )(g, land, send_sems, recv_sems, after)


NEAR = ((0, 0, 1), (1, 0, 0), (0, 1, 0), (1, 1, 0))


def _gather2_start(blocks, name):
    n = len(blocks)
    lands = [lax.empty((N_DEV,) + b.shape, b.dtype) for b in blocks]

    def body(*refs):
        src_refs, land_refs = refs[:n], refs[n : 2 * n]
        send_sems, recv_sems, token = refs[2 * n], refs[2 * n + 1], refs[-1]
        me = _dev_index(_related((0, 0, 0)))
        for a in range(n):
            for k, rel in enumerate(NEAR):
                pltpu.make_async_remote_copy(
                    src_ref=src_refs[a],
                    dst_ref=land_refs[a].at[me],
                    send_sem=send_sems.at[len(NEAR) * a + k],
                    recv_sem=recv_sems.at[len(NEAR) * a + k],
                    device_id=_related(rel),
                    device_id_type=MESH,
                ).start()
        token[...] = jnp.zeros_like(token)

    nsem = len(NEAR) * n
    outs = pl.pallas_call(
        body,
        name=name,
        out_shape=(
            pltpu.SemaphoreType.DMA((nsem,)),
            pltpu.SemaphoreType.DMA((nsem,)),
            *[pltpu.HBM(a.shape, a.dtype) for a in list(blocks) + lands],
            jax.ShapeDtypeStruct((8, 128), F32),
        ),
        in_specs=[HBM] * (2 * n),
        out_specs=(SEM, SEM, *[HBM] * (2 * n), pl.BlockSpec(memory_space=pltpu.VMEM)),
        input_output_aliases={i: 2 + i for i in range(2 * n)},
        compiler_params=pltpu.CompilerParams(has_side_effects=EFFECT),
    )(*[pltpu.with_memory_space_constraint(a, pltpu.HBM) for a in list(blocks) + lands])
    return (outs[0], outs[1], list(outs[2 : 2 + n]), list(outs[2 + n : 2 + 2 * n])), outs[-1]


def _gather2_forward(handle, after, name):
    send1, recv1, srcs, lands = handle
    n = len(srcs)

    def body(*refs):
        src_refs, land_refs = refs[:n], refs[n : 2 * n]
        s1, r1 = refs[2 * n], refs[2 * n + 1]
        s2, r2 = refs[-2], refs[-1]
        sibling = _related(NEAR[0])
        for a in range(n):
            for k, rel in enumerate(NEAR):
                first = pltpu.make_async_remote_copy(
                    src_ref=src_refs[a],
                    dst_ref=land_refs[a].at[0],
                    send_sem=s1.at[len(NEAR) * a + k],
                    recv_sem=r1.at[len(NEAR) * a + k],
                    device_id=_related(rel),
                    device_id_type=MESH,
                )
                first.wait_send()
                first.wait_recv()
                if k:
                    slot = land_refs[a].at[_dev_index(_related(rel))]
                    pltpu.make_async_remote_copy(
                        src_ref=slot,
                        dst_ref=slot,
                        send_sem=s2.at[3 * a + k - 1],
                        recv_sem=r2.at[3 * a + k - 1],
                        device_id=sibling,
                        device_id_type=MESH,
                    ).start()

    outs = pl.pallas_call(
        body,
        name=name,
        out_shape=(
            *[pltpu.HBM(a.shape, a.dtype) for a in srcs + lands],
            pltpu.SemaphoreType.DMA((3 * n,)),
            pltpu.SemaphoreType.DMA((3 * n,)),
        ),
        in_specs=[HBM] * (2 * n) + [SEM, SEM, ANY],
        out_specs=(*[HBM] * (2 * n), SEM, SEM),
        input_output_aliases={i: i for i in range(2 * n)},
        compiler_params=pltpu.CompilerParams(has_side_effects=EFFECT),
    )(*srcs, *lands, send1, recv1, after)
    return outs[-2], outs[-1], list(outs[:n]), list(outs[n : 2 * n])


def _gather2_wait(handle, name):
    send2, recv2, srcs, lands = handle
    n = len(srcs)

    def body(*refs):
        land_refs = refs[n : 2 * n]
        s2, r2 = refs[2 * n], refs[2 * n + 1]
        for a in range(n):
            for j in range(3):
                passed = pltpu.make_async_remote_copy(
                    src_ref=land_refs[a].at[0],
                    dst_ref=land_refs[a].at[0],
                    send_sem=s2.at[3 * a + j],
                    recv_sem=r2.at[3 * a + j],
                    device_id=_related(NEAR[0]),
                    device_id_type=MESH,
                )
                passed.wait_send()
                passed.wait_recv()

    outs = pl.pallas_call(
        body,
        name=name,
        out_shape=tuple(pltpu.HBM(a.shape, a.dtype) for a in srcs + lands),
        in_specs=[HBM] * (2 * n) + [SEM, SEM],
        out_specs=tuple([HBM] * (2 * n)),
        input_output_aliases={i: i for i in range(2 * n)},
        compiler_params=pltpu.CompilerParams(has_side_effects=EFFECT),
    )(*srcs, *lands, send2, recv2)
    return list(outs[:n]), list(outs[n:])


def _slot_sum(parts, name):
    n, R, C = parts.shape
    tr = _row_tile(R) if R % 16 == 0 else R

    def body(p_ref, o_ref):
        acc = p_ref[0].astype(F32)
        for k in range(1, n):
            acc = acc + p_ref[k].astype(F32)
        o_ref[...] = acc

    return pl.pallas_call(
        body,
        name=name,
        grid=(R // tr,),
        in_specs=[pl.BlockSpec((n, tr, C), lambda i: (0, i, 0))],
        out_specs=pl.BlockSpec((tr, C), lambda i: (i, 0)),
        out_shape=jax.ShapeDtypeStruct((R, C), F32),
        compiler_params=pltpu.CompilerParams(dimension_semantics=("parallel",)),
    )(parts)


W_IN_TC = 256
W_IN_BOUNDS = (0, LAT, LAT + 3 * DIL_QKV, LAT + 3 * DIL_QKV + D_MODEL, D_IN)


def _dqkv_chunks():
    return [((g * 3 + part) * DIL_OUT, LAT + part * DIL_QKV + g * DIL_OUT) for g in range(DIL_GROUPS) for part in range(3)]


def _w_in_regroup(slots, after, name):
    tc = W_IN_TC

    def body(s_ref, _, lat_ref, dqkv_ref, g_ref, buf):
        for j in range(N_DEV):
            buf[j * IN_ROWS : (j + 1) * IN_ROWS, :] = s_ref[j].astype(F32)[:IN_ROWS, :]
        lat_ref[:LAT, :] = buf[:LAT, :].astype(BF16)
        lat_ref[LAT:, :] = jnp.zeros((LAT_PAD - LAT, tc), BF16)
        for dst, src in _dqkv_chunks():
            dqkv_ref[dst : dst + DIL_OUT, :] = buf[src : src + DIL_OUT, :].astype(BF16)
        g_ref[...] = buf[W_IN_BOUNDS[2] :, :].astype(BF16)

    def col(rows):
        return pl.BlockSpec((rows, tc), lambda k: (0, k))

    return pl.pallas_call(
        body,
        name=name,
        grid=(D_MODEL // tc,),
        in_specs=[pl.BlockSpec((N_DEV, IN_ROWS_PAD, tc), lambda k: (0, 0, k)), pl.BlockSpec((8, 128), lambda k: (0, 0))],
        out_specs=[col(LAT_PAD), col(3 * DIL_QKV), col(2 * D_MODEL)],
        out_shape=[
            jax.ShapeDtypeStruct((LAT_PAD, D_MODEL), BF16),
            jax.ShapeDtypeStruct((3 * DIL_QKV, D_MODEL), BF16),
            jax.ShapeDtypeStruct((2 * D_MODEL, D_MODEL), BF16),
        ],
        scratch_shapes=[pltpu.VMEM((D_IN, tc), F32)],
        compiler_params=pltpu.CompilerParams(dimension_semantics=("parallel",)),
    )(slots, after)


def _w_in_grad_regroup(g_lat, g_dqkv, g_ga, g_gb, name):
    tc = W_IN_TC

    def body(lat_ref, dqkv_ref, ga_ref, gb_ref, o_ref, buf):
        b = W_IN_BOUNDS
        buf[b[0] : b[1], :] = lat_ref[:LAT, :].astype(F32)
        for dst, src in _dqkv_chunks():
            buf[src : src + DIL_OUT, :] = dqkv_ref[dst : dst + DIL_OUT, :].astype(F32)
        buf[b[2] : b[3], :] = ga_ref[...].astype(F32)
        buf[b[3] : b[4], :] = gb_ref[...].astype(F32)
        fill = jnp.zeros((IN_ROWS_PAD - IN_ROWS, tc), F32)
        for j in range(N_DEV):
            o_ref[j] = jnp.concatenate([buf[j * IN_ROWS : (j + 1) * IN_ROWS, :], fill], axis=0).astype(BF16)

    def col(rows):
        return pl.BlockSpec((rows, tc), lambda k: (0, k))

    return pl.pallas_call(
        body,
        name=name,
        grid=(D_MODEL // tc,),
        in_specs=[col(LAT_PAD), col(3 * DIL_QKV), col(D_MODEL), col(D_MODEL)],
        out_specs=pl.BlockSpec((N_DEV, IN_ROWS_PAD, tc), lambda k: (0, 0, k)),
        out_shape=jax.ShapeDtypeStruct((N_DEV, IN_ROWS_PAD, D_MODEL), BF16),
        scratch_shapes=[pltpu.VMEM((D_IN, tc), F32)],
        compiler_params=pltpu.CompilerParams(dimension_semantics=("parallel",)),
    )(g_lat, g_dqkv, g_ga, g_gb)


def _ffn_pad(a, axis):
    a = jnp.moveaxis(a, axis, -1)
    g = a.reshape(a.shape[:-1] + (2 * N_DEV, FF_GROUP))
    g = jnp.pad(g, [(0, 0)] * (g.ndim - 1) + [(0, FF_GROUP_PAD - FF_GROUP)])
    return jnp.moveaxis(g.reshape(a.shape[:-1] + (2 * D_FF_PAD,)), -1, axis)


def _ffn_unpad(a, axis):
    a = jnp.moveaxis(a, axis, -1)
    g = a.reshape(a.shape[:-1] + (2 * N_DEV, FF_GROUP_PAD))[..., :FF_GROUP]
    return jnp.moveaxis(g.reshape(a.shape[:-1] + (2 * D_FF,)), -1, axis)


MISC = (("w_o_mla", (256, 1024)), ("w_o_dil", (256, 512)), ("w_uq", (192, 512)), ("w_ukv", (256, 256)))
BIG_WEIGHTS = ("w_in", "w_up", "w_down", "w_out") + tuple(n for n, _ in MISC)


def _exchange_blocks(w):
    def t(a):
        return a.astype(BF16).T

    up = t(w["w_up"]).reshape(2, FF_GROUP, D_MODEL)
    return [
        jnp.pad(t(w["w_in"]), ((0, IN_ROWS_PAD - IN_ROWS), (0, 0))),
        jnp.pad(up, ((0, 0), (0, FF_GROUP_PAD - FF_GROUP), (0, 0))).reshape(2 * FF_GROUP_PAD, D_MODEL),
        jnp.pad(w["w_down"].astype(BF16), ((0, FF_GROUP_PAD - FF_GROUP), (0, 0))),
        w["w_out"].astype(BF16),
        jnp.concatenate([t(w[n]).reshape(-1, D_MODEL) for n, _ in MISC], axis=0),
    ]


def _misc_split(misc):
    out, off = {}, 0
    for n, (r, c) in MISC:
        rows = r * c // D_MODEL
        out[n] = misc[..., off : off + rows, :].reshape(misc.shape[:-2] + (r, c))
        off += rows
    return out


def _small_matrices(g_misc):
    misc = _misc_split(g_misc)
    uq_t = jnp.pad(misc["w_uq"], ((0, 0), (0, HEAD_PAD - QK_NOPE - QK_ROPE), (0, 0)))
    return {
        "uq_t": uq_t.reshape(MLA_HEADS * HEAD_PAD, Q_LORA),
        "ukv_t": misc["w_ukv"].reshape(MLA_HEADS * HEAD_PAD, KV_LORA),
        "o_mla_t": misc["w_o_mla"].reshape(D_MODEL, MLA_HEADS * V_HEAD),
        "o_dil_t": misc["w_o_dil"].reshape(D_MODEL, DIL_OUT),
    }


def _small_grad_blocks(g):
    uq_t = g["uq_t"].reshape(MLA_HEADS, HEAD_PAD, Q_LORA)[:, : QK_NOPE + QK_ROPE]
    misc = {"w_o_mla": g["o_mla_t"], "w_o_dil": g["o_dil_t"], "w_uq": uq_t, "w_ukv": g["ukv_t"]}
    return [
        g["w_out"].reshape(N_DEV, -1, D_MODEL),
        jnp.concatenate([misc[n].reshape(N_DEV, -1, D_MODEL) for n, _ in MISC], axis=1),
    ]


def _grad_shards(sums):
    s_in, s_out, s_misc, s_up, s_down = sums
    out = {
        "w_in": s_in[:IN_ROWS].T,
        "w_up": s_up.reshape(2, FF_GROUP_PAD, D_MODEL)[:, :FF_GROUP].reshape(2 * FF_GROUP, D_MODEL).T,
        "w_down": s_down[:FF_GROUP],
        "w_out": s_out,
    }
    out.update({n: v.T for n, v in _misc_split(s_misc).items()})
    return out


def _local_step(x, h, tgt, wt, conv_w, small, small_matrices, ffn_weight, send_ffn_grads, send_small_grads, send_w_in_grads, forward_w_in_grads):
    S = x.shape[0]
    lat_t, dqkv_t, g_t = wt
    cw = jnp.pad(_ffn_pad(conv_w, 1), ((0, 5), (0, 0)))
    cb = _ffn_pad(small["conv_b"], 1)
    cos_t, sin_t = _rope_tables(S)
    bias = _dil_bias()
    g1, g2, g3 = small["attn_norm_g"], small["ffn_norm_g"], small["final_norm_g"]
    gq, gkv = small["q_norm_g"], small["kv_norm_g"]

    lat = _mm(h, lat_t, "nt", F32, 1024, LAT_PAD, D_MODEL, "proj_lat")
    dqkv = _mm(h, dqkv_t, "nt", F32, 1024, 1536, D_MODEL, "proj_dqkv")
    gates = _mm(h, g_t, "nt", F32, 1024, 1024, D_MODEL, "proj_gates", bias=small["b_gate"], act="sigmoid")
    sm = small_matrices(gates)
    uq_t, ukv_t, o_mla_t, o_dil_t = sm["uq_t"], sm["ukv_t"], sm["o_mla_t"], sm["o_dil_t"]
    cqn, ckvn, kpe = _mla_prep1(lat, gq, gkv, cos_t, sin_t, "mla_prep1")
    q_raw = _mm(cqn, uq_t, "nt", F32, 1024, 1024, Q_LORA, "mla_uq")
    kv = _mm(ckvn, ukv_t, "nt", BF16, 1024, 1024, KV_LORA, "mla_ukv")
    q_att, k_att = _mla_prep2(q_raw, kv, kpe, cos_t, sin_t, "mla_prep2")
    o, lse = _flash2_fwd(q_att, k_att, kv, "mla_flash_fwd")
    o_a = _mm(o, o_mla_t, "nt", F32, 1024, 1024, MLA_HEADS * V_HEAD, "mla_out")

    d_os, d_ls = [], []
    for g, (_, dil) in enumerate(DIL_PATTERNS):
        og, lg = _dil_fwd_group(dqkv, bias[g], g, dil, f"dil_fwd_{g}")
        d_os.append(og)
        d_ls.append(lg)
    od, dil_lse = _dil_combine(d_os, d_ls, "dil_combine")
    o_b = _mm(od, o_dil_t, "nt", F32, 1024, 1024, DIL_OUT, "dil_out")

    mrg = _merge_fwd(gates, o_a, o_b, "merge_fwd")
    w_out = ffn_weight("w_out", mrg)
    x1, h2 = _mm_res_rms(mrg, w_out, x, g2, "mix_out")
    up_t = ffn_weight("up_t", h2)
    u0 = _mm(h2, up_t, "nt", F32, 1024, 1024, D_MODEL, "ffn_up")
    a = _ffn_fwd(u0, cw, cb, "ffn_conv_fwd")
    w_down = ffn_weight("w_down", a)
    x2 = _mm(a, w_down, "nn", F32, 1024, 1024, D_FF_PAD // 2, "ffn_down", res=x1)
    loss_part, dx2, dx2b, dg3 = _final_loss(x2, g3, tgt, "final_loss")

    da = _mm(dx2b, w_down, "nt", F32, 1024, D_FF_PAD // 4, D_MODEL, "ffn_down_dx")
    gw_down = _mm(a, dx2b, "tn", BF16, 512, D_MODEL, S, "ffn_down_dw")
    du0, dcw, dcb = _ffn_bwd(u0, da, cw, cb, "ffn_conv_bwd")
    du0 = du0.reshape(2 * S, D_FF_PAD)
    gw_up_t = _mm(du0, h2, "tn", BF16, 512, D_MODEL, S, "ffn_up_dw", a_halves=2)
    sent = send_ffn_grads(gw_up_t, gw_down)
    dh2 = _mm(du0, up_t, "nn", F32, 1024, 1024, D_FF_PAD // 2, "ffn_up_dx", a_halves=2)
    dx1, dx1b, dg2 = _rms_bwd(dh2, x1, g2 + sent, dx2, "rms_ffn_bwd")

    dmrg = _mm(dx1b, w_out, "nt", F32, 1024, 1024, D_MODEL, "mix_out_dx")
    gw_out = _mm(mrg, dx1b, "tn", BF16, 512, D_MODEL, S, "mix_out_dw")
    do_a, do_b, dga, dgb, dba, dbb = _merge_bwd(dmrg, gates, o_a, o_b, "merge_bwd")

    do = _mm(do_a, o_mla_t, "nn", BF16, 1024, 1024, D_MODEL, "mla_out_dx")
    gw_o_mla_t = _mm(do_a, o, "tn", BF16, 1024, 1024, 1024, "mla_out_dw")
    dod = _mm(do_b, o_dil_t, "nn", F32, 1024, DIL_OUT, D_MODEL, "dil_out_dx")
    gw_o_dil_t = _mm(do_b, od, "tn", BF16, 1024, DIL_OUT, 1024, "dil_out_dw")

    delta = _flash_delta(do, o, "mla_flash_delta")
    lse_row = lse[:, :, 0][:, None, :]
    delta_row = delta[:, :MLA_HEADS].T[:, None, :]
    dq_att, dk_att, dv = _flash2_bwd(q_att, k_att, kv, do, lse_row, delta_row, "mla_flash_bwd")
    dq_raw, dkv, dkpe = _mla_post(dq_att, dk_att, dv, cos_t, sin_t, "mla_post")
    dcqn = _mm(dq_raw, uq_t, "nn", F32, 1024, Q_LORA, MLA_HEADS * HEAD_PAD, "mla_uq_dx")
    gw_uq_t = _mm(dq_raw, cqn, "tn", BF16, 1024, Q_LORA, 1024, "mla_uq_dw")
    dckvn = _mm(dkv, ukv_t, "nn", F32, 1024, KV_LORA, MLA_HEADS * HEAD_PAD, "mla_ukv_dx")
    gw_ukv_t = _mm(dkv, ckvn, "tn", BF16, 1024, KV_LORA, 1024, "mla_ukv_dw")
    sent = send_small_grads({"uq_t": gw_uq_t, "ukv_t": gw_ukv_t, "o_mla_t": gw_o_mla_t, "o_dil_t": gw_o_dil_t, "w_out": gw_out})
    dlat, dgq, dgkv = _lat_bwd(dcqn, dckvn, dkpe, lat, gq + sent, gkv, "lat_bwd")

    dd = _dil_rowdot(dod, od, "dil_rowdot")
    ddqkv = lax.empty((3 * DIL_GROUPS, S, DIL_OUT), F32)
    for g, (_, dil) in enumerate(DIL_PATTERNS):
        ddqkv = _dil_bwd_group(dqkv, bias[g], dod, dd, dil_lse, ddqkv, g, dil, f"dil_bwd_{g}")
    gw_lat_t = _mm(dlat, h, "tn", BF16, LAT_PAD, 1024, S, "proj_lat_dw")
    gw_dqkv_t = _mm(ddqkv.reshape(3 * DIL_GROUPS * S, DIL_OUT), h, "tn", BF16, 512, 1024, S, "proj_dqkv_dw", a_halves=3 * DIL_GROUPS)
    gw_ga_t = _mm(dga, h, "tn", BF16, 512, D_MODEL, S, "proj_ga_dw")
    gw_gb_t = _mm(dgb, h, "tn", BF16, 512, D_MODEL, S, "proj_gb_dw")
    sent = send_w_in_grads(gw_lat_t, gw_dqkv_t, gw_ga_t, gw_gb_t)
    dh = _mm(dlat + sent.astype(BF16), lat_t, "nn", F32, 1024, 1024, LAT_PAD, "proj_lat_dx")
    dh = _stacked_mm(ddqkv, dqkv_t, dh, forward_w_in_grads(dh), "proj_dqkv_dx")
    dh = _mm(dga, g_t, "nn", F32, 1024, 1024, D_MODEL, "proj_ga_dx", res=dh)
    grad_x, dg1 = _mm_rms_bwd(dgb, g_t, 1, dh, x, g1, dx1, "proj_gb_dx_rms_attn_bwd")

    small_grads = {
        "attn_norm_g": dg1,
        "b_gate": jnp.concatenate([dba, dbb], axis=1),
        "q_norm_g": dgq,
        "kv_norm_g": dgkv,
        "ffn_norm_g": dg2,
        "conv_b": _ffn_unpad(jnp.concatenate([dcb[0], dcb[1]], axis=1), 1),
        "final_norm_g": dg3,
        "conv_w": _ffn_unpad(jnp.concatenate([dcw[0, :3], dcw[1, :3]], axis=1), 1),
    }
    return loss_part, grad_x, small_grads


SMALL_ORDER = ("attn_norm_g", "b_gate", "q_norm_g", "kv_norm_g", "ffn_norm_g", "conv_b", "final_norm_g", "conv_w")
WEIGHT_ORDER = (
    "attn_norm_g", "w_in", "b_gate", "q_norm_g", "w_uq", "kv_norm_g", "w_ukv", "w_o_mla", "w_o_dil", "w_out",
    "ffn_norm_g", "w_up", "conv_w", "conv_b", "w_down", "final_norm_g",
)


def kernel(x, attn_norm_g, w_in, b_gate, q_norm_g, w_uq, kv_norm_g, w_ukv, w_o_mla, w_o_dil, w_out, ffn_norm_g, w_up, conv_w, conv_b, w_down, final_norm_g, loss_target, m_attn_norm_g, m_w_in, m_b_gate, m_q_norm_g, m_w_uq, m_kv_norm_g, m_w_ukv, m_w_o_mla, m_w_o_dil, m_w_out, m_ffn_norm_g, m_w_up, m_conv_w, m_conv_b, m_w_down, m_final_norm_g, v_attn_norm_g, v_w_in, v_b_gate, v_q_norm_g, v_w_uq, v_kv_norm_g, v_w_ukv, v_w_o_mla, v_w_o_dil, v_w_out, v_ffn_norm_g, v_w_up, v_conv_w, v_conv_b, v_w_down, v_final_norm_g):
    env = dict(locals())
    dev = 4 * lax.axis_index("x") + 2 * lax.axis_index("y") + lax.axis_index("c")
    core = lax.axis_index("c").astype(jnp.int32).reshape(1)

    def two_d(a):
        return a.reshape(-1, a.shape[-1])

    w = {n: two_d(env[n]) for n in WEIGHT_ORDER}
    m = {n: two_d(env["m_" + n]) for n in WEIGHT_ORDER}
    v = {n: two_d(env["v_" + n]) for n in WEIGHT_ORDER}

    chip = 2 * lax.axis_index("x") + lax.axis_index("y")

    def own_slot_in(lands, own, slot=dev):
        return [lax.dynamic_update_slice(l, o[None], (slot, 0, 0)) for l, o in zip(lands, own)]

    b_in = _exchange_blocks(w)[0]
    r, c = CONV_SHARD
    conv = jnp.pad(w["conv_w"].reshape(-1), (0, 8 * SMALL_COLS - r * c)).reshape(8, SMALL_COLS)
    first_level, token = _gather2_start([b_in, conv], "ag_w_in_start")
    tied = {n: w[n] + token[0, 0] for n in BIG_WEIGHTS}
    _, b_up, b_down, b_out, b_misc = _exchange_blocks(tied)
    h = _rms_fwd(x[0], w["attn_norm_g"] + token[0, 0], "rms_attn")
    prepared = b_up[:1, :1] + b_down[:1, :1] + b_out[:1, :1] + b_misc[:1, :1] + h[:1, :1]
    own, lands = _gather2_wait(_gather2_forward(first_level, prepared, "ag_w_in_forward"), "ag_w_in_wait")
    g_in, conv = own_slot_in(lands, own)
    misc_gather, started = _exchange_start([b_misc], False, conv, "ag_small_start")
    ffn_gathers, started2 = {}, started
    for key, block in (("w_out", b_out), ("up_t", b_up), ("w_down", b_down)):
        ffn_gathers[key], started2 = _exchange_start([block], False, started2, f"ag_{key}_start")
    wt = _w_in_regroup(g_in, started2, "w_in_regroup")
    conv = conv.reshape(N_DEV, 8 * SMALL_COLS)[:, : r * c].reshape(N_DEV, r, c)
    conv_w_full = conv.transpose(1, 0, 2).reshape(r, N_DEV * c)
    small = {n: w[n] for n in SMALL_ORDER if n != "conv_w"}

    def small_matrices(after):
        own, lands = _exchange_wait(misc_gather, False, after, "ag_small_wait")
        return _small_matrices(own_slot_in(lands, own)[0])

    def ffn_weight(key, after):
        own, lands = _exchange_wait(ffn_gathers[key], False, after, f"ag_{key}_wait")
        return own_slot_in(lands, own)[0].reshape(-1, D_MODEL)

    reduces = {}

    def send_ffn_grads(gw_up_t, gw_down):
        blocks = [gw_up_t.reshape(N_DEV, 2 * FF_GROUP_PAD, D_MODEL), gw_down.reshape(N_DEV, FF_GROUP_PAD, D_MODEL)]
        reduces["ffn"], token = _exchange_start(blocks, True, None, "rs_ffn_start")
        return token[0, 0]

    def send_small_grads(g):
        reduces["small"], token = _exchange_start(_small_grad_blocks(g), True, None, "rs_small_start")
        return token[0, 0]

    def send_w_in_grads(g_lat, g_dqkv, g_ga, g_gb):
        e_in = _w_in_grad_regroup(g_lat, g_dqkv, g_ga, g_gb, "w_in_grad_regroup")
        reduces["pair"], token = _pair_start(e_in, "rs_w_in_pair_start")
        return token[0, 0]

    def forward_w_in_grads(after):
        e_in, recv = _pair_wait(reduces.pop("pair"), after, "rs_w_in_pair_wait")
        pair = _pair_add(e_in, recv, core, "rs_w_in_pair_add")
        reduces["w_in"], token = _exchange_start([pair], True, None, "rs_w_in_start", chips=True)
        return token

    loss_part, grad_x, small_grads = _local_step(
        x[0], h, loss_target[0], wt, conv_w_full, small, small_matrices, ffn_weight,
        send_ffn_grads, send_small_grads, send_w_in_grads, forward_w_in_grads,
    )
    loss = lax.psum(loss_part[0, 0], AXES)
    sflat = jnp.concatenate([small_grads[n].reshape(-1) for n in SMALL_ORDER])
    sflat = jnp.pad(sflat, (0, SMALL_ROWS * SMALL_COLS - sflat.shape[0])).reshape(SMALL_ROWS, SMALL_COLS)
    vec_gather, _ = _exchange_start([sflat], False, None, "rs_vec_start")

    def finish(key, by_chip, name):
        sent, lands = _exchange_wait(reduces[key], True, grad_x, name + "_wait", chips=by_chip)
        slot = chip if by_chip else dev
        own = [lax.dynamic_index_in_dim(s, slot, 0, keepdims=False) for s in sent]
        return [_slot_sum(p, f"{name}_sum_{i}") for i, p in enumerate(own_slot_in(lands, own, slot))]

    (s_in,) = finish("w_in", True, "rs_w_in")
    s_out, s_misc = finish("small", False, "rs_small")
    s_up, s_down = finish("ffn", False, "rs_ffn")
    gshard = _grad_shards([s_in, s_out, s_misc, s_up, s_down])

    updates = {n: _adamw(w[n], gshard[n], m[n], v[n], "adamw_" + n) for n in BIG_WEIGHTS}

    big_done = sum(updates[n][0][:1, :1] for n in BIG_WEIGHTS)
    own, lands = _exchange_wait(vec_gather, False, big_done, "rs_vec_wait")
    ssum = _slot_sum(own_slot_in(lands, own)[0], "small_sum").reshape(-1)
    gsmall, off = {}, 0
    for n in SMALL_ORDER:
        shape = (3, 2 * D_FF) if n == "conv_w" else w[n].shape
        size = shape[0] * shape[1]
        gsmall[n] = ssum[off : off + size].reshape(shape)
        off += size
    gsmall["conv_w"] = lax.dynamic_slice_in_dim(gsmall["conv_w"], dev * CONV_SHARD[1], CONV_SHARD[1], axis=1)
    updates.update({n: _adamw(w[n], gsmall[n], m[n], v[n], "adamw_" + n) for n in SMALL_ORDER})

    g_all = {**gshard, **gsmall}
    out_g, out_d, out_m, out_v = [], [], [], []
    for n in WEIGHT_ORDER:
        d, nm, nv = updates[n]
        shape = env[n].shape
        out_g.append(g_all[n].reshape(shape))
        out_d.append(d.reshape(shape))
        out_m.append(nm.reshape(shape))
        out_v.append(nv.reshape(shape))
    return (loss, grad_x[None], *out_g, *out_d, *out_m, *out_v)
```

```python
import functools

import jax
import jax.numpy as jnp
import numpy as np
from jax import lax
from jax.experimental import pallas as pl
from jax.experimental.pallas import tpu as pltpu

F32 = jnp.float32
BF16 = jnp.bfloat16

N_DEV = 8
N_CHIP = 4
AXES = ("x", "y", "c")
MESH = pl.DeviceIdType.MESH

D_MODEL = 2048
MLA_HEADS = 8
QK_NOPE = 128
QK_ROPE = 64
V_HEAD = 128
Q_LORA = 512
KV_LORA = 256
ROPE_THETA = 10000.0
HEAD_PAD = 256
DIL_PATTERNS = ((128, 1), (512, 4), (2048, 16))
DIL_GROUPS = 3
DIL_HG = 4
DIL_HEADS = 12
DIL_HD = 128
DIL_BLK = 128
DIL_QKV = DIL_HEADS * DIL_HD
DIL_OUT = DIL_HG * DIL_HD
ALIBI_MAX_BIAS = 8.0
D_FF = 5504
D_FF_PAD = 5632
NORM_EPS = 1e-6
LAT = Q_LORA + KV_LORA + QK_ROPE
LAT_PAD = 896
D_IN = LAT + 3 * DIL_QKV + 2 * D_MODEL
NEG = -1e30

ADAM_LR = 0.001
ADAM_B1 = 0.9
ADAM_B2 = 0.999
ADAM_EPS = 1e-08
ADAM_WD = 0.01
ADAM_STEP = 10

SMALL_ROWS = 56
SMALL_COLS = 1024

IN_ROWS = 1192
IN_ROWS_PAD = 1200
FF_GROUP = D_FF // N_DEV
FF_GROUP_PAD = D_FF_PAD // N_DEV
CONV_SHARD = (3, 1376)

NT = (((1,), (1,)), ((), ()))
TN = (((0,), (0,)), ((), ()))


def _dot(a, b, dims=(((1,), (0,)), ((), ()))):
    return lax.dot_general(a, b, dims, preferred_element_type=F32)


def _mm(a, b, mode, out_dtype, tm, tn, tk, name, bias=None, act=None, res=None, b_koff=0, a_halves=1):
    H = a_halves
    if mode == "nn":
        (M, K), (K2, N) = (a.shape[0] // H, a.shape[1] * H), b.shape
        assert (b_koff + 1) * K <= K2, (name, a.shape, b.shape)
        koff, K2 = b_koff * (K // tk), K
        kper, mrows = a.shape[1] // tk, M // tm
        a_spec = pl.BlockSpec((tm, tk), lambda i, j, k: (i + (k // kper) * mrows, k % kper))
        b_spec = pl.BlockSpec((tk, tn), lambda i, j, k: (k + koff, j))
        dims = (((1,), (0,)), ((), ()))
    elif mode == "nt":
        (M, K), (N, K2) = a.shape, b.shape
        a_spec = pl.BlockSpec((tm, tk), lambda i, j, k: (i, k))
        b_spec = pl.BlockSpec((tn, tk), lambda i, j, k: (j, k))
        dims = NT
    else:
        (K, M), (K2, N) = (a.shape[0] // H, a.shape[1] * H), b.shape
        mper, krows = a.shape[1] // tm, K // tk
        a_spec = pl.BlockSpec((tk, tm), lambda i, j, k: (k + (i // mper) * krows, i % mper))
        b_spec = pl.BlockSpec((tk, tn), lambda i, j, k: (k, j))
        dims = TN
    assert K == K2 and M % tm == 0 and N % tn == 0 and K % tk == 0, (name, a.shape, b.shape)
    nk = K // tk
    has_bias, has_res = bias is not None, res is not None

    def body(*refs):
        refs = list(refs)
        a_ref, b_ref = refs[0], refs[1]
        pos = 2
        bias_ref = res_ref = None
        if has_bias:
            bias_ref = refs[pos]
            pos += 1
        if has_res:
            res_ref = refs[pos]
            pos += 1
        o_ref = refs[pos]
        p = _dot(a_ref[...].astype(BF16), b_ref[...].astype(BF16), dims)

        def finish(acc):
            if has_bias:
                acc = acc + bias_ref[...]
            if act == "sigmoid":
                acc = jax.nn.sigmoid(acc)
            if has_res:
                acc = res_ref[...] + acc
            o_ref[...] = acc.astype(o_ref.dtype)

        if nk == 1:
            finish(p)
        else:
            acc_ref = refs[pos + 1]
            k = pl.program_id(2)

            @pl.when(k == 0)
            def _():
                acc_ref[...] = p

            @pl.when(k != 0)
            def _():
                acc_ref[...] += p

            @pl.when(k == nk - 1)
            def _():
                finish(acc_ref[...])

    in_specs = [a_spec, b_spec]
    args = [a, b]
    if has_bias:
        in_specs.append(pl.BlockSpec((1, tn), lambda i, j, k: (0, j)))
        args.append(bias)
    if has_res:
        in_specs.append(pl.BlockSpec((tm, tn), lambda i, j, k: (i, j)))
        args.append(res)
    return pl.pallas_call(
        body,
        name=name,
        grid=(M // tm, N // tn, nk),
        in_specs=in_specs,
        out_specs=pl.BlockSpec((tm, tn), lambda i, j, k: (i, j)),
        out_shape=jax.ShapeDtypeStruct((M, N), out_dtype),
        scratch_shapes=[pltpu.VMEM((tm, tn), F32)] if nk > 1 else [],
        compiler_params=pltpu.CompilerParams(dimension_semantics=("parallel", "parallel", "arbitrary")),
    )(*args)


def _stacked_mm(pieces, w_t, res, after, name, tm=512, tn=1024):
    P, M, W = pieces.shape
    N = w_t.shape[1]

    def body(a_ref, b_ref, r_ref, _, o_ref):
        acc = r_ref[...]
        for p in range(P):
            acc = acc + _dot(a_ref[p].astype(BF16), b_ref[p * W : (p + 1) * W, :])
        o_ref[...] = acc

    tile = pl.BlockSpec((tm, tn), lambda i, j: (i, j))
    return pl.pallas_call(
        body,
        name=name,
        grid=(M // tm, N // tn),
        in_specs=[
            pl.BlockSpec((P, tm, W), lambda i, j: (0, i, 0)),
            pl.BlockSpec((P * W, tn), lambda i, j: (0, j)),
            tile,
            pl.BlockSpec((8, 128), lambda i, j: (0, 0)),
        ],
        out_specs=tile,
        out_shape=jax.ShapeDtypeStruct((M, N), F32),
        compiler_params=pltpu.CompilerParams(dimension_semantics=("parallel", "parallel")),
    )(pieces, w_t, res, after)


def _rstd(x):
    return lax.rsqrt(jnp.mean(x * x, axis=-1, keepdims=True) + NORM_EPS)


def _rms_bwd_math(dy, x, g):
    r = _rstd(x)
    xh = x * r
    dg = jnp.sum(dy * xh, axis=0, keepdims=True)
    dxh = dy * g
    dx = r * (dxh - xh * jnp.mean(dxh * xh, axis=-1, keepdims=True))
    return dx, dg


def _rms_fwd(x, g, name, tr=256):
    S, D = x.shape

    def body(x_ref, g_ref, o_ref):
        xv = x_ref[...]
        o_ref[...] = ((xv * _rstd(xv)) * g_ref[...]).astype(o_ref.dtype)

    return pl.pallas_call(
        body,
        name=name,
        grid=(S // tr,),
        in_specs=[pl.BlockSpec((tr, D), lambda i: (i, 0)), pl.BlockSpec((1, D), lambda i: (0, 0))],
        out_specs=pl.BlockSpec((tr, D), lambda i: (i, 0)),
        out_shape=jax.ShapeDtypeStruct((S, D), BF16),
        compiler_params=pltpu.CompilerParams(dimension_semantics=("parallel",)),
    )(x, g)


def _rms_bwd(dy, x, g, res, name, tr=256):
    S, D = x.shape

    def body(dy_ref, x_ref, g_ref, res_ref, dx_ref, dxb_ref, dg_ref):
        dx, dg = _rms_bwd_math(dy_ref[...], x_ref[...], g_ref[...])
        dx = dx + res_ref[...]
        dx_ref[...] = dx
        dxb_ref[...] = dx.astype(BF16)

        @pl.when(pl.program_id(0) == 0)
        def _():
            dg_ref[...] = dg

        @pl.when(pl.program_id(0) != 0)
        def _():
            dg_ref[...] += dg

    row = pl.BlockSpec((tr, D), lambda i: (i, 0))
    vec = pl.BlockSpec((1, D), lambda i: (0, 0))
    return pl.pallas_call(
        body,
        name=name,
        grid=(S // tr,),
        in_specs=[row, row, vec, row],
        out_specs=[row, row, vec],
        out_shape=[jax.ShapeDtypeStruct((S, D), F32), jax.ShapeDtypeStruct((S, D), BF16), jax.ShapeDtypeStruct((1, D), F32)],
        compiler_params=pltpu.CompilerParams(dimension_semantics=("arbitrary",)),
    )(dy, x, g, res)


def _mm_res_rms(a, b, res, g, name, tm=256):
    M, K = a.shape
    D = b.shape[1]

    def body(a_ref, b_ref, res_ref, g_ref, y_ref, h_ref):
        y = res_ref[...] + _dot(a_ref[...], b_ref[...])
        y_ref[...] = y
        h_ref[...] = ((y * _rstd(y)) * g_ref[...]).astype(BF16)

    row = pl.BlockSpec((tm, D), lambda i: (i, 0))
    return pl.pallas_call(
        body,
        name=name,
        grid=(M // tm,),
        in_specs=[pl.BlockSpec((tm, K), lambda i: (i, 0)), pl.BlockSpec((K, D), lambda i: (0, 0)), row, pl.BlockSpec((1, D), lambda i: (0, 0))],
        out_specs=[row, row],
        out_shape=[jax.ShapeDtypeStruct((M, D), F32), jax.ShapeDtypeStruct((M, D), BF16)],
        compiler_params=pltpu.CompilerParams(dimension_semantics=("parallel",)),
    )(a, b, res, g)


def _mm_rms_bwd(a, b, b_koff, dy_part, x, g, res, name, tm=256):
    M, K = a.shape
    D = x.shape[1]

    def body(a_ref, b_ref, dyp_ref, x_ref, g_ref, res_ref, dx_ref, dg_ref):
        dy = dyp_ref[...] + _dot(a_ref[...], b_ref[...])
        dx, dg = _rms_bwd_math(dy, x_ref[...], g_ref[...])
        dx_ref[...] = dx + res_ref[...]

        @pl.when(pl.program_id(0) == 0)
        def _():
            dg_ref[...] = dg

        @pl.when(pl.program_id(0) != 0)
        def _():
            dg_ref[...] += dg

    row = pl.BlockSpec((tm, D), lambda i: (i, 0))
    vec = pl.BlockSpec((1, D), lambda i: (0, 0))
    return pl.pallas_call(
        body,
        name=name,
        grid=(M // tm,),
        in_specs=[pl.BlockSpec((tm, K), lambda i: (i, 0)), pl.BlockSpec((K, D), lambda i: (b_koff, 0)), row, row, vec, row],
        out_specs=[row, vec],
        out_shape=[jax.ShapeDtypeStruct((M, D), F32), jax.ShapeDtypeStruct((1, D), F32)],
        compiler_params=pltpu.CompilerParams(dimension_semantics=("arbitrary",)),
    )(a, b, dy_part, x, g, res)


def _final_loss(x2, g, tgt, name, tr=256):
    S, D = x2.shape

    def body(x_ref, g_ref, t_ref, loss_ref, dx_ref, dxb_ref, dg_ref):
        xv, gv = x_ref[...], g_ref[...]
        y = (xv * _rstd(xv)) * gv
        e = y - t_ref[...]
        part = 0.5 * jnp.sum(jnp.mean(e * e, axis=-1, keepdims=True), axis=0, keepdims=True)
        dx, dg = _rms_bwd_math(e * (1.0 / D), xv, gv)
        dx_ref[...] = dx
        dxb_ref[...] = dx.astype(BF16)
        part = jnp.broadcast_to(part, (1, 128))

        @pl.when(pl.program_id(0) == 0)
        def _():
            dg_ref[...] = dg
            loss_ref[...] = part

        @pl.when(pl.program_id(0) != 0)
        def _():
            dg_ref[...] += dg
            loss_ref[...] += part

    row = pl.BlockSpec((tr, D), lambda i: (i, 0))
    vec = pl.BlockSpec((1, D), lambda i: (0, 0))
    return pl.pallas_call(
        body,
        name=name,
        grid=(S // tr,),
        in_specs=[row, vec, row],
        out_specs=[pl.BlockSpec((1, 128), lambda i: (0, 0)), row, row, vec],
        out_shape=[
            jax.ShapeDtypeStruct((1, 128), F32),
            jax.ShapeDtypeStruct((S, D), F32),
            jax.ShapeDtypeStruct((S, D), BF16),
            jax.ShapeDtypeStruct((1, D), F32),
        ],
        compiler_params=pltpu.CompilerParams(dimension_semantics=("arbitrary",)),
    )(x2, g, tgt)


def _rope_tables(S):
    pos = jnp.arange(S, dtype=F32)
    inv_freq = ROPE_THETA ** (-jnp.arange(0, QK_ROPE, 2, dtype=F32) / QK_ROPE)
    ang = pos[:, None] * inv_freq[None, :]
    cos, sin = jnp.cos(ang), jnp.sin(ang)
    zero = jnp.zeros((S, 128 - QK_ROPE), F32)
    return jnp.concatenate([cos, cos, zero], axis=1), jnp.concatenate([-sin, sin, zero], axis=1)


def _rope_tile(x, cos_t, sin_t):
    lane = lax.broadcasted_iota(jnp.int32, x.shape, 1)
    partner = jnp.where(lane < QK_ROPE // 2, pltpu.roll(x, 128 - QK_ROPE // 2, 1), pltpu.roll(x, QK_ROPE // 2, 1))
    return x * cos_t + partner * sin_t


def _mla_prep1(lat, gq, gkv, cos_t, sin_t, name, tr=256):
    S = lat.shape[0]

    def body(lat_ref, gq_ref, gkv_ref, cos_ref, sin_ref, cq_ref, ckv_ref, kpe_ref):
        cq = lat_ref[:, :Q_LORA]
        ckv = lat_ref[:, Q_LORA : Q_LORA + KV_LORA]
        cq_ref[...] = ((cq * _rstd(cq)) * gq_ref[...]).astype(BF16)
        ckv_ref[...] = ((ckv * _rstd(ckv)) * gkv_ref[...]).astype(BF16)
        kpe_ref[...] = _rope_tile(lat_ref[:, Q_LORA + KV_LORA :], cos_ref[...], sin_ref[...]).astype(BF16)

    def row(n):
        return pl.BlockSpec((tr, n), lambda i: (i, 0))

    def vec(n):
        return pl.BlockSpec((1, n), lambda i: (0, 0))

    return pl.pallas_call(
        body,
        name=name,
        grid=(S // tr,),
        in_specs=[row(LAT_PAD), vec(Q_LORA), vec(KV_LORA), row(128), row(128)],
        out_specs=[row(Q_LORA), row(KV_LORA), row(128)],
        out_shape=[
            jax.ShapeDtypeStruct((S, Q_LORA), BF16),
            jax.ShapeDtypeStruct((S, KV_LORA), BF16),
            jax.ShapeDtypeStruct((S, 128), BF16),
        ],
        compiler_params=pltpu.CompilerParams(dimension_semantics=("parallel",)),
    )(lat, gq, gkv, cos_t, sin_t)


def _mla_prep2(q_raw, kv, kpe, cos_t, sin_t, name, tr=256):
    S = q_raw.shape[0]
    W = MLA_HEADS * HEAD_PAD

    def body(q_ref, kv_ref, kpe_ref, cos_ref, sin_ref, qa_ref, ka_ref):
        cos_v, sin_v, kpe_v = cos_ref[...], sin_ref[...], kpe_ref[...]
        for h in range(MLA_HEADS):
            lo = h * HEAD_PAD
            qa_ref[:, lo : lo + 128] = q_ref[:, lo : lo + 128].astype(BF16)
            qa_ref[:, lo + 128 : lo + 256] = _rope_tile(q_ref[:, lo + 128 : lo + 256], cos_v, sin_v).astype(BF16)
            ka_ref[:, lo : lo + 128] = kv_ref[:, lo : lo + 128]
            ka_ref[:, lo + 128 : lo + 256] = kpe_v

    def row(n):
        return pl.BlockSpec((tr, n), lambda i: (i, 0))

    return pl.pallas_call(
        body,
        name=name,
        grid=(S // tr,),
        in_specs=[row(W), row(W), row(128), row(128), row(128)],
        out_specs=[row(W), row(W)],
        out_shape=[jax.ShapeDtypeStruct((S, W), BF16), jax.ShapeDtypeStruct((S, W), BF16)],
        compiler_params=pltpu.CompilerParams(dimension_semantics=("parallel",)),
    )(q_raw, kv, kpe, cos_t, sin_t)


def _mla_post(dq_att, dk_att, dv, cos_t, sin_t, name, tr=256):
    S = dq_att.shape[0]
    W = MLA_HEADS * HEAD_PAD

    def body(dq_ref, dk_ref, dv_ref, cos_ref, sin_ref, dqr_ref, dkv_ref, dkpe_ref):
        cos_v, nsin_v = cos_ref[...], -sin_ref[...]
        kpe = jnp.zeros((tr, 128), F32)
        for h in range(MLA_HEADS):
            lo = h * HEAD_PAD
            dqr_ref[:, lo : lo + 128] = dq_ref[:, lo : lo + 128].astype(BF16)
            dqr_ref[:, lo + 128 : lo + 256] = _rope_tile(dq_ref[:, lo + 128 : lo + 256], cos_v, nsin_v).astype(BF16)
            dkv_ref[:, lo : lo + 128] = dk_ref[:, lo : lo + 128].astype(BF16)
            dkv_ref[:, lo + 128 : lo + 256] = dv_ref[:, h * 128 : (h + 1) * 128].astype(BF16)
            kpe = kpe + dk_ref[:, lo + 128 : lo + 256]
        dkpe_ref[...] = _rope_tile(kpe, cos_v, nsin_v)

    def row(n):
        return pl.BlockSpec((tr, n), lambda i: (i, 0))

    return pl.pallas_call(
        body,
        name=name,
        grid=(S // tr,),
        in_specs=[row(W), row(W), row(MLA_HEADS * V_HEAD), row(128), row(128)],
        out_specs=[row(W), row(W), row(128)],
        out_shape=[jax.ShapeDtypeStruct((S, W), BF16), jax.ShapeDtypeStruct((S, W), BF16), jax.ShapeDtypeStruct((S, 128), F32)],
        compiler_params=pltpu.CompilerParams(dimension_semantics=("parallel",)),
    )(dq_att, dk_att, dv, cos_t, sin_t)


def _lat_bwd(dcqn, dckvn, dkpe, lat, gq, gkv, name, tr=256):
    S = lat.shape[0]

    def body(dcq_ref, dckv_ref, dkpe_ref, lat_ref, gq_ref, gkv_ref, dlat_ref, dgq_ref, dgkv_ref):
        dq, dgq = _rms_bwd_math(dcq_ref[...], lat_ref[:, :Q_LORA], gq_ref[...])
        dkv, dgkv = _rms_bwd_math(dckv_ref[...], lat_ref[:, Q_LORA : Q_LORA + KV_LORA], gkv_ref[...])
        dlat_ref[:, :Q_LORA] = dq.astype(BF16)
        dlat_ref[:, Q_LORA : Q_LORA + KV_LORA] = dkv.astype(BF16)
        dlat_ref[:, Q_LORA + KV_LORA :] = dkpe_ref[...].astype(BF16)

        @pl.when(pl.program_id(0) == 0)
        def _():
            dgq_ref[...] = dgq
            dgkv_ref[...] = dgkv

        @pl.when(pl.program_id(0) != 0)
        def _():
            dgq_ref[...] += dgq
            dgkv_ref[...] += dgkv

    def row(n):
        return pl.BlockSpec((tr, n), lambda i: (i, 0))

    def vec(n):
        return pl.BlockSpec((1, n), lambda i: (0, 0))

    return pl.pallas_call(
        body,
        name=name,
        grid=(S // tr,),
        in_specs=[row(Q_LORA), row(KV_LORA), row(128), row(LAT_PAD), vec(Q_LORA), vec(KV_LORA)],
        out_specs=[row(LAT_PAD), vec(Q_LORA), vec(KV_LORA)],
        out_shape=[
            jax.ShapeDtypeStruct((S, LAT_PAD), BF16),
            jax.ShapeDtypeStruct((1, Q_LORA), F32),
            jax.ShapeDtypeStruct((1, KV_LORA), F32),
        ],
        compiler_params=pltpu.CompilerParams(dimension_semantics=("arbitrary",)),
    )(dcqn, dckvn, dkpe, lat, gq, gkv)


MLA_SCALE = (QK_NOPE + QK_ROPE) ** -0.5
LOG2E = 1.4426950408889634
MLA_C2 = MLA_SCALE * LOG2E
FLASH_T = 1024


def _causal_pairs(n, by_key):
    pairs = [(i, j) for j in range(n) for i in range(j, n)] if by_key else [(i, j) for i in range(n) for j in range(i + 1)]
    return jnp.asarray([p[0] for p in pairs], jnp.int32), jnp.asarray([p[1] for p in pairs], jnp.int32)


def _causal_mask(shape, shift, keys_first=False):
    q_axis, k_axis = (1, 0) if keys_first else (0, 1)
    return lax.broadcasted_iota(jnp.int32, shape, k_axis) <= lax.broadcasted_iota(jnp.int32, shape, q_axis) + shift


def _lanes(x, n):
    return jnp.tile(x, (1, n // 128))


def _flash_grid(npairs, in_specs, out_specs, scratch):
    return pltpu.PrefetchScalarGridSpec(
        num_scalar_prefetch=2, grid=(MLA_HEADS, npairs), in_specs=in_specs, out_specs=out_specs, scratch_shapes=scratch
    )


def _flash2_fwd(q_att, k_att, kv, name, t=FLASH_T):
    S = q_att.shape[0]
    half = t // 2
    qi_tab, kj_tab = _causal_pairs(S // t, by_key=False)

    def body(qi_ref, kj_ref, q_ref, k_ref, v_ref, o_ref, lse_ref, m_sc, l_sc, acc_sc):
        step = pl.program_id(1)
        qi, kj = qi_ref[step], kj_ref[step]

        @pl.when(kj == 0)
        def _():
            m_sc[...] = jnp.full((t, 128), NEG, F32)
            l_sc[...] = jnp.zeros((t, 128), F32)
            acc_sc[...] = jnp.zeros((t, V_HEAD), F32)

        def update(rows, s, v):
            m_prev = m_sc[rows, :]
            m_new = jnp.maximum(m_prev, jnp.max(s, axis=1, keepdims=True))
            p = jnp.exp2((s - _lanes(m_new, s.shape[1])) * MLA_C2)
            alpha = jnp.exp2((m_prev - m_new) * MLA_C2)
            l_sc[rows, :] = alpha * l_sc[rows, :] + jnp.sum(p, axis=1, keepdims=True)
            acc_sc[rows, :] = alpha * acc_sc[rows, :] + _dot(p.astype(BF16), v)
            m_sc[rows, :] = m_new

        @pl.when(kj < qi)
        def _():
            update(slice(0, t), _dot(q_ref[...], k_ref[...], NT), v_ref[...])

        @pl.when(kj == qi)
        def _():
            top = _dot(q_ref[:half, :], k_ref[:half, :], NT)
            update(slice(0, half), jnp.where(_causal_mask(top.shape, 0), top, NEG), v_ref[:half, :])
            bot = _dot(q_ref[half:, :], k_ref[...], NT)
            update(slice(half, t), jnp.where(_causal_mask(bot.shape, half), bot, NEG), v_ref[...])
            l = l_sc[...]
            o_ref[...] = acc_sc[...] / l
            lse_ref[0] = m_sc[...] * MLA_SCALE + jnp.log(l)

    return pl.pallas_call(
        body,
        name=name,
        grid_spec=_flash_grid(
            qi_tab.shape[0],
            [
                pl.BlockSpec((t, HEAD_PAD), lambda h, p, qi, kj: (qi[p], h)),
                pl.BlockSpec((t, HEAD_PAD), lambda h, p, qi, kj: (kj[p], h)),
                pl.BlockSpec((t, V_HEAD), lambda h, p, qi, kj: (kj[p], 2 * h + 1)),
            ],
            [
                pl.BlockSpec((t, V_HEAD), lambda h, p, qi, kj: (qi[p], h)),
                pl.BlockSpec((1, t, 128), lambda h, p, qi, kj: (h, qi[p], 0)),
            ],
            [pltpu.VMEM((t, 128), F32), pltpu.VMEM((t, 128), F32), pltpu.VMEM((t, V_HEAD), F32)],
        ),
        out_shape=[jax.ShapeDtypeStruct((S, MLA_HEADS * V_HEAD), F32), jax.ShapeDtypeStruct((MLA_HEADS, S, 128), F32)],
        compiler_params=pltpu.CompilerParams(dimension_semantics=("parallel", "arbitrary")),
    )(qi_tab, kj_tab, q_att, k_att, kv)


def _flash_delta(do, o, name, tr=512):
    S = o.shape[0]

    def body(do_ref, o_ref, d_ref):
        lane = lax.broadcasted_iota(jnp.int32, (tr, 128), 1)
        acc = jnp.zeros((tr, 128), F32)
        for h in range(MLA_HEADS):
            sl = slice(h * V_HEAD, (h + 1) * V_HEAD)
            acc = jnp.where(lane == h, jnp.sum(do_ref[:, sl].astype(F32) * o_ref[:, sl], axis=1, keepdims=True), acc)
        d_ref[...] = acc

    row = pl.BlockSpec((tr, MLA_HEADS * V_HEAD), lambda i: (i, 0))
    return pl.pallas_call(
        body,
        name=name,
        grid=(S // tr,),
        in_specs=[row, row],
        out_specs=pl.BlockSpec((tr, 128), lambda i: (i, 0)),
        out_shape=jax.ShapeDtypeStruct((S, 128), F32),
        compiler_params=pltpu.CompilerParams(dimension_semantics=("parallel",)),
    )(do, o)


def _flash2_bwd(q_att, k_att, kv, do, lse_row, delta_row, name, t=FLASH_T):
    S = q_att.shape[0]
    n = S // t
    qi_tab, kj_tab = _causal_pairs(n, by_key=True)
    last = qi_tab.shape[0] - 1
    half = t // 2

    def body(qi_ref, kj_ref, q_ref, k_ref, v_ref, do_ref, lse_ref, dl_ref, dq_ref, dk_ref, dv_ref, dk_sc, dv_sc):
        step = pl.program_id(1)
        qi, kj = qi_ref[step], kj_ref[step]

        @pl.when(step == 0)
        def _():
            dq_ref[...] = jnp.zeros((S, HEAD_PAD), F32)

        def update(k0, q0, st):
            nk, nq = st.shape
            kr, qr = slice(k0, k0 + nk), slice(q0, q0 + nq)
            q, do_v = q_ref[qr, :], do_ref[qr, :]
            pt = jnp.exp2(st * MLA_C2 - lse_ref[0][:, qr] * LOG2E)
            dv_sc[kr, :] += _dot(pt.astype(BF16), do_v)
            dpt = _dot(v_ref[kr, :], do_v, NT)
            dst = (pt * (dpt - dl_ref[0][:, qr])).astype(BF16)
            dk_sc[kr, :] += _dot(dst, q)
            rows = pl.ds(pl.multiple_of(qi * t + q0, half), nq)
            dq_ref[rows, :] += _dot(dst, k_ref[kr, :], TN)

        @pl.when(qi == kj)
        def _():
            dk_sc[...] = jnp.zeros((t, HEAD_PAD), F32)
            dv_sc[...] = jnp.zeros((t, V_HEAD), F32)
            top = _dot(k_ref[:half, :], q_ref[...], NT)
            update(0, 0, jnp.where(_causal_mask(top.shape, 0, keys_first=True), top, NEG))
            bot = _dot(k_ref[half:, :], q_ref[half:, :], NT)
            update(half, half, jnp.where(_causal_mask(bot.shape, 0, keys_first=True), bot, NEG))

        @pl.when(qi > kj)
        def _():
            update(0, 0, _dot(k_ref[...], q_ref[...], NT))

        @pl.when(qi == n - 1)
        def _():
            dk_ref[...] = dk_sc[...] * MLA_SCALE
            dv_ref[...] = dv_sc[...]

        @pl.when(step == last)
        def _():
            dq_ref[...] = dq_ref[...] * MLA_SCALE

    qrow = lambda h, p, qi, kj: (qi[p], h)
    krow = lambda h, p, qi, kj: (kj[p], h)
    stat = pl.BlockSpec((1, 1, t), lambda h, p, qi, kj: (h, 0, qi[p]))
    return pl.pallas_call(
        body,
        name=name,
        grid_spec=_flash_grid(
            qi_tab.shape[0],
            [
                pl.BlockSpec((t, HEAD_PAD), qrow),
                pl.BlockSpec((t, HEAD_PAD), krow),
                pl.BlockSpec((t, V_HEAD), lambda h, p, qi, kj: (kj[p], 2 * h + 1)),
                pl.BlockSpec((t, V_HEAD), qrow),
                stat,
                stat,
            ],
            [
                pl.BlockSpec((S, HEAD_PAD), lambda h, p, qi, kj: (0, h)),
                pl.BlockSpec((t, HEAD_PAD), krow),
                pl.BlockSpec((t, V_HEAD), krow),
            ],
            [pltpu.VMEM((t, HEAD_PAD), F32), pltpu.VMEM((t, V_HEAD), F32)],
        ),
        out_shape=[
            jax.ShapeDtypeStruct((S, MLA_HEADS * HEAD_PAD), F32),
            jax.ShapeDtypeStruct((S, MLA_HEADS * HEAD_PAD), F32),
            jax.ShapeDtypeStruct((S, MLA_HEADS * V_HEAD), F32),
        ],
        compiler_params=pltpu.CompilerParams(dimension_semantics=("parallel", "arbitrary")),
    )(qi_tab, kj_tab, q_att, k_att, kv, do, lse_row, delta_row)


DIL_SCALE = DIL_HD**-0.5


def _dil_bias():
    slopes = 2.0 ** (-ALIBI_MAX_BIAS * np.arange(1, DIL_HEADS + 1, dtype=np.float64) / DIL_HEADS)
    slopes = slopes.astype(np.float32).reshape(DIL_GROUPS, DIL_HG)
    p = np.arange(DIL_BLK)[:, None]
    kidx = np.arange(2 * DIL_BLK)[None, :]
    j = p + DIL_BLK - kidx
    out = np.zeros((DIL_GROUPS, DIL_HG, DIL_BLK, 2 * DIL_BLK), np.float32)
    for g, (window, dil) in enumerate(DIL_PATTERNS):
        valid = (j >= 0) & (j <= window // dil)
        for h in range(DIL_HG):
            alibi = -slopes[g, h] * (dil * j).astype(np.float32)
            out[g, h] = np.where(valid, alibi, np.float32(NEG))
    return jnp.asarray(out)


DIL_UNROLL = 4


def _unrolled_loop(lo, hi, fn, unroll=DIL_UNROLL):
    groups = (hi - lo) // unroll
    done = lo
    if groups > 1:

        def step(i, carry):
            for u in range(unroll):
                fn(lo + i * unroll + u)
            return carry

        lax.fori_loop(0, groups, step, 0)
        done = lo + groups * unroll
    for n in range(done, hi):
        fn(n)


def _dil_rows(r, n, count, dil):
    if dil == 1:
        if isinstance(n, int):
            return slice(n * DIL_BLK, (n + count) * DIL_BLK)
        return pl.ds(pl.multiple_of(n * DIL_BLK, DIL_BLK), count * DIL_BLK)
    return pl.ds(n * DIL_BLK * dil + r, count * DIL_BLK, stride=dil)


def _dil_each_block(S, dil, block):
    nb = S // dil // DIL_BLK
    if dil == 1:
        block(0, 0, True)
        _unrolled_loop(1, nb, lambda n: block(0, n, False))
    else:
        for r in range(dil):
            for n in range(nb):
                block(r, n, n == 0)


def _dil_col(g, part, h):
    return (g * 3 + part) * DIL_HG + h


def _dil_fwd_group(dqkv, bias_g, g, dil, name):
    S = dqkv.shape[0]

    def body(bias_ref, q_ref, k_ref, v_ref, o_ref, lse_ref):
        def block(r, n, first):
            cur = _dil_rows(r, n, 1, dil)
            both = cur if first else _dil_rows(r, n - 1, 2, dil)
            b = bias_ref[0][:, DIL_BLK:] if first else bias_ref[0]
            q, kk, vv = q_ref[cur, :].astype(BF16), k_ref[both, :].astype(BF16), v_ref[both, :].astype(BF16)
            s = _dot(q, kk, NT) * DIL_SCALE + b
            m = jnp.max(s, axis=1, keepdims=True)
            e = jnp.exp(s - m)
            l = jnp.sum(e, axis=1, keepdims=True)
            p = e * (1.0 / l)
            o_ref[cur, :] = _dot(p.astype(BF16), vv)
            lse_ref[cur, :] = jnp.broadcast_to(m + jnp.log(l), (DIL_BLK, 128))

        _dil_each_block(S, dil, block)

    def col(part):
        return pl.BlockSpec((S, DIL_HD), lambda h: (0, _dil_col(g, part, h)))

    out = pl.BlockSpec((S, DIL_HD), lambda h: (0, h))
    return pl.pallas_call(
        body,
        name=name,
        grid=(DIL_HG,),
        in_specs=[pl.BlockSpec((1, DIL_BLK, 2 * DIL_BLK), lambda h: (h, 0, 0)), col(0), col(1), col(2)],
        out_specs=[out, out],
        out_shape=[jax.ShapeDtypeStruct((S, DIL_OUT), F32), jax.ShapeDtypeStruct((S, DIL_OUT), F32)],
        compiler_params=pltpu.CompilerParams(dimension_semantics=("parallel",)),
    )(bias_g, dqkv, dqkv, dqkv)


def _dil_combine(os_, ls_, name, tr=512):
    S = os_[0].shape[0]

    def body(o0, o1, o2, l0, l1, l2, out_ref, lse_ref):
        a, b, c = l0[...], l1[...], l2[...]
        m = jnp.maximum(jnp.maximum(a, b), c)
        ea, eb, ec = jnp.exp(a - m), jnp.exp(b - m), jnp.exp(c - m)
        den = ea + eb + ec
        inv = 1.0 / den
        out_ref[...] = (ea * inv) * o0[...] + (eb * inv) * o1[...] + (ec * inv) * o2[...]
        lse_ref[...] = m + jnp.log(den)

    row = pl.BlockSpec((tr, DIL_OUT), lambda i: (i, 0))
    return pl.pallas_call(
        body,
        name=name,
        grid=(S // tr,),
        in_specs=[row] * 6,
        out_specs=[row, row],
        out_shape=[jax.ShapeDtypeStruct((S, DIL_OUT), F32)] * 2,
        compiler_params=pltpu.CompilerParams(dimension_semantics=("parallel",)),
    )(*os_, *ls_)


def _dil_rowdot(dod, od, name, tr=512):
    S = dod.shape[0]

    def body(d_ref, o_ref, dd_ref):
        for h in range(DIL_HG):
            sl = slice(h * 128, (h + 1) * 128)
            sm = jnp.sum(d_ref[:, sl] * o_ref[:, sl], axis=1, keepdims=True)
            dd_ref[:, sl] = jnp.broadcast_to(sm, (tr, 128))

    row = pl.BlockSpec((tr, DIL_OUT), lambda i: (i, 0))
    return pl.pallas_call(
        body,
        name=name,
        grid=(S // tr,),
        in_specs=[row, row],
        out_specs=row,
        out_shape=jax.ShapeDtypeStruct((S, DIL_OUT), F32),
        compiler_params=pltpu.CompilerParams(dimension_semantics=("parallel",)),
    )(dod, od)


def _dil_bwd_group(dqkv, bias_g, dod, dd, lse, grads, g, dil, name):
    S = dqkv.shape[0]

    def body(bias_ref, q_ref, k_ref, v_ref, do_ref, dd_ref, lse_ref, _, out_ref):
        out_ref[1] = jnp.zeros((S, DIL_HD), F32)
        out_ref[2] = jnp.zeros((S, DIL_HD), F32)

        def block(r, n, first):
            cur = _dil_rows(r, n, 1, dil)
            both = cur if first else _dil_rows(r, n - 1, 2, dil)
            b = bias_ref[0][:, DIL_BLK:] if first else bias_ref[0]
            q, kk, vv = q_ref[cur, :].astype(BF16), k_ref[both, :].astype(BF16), v_ref[both, :].astype(BF16)
            do = do_ref[cur, :].astype(BF16)
            s = _dot(q, kk, NT) * DIL_SCALE + b
            p = jnp.exp(s - lse_ref[cur, 0:1])
            dp = _dot(do, vv, NT)
            ds = ((p * (dp - dd_ref[cur, 0:1])) * DIL_SCALE).astype(BF16)
            out_ref[0, cur, :] = _dot(ds, kk)
            out_ref[1, both, :] += _dot(ds, q, TN)
            out_ref[2, both, :] += _dot(p.astype(BF16), do, TN)

        _dil_each_block(S, dil, block)

    def col(part):
        return pl.BlockSpec((S, DIL_HD), lambda h: (0, _dil_col(g, part, h)))

    nat = pl.BlockSpec((S, DIL_HD), lambda h: (0, h))
    return pl.pallas_call(
        body,
        name=name,
        grid=(DIL_HG,),
        in_specs=[pl.BlockSpec((1, DIL_BLK, 2 * DIL_BLK), lambda h: (h, 0, 0)), col(0), col(1), col(2), nat, nat, nat, ANY],
        out_specs=pl.BlockSpec((3, S, DIL_HD), lambda h: (g, 0, h)),
        out_shape=jax.ShapeDtypeStruct(grads.shape, F32),
        input_output_aliases={7: 0},
        compiler_params=pltpu.CompilerParams(dimension_semantics=("parallel",)),
    )(bias_g, dqkv, dqkv, dqkv, dod, dd, lse, grads)


def _merge_fwd(gates, o_a, o_b, name, tr=256):
    S = o_a.shape[0]

    def body(ga_ref, gb_ref, oa_ref, ob_ref, m_ref):
        m_ref[...] = (ga_ref[...] * oa_ref[...] + gb_ref[...] * ob_ref[...]).astype(BF16)

    row = pl.BlockSpec((tr, D_MODEL), lambda i: (i, 0))
    return pl.pallas_call(
        body,
        name=name,
        grid=(S // tr,),
        in_specs=[row, pl.BlockSpec((tr, D_MODEL), lambda i: (i, 1)), row, row],
        out_specs=row,
        out_shape=jax.ShapeDtypeStruct((S, D_MODEL), BF16),
        compiler_params=pltpu.CompilerParams(dimension_semantics=("parallel",)),
    )(gates, gates, o_a, o_b)


def _merge_bwd(dmrg, gates, o_a, o_b, name, tr=256):
    S = o_a.shape[0]

    def body(dm_ref, ga_ref, gb_ref, oa_ref, ob_ref, doa_ref, dob_ref, dga_ref, dgb_ref, dba_ref, dbb_ref):
        dm, ga, gb = dm_ref[...], ga_ref[...], gb_ref[...]
        doa_ref[...] = (dm * ga).astype(BF16)
        dob_ref[...] = (dm * gb).astype(BF16)
        dga = (dm * oa_ref[...]) * (ga * (1.0 - ga))
        dgb = (dm * ob_ref[...]) * (gb * (1.0 - gb))
        dga_ref[...] = dga.astype(BF16)
        dgb_ref[...] = dgb.astype(BF16)
        sa = jnp.sum(dga, axis=0, keepdims=True)
        sb = jnp.sum(dgb, axis=0, keepdims=True)

        @pl.when(pl.program_id(0) == 0)
        def _():
            dba_ref[...] = sa
            dbb_ref[...] = sb

        @pl.when(pl.program_id(0) != 0)
        def _():
            dba_ref[...] += sa
            dbb_ref[...] += sb

    row = pl.BlockSpec((tr, D_MODEL), lambda i: (i, 0))
    row1 = pl.BlockSpec((tr, D_MODEL), lambda i: (i, 1))
    vec = pl.BlockSpec((1, D_MODEL), lambda i: (0, 0))
    outs = pl.pallas_call(
        body,
        name=name,
        grid=(S // tr,),
        in_specs=[row, row, row1, row, row],
        out_specs=[row, row, row, row, vec, vec],
        out_shape=[jax.ShapeDtypeStruct((S, D_MODEL), BF16)] * 4 + [jax.ShapeDtypeStruct((1, D_MODEL), F32)] * 2,
        compiler_params=pltpu.CompilerParams(dimension_semantics=("arbitrary",)),
    )(dmrg, gates, gates, o_a, o_b)
    return outs


CONV_TR = 512
CONV_TC = 512
N_FFC = D_FF_PAD // CONV_TC


def _conv_taps(x, before, w_ref, b_ref):
    x0 = jnp.concatenate([before, x], axis=0)
    x1 = pltpu.roll(x0, 1, 0)
    x2 = pltpu.roll(x0, 2, 0)
    u = ((b_ref[...] + w_ref[0:1, :] * x2) + w_ref[1:2, :] * x1) + w_ref[2:3, :] * x0
    return u, x0, x1, x2


def _prev_halo(tr):
    return lambda i, j: (jnp.maximum(i * (tr // 8) - 1, 0), j)


def _ffn_fwd(u0, cw, cb, name):
    S = u0.shape[0]
    tr, tc = CONV_TR, CONV_TC

    def body(up_ref, gt_ref, hup_ref, hgt_ref, wu_ref, wg_ref, bu_ref, bg_ref, a_ref):
        live = (pl.program_id(0) > 0).astype(F32)
        up = _conv_taps(up_ref[...], hup_ref[...] * live, wu_ref, bu_ref)[0][8:]
        gt = _conv_taps(gt_ref[...], hgt_ref[...] * live, wg_ref, bg_ref)[0][8:]
        a_ref[...] = ((gt * jax.nn.sigmoid(gt)) * up).astype(BF16)

    return pl.pallas_call(
        body,
        name=name,
        grid=(S // tr, N_FFC),
        in_specs=[
            pl.BlockSpec((tr, tc), lambda i, j: (i, j)),
            pl.BlockSpec((tr, tc), lambda i, j: (i, j + N_FFC)),
            pl.BlockSpec((8, tc), _prev_halo(tr)),
            pl.BlockSpec((8, tc), lambda i, j: (jnp.maximum(i * (tr // 8) - 1, 0), j + N_FFC)),
            pl.BlockSpec((8, tc), lambda i, j: (0, j)),
            pl.BlockSpec((8, tc), lambda i, j: (0, j + N_FFC)),
            pl.BlockSpec((1, tc), lambda i, j: (0, j)),
            pl.BlockSpec((1, tc), lambda i, j: (0, j + N_FFC)),
        ],
        out_specs=pl.BlockSpec((tr, tc), lambda i, j: (i, j)),
        out_shape=jax.ShapeDtypeStruct((S, D_FF_PAD), BF16),
        compiler_params=pltpu.CompilerParams(dimension_semantics=("parallel", "parallel")),
    )(u0, u0, u0, u0, cw, cw, cb, cb)


def _ffn_bwd(u0, da, cw, cb, name):
    S = u0.shape[0]
    tr, tc = CONV_TR, CONV_TC
    nrow, te = S // tr, tr + 8

    def body(up_ref, gt_ref, hup_ref, hgt_ref, nup_ref, ngt_ref, da_ref, nda_ref, wu_ref, wg_ref, bu_ref, bg_ref, du0_ref, dcw_ref, dcb_ref):
        i = pl.program_id(1)
        prev_live = (i > 0).astype(F32)
        next_live = (i < nrow - 1).astype(F32)

        def conv(x_ref, nx_ref, h_ref, w_ref, b_ref):
            x = jnp.concatenate([x_ref[...], nx_ref[...] * next_live], axis=0)
            return [t[8:] for t in _conv_taps(x, h_ref[...] * prev_live, w_ref, b_ref)]

        up, xu0, xu1, xu2 = conv(up_ref, nup_ref, hup_ref, wu_ref, bu_ref)
        gt, xg0, xg1, xg2 = conv(gt_ref, ngt_ref, hgt_ref, wg_ref, bg_ref)
        da_v = jnp.concatenate([da_ref[...], nda_ref[...] * next_live], axis=0)
        sg = jax.nn.sigmoid(gt)
        d_up = da_v * (gt * sg)
        d_gt = (da_v * up) * (sg * (1.0 + gt * (1.0 - sg)))
        tap = lax.broadcasted_iota(jnp.int32, (8, tc), 0)

        def finish(half, du, x0, x1, x2, w_ref):
            n1 = pltpu.roll(du, te - 1, 0)
            n2 = pltpu.roll(du, te - 2, 0)
            du0 = (w_ref[2:3, :] * du + w_ref[1:2, :] * n1) + w_ref[0:1, :] * n2
            du0_ref[half] = du0[:tr].astype(BF16)
            d = du[:tr]
            dcw = jnp.where(
                tap == 0,
                jnp.sum(d * x2[:tr], axis=0, keepdims=True),
                jnp.where(tap == 1, jnp.sum(d * x1[:tr], axis=0, keepdims=True), jnp.where(tap == 2, jnp.sum(d * x0[:tr], axis=0, keepdims=True), 0.0)),
            )
            dcb = jnp.sum(d, axis=0, keepdims=True)

            @pl.when(i == 0)
            def _():
                dcw_ref[half] = dcw
                dcb_ref[half] = dcb

            @pl.when(i != 0)
            def _():
                dcw_ref[half] += dcw
                dcb_ref[half] += dcb

        finish(0, d_up, xu0, xu1, xu2, wu_ref)
        finish(1, d_gt, xg0, xg1, xg2, wg_ref)

    def prev8(off):
        return pl.BlockSpec((8, tc), lambda j, i: (jnp.maximum(i * (tr // 8) - 1, 0), j + off))

    def next8(off):
        return pl.BlockSpec((8, tc), lambda j, i: (jnp.minimum((i + 1) * (tr // 8), S // 8 - 1), j + off))

    return pl.pallas_call(
        body,
        name=name,
        grid=(N_FFC, nrow),
        in_specs=[
            pl.BlockSpec((tr, tc), lambda j, i: (i, j)),
            pl.BlockSpec((tr, tc), lambda j, i: (i, j + N_FFC)),
            prev8(0),
            prev8(N_FFC),
            next8(0),
            next8(N_FFC),
            pl.BlockSpec((tr, tc), lambda j, i: (i, j)),
            next8(0),
            pl.BlockSpec((8, tc), lambda j, i: (0, j)),
            pl.BlockSpec((8, tc), lambda j, i: (0, j + N_FFC)),
            pl.BlockSpec((1, tc), lambda j, i: (0, j)),
            pl.BlockSpec((1, tc), lambda j, i: (0, j + N_FFC)),
        ],
        out_specs=[
            pl.BlockSpec((2, tr, tc), lambda j, i: (0, i, j)),
            pl.BlockSpec((2, 8, tc), lambda j, i: (0, 0, j)),
            pl.BlockSpec((2, 1, tc), lambda j, i: (0, 0, j)),
        ],
        out_shape=[
            jax.ShapeDtypeStruct((2, S, D_FF_PAD), BF16),
            jax.ShapeDtypeStruct((2, 8, D_FF_PAD), F32),
            jax.ShapeDtypeStruct((2, 1, D_FF_PAD), F32),
        ],
        compiler_params=pltpu.CompilerParams(dimension_semantics=("parallel", "arbitrary")),
    )(u0, u0, u0, u0, u0, u0, da, da, cw, cw, cb, cb)


ADAMW_BLOCK_BYTES = 3 << 20


def _adamw(w, g, m, v, name):
    R, C = w.shape
    fits = [t for t in range(8, R + 1, 8) if R % t == 0 and t * C * 4 <= ADAMW_BLOCK_BYTES]
    tr = max(fits) if fits else R

    def body(w_ref, g_ref, m_ref, v_ref, d_ref, nm_ref, nv_ref):
        gv = g_ref[...]
        nm = ADAM_B1 * m_ref[...] + (1.0 - ADAM_B1) * gv
        nv = ADAM_B2 * v_ref[...] + (1.0 - ADAM_B2) * (gv * gv)
        m_hat = nm / (1.0 - ADAM_B1**ADAM_STEP)
        v_hat = nv / (1.0 - ADAM_B2**ADAM_STEP)
        d_ref[...] = -ADAM_LR * (m_hat / (jnp.sqrt(v_hat) + ADAM_EPS) + ADAM_WD * w_ref[...])
        nm_ref[...] = nm
        nv_ref[...] = nv

    blk = pl.BlockSpec((tr, C), lambda i: (i, 0))
    return pl.pallas_call(
        body,
        name=name,
        grid=(R // tr,),
        in_specs=[blk] * 4,
        out_specs=[blk] * 3,
        out_shape=[jax.ShapeDtypeStruct((R, C), F32)] * 3,
        compiler_params=pltpu.CompilerParams(dimension_semantics=("parallel",)),
    )(w, g, m, v)


ANY = pl.BlockSpec(memory_space=pl.ANY)


def _row_tile(rows):
    return max(t for t in range(16, 353, 16) if rows % t == 0)


def _pair_add(g, recv, core, name):
    _, R, C = g.shape
    tr = _row_tile(R)

    def body(core_ref, g_ref, r_ref, o_ref):
        o_ref[...] = (g_ref[...].astype(F32) + r_ref[...].astype(F32)).astype(o_ref.dtype)

    return pl.pallas_call(
        body,
        name=name,
        grid_spec=pltpu.PrefetchScalarGridSpec(
            num_scalar_prefetch=1,
            grid=(N_CHIP, R // tr),
            in_specs=[
                pl.BlockSpec((1, tr, C), lambda k, i, core_ref: (2 * k + core_ref[0], i, 0)),
                pl.BlockSpec((1, tr, C), lambda k, i, core_ref: (k, i, 0)),
            ],
            out_specs=pl.BlockSpec((1, tr, C), lambda k, i, core_ref: (k, i, 0)),
        ),
        out_shape=jax.ShapeDtypeStruct((N_CHIP, R, C), g.dtype),
        compiler_params=pltpu.CompilerParams(dimension_semantics=("parallel", "parallel")),
    )(core, g, recv)


HBM = pl.BlockSpec(memory_space=pltpu.HBM)
SEM = pl.BlockSpec(memory_space=pltpu.SEMAPHORE)
EFFECT = pltpu.SideEffectType.DATAFLOW_SIDE_EFFECTING
RELATIONS = tuple((dx, dy, dc) for dx in (0, 1) for dy in (0, 1) for dc in (0, 1))[1:]


def _related(rel):
    x, y, c = lax.axis_index("x"), lax.axis_index("y"), lax.axis_index("c")
    return (1 - x if rel[0] else x, 1 - y if rel[1] else y, 1 - c if rel[2] else c)


def _dev_index(pos):
    return 4 * pos[0] + 2 * pos[1] + pos[2]


def _peers(chips):
    if chips:
        return [r for r in RELATIONS if not r[2]], N_CHIP, lambda pos: 2 * pos[0] + pos[1]
    return list(RELATIONS), N_DEV, _dev_index


def _exchange_start(srcs, by_slot, after, name, chips=False):
    n = len(srcs)
    extra = [] if after is None else [after]
    rels, slots, slot_of = _peers(chips)
    lands = [lax.empty((slots,) + (s.shape[1:] if by_slot else s.shape), s.dtype) for s in srcs]
    nsem = len(rels) * n

    def body(*refs):
        src_refs, land_refs = refs[:n], refs[n : 2 * n]
        send_sems, recv_sems = refs[2 * n + len(extra)], refs[2 * n + len(extra) + 1]
        token = refs[-1]
        me = slot_of(_related((0, 0, 0)))
        for a in range(n):
            for k, rel in enumerate(rels):
                peer = _related(rel)
                pltpu.make_async_remote_copy(
                    src_ref=src_refs[a].at[slot_of(peer)] if by_slot else src_refs[a],
                    dst_ref=land_refs[a].at[me],
                    send_sem=send_sems.at[len(rels) * a + k],
                    recv_sem=recv_sems.at[len(rels) * a + k],
                    device_id=peer,
                    device_id_type=MESH,
                ).start()
        token[...] = jnp.zeros_like(token)

    def hbm(a):
        return pltpu.HBM(a.shape, a.dtype)

    outs = pl.pallas_call(
        body,
        name=name,
        out_shape=(
            pltpu.SemaphoreType.DMA((nsem,)),
            pltpu.SemaphoreType.DMA((nsem,)),
            *[hbm(s) for s in srcs],
            *[hbm(l) for l in lands],
            jax.ShapeDtypeStruct((8, 128), F32),
        ),
        in_specs=[HBM] * (2 * n) + [ANY] * len(extra),
        out_specs=(SEM, SEM, *[HBM] * (2 * n), pl.BlockSpec(memory_space=pltpu.VMEM)),
        input_output_aliases={i: 2 + i for i in range(2 * n)},
        compiler_params=pltpu.CompilerParams(has_side_effects=EFFECT),
    )(*[pltpu.with_memory_space_constraint(a, pltpu.HBM) for a in list(srcs) + lands], *extra)
    return (outs[0], outs[1], list(outs[2 : 2 + n]), list(outs[2 + n : 2 + 2 * n])), outs[-1]


def _exchange_wait(handle, by_slot, after, name, chips=False):
    send_sems, recv_sems, srcs, lands = handle
    n = len(srcs)
    rels = _peers(chips)[0]

    def body(*refs):
        src_refs, land_refs = refs[:n], refs[n : 2 * n]
        s_sems, r_sems = refs[2 * n], refs[2 * n + 1]
        for a in range(n):
            for k, rel in enumerate(rels):
                copy = pltpu.make_async_remote_copy(
                    src_ref=src_refs[a].at[0] if by_slot else src_refs[a],
                    dst_ref=land_refs[a].at[0],
                    send_sem=s_sems.at[len(rels) * a + k],
                    recv_sem=r_sems.at[len(rels) * a + k],
                    device_id=_related(rel),
                    device_id_type=MESH,
                )
                copy.wait_send()
                copy.wait_recv()

    outs = pl.pallas_call(
        body,
        name=name,
        out_shape=tuple(pltpu.HBM(a.shape, a.dtype) for a in srcs + lands),
        in_specs=[HBM] * (2 * n) + [SEM, SEM, ANY],
        out_specs=tuple([HBM] * (2 * n)),
        input_output_aliases={i: i for i in range(2 * n)},
        compiler_params=pltpu.CompilerParams(has_side_effects=EFFECT),
    )(*srcs, *lands, send_sems, recv_sems, after)
    return list(outs[:n]), list(outs[n:])


def _pair_start(g, name):
    land = lax.empty((N_CHIP,) + g.shape[1:], g.dtype)

    def body(g_ref, land_ref, send_sems, recv_sems, g_thru, land_thru, token):
        c = lax.axis_index("c")
        for k in range(N_CHIP):
            pltpu.make_async_remote_copy(
                src_ref=g_ref.at[2 * k + (1 - c)],
                dst_ref=land_ref.at[k],
                send_sem=send_sems.at[k],
                recv_sem=recv_sems.at[k],
                device_id=_related((0, 0, 1)),
                device_id_type=MESH,
            ).start()
        token[...] = jnp.zeros_like(token)

    outs = pl.pallas_call(
        body,
        name=name,
        out_shape=(
            pltpu.SemaphoreType.DMA((N_CHIP,)),
            pltpu.SemaphoreType.DMA((N_CHIP,)),
            pltpu.HBM(g.shape, g.dtype),
            pltpu.HBM(land.shape, land.dtype),
            jax.ShapeDtypeStruct((8, 128), F32),
        ),
        in_specs=[HBM, HBM],
        out_specs=(SEM, SEM, HBM, HBM, pl.BlockSpec(memory_space=pltpu.VMEM)),
        input_output_aliases={0: 2, 1: 3},
        compiler_params=pltpu.CompilerParams(has_side_effects=EFFECT),
    )(pltpu.with_memory_space_constraint(g, pltpu.HBM), pltpu.with_memory_space_constraint(land, pltpu.HBM))
    return outs[:4], outs[4]


def _pair_wait(handle, after, name):
    send_sems, recv_sems, g, land = handle

    def body(g_ref, land_ref, s_sems, r_sems, _, g_out, land_out):
        for k in range(N_CHIP):
            copy = pltpu.make_async_remote_copy(
                src_ref=g_ref.at[0],
                dst_ref=land_ref.at[0],
                send_sem=s_sems.at[k],
                recv_sem=r_sems.at[k],
                device_id=_related((0, 0, 1)),
                device_id_type=MESH,
            )
            copy.wait_send()
            copy.wait_recv()

    return pl.pallas_call(
        body,
        name=name,
        out_shape=(pltpu.HBM(g.shape, g.dtype), pltpu.HBM(land.shape, land.dtype)),
        in_specs=[HBM, HBM, SEM, SEM, ANY],
        out_specs=(HBM, HBM),
        input_output_aliases={0: 0, 1: 1},
        compiler_params=pltpu.CompilerParams(has_side_effects=EFFECT),
    )(g, land, send_sems, recv_sems, after)


NEAR = ((0, 0, 1), (1, 0, 0), (0, 1, 0), (1, 1, 0))


def _gather2_start(blocks, name):
    n = len(blocks)
    lands = [lax.empty((N_DEV,) + b.shape, b.dtype) for b in blocks]

    def body(*refs):
        src_refs, land_refs = refs[:n], refs[n : 2 * n]
        send_sems, recv_sems, token = refs[2 * n], refs[2 * n + 1], refs[-1]
        me = _dev_index(_related((0, 0, 0)))
        for a in range(n):
            for k, rel in enumerate(NEAR):
                pltpu.make_async_remote_copy(
                    src_ref=src_refs[a],
                    dst_ref=land_refs[a].at[me],
                    send_sem=send_sems.at[len(NEAR) * a + k],
                    recv_sem=recv_sems.at[len(NEAR) * a + k],
                    device_id=_related(rel),
                    device_id_type=MESH,
                ).start()
        token[...] = jnp.zeros_like(token)

    nsem = len(NEAR) * n
    outs = pl.pallas_call(
        body,
        name=name,
        out_shape=(
            pltpu.SemaphoreType.DMA((nsem,)),
            pltpu.SemaphoreType.DMA((nsem,)),
            *[pltpu.HBM(a.shape, a.dtype) for a in list(blocks) + lands],
            jax.ShapeDtypeStruct((8, 128), F32),
        ),
        in_specs=[HBM] * (2 * n),
        out_specs=(SEM, SEM, *[HBM] * (2 * n), pl.BlockSpec(memory_space=pltpu.VMEM)),
        input_output_aliases={i: 2 + i for i in range(2 * n)},
        compiler_params=pltpu.CompilerParams(has_side_effects=EFFECT),
    )(*[pltpu.with_memory_space_constraint(a, pltpu.HBM) for a in list(blocks) + lands])
    return (outs[0], outs[1], list(outs[2 : 2 + n]), list(outs[2 + n : 2 + 2 * n])), outs[-1]


def _gather2_forward(handle, after, name):
    send1, recv1, srcs, lands = handle
    n = len(srcs)

    def body(*refs):
        src_refs, land_refs = refs[:n], refs[n : 2 * n]
        s1, r1 = refs[2 * n], refs[2 * n + 1]
        s2, r2 = refs[-2], refs[-1]
        sibling = _related(NEAR[0])
        for a in range(n):
            for k, rel in enumerate(NEAR):
                first = pltpu.make_async_remote_copy(
                    src_ref=src_refs[a],
                    dst_ref=land_refs[a].at[0],
                    send_sem=s1.at[len(NEAR) * a + k],
                    recv_sem=r1.at[len(NEAR) * a + k],
                    device_id=_related(rel),
                    device_id_type=MESH,
                )
                first.wait_send()
                first.wait_recv()
                if k:
                    slot = land_refs[a].at[_dev_index(_related(rel))]
                    pltpu.make_async_remote_copy(
                        src_ref=slot,
                        dst_ref=slot,
                        send_sem=s2.at[3 * a + k - 1],
                        recv_sem=r2.at[3 * a + k - 1],
                        device_id=sibling,
                        device_id_type=MESH,
                    ).start()

    outs = pl.pallas_call(
        body,
        name=name,
        out_shape=(
            *[pltpu.HBM(a.shape, a.dtype) for a in srcs + lands],
            pltpu.SemaphoreType.DMA((3 * n,)),
            pltpu.SemaphoreType.DMA((3 * n,)),
        ),
        in_specs=[HBM] * (2 * n) + [SEM, SEM, ANY],
        out_specs=(*[HBM] * (2 * n), SEM, SEM),
        input_output_aliases={i: i for i in range(2 * n)},
        compiler_params=pltpu.CompilerParams(has_side_effects=EFFECT),
    )(*srcs, *lands, send1, recv1, after)
    return outs[-2], outs[-1], list(outs[:n]), list(outs[n : 2 * n])


def _gather2_wait(handle, name):
    send2, recv2, srcs, lands = handle
    n = len(srcs)

    def body(*refs):
        land_refs = refs[n : 2 * n]
        s2, r2 = refs[2 * n], refs[2 * n + 1]
        for a in range(n):
            for j in range(3):
                passed = pltpu.make_async_remote_copy(
                    src_ref=land_refs[a].at[0],
                    dst_ref=land_refs[a].at[0],
                    send_sem=s2.at[3 * a + j],
                    recv_sem=r2.at[3 * a + j],
                    device_id=_related(NEAR[0]),
                    device_id_type=MESH,
                )
                passed.wait_send()
                passed.wait_recv()

    outs = pl.pallas_call(
        body,
        name=name,
        out_shape=tuple(pltpu.HBM(a.shape, a.dtype) for a in srcs + lands),
        in_specs=[HBM] * (2 * n) + [SEM, SEM],
        out_specs=tuple([HBM] * (2 * n)),
        input_output_aliases={i: i for i in range(2 * n)},
        compiler_params=pltpu.CompilerParams(has_side_effects=EFFECT),
    )(*srcs, *lands, send2, recv2)
    return list(outs[:n]), list(outs[n:])


def _slot_sum(parts, name):
    n, R, C = parts.shape
    tr = _row_tile(R) if R % 16 == 0 else R

    def body(p_ref, o_ref):
        acc = p_ref[0].astype(F32)
        for k in range(1, n):
            acc = acc + p_ref[k].astype(F32)
        o_ref[...] = acc

    return pl.pallas_call(
        body,
        name=name,
        grid=(R // tr,),
        in_specs=[pl.BlockSpec((n, tr, C), lambda i: (0, i, 0))],
        out_specs=pl.BlockSpec((tr, C), lambda i: (i, 0)),
        out_shape=jax.ShapeDtypeStruct((R, C), F32),
        compiler_params=pltpu.CompilerParams(dimension_semantics=("parallel",)),
    )(parts)


W_IN_TC = 256
W_IN_BOUNDS = (0, LAT, LAT + 3 * DIL_QKV, LAT + 3 * DIL_QKV + D_MODEL, D_IN)


def _dqkv_chunks():
    return [((g * 3 + part) * DIL_OUT, LAT + part * DIL_QKV + g * DIL_OUT) for g in range(DIL_GROUPS) for part in range(3)]


def _w_in_regroup(slots, after, name):
    tc = W_IN_TC

    def body(s_ref, _, lat_ref, dqkv_ref, g_ref, buf):
        for j in range(N_DEV):
            buf[j * IN_ROWS : (j + 1) * IN_ROWS, :] = s_ref[j].astype(F32)[:IN_ROWS, :]
        lat_ref[:LAT, :] = buf[:LAT, :].astype(BF16)
        lat_ref[LAT:, :] = jnp.zeros((LAT_PAD - LAT, tc), BF16)
        for dst, src in _dqkv_chunks():
            dqkv_ref[dst : dst + DIL_OUT, :] = buf[src : src + DIL_OUT, :].astype(BF16)
        g_ref[...] = buf[W_IN_BOUNDS[2] :, :].astype(BF16)

    def col(rows):
        return pl.BlockSpec((rows, tc), lambda k: (0, k))

    return pl.pallas_call(
        body,
        name=name,
        grid=(D_MODEL // tc,),
        in_specs=[pl.BlockSpec((N_DEV, IN_ROWS_PAD, tc), lambda k: (0, 0, k)), pl.BlockSpec((8, 128), lambda k: (0, 0))],
        out_specs=[col(LAT_PAD), col(3 * DIL_QKV), col(2 * D_MODEL)],
        out_shape=[
            jax.ShapeDtypeStruct((LAT_PAD, D_MODEL), BF16),
            jax.ShapeDtypeStruct((3 * DIL_QKV, D_MODEL), BF16),
            jax.ShapeDtypeStruct((2 * D_MODEL, D_MODEL), BF16),
        ],
        scratch_shapes=[pltpu.VMEM((D_IN, tc), F32)],
        compiler_params=pltpu.CompilerParams(dimension_semantics=("parallel",)),
    )(slots, after)


def _w_in_grad_regroup(g_lat, g_dqkv, g_ga, g_gb, name):
    tc = W_IN_TC

    def body(lat_ref, dqkv_ref, ga_ref, gb_ref, o_ref, buf):
        b = W_IN_BOUNDS
        buf[b[0] : b[1], :] = lat_ref[:LAT, :].astype(F32)
        for dst, src in _dqkv_chunks():
            buf[src : src + DIL_OUT, :] = dqkv_ref[dst : dst + DIL_OUT, :].astype(F32)
        buf[b[2] : b[3], :] = ga_ref[...].astype(F32)
        buf[b[3] : b[4], :] = gb_ref[...].astype(F32)
        fill = jnp.zeros((IN_ROWS_PAD - IN_ROWS, tc), F32)
        for j in range(N_DEV):
            o_ref[j] = jnp.concatenate([buf[j * IN_ROWS : (j + 1) * IN_ROWS, :], fill], axis=0).astype(BF16)

    def col(rows):
        return pl.BlockSpec((rows, tc), lambda k: (0, k))

    return pl.pallas_call(
        body,
        name=name,
        grid=(D_MODEL // tc,),
        in_specs=[col(LAT_PAD), col(3 * DIL_QKV), col(D_MODEL), col(D_MODEL)],
        out_specs=pl.BlockSpec((N_DEV, IN_ROWS_PAD, tc), lambda k: (0, 0, k)),
        out_shape=jax.ShapeDtypeStruct((N_DEV, IN_ROWS_PAD, D_MODEL), BF16),
        scratch_shapes=[pltpu.VMEM((D_IN, tc), F32)],
        compiler_params=pltpu.CompilerParams(dimension_semantics=("parallel",)),
    )(g_lat, g_dqkv, g_ga, g_gb)


def _ffn_pad(a, axis):
    a = jnp.moveaxis(a, axis, -1)
    g = a.reshape(a.shape[:-1] + (2 * N_DEV, FF_GROUP))
    g = jnp.pad(g, [(0, 0)] * (g.ndim - 1) + [(0, FF_GROUP_PAD - FF_GROUP)])
    return jnp.moveaxis(g.reshape(a.shape[:-1] + (2 * D_FF_PAD,)), -1, axis)


def _ffn_unpad(a, axis):
    a = jnp.moveaxis(a, axis, -1)
    g = a.reshape(a.shape[:-1] + (2 * N_DEV, FF_GROUP_PAD))[..., :FF_GROUP]
    return jnp.moveaxis(g.reshape(a.shape[:-1] + (2 * D_FF,)), -1, axis)


MISC = (("w_o_mla", (256, 1024)), ("w_o_dil", (256, 512)), ("w_uq", (192, 512)), ("w_ukv", (256, 256)))
BIG_WEIGHTS = ("w_in", "w_up", "w_down", "w_out") + tuple(n for n, _ in MISC)


def _exchange_blocks(w):
    def t(a):
        return a.astype(BF16).T

    up = t(w["w_up"]).reshape(2, FF_GROUP, D_MODEL)
    return [
        jnp.pad(t(w["w_in"]), ((0, IN_ROWS_PAD - IN_ROWS), (0, 0))),
        jnp.pad(up, ((0, 0), (0, FF_GROUP_PAD - FF_GROUP), (0, 0))).reshape(2 * FF_GROUP_PAD, D_MODEL),
        jnp.pad(w["w_down"].astype(BF16), ((0, FF_GROUP_PAD - FF_GROUP), (0, 0))),
        w["w_out"].astype(BF16),
        jnp.concatenate([t(w[n]).reshape(-1, D_MODEL) for n, _ in MISC], axis=0),
    ]


def _misc_split(misc):
    out, off = {}, 0
    for n, (r, c) in MISC:
        rows = r * c // D_MODEL
        out[n] = misc[..., off : off + rows, :].reshape(misc.shape[:-2] + (r, c))
        off += rows
    return out


def _small_matrices(g_misc):
    misc = _misc_split(g_misc)
    uq_t = jnp.pad(misc["w_uq"], ((0, 0), (0, HEAD_PAD - QK_NOPE - QK_ROPE), (0, 0)))
    return {
        "uq_t": uq_t.reshape(MLA_HEADS * HEAD_PAD, Q_LORA),
        "ukv_t": misc["w_ukv"].reshape(MLA_HEADS * HEAD_PAD, KV_LORA),
        "o_mla_t": misc["w_o_mla"].reshape(D_MODEL, MLA_HEADS * V_HEAD),
        "o_dil_t": misc["w_o_dil"].reshape(D_MODEL, DIL_OUT),
    }


def _small_grad_blocks(g):
    uq_t = g["uq_t"].reshape(MLA_HEADS, HEAD_PAD, Q_LORA)[:, : QK_NOPE + QK_ROPE]
    misc = {"w_o_mla": g["o_mla_t"], "w_o_dil": g["o_dil_t"], "w_uq": uq_t, "w_ukv": g["ukv_t"]}
    return [
        g["w_out"].reshape(N_DEV, -1, D_MODEL),
        jnp.concatenate([misc[n].reshape(N_DEV, -1, D_MODEL) for n, _ in MISC], axis=1),
    ]


def _grad_shards(sums):
    s_in, s_out, s_misc, s_up, s_down = sums
    out = {
        "w_in": s_in[:IN_ROWS].T,
        "w_up": s_up.reshape(2, FF_GROUP_PAD, D_MODEL)[:, :FF_GROUP].reshape(2 * FF_GROUP, D_MODEL).T,
        "w_down": s_down[:FF_GROUP],
        "w_out": s_out,
    }
    out.update({n: v.T for n, v in _misc_split(s_misc).items()})
    return out


def _local_step(x, h, tgt, wt, conv_w, small, small_matrices, ffn_weight, send_ffn_grads, send_small_grads, send_w_in_grads, forward_w_in_grads):
    S = x.shape[0]
    lat_t, dqkv_t, g_t = wt
    cw = jnp.pad(_ffn_pad(conv_w, 1), ((0, 5), (0, 0)))
    cb = _ffn_pad(small["conv_b"], 1)
    cos_t, sin_t = _rope_tables(S)
    bias = _dil_bias()
    g1, g2, g3 = small["attn_norm_g"], small["ffn_norm_g"], small["final_norm_g"]
    gq, gkv = small["q_norm_g"], small["kv_norm_g"]

    lat = _mm(h, lat_t, "nt", F32, 1024, LAT_PAD, D_MODEL, "proj_lat")
    dqkv = _mm(h, dqkv_t, "nt", F32, 1024, 1536, D_MODEL, "proj_dqkv")
    gates = _mm(h, g_t, "nt", F32, 2048, 1024, D_MODEL, "proj_gates", bias=small["b_gate"], act="sigmoid")
    sm = small_matrices(gates)
    uq_t, ukv_t, o_mla_t, o_dil_t = sm["uq_t"], sm["ukv_t"], sm["o_mla_t"], sm["o_dil_t"]
    cqn, ckvn, kpe = _mla_prep1(lat, gq, gkv, cos_t, sin_t, "mla_prep1")
    q_raw = _mm(cqn, uq_t, "nt", F32, 1024, 1024, Q_LORA, "mla_uq")
    kv = _mm(ckvn, ukv_t, "nt", BF16, 1024, 1024, KV_LORA, "mla_ukv")
    q_att, k_att = _mla_prep2(q_raw, kv, kpe, cos_t, sin_t, "mla_prep2")
    o, lse = _flash2_fwd(q_att, k_att, kv, "mla_flash_fwd")
    o_a = _mm(o, o_mla_t, "nt", F32, 1024, 1024, MLA_HEADS * V_HEAD, "mla_out")

    d_os, d_ls = [], []
    for g, (_, dil) in enumerate(DIL_PATTERNS):
        og, lg = _dil_fwd_group(dqkv, bias[g], g, dil, f"dil_fwd_{g}")
        d_os.append(og)
        d_ls.append(lg)
    od, dil_lse = _dil_combine(d_os, d_ls, "dil_combine")
    o_b = _mm(od, o_dil_t, "nt", F32, 1024, 1024, DIL_OUT, "dil_out")

    mrg = _merge_fwd(gates, o_a, o_b, "merge_fwd")
    w_out = ffn_weight("w_out", mrg)
    x1, h2 = _mm_res_rms(mrg, w_out, x, g2, "mix_out")
    up_t = ffn_weight("up_t", h2)
    u0 = _mm(h2, up_t, "nt", F32, 2048, 1024, D_MODEL, "ffn_up")
    a = _ffn_fwd(u0, cw, cb, "ffn_conv_fwd")
    w_down = ffn_weight("w_down", a)
    x2 = _mm(a, w_down, "nn", F32, 1024, 1024, D_FF_PAD // 2, "ffn_down", res=x1)
    loss_part, dx2, dx2b, dg3 = _final_loss(x2, g3, tgt, "final_loss")

    da = _mm(dx2b, w_down, "nt", F32, 1024, D_FF_PAD // 4, D_MODEL, "ffn_down_dx")
    gw_down = _mm(a, dx2b, "tn", BF16, 512, D_MODEL, S, "ffn_down_dw")
    du0, dcw, dcb = _ffn_bwd(u0, da, cw, cb, "ffn_conv_bwd")
    du0 = du0.reshape(2 * S, D_FF_PAD)
    gw_up_t = _mm(du0, h2, "tn", BF16, 512, D_MODEL, S, "ffn_up_dw", a_halves=2)
    sent = send_ffn_grads(gw_up_t, gw_down)
    dh2 = _mm(du0, up_t, "nn", F32, 1024, 1024, D_FF_PAD // 2, "ffn_up_dx", a_halves=2)
    dx1, dx1b, dg2 = _rms_bwd(dh2, x1, g2 + sent, dx2, "rms_ffn_bwd")

    dmrg = _mm(dx1b, w_out, "nt", F32, 1024, 1024, D_MODEL, "mix_out_dx")
    gw_out = _mm(mrg, dx1b, "tn", BF16, 512, D_MODEL, S, "mix_out_dw")
    do_a, do_b, dga, dgb, dba, dbb = _merge_bwd(dmrg, gates, o_a, o_b, "merge_bwd")

    do = _mm(do_a, o_mla_t, "nn", BF16, 1024, 1024, D_MODEL, "mla_out_dx")
    gw_o_mla_t = _mm(do_a, o, "tn", BF16, 1024, 1024, 1024, "mla_out_dw")
    dod = _mm(do_b, o_dil_t, "nn", F32, 1024, DIL_OUT, D_MODEL, "dil_out_dx")
    gw_o_dil_t = _mm(do_b, od, "tn", BF16, 1024, DIL_OUT, 1024, "dil_out_dw")

    delta = _flash_delta(do, o, "mla_flash_delta")
    lse_row = lse[:, :, 0][:, None, :]
    delta_row = delta[:, :MLA_HEADS].T[:, None, :]
    dq_att, dk_att, dv = _flash2_bwd(q_att, k_att, kv, do, lse_row, delta_row, "mla_flash_bwd")
    dq_raw, dkv, dkpe = _mla_post(dq_att, dk_att, dv, cos_t, sin_t, "mla_post")
    dcqn = _mm(dq_raw, uq_t, "nn", F32, 1024, Q_LORA, MLA_HEADS * HEAD_PAD, "mla_uq_dx")
    gw_uq_t = _mm(dq_raw, cqn, "tn", BF16, 1024, Q_LORA, 1024, "mla_uq_dw")
    dckvn = _mm(dkv, ukv_t, "nn", F32, 1024, KV_LORA, MLA_HEADS * HEAD_PAD, "mla_ukv_dx")
    gw_ukv_t = _mm(dkv, ckvn, "tn", BF16, 1024, KV_LORA, 1024, "mla_ukv_dw")
    sent = send_small_grads({"uq_t": gw_uq_t, "ukv_t": gw_ukv_t, "o_mla_t": gw_o_mla_t, "o_dil_t": gw_o_dil_t, "w_out": gw_out})
    dlat, dgq, dgkv = _lat_bwd(dcqn, dckvn, dkpe, lat, gq + sent, gkv, "lat_bwd")

    dd = _dil_rowdot(dod, od, "dil_rowdot")
    ddqkv = lax.empty((3 * DIL_GROUPS, S, DIL_OUT), F32)
    for g, (_, dil) in enumerate(DIL_PATTERNS):
        ddqkv = _dil_bwd_group(dqkv, bias[g], dod, dd, dil_lse, ddqkv, g, dil, f"dil_bwd_{g}")
    gw_lat_t = _mm(dlat, h, "tn", BF16, LAT_PAD, 1024, S, "proj_lat_dw")
    gw_dqkv_t = _mm(ddqkv.reshape(3 * DIL_GROUPS * S, DIL_OUT), h, "tn", BF16, 512, 1024, S, "proj_dqkv_dw", a_halves=3 * DIL_GROUPS)
    gw_ga_t = _mm(dga, h, "tn", BF16, 512, D_MODEL, S, "proj_ga_dw")
    gw_gb_t = _mm(dgb, h, "tn", BF16, 512, D_MODEL, S, "proj_gb_dw")
    sent = send_w_in_grads(gw_lat_t, gw_dqkv_t, gw_ga_t, gw_gb_t)
    dh = _mm(dlat + sent.astype(BF16), lat_t, "nn", F32, 1024, 1024, LAT_PAD, "proj_lat_dx")
    dh = _stacked_mm(ddqkv, dqkv_t, dh, forward_w_in_grads(dh), "proj_dqkv_dx")
    dh = _mm(dga, g_t, "nn", F32, 1024, 1024, D_MODEL, "proj_ga_dx", res=dh)
    grad_x, dg1 = _mm_rms_bwd(dgb, g_t, 1, dh, x, g1, dx1, "proj_gb_dx_rms_attn_bwd")

    small_grads = {
        "attn_norm_g": dg1,
        "b_gate": jnp.concatenate([dba, dbb], axis=1),
        "q_norm_g": dgq,
        "kv_norm_g": dgkv,
        "ffn_norm_g": dg2,
        "conv_b": _ffn_unpad(jnp.concatenate([dcb[0], dcb[1]], axis=1), 1),
        "final_norm_g": dg3,
        "conv_w": _ffn_unpad(jnp.concatenate([dcw[0, :3], dcw[1, :3]], axis=1), 1),
    }
    return loss_part, grad_x, small_grads


SMALL_ORDER = ("attn_norm_g", "b_gate", "q_norm_g", "kv_norm_g", "ffn_norm_g", "conv_b", "final_norm_g", "conv_w")
WEIGHT_ORDER = (
    "attn_norm_g", "w_in", "b_gate", "q_norm_g", "w_uq", "kv_norm_g", "w_ukv", "w_o_mla", "w_o_dil", "w_out",
    "ffn_norm_g", "w_up", "conv_w", "conv_b", "w_down", "final_norm_g",
)


def kernel(x, attn_norm_g, w_in, b_gate, q_norm_g, w_uq, kv_norm_g, w_ukv, w_o_mla, w_o_dil, w_out, ffn_norm_g, w_up, conv_w, conv_b, w_down, final_norm_g, loss_target, m_attn_norm_g, m_w_in, m_b_gate, m_q_norm_g, m_w_uq, m_kv_norm_g, m_w_ukv, m_w_o_mla, m_w_o_dil, m_w_out, m_ffn_norm_g, m_w_up, m_conv_w, m_conv_b, m_w_down, m_final_norm_g, v_attn_norm_g, v_w_in, v_b_gate, v_q_norm_g, v_w_uq, v_kv_norm_g, v_w_ukv, v_w_o_mla, v_w_o_dil, v_w_out, v_ffn_norm_g, v_w_up, v_conv_w, v_conv_b, v_w_down, v_final_norm_g):
    env = dict(locals())
    dev = 4 * lax.axis_index("x") + 2 * lax.axis_index("y") + lax.axis_index("c")
    core = lax.axis_index("c").astype(jnp.int32).reshape(1)

    def two_d(a):
        return a.reshape(-1, a.shape[-1])

    w = {n: two_d(env[n]) for n in WEIGHT_ORDER}
    m = {n: two_d(env["m_" + n]) for n in WEIGHT_ORDER}
    v = {n: two_d(env["v_" + n]) for n in WEIGHT_ORDER}

    chip = 2 * lax.axis_index("x") + lax.axis_index("y")

    def own_slot_in(lands, own, slot=dev):
        return [lax.dynamic_update_slice(l, o[None], (slot, 0, 0)) for l, o in zip(lands, own)]

    b_in = _exchange_blocks(w)[0]
    r, c = CONV_SHARD
    conv = jnp.pad(w["conv_w"].reshape(-1), (0, 8 * SMALL_COLS - r * c)).reshape(8, SMALL_COLS)
    first_level, token = _gather2_start([b_in, conv], "ag_w_in_start")
    tied = {n: w[n] + token[0, 0] for n in BIG_WEIGHTS}
    _, b_up, b_down, b_out, b_misc = _exchange_blocks(tied)
    h = _rms_fwd(x[0], w["attn_norm_g"] + token[0, 0], "rms_attn")
    prepared = b_up[:1, :1] + b_down[:1, :1] + b_out[:1, :1] + b_misc[:1, :1] + h[:1, :1]
    own, lands = _gather2_wait(_gather2_forward(first_level, prepared, "ag_w_in_forward"), "ag_w_in_wait")
    g_in, conv = own_slot_in(lands, own)
    misc_gather, started = _exchange_start([b_misc], False, conv, "ag_small_start")
    ffn_gathers, started2 = {}, started
    for key, block in (("w_out", b_out), ("up_t", b_up), ("w_down", b_down)):
        ffn_gathers[key], started2 = _exchange_start([block], False, started2, f"ag_{key}_start")
    wt = _w_in_regroup(g_in, started2, "w_in_regroup")
    conv = conv.reshape(N_DEV, 8 * SMALL_COLS)[:, : r * c].reshape(N_DEV, r, c)
    conv_w_full = conv.transpose(1, 0, 2).reshape(r, N_DEV * c)
    small = {n: w[n] for n in SMALL_ORDER if n != "conv_w"}

    def small_matrices(after):
        own, lands = _exchange_wait(misc_gather, False, after, "ag_small_wait")
        return _small_matrices(own_slot_in(lands, own)[0])

    def ffn_weight(key, after):
        own, lands = _exchange_wait(ffn_gathers[key], False, after, f"ag_{key}_wait")
        return own_slot_in(lands, own)[0].reshape(-1, D_MODEL)

    reduces = {}

    def send_ffn_grads(gw_up_t, gw_down):
        blocks = [gw_up_t.reshape(N_DEV, 2 * FF_GROUP_PAD, D_MODEL), gw_down.reshape(N_DEV, FF_GROUP_PAD, D_MODEL)]
        reduces["ffn"], token = _exchange_start(blocks, True, None, "rs_ffn_start")
        return token[0, 0]

    def send_small_grads(g):
        reduces["small"], token = _exchange_start(_small_grad_blocks(g), True, None, "rs_small_start")
        return token[0, 0]

    def send_w_in_grads(g_lat, g_dqkv, g_ga, g_gb):
        e_in = _w_in_grad_regroup(g_lat, g_dqkv, g_ga, g_gb, "w_in_grad_regroup")
        reduces["pair"], token = _pair_start(e_in, "rs_w_in_pair_start")
        return token[0, 0]

    def forward_w_in_grads(after):
        e_in, recv = _pair_wait(reduces.pop("pair"), after, "rs_w_in_pair_wait")
        pair = _pair_add(e_in, recv, core, "rs_w_in_pair_add")
        reduces["w_in"], token = _exchange_start([pair], True, None, "rs_w_in_start", chips=True)
        return token

    loss_part, grad_x, small_grads = _local_step(
        x[0], h, loss_target[0], wt, conv_w_full, small, small_matrices, ffn_weight,
        send_ffn_grads, send_small_grads, send_w_in_grads, forward_w_in_grads,
    )
    loss = lax.psum(loss_part[0, 0], AXES)
    sflat = jnp.concatenate([small_grads[n].reshape(-1) for n in SMALL_ORDER])
    sflat = jnp.pad(sflat, (0, SMALL_ROWS * SMALL_COLS - sflat.shape[0])).reshape(SMALL_ROWS, SMALL_COLS)
    vec_gather, _ = _exchange_start([sflat], False, None, "rs_vec_start")

    def finish(key, by_chip, name):
        sent, lands = _exchange_wait(reduces[key], True, grad_x, name + "_wait", chips=by_chip)
        slot = chip if by_chip else dev
        own = [lax.dynamic_index_in_dim(s, slot, 0, keepdims=False) for s in sent]
        return [_slot_sum(p, f"{name}_sum_{i}") for i, p in enumerate(own_slot_in(lands, own, slot))]

    (s_in,) = finish("w_in", True, "rs_w_in")
    s_out, s_misc = finish("small", False, "rs_small")
    s_up, s_down = finish("ffn", False, "rs_ffn")
    gshard = _grad_shards([s_in, s_out, s_misc, s_up, s_down])

    updates = {n: _adamw(w[n], gshard[n], m[n], v[n], "adamw_" + n) for n in BIG_WEIGHTS}

    big_done = sum(updates[n][0][:1, :1] for n in BIG_WEIGHTS)
    own, lands = _exchange_wait(vec_gather, False, big_done, "rs_vec_wait")
    ssum = _slot_sum(own_slot_in(lands, own)[0], "small_sum").reshape(-1)
    gsmall, off = {}, 0
    for n in SMALL_ORDER:
        shape = (3, 2 * D_FF) if n == "conv_w" else w[n].shape
        size = shape[0] * shape[1]
        gsmall[n] = ssum[off : off + size].reshape(shape)
        off += size
    gsmall["conv_w"] = lax.dynamic_slice_in_dim(gsmall["conv_w"], dev * CONV_SHARD[1], CONV_SHARD[1], axis=1)
    updates.update({n: _adamw(w[n], gsmall[n], m[n], v[n], "adamw_" + n) for n in SMALL_ORDER})

    g_all = {**gshard, **gsmall}
    out_g, out_d, out_m, out_v = [], [], [], []
    for n in WEIGHT_ORDER:
        d, nm, nv = updates[n]
        shape = env[n].shape
        out_g.append(g_all[n].reshape(shape))
        out_d.append(d.reshape(shape))
        out_m.append(nm.reshape(shape))
        out_v.append(nv.reshape(shape))
    return (loss, grad_x[None], *out_g, *out_d, *out_m, *out_v)
```

```python
import functools

import jax
import jax.numpy as jnp
import numpy as np
from jax import lax
from jax.experimental import pallas as pl
from jax.experimental.pallas import tpu as pltpu

F32 = jnp.float32
BF16 = jnp.bfloat16

N_DEV = 8
N_CHIP = 4
AXES = ("x", "y", "c")
MESH = pl.DeviceIdType.MESH

D_MODEL = 2048
MLA_HEADS = 8
QK_NOPE = 128
QK_ROPE = 64
V_HEAD = 128
Q_LORA = 512
KV_LORA = 256
ROPE_THETA = 10000.0
HEAD_PAD = 256
DIL_PATTERNS = ((128, 1), (512, 4), (2048, 16))
DIL_GROUPS = 3
DIL_HG = 4
DIL_HEADS = 12
DIL_HD = 128
DIL_BLK = 128
DIL_QKV = DIL_HEADS * DIL_HD
DIL_OUT = DIL_HG * DIL_HD
ALIBI_MAX_BIAS = 8.0
D_FF = 5504
D_FF_PAD = 5632
NORM_EPS = 1e-6
LAT = Q_LORA + KV_LORA + QK_ROPE
LAT_PAD = 896
D_IN = LAT + 3 * DIL_QKV + 2 * D_MODEL
NEG = -1e30

ADAM_LR = 0.001
ADAM_B1 = 0.9
ADAM_B2 = 0.999
ADAM_EPS = 1e-08
ADAM_WD = 0.01
ADAM_STEP = 10

SMALL_ROWS = 56
SMALL_COLS = 1024

IN_ROWS = 1192
IN_ROWS_PAD = 1200
FF_GROUP = D_FF // N_DEV
FF_GROUP_PAD = D_FF_PAD // N_DEV
CONV_SHARD = (3, 1376)

NT = (((1,), (1,)), ((), ()))
TN = (((0,), (0,)), ((), ()))


def _dot(a, b, dims=(((1,), (0,)), ((), ()))):
    return lax.dot_general(a, b, dims, preferred_element_type=F32)


def _mm(a, b, mode, out_dtype, tm, tn, tk, name, bias=None, act=None, res=None, b_koff=0, a_halves=1):
    H = a_halves
    if mode == "nn":
        (M, K), (K2, N) = (a.shape[0] // H, a.shape[1] * H), b.shape
        assert (b_koff + 1) * K <= K2, (name, a.shape, b.shape)
        koff, K2 = b_koff * (K // tk), K
        kper, mrows = a.shape[1] // tk, M // tm
        a_spec = pl.BlockSpec((tm, tk), lambda i, j, k: (i + (k // kper) * mrows, k % kper))
        b_spec = pl.BlockSpec((tk, tn), lambda i, j, k: (k + koff, j))
        dims = (((1,), (0,)), ((), ()))
    elif mode == "nt":
        (M, K), (N, K2) = a.shape, b.shape
        a_spec = pl.BlockSpec((tm, tk), lambda i, j, k: (i, k))
        b_spec = pl.BlockSpec((tn, tk), lambda i, j, k: (j, k))
        dims = NT
    else:
        (K, M), (K2, N) = (a.shape[0] // H, a.shape[1] * H), b.shape
        mper, krows = a.shape[1] // tm, K // tk
        a_spec = pl.BlockSpec((tk, tm), lambda i, j, k: (k + (i // mper) * krows, i % mper))
        b_spec = pl.BlockSpec((tk, tn), lambda i, j, k: (k, j))
        dims = TN
    assert K == K2 and M % tm == 0 and N % tn == 0 and K % tk == 0, (name, a.shape, b.shape)
    nk = K // tk
    has_bias, has_res = bias is not None, res is not None

    def body(*refs):
        refs = list(refs)
        a_ref, b_ref = refs[0], refs[1]
        pos = 2
        bias_ref = res_ref = None
        if has_bias:
            bias_ref = refs[pos]
            pos += 1
        if has_res:
            res_ref = refs[pos]
            pos += 1
        o_ref = refs[pos]
        p = _dot(a_ref[...].astype(BF16), b_ref[...].astype(BF16), dims)

        def finish(acc):
            if has_bias:
                acc = acc + bias_ref[...]
            if act == "sigmoid":
                acc = jax.nn.sigmoid(acc)
            if has_res:
                acc = res_ref[...] + acc
            o_ref[...] = acc.astype(o_ref.dtype)

        if nk == 1:
            finish(p)
        else:
            acc_ref = refs[pos + 1]
            k = pl.program_id(2)

            @pl.when(k == 0)
            def _():
                acc_ref[...] = p

            @pl.when(k != 0)
            def _():
                acc_ref[...] += p

            @pl.when(k == nk - 1)
            def _():
                finish(acc_ref[...])

    in_specs = [a_spec, b_spec]
    args = [a, b]
    if has_bias:
        in_specs.append(pl.BlockSpec((1, tn), lambda i, j, k: (0, j)))
        args.append(bias)
    if has_res:
        in_specs.append(pl.BlockSpec((tm, tn), lambda i, j, k: (i, j)))
        args.append(res)
    return pl.pallas_call(
        body,
        name=name,
        grid=(M // tm, N // tn, nk),
        in_specs=in_specs,
        out_specs=pl.BlockSpec((tm, tn), lambda i, j, k: (i, j)),
        out_shape=jax.ShapeDtypeStruct((M, N), out_dtype),
        scratch_shapes=[pltpu.VMEM((tm, tn), F32)] if nk > 1 else [],
        compiler_params=pltpu.CompilerParams(dimension_semantics=("parallel", "parallel", "arbitrary")),
    )(*args)


def _stacked_mm(pieces, w_t, res, after, name, tm=512, tn=1024):
    P, M, W = pieces.shape
    N = w_t.shape[1]

    def body(a_ref, b_ref, r_ref, _, o_ref):
        acc = r_ref[...]
        for p in range(P):
            acc = acc + _dot(a_ref[p].astype(BF16), b_ref[p * W : (p + 1) * W, :])
        o_ref[...] = acc

    tile = pl.BlockSpec((tm, tn), lambda i, j: (i, j))
    return pl.pallas_call(
        body,
        name=name,
        grid=(M // tm, N // tn),
        in_specs=[
            pl.BlockSpec((P, tm, W), lambda i, j: (0, i, 0)),
            pl.BlockSpec((P * W, tn), lambda i, j: (0, j)),
            tile,
            pl.BlockSpec((8, 128), lambda i, j: (0, 0)),
        ],
        out_specs=tile,
        out_shape=jax.ShapeDtypeStruct((M, N), F32),
        compiler_params=pltpu.CompilerParams(dimension_semantics=("parallel", "parallel")),
    )(pieces, w_t, res, after)


def _rstd(x):
    return lax.rsqrt(jnp.mean(x * x, axis=-1, keepdims=True) + NORM_EPS)


def _rms_bwd_math(dy, x, g):
    r = _rstd(x)
    xh = x * r
    dg = jnp.sum(dy * xh, axis=0, keepdims=True)
    dxh = dy * g
    dx = r * (dxh - xh * jnp.mean(dxh * xh, axis=-1, keepdims=True))
    return dx, dg


def _rms_fwd(x, g, name, tr=256):
    S, D = x.shape

    def body(x_ref, g_ref, o_ref):
        xv = x_ref[...]
        o_ref[...] = ((xv * _rstd(xv)) * g_ref[...]).astype(o_ref.dtype)

    return pl.pallas_call(
        body,
        name=name,
        grid=(S // tr,),
        in_specs=[pl.BlockSpec((tr, D), lambda i: (i, 0)), pl.BlockSpec((1, D), lambda i: (0, 0))],
        out_specs=pl.BlockSpec((tr, D), lambda i: (i, 0)),
        out_shape=jax.ShapeDtypeStruct((S, D), BF16),
        compiler_params=pltpu.CompilerParams(dimension_semantics=("parallel",)),
    )(x, g)


def _rms_bwd(dy, x, g, res, name, tr=256):
    S, D = x.shape

    def body(dy_ref, x_ref, g_ref, res_ref, dx_ref, dxb_ref, dg_ref):
        dx, dg = _rms_bwd_math(dy_ref[...], x_ref[...], g_ref[...])
        dx = dx + res_ref[...]
        dx_ref[...] = dx
        dxb_ref[...] = dx.astype(BF16)

        @pl.when(pl.program_id(0) == 0)
        def _():
            dg_ref[...] = dg

        @pl.when(pl.program_id(0) != 0)
        def _():
            dg_ref[...] += dg

    row = pl.BlockSpec((tr, D), lambda i: (i, 0))
    vec = pl.BlockSpec((1, D), lambda i: (0, 0))
    return pl.pallas_call(
        body,
        name=name,
        grid=(S // tr,),
        in_specs=[row, row, vec, row],
        out_specs=[row, row, vec],
        out_shape=[jax.ShapeDtypeStruct((S, D), F32), jax.ShapeDtypeStruct((S, D), BF16), jax.ShapeDtypeStruct((1, D), F32)],
        compiler_params=pltpu.CompilerParams(dimension_semantics=("arbitrary",)),
    )(dy, x, g, res)


def _mm_res_rms(a, b, res, g, name, tm=256):
    M, K = a.shape
    D = b.shape[1]

    def body(a_ref, b_ref, res_ref, g_ref, y_ref, h_ref):
        y = res_ref[...] + _dot(a_ref[...], b_ref[...])
        y_ref[...] = y
        h_ref[...] = ((y * _rstd(y)) * g_ref[...]).astype(BF16)

    row = pl.BlockSpec((tm, D), lambda i: (i, 0))
    return pl.pallas_call(
        body,
        name=name,
        grid=(M // tm,),
        in_specs=[pl.BlockSpec((tm, K), lambda i: (i, 0)), pl.BlockSpec((K, D), lambda i: (0, 0)), row, pl.BlockSpec((1, D), lambda i: (0, 0))],
        out_specs=[row, row],
        out_shape=[jax.ShapeDtypeStruct((M, D), F32), jax.ShapeDtypeStruct((M, D), BF16)],
        compiler_params=pltpu.CompilerParams(dimension_semantics=("parallel",)),
    )(a, b, res, g)


def _mm_rms_bwd(a, b, b_koff, dy_part, x, g, res, name, tm=256):
    M, K = a.shape
    D = x.shape[1]

    def body(a_ref, b_ref, dyp_ref, x_ref, g_ref, res_ref, dx_ref, dg_ref):
        dy = dyp_ref[...] + _dot(a_ref[...], b_ref[...])
        dx, dg = _rms_bwd_math(dy, x_ref[...], g_ref[...])
        dx_ref[...] = dx + res_ref[...]

        @pl.when(pl.program_id(0) == 0)
        def _():
            dg_ref[...] = dg

        @pl.when(pl.program_id(0) != 0)
        def _():
            dg_ref[...] += dg

    row = pl.BlockSpec((tm, D), lambda i: (i, 0))
    vec = pl.BlockSpec((1, D), lambda i: (0, 0))
    return pl.pallas_call(
        body,
        name=name,
        grid=(M // tm,),
        in_specs=[pl.BlockSpec((tm, K), lambda i: (i, 0)), pl.BlockSpec((K, D), lambda i: (b_koff, 0)), row, row, vec, row],
        out_specs=[row, vec],
        out_shape=[jax.ShapeDtypeStruct((M, D), F32), jax.ShapeDtypeStruct((1, D), F32)],
        compiler_params=pltpu.CompilerParams(dimension_semantics=("arbitrary",)),
    )(a, b, dy_part, x, g, res)


def _final_loss(x2, g, tgt, name, tr=256):
    S, D = x2.shape

    def body(x_ref, g_ref, t_ref, loss_ref, dx_ref, dxb_ref, dg_ref):
        xv, gv = x_ref[...], g_ref[...]
        y = (xv * _rstd(xv)) * gv
        e = y - t_ref[...]
        part = 0.5 * jnp.sum(jnp.mean(e * e, axis=-1, keepdims=True), axis=0, keepdims=True)
        dx, dg = _rms_bwd_math(e * (1.0 / D), xv, gv)
        dx_ref[...] = dx
        dxb_ref[...] = dx.astype(BF16)
        part = jnp.broadcast_to(part, (1, 128))

        @pl.when(pl.program_id(0) == 0)
        def _():
            dg_ref[...] = dg
            loss_ref[...] = part

        @pl.when(pl.program_id(0) != 0)
        def _():
            dg_ref[...] += dg
            loss_ref[...] += part

    row = pl.BlockSpec((tr, D), lambda i: (i, 0))
    vec = pl.BlockSpec((1, D), lambda i: (0, 0))
    return pl.pallas_call(
        body,
        name=name,
        grid=(S // tr,),
        in_specs=[row, vec, row],
        out_specs=[pl.BlockSpec((1, 128), lambda i: (0, 0)), row, row, vec],
        out_shape=[
            jax.ShapeDtypeStruct((1, 128), F32),
            jax.ShapeDtypeStruct((S, D), F32),
            jax.ShapeDtypeStruct((S, D), BF16),
            jax.ShapeDtypeStruct((1, D), F32),
        ],
        compiler_params=pltpu.CompilerParams(dimension_semantics=("arbitrary",)),
    )(x2, g, tgt)


def _rope_tables(S):
    pos = jnp.arange(S, dtype=F32)
    inv_freq = ROPE_THETA ** (-jnp.arange(0, QK_ROPE, 2, dtype=F32) / QK_ROPE)
    ang = pos[:, None] * inv_freq[None, :]
    cos, sin = jnp.cos(ang), jnp.sin(ang)
    zero = jnp.zeros((S, 128 - QK_ROPE), F32)
    return jnp.concatenate([cos, cos, zero], axis=1), jnp.concatenate([-sin, sin, zero], axis=1)


def _rope_tile(x, cos_t, sin_t):
    lane = lax.broadcasted_iota(jnp.int32, x.shape, 1)
    partner = jnp.where(lane < QK_ROPE // 2, pltpu.roll(x, 128 - QK_ROPE // 2, 1), pltpu.roll(x, QK_ROPE // 2, 1))
    return x * cos_t + partner * sin_t


def _mla_prep1(lat, gq, gkv, cos_t, sin_t, name, tr=256):
    S = lat.shape[0]

    def body(lat_ref, gq_ref, gkv_ref, cos_ref, sin_ref, cq_ref, ckv_ref, kpe_ref):
        cq = lat_ref[:, :Q_LORA]
        ckv = lat_ref[:, Q_LORA : Q_LORA + KV_LORA]
        cq_ref[...] = ((cq * _rstd(cq)) * gq_ref[...]).astype(BF16)
        ckv_ref[...] = ((ckv * _rstd(ckv)) * gkv_ref[...]).astype(BF16)
        kpe_ref[...] = _rope_tile(lat_ref[:, Q_LORA + KV_LORA :], cos_ref[...], sin_ref[...]).astype(BF16)

    def row(n):
        return pl.BlockSpec((tr, n), lambda i: (i, 0))

    def vec(n):
        return pl.BlockSpec((1, n), lambda i: (0, 0))

    return pl.pallas_call(
        body,
        name=name,
        grid=(S // tr,),
        in_specs=[row(LAT_PAD), vec(Q_LORA), vec(KV_LORA), row(128), row(128)],
        out_specs=[row(Q_LORA), row(KV_LORA), row(128)],
        out_shape=[
            jax.ShapeDtypeStruct((S, Q_LORA), BF16),
            jax.ShapeDtypeStruct((S, KV_LORA), BF16),
            jax.ShapeDtypeStruct((S, 128), BF16),
        ],
        compiler_params=pltpu.CompilerParams(dimension_semantics=("parallel",)),
    )(lat, gq, gkv, cos_t, sin_t)


def _mla_prep2(q_raw, kv, kpe, cos_t, sin_t, name, tr=256):
    S = q_raw.shape[0]
    W = MLA_HEADS * HEAD_PAD

    def body(q_ref, kv_ref, kpe_ref, cos_ref, sin_ref, qa_ref, ka_ref):
        cos_v, sin_v, kpe_v = cos_ref[...], sin_ref[...], kpe_ref[...]
        for h in range(MLA_HEADS):
            lo = h * HEAD_PAD
            qa_ref[:, lo : lo + 128] = q_ref[:, lo : lo + 128].astype(BF16)
            qa_ref[:, lo + 128 : lo + 256] = _rope_tile(q_ref[:, lo + 128 : lo + 256], cos_v, sin_v).astype(BF16)
            ka_ref[:, lo : lo + 128] = kv_ref[:, lo : lo + 128]
            ka_ref[:, lo + 128 : lo + 256] = kpe_v

    def row(n):
        return pl.BlockSpec((tr, n), lambda i: (i, 0))

    return pl.pallas_call(
        body,
        name=name,
        grid=(S // tr,),
        in_specs=[row(W), row(W), row(128), row(128), row(128)],
        out_specs=[row(W), row(W)],
        out_shape=[jax.ShapeDtypeStruct((S, W), BF16), jax.ShapeDtypeStruct((S, W), BF16)],
        compiler_params=pltpu.CompilerParams(dimension_semantics=("parallel",)),
    )(q_raw, kv, kpe, cos_t, sin_t)


def _mla_post(dq_att, dk_att, dv, cos_t, sin_t, name, tr=256):
    S = dq_att.shape[0]
    W = MLA_HEADS * HEAD_PAD

    def body(dq_ref, dk_ref, dv_ref, cos_ref, sin_ref, dqr_ref, dkv_ref, dkpe_ref):
        cos_v, nsin_v = cos_ref[...], -sin_ref[...]
        kpe = jnp.zeros((tr, 128), F32)
        for h in range(MLA_HEADS):
            lo = h * HEAD_PAD
            dqr_ref[:, lo : lo + 128] = dq_ref[:, lo : lo + 128].astype(BF16)
            dqr_ref[:, lo + 128 : lo + 256] = _rope_tile(dq_ref[:, lo + 128 : lo + 256], cos_v, nsin_v).astype(BF16)
            dkv_ref[:, lo : lo + 128] = dk_ref[:, lo : lo + 128].astype(BF16)
            dkv_ref[:, lo + 128 : lo + 256] = dv_ref[:, h * 128 : (h + 1) * 128].astype(BF16)
            kpe = kpe + dk_ref[:, lo + 128 : lo + 256]
        dkpe_ref[...] = _rope_tile(kpe, cos_v, nsin_v)

    def row(n):
        return pl.BlockSpec((tr, n), lambda i: (i, 0))

    return pl.pallas_call(
        body,
        name=name,
        grid=(S // tr,),
        in_specs=[row(W), row(W), row(MLA_HEADS * V_HEAD), row(128), row(128)],
        out_specs=[row(W), row(W), row(128)],
        out_shape=[jax.ShapeDtypeStruct((S, W), BF16), jax.ShapeDtypeStruct((S, W), BF16), jax.ShapeDtypeStruct((S, 128), F32)],
        compiler_params=pltpu.CompilerParams(dimension_semantics=("parallel",)),
    )(dq_att, dk_att, dv, cos_t, sin_t)


def _lat_bwd(dcqn, dckvn, dkpe, lat, gq, gkv, name, tr=256):
    S = lat.shape[0]

    def body(dcq_ref, dckv_ref, dkpe_ref, lat_ref, gq_ref, gkv_ref, dlat_ref, dgq_ref, dgkv_ref):
        dq, dgq = _rms_bwd_math(dcq_ref[...], lat_ref[:, :Q_LORA], gq_ref[...])
        dkv, dgkv = _rms_bwd_math(dckv_ref[...], lat_ref[:, Q_LORA : Q_LORA + KV_LORA], gkv_ref[...])
        dlat_ref[:, :Q_LORA] = dq.astype(BF16)
        dlat_ref[:, Q_LORA : Q_LORA + KV_LORA] = dkv.astype(BF16)
        dlat_ref[:, Q_LORA + KV_LORA :] = dkpe_ref[...].astype(BF16)

        @pl.when(pl.program_id(0) == 0)
        def _():
            dgq_ref[...] = dgq
            dgkv_ref[...] = dgkv

        @pl.when(pl.program_id(0) != 0)
        def _():
            dgq_ref[...] += dgq
            dgkv_ref[...] += dgkv

    def row(n):
        return pl.BlockSpec((tr, n), lambda i: (i, 0))

    def vec(n):
        return pl.BlockSpec((1, n), lambda i: (0, 0))

    return pl.pallas_call(
        body,
        name=name,
        grid=(S // tr,),
        in_specs=[row(Q_LORA), row(KV_LORA), row(128), row(LAT_PAD), vec(Q_LORA), vec(KV_LORA)],
        out_specs=[row(LAT_PAD), vec(Q_LORA), vec(KV_LORA)],
        out_shape=[
            jax.ShapeDtypeStruct((S, LAT_PAD), BF16),
            jax.ShapeDtypeStruct((1, Q_LORA), F32),
            jax.ShapeDtypeStruct((1, KV_LORA), F32),
        ],
        compiler_params=pltpu.CompilerParams(dimension_semantics=("arbitrary",)),
    )(dcqn, dckvn, dkpe, lat, gq, gkv)


MLA_SCALE = (QK_NOPE + QK_ROPE) ** -0.5
LOG2E = 1.4426950408889634
MLA_C2 = MLA_SCALE * LOG2E
FLASH_T = 1024


def _causal_pairs(n, by_key):
    pairs = [(i, j) for j in range(n) for i in range(j, n)] if by_key else [(i, j) for i in range(n) for j in range(i + 1)]
    return jnp.asarray([p[0] for p in pairs], jnp.int32), jnp.asarray([p[1] for p in pairs], jnp.int32)


def _causal_mask(shape, shift, keys_first=False):
    q_axis, k_axis = (1, 0) if keys_first else (0, 1)
    return lax.broadcasted_iota(jnp.int32, shape, k_axis) <= lax.broadcasted_iota(jnp.int32, shape, q_axis) + shift


def _lanes(x, n):
    return jnp.tile(x, (1, n // 128))


def _flash_grid(npairs, in_specs, out_specs, scratch):
    return pltpu.PrefetchScalarGridSpec(
        num_scalar_prefetch=2, grid=(MLA_HEADS, npairs), in_specs=in_specs, out_specs=out_specs, scratch_shapes=scratch
    )


def _flash2_fwd(q_att, k_att, kv, name, t=FLASH_T):
    S = q_att.shape[0]
    half = t // 2
    qi_tab, kj_tab = _causal_pairs(S // t, by_key=False)

    def body(qi_ref, kj_ref, q_ref, k_ref, v_ref, o_ref, lse_ref, m_sc, l_sc, acc_sc):
        step = pl.program_id(1)
        qi, kj = qi_ref[step], kj_ref[step]

        @pl.when(kj == 0)
        def _():
            m_sc[...] = jnp.full((t, 128), NEG, F32)
            l_sc[...] = jnp.zeros((t, 128), F32)
            acc_sc[...] = jnp.zeros((t, V_HEAD), F32)

        def update(rows, s, v):
            m_prev = m_sc[rows, :]
            m_new = jnp.maximum(m_prev, jnp.max(s, axis=1, keepdims=True))
            p = jnp.exp2((s - _lanes(m_new, s.shape[1])) * MLA_C2)
            alpha = jnp.exp2((m_prev - m_new) * MLA_C2)
            l_sc[rows, :] = alpha * l_sc[rows, :] + jnp.sum(p, axis=1, keepdims=True)
            acc_sc[rows, :] = alpha * acc_sc[rows, :] + _dot(p.astype(BF16), v)
            m_sc[rows, :] = m_new

        @pl.when(kj < qi)
        def _():
            update(slice(0, t), _dot(q_ref[...], k_ref[...], NT), v_ref[...])

        @pl.when(kj == qi)
        def _():
            top = _dot(q_ref[:half, :], k_ref[:half, :], NT)
            update(slice(0, half), jnp.where(_causal_mask(top.shape, 0), top, NEG), v_ref[:half, :])
            bot = _dot(q_ref[half:, :], k_ref[...], NT)
            update(slice(half, t), jnp.where(_causal_mask(bot.shape, half), bot, NEG), v_ref[...])
            l = l_sc[...]
            o_ref[...] = acc_sc[...] / l
            lse_ref[0] = m_sc[...] * MLA_SCALE + jnp.log(l)

    return pl.pallas_call(
        body,
        name=name,
        grid_spec=_flash_grid(
            qi_tab.shape[0],
            [
                pl.BlockSpec((t, HEAD_PAD), lambda h, p, qi, kj: (qi[p], h)),
                pl.BlockSpec((t, HEAD_PAD), lambda h, p, qi, kj: (kj[p], h)),
                pl.BlockSpec((t, V_HEAD), lambda h, p, qi, kj: (kj[p], 2 * h + 1)),
            ],
            [
                pl.BlockSpec((t, V_HEAD), lambda h, p, qi, kj: (qi[p], h)),
                pl.BlockSpec((1, t, 128), lambda h, p, qi, kj: (h, qi[p], 0)),
            ],
            [pltpu.VMEM((t, 128), F32), pltpu.VMEM((t, 128), F32), pltpu.VMEM((t, V_HEAD), F32)],
        ),
        out_shape=[jax.ShapeDtypeStruct((S, MLA_HEADS * V_HEAD), F32), jax.ShapeDtypeStruct((MLA_HEADS, S, 128), F32)],
        compiler_params=pltpu.CompilerParams(dimension_semantics=("parallel", "arbitrary")),
    )(qi_tab, kj_tab, q_att, k_att, kv)


def _flash_delta(do, o, name, tr=512):
    S = o.shape[0]

    def body(do_ref, o_ref, d_ref):
        lane = lax.broadcasted_iota(jnp.int32, (tr, 128), 1)
        acc = jnp.zeros((tr, 128), F32)
        for h in range(MLA_HEADS):
            sl = slice(h * V_HEAD, (h + 1) * V_HEAD)
            acc = jnp.where(lane == h, jnp.sum(do_ref[:, sl].astype(F32) * o_ref[:, sl], axis=1, keepdims=True), acc)
        d_ref[...] = acc

    row = pl.BlockSpec((tr, MLA_HEADS * V_HEAD), lambda i: (i, 0))
    return pl.pallas_call(
        body,
        name=name,
        grid=(S // tr,),
        in_specs=[row, row],
        out_specs=pl.BlockSpec((tr, 128), lambda i: (i, 0)),
        out_shape=jax.ShapeDtypeStruct((S, 128), F32),
        compiler_params=pltpu.CompilerParams(dimension_semantics=("parallel",)),
    )(do, o)


def _flash2_bwd(q_att, k_att, kv, do, lse_row, delta_row, name, t=FLASH_T):
    S = q_att.shape[0]
    n = S // t
    qi_tab, kj_tab = _causal_pairs(n, by_key=True)
    last = qi_tab.shape[0] - 1
    half = t // 2

    def body(qi_ref, kj_ref, q_ref, k_ref, v_ref, do_ref, lse_ref, dl_ref, dq_ref, dk_ref, dv_ref, dk_sc, dv_sc):
        step = pl.program_id(1)
        qi, kj = qi_ref[step], kj_ref[step]

        @pl.when(step == 0)
        def _():
            dq_ref[...] = jnp.zeros((S, HEAD_PAD), F32)

        def update(k0, q0, st):
            nk, nq = st.shape
            kr, qr = slice(k0, k0 + nk), slice(q0, q0 + nq)
            q, do_v = q_ref[qr, :], do_ref[qr, :]
            pt = jnp.exp2(st * MLA_C2 - lse_ref[0][:, qr] * LOG2E)
            dv_sc[kr, :] += _dot(pt.astype(BF16), do_v)
            dpt = _dot(v_ref[kr, :], do_v, NT)
            dst = (pt * (dpt - dl_ref[0][:, qr])).astype(BF16)
            dk_sc[kr, :] += _dot(dst, q)
            rows = pl.ds(pl.multiple_of(qi * t + q0, half), nq)
            dq_ref[rows, :] += _dot(dst, k_ref[kr, :], TN)

        @pl.when(qi == kj)
        def _():
            dk_sc[...] = jnp.zeros((t, HEAD_PAD), F32)
            dv_sc[...] = jnp.zeros((t, V_HEAD), F32)
            top = _dot(k_ref[:half, :], q_ref[...], NT)
            update(0, 0, jnp.where(_causal_mask(top.shape, 0, keys_first=True), top, NEG))
            bot = _dot(k_ref[half:, :], q_ref[half:, :], NT)
            update(half, half, jnp.where(_causal_mask(bot.shape, 0, keys_first=True), bot, NEG))

        @pl.when(qi > kj)
        def _():
            update(0, 0, _dot(k_ref[...], q_ref[...], NT))

        @pl.when(qi == n - 1)
        def _():
            dk_ref[...] = dk_sc[...] * MLA_SCALE
            dv_ref[...] = dv_sc[...]

        @pl.when(step == last)
        def _():
            dq_ref[...] = dq_ref[...] * MLA_SCALE

    qrow = lambda h, p, qi, kj: (qi[p], h)
    krow = lambda h, p, qi, kj: (kj[p], h)
    stat = pl.BlockSpec((1, 1, t), lambda h, p, qi, kj: (h, 0, qi[p]))
    return pl.pallas_call(
        body,
        name=name,
        grid_spec=_flash_grid(
            qi_tab.shape[0],
            [
                pl.BlockSpec((t, HEAD_PAD), qrow),
                pl.BlockSpec((t, HEAD_PAD), krow),
                pl.BlockSpec((t, V_HEAD), lambda h, p, qi, kj: (kj[p], 2 * h + 1)),
                pl.BlockSpec((t, V_HEAD), qrow),
                stat,
                stat,
            ],
            [
                pl.BlockSpec((S, HEAD_PAD), lambda h, p, qi, kj: (0, h)),
                pl.BlockSpec((t, HEAD_PAD), krow),
                pl.BlockSpec((t, V_HEAD), krow),
            ],
            [pltpu.VMEM((t, HEAD_PAD), F32), pltpu.VMEM((t, V_HEAD), F32)],
        ),
        out_shape=[
            jax.ShapeDtypeStruct((S, MLA_HEADS * HEAD_PAD), F32),
            jax.ShapeDtypeStruct((S, MLA_HEADS * HEAD_PAD), F32),
            jax.ShapeDtypeStruct((S, MLA_HEADS * V_HEAD), F32),
        ],
        compiler_params=pltpu.CompilerParams(dimension_semantics=("parallel", "arbitrary")),
    )(qi_tab, kj_tab, q_att, k_att, kv, do, lse_row, delta_row)


DIL_SCALE = DIL_HD**-0.5


def _dil_bias():
    slopes = 2.0 ** (-ALIBI_MAX_BIAS * np.arange(1, DIL_HEADS + 1, dtype=np.float64) / DIL_HEADS)
    slopes = slopes.astype(np.float32).reshape(DIL_GROUPS, DIL_HG)
    p = np.arange(DIL_BLK)[:, None]
    kidx = np.arange(2 * DIL_BLK)[None, :]
    j = p + DIL_BLK - kidx
    out = np.zeros((DIL_GROUPS, DIL_HG, DIL_BLK, 2 * DIL_BLK), np.float32)
    for g, (window, dil) in enumerate(DIL_PATTERNS):
        valid = (j >= 0) & (j <= window // dil)
        for h in range(DIL_HG):
            alibi = -slopes[g, h] * (dil * j).astype(np.float32)
            out[g, h] = np.where(valid, alibi, np.float32(NEG))
    return jnp.asarray(out)


DIL_UNROLL = 4


def _unrolled_loop(lo, hi, fn, unroll=DIL_UNROLL):
    groups = (hi - lo) // unroll
    done = lo
    if groups > 1:

        def step(i, carry):
            for u in range(unroll):
                fn(lo + i * unroll + u)
            return carry

        lax.fori_loop(0, groups, step, 0)
        done = lo + groups * unroll
    for n in range(done, hi):
        fn(n)


def _dil_rows(r, n, count, dil):
    if dil == 1:
        if isinstance(n, int):
            return slice(n * DIL_BLK, (n + count) * DIL_BLK)
        return pl.ds(pl.multiple_of(n * DIL_BLK, DIL_BLK), count * DIL_BLK)
    return pl.ds(n * DIL_BLK * dil + r, count * DIL_BLK, stride=dil)


def _dil_each_block(S, dil, block):
    nb = S // dil // DIL_BLK
    if dil == 1:
        block(0, 0, True)
        _unrolled_loop(1, nb, lambda n: block(0, n, False))
    else:
        for r in range(dil):
            for n in range(nb):
                block(r, n, n == 0)


def _dil_col(g, part, h):
    return (g * 3 + part) * DIL_HG + h


def _dil_fwd_group(dqkv, bias_g, g, dil, name):
    S = dqkv.shape[0]

    def body(bias_ref, q_ref, k_ref, v_ref, o_ref, lse_ref):
        def block(r, n, first):
            cur = _dil_rows(r, n, 1, dil)
            both = cur if first else _dil_rows(r, n - 1, 2, dil)
            b = bias_ref[0][:, DIL_BLK:] if first else bias_ref[0]
            q, kk, vv = q_ref[cur, :].astype(BF16), k_ref[both, :].astype(BF16), v_ref[both, :].astype(BF16)
            s = _dot(q, kk, NT) * DIL_SCALE + b
            m = jnp.max(s, axis=1, keepdims=True)
            e = jnp.exp(s - m)
            l = jnp.sum(e, axis=1, keepdims=True)
            p = e * (1.0 / l)
            o_ref[cur, :] = _dot(p.astype(BF16), vv)
            lse_ref[cur, :] = jnp.broadcast_to(m + jnp.log(l), (DIL_BLK, 128))

        _dil_each_block(S, dil, block)

    def col(part):
        return pl.BlockSpec((S, DIL_HD), lambda h: (0, _dil_col(g, part, h)))

    out = pl.BlockSpec((S, DIL_HD), lambda h: (0, h))
    return pl.pallas_call(
        body,
        name=name,
        grid=(DIL_HG,),
        in_specs=[pl.BlockSpec((1, DIL_BLK, 2 * DIL_BLK), lambda h: (h, 0, 0)), col(0), col(1), col(2)],
        out_specs=[out, out],
        out_shape=[jax.ShapeDtypeStruct((S, DIL_OUT), F32), jax.ShapeDtypeStruct((S, DIL_OUT), F32)],
        compiler_params=pltpu.CompilerParams(dimension_semantics=("parallel",)),
    )(bias_g, dqkv, dqkv, dqkv)


def _dil_combine(os_, ls_, name, tr=512):
    S = os_[0].shape[0]

    def body(o0, o1, o2, l0, l1, l2, out_ref, lse_ref):
        a, b, c = l0[...], l1[...], l2[...]
        m = jnp.maximum(jnp.maximum(a, b), c)
        ea, eb, ec = jnp.exp(a - m), jnp.exp(b - m), jnp.exp(c - m)
        den = ea + eb + ec
        inv = 1.0 / den
        out_ref[...] = (ea * inv) * o0[...] + (eb * inv) * o1[...] + (ec * inv) * o2[...]
        lse_ref[...] = m + jnp.log(den)

    row = pl.BlockSpec((tr, DIL_OUT), lambda i: (i, 0))
    return pl.pallas_call(
        body,
        name=name,
        grid=(S // tr,),
        in_specs=[row] * 6,
        out_specs=[row, row],
        out_shape=[jax.ShapeDtypeStruct((S, DIL_OUT), F32)] * 2,
        compiler_params=pltpu.CompilerParams(dimension_semantics=("parallel",)),
    )(*os_, *ls_)


def _dil_rowdot(dod, od, name, tr=512):
    S = dod.shape[0]

    def body(d_ref, o_ref, dd_ref):
        for h in range(DIL_HG):
            sl = slice(h * 128, (h + 1) * 128)
            sm = jnp.sum(d_ref[:, sl] * o_ref[:, sl], axis=1, keepdims=True)
            dd_ref[:, sl] = jnp.broadcast_to(sm, (tr, 128))

    row = pl.BlockSpec((tr, DIL_OUT), lambda i: (i, 0))
    return pl.pallas_call(
        body,
        name=name,
        grid=(S // tr,),
        in_specs=[row, row],
        out_specs=row,
        out_shape=jax.ShapeDtypeStruct((S, DIL_OUT), F32),
        compiler_params=pltpu.CompilerParams(dimension_semantics=("parallel",)),
    )(dod, od)


def _dil_bwd_group(dqkv, bias_g, dod, dd, lse, grads, g, dil, name):
    S = dqkv.shape[0]

    def body(bias_ref, q_ref, k_ref, v_ref, do_ref, dd_ref, lse_ref, _, out_ref):
        out_ref[1] = jnp.zeros((S, DIL_HD), F32)
        out_ref[2] = jnp.zeros((S, DIL_HD), F32)

        def block(r, n, first):
            cur = _dil_rows(r, n, 1, dil)
            both = cur if first else _dil_rows(r, n - 1, 2, dil)
            b = bias_ref[0][:, DIL_BLK:] if first else bias_ref[0]
            q, kk, vv = q_ref[cur, :].astype(BF16), k_ref[both, :].astype(BF16), v_ref[both, :].astype(BF16)
            do = do_ref[cur, :].astype(BF16)
            s = _dot(q, kk, NT) * DIL_SCALE + b
            p = jnp.exp(s - lse_ref[cur, 0:1])
            dp = _dot(do, vv, NT)
            ds = ((p * (dp - dd_ref[cur, 0:1])) * DIL_SCALE).astype(BF16)
            out_ref[0, cur, :] = _dot(ds, kk)
            out_ref[1, both, :] += _dot(ds, q, TN)
            out_ref[2, both, :] += _dot(p.astype(BF16), do, TN)

        _dil_each_block(S, dil, block)

    def col(part):
        return pl.BlockSpec((S, DIL_HD), lambda h: (0, _dil_col(g, part, h)))

    nat = pl.BlockSpec((S, DIL_HD), lambda h: (0, h))
    return pl.pallas_call(
        body,
        name=name,
        grid=(DIL_HG,),
        in_specs=[pl.BlockSpec((1, DIL_BLK, 2 * DIL_BLK), lambda h: (h, 0, 0)), col(0), col(1), col(2), nat, nat, nat, ANY],
        out_specs=pl.BlockSpec((3, S, DIL_HD), lambda h: (g, 0, h)),
        out_shape=jax.ShapeDtypeStruct(grads.shape, F32),
        input_output_aliases={7: 0},
        compiler_params=pltpu.CompilerParams(dimension_semantics=("parallel",)),
    )(bias_g, dqkv, dqkv, dqkv, dod, dd, lse, grads)


def _merge_fwd(gates, o_a, o_b, name, tr=256):
    S = o_a.shape[0]

    def body(ga_ref, gb_ref, oa_ref, ob_ref, m_ref):
        m_ref[...] = (ga_ref[...] * oa_ref[...] + gb_ref[...] * ob_ref[...]).astype(BF16)

    row = pl.BlockSpec((tr, D_MODEL), lambda i: (i, 0))
    return pl.pallas_call(
        body,
        name=name,
        grid=(S // tr,),
        in_specs=[row, pl.BlockSpec((tr, D_MODEL), lambda i: (i, 1)), row, row],
        out_specs=row,
        out_shape=jax.ShapeDtypeStruct((S, D_MODEL), BF16),
        compiler_params=pltpu.CompilerParams(dimension_semantics=("parallel",)),
    )(gates, gates, o_a, o_b)


def _merge_bwd(dmrg, gates, o_a, o_b, name, tr=256):
    S = o_a.shape[0]

    def body(dm_ref, ga_ref, gb_ref, oa_ref, ob_ref, doa_ref, dob_ref, dga_ref, dgb_ref, dba_ref, dbb_ref):
        dm, ga, gb = dm_ref[...], ga_ref[...], gb_ref[...]
        doa_ref[...] = (dm * ga).astype(BF16)
        dob_ref[...] = (dm * gb).astype(BF16)
        dga = (dm * oa_ref[...]) * (ga * (1.0 - ga))
        dgb = (dm * ob_ref[...]) * (gb * (1.0 - gb))
        dga_ref[...] = dga.astype(BF16)
        dgb_ref[...] = dgb.astype(BF16)
        sa = jnp.sum(dga, axis=0, keepdims=True)
        sb = jnp.sum(dgb, axis=0, keepdims=True)

        @pl.when(pl.program_id(0) == 0)
        def _():
            dba_ref[...] = sa
            dbb_ref[...] = sb

        @pl.when(pl.program_id(0) != 0)
        def _():
            dba_ref[...] += sa
            dbb_ref[...] += sb

    row = pl.BlockSpec((tr, D_MODEL), lambda i: (i, 0))
    row1 = pl.BlockSpec((tr, D_MODEL), lambda i: (i, 1))
    vec = pl.BlockSpec((1, D_MODEL), lambda i: (0, 0))
    outs = pl.pallas_call(
        body,
        name=name,
        grid=(S // tr,),
        in_specs=[row, row, row1, row, row],
        out_specs=[row, row, row, row, vec, vec],
        out_shape=[jax.ShapeDtypeStruct((S, D_MODEL), BF16)] * 4 + [jax.ShapeDtypeStruct((1, D_MODEL), F32)] * 2,
        compiler_params=pltpu.CompilerParams(dimension_semantics=("arbitrary",)),
    )(dmrg, gates, gates, o_a, o_b)
    return outs


CONV_TR = 512
CONV_TC = 512
N_FFC = D_FF_PAD // CONV_TC


def _conv_taps(x, before, w_ref, b_ref):
    x0 = jnp.concatenate([before, x], axis=0)
    x1 = pltpu.roll(x0, 1, 0)
    x2 = pltpu.roll(x0, 2, 0)
    u = ((b_ref[...] + w_ref[0:1, :] * x2) + w_ref[1:2, :] * x1) + w_ref[2:3, :] * x0
    return u, x0, x1, x2


def _prev_halo(tr):
    return lambda i, j: (jnp.maximum(i * (tr // 8) - 1, 0), j)


def _ffn_fwd(u0, cw, cb, name):
    S = u0.shape[0]
    tr, tc = CONV_TR, CONV_TC

    def body(up_ref, gt_ref, hup_ref, hgt_ref, wu_ref, wg_ref, bu_ref, bg_ref, a_ref):
        live = (pl.program_id(0) > 0).astype(F32)
        up = _conv_taps(up_ref[...], hup_ref[...] * live, wu_ref, bu_ref)[0][8:]
        gt = _conv_taps(gt_ref[...], hgt_ref[...] * live, wg_ref, bg_ref)[0][8:]
        a_ref[...] = ((gt * jax.nn.sigmoid(gt)) * up).astype(BF16)

    return pl.pallas_call(
        body,
        name=name,
        grid=(S // tr, N_FFC),
        in_specs=[
            pl.BlockSpec((tr, tc), lambda i, j: (i, j)),
            pl.BlockSpec((tr, tc), lambda i, j: (i, j + N_FFC)),
            pl.BlockSpec((8, tc), _prev_halo(tr)),
            pl.BlockSpec((8, tc), lambda i, j: (jnp.maximum(i * (tr // 8) - 1, 0), j + N_FFC)),
            pl.BlockSpec((8, tc), lambda i, j: (0, j)),
            pl.BlockSpec((8, tc), lambda i, j: (0, j + N_FFC)),
            pl.BlockSpec((1, tc), lambda i, j: (0, j)),
            pl.BlockSpec((1, tc), lambda i, j: (0, j + N_FFC)),
        ],
        out_specs=pl.BlockSpec((tr, tc), lambda i, j: (i, j)),
        out_shape=jax.ShapeDtypeStruct((S, D_FF_PAD), BF16),
        compiler_params=pltpu.CompilerParams(dimension_semantics=("parallel", "parallel")),
    )(u0, u0, u0, u0, cw, cw, cb, cb)


def _ffn_bwd(u0, da, cw, cb, name):
    S = u0.shape[0]
    tr, tc = CONV_TR, CONV_TC
    nrow, te = S // tr, tr + 8

    def body(up_ref, gt_ref, hup_ref, hgt_ref, nup_ref, ngt_ref, da_ref, nda_ref, wu_ref, wg_ref, bu_ref, bg_ref, du0_ref, dcw_ref, dcb_ref):
        i = pl.program_id(1)
        prev_live = (i > 0).astype(F32)
        next_live = (i < nrow - 1).astype(F32)

        def conv(x_ref, nx_ref, h_ref, w_ref, b_ref):
            x = jnp.concatenate([x_ref[...], nx_ref[...] * next_live], axis=0)
            return [t[8:] for t in _conv_taps(x, h_ref[...] * prev_live, w_ref, b_ref)]

        up, xu0, xu1, xu2 = conv(up_ref, nup_ref, hup_ref, wu_ref, bu_ref)
        gt, xg0, xg1, xg2 = conv(gt_ref, ngt_ref, hgt_ref, wg_ref, bg_ref)
        da_v = jnp.concatenate([da_ref[...], nda_ref[...] * next_live], axis=0)
        sg = jax.nn.sigmoid(gt)
        d_up = da_v * (gt * sg)
        d_gt = (da_v * up) * (sg * (1.0 + gt * (1.0 - sg)))
        tap = lax.broadcasted_iota(jnp.int32, (8, tc), 0)

        def finish(half, du, x0, x1, x2, w_ref):
            n1 = pltpu.roll(du, te - 1, 0)
            n2 = pltpu.roll(du, te - 2, 0)
            du0 = (w_ref[2:3, :] * du + w_ref[1:2, :] * n1) + w_ref[0:1, :] * n2
            du0_ref[half] = du0[:tr].astype(BF16)
            d = du[:tr]
            dcw = jnp.where(
                tap == 0,
                jnp.sum(d * x2[:tr], axis=0, keepdims=True),
                jnp.where(tap == 1, jnp.sum(d * x1[:tr], axis=0, keepdims=True), jnp.where(tap == 2, jnp.sum(d * x0[:tr], axis=0, keepdims=True), 0.0)),
            )
            dcb = jnp.sum(d, axis=0, keepdims=True)

            @pl.when(i == 0)
            def _():
                dcw_ref[half] = dcw
                dcb_ref[half] = dcb

            @pl.when(i != 0)
            def _():
                dcw_ref[half] += dcw
                dcb_ref[half] += dcb

        finish(0, d_up, xu0, xu1, xu2, wu_ref)
        finish(1, d_gt, xg0, xg1, xg2, wg_ref)

    def prev8(off):
        return pl.BlockSpec((8, tc), lambda j, i: (jnp.maximum(i * (tr // 8) - 1, 0), j + off))

    def next8(off):
        return pl.BlockSpec((8, tc), lambda j, i: (jnp.minimum((i + 1) * (tr // 8), S // 8 - 1), j + off))

    return pl.pallas_call(
        body,
        name=name,
        grid=(N_FFC, nrow),
        in_specs=[
            pl.BlockSpec((tr, tc), lambda j, i: (i, j)),
            pl.BlockSpec((tr, tc), lambda j, i: (i, j + N_FFC)),
            prev8(0),
            prev8(N_FFC),
            next8(0),
            next8(N_FFC),
            pl.BlockSpec((tr, tc), lambda j, i: (i, j)),
            next8(0),
            pl.BlockSpec((8, tc), lambda j, i: (0, j)),
            pl.BlockSpec((8, tc), lambda j, i: (0, j + N_FFC)),
            pl.BlockSpec((1, tc), lambda j, i: (0, j)),
            pl.BlockSpec((1, tc), lambda j, i: (0, j + N_FFC)),
        ],
        out_specs=[
            pl.BlockSpec((2, tr, tc), lambda j, i: (0, i, j)),
            pl.BlockSpec((2, 8, tc), lambda j, i: (0, 0, j)),
            pl.BlockSpec((2, 1, tc), lambda j, i: (0, 0, j)),
        ],
        out_shape=[
            jax.ShapeDtypeStruct((2, S, D_FF_PAD), BF16),
            jax.ShapeDtypeStruct((2, 8, D_FF_PAD), F32),
            jax.ShapeDtypeStruct((2, 1, D_FF_PAD), F32),
        ],
        compiler_params=pltpu.CompilerParams(dimension_semantics=("parallel", "arbitrary")),
    )(u0, u0, u0, u0, u0, u0, da, da, cw, cw, cb, cb)


ADAMW_BLOCK_BYTES = 3 << 20


def _adamw(w, g, m, v, name):
    R, C = w.shape
    fits = [t for t in range(8, R + 1, 8) if R % t == 0 and t * C * 4 <= ADAMW_BLOCK_BYTES]
    tr = max(fits) if fits else R

    def body(w_ref, g_ref, m_ref, v_ref, d_ref, nm_ref, nv_ref):
        gv = g_ref[...]
        nm = ADAM_B1 * m_ref[...] + (1.0 - ADAM_B1) * gv
        nv = ADAM_B2 * v_ref[...] + (1.0 - ADAM_B2) * (gv * gv)
        m_hat = nm / (1.0 - ADAM_B1**ADAM_STEP)
        v_hat = nv / (1.0 - ADAM_B2**ADAM_STEP)
        d_ref[...] = -ADAM_LR * (m_hat / (jnp.sqrt(v_hat) + ADAM_EPS) + ADAM_WD * w_ref[...])
        nm_ref[...] = nm
        nv_ref[...] = nv

    blk = pl.BlockSpec((tr, C), lambda i: (i, 0))
    return pl.pallas_call(
        body,
        name=name,
        grid=(R // tr,),
        in_specs=[blk] * 4,
        out_specs=[blk] * 3,
        out_shape=[jax.ShapeDtypeStruct((R, C), F32)] * 3,
        compiler_params=pltpu.CompilerParams(dimension_semantics=("parallel",)),
    )(w, g, m, v)


ANY = pl.BlockSpec(memory_space=pl.ANY)


def _row_tile(rows):
    return max(t for t in range(16, 353, 16) if rows % t == 0)


def _pair_add(g, recv, core, name):
    _, R, C = g.shape
    tr = _row_tile(R)

    def body(core_ref, g_ref, r_ref, o_ref):
        o_ref[...] = (g_ref[...].astype(F32) + r_ref[...].astype(F32)).astype(o_ref.dtype)

    return pl.pallas_call(
        body,
        name=name,
        grid_spec=pltpu.PrefetchScalarGridSpec(
            num_scalar_prefetch=1,
            grid=(N_CHIP, R // tr),
            in_specs=[
                pl.BlockSpec((1, tr, C), lambda k, i, core_ref: (2 * k + core_ref[0], i, 0)),
                pl.BlockSpec((1, tr, C), lambda k, i, core_ref: (k, i, 0)),
            ],
            out_specs=pl.BlockSpec((1, tr, C), lambda k, i, core_ref: (k, i, 0)),
        ),
        out_shape=jax.ShapeDtypeStruct((N_CHIP, R, C), g.dtype),
        compiler_params=pltpu.CompilerParams(dimension_semantics=("parallel", "parallel")),
    )(core, g, recv)


HBM = pl.BlockSpec(memory_space=pltpu.HBM)
SEM = pl.BlockSpec(memory_space=pltpu.SEMAPHORE)
EFFECT = pltpu.SideEffectType.DATAFLOW_SIDE_EFFECTING
RELATIONS = tuple((dx, dy, dc) for dx in (0, 1) for dy in (0, 1) for dc in (0, 1))[1:]


def _related(rel):
    x, y, c = lax.axis_index("x"), lax.axis_index("y"), lax.axis_index("c")
    return (1 - x if rel[0] else x, 1 - y if rel[1] else y, 1 - c if rel[2] else c)


def _dev_index(pos):
    return 4 * pos[0] + 2 * pos[1] + pos[2]


def _peers(chips):
    if chips:
        return [r for r in RELATIONS if not r[2]], N_CHIP, lambda pos: 2 * pos[0] + pos[1]
    return list(RELATIONS), N_DEV, _dev_index


def _exchange_start(srcs, by_slot, after, name, chips=False):
    n = len(srcs)
    extra = [] if after is None else [after]
    rels, slots, slot_of = _peers(chips)
    lands = [lax.empty((slots,) + (s.shape[1:] if by_slot else s.shape), s.dtype) for s in srcs]
    nsem = len(rels) * n

    def body(*refs):
        src_refs, land_refs = refs[:n], refs[n : 2 * n]
        send_sems, recv_sems = refs[2 * n + len(extra)], refs[2 * n + len(extra) + 1]
        token = refs[-1]
        me = slot_of(_related((0, 0, 0)))
        for a in range(n):
            for k, rel in enumerate(rels):
                peer = _related(rel)
                pltpu.make_async_remote_copy(
                    src_ref=src_refs[a].at[slot_of(peer)] if by_slot else src_refs[a],
                    dst_ref=land_refs[a].at[me],
                    send_sem=send_sems.at[len(rels) * a + k],
                    recv_sem=recv_sems.at[len(rels) * a + k],
                    device_id=peer,
                    device_id_type=MESH,
                ).start()
        token[...] = jnp.zeros_like(token)

    def hbm(a):
        return pltpu.HBM(a.shape, a.dtype)

    outs = pl.pallas_call(
        body,
        name=name,
        out_shape=(
            pltpu.SemaphoreType.DMA((nsem,)),
            pltpu.SemaphoreType.DMA((nsem,)),
            *[hbm(s) for s in srcs],
            *[hbm(l) for l in lands],
            jax.ShapeDtypeStruct((8, 128), F32),
        ),
        in_specs=[HBM] * (2 * n) + [ANY] * len(extra),
        out_specs=(SEM, SEM, *[HBM] * (2 * n), pl.BlockSpec(memory_space=pltpu.VMEM)),
        input_output_aliases={i: 2 + i for i in range(2 * n)},
        compiler_params=pltpu.CompilerParams(has_side_effects=EFFECT),
    )(*[pltpu.with_memory_space_constraint(a, pltpu.HBM) for a in list(srcs) + lands], *extra)
    return (outs[0], outs[1], list(outs[2 : 2 + n]), list(outs[2 + n : 2 + 2 * n])), outs[-1]


def _exchange_wait(handle, by_slot, after, name, chips=False):
    send_sems, recv_sems, srcs, lands = handle
    n = len(srcs)
    rels = _peers(chips)[0]

    def body(*refs):
        src_refs, land_refs = refs[:n], refs[n : 2 * n]
        s_sems, r_sems = refs[2 * n], refs[2 * n + 1]
        for a in range(n):
            for k, rel in enumerate(rels):
                copy = pltpu.make_async_remote_copy(
                    src_ref=src_refs[a].at[0] if by_slot else src_refs[a],
                    dst_ref=land_refs[a].at[0],
                    send_sem=s_sems.at[len(rels) * a + k],
                    recv_sem=r_sems.at[len(rels) * a + k],
                    device_id=_related(rel),
                    device_id_type=MESH,
                )
                copy.wait_send()
                copy.wait_recv()

    outs = pl.pallas_call(
        body,
        name=name,
        out_shape=tuple(pltpu.HBM(a.shape, a.dtype) for a in srcs + lands),
        in_specs=[HBM] * (2 * n) + [SEM, SEM, ANY],
        out_specs=tuple([HBM] * (2 * n)),
        input_output_aliases={i: i for i in range(2 * n)},
        compiler_params=pltpu.CompilerParams(has_side_effects=EFFECT),
    )(*srcs, *lands, send_sems, recv_sems, after)
    return list(outs[:n]), list(outs[n:])


def _pair_start(g, name):
    land = lax.empty((N_CHIP,) + g.shape[1:], g.dtype)

    def body(g_ref, land_ref, send_sems, recv_sems, g_thru, land_thru, token):
        c = lax.axis_index("c")
        for k in range(N_CHIP):
            pltpu.make_async_remote_copy(
                src_ref=g_ref.at[2 * k + (1 - c)],
                dst_ref=land_ref.at[k],
                send_sem=send_sems.at[k],
                recv_sem=recv_sems.at[k],
                device_id=_related((0, 0, 1)),
                device_id_type=MESH,
            ).start()
        token[...] = jnp.zeros_like(token)

    outs = pl.pallas_call(
        body,
        name=name,
        out_shape=(
            pltpu.SemaphoreType.DMA((N_CHIP,)),
            pltpu.SemaphoreType.DMA((N_CHIP,)),
            pltpu.HBM(g.shape, g.dtype),
            pltpu.HBM(land.shape, land.dtype),
            jax.ShapeDtypeStruct((8, 128), F32),
        ),
        in_specs=[HBM, HBM],
        out_specs=(SEM, SEM, HBM, HBM, pl.BlockSpec(memory_space=pltpu.VMEM)),
        input_output_aliases={0: 2, 1: 3},
        compiler_params=pltpu.CompilerParams(has_side_effects=EFFECT),
    )(pltpu.with_memory_space_constraint(g, pltpu.HBM), pltpu.with_memory_space_constraint(land, pltpu.HBM))
    return outs[:4], outs[4]


def _pair_wait(handle, after, name):
    send_sems, recv_sems, g, land = handle

    def body(g_ref, land_ref, s_sems, r_sems, _, g_out, land_out):
        for k in range(N_CHIP):
            copy = pltpu.make_async_remote_copy(
                src_ref=g_ref.at[0],
                dst_ref=land_ref.at[0],
                send_sem=s_sems.at[k],
                recv_sem=r_sems.at[k],
                device_id=_related((0, 0, 1)),
                device_id_type=MESH,
            )
            copy.wait_send()
            copy.wait_recv()

    return pl.pallas_call(
        body,
        name=name,
        out_shape=(pltpu.HBM(g.shape, g.dtype), pltpu.HBM(land.shape, land.dtype)),
        in_specs=[HBM, HBM, SEM, SEM, ANY],
        out_specs=(HBM, HBM),
        input_output_aliases={0: 0, 1: 1},
        compiler_params=pltpu.CompilerParams(has_side_effects=EFFECT),
    )(g, land, send_sems, recv_sems, after)


NEAR = ((0, 0, 1), (1, 0, 0), (0, 1, 0), (1, 1, 0))


def _gather2_start(blocks, name):
    n = len(blocks)
    lands = [lax.empty((N_DEV,) + b.shape, b.dtype) for b in blocks]

    def body(*refs):
        src_refs, land_refs = refs[:n], refs[n : 2 * n]
        send_sems, recv_sems, token = refs[2 * n], refs[2 * n + 1], refs[-1]
        me = _dev_index(_related((0, 0, 0)))
        for a in range(n):
            for k, rel in enumerate(NEAR):
                pltpu.make_async_remote_copy(
                    src_ref=src_refs[a],
                    dst_ref=land_refs[a].at[me],
                    send_sem=send_sems.at[len(NEAR) * a + k],
                    recv_sem=recv_sems.at[len(NEAR) * a + k],
                    device_id=_related(rel),
                    device_id_type=MESH,
                ).start()
        token[...] = jnp.zeros_like(token)

    nsem = len(NEAR) * n
    outs = pl.pallas_call(
        body,
        name=name,
        out_shape=(
            pltpu.SemaphoreType.DMA((nsem,)),
            pltpu.SemaphoreType.DMA((nsem,)),
            *[pltpu.HBM(a.shape, a.dtype) for a in list(blocks) + lands],
            jax.ShapeDtypeStruct((8, 128), F32),
        ),
        in_specs=[HBM] * (2 * n),
        out_specs=(SEM, SEM, *[HBM] * (2 * n), pl.BlockSpec(memory_space=pltpu.VMEM)),
        input_output_aliases={i: 2 + i for i in range(2 * n)},
        compiler_params=pltpu.CompilerParams(has_side_effects=EFFECT),
    )(*[pltpu.with_memory_space_constraint(a, pltpu.HBM) for a in list(blocks) + lands])
    return (outs[0], outs[1], list(outs[2 : 2 + n]), list(outs[2 + n : 2 + 2 * n])), outs[-1]


def _gather2_forward(handle, after, name):
    send1, recv1, srcs, lands = handle
    n = len(srcs)

    def body(*refs):
        src_refs, land_refs = refs[:n], refs[n : 2 * n]
        s1, r1 = refs[2 * n], refs[2 * n + 1]
        s2, r2 = refs[-2], refs[-1]
        sibling = _related(NEAR[0])
        for a in range(n):
            for k, rel in enumerate(NEAR):
                first = pltpu.make_async_remote_copy(
                    src_ref=src_refs[a],
                    dst_ref=land_refs[a].at[0],
                    send_sem=s1.at[len(NEAR) * a + k],
                    recv_sem=r1.at[len(NEAR) * a + k],
                    device_id=_related(rel),
                    device_id_type=MESH,
                )
                first.wait_send()
                first.wait_recv()
                if k:
                    slot = land_refs[a].at[_dev_index(_related(rel))]
                    pltpu.make_async_remote_copy(
                        src_ref=slot,
                        dst_ref=slot,
                        send_sem=s2.at[3 * a + k - 1],
                        recv_sem=r2.at[3 * a + k - 1],
                        device_id=sibling,
                        device_id_type=MESH,
                    ).start()

    outs = pl.pallas_call(
        body,
        name=name,
        out_shape=(
            *[pltpu.HBM(a.shape, a.dtype) for a in srcs + lands],
            pltpu.SemaphoreType.DMA((3 * n,)),
            pltpu.SemaphoreType.DMA((3 * n,)),
        ),
        in_specs=[HBM] * (2 * n) + [SEM, SEM, ANY],
        out_specs=(*[HBM] * (2 * n), SEM, SEM),
        input_output_aliases={i: i for i in range(2 * n)},
        compiler_params=pltpu.CompilerParams(has_side_effects=EFFECT),
    )(*srcs, *lands, send1, recv1, after)
    return outs[-2], outs[-1], list(outs[:n]), list(outs[n : 2 * n])


def _gather2_wait(handle, name):
    send2, recv2, srcs, lands = handle
    n = len(srcs)

    def body(*refs):
        land_refs = refs[n : 2 * n]
        s2, r2 = refs[2 * n], refs[2 * n + 1]
        for a in range(n):
            for j in range(3):
                passed = pltpu.make_async_remote_copy(
                    src_ref=land_refs[a].at[0],
                    dst_ref=land_refs[a].at[0],
                    send_sem=s2.at[3 * a + j],
                    recv_sem=r2.at[3 * a + j],
                    device_id=_related(NEAR[0]),
                    device_id_type=MESH,
                )
                passed.wait_send()
                passed.wait_recv()

    outs = pl.pallas_call(
        body,
        name=name,
        out_shape=tuple(pltpu.HBM(a.shape, a.dtype) for a in srcs + lands),
        in_specs=[HBM] * (2 * n) + [SEM, SEM],
        out_specs=tuple([HBM] * (2 * n)),
        input_output_aliases={i: i for i in range(2 * n)},
        compiler_params=pltpu.CompilerParams(has_side_effects=EFFECT),
    )(*srcs, *lands, send2, recv2)
    return list(outs[:n]), list(outs[n:])


def _slot_sum(parts, name, rows=None):
    n, R, C = parts.shape
    if rows is not None:
        tc = 512

        def head(p_ref, o_ref):
            acc = p_ref[0].astype(F32)
            for k in range(1, n):
                acc = acc + p_ref[k].astype(F32)
            o_ref[...] = acc[:rows, :]

        return pl.pallas_call(
            head,
            name=name,
            grid=(C // tc,),
            in_specs=[pl.BlockSpec((n, R, tc), lambda j: (0, 0, j))],
            out_specs=pl.BlockSpec((rows, tc), lambda j: (0, j)),
            out_shape=jax.ShapeDtypeStruct((rows, C), F32),
            compiler_params=pltpu.CompilerParams(dimension_semantics=("parallel",)),
        )(parts)
    tr = _row_tile(R) if R % 16 == 0 else R

    def body(p_ref, o_ref):
        acc = p_ref[0].astype(F32)
        for k in range(1, n):
            acc = acc + p_ref[k].astype(F32)
        o_ref[...] = acc

    return pl.pallas_call(
        body,
        name=name,
        grid=(R // tr,),
        in_specs=[pl.BlockSpec((n, tr, C), lambda i: (0, i, 0))],
        out_specs=pl.BlockSpec((tr, C), lambda i: (i, 0)),
        out_shape=jax.ShapeDtypeStruct((R, C), F32),
        compiler_params=pltpu.CompilerParams(dimension_semantics=("parallel",)),
    )(parts)


W_IN_TC = 256
W_IN_BOUNDS = (0, LAT, LAT + 3 * DIL_QKV, LAT + 3 * DIL_QKV + D_MODEL, D_IN)


def _dqkv_chunks():
    return [((g * 3 + part) * DIL_OUT, LAT + part * DIL_QKV + g * DIL_OUT) for g in range(DIL_GROUPS) for part in range(3)]


def _w_in_regroup(slots, after, name):
    tc = W_IN_TC

    def body(s_ref, _, lat_ref, dqkv_ref, g_ref, buf):
        for j in range(N_DEV):
            buf[j * IN_ROWS : (j + 1) * IN_ROWS, :] = s_ref[j].astype(F32)[:IN_ROWS, :]
        lat_ref[:LAT, :] = buf[:LAT, :].astype(BF16)
        lat_ref[LAT:, :] = jnp.zeros((LAT_PAD - LAT, tc), BF16)
        for dst, src in _dqkv_chunks():
            dqkv_ref[dst : dst + DIL_OUT, :] = buf[src : src + DIL_OUT, :].astype(BF16)
        g_ref[...] = buf[W_IN_BOUNDS[2] :, :].astype(BF16)

    def col(rows):
        return pl.BlockSpec((rows, tc), lambda k: (0, k))

    return pl.pallas_call(
        body,
        name=name,
        grid=(D_MODEL // tc,),
        in_specs=[pl.BlockSpec((N_DEV, IN_ROWS_PAD, tc), lambda k: (0, 0, k)), pl.BlockSpec((8, 128), lambda k: (0, 0))],
        out_specs=[col(LAT_PAD), col(3 * DIL_QKV), col(2 * D_MODEL)],
        out_shape=[
            jax.ShapeDtypeStruct((LAT_PAD, D_MODEL), BF16),
            jax.ShapeDtypeStruct((3 * DIL_QKV, D_MODEL), BF16),
            jax.ShapeDtypeStruct((2 * D_MODEL, D_MODEL), BF16),
        ],
        scratch_shapes=[pltpu.VMEM((D_IN, tc), F32)],
        compiler_params=pltpu.CompilerParams(dimension_semantics=("parallel",)),
    )(slots, after)


def _w_in_grad_regroup(g_lat, g_dqkv, g_ga, g_gb, name):
    tc = W_IN_TC

    def body(lat_ref, dqkv_ref, ga_ref, gb_ref, o_ref, buf):
        b = W_IN_BOUNDS
        buf[b[0] : b[1], :] = lat_ref[:LAT, :].astype(F32)
        for dst, src in _dqkv_chunks():
            buf[src : src + DIL_OUT, :] = dqkv_ref[dst : dst + DIL_OUT, :].astype(F32)
        buf[b[2] : b[3], :] = ga_ref[...].astype(F32)
        buf[b[3] : b[4], :] = gb_ref[...].astype(F32)
        fill = jnp.zeros((IN_ROWS_PAD - IN_ROWS, tc), F32)
        for j in range(N_DEV):
            o_ref[j] = jnp.concatenate([buf[j * IN_ROWS : (j + 1) * IN_ROWS, :], fill], axis=0).astype(BF16)

    def col(rows):
        return pl.BlockSpec((rows, tc), lambda k: (0, k))

    return pl.pallas_call(
        body,
        name=name,
        grid=(D_MODEL // tc,),
        in_specs=[col(LAT_PAD), col(3 * DIL_QKV), col(D_MODEL), col(D_MODEL)],
        out_specs=pl.BlockSpec((N_DEV, IN_ROWS_PAD, tc), lambda k: (0, 0, k)),
        out_shape=jax.ShapeDtypeStruct((N_DEV, IN_ROWS_PAD, D_MODEL), BF16),
        scratch_shapes=[pltpu.VMEM((D_IN, tc), F32)],
        compiler_params=pltpu.CompilerParams(dimension_semantics=("parallel",)),
    )(g_lat, g_dqkv, g_ga, g_gb)


def _ffn_pad(a, axis):
    a = jnp.moveaxis(a, axis, -1)
    g = a.reshape(a.shape[:-1] + (2 * N_DEV, FF_GROUP))
    g = jnp.pad(g, [(0, 0)] * (g.ndim - 1) + [(0, FF_GROUP_PAD - FF_GROUP)])
    return jnp.moveaxis(g.reshape(a.shape[:-1] + (2 * D_FF_PAD,)), -1, axis)


def _ffn_unpad(a, axis):
    a = jnp.moveaxis(a, axis, -1)
    g = a.reshape(a.shape[:-1] + (2 * N_DEV, FF_GROUP_PAD))[..., :FF_GROUP]
    return jnp.moveaxis(g.reshape(a.shape[:-1] + (2 * D_FF,)), -1, axis)


MISC = (("w_o_mla", (256, 1024)), ("w_o_dil", (256, 512)), ("w_uq", (192, 512)), ("w_ukv", (256, 256)))
BIG_WEIGHTS = ("w_in", "w_up", "w_down", "w_out") + tuple(n for n, _ in MISC)


def _exchange_blocks(w):
    def t(a):
        return a.astype(BF16).T

    up = t(w["w_up"]).reshape(2, FF_GROUP, D_MODEL)
    return [
        jnp.pad(t(w["w_in"]), ((0, IN_ROWS_PAD - IN_ROWS), (0, 0))),
        jnp.pad(up, ((0, 0), (0, FF_GROUP_PAD - FF_GROUP), (0, 0))).reshape(2 * FF_GROUP_PAD, D_MODEL),
        jnp.pad(w["w_down"].astype(BF16), ((0, FF_GROUP_PAD - FF_GROUP), (0, 0))),
        w["w_out"].astype(BF16),
        jnp.concatenate([t(w[n]).reshape(-1, D_MODEL) for n, _ in MISC], axis=0),
    ]


def _misc_split(misc):
    out, off = {}, 0
    for n, (r, c) in MISC:
        rows = r * c // D_MODEL
        out[n] = misc[..., off : off + rows, :].reshape(misc.shape[:-2] + (r, c))
        off += rows
    return out


def _small_matrices(g_misc):
    misc = _misc_split(g_misc)
    uq_t = jnp.pad(misc["w_uq"], ((0, 0), (0, HEAD_PAD - QK_NOPE - QK_ROPE), (0, 0)))
    return {
        "uq_t": uq_t.reshape(MLA_HEADS * HEAD_PAD, Q_LORA),
        "ukv_t": misc["w_ukv"].reshape(MLA_HEADS * HEAD_PAD, KV_LORA),
        "o_mla_t": misc["w_o_mla"].reshape(D_MODEL, MLA_HEADS * V_HEAD),
        "o_dil_t": misc["w_o_dil"].reshape(D_MODEL, DIL_OUT),
    }


def _small_grad_blocks(g):
    uq_t = g["uq_t"].reshape(MLA_HEADS, HEAD_PAD, Q_LORA)[:, : QK_NOPE + QK_ROPE]
    misc = {"w_o_mla": g["o_mla_t"], "w_o_dil": g["o_dil_t"], "w_uq": uq_t, "w_ukv": g["ukv_t"]}
    return [
        g["w_out"].reshape(N_DEV, -1, D_MODEL),
        jnp.concatenate([misc[n].reshape(N_DEV, -1, D_MODEL) for n, _ in MISC], axis=1),
    ]


def _grad_shards(sums):
    s_in, s_out, s_misc, s_up, s_down = sums
    out = {
        "w_in": s_in.T,
        "w_up": s_up.reshape(2, FF_GROUP_PAD, D_MODEL)[:, :FF_GROUP].reshape(2 * FF_GROUP, D_MODEL).T,
        "w_down": s_down[:FF_GROUP],
        "w_out": s_out,
    }
    out.update({n: v.T for n, v in _misc_split(s_misc).items()})
    return out


def _local_step(x, h, tgt, wt, conv_w, small, small_matrices, ffn_weight, send_ffn_grads, send_small_grads, send_w_in_grads, forward_w_in_grads):
    S = x.shape[0]
    lat_t, dqkv_t, g_t = wt
    cw = jnp.pad(_ffn_pad(conv_w, 1), ((0, 5), (0, 0)))
    cb = _ffn_pad(small["conv_b"], 1)
    cos_t, sin_t = _rope_tables(S)
    bias = _dil_bias()
    g1, g2, g3 = small["attn_norm_g"], small["ffn_norm_g"], small["final_norm_g"]
    gq, gkv = small["q_norm_g"], small["kv_norm_g"]

    lat = _mm(h, lat_t, "nt", F32, 1024, LAT_PAD, D_MODEL, "proj_lat")
    dqkv = _mm(h, dqkv_t, "nt", F32, 1024, 1536, D_MODEL, "proj_dqkv")
    gates = _mm(h, g_t, "nt", F32, 1024, 1024, D_MODEL, "proj_gates", bias=small["b_gate"], act="sigmoid")
    sm = small_matrices(gates)
    uq_t, ukv_t, o_mla_t, o_dil_t = sm["uq_t"], sm["ukv_t"], sm["o_mla_t"], sm["o_dil_t"]
    cqn, ckvn, kpe = _mla_prep1(lat, gq, gkv, cos_t, sin_t, "mla_prep1")
    q_raw = _mm(cqn, uq_t, "nt", F32, 1024, 1024, Q_LORA, "mla_uq")
    kv = _mm(ckvn, ukv_t, "nt", BF16, 1024, 1024, KV_LORA, "mla_ukv")
    q_att, k_att = _mla_prep2(q_raw, kv, kpe, cos_t, sin_t, "mla_prep2")
    o, lse = _flash2_fwd(q_att, k_att, kv, "mla_flash_fwd")
    o_a = _mm(o, o_mla_t, "nt", F32, 1024, 1024, MLA_HEADS * V_HEAD, "mla_out")

    d_os, d_ls = [], []
    for g, (_, dil) in enumerate(DIL_PATTERNS):
        og, lg = _dil_fwd_group(dqkv, bias[g], g, dil, f"dil_fwd_{g}")
        d_os.append(og)
        d_ls.append(lg)
    od, dil_lse = _dil_combine(d_os, d_ls, "dil_combine")
    o_b = _mm(od, o_dil_t, "nt", F32, 1024, 1024, DIL_OUT, "dil_out")

    mrg = _merge_fwd(gates, o_a, o_b, "merge_fwd")
    w_out = ffn_weight("w_out", mrg)
    x1, h2 = _mm_res_rms(mrg, w_out, x, g2, "mix_out")
    up_t = ffn_weight("up_t", h2)
    u0 = _mm(h2, up_t, "nt", F32, 2048, 1024, D_MODEL, "ffn_up")
    a = _ffn_fwd(u0, cw, cb, "ffn_conv_fwd")
    w_down = ffn_weight("w_down", a)
    x2 = _mm(a, w_down, "nn", F32, 1024, 1024, D_FF_PAD // 2, "ffn_down", res=x1)
    loss_part, dx2, dx2b, dg3 = _final_loss(x2, g3, tgt, "final_loss")

    da = _mm(dx2b, w_down, "nt", F32, 1024, D_FF_PAD // 4, D_MODEL, "ffn_down_dx")
    gw_down = _mm(a, dx2b, "tn", BF16, 512, D_MODEL, S, "ffn_down_dw")
    du0, dcw, dcb = _ffn_bwd(u0, da, cw, cb, "ffn_conv_bwd")
    du0 = du0.reshape(2 * S, D_FF_PAD)
    gw_up_t = _mm(du0, h2, "tn", BF16, 512, D_MODEL, S, "ffn_up_dw", a_halves=2)
    sent = send_ffn_grads(gw_up_t, gw_down)
    dh2 = _mm(du0, up_t, "nn", F32, 1024, 1024, D_FF_PAD // 2, "ffn_up_dx", a_halves=2)
    dx1, dx1b, dg2 = _rms_bwd(dh2, x1, g2 + sent, dx2, "rms_ffn_bwd")

    dmrg = _mm(dx1b, w_out, "nt", F32, 1024, 1024, D_MODEL, "mix_out_dx")
    gw_out = _mm(mrg, dx1b, "tn", BF16, 512, D_MODEL, S, "mix_out_dw")
    do_a, do_b, dga, dgb, dba, dbb = _merge_bwd(dmrg, gates, o_a, o_b, "merge_bwd")

    do = _mm(do_a, o_mla_t, "nn", BF16, 1024, 1024, D_MODEL, "mla_out_dx")
    gw_o_mla_t = _mm(do_a, o, "tn", BF16, 1024, 1024, 1024, "mla_out_dw")
    dod = _mm(do_b, o_dil_t, "nn", F32, 1024, DIL_OUT, D_MODEL, "dil_out_dx")
    gw_o_dil_t = _mm(do_b, od, "tn", BF16, 1024, DIL_OUT, 1024, "dil_out_dw")

    delta = _flash_delta(do, o, "mla_flash_delta")
    lse_row = lse[:, :, 0][:, None, :]
    delta_row = delta[:, :MLA_HEADS].T[:, None, :]
    dq_att, dk_att, dv = _flash2_bwd(q_att, k_att, kv, do, lse_row, delta_row, "mla_flash_bwd")
    dq_raw, dkv, dkpe = _mla_post(dq_att, dk_att, dv, cos_t, sin_t, "mla_post")
    dcqn = _mm(dq_raw, uq_t, "nn", F32, 1024, Q_LORA, MLA_HEADS * HEAD_PAD, "mla_uq_dx")
    gw_uq_t = _mm(dq_raw, cqn, "tn", BF16, 1024, Q_LORA, 1024, "mla_uq_dw")
    dckvn = _mm(dkv, ukv_t, "nn", F32, 1024, KV_LORA, MLA_HEADS * HEAD_PAD, "mla_ukv_dx")
    gw_ukv_t = _mm(dkv, ckvn, "tn", BF16, 1024, KV_LORA, 1024, "mla_ukv_dw")
    sent = send_small_grads({"uq_t": gw_uq_t, "ukv_t": gw_ukv_t, "o_mla_t": gw_o_mla_t, "o_dil_t": gw_o_dil_t, "w_out": gw_out})
    dlat, dgq, dgkv = _lat_bwd(dcqn, dckvn, dkpe, lat, gq + sent, gkv, "lat_bwd")

    dd = _dil_rowdot(dod, od, "dil_rowdot")
    ddqkv = lax.empty((3 * DIL_GROUPS, S, DIL_OUT), F32)
    for g, (_, dil) in enumerate(DIL_PATTERNS):
        ddqkv = _dil_bwd_group(dqkv, bias[g], dod, dd, dil_lse, ddqkv, g, dil, f"dil_bwd_{g}")
    gw_lat_t = _mm(dlat, h, "tn", BF16, LAT_PAD, 1024, S, "proj_lat_dw")
    gw_dqkv_t = _mm(ddqkv.reshape(3 * DIL_GROUPS * S, DIL_OUT), h, "tn", BF16, 512, 1024, S, "proj_dqkv_dw", a_halves=3 * DIL_GROUPS)
    gw_ga_t = _mm(dga, h, "tn", BF16, 512, D_MODEL, S, "proj_ga_dw")
    gw_gb_t = _mm(dgb, h, "tn", BF16, 512, D_MODEL, S, "proj_gb_dw")
    sent = send_w_in_grads(gw_lat_t, gw_dqkv_t, gw_ga_t, gw_gb_t)
    dh = _mm(dlat + sent.astype(BF16), lat_t, "nn", F32, 1024, 1024, LAT_PAD, "proj_lat_dx")
    dh = _stacked_mm(ddqkv, dqkv_t, dh, forward_w_in_grads(dh), "proj_dqkv_dx")
    dh = _mm(dga, g_t, "nn", F32, 1024, 1024, D_MODEL, "proj_ga_dx", res=dh)
    grad_x, dg1 = _mm_rms_bwd(dgb, g_t, 1, dh, x, g1, dx1, "proj_gb_dx_rms_attn_bwd")

    small_grads = {
        "attn_norm_g": dg1,
        "b_gate": jnp.concatenate([dba, dbb], axis=1),
        "q_norm_g": dgq,
        "kv_norm_g": dgkv,
        "ffn_norm_g": dg2,
        "conv_b": _ffn_unpad(jnp.concatenate([dcb[0], dcb[1]], axis=1), 1),
        "final_norm_g": dg3,
        "conv_w": _ffn_unpad(jnp.concatenate([dcw[0, :3], dcw[1, :3]], axis=1), 1),
    }
    return loss_part, grad_x, small_grads


SMALL_ORDER = ("attn_norm_g", "b_gate", "q_norm_g", "kv_norm_g", "ffn_norm_g", "conv_b", "final_norm_g", "conv_w")
WEIGHT_ORDER = (
    "attn_norm_g", "w_in", "b_gate", "q_norm_g", "w_uq", "kv_norm_g", "w_ukv", "w_o_mla", "w_o_dil", "w_out",
    "ffn_norm_g", "w_up", "conv_w", "conv_b", "w_down", "final_norm_g",
)


def kernel(x, attn_norm_g, w_in, b_gate, q_norm_g, w_uq, kv_norm_g, w_ukv, w_o_mla, w_o_dil, w_out, ffn_norm_g, w_up, conv_w, conv_b, w_down, final_norm_g, loss_target, m_attn_norm_g, m_w_in, m_b_gate, m_q_norm_g, m_w_uq, m_kv_norm_g, m_w_ukv, m_w_o_mla, m_w_o_dil, m_w_out, m_ffn_norm_g, m_w_up, m_conv_w, m_conv_b, m_w_down, m_final_norm_g, v_attn_norm_g, v_w_in, v_b_gate, v_q_norm_g, v_w_uq, v_kv_norm_g, v_w_ukv, v_w_o_mla, v_w_o_dil, v_w_out, v_ffn_norm_g, v_w_up, v_conv_w, v_conv_b, v_w_down, v_final_norm_g):
    env = dict(locals())
    dev = 4 * lax.axis_index("x") + 2 * lax.axis_index("y") + lax.axis_index("c")
    core = lax.axis_index("c").astype(jnp.int32).reshape(1)

    def two_d(a):
        return a.reshape(-1, a.shape[-1])

    w = {n: two_d(env[n]) for n in WEIGHT_ORDER}
    m = {n: two_d(env["m_" + n]) for n in WEIGHT_ORDER}
    v = {n: two_d(env["v_" + n]) for n in WEIGHT_ORDER}

    chip = 2 * lax.axis_index("x") + lax.axis_index("y")

    def own_slot_in(lands, own, slot=dev):
        return [lax.dynamic_update_slice(l, o[None], (slot, 0, 0)) for l, o in zip(lands, own)]

    b_in = _exchange_blocks(w)[0]
    r, c = CONV_SHARD
    conv = jnp.pad(w["conv_w"].reshape(-1), (0, 8 * SMALL_COLS - r * c)).reshape(8, SMALL_COLS)
    first_level, token = _gather2_start([b_in, conv], "ag_w_in_start")
    tied = {n: w[n] + token[0, 0] for n in BIG_WEIGHTS}
    _, b_up, b_down, b_out, b_misc = _exchange_blocks(tied)
    h = _rms_fwd(x[0], w["attn_norm_g"] + token[0, 0], "rms_attn")
    prepared = b_up[:1, :1] + b_down[:1, :1] + b_out[:1, :1] + b_misc[:1, :1] + h[:1, :1]
    own, lands = _gather2_wait(_gather2_forward(first_level, prepared, "ag_w_in_forward"), "ag_w_in_wait")
    g_in, conv = own_slot_in(lands, own)
    misc_gather, started = _exchange_start([b_misc], False, conv, "ag_small_start")
    ffn_gathers, started2 = {}, started
    for key, block in (("w_out", b_out), ("up_t", b_up), ("w_down", b_down)):
        ffn_gathers[key], started2 = _exchange_start([block], False, started2, f"ag_{key}_start")
    wt = _w_in_regroup(g_in, started2, "w_in_regroup")
    conv = conv.reshape(N_DEV, 8 * SMALL_COLS)[:, : r * c].reshape(N_DEV, r, c)
    conv_w_full = conv.transpose(1, 0, 2).reshape(r, N_DEV * c)
    small = {n: w[n] for n in SMALL_ORDER if n != "conv_w"}

    def small_matrices(after):
        own, lands = _exchange_wait(misc_gather, False, after, "ag_small_wait")
        return _small_matrices(own_slot_in(lands, own)[0])

    def ffn_weight(key, after):
        own, lands = _exchange_wait(ffn_gathers[key], False, after, f"ag_{key}_wait")
        return own_slot_in(lands, own)[0].reshape(-1, D_MODEL)

    reduces = {}

    def send_ffn_grads(gw_up_t, gw_down):
        blocks = [gw_up_t.reshape(N_DEV, 2 * FF_GROUP_PAD, D_MODEL), gw_down.reshape(N_DEV, FF_GROUP_PAD, D_MODEL)]
        reduces["ffn"], token = _exchange_start(blocks, True, None, "rs_ffn_start")
        return token[0, 0]

    def send_small_grads(g):
        reduces["small"], token = _exchange_start(_small_grad_blocks(g), True, None, "rs_small_start")
        return token[0, 0]

    def send_w_in_grads(g_lat, g_dqkv, g_ga, g_gb):
        e_in = _w_in_grad_regroup(g_lat, g_dqkv, g_ga, g_gb, "w_in_grad_regroup")
        reduces["pair"], token = _pair_start(e_in, "rs_w_in_pair_start")
        return token[0, 0]

    def forward_w_in_grads(after):
        e_in, recv = _pair_wait(reduces.pop("pair"), after, "rs_w_in_pair_wait")
        pair = _pair_add(e_in, recv, core, "rs_w_in_pair_add")
        reduces["w_in"], token = _exchange_start([pair], True, None, "rs_w_in_start", chips=True)
        return token

    loss_part, grad_x, small_grads = _local_step(
        x[0], h, loss_target[0], wt, conv_w_full, small, small_matrices, ffn_weight,
        send_ffn_grads, send_small_grads, send_w_in_grads, forward_w_in_grads,
    )
    loss = lax.psum(loss_part[0, 0], AXES)
    sflat = jnp.concatenate([small_grads[n].reshape(-1) for n in SMALL_ORDER])
    sflat = jnp.pad(sflat, (0, SMALL_ROWS * SMALL_COLS - sflat.shape[0])).reshape(SMALL_ROWS, SMALL_COLS)
    vec_gather, _ = _exchange_start([sflat], False, None, "rs_vec_start")

    def finish(key, by_chip, name, rows=None):
        sent, lands = _exchange_wait(reduces[key], True, grad_x, name + "_wait", chips=by_chip)
        slot = chip if by_chip else dev
        own = [lax.dynamic_index_in_dim(s, slot, 0, keepdims=False) for s in sent]
        return [_slot_sum(p, f"{name}_sum_{i}", rows) for i, p in enumerate(own_slot_in(lands, own, slot))]

    (s_in,) = finish("w_in", True, "rs_w_in", IN_ROWS)
    s_out, s_misc = finish("small", False, "rs_small")
    s_up, s_down = finish("ffn", False, "rs_ffn")
    gshard = _grad_shards([s_in, s_out, s_misc, s_up, s_down])

    updates = {n: _adamw(w[n], gshard[n], m[n], v[n], "adamw_" + n) for n in BIG_WEIGHTS}

    big_done = sum(updates[n][0][:1, :1] for n in BIG_WEIGHTS)
    own, lands = _exchange_wait(vec_gather, False, big_done, "rs_vec_wait")
    ssum = _slot_sum(own_slot_in(lands, own)[0], "small_sum").reshape(-1)
    gsmall, off = {}, 0
    for n in SMALL_ORDER:
        shape = (3, 2 * D_FF) if n == "conv_w" else w[n].shape
        size = shape[0] * shape[1]
        gsmall[n] = ssum[off : off + size].reshape(shape)
        off += size
    gsmall["conv_w"] = lax.dynamic_slice_in_dim(gsmall["conv_w"], dev * CONV_SHARD[1], CONV_SHARD[1], axis=1)
    updates.update({n: _adamw(w[n], gsmall[n], m[n], v[n], "adamw_" + n) for n in SMALL_ORDER})

    g_all = {**gshard, **gsmall}
    out_g, out_d, out_m, out_v = [], [], [], []
    for n in WEIGHT_ORDER:
        d, nm, nv = updates[n]
        shape = env[n].shape
        out_g.append(g_all[n].reshape(shape))
        out_d.append(d.reshape(shape))
        out_m.append(nm.reshape(shape))
        out_v.append(nv.reshape(shape))
    return (loss, grad_x[None], *out_g, *out_d, *out_m, *out_v)
```

```python
import functools

import jax
import jax.numpy as jnp
import numpy as np
from jax import lax
from jax.experimental import pallas as pl
from jax.experimental.pallas import tpu as pltpu

F32 = jnp.float32
BF16 = jnp.bfloat16

N_DEV = 8
N_CHIP = 4
AXES = ("x", "y", "c")
MESH = pl.DeviceIdType.MESH

D_MODEL = 2048
MLA_HEADS = 8
QK_NOPE = 128
QK_ROPE = 64
V_HEAD = 128
Q_LORA = 512
KV_LORA = 256
ROPE_THETA = 10000.0
HEAD_PAD = 256
DIL_PATTERNS = ((128, 1), (512, 4), (2048, 16))
DIL_GROUPS = 3
DIL_HG = 4
DIL_HEADS = 12
DIL_HD = 128
DIL_BLK = 128
DIL_QKV = DIL_HEADS * DIL_HD
DIL_OUT = DIL_HG * DIL_HD
ALIBI_MAX_BIAS = 8.0
D_FF = 5504
D_FF_PAD = 5632
NORM_EPS = 1e-6
LAT = Q_LORA + KV_LORA + QK_ROPE
LAT_PAD = 896
D_IN = LAT + 3 * DIL_QKV + 2 * D_MODEL
NEG = -1e30

ADAM_LR = 0.001
ADAM_B1 = 0.9
ADAM_B2 = 0.999
ADAM_EPS = 1e-08
ADAM_WD = 0.01
ADAM_STEP = 10

SMALL_ROWS = 56
SMALL_COLS = 1024

IN_ROWS = 1192
IN_ROWS_PAD = 1200
FF_GROUP = D_FF // N_DEV
FF_GROUP_PAD = D_FF_PAD // N_DEV
CONV_SHARD = (3, 1376)

NT = (((1,), (1,)), ((), ()))
TN = (((0,), (0,)), ((), ()))


def _dot(a, b, dims=(((1,), (0,)), ((), ()))):
    return lax.dot_general(a, b, dims, preferred_element_type=F32)


def _mm(a, b, mode, out_dtype, tm, tn, tk, name, bias=None, act=None, res=None, b_koff=0, a_halves=1):
    H = a_halves
    if mode == "nn":
        (M, K), (K2, N) = (a.shape[0] // H, a.shape[1] * H), b.shape
        assert (b_koff + 1) * K <= K2, (name, a.shape, b.shape)
        koff, K2 = b_koff * (K // tk), K
        kper, mrows = a.shape[1] // tk, M // tm
        a_spec = pl.BlockSpec((tm, tk), lambda i, j, k: (i + (k // kper) * mrows, k % kper))
        b_spec = pl.BlockSpec((tk, tn), lambda i, j, k: (k + koff, j))
        dims = (((1,), (0,)), ((), ()))
    elif mode == "nt":
        (M, K), (N, K2) = a.shape, b.shape
        a_spec = pl.BlockSpec((tm, tk), lambda i, j, k: (i, k))
        b_spec = pl.BlockSpec((tn, tk), lambda i, j, k: (j, k))
        dims = NT
    else:
        (K, M), (K2, N) = (a.shape[0] // H, a.shape[1] * H), b.shape
        mper, krows = a.shape[1] // tm, K // tk
        a_spec = pl.BlockSpec((tk, tm), lambda i, j, k: (k + (i // mper) * krows, i % mper))
        b_spec = pl.BlockSpec((tk, tn), lambda i, j, k: (k, j))
        dims = TN
    assert K == K2 and M % tm == 0 and N % tn == 0 and K % tk == 0, (name, a.shape, b.shape)
    nk = K // tk
    has_bias, has_res = bias is not None, res is not None

    def body(*refs):
        refs = list(refs)
        a_ref, b_ref = refs[0], refs[1]
        pos = 2
        bias_ref = res_ref = None
        if has_bias:
            bias_ref = refs[pos]
            pos += 1
        if has_res:
            res_ref = refs[pos]
            pos += 1
        o_ref = refs[pos]
        p = _dot(a_ref[...].astype(BF16), b_ref[...].astype(BF16), dims)

        def finish(acc):
            if has_bias:
                acc = acc + bias_ref[...]
            if act == "sigmoid":
                acc = jax.nn.sigmoid(acc)
            if has_res:
                acc = res_ref[...] + acc
            o_ref[...] = acc.astype(o_ref.dtype)

        if nk == 1:
            finish(p)
        else:
            acc_ref = refs[pos + 1]
            k = pl.program_id(2)

            @pl.when(k == 0)
            def _():
                acc_ref[...] = p

            @pl.when(k != 0)
            def _():
                acc_ref[...] += p

            @pl.when(k == nk - 1)
            def _():
                finish(acc_ref[...])

    in_specs = [a_spec, b_spec]
    args = [a, b]
    if has_bias:
        in_specs.append(pl.BlockSpec((1, tn), lambda i, j, k: (0, j)))
        args.append(bias)
    if has_res:
        in_specs.append(pl.BlockSpec((tm, tn), lambda i, j, k: (i, j)))
        args.append(res)
    return pl.pallas_call(
        body,
        name=name,
        grid=(M // tm, N // tn, nk),
        in_specs=in_specs,
        out_specs=pl.BlockSpec((tm, tn), lambda i, j, k: (i, j)),
        out_shape=jax.ShapeDtypeStruct((M, N), out_dtype),
        scratch_shapes=[pltpu.VMEM((tm, tn), F32)] if nk > 1 else [],
        compiler_params=pltpu.CompilerParams(dimension_semantics=("parallel", "parallel", "arbitrary")),
    )(*args)


def _stacked_mm(pieces, w_t, res, after, name, tm=512, tn=1024):
    P, M, W = pieces.shape
    N = w_t.shape[1]

    def body(a_ref, b_ref, r_ref, _, o_ref):
        acc = r_ref[...]
        for p in range(P):
            acc = acc + _dot(a_ref[p].astype(BF16), b_ref[p * W : (p + 1) * W, :])
        o_ref[...] = acc

    tile = pl.BlockSpec((tm, tn), lambda i, j: (i, j))
    return pl.pallas_call(
        body,
        name=name,
        grid=(M // tm, N // tn),
        in_specs=[
            pl.BlockSpec((P, tm, W), lambda i, j: (0, i, 0)),
            pl.BlockSpec((P * W, tn), lambda i, j: (0, j)),
            tile,
            pl.BlockSpec((8, 128), lambda i, j: (0, 0)),
        ],
        out_specs=tile,
        out_shape=jax.ShapeDtypeStruct((M, N), F32),
        compiler_params=pltpu.CompilerParams(dimension_semantics=("parallel", "parallel")),
    )(pieces, w_t, res, after)


def _rstd(x):
    return lax.rsqrt(jnp.mean(x * x, axis=-1, keepdims=True) + NORM_EPS)


def _rms_bwd_math(dy, x, g):
    r = _rstd(x)
    xh = x * r
    dg = jnp.sum(dy * xh, axis=0, keepdims=True)
    dxh = dy * g
    dx = r * (dxh - xh * jnp.mean(dxh * xh, axis=-1, keepdims=True))
    return dx, dg


def _rms_fwd(x, g, name, tr=256):
    S, D = x.shape

    def body(x_ref, g_ref, o_ref):
        xv = x_ref[...]
        o_ref[...] = ((xv * _rstd(xv)) * g_ref[...]).astype(o_ref.dtype)

    return pl.pallas_call(
        body,
        name=name,
        grid=(S // tr,),
        in_specs=[pl.BlockSpec((tr, D), lambda i: (i, 0)), pl.BlockSpec((1, D), lambda i: (0, 0))],
        out_specs=pl.BlockSpec((tr, D), lambda i: (i, 0)),
        out_shape=jax.ShapeDtypeStruct((S, D), BF16),
        compiler_params=pltpu.CompilerParams(dimension_semantics=("parallel",)),
    )(x, g)


def _rms_bwd(dy, x, g, res, name, tr=256):
    S, D = x.shape

    def body(dy_ref, x_ref, g_ref, res_ref, dx_ref, dxb_ref, dg_ref):
        dx, dg = _rms_bwd_math(dy_ref[...], x_ref[...], g_ref[...])
        dx = dx + res_ref[...]
        dx_ref[...] = dx
        dxb_ref[...] = dx.astype(BF16)

        @pl.when(pl.program_id(0) == 0)
        def _():
            dg_ref[...] = dg

        @pl.when(pl.program_id(0) != 0)
        def _():
            dg_ref[...] += dg

    row = pl.BlockSpec((tr, D), lambda i: (i, 0))
    vec = pl.BlockSpec((1, D), lambda i: (0, 0))
    return pl.pallas_call(
        body,
        name=name,
        grid=(S // tr,),
        in_specs=[row, row, vec, row],
        out_specs=[row, row, vec],
        out_shape=[jax.ShapeDtypeStruct((S, D), F32), jax.ShapeDtypeStruct((S, D), BF16), jax.ShapeDtypeStruct((1, D), F32)],
        compiler_params=pltpu.CompilerParams(dimension_semantics=("arbitrary",)),
    )(dy, x, g, res)


def _mm_res_rms(a, b, res, g, name, tm=256):
    M, K = a.shape
    D = b.shape[1]

    def body(a_ref, b_ref, res_ref, g_ref, y_ref, h_ref):
        y = res_ref[...] + _dot(a_ref[...], b_ref[...])
        y_ref[...] = y
        h_ref[...] = ((y * _rstd(y)) * g_ref[...]).astype(BF16)

    row = pl.BlockSpec((tm, D), lambda i: (i, 0))
    return pl.pallas_call(
        body,
        name=name,
        grid=(M // tm,),
        in_specs=[pl.BlockSpec((tm, K), lambda i: (i, 0)), pl.BlockSpec((K, D), lambda i: (0, 0)), row, pl.BlockSpec((1, D), lambda i: (0, 0))],
        out_specs=[row, row],
        out_shape=[jax.ShapeDtypeStruct((M, D), F32), jax.ShapeDtypeStruct((M, D), BF16)],
        compiler_params=pltpu.CompilerParams(dimension_semantics=("parallel",)),
    )(a, b, res, g)


def _mm_rms_bwd(a, b, b_koff, dy_part, x, g, res, name, tm=256):
    M, K = a.shape
    D = x.shape[1]

    def body(a_ref, b_ref, dyp_ref, x_ref, g_ref, res_ref, dx_ref, dg_ref):
        dy = dyp_ref[...] + _dot(a_ref[...], b_ref[...])
        dx, dg = _rms_bwd_math(dy, x_ref[...], g_ref[...])
        dx_ref[...] = dx + res_ref[...]

        @pl.when(pl.program_id(0) == 0)
        def _():
            dg_ref[...] = dg

        @pl.when(pl.program_id(0) != 0)
        def _():
            dg_ref[...] += dg

    row = pl.BlockSpec((tm, D), lambda i: (i, 0))
    vec = pl.BlockSpec((1, D), lambda i: (0, 0))
    return pl.pallas_call(
        body,
        name=name,
        grid=(M // tm,),
        in_specs=[pl.BlockSpec((tm, K), lambda i: (i, 0)), pl.BlockSpec((K, D), lambda i: (b_koff, 0)), row, row, vec, row],
        out_specs=[row, vec],
        out_shape=[jax.ShapeDtypeStruct((M, D), F32), jax.ShapeDtypeStruct((1, D), F32)],
        compiler_params=pltpu.CompilerParams(dimension_semantics=("arbitrary",)),
    )(a, b, dy_part, x, g, res)


def _final_loss(x2, g, tgt, name, tr=256):
    S, D = x2.shape

    def body(x_ref, g_ref, t_ref, loss_ref, dx_ref, dxb_ref, dg_ref):
        xv, gv = x_ref[...], g_ref[...]
        y = (xv * _rstd(xv)) * gv
        e = y - t_ref[...]
        part = 0.5 * jnp.sum(jnp.mean(e * e, axis=-1, keepdims=True), axis=0, keepdims=True)
        dx, dg = _rms_bwd_math(e * (1.0 / D), xv, gv)
        dx_ref[...] = dx
        dxb_ref[...] = dx.astype(BF16)
        part = jnp.broadcast_to(part, (1, 128))

        @pl.when(pl.program_id(0) == 0)
        def _():
            dg_ref[...] = dg
            loss_ref[...] = part

        @pl.when(pl.program_id(0) != 0)
        def _():
            dg_ref[...] += dg
            loss_ref[...] += part

    row = pl.BlockSpec((tr, D), lambda i: (i, 0))
    vec = pl.BlockSpec((1, D), lambda i: (0, 0))
    return pl.pallas_call(
        body,
        name=name,
        grid=(S // tr,),
        in_specs=[row, vec, row],
        out_specs=[pl.BlockSpec((1, 128), lambda i: (0, 0)), row, row, vec],
        out_shape=[
            jax.ShapeDtypeStruct((1, 128), F32),
            jax.ShapeDtypeStruct((S, D), F32),
            jax.ShapeDtypeStruct((S, D), BF16),
            jax.ShapeDtypeStruct((1, D), F32),
        ],
        compiler_params=pltpu.CompilerParams(dimension_semantics=("arbitrary",)),
    )(x2, g, tgt)


def _rope_tables(S):
    pos = jnp.arange(S, dtype=F32)
    inv_freq = ROPE_THETA ** (-jnp.arange(0, QK_ROPE, 2, dtype=F32) / QK_ROPE)
    ang = pos[:, None] * inv_freq[None, :]
    cos, sin = jnp.cos(ang), jnp.sin(ang)
    zero = jnp.zeros((S, 128 - QK_ROPE), F32)
    return jnp.concatenate([cos, cos, zero], axis=1), jnp.concatenate([-sin, sin, zero], axis=1)


def _rope_tile(x, cos_t, sin_t):
    lane = lax.broadcasted_iota(jnp.int32, x.shape, 1)
    partner = jnp.where(lane < QK_ROPE // 2, pltpu.roll(x, 128 - QK_ROPE // 2, 1), pltpu.roll(x, QK_ROPE // 2, 1))
    return x * cos_t + partner * sin_t


def _mla_prep1(lat, gq, gkv, cos_t, sin_t, name, tr=256):
    S = lat.shape[0]

    def body(lat_ref, gq_ref, gkv_ref, cos_ref, sin_ref, cq_ref, ckv_ref, kpe_ref):
        cq = lat_ref[:, :Q_LORA]
        ckv = lat_ref[:, Q_LORA : Q_LORA + KV_LORA]
        cq_ref[...] = ((cq * _rstd(cq)) * gq_ref[...]).astype(BF16)
        ckv_ref[...] = ((ckv * _rstd(ckv)) * gkv_ref[...]).astype(BF16)
        kpe_ref[...] = _rope_tile(lat_ref[:, Q_LORA + KV_LORA :], cos_ref[...], sin_ref[...]).astype(BF16)

    def row(n):
        return pl.BlockSpec((tr, n), lambda i: (i, 0))

    def vec(n):
        return pl.BlockSpec((1, n), lambda i: (0, 0))

    return pl.pallas_call(
        body,
        name=name,
        grid=(S // tr,),
        in_specs=[row(LAT_PAD), vec(Q_LORA), vec(KV_LORA), row(128), row(128)],
        out_specs=[row(Q_LORA), row(KV_LORA), row(128)],
        out_shape=[
            jax.ShapeDtypeStruct((S, Q_LORA), BF16),
            jax.ShapeDtypeStruct((S, KV_LORA), BF16),
            jax.ShapeDtypeStruct((S, 128), BF16),
        ],
        compiler_params=pltpu.CompilerParams(dimension_semantics=("parallel",)),
    )(lat, gq, gkv, cos_t, sin_t)


def _mla_prep2(q_raw, kv, kpe, cos_t, sin_t, name, tr=256):
    S = q_raw.shape[0]
    W = MLA_HEADS * HEAD_PAD

    def body(q_ref, kv_ref, kpe_ref, cos_ref, sin_ref, qa_ref, ka_ref):
        cos_v, sin_v, kpe_v = cos_ref[...], sin_ref[...], kpe_ref[...]
        for h in range(MLA_HEADS):
            lo = h * HEAD_PAD
            qa_ref[:, lo : lo + 128] = q_ref[:, lo : lo + 128].astype(BF16)
            qa_ref[:, lo + 128 : lo + 256] = _rope_tile(q_ref[:, lo + 128 : lo + 256], cos_v, sin_v).astype(BF16)
            ka_ref[:, lo : lo + 128] = kv_ref[:, lo : lo + 128]
            ka_ref[:, lo + 128 : lo + 256] = kpe_v

    def row(n):
        return pl.BlockSpec((tr, n), lambda i: (i, 0))

    return pl.pallas_call(
        body,
        name=name,
        grid=(S // tr,),
        in_specs=[row(W), row(W), row(128), row(128), row(128)],
        out_specs=[row(W), row(W)],
        out_shape=[jax.ShapeDtypeStruct((S, W), BF16), jax.ShapeDtypeStruct((S, W), BF16)],
        compiler_params=pltpu.CompilerParams(dimension_semantics=("parallel",)),
    )(q_raw, kv, kpe, cos_t, sin_t)


def _mla_post(dq_att, dk_att, dv, cos_t, sin_t, name, tr=256):
    S = dq_att.shape[0]
    W = MLA_HEADS * HEAD_PAD

    def body(dq_ref, dk_ref, dv_ref, cos_ref, sin_ref, dqr_ref, dkv_ref, dkpe_ref):
        cos_v, nsin_v = cos_ref[...], -sin_ref[...]
        kpe = jnp.zeros((tr, 128), F32)
        for h in range(MLA_HEADS):
            lo = h * HEAD_PAD
            dqr_ref[:, lo : lo + 128] = dq_ref[:, lo : lo + 128].astype(BF16)
            dqr_ref[:, lo + 128 : lo + 256] = _rope_tile(dq_ref[:, lo + 128 : lo + 256], cos_v, nsin_v).astype(BF16)
            dkv_ref[:, lo : lo + 128] = dk_ref[:, lo : lo + 128].astype(BF16)
            dkv_ref[:, lo + 128 : lo + 256] = dv_ref[:, h * 128 : (h + 1) * 128].astype(BF16)
            kpe = kpe + dk_ref[:, lo + 128 : lo + 256]
        dkpe_ref[...] = _rope_tile(kpe, cos_v, nsin_v)

    def row(n):
        return pl.BlockSpec((tr, n), lambda i: (i, 0))

    return pl.pallas_call(
        body,
        name=name,
        grid=(S // tr,),
        in_specs=[row(W), row(W), row(MLA_HEADS * V_HEAD), row(128), row(128)],
        out_specs=[row(W), row(W), row(128)],
        out_shape=[jax.ShapeDtypeStruct((S, W), BF16), jax.ShapeDtypeStruct((S, W), BF16), jax.ShapeDtypeStruct((S, 128), F32)],
        compiler_params=pltpu.CompilerParams(dimension_semantics=("parallel",)),
    )(dq_att, dk_att, dv, cos_t, sin_t)


def _lat_bwd(dcqn, dckvn, dkpe, lat, gq, gkv, name, tr=256):
    S = lat.shape[0]

    def body(dcq_ref, dckv_ref, dkpe_ref, lat_ref, gq_ref, gkv_ref, dlat_ref, dgq_ref, dgkv_ref):
        dq, dgq = _rms_bwd_math(dcq_ref[...], lat_ref[:, :Q_LORA], gq_ref[...])
        dkv, dgkv = _rms_bwd_math(dckv_ref[...], lat_ref[:, Q_LORA : Q_LORA + KV_LORA], gkv_ref[...])
        dlat_ref[:, :Q_LORA] = dq.astype(BF16)
        dlat_ref[:, Q_LORA : Q_LORA + KV_LORA] = dkv.astype(BF16)
        dlat_ref[:, Q_LORA + KV_LORA :] = dkpe_ref[...].astype(BF16)

        @pl.when(pl.program_id(0) == 0)
        def _():
            dgq_ref[...] = dgq
            dgkv_ref[...] = dgkv

        @pl.when(pl.program_id(0) != 0)
        def _():
            dgq_ref[...] += dgq
            dgkv_ref[...] += dgkv

    def row(n):
        return pl.BlockSpec((tr, n), lambda i: (i, 0))

    def vec(n):
        return pl.BlockSpec((1, n), lambda i: (0, 0))

    return pl.pallas_call(
        body,
        name=name,
        grid=(S // tr,),
        in_specs=[row(Q_LORA), row(KV_LORA), row(128), row(LAT_PAD), vec(Q_LORA), vec(KV_LORA)],
        out_specs=[row(LAT_PAD), vec(Q_LORA), vec(KV_LORA)],
        out_shape=[
            jax.ShapeDtypeStruct((S, LAT_PAD), BF16),
            jax.ShapeDtypeStruct((1, Q_LORA), F32),
            jax.ShapeDtypeStruct((1, KV_LORA), F32),
        ],
        compiler_params=pltpu.CompilerParams(dimension_semantics=("arbitrary",)),
    )(dcqn, dckvn, dkpe, lat, gq, gkv)


MLA_SCALE = (QK_NOPE + QK_ROPE) ** -0.5
LOG2E = 1.4426950408889634
MLA_C2 = MLA_SCALE * LOG2E
FLASH_T = 1024


def _causal_pairs(n, by_key):
    pairs = [(i, j) for j in range(n) for i in range(j, n)] if by_key else [(i, j) for i in range(n) for j in range(i + 1)]
    return jnp.asarray([p[0] for p in pairs], jnp.int32), jnp.asarray([p[1] for p in pairs], jnp.int32)


def _causal_mask(shape, shift, keys_first=False):
    q_axis, k_axis = (1, 0) if keys_first else (0, 1)
    return lax.broadcasted_iota(jnp.int32, shape, k_axis) <= lax.broadcasted_iota(jnp.int32, shape, q_axis) + shift


def _lanes(x, n):
    return jnp.tile(x, (1, n // 128))


def _flash_grid(npairs, in_specs, out_specs, scratch):
    return pltpu.PrefetchScalarGridSpec(
        num_scalar_prefetch=2, grid=(MLA_HEADS, npairs), in_specs=in_specs, out_specs=out_specs, scratch_shapes=scratch
    )


def _flash2_fwd(q_att, k_att, kv, name, t=FLASH_T):
    S = q_att.shape[0]
    half = t // 2
    qi_tab, kj_tab = _causal_pairs(S // t, by_key=False)

    def body(qi_ref, kj_ref, q_ref, k_ref, v_ref, o_ref, lse_ref, m_sc, l_sc, acc_sc):
        step = pl.program_id(1)
        qi, kj = qi_ref[step], kj_ref[step]

        @pl.when(kj == 0)
        def _():
            m_sc[...] = jnp.full((t, 128), NEG, F32)
            l_sc[...] = jnp.zeros((t, 128), F32)
            acc_sc[...] = jnp.zeros((t, V_HEAD), F32)

        def update(rows, s, v):
            m_prev = m_sc[rows, :]
            m_new = jnp.maximum(m_prev, jnp.max(s, axis=1, keepdims=True))
            p = jnp.exp2((s - _lanes(m_new, s.shape[1])) * MLA_C2)
            alpha = jnp.exp2((m_prev - m_new) * MLA_C2)
            l_sc[rows, :] = alpha * l_sc[rows, :] + jnp.sum(p, axis=1, keepdims=True)
            acc_sc[rows, :] = alpha * acc_sc[rows, :] + _dot(p.astype(BF16), v)
            m_sc[rows, :] = m_new

        @pl.when(kj < qi)
        def _():
            update(slice(0, t), _dot(q_ref[...], k_ref[...], NT), v_ref[...])

        @pl.when(kj == qi)
        def _():
            top = _dot(q_ref[:half, :], k_ref[:half, :], NT)
            update(slice(0, half), jnp.where(_causal_mask(top.shape, 0), top, NEG), v_ref[:half, :])
            bot = _dot(q_ref[half:, :], k_ref[...], NT)
            update(slice(half, t), jnp.where(_causal_mask(bot.shape, half), bot, NEG), v_ref[...])
            l = l_sc[...]
            o_ref[...] = acc_sc[...] / l
            lse_ref[0] = m_sc[...] * MLA_SCALE + jnp.log(l)

    return pl.pallas_call(
        body,
        name=name,
        grid_spec=_flash_grid(
            qi_tab.shape[0],
            [
                pl.BlockSpec((t, HEAD_PAD), lambda h, p, qi, kj: (qi[p], h)),
                pl.BlockSpec((t, HEAD_PAD), lambda h, p, qi, kj: (kj[p], h)),
                pl.BlockSpec((t, V_HEAD), lambda h, p, qi, kj: (kj[p], 2 * h + 1)),
            ],
            [
                pl.BlockSpec((t, V_HEAD), lambda h, p, qi, kj: (qi[p], h)),
                pl.BlockSpec((1, t, 128), lambda h, p, qi, kj: (h, qi[p], 0)),
            ],
            [pltpu.VMEM((t, 128), F32), pltpu.VMEM((t, 128), F32), pltpu.VMEM((t, V_HEAD), F32)],
        ),
        out_shape=[jax.ShapeDtypeStruct((S, MLA_HEADS * V_HEAD), F32), jax.ShapeDtypeStruct((MLA_HEADS, S, 128), F32)],
        compiler_params=pltpu.CompilerParams(dimension_semantics=("parallel", "arbitrary")),
    )(qi_tab, kj_tab, q_att, k_att, kv)


def _flash_delta(do, o, name, tr=512):
    S = o.shape[0]

    def body(do_ref, o_ref, d_ref):
        lane = lax.broadcasted_iota(jnp.int32, (tr, 128), 1)
        acc = jnp.zeros((tr, 128), F32)
        for h in range(MLA_HEADS):
            sl = slice(h * V_HEAD, (h + 1) * V_HEAD)
            acc = jnp.where(lane == h, jnp.sum(do_ref[:, sl].astype(F32) * o_ref[:, sl], axis=1, keepdims=True), acc)
        d_ref[...] = acc

    row = pl.BlockSpec((tr, MLA_HEADS * V_HEAD), lambda i: (i, 0))
    return pl.pallas_call(
        body,
        name=name,
        grid=(S // tr,),
        in_specs=[row, row],
        out_specs=pl.BlockSpec((tr, 128), lambda i: (i, 0)),
        out_shape=jax.ShapeDtypeStruct((S, 128), F32),
        compiler_params=pltpu.CompilerParams(dimension_semantics=("parallel",)),
    )(do, o)


def _flash2_bwd(q_att, k_att, kv, do, lse_row, delta_row, name, t=FLASH_T):
    S = q_att.shape[0]
    n = S // t
    qi_tab, kj_tab = _causal_pairs(n, by_key=True)
    last = qi_tab.shape[0] - 1
    half = t // 2

    def body(qi_ref, kj_ref, q_ref, k_ref, v_ref, do_ref, lse_ref, dl_ref, dq_ref, dk_ref, dv_ref, dk_sc, dv_sc):
        step = pl.program_id(1)
        qi, kj = qi_ref[step], kj_ref[step]

        @pl.when(step == 0)
        def _():
            dq_ref[...] = jnp.zeros((S, HEAD_PAD), F32)

        def update(k0, q0, st):
            nk, nq = st.shape
            kr, qr = slice(k0, k0 + nk), slice(q0, q0 + nq)
            q, do_v = q_ref[qr, :], do_ref[qr, :]
            pt = jnp.exp2(st * MLA_C2 - lse_ref[0][:, qr] * LOG2E)
            dv_sc[kr, :] += _dot(pt.astype(BF16), do_v)
            dpt = _dot(v_ref[kr, :], do_v, NT)
            dst = (pt * (dpt - dl_ref[0][:, qr])).astype(BF16)
            dk_sc[kr, :] += _dot(dst, q)
            rows = pl.ds(pl.multiple_of(qi * t + q0, half), nq)
            dq_ref[rows, :] += _dot(dst, k_ref[kr, :], TN)

        @pl.when(qi == kj)
        def _():
            dk_sc[...] = jnp.zeros((t, HEAD_PAD), F32)
            dv_sc[...] = jnp.zeros((t, V_HEAD), F32)
            top = _dot(k_ref[:half, :], q_ref[...], NT)
            update(0, 0, jnp.where(_causal_mask(top.shape, 0, keys_first=True), top, NEG))
            bot = _dot(k_ref[half:, :], q_ref[half:, :], NT)
            update(half, half, jnp.where(_causal_mask(bot.shape, 0, keys_first=True), bot, NEG))

        @pl.when(qi > kj)
        def _():
            update(0, 0, _dot(k_ref[...], q_ref[...], NT))

        @pl.when(qi == n - 1)
        def _():
            dk_ref[...] = dk_sc[...] * MLA_SCALE
            dv_ref[...] = dv_sc[...]

        @pl.when(step == last)
        def _():
            dq_ref[...] = dq_ref[...] * MLA_SCALE

    qrow = lambda h, p, qi, kj: (qi[p], h)
    krow = lambda h, p, qi, kj: (kj[p], h)
    stat = pl.BlockSpec((1, 1, t), lambda h, p, qi, kj: (h, 0, qi[p]))
    return pl.pallas_call(
        body,
        name=name,
        grid_spec=_flash_grid(
            qi_tab.shape[0],
            [
                pl.BlockSpec((t, HEAD_PAD), qrow),
                pl.BlockSpec((t, HEAD_PAD), krow),
                pl.BlockSpec((t, V_HEAD), lambda h, p, qi, kj: (kj[p], 2 * h + 1)),
                pl.BlockSpec((t, V_HEAD), qrow),
                stat,
                stat,
            ],
            [
                pl.BlockSpec((S, HEAD_PAD), lambda h, p, qi, kj: (0, h)),
                pl.BlockSpec((t, HEAD_PAD), krow),
                pl.BlockSpec((t, V_HEAD), krow),
            ],
            [pltpu.VMEM((t, HEAD_PAD), F32), pltpu.VMEM((t, V_HEAD), F32)],
        ),
        out_shape=[
            jax.ShapeDtypeStruct((S, MLA_HEADS * HEAD_PAD), F32),
            jax.ShapeDtypeStruct((S, MLA_HEADS * HEAD_PAD), F32),
            jax.ShapeDtypeStruct((S, MLA_HEADS * V_HEAD), F32),
        ],
        compiler_params=pltpu.CompilerParams(dimension_semantics=("parallel", "arbitrary")),
    )(qi_tab, kj_tab, q_att, k_att, kv, do, lse_row, delta_row)


DIL_SCALE = DIL_HD**-0.5


def _dil_bias():
    slopes = 2.0 ** (-ALIBI_MAX_BIAS * np.arange(1, DIL_HEADS + 1, dtype=np.float64) / DIL_HEADS)
    slopes = slopes.astype(np.float32).reshape(DIL_GROUPS, DIL_HG)
    p = np.arange(DIL_BLK)[:, None]
    kidx = np.arange(2 * DIL_BLK)[None, :]
    j = p + DIL_BLK - kidx
    out = np.zeros((DIL_GROUPS, DIL_HG, DIL_BLK, 2 * DIL_BLK), np.float32)
    for g, (window, dil) in enumerate(DIL_PATTERNS):
        valid = (j >= 0) & (j <= window // dil)
        for h in range(DIL_HG):
            alibi = -slopes[g, h] * (dil * j).astype(np.float32)
            out[g, h] = np.where(valid, alibi, np.float32(NEG))
    return jnp.asarray(out)


DIL_UNROLL = 4


def _unrolled_loop(lo, hi, fn, unroll=DIL_UNROLL):
    groups = (hi - lo) // unroll
    done = lo
    if groups > 1:

        def step(i, carry):
            for u in range(unroll):
                fn(lo + i * unroll + u)
            return carry

        lax.fori_loop(0, groups, step, 0)
        done = lo + groups * unroll
    for n in range(done, hi):
        fn(n)


def _dil_rows(r, n, count, dil):
    if dil == 1:
        if isinstance(n, int):
            return slice(n * DIL_BLK, (n + count) * DIL_BLK)
        return pl.ds(pl.multiple_of(n * DIL_BLK, DIL_BLK), count * DIL_BLK)
    return pl.ds(n * DIL_BLK * dil + r, count * DIL_BLK, stride=dil)


def _dil_each_block(S, dil, block):
    nb = S // dil // DIL_BLK
    if dil == 1:
        block(0, 0, True)
        _unrolled_loop(1, nb, lambda n: block(0, n, False))
    else:
        for r in range(dil):
            for n in range(nb):
                block(r, n, n == 0)


def _dil_col(g, part, h):
    return (g * 3 + part) * DIL_HG + h


def _dil_fwd_group(dqkv, bias_g, g, dil, name):
    S = dqkv.shape[0]

    def body(bias_ref, q_ref, k_ref, v_ref, o_ref, lse_ref):
        def block(r, n, first):
            cur = _dil_rows(r, n, 1, dil)
            both = cur if first else _dil_rows(r, n - 1, 2, dil)
            b = bias_ref[0][:, DIL_BLK:] if first else bias_ref[0]
            q, kk, vv = q_ref[cur, :].astype(BF16), k_ref[both, :].astype(BF16), v_ref[both, :].astype(BF16)
            s = _dot(q, kk, NT) * DIL_SCALE + b
            m = jnp.max(s, axis=1, keepdims=True)
            e = jnp.exp(s - m)
            l = jnp.sum(e, axis=1, keepdims=True)
            p = e * (1.0 / l)
            o_ref[cur, :] = _dot(p.astype(BF16), vv)
            lse_ref[cur, :] = jnp.broadcast_to(m + jnp.log(l), (DIL_BLK, 128))

        _dil_each_block(S, dil, block)

    def col(part):
        return pl.BlockSpec((S, DIL_HD), lambda h: (0, _dil_col(g, part, h)))

    out = pl.BlockSpec((S, DIL_HD), lambda h: (0, h))
    return pl.pallas_call(
        body,
        name=name,
        grid=(DIL_HG,),
        in_specs=[pl.BlockSpec((1, DIL_BLK, 2 * DIL_BLK), lambda h: (h, 0, 0)), col(0), col(1), col(2)],
        out_specs=[out, out],
        out_shape=[jax.ShapeDtypeStruct((S, DIL_OUT), F32), jax.ShapeDtypeStruct((S, DIL_OUT), F32)],
        compiler_params=pltpu.CompilerParams(dimension_semantics=("parallel",)),
    )(bias_g, dqkv, dqkv, dqkv)


def _dil_combine(os_, ls_, name, tr=512):
    S = os_[0].shape[0]

    def body(o0, o1, o2, l0, l1, l2, out_ref, lse_ref):
        a, b, c = l0[...], l1[...], l2[...]
        m = jnp.maximum(jnp.maximum(a, b), c)
        ea, eb, ec = jnp.exp(a - m), jnp.exp(b - m), jnp.exp(c - m)
        den = ea + eb + ec
        inv = 1.0 / den
        out_ref[...] = (ea * inv) * o0[...] + (eb * inv) * o1[...] + (ec * inv) * o2[...]
        lse_ref[...] = m + jnp.log(den)

    row = pl.BlockSpec((tr, DIL_OUT), lambda i: (i, 0))
    return pl.pallas_call(
        body,
        name=name,
        grid=(S // tr,),
        in_specs=[row] * 6,
        out_specs=[row, row],
        out_shape=[jax.ShapeDtypeStruct((S, DIL_OUT), F32)] * 2,
        compiler_params=pltpu.CompilerParams(dimension_semantics=("parallel",)),
    )(*os_, *ls_)


def _dil_rowdot(dod, od, name, tr=512):
    S = dod.shape[0]

    def body(d_ref, o_ref, dd_ref):
        for h in range(DIL_HG):
            sl = slice(h * 128, (h + 1) * 128)
            sm = jnp.sum(d_ref[:, sl] * o_ref[:, sl], axis=1, keepdims=True)
            dd_ref[:, sl] = jnp.broadcast_to(sm, (tr, 128))

    row = pl.BlockSpec((tr, DIL_OUT), lambda i: (i, 0))
    return pl.pallas_call(
        body,
        name=name,
        grid=(S // tr,),
        in_specs=[row, row],
        out_specs=row,
        out_shape=jax.ShapeDtypeStruct((S, DIL_OUT), F32),
        compiler_params=pltpu.CompilerParams(dimension_semantics=("parallel",)),
    )(dod, od)


def _dil_bwd_group(dqkv, bias_g, dod, dd, lse, grads, g, dil, name):
    S = dqkv.shape[0]

    def body(bias_ref, q_ref, k_ref, v_ref, do_ref, dd_ref, lse_ref, _, out_ref):
        out_ref[1] = jnp.zeros((S, DIL_HD), F32)
        out_ref[2] = jnp.zeros((S, DIL_HD), F32)

        def block(r, n, first):
            cur = _dil_rows(r, n, 1, dil)
            both = cur if first else _dil_rows(r, n - 1, 2, dil)
            b = bias_ref[0][:, DIL_BLK:] if first else bias_ref[0]
            q, kk, vv = q_ref[cur, :].astype(BF16), k_ref[both, :].astype(BF16), v_ref[both, :].astype(BF16)
            do = do_ref[cur, :].astype(BF16)
            s = _dot(q, kk, NT) * DIL_SCALE + b
            p = jnp.exp(s - lse_ref[cur, 0:1])
            dp = _dot(do, vv, NT)
            ds = ((p * (dp - dd_ref[cur, 0:1])) * DIL_SCALE).astype(BF16)
            out_ref[0, cur, :] = _dot(ds, kk)
            out_ref[1, both, :] += _dot(ds, q, TN)
            out_ref[2, both, :] += _dot(p.astype(BF16), do, TN)

        _dil_each_block(S, dil, block)

    def col(part):
        return pl.BlockSpec((S, DIL_HD), lambda h: (0, _dil_col(g, part, h)))

    nat = pl.BlockSpec((S, DIL_HD), lambda h: (0, h))
    return pl.pallas_call(
        body,
        name=name,
        grid=(DIL_HG,),
        in_specs=[pl.BlockSpec((1, DIL_BLK, 2 * DIL_BLK), lambda h: (h, 0, 0)), col(0), col(1), col(2), nat, nat, nat, ANY],
        out_specs=pl.BlockSpec((3, S, DIL_HD), lambda h: (g, 0, h)),
        out_shape=jax.ShapeDtypeStruct(grads.shape, F32),
        input_output_aliases={7: 0},
        compiler_params=pltpu.CompilerParams(dimension_semantics=("parallel",)),
    )(bias_g, dqkv, dqkv, dqkv, dod, dd, lse, grads)


def _merge_fwd(gates, o_a, o_b, name, tr=256):
    S = o_a.shape[0]

    def body(ga_ref, gb_ref, oa_ref, ob_ref, m_ref):
        m_ref[...] = (ga_ref[...] * oa_ref[...] + gb_ref[...] * ob_ref[...]).astype(BF16)

    row = pl.BlockSpec((tr, D_MODEL), lambda i: (i, 0))
    return pl.pallas_call(
        body,
        name=name,
        grid=(S // tr,),
        in_specs=[row, pl.BlockSpec((tr, D_MODEL), lambda i: (i, 1)), row, row],
        out_specs=row,
        out_shape=jax.ShapeDtypeStruct((S, D_MODEL), BF16),
        compiler_params=pltpu.CompilerParams(dimension_semantics=("parallel",)),
    )(gates, gates, o_a, o_b)


def _merge_bwd(dmrg, gates, o_a, o_b, name, tr=256):
    S = o_a.shape[0]

    def body(dm_ref, ga_ref, gb_ref, oa_ref, ob_ref, doa_ref, dob_ref, dga_ref, dgb_ref, dba_ref, dbb_ref):
        dm, ga, gb = dm_ref[...], ga_ref[...], gb_ref[...]
        doa_ref[...] = (dm * ga).astype(BF16)
        dob_ref[...] = (dm * gb).astype(BF16)
        dga = (dm * oa_ref[...]) * (ga * (1.0 - ga))
        dgb = (dm * ob_ref[...]) * (gb * (1.0 - gb))
        dga_ref[...] = dga.astype(BF16)
        dgb_ref[...] = dgb.astype(BF16)
        sa = jnp.sum(dga, axis=0, keepdims=True)
        sb = jnp.sum(dgb, axis=0, keepdims=True)

        @pl.when(pl.program_id(0) == 0)
        def _():
            dba_ref[...] = sa
            dbb_ref[...] = sb

        @pl.when(pl.program_id(0) != 0)
        def _():
            dba_ref[...] += sa
            dbb_ref[...] += sb

    row = pl.BlockSpec((tr, D_MODEL), lambda i: (i, 0))
    row1 = pl.BlockSpec((tr, D_MODEL), lambda i: (i, 1))
    vec = pl.BlockSpec((1, D_MODEL), lambda i: (0, 0))
    outs = pl.pallas_call(
        body,
        name=name,
        grid=(S // tr,),
        in_specs=[row, row, row1, row, row],
        out_specs=[row, row, row, row, vec, vec],
        out_shape=[jax.ShapeDtypeStruct((S, D_MODEL), BF16)] * 4 + [jax.ShapeDtypeStruct((1, D_MODEL), F32)] * 2,
        compiler_params=pltpu.CompilerParams(dimension_semantics=("arbitrary",)),
    )(dmrg, gates, gates, o_a, o_b)
    return outs


CONV_TR = 512
CONV_TC = 512
N_FFC = D_FF_PAD // CONV_TC


def _conv_taps(x, before, w_ref, b_ref):
    x0 = jnp.concatenate([before, x], axis=0)
    x1 = pltpu.roll(x0, 1, 0)
    x2 = pltpu.roll(x0, 2, 0)
    u = ((b_ref[...] + w_ref[0:1, :] * x2) + w_ref[1:2, :] * x1) + w_ref[2:3, :] * x0
    return u, x0, x1, x2


def _prev_halo(tr):
    return lambda i, j: (jnp.maximum(i * (tr // 8) - 1, 0), j)


def _ffn_fwd(u0, cw, cb, name):
    S = u0.shape[0]
    tr, tc = CONV_TR, CONV_TC

    def body(up_ref, gt_ref, hup_ref, hgt_ref, wu_ref, wg_ref, bu_ref, bg_ref, a_ref):
        live = (pl.program_id(0) > 0).astype(F32)
        up = _conv_taps(up_ref[...], hup_ref[...] * live, wu_ref, bu_ref)[0][8:]
        gt = _conv_taps(gt_ref[...], hgt_ref[...] * live, wg_ref, bg_ref)[0][8:]
        a_ref[...] = ((gt * jax.nn.sigmoid(gt)) * up).astype(BF16)

    return pl.pallas_call(
        body,
        name=name,
        grid=(S // tr, N_FFC),
        in_specs=[
            pl.BlockSpec((tr, tc), lambda i, j: (i, j)),
            pl.BlockSpec((tr, tc), lambda i, j: (i, j + N_FFC)),
            pl.BlockSpec((8, tc), _prev_halo(tr)),
            pl.BlockSpec((8, tc), lambda i, j: (jnp.maximum(i * (tr // 8) - 1, 0), j + N_FFC)),
            pl.BlockSpec((8, tc), lambda i, j: (0, j)),
            pl.BlockSpec((8, tc), lambda i, j: (0, j + N_FFC)),
            pl.BlockSpec((1, tc), lambda i, j: (0, j)),
            pl.BlockSpec((1, tc), lambda i, j: (0, j + N_FFC)),
        ],
        out_specs=pl.BlockSpec((tr, tc), lambda i, j: (i, j)),
        out_shape=jax.ShapeDtypeStruct((S, D_FF_PAD), BF16),
        compiler_params=pltpu.CompilerParams(dimension_semantics=("parallel", "parallel")),
    )(u0, u0, u0, u0, cw, cw, cb, cb)


def _ffn_bwd(u0, da, cw, cb, name):
    S = u0.shape[0]
    tr, tc = CONV_TR, CONV_TC
    nrow, te = S // tr, tr + 8

    def body(up_ref, gt_ref, hup_ref, hgt_ref, nup_ref, ngt_ref, da_ref, nda_ref, wu_ref, wg_ref, bu_ref, bg_ref, du0_ref, dcw_ref, dcb_ref):
        i = pl.program_id(1)
        prev_live = (i > 0).astype(F32)
        next_live = (i < nrow - 1).astype(F32)

        def conv(x_ref, nx_ref, h_ref, w_ref, b_ref):
            x = jnp.concatenate([x_ref[...], nx_ref[...] * next_live], axis=0)
            return [t[8:] for t in _conv_taps(x, h_ref[...] * prev_live, w_ref, b_ref)]

        up, xu0, xu1, xu2 = conv(up_ref, nup_ref, hup_ref, wu_ref, bu_ref)
        gt, xg0, xg1, xg2 = conv(gt_ref, ngt_ref, hgt_ref, wg_ref, bg_ref)
        da_v = jnp.concatenate([da_ref[...], nda_ref[...] * next_live], axis=0)
        sg = jax.nn.sigmoid(gt)
        d_up = da_v * (gt * sg)
        d_gt = (da_v * up) * (sg * (1.0 + gt * (1.0 - sg)))
        tap = lax.broadcasted_iota(jnp.int32, (8, tc), 0)

        def finish(half, du, x0, x1, x2, w_ref):
            n1 = pltpu.roll(du, te - 1, 0)
            n2 = pltpu.roll(du, te - 2, 0)
            du0 = (w_ref[2:3, :] * du + w_ref[1:2, :] * n1) + w_ref[0:1, :] * n2
            du0_ref[half] = du0[:tr].astype(BF16)
            d = du[:tr]
            dcw = jnp.where(
                tap == 0,
                jnp.sum(d * x2[:tr], axis=0, keepdims=True),
                jnp.where(tap == 1, jnp.sum(d * x1[:tr], axis=0, keepdims=True), jnp.where(tap == 2, jnp.sum(d * x0[:tr], axis=0, keepdims=True), 0.0)),
            )
            dcb = jnp.sum(d, axis=0, keepdims=True)

            @pl.when(i == 0)
            def _():
                dcw_ref[half] = dcw
                dcb_ref[half] = dcb

            @pl.when(i != 0)
            def _():
                dcw_ref[half] += dcw
                dcb_ref[half] += dcb

        finish(0, d_up, xu0, xu1, xu2, wu_ref)
        finish(1, d_gt, xg0, xg1, xg2, wg_ref)

    def prev8(off):
        return pl.BlockSpec((8, tc), lambda j, i: (jnp.maximum(i * (tr // 8) - 1, 0), j + off))

    def next8(off):
        return pl.BlockSpec((8, tc), lambda j, i: (jnp.minimum((i + 1) * (tr // 8), S // 8 - 1), j + off))

    return pl.pallas_call(
        body,
        name=name,
        grid=(N_FFC, nrow),
        in_specs=[
            pl.BlockSpec((tr, tc), lambda j, i: (i, j)),
            pl.BlockSpec((tr, tc), lambda j, i: (i, j + N_FFC)),
            prev8(0),
            prev8(N_FFC),
            next8(0),
            next8(N_FFC),
            pl.BlockSpec((tr, tc), lambda j, i: (i, j)),
            next8(0),
            pl.BlockSpec((8, tc), lambda j, i: (0, j)),
            pl.BlockSpec((8, tc), lambda j, i: (0, j + N_FFC)),
            pl.BlockSpec((1, tc), lambda j, i: (0, j)),
            pl.BlockSpec((1, tc), lambda j, i: (0, j + N_FFC)),
        ],
        out_specs=[
            pl.BlockSpec((2, tr, tc), lambda j, i: (0, i, j)),
            pl.BlockSpec((2, 8, tc), lambda j, i: (0, 0, j)),
            pl.BlockSpec((2, 1, tc), lambda j, i: (0, 0, j)),
        ],
        out_shape=[
            jax.ShapeDtypeStruct((2, S, D_FF_PAD), BF16),
            jax.ShapeDtypeStruct((2, 8, D_FF_PAD), F32),
            jax.ShapeDtypeStruct((2, 1, D_FF_PAD), F32),
        ],
        compiler_params=pltpu.CompilerParams(dimension_semantics=("parallel", "arbitrary")),
    )(u0, u0, u0, u0, u0, u0, da, da, cw, cw, cb, cb)


ADAMW_BLOCK_BYTES = 3 << 20


def _adamw(w, g, m, v, name):
    R, C = w.shape
    fits = [t for t in range(8, R + 1, 8) if R % t == 0 and t * C * 4 <= ADAMW_BLOCK_BYTES]
    tr = max(fits) if fits else R

    def body(w_ref, g_ref, m_ref, v_ref, d_ref, nm_ref, nv_ref):
        gv = g_ref[...]
        nm = ADAM_B1 * m_ref[...] + (1.0 - ADAM_B1) * gv
        nv = ADAM_B2 * v_ref[...] + (1.0 - ADAM_B2) * (gv * gv)
        m_hat = nm / (1.0 - ADAM_B1**ADAM_STEP)
        v_hat = nv / (1.0 - ADAM_B2**ADAM_STEP)
        d_ref[...] = -ADAM_LR * (m_hat / (jnp.sqrt(v_hat) + ADAM_EPS) + ADAM_WD * w_ref[...])
        nm_ref[...] = nm
        nv_ref[...] = nv

    blk = pl.BlockSpec((tr, C), lambda i: (i, 0))
    return pl.pallas_call(
        body,
        name=name,
        grid=(R // tr,),
        in_specs=[blk] * 4,
        out_specs=[blk] * 3,
        out_shape=[jax.ShapeDtypeStruct((R, C), F32)] * 3,
        compiler_params=pltpu.CompilerParams(dimension_semantics=("parallel",)),
    )(w, g, m, v)


ANY = pl.BlockSpec(memory_space=pl.ANY)


def _row_tile(rows):
    return max(t for t in range(16, 353, 16) if rows % t == 0)


def _pair_add(g, recv, core, name):
    _, R, C = g.shape
    tr = _row_tile(R)

    def body(core_ref, g_ref, r_ref, o_ref):
        o_ref[...] = (g_ref[...].astype(F32) + r_ref[...].astype(F32)).astype(o_ref.dtype)

    return pl.pallas_call(
        body,
        name=name,
        grid_spec=pltpu.PrefetchScalarGridSpec(
            num_scalar_prefetch=1,
            grid=(N_CHIP, R // tr),
            in_specs=[
                pl.BlockSpec((1, tr, C), lambda k, i, core_ref: (2 * k + core_ref[0], i, 0)),
                pl.BlockSpec((1, tr, C), lambda k, i, core_ref: (k, i, 0)),
            ],
            out_specs=pl.BlockSpec((1, tr, C), lambda k, i, core_ref: (k, i, 0)),
        ),
        out_shape=jax.ShapeDtypeStruct((N_CHIP, R, C), g.dtype),
        compiler_params=pltpu.CompilerParams(dimension_semantics=("parallel", "parallel")),
    )(core, g, recv)


HBM = pl.BlockSpec(memory_space=pltpu.HBM)
SEM = pl.BlockSpec(memory_space=pltpu.SEMAPHORE)
EFFECT = pltpu.SideEffectType.DATAFLOW_SIDE_EFFECTING
RELATIONS = tuple((dx, dy, dc) for dx in (0, 1) for dy in (0, 1) for dc in (0, 1))[1:]


def _related(rel):
    x, y, c = lax.axis_index("x"), lax.axis_index("y"), lax.axis_index("c")
    return (1 - x if rel[0] else x, 1 - y if rel[1] else y, 1 - c if rel[2] else c)


def _dev_index(pos):
    return 4 * pos[0] + 2 * pos[1] + pos[2]


def _peers(chips):
    if chips:
        return [r for r in RELATIONS if not r[2]], N_CHIP, lambda pos: 2 * pos[0] + pos[1]
    return list(RELATIONS), N_DEV, _dev_index


def _exchange_start(srcs, by_slot, after, name, chips=False):
    n = len(srcs)
    extra = [] if after is None else [after]
    rels, slots, slot_of = _peers(chips)
    lands = [lax.empty((slots,) + (s.shape[1:] if by_slot else s.shape), s.dtype) for s in srcs]
    nsem = len(rels) * n

    def body(*refs):
        src_refs, land_refs = refs[:n], refs[n : 2 * n]
        send_sems, recv_sems = refs[2 * n + len(extra)], refs[2 * n + len(extra) + 1]
        token = refs[-1]
        me = slot_of(_related((0, 0, 0)))
        for a in range(n):
            for k, rel in enumerate(rels):
                peer = _related(rel)
                pltpu.make_async_remote_copy(
                    src_ref=src_refs[a].at[slot_of(peer)] if by_slot else src_refs[a],
                    dst_ref=land_refs[a].at[me],
                    send_sem=send_sems.at[len(rels) * a + k],
                    recv_sem=recv_sems.at[len(rels) * a + k],
                    device_id=peer,
                    device_id_type=MESH,
                ).start()
        token[...] = jnp.zeros_like(token)

    def hbm(a):
        return pltpu.HBM(a.shape, a.dtype)

    outs = pl.pallas_call(
        body,
        name=name,
        out_shape=(
            pltpu.SemaphoreType.DMA((nsem,)),
            pltpu.SemaphoreType.DMA((nsem,)),
            *[hbm(s) for s in srcs],
            *[hbm(l) for l in lands],
            jax.ShapeDtypeStruct((8, 128), F32),
        ),
        in_specs=[HBM] * (2 * n) + [ANY] * len(extra),
        out_specs=(SEM, SEM, *[HBM] * (2 * n), pl.BlockSpec(memory_space=pltpu.VMEM)),
        input_output_aliases={i: 2 + i for i in range(2 * n)},
        compiler_params=pltpu.CompilerParams(has_side_effects=EFFECT),
    )(*[pltpu.with_memory_space_constraint(a, pltpu.HBM) for a in list(srcs) + lands], *extra)
    return (outs[0], outs[1], list(outs[2 : 2 + n]), list(outs[2 + n : 2 + 2 * n])), outs[-1]


def _exchange_wait(handle, by_slot, after, name, chips=False):
    send_sems, recv_sems, srcs, lands = handle
    n = len(srcs)
    rels = _peers(chips)[0]

    def body(*refs):
        src_refs, land_refs = refs[:n], refs[n : 2 * n]
        s_sems, r_sems = refs[2 * n], refs[2 * n + 1]
        for a in range(n):
            for k, rel in enumerate(rels):
                copy = pltpu.make_async_remote_copy(
                    src_ref=src_refs[a].at[0] if by_slot else src_refs[a],
                    dst_ref=land_refs[a].at[0],
                    send_sem=s_sems.at[len(rels) * a + k],
                    recv_sem=r_sems.at[len(rels) * a + k],
                    device_id=_related(rel),
                    device_id_type=MESH,
                )
                copy.wait_send()
                copy.wait_recv()

    outs = pl.pallas_call(
        body,
        name=name,
        out_shape=tuple(pltpu.HBM(a.shape, a.dtype) for a in srcs + lands),
        in_specs=[HBM] * (2 * n) + [SEM, SEM, ANY],
        out_specs=tuple([HBM] * (2 * n)),
        input_output_aliases={i: i for i in range(2 * n)},
        compiler_params=pltpu.CompilerParams(has_side_effects=EFFECT),
    )(*srcs, *lands, send_sems, recv_sems, after)
    return list(outs[:n]), list(outs[n:])


def _pair_start(g, name):
    land = lax.empty((N_CHIP,) + g.shape[1:], g.dtype)

    def body(g_ref, land_ref, send_sems, recv_sems, g_thru, land_thru, token):
        c = lax.axis_index("c")
        for k in range(N_CHIP):
            pltpu.make_async_remote_copy(
                src_ref=g_ref.at[2 * k + (1 - c)],
                dst_ref=land_ref.at[k],
                send_sem=send_sems.at[k],
                recv_sem=recv_sems.at[k],
                device_id=_related((0, 0, 1)),
                device_id_type=MESH,
            ).start()
        token[...] = jnp.zeros_like(token)

    outs = pl.pallas_call(
        body,
        name=name,
        out_shape=(
            pltpu.SemaphoreType.DMA((N_CHIP,)),
            pltpu.SemaphoreType.DMA((N_CHIP,)),
            pltpu.HBM(g.shape, g.dtype),
            pltpu.HBM(land.shape, land.dtype),
            jax.ShapeDtypeStruct((8, 128), F32),
        ),
        in_specs=[HBM, HBM],
        out_specs=(SEM, SEM, HBM, HBM, pl.BlockSpec(memory_space=pltpu.VMEM)),
        input_output_aliases={0: 2, 1: 3},
        compiler_params=pltpu.CompilerParams(has_side_effects=EFFECT),
    )(pltpu.with_memory_space_constraint(g, pltpu.HBM), pltpu.with_memory_space_constraint(land, pltpu.HBM))
    return outs[:4], outs[4]


def _pair_wait(handle, after, name):
    send_sems, recv_sems, g, land = handle

    def body(g_ref, land_ref, s_sems, r_sems, _, g_out, land_out):
        for k in range(N_CHIP):
            copy = pltpu.make_async_remote_copy(
                src_ref=g_ref.at[0],
                dst_ref=land_ref.at[0],
                send_sem=s_sems.at[k],
                recv_sem=r_sems.at[k],
                device_id=_related((0, 0, 1)),
                device_id_type=MESH,
            )
            copy.wait_send()
            copy.wait_recv()

    return pl.pallas_call(
        body,
        name=name,
        out_shape=(pltpu.HBM(g.shape, g.dtype), pltpu.HBM(land.shape, land.dtype)),
        in_specs=[HBM, HBM, SEM, SEM, ANY],
        out_specs=(HBM, HBM),
        input_output_aliases={0: 0, 1: 1},
        compiler_params=pltpu.CompilerParams(has_side_effects=EFFECT),
    )(g, land, send_sems, recv_sems, after)


NEAR = ((0, 0, 1), (1, 0, 0), (0, 1, 0), (1, 1, 0))


def _gather2_start(blocks, name):
    n = len(blocks)
    lands = [lax.empty((N_DEV,) + b.shape, b.dtype) for b in blocks]

    def body(*refs):
        src_refs, land_refs = refs[:n], refs[n : 2 * n]
        send_sems, recv_sems, token = refs[2 * n], refs[2 * n + 1], refs[-1]
        me = _dev_index(_related((0, 0, 0)))
        for a in range(n):
            for k, rel in enumerate(NEAR):
                pltpu.make_async_remote_copy(
                    src_ref=src_refs[a],
                    dst_ref=land_refs[a].at[me],
                    send_sem=send_sems.at[len(NEAR) * a + k],
                    recv_sem=recv_sems.at[len(NEAR) * a + k],
                    device_id=_related(rel),
                    device_id_type=MESH,
                ).start()
        token[...] = jnp.zeros_like(token)

    nsem = len(NEAR) * n
    outs = pl.pallas_call(
        body,
        name=name,
        out_shape=(
            pltpu.SemaphoreType.DMA((nsem,)),
            pltpu.SemaphoreType.DMA((nsem,)),
            *[pltpu.HBM(a.shape, a.dtype) for a in list(blocks) + lands],
            jax.ShapeDtypeStruct((8, 128), F32),
        ),
        in_specs=[HBM] * (2 * n),
        out_specs=(SEM, SEM, *[HBM] * (2 * n), pl.BlockSpec(memory_space=pltpu.VMEM)),
        input_output_aliases={i: 2 + i for i in range(2 * n)},
        compiler_params=pltpu.CompilerParams(has_side_effects=EFFECT),
    )(*[pltpu.with_memory_space_constraint(a, pltpu.HBM) for a in list(blocks) + lands])
    return (outs[0], outs[1], list(outs[2 : 2 + n]), list(outs[2 + n : 2 + 2 * n])), outs[-1]


def _gather2_forward(handle, after, name):
    send1, recv1, srcs, lands = handle
    n = len(srcs)

    def body(*refs):
        src_refs, land_refs = refs[:n], refs[n : 2 * n]
        s1, r1 = refs[2 * n], refs[2 * n + 1]
        s2, r2 = refs[-2], refs[-1]
        sibling = _related(NEAR[0])
        for a in range(n):
            for k, rel in enumerate(NEAR):
                first = pltpu.make_async_remote_copy(
                    src_ref=src_refs[a],
                    dst_ref=land_refs[a].at[0],
                    send_sem=s1.at[len(NEAR) * a + k],
                    recv_sem=r1.at[len(NEAR) * a + k],
                    device_id=_related(rel),
                    device_id_type=MESH,
                )
                first.wait_send()
                first.wait_recv()
                if k:
                    slot = land_refs[a].at[_dev_index(_related(rel))]
                    pltpu.make_async_remote_copy(
                        src_ref=slot,
                        dst_ref=slot,
                        send_sem=s2.at[3 * a + k - 1],
                        recv_sem=r2.at[3 * a + k - 1],
                        device_id=sibling,
                        device_id_type=MESH,
                    ).start()

    outs = pl.pallas_call(
        body,
        name=name,
        out_shape=(
            *[pltpu.HBM(a.shape, a.dtype) for a in srcs + lands],
            pltpu.SemaphoreType.DMA((3 * n,)),
            pltpu.SemaphoreType.DMA((3 * n,)),
        ),
        in_specs=[HBM] * (2 * n) + [SEM, SEM, ANY],
        out_specs=(*[HBM] * (2 * n), SEM, SEM),
        input_output_aliases={i: i for i in range(2 * n)},
        compiler_params=pltpu.CompilerParams(has_side_effects=EFFECT),
    )(*srcs, *lands, send1, recv1, after)
    return outs[-2], outs[-1], list(outs[:n]), list(outs[n : 2 * n])


def _gather2_wait(handle, name):
    send2, recv2, srcs, lands = handle
    n = len(srcs)

    def body(*refs):
        land_refs = refs[n : 2 * n]
        s2, r2 = refs[2 * n], refs[2 * n + 1]
        for a in range(n):
            for j in range(3):
                passed = pltpu.make_async_remote_copy(
                    src_ref=land_refs[a].at[0],
                    dst_ref=land_refs[a].at[0],
                    send_sem=s2.at[3 * a + j],
                    recv_sem=r2.at[3 * a + j],
                    device_id=_related(NEAR[0]),
                    device_id_type=MESH,
                )
                passed.wait_send()
                passed.wait_recv()

    outs = pl.pallas_call(
        body,
        name=name,
        out_shape=tuple(pltpu.HBM(a.shape, a.dtype) for a in srcs + lands),
        in_specs=[HBM] * (2 * n) + [SEM, SEM],
        out_specs=tuple([HBM] * (2 * n)),
        input_output_aliases={i: i for i in range(2 * n)},
        compiler_params=pltpu.CompilerParams(has_side_effects=EFFECT),
    )(*srcs, *lands, send2, recv2)
    return list(outs[:n]), list(outs[n:])


def _slot_sum(parts, name, keep=None):
    n, R, C = parts.shape
    if keep is not None:
        tc = 256
        rows = sum(size for _, size in keep)

        def kept(p_ref, o_ref):
            acc = p_ref[0].astype(F32)
            for k in range(1, n):
                acc = acc + p_ref[k].astype(F32)
            off = 0
            for start, size in keep:
                o_ref[off : off + size, :] = acc[start : start + size, :]
                off += size

        return pl.pallas_call(
            kept,
            name=name,
            grid=(C // tc,),
            in_specs=[pl.BlockSpec((n, R, tc), lambda j: (0, 0, j))],
            out_specs=pl.BlockSpec((rows, tc), lambda j: (0, j)),
            out_shape=jax.ShapeDtypeStruct((rows, C), F32),
            compiler_params=pltpu.CompilerParams(dimension_semantics=("parallel",)),
        )(parts)
    tr = _row_tile(R) if R % 16 == 0 else R

    def body(p_ref, o_ref):
        acc = p_ref[0].astype(F32)
        for k in range(1, n):
            acc = acc + p_ref[k].astype(F32)
        o_ref[...] = acc

    return pl.pallas_call(
        body,
        name=name,
        grid=(R // tr,),
        in_specs=[pl.BlockSpec((n, tr, C), lambda i: (0, i, 0))],
        out_specs=pl.BlockSpec((tr, C), lambda i: (i, 0)),
        out_shape=jax.ShapeDtypeStruct((R, C), F32),
        compiler_params=pltpu.CompilerParams(dimension_semantics=("parallel",)),
    )(parts)


W_IN_TC = 256
W_IN_BOUNDS = (0, LAT, LAT + 3 * DIL_QKV, LAT + 3 * DIL_QKV + D_MODEL, D_IN)


def _dqkv_chunks():
    return [((g * 3 + part) * DIL_OUT, LAT + part * DIL_QKV + g * DIL_OUT) for g in range(DIL_GROUPS) for part in range(3)]


def _w_in_regroup(slots, after, name):
    tc = W_IN_TC

    def body(s_ref, _, lat_ref, dqkv_ref, g_ref, buf):
        for j in range(N_DEV):
            buf[j * IN_ROWS : (j + 1) * IN_ROWS, :] = s_ref[j].astype(F32)[:IN_ROWS, :]
        lat_ref[:LAT, :] = buf[:LAT, :].astype(BF16)
        lat_ref[LAT:, :] = jnp.zeros((LAT_PAD - LAT, tc), BF16)
        for dst, src in _dqkv_chunks():
            dqkv_ref[dst : dst + DIL_OUT, :] = buf[src : src + DIL_OUT, :].astype(BF16)
        g_ref[...] = buf[W_IN_BOUNDS[2] :, :].astype(BF16)

    def col(rows):
        return pl.BlockSpec((rows, tc), lambda k: (0, k))

    return pl.pallas_call(
        body,
        name=name,
        grid=(D_MODEL // tc,),
        in_specs=[pl.BlockSpec((N_DEV, IN_ROWS_PAD, tc), lambda k: (0, 0, k)), pl.BlockSpec((8, 128), lambda k: (0, 0))],
        out_specs=[col(LAT_PAD), col(3 * DIL_QKV), col(2 * D_MODEL)],
        out_shape=[
            jax.ShapeDtypeStruct((LAT_PAD, D_MODEL), BF16),
            jax.ShapeDtypeStruct((3 * DIL_QKV, D_MODEL), BF16),
            jax.ShapeDtypeStruct((2 * D_MODEL, D_MODEL), BF16),
        ],
        scratch_shapes=[pltpu.VMEM((D_IN, tc), F32)],
        compiler_params=pltpu.CompilerParams(dimension_semantics=("parallel",)),
    )(slots, after)


def _w_in_grad_regroup(g_lat, g_dqkv, g_ga, g_gb, name):
    tc = W_IN_TC

    def body(lat_ref, dqkv_ref, ga_ref, gb_ref, o_ref, buf):
        b = W_IN_BOUNDS
        buf[b[0] : b[1], :] = lat_ref[:LAT, :].astype(F32)
        for dst, src in _dqkv_chunks():
            buf[src : src + DIL_OUT, :] = dqkv_ref[dst : dst + DIL_OUT, :].astype(F32)
        buf[b[2] : b[3], :] = ga_ref[...].astype(F32)
        buf[b[3] : b[4], :] = gb_ref[...].astype(F32)
        fill = jnp.zeros((IN_ROWS_PAD - IN_ROWS, tc), F32)
        for j in range(N_DEV):
            o_ref[j] = jnp.concatenate([buf[j * IN_ROWS : (j + 1) * IN_ROWS, :], fill], axis=0).astype(BF16)

    def col(rows):
        return pl.BlockSpec((rows, tc), lambda k: (0, k))

    return pl.pallas_call(
        body,
        name=name,
        grid=(D_MODEL // tc,),
        in_specs=[col(LAT_PAD), col(3 * DIL_QKV), col(D_MODEL), col(D_MODEL)],
        out_specs=pl.BlockSpec((N_DEV, IN_ROWS_PAD, tc), lambda k: (0, 0, k)),
        out_shape=jax.ShapeDtypeStruct((N_DEV, IN_ROWS_PAD, D_MODEL), BF16),
        scratch_shapes=[pltpu.VMEM((D_IN, tc), F32)],
        compiler_params=pltpu.CompilerParams(dimension_semantics=("parallel",)),
    )(g_lat, g_dqkv, g_ga, g_gb)


def _ffn_pad(a, axis):
    a = jnp.moveaxis(a, axis, -1)
    g = a.reshape(a.shape[:-1] + (2 * N_DEV, FF_GROUP))
    g = jnp.pad(g, [(0, 0)] * (g.ndim - 1) + [(0, FF_GROUP_PAD - FF_GROUP)])
    return jnp.moveaxis(g.reshape(a.shape[:-1] + (2 * D_FF_PAD,)), -1, axis)


def _ffn_unpad(a, axis):
    a = jnp.moveaxis(a, axis, -1)
    g = a.reshape(a.shape[:-1] + (2 * N_DEV, FF_GROUP_PAD))[..., :FF_GROUP]
    return jnp.moveaxis(g.reshape(a.shape[:-1] + (2 * D_FF,)), -1, axis)


MISC = (("w_o_mla", (256, 1024)), ("w_o_dil", (256, 512)), ("w_uq", (192, 512)), ("w_ukv", (256, 256)))
BIG_WEIGHTS = ("w_in", "w_up", "w_down", "w_out") + tuple(n for n, _ in MISC)


def _exchange_blocks(w):
    def t(a):
        return a.astype(BF16).T

    up = t(w["w_up"]).reshape(2, FF_GROUP, D_MODEL)
    return [
        jnp.pad(t(w["w_in"]), ((0, IN_ROWS_PAD - IN_ROWS), (0, 0))),
        jnp.pad(up, ((0, 0), (0, FF_GROUP_PAD - FF_GROUP), (0, 0))).reshape(2 * FF_GROUP_PAD, D_MODEL),
        jnp.pad(w["w_down"].astype(BF16), ((0, FF_GROUP_PAD - FF_GROUP), (0, 0))),
        w["w_out"].astype(BF16),
        jnp.concatenate([t(w[n]).reshape(-1, D_MODEL) for n, _ in MISC], axis=0),
    ]


def _misc_split(misc):
    out, off = {}, 0
    for n, (r, c) in MISC:
        rows = r * c // D_MODEL
        out[n] = misc[..., off : off + rows, :].reshape(misc.shape[:-2] + (r, c))
        off += rows
    return out


def _small_matrices(g_misc):
    misc = _misc_split(g_misc)
    uq_t = jnp.pad(misc["w_uq"], ((0, 0), (0, HEAD_PAD - QK_NOPE - QK_ROPE), (0, 0)))
    return {
        "uq_t": uq_t.reshape(MLA_HEADS * HEAD_PAD, Q_LORA),
        "ukv_t": misc["w_ukv"].reshape(MLA_HEADS * HEAD_PAD, KV_LORA),
        "o_mla_t": misc["w_o_mla"].reshape(D_MODEL, MLA_HEADS * V_HEAD),
        "o_dil_t": misc["w_o_dil"].reshape(D_MODEL, DIL_OUT),
    }


def _small_grad_blocks(g):
    uq_t = g["uq_t"].reshape(MLA_HEADS, HEAD_PAD, Q_LORA)[:, : QK_NOPE + QK_ROPE]
    misc = {"w_o_mla": g["o_mla_t"], "w_o_dil": g["o_dil_t"], "w_uq": uq_t, "w_ukv": g["ukv_t"]}
    return [
        g["w_out"].reshape(N_DEV, -1, D_MODEL),
        jnp.concatenate([misc[n].reshape(N_DEV, -1, D_MODEL) for n, _ in MISC], axis=1),
    ]


def _grad_shards(sums):
    s_in, s_out, s_misc, s_up, s_down = sums
    out = {
        "w_in": s_in.T,
        "w_up": s_up.T,
        "w_down": s_down,
        "w_out": s_out,
    }
    out.update({n: v.T for n, v in _misc_split(s_misc).items()})
    return out


def _local_step(x, h, tgt, wt, conv_w, small, small_matrices, ffn_weight, send_ffn_grads, send_small_grads, send_w_in_grads, forward_w_in_grads):
    S = x.shape[0]
    lat_t, dqkv_t, g_t = wt
    cw = jnp.pad(_ffn_pad(conv_w, 1), ((0, 5), (0, 0)))
    cb = _ffn_pad(small["conv_b"], 1)
    cos_t, sin_t = _rope_tables(S)
    bias = _dil_bias()
    g1, g2, g3 = small["attn_norm_g"], small["ffn_norm_g"], small["final_norm_g"]
    gq, gkv = small["q_norm_g"], small["kv_norm_g"]

    lat = _mm(h, lat_t, "nt", F32, 1024, LAT_PAD, D_MODEL, "proj_lat")
    dqkv = _mm(h, dqkv_t, "nt", F32, 1024, 1536, D_MODEL, "proj_dqkv")
    gates = _mm(h, g_t, "nt", F32, 1024, 1024, D_MODEL, "proj_gates", bias=small["b_gate"], act="sigmoid")
    sm = small_matrices(gates)
    uq_t, ukv_t, o_mla_t, o_dil_t = sm["uq_t"], sm["ukv_t"], sm["o_mla_t"], sm["o_dil_t"]
    cqn, ckvn, kpe = _mla_prep1(lat, gq, gkv, cos_t, sin_t, "mla_prep1")
    q_raw = _mm(cqn, uq_t, "nt", F32, 1024, 1024, Q_LORA, "mla_uq")
    kv = _mm(ckvn, ukv_t, "nt", BF16, 1024, 1024, KV_LORA, "mla_ukv")
    q_att, k_att = _mla_prep2(q_raw, kv, kpe, cos_t, sin_t, "mla_prep2")
    o, lse = _flash2_fwd(q_att, k_att, kv, "mla_flash_fwd")
    o_a = _mm(o, o_mla_t, "nt", F32, 1024, 1024, MLA_HEADS * V_HEAD, "mla_out")

    d_os, d_ls = [], []
    for g, (_, dil) in enumerate(DIL_PATTERNS):
        og, lg = _dil_fwd_group(dqkv, bias[g], g, dil, f"dil_fwd_{g}")
        d_os.append(og)
        d_ls.append(lg)
    od, dil_lse = _dil_combine(d_os, d_ls, "dil_combine")
    o_b = _mm(od, o_dil_t, "nt", F32, 1024, 1024, DIL_OUT, "dil_out")

    mrg = _merge_fwd(gates, o_a, o_b, "merge_fwd")
    w_out = ffn_weight("w_out", mrg)
    x1, h2 = _mm_res_rms(mrg, w_out, x, g2, "mix_out")
    up_t = ffn_weight("up_t", h2)
    u0 = _mm(h2, up_t, "nt", F32, 2048, 1024, D_MODEL, "ffn_up")
    a = _ffn_fwd(u0, cw, cb, "ffn_conv_fwd")
    w_down = ffn_weight("w_down", a)
    x2 = _mm(a, w_down, "nn", F32, 1024, 1024, D_FF_PAD // 2, "ffn_down", res=x1)
    loss_part, dx2, dx2b, dg3 = _final_loss(x2, g3, tgt, "final_loss")

    da = _mm(dx2b, w_down, "nt", F32, 1024, D_FF_PAD // 4, D_MODEL, "ffn_down_dx")
    gw_down = _mm(a, dx2b, "tn", BF16, 512, D_MODEL, S, "ffn_down_dw")
    du0, dcw, dcb = _ffn_bwd(u0, da, cw, cb, "ffn_conv_bwd")
    du0 = du0.reshape(2 * S, D_FF_PAD)
    gw_up_t = _mm(du0, h2, "tn", BF16, 512, D_MODEL, S, "ffn_up_dw", a_halves=2)
    sent = send_ffn_grads(gw_up_t, gw_down)
    dh2 = _mm(du0, up_t, "nn", F32, 1024, 1024, D_FF_PAD // 2, "ffn_up_dx", a_halves=2)
    dx1, dx1b, dg2 = _rms_bwd(dh2, x1, g2 + sent, dx2, "rms_ffn_bwd")

    dmrg = _mm(dx1b, w_out, "nt", F32, 1024, 1024, D_MODEL, "mix_out_dx")
    gw_out = _mm(mrg, dx1b, "tn", BF16, 512, D_MODEL, S, "mix_out_dw")
    do_a, do_b, dga, dgb, dba, dbb = _merge_bwd(dmrg, gates, o_a, o_b, "merge_bwd")

    do = _mm(do_a, o_mla_t, "nn", BF16, 1024, 1024, D_MODEL, "mla_out_dx")
    gw_o_mla_t = _mm(do_a, o, "tn", BF16, 1024, 1024, 1024, "mla_out_dw")
    dod = _mm(do_b, o_dil_t, "nn", F32, 1024, DIL_OUT, D_MODEL, "dil_out_dx")
    gw_o_dil_t = _mm(do_b, od, "tn", BF16, 1024, DIL_OUT, 1024, "dil_out_dw")

    delta = _flash_delta(do, o, "mla_flash_delta")
    lse_row = lse[:, :, 0][:, None, :]
    delta_row = delta[:, :MLA_HEADS].T[:, None, :]
    dq_att, dk_att, dv = _flash2_bwd(q_att, k_att, kv, do, lse_row, delta_row, "mla_flash_bwd")
    dq_raw, dkv, dkpe = _mla_post(dq_att, dk_att, dv, cos_t, sin_t, "mla_post")
    dcqn = _mm(dq_raw, uq_t, "nn", F32, 1024, Q_LORA, MLA_HEADS * HEAD_PAD, "mla_uq_dx")
    gw_uq_t = _mm(dq_raw, cqn, "tn", BF16, 1024, Q_LORA, 1024, "mla_uq_dw")
    dckvn = _mm(dkv, ukv_t, "nn", F32, 1024, KV_LORA, MLA_HEADS * HEAD_PAD, "mla_ukv_dx")
    gw_ukv_t = _mm(dkv, ckvn, "tn", BF16, 1024, KV_LORA, 1024, "mla_ukv_dw")
    sent = send_small_grads({"uq_t": gw_uq_t, "ukv_t": gw_ukv_t, "o_mla_t": gw_o_mla_t, "o_dil_t": gw_o_dil_t, "w_out": gw_out})
    dlat, dgq, dgkv = _lat_bwd(dcqn, dckvn, dkpe, lat, gq + sent, gkv, "lat_bwd")

    dd = _dil_rowdot(dod, od, "dil_rowdot")
    ddqkv = lax.empty((3 * DIL_GROUPS, S, DIL_OUT), F32)
    for g, (_, dil) in enumerate(DIL_PATTERNS):
        ddqkv = _dil_bwd_group(dqkv, bias[g], dod, dd, dil_lse, ddqkv, g, dil, f"dil_bwd_{g}")
    gw_lat_t = _mm(dlat, h, "tn", BF16, LAT_PAD, 1024, S, "proj_lat_dw")
    gw_dqkv_t = _mm(ddqkv.reshape(3 * DIL_GROUPS * S, DIL_OUT), h, "tn", BF16, 512, 1024, S, "proj_dqkv_dw", a_halves=3 * DIL_GROUPS)
    gw_ga_t = _mm(dga, h, "tn", BF16, 512, D_MODEL, S, "proj_ga_dw")
    gw_gb_t = _mm(dgb, h, "tn", BF16, 512, D_MODEL, S, "proj_gb_dw")
    sent = send_w_in_grads(gw_lat_t, gw_dqkv_t, gw_ga_t, gw_gb_t)
    dh = _mm(dlat + sent.astype(BF16), lat_t, "nn", F32, 1024, 1024, LAT_PAD, "proj_lat_dx")
    dh = _stacked_mm(ddqkv, dqkv_t, dh, forward_w_in_grads(dh), "proj_dqkv_dx")
    dh = _mm(dga, g_t, "nn", F32, 1024, 1024, D_MODEL, "proj_ga_dx", res=dh)
    grad_x, dg1 = _mm_rms_bwd(dgb, g_t, 1, dh, x, g1, dx1, "proj_gb_dx_rms_attn_bwd")

    small_grads = {
        "attn_norm_g": dg1,
        "b_gate": jnp.concatenate([dba, dbb], axis=1),
        "q_norm_g": dgq,
        "kv_norm_g": dgkv,
        "ffn_norm_g": dg2,
        "conv_b": _ffn_unpad(jnp.concatenate([dcb[0], dcb[1]], axis=1), 1),
        "final_norm_g": dg3,
        "conv_w": _ffn_unpad(jnp.concatenate([dcw[0, :3], dcw[1, :3]], axis=1), 1),
    }
    return loss_part, grad_x, small_grads


SMALL_ORDER = ("attn_norm_g", "b_gate", "q_norm_g", "kv_norm_g", "ffn_norm_g", "conv_b", "final_norm_g", "conv_w")
WEIGHT_ORDER = (
    "attn_norm_g", "w_in", "b_gate", "q_norm_g", "w_uq", "kv_norm_g", "w_ukv", "w_o_mla", "w_o_dil", "w_out",
    "ffn_norm_g", "w_up", "conv_w", "conv_b", "w_down", "final_norm_g",
)


def kernel(x, attn_norm_g, w_in, b_gate, q_norm_g, w_uq, kv_norm_g, w_ukv, w_o_mla, w_o_dil, w_out, ffn_norm_g, w_up, conv_w, conv_b, w_down, final_norm_g, loss_target, m_attn_norm_g, m_w_in, m_b_gate, m_q_norm_g, m_w_uq, m_kv_norm_g, m_w_ukv, m_w_o_mla, m_w_o_dil, m_w_out, m_ffn_norm_g, m_w_up, m_conv_w, m_conv_b, m_w_down, m_final_norm_g, v_attn_norm_g, v_w_in, v_b_gate, v_q_norm_g, v_w_uq, v_kv_norm_g, v_w_ukv, v_w_o_mla, v_w_o_dil, v_w_out, v_ffn_norm_g, v_w_up, v_conv_w, v_conv_b, v_w_down, v_final_norm_g):
    env = dict(locals())
    dev = 4 * lax.axis_index("x") + 2 * lax.axis_index("y") + lax.axis_index("c")
    core = lax.axis_index("c").astype(jnp.int32).reshape(1)

    def two_d(a):
        return a.reshape(-1, a.shape[-1])

    w = {n: two_d(env[n]) for n in WEIGHT_ORDER}
    m = {n: two_d(env["m_" + n]) for n in WEIGHT_ORDER}
    v = {n: two_d(env["v_" + n]) for n in WEIGHT_ORDER}

    chip = 2 * lax.axis_index("x") + lax.axis_index("y")

    def own_slot_in(lands, own, slot=dev):
        return [lax.dynamic_update_slice(l, o[None], (slot, 0, 0)) for l, o in zip(lands, own)]

    b_in = _exchange_blocks(w)[0]
    r, c = CONV_SHARD
    conv = jnp.pad(w["conv_w"].reshape(-1), (0, 8 * SMALL_COLS - r * c)).reshape(8, SMALL_COLS)
    first_level, token = _gather2_start([b_in, conv], "ag_w_in_start")
    tied = {n: w[n] + token[0, 0] for n in BIG_WEIGHTS}
    _, b_up, b_down, b_out, b_misc = _exchange_blocks(tied)
    h = _rms_fwd(x[0], w["attn_norm_g"] + token[0, 0], "rms_attn")
    prepared = b_up[:1, :1] + b_down[:1, :1] + b_out[:1, :1] + b_misc[:1, :1] + h[:1, :1]
    own, lands = _gather2_wait(_gather2_forward(first_level, prepared, "ag_w_in_forward"), "ag_w_in_wait")
    g_in, conv = own_slot_in(lands, own)
    misc_gather, started = _exchange_start([b_misc], False, conv, "ag_small_start")
    ffn_gathers, started2 = {}, started
    for key, block in (("w_out", b_out), ("up_t", b_up), ("w_down", b_down)):
        ffn_gathers[key], started2 = _exchange_start([block], False, started2, f"ag_{key}_start")
    wt = _w_in_regroup(g_in, started2, "w_in_regroup")
    conv = conv.reshape(N_DEV, 8 * SMALL_COLS)[:, : r * c].reshape(N_DEV, r, c)
    conv_w_full = conv.transpose(1, 0, 2).reshape(r, N_DEV * c)
    small = {n: w[n] for n in SMALL_ORDER if n != "conv_w"}

    def small_matrices(after):
        own, lands = _exchange_wait(misc_gather, False, after, "ag_small_wait")
        return _small_matrices(own_slot_in(lands, own)[0])

    def ffn_weight(key, after):
        own, lands = _exchange_wait(ffn_gathers[key], False, after, f"ag_{key}_wait")
        return own_slot_in(lands, own)[0].reshape(-1, D_MODEL)

    reduces = {}

    def send_ffn_grads(gw_up_t, gw_down):
        blocks = [gw_up_t.reshape(N_DEV, 2 * FF_GROUP_PAD, D_MODEL), gw_down.reshape(N_DEV, FF_GROUP_PAD, D_MODEL)]
        reduces["ffn"], token = _exchange_start(blocks, True, None, "rs_ffn_start")
        return token[0, 0]

    def send_small_grads(g):
        reduces["small"], token = _exchange_start(_small_grad_blocks(g), True, None, "rs_small_start")
        return token[0, 0]

    def send_w_in_grads(g_lat, g_dqkv, g_ga, g_gb):
        e_in = _w_in_grad_regroup(g_lat, g_dqkv, g_ga, g_gb, "w_in_grad_regroup")
        reduces["pair"], token = _pair_start(e_in, "rs_w_in_pair_start")
        return token[0, 0]

    def forward_w_in_grads(after):
        e_in, recv = _pair_wait(reduces.pop("pair"), after, "rs_w_in_pair_wait")
        pair = _pair_add(e_in, recv, core, "rs_w_in_pair_add")
        reduces["w_in"], token = _exchange_start([pair], True, None, "rs_w_in_start", chips=True)
        return token

    loss_part, grad_x, small_grads = _local_step(
        x[0], h, loss_target[0], wt, conv_w_full, small, small_matrices, ffn_weight,
        send_ffn_grads, send_small_grads, send_w_in_grads, forward_w_in_grads,
    )
    loss = lax.psum(loss_part[0, 0], AXES)
    sflat = jnp.concatenate([small_grads[n].reshape(-1) for n in SMALL_ORDER])
    sflat = jnp.pad(sflat, (0, SMALL_ROWS * SMALL_COLS - sflat.shape[0])).reshape(SMALL_ROWS, SMALL_COLS)
    vec_gather, _ = _exchange_start([sflat], False, None, "rs_vec_start")

    def finish(key, by_chip, name, keeps):
        sent, lands = _exchange_wait(reduces[key], True, grad_x, name + "_wait", chips=by_chip)
        slot = chip if by_chip else dev
        own = [lax.dynamic_index_in_dim(s, slot, 0, keepdims=False) for s in sent]
        return [_slot_sum(p, f"{name}_sum_{i}", keeps[i]) for i, p in enumerate(own_slot_in(lands, own, slot))]

    group = (0, FF_GROUP)
    (s_in,) = finish("w_in", True, "rs_w_in", [((0, IN_ROWS),)])
    s_out, s_misc = finish("small", False, "rs_small", [None, None])
    s_up, s_down = finish("ffn", False, "rs_ffn", [(group, (FF_GROUP_PAD, FF_GROUP)), (group,)])
    gshard = _grad_shards([s_in, s_out, s_misc, s_up, s_down])

    updates = {n: _adamw(w[n], gshard[n], m[n], v[n], "adamw_" + n) for n in BIG_WEIGHTS}

    big_done = sum(updates[n][0][:1, :1] for n in BIG_WEIGHTS)
    own, lands = _exchange_wait(vec_gather, False, big_done, "rs_vec_wait")
    ssum = _slot_sum(own_slot_in(lands, own)[0], "small_sum").reshape(-1)
    gsmall, off = {}, 0
    for n in SMALL_ORDER:
        shape = (3, 2 * D_FF) if n == "conv_w" else w[n].shape
        size = shape[0] * shape[1]
        gsmall[n] = ssum[off : off + size].reshape(shape)
        off += size
    gsmall["conv_w"] = lax.dynamic_slice_in_dim(gsmall["conv_w"], dev * CONV_SHARD[1], CONV_SHARD[1], axis=1)
    updates.update({n: _adamw(w[n], gsmall[n], m[n], v[n], "adamw_" + n) for n in SMALL_ORDER})

    g_all = {**gshard, **gsmall}
    out_g, out_d, out_m, out_v = [], [], [], []
    for n in WEIGHT_ORDER:
        d, nm, nv = updates[n]
        shape = env[n].shape
        out_g.append(g_all[n].reshape(shape))
        out_d.append(d.reshape(shape))
        out_m.append(nm.reshape(shape))
        out_v.append(nv.reshape(shape))
    return (loss, grad_x[None], *out_g, *out_d, *out_m, *out_v)
```

```python
import functools

import jax
import jax.numpy as jnp
import numpy as np
from jax import lax
from jax.experimental import pallas as pl
from jax.experimental.pallas import tpu as pltpu

F32 = jnp.float32
BF16 = jnp.bfloat16

N_DEV = 8
N_CHIP = 4
AXES = ("x", "y", "c")
MESH = pl.DeviceIdType.MESH

D_MODEL = 2048
MLA_HEADS = 8
QK_NOPE = 128
QK_ROPE = 64
V_HEAD = 128
Q_LORA = 512
KV_LORA = 256
ROPE_THETA = 10000.0
HEAD_PAD = 256
DIL_PATTERNS = ((128, 1), (512, 4), (2048, 16))
DIL_GROUPS = 3
DIL_HG = 4
DIL_HEADS = 12
DIL_HD = 128
DIL_BLK = 128
DIL_QKV = DIL_HEADS * DIL_HD
DIL_OUT = DIL_HG * DIL_HD
ALIBI_MAX_BIAS = 8.0
D_FF = 5504
D_FF_PAD = 5632
NORM_EPS = 1e-6
LAT = Q_LORA + KV_LORA + QK_ROPE
LAT_PAD = 896
D_IN = LAT + 3 * DIL_QKV + 2 * D_MODEL
NEG = -1e30

ADAM_LR = 0.001
ADAM_B1 = 0.9
ADAM_B2 = 0.999
ADAM_EPS = 1e-08
ADAM_WD = 0.01
ADAM_STEP = 10

SMALL_ROWS = 56
SMALL_COLS = 1024

IN_ROWS = 1192
IN_ROWS_PAD = 1200
FF_GROUP = D_FF // N_DEV
FF_GROUP_PAD = D_FF_PAD // N_DEV
CONV_SHARD = (3, 1376)

NT = (((1,), (1,)), ((), ()))
TN = (((0,), (0,)), ((), ()))


def _dot(a, b, dims=(((1,), (0,)), ((), ()))):
    return lax.dot_general(a, b, dims, preferred_element_type=F32)


def _mm(a, b, mode, out_dtype, tm, tn, tk, name, bias=None, act=None, res=None, b_koff=0, a_halves=1):
    H = a_halves
    if mode == "nn":
        (M, K), (K2, N) = (a.shape[0] // H, a.shape[1] * H), b.shape
        assert (b_koff + 1) * K <= K2, (name, a.shape, b.shape)
        koff, K2 = b_koff * (K // tk), K
        kper, mrows = a.shape[1] // tk, M // tm
        a_spec = pl.BlockSpec((tm, tk), lambda i, j, k: (i + (k // kper) * mrows, k % kper))
        b_spec = pl.BlockSpec((tk, tn), lambda i, j, k: (k + koff, j))
        dims = (((1,), (0,)), ((), ()))
    elif mode == "nt":
        (M, K), (N, K2) = a.shape, b.shape
        a_spec = pl.BlockSpec((tm, tk), lambda i, j, k: (i, k))
        b_spec = pl.BlockSpec((tn, tk), lambda i, j, k: (j, k))
        dims = NT
    else:
        (K, M), (K2, N) = (a.shape[0] // H, a.shape[1] * H), b.shape
        mper, krows = a.shape[1] // tm, K // tk
        a_spec = pl.BlockSpec((tk, tm), lambda i, j, k: (k + (i // mper) * krows, i % mper))
        b_spec = pl.BlockSpec((tk, tn), lambda i, j, k: (k, j))
        dims = TN
    assert K == K2 and M % tm == 0 and N % tn == 0 and K % tk == 0, (name, a.shape, b.shape)
    nk = K // tk
    has_bias, has_res = bias is not None, res is not None

    def body(*refs):
        refs = list(refs)
        a_ref, b_ref = refs[0], refs[1]
        pos = 2
        bias_ref = res_ref = None
        if has_bias:
            bias_ref = refs[pos]
            pos += 1
        if has_res:
            res_ref = refs[pos]
            pos += 1
        o_ref = refs[pos]
        p = _dot(a_ref[...].astype(BF16), b_ref[...].astype(BF16), dims)

        def finish(acc):
            if has_bias:
                acc = acc + bias_ref[...]
            if act == "sigmoid":
                acc = jax.nn.sigmoid(acc)
            if has_res:
                acc = res_ref[...] + acc
            o_ref[...] = acc.astype(o_ref.dtype)

        if nk == 1:
            finish(p)
        else:
            acc_ref = refs[pos + 1]
            k = pl.program_id(2)

            @pl.when(k == 0)
            def _():
                acc_ref[...] = p

            @pl.when(k != 0)
            def _():
                acc_ref[...] += p

            @pl.when(k == nk - 1)
            def _():
                finish(acc_ref[...])

    in_specs = [a_spec, b_spec]
    args = [a, b]
    if has_bias:
        in_specs.append(pl.BlockSpec((1, tn), lambda i, j, k: (0, j)))
        args.append(bias)
    if has_res:
        in_specs.append(pl.BlockSpec((tm, tn), lambda i, j, k: (i, j)))
        args.append(res)
    return pl.pallas_call(
        body,
        name=name,
        grid=(M // tm, N // tn, nk),
        in_specs=in_specs,
        out_specs=pl.BlockSpec((tm, tn), lambda i, j, k: (i, j)),
        out_shape=jax.ShapeDtypeStruct((M, N), out_dtype),
        scratch_shapes=[pltpu.VMEM((tm, tn), F32)] if nk > 1 else [],
        compiler_params=pltpu.CompilerParams(dimension_semantics=("parallel", "parallel", "arbitrary")),
    )(*args)


def _stacked_mm(pieces, w_t, res, after, name, tm=512, tn=1024):
    P, M, W = pieces.shape
    N = w_t.shape[1]

    def body(a_ref, b_ref, r_ref, _, o_ref):
        acc = r_ref[...]
        for p in range(P):
            acc = acc + _dot(a_ref[p].astype(BF16), b_ref[p * W : (p + 1) * W, :])
        o_ref[...] = acc

    tile = pl.BlockSpec((tm, tn), lambda i, j: (i, j))
    return pl.pallas_call(
        body,
        name=name,
        grid=(M // tm, N // tn),
        in_specs=[
            pl.BlockSpec((P, tm, W), lambda i, j: (0, i, 0)),
            pl.BlockSpec((P * W, tn), lambda i, j: (0, j)),
            tile,
            pl.BlockSpec((8, 128), lambda i, j: (0, 0)),
        ],
        out_specs=tile,
        out_shape=jax.ShapeDtypeStruct((M, N), F32),
        compiler_params=pltpu.CompilerParams(dimension_semantics=("parallel", "parallel")),
    )(pieces, w_t, res, after)


def _rstd(x):
    return lax.rsqrt(jnp.mean(x * x, axis=-1, keepdims=True) + NORM_EPS)


def _rms_bwd_math(dy, x, g):
    r = _rstd(x)
    xh = x * r
    dg = jnp.sum(dy * xh, axis=0, keepdims=True)
    dxh = dy * g
    dx = r * (dxh - xh * jnp.mean(dxh * xh, axis=-1, keepdims=True))
    return dx, dg


def _rms_fwd(x, g, name, tr=256):
    S, D = x.shape

    def body(x_ref, g_ref, o_ref):
        xv = x_ref[...]
        o_ref[...] = ((xv * _rstd(xv)) * g_ref[...]).astype(o_ref.dtype)

    return pl.pallas_call(
        body,
        name=name,
        grid=(S // tr,),
        in_specs=[pl.BlockSpec((tr, D), lambda i: (i, 0)), pl.BlockSpec((1, D), lambda i: (0, 0))],
        out_specs=pl.BlockSpec((tr, D), lambda i: (i, 0)),
        out_shape=jax.ShapeDtypeStruct((S, D), BF16),
        compiler_params=pltpu.CompilerParams(dimension_semantics=("parallel",)),
    )(x, g)


def _rms_bwd(dy, x, g, res, name, tr=256):
    S, D = x.shape

    def body(dy_ref, x_ref, g_ref, res_ref, dx_ref, dxb_ref, dg_ref):
        dx, dg = _rms_bwd_math(dy_ref[...], x_ref[...], g_ref[...])
        dx = dx + res_ref[...]
        dx_ref[...] = dx
        dxb_ref[...] = dx.astype(BF16)

        @pl.when(pl.program_id(0) == 0)
        def _():
            dg_ref[...] = dg

        @pl.when(pl.program_id(0) != 0)
        def _():
            dg_ref[...] += dg

    row = pl.BlockSpec((tr, D), lambda i: (i, 0))
    vec = pl.BlockSpec((1, D), lambda i: (0, 0))
    return pl.pallas_call(
        body,
        name=name,
        grid=(S // tr,),
        in_specs=[row, row, vec, row],
        out_specs=[row, row, vec],
        out_shape=[jax.ShapeDtypeStruct((S, D), F32), jax.ShapeDtypeStruct((S, D), BF16), jax.ShapeDtypeStruct((1, D), F32)],
        compiler_params=pltpu.CompilerParams(dimension_semantics=("arbitrary",)),
    )(dy, x, g, res)


def _mm_res_rms(a, b, res, g, name, tm=256):
    M, K = a.shape
    D = b.shape[1]

    def body(a_ref, b_ref, res_ref, g_ref, y_ref, h_ref):
        y = res_ref[...] + _dot(a_ref[...], b_ref[...])
        y_ref[...] = y
        h_ref[...] = ((y * _rstd(y)) * g_ref[...]).astype(BF16)

    row = pl.BlockSpec((tm, D), lambda i: (i, 0))
    return pl.pallas_call(
        body,
        name=name,
        grid=(M // tm,),
        in_specs=[pl.BlockSpec((tm, K), lambda i: (i, 0)), pl.BlockSpec((K, D), lambda i: (0, 0)), row, pl.BlockSpec((1, D), lambda i: (0, 0))],
        out_specs=[row, row],
        out_shape=[jax.ShapeDtypeStruct((M, D), F32), jax.ShapeDtypeStruct((M, D), BF16)],
        compiler_params=pltpu.CompilerParams(dimension_semantics=("parallel",)),
    )(a, b, res, g)


def _mm_rms_bwd(a, b, b_koff, dy_part, x, g, res, name, tm=256):
    M, K = a.shape
    D = x.shape[1]

    def body(a_ref, b_ref, dyp_ref, x_ref, g_ref, res_ref, dx_ref, dg_ref):
        dy = dyp_ref[...] + _dot(a_ref[...], b_ref[...])
        dx, dg = _rms_bwd_math(dy, x_ref[...], g_ref[...])
        dx_ref[...] = dx + res_ref[...]

        @pl.when(pl.program_id(0) == 0)
        def _():
            dg_ref[...] = dg

        @pl.when(pl.program_id(0) != 0)
        def _():
            dg_ref[...] += dg

    row = pl.BlockSpec((tm, D), lambda i: (i, 0))
    vec = pl.BlockSpec((1, D), lambda i: (0, 0))
    return pl.pallas_call(
        body,
        name=name,
        grid=(M // tm,),
        in_specs=[pl.BlockSpec((tm, K), lambda i: (i, 0)), pl.BlockSpec((K, D), lambda i: (b_koff, 0)), row, row, vec, row],
        out_specs=[row, vec],
        out_shape=[jax.ShapeDtypeStruct((M, D), F32), jax.ShapeDtypeStruct((1, D), F32)],
        compiler_params=pltpu.CompilerParams(dimension_semantics=("arbitrary",)),
    )(a, b, dy_part, x, g, res)


def _final_loss(x2, g, tgt, name, tr=256):
    S, D = x2.shape

    def body(x_ref, g_ref, t_ref, loss_ref, dx_ref, dxb_ref, dg_ref):
        xv, gv = x_ref[...], g_ref[...]
        y = (xv * _rstd(xv)) * gv
        e = y - t_ref[...]
        part = 0.5 * jnp.sum(jnp.mean(e * e, axis=-1, keepdims=True), axis=0, keepdims=True)
        dx, dg = _rms_bwd_math(e * (1.0 / D), xv, gv)
        dx_ref[...] = dx
        dxb_ref[...] = dx.astype(BF16)
        part = jnp.broadcast_to(part, (1, 128))

        @pl.when(pl.program_id(0) == 0)
        def _():
            dg_ref[...] = dg
            loss_ref[...] = part

        @pl.when(pl.program_id(0) != 0)
        def _():
            dg_ref[...] += dg
            loss_ref[...] += part

    row = pl.BlockSpec((tr, D), lambda i: (i, 0))
    vec = pl.BlockSpec((1, D), lambda i: (0, 0))
    return pl.pallas_call(
        body,
        name=name,
        grid=(S // tr,),
        in_specs=[row, vec, row],
        out_specs=[pl.BlockSpec((1, 128), lambda i: (0, 0)), row, row, vec],
        out_shape=[
            jax.ShapeDtypeStruct((1, 128), F32),
            jax.ShapeDtypeStruct((S, D), F32),
            jax.ShapeDtypeStruct((S, D), BF16),
            jax.ShapeDtypeStruct((1, D), F32),
        ],
        compiler_params=pltpu.CompilerParams(dimension_semantics=("arbitrary",)),
    )(x2, g, tgt)


def _rope_tables(S):
    pos = jnp.arange(S, dtype=F32)
    inv_freq = ROPE_THETA ** (-jnp.arange(0, QK_ROPE, 2, dtype=F32) / QK_ROPE)
    ang = pos[:, None] * inv_freq[None, :]
    cos, sin = jnp.cos(ang), jnp.sin(ang)
    zero = jnp.zeros((S, 128 - QK_ROPE), F32)
    return jnp.concatenate([cos, cos, zero], axis=1), jnp.concatenate([-sin, sin, zero], axis=1)


def _rope_tile(x, cos_t, sin_t):
    lane = lax.broadcasted_iota(jnp.int32, x.shape, 1)
    partner = jnp.where(lane < QK_ROPE // 2, pltpu.roll(x, 128 - QK_ROPE // 2, 1), pltpu.roll(x, QK_ROPE // 2, 1))
    return x * cos_t + partner * sin_t


def _mla_prep1(lat, gq, gkv, cos_t, sin_t, name, tr=256):
    S = lat.shape[0]

    def body(lat_ref, gq_ref, gkv_ref, cos_ref, sin_ref, cq_ref, ckv_ref, kpe_ref):
        cq = lat_ref[:, :Q_LORA]
        ckv = lat_ref[:, Q_LORA : Q_LORA + KV_LORA]
        cq_ref[...] = ((cq * _rstd(cq)) * gq_ref[...]).astype(BF16)
        ckv_ref[...] = ((ckv * _rstd(ckv)) * gkv_ref[...]).astype(BF16)
        kpe_ref[...] = _rope_tile(lat_ref[:, Q_LORA + KV_LORA :], cos_ref[...], sin_ref[...]).astype(BF16)

    def row(n):
        return pl.BlockSpec((tr, n), lambda i: (i, 0))

    def vec(n):
        return pl.BlockSpec((1, n), lambda i: (0, 0))

    return pl.pallas_call(
        body,
        name=name,
        grid=(S // tr,),
        in_specs=[row(LAT_PAD), vec(Q_LORA), vec(KV_LORA), row(128), row(128)],
        out_specs=[row(Q_LORA), row(KV_LORA), row(128)],
        out_shape=[
            jax.ShapeDtypeStruct((S, Q_LORA), BF16),
            jax.ShapeDtypeStruct((S, KV_LORA), BF16),
            jax.ShapeDtypeStruct((S, 128), BF16),
        ],
        compiler_params=pltpu.CompilerParams(dimension_semantics=("parallel",)),
    )(lat, gq, gkv, cos_t, sin_t)


def _mla_prep2(q_raw, kv, kpe, cos_t, sin_t, name, tr=256):
    S = q_raw.shape[0]
    W = MLA_HEADS * HEAD_PAD

    def body(q_ref, kv_ref, kpe_ref, cos_ref, sin_ref, qa_ref, ka_ref):
        cos_v, sin_v, kpe_v = cos_ref[...], sin_ref[...], kpe_ref[...]
        for h in range(MLA_HEADS):
            lo = h * HEAD_PAD
            qa_ref[:, lo : lo + 128] = q_ref[:, lo : lo + 128].astype(BF16)
            qa_ref[:, lo + 128 : lo + 256] = _rope_tile(q_ref[:, lo + 128 : lo + 256], cos_v, sin_v).astype(BF16)
            ka_ref[:, lo : lo + 128] = kv_ref[:, lo : lo + 128]
            ka_ref[:, lo + 128 : lo + 256] = kpe_v

    def row(n):
        return pl.BlockSpec((tr, n), lambda i: (i, 0))

    return pl.pallas_call(
        body,
        name=name,
        grid=(S // tr,),
        in_specs=[row(W), row(W), row(128), row(128), row(128)],
        out_specs=[row(W), row(W)],
        out_shape=[jax.ShapeDtypeStruct((S, W), BF16), jax.ShapeDtypeStruct((S, W), BF16)],
        compiler_params=pltpu.CompilerParams(dimension_semantics=("parallel",)),
    )(q_raw, kv, kpe, cos_t, sin_t)


def _mla_post(dq_att, dk_att, dv, cos_t, sin_t, name, tr=256):
    S = dq_att.shape[0]
    W = MLA_HEADS * HEAD_PAD

    def body(dq_ref, dk_ref, dv_ref, cos_ref, sin_ref, dqr_ref, dkv_ref, dkpe_ref):
        cos_v, nsin_v = cos_ref[...], -sin_ref[...]
        kpe = jnp.zeros((tr, 128), F32)
        for h in range(MLA_HEADS):
            lo = h * HEAD_PAD
            dqr_ref[:, lo : lo + 128] = dq_ref[:, lo : lo + 128].astype(BF16)
            dqr_ref[:, lo + 128 : lo + 256] = _rope_tile(dq_ref[:, lo + 128 : lo + 256], cos_v, nsin_v).astype(BF16)
            dkv_ref[:, lo : lo + 128] = dk_ref[:, lo : lo + 128].astype(BF16)
            dkv_ref[:, lo + 128 : lo + 256] = dv_ref[:, h * 128 : (h + 1) * 128].astype(BF16)
            kpe = kpe + dk_ref[:, lo + 128 : lo + 256]
        dkpe_ref[...] = _rope_tile(kpe, cos_v, nsin_v)

    def row(n):
        return pl.BlockSpec((tr, n), lambda i: (i, 0))

    return pl.pallas_call(
        body,
        name=name,
        grid=(S // tr,),
        in_specs=[row(W), row(W), row(MLA_HEADS * V_HEAD), row(128), row(128)],
        out_specs=[row(W), row(W), row(128)],
        out_shape=[jax.ShapeDtypeStruct((S, W), BF16), jax.ShapeDtypeStruct((S, W), BF16), jax.ShapeDtypeStruct((S, 128), F32)],
        compiler_params=pltpu.CompilerParams(dimension_semantics=("parallel",)),
    )(dq_att, dk_att, dv, cos_t, sin_t)


def _lat_bwd(dcqn, dckvn, dkpe, lat, gq, gkv, name, tr=256):
    S = lat.shape[0]

    def body(dcq_ref, dckv_ref, dkpe_ref, lat_ref, gq_ref, gkv_ref, dlat_ref, dgq_ref, dgkv_ref):
        dq, dgq = _rms_bwd_math(dcq_ref[...], lat_ref[:, :Q_LORA], gq_ref[...])
        dkv, dgkv = _rms_bwd_math(dckv_ref[...], lat_ref[:, Q_LORA : Q_LORA + KV_LORA], gkv_ref[...])
        dlat_ref[:, :Q_LORA] = dq.astype(BF16)
        dlat_ref[:, Q_LORA : Q_LORA + KV_LORA] = dkv.astype(BF16)
        dlat_ref[:, Q_LORA + KV_LORA :] = dkpe_ref[...].astype(BF16)

        @pl.when(pl.program_id(0) == 0)
        def _():
            dgq_ref[...] = dgq
            dgkv_ref[...] = dgkv

        @pl.when(pl.program_id(0) != 0)
        def _():
            dgq_ref[...] += dgq
            dgkv_ref[...] += dgkv

    def row(n):
        return pl.BlockSpec((tr, n), lambda i: (i, 0))

    def vec(n):
        return pl.BlockSpec((1, n), lambda i: (0, 0))

    return pl.pallas_call(
        body,
        name=name,
        grid=(S // tr,),
        in_specs=[row(Q_LORA), row(KV_LORA), row(128), row(LAT_PAD), vec(Q_LORA), vec(KV_LORA)],
        out_specs=[row(LAT_PAD), vec(Q_LORA), vec(KV_LORA)],
        out_shape=[
            jax.ShapeDtypeStruct((S, LAT_PAD), BF16),
            jax.ShapeDtypeStruct((1, Q_LORA), F32),
            jax.ShapeDtypeStruct((1, KV_LORA), F32),
        ],
        compiler_params=pltpu.CompilerParams(dimension_semantics=("arbitrary",)),
    )(dcqn, dckvn, dkpe, lat, gq, gkv)


MLA_SCALE = (QK_NOPE + QK_ROPE) ** -0.5
LOG2E = 1.4426950408889634
MLA_C2 = MLA_SCALE * LOG2E
FLASH_T = 1024


def _causal_pairs(n, by_key):
    pairs = [(i, j) for j in range(n) for i in range(j, n)] if by_key else [(i, j) for i in range(n) for j in range(i + 1)]
    return jnp.asarray([p[0] for p in pairs], jnp.int32), jnp.asarray([p[1] for p in pairs], jnp.int32)


def _causal_mask(shape, shift, keys_first=False):
    q_axis, k_axis = (1, 0) if keys_first else (0, 1)
    return lax.broadcasted_iota(jnp.int32, shape, k_axis) <= lax.broadcasted_iota(jnp.int32, shape, q_axis) + shift


def _lanes(x, n):
    return jnp.tile(x, (1, n // 128))


def _flash_grid(npairs, in_specs, out_specs, scratch):
    return pltpu.PrefetchScalarGridSpec(
        num_scalar_prefetch=2, grid=(MLA_HEADS, npairs), in_specs=in_specs, out_specs=out_specs, scratch_shapes=scratch
    )


def _flash2_fwd(q_att, k_att, kv, name, t=FLASH_T):
    S = q_att.shape[0]
    half = t // 2
    qi_tab, kj_tab = _causal_pairs(S // t, by_key=False)

    def body(qi_ref, kj_ref, q_ref, k_ref, v_ref, o_ref, lse_ref, m_sc, l_sc, acc_sc):
        step = pl.program_id(1)
        qi, kj = qi_ref[step], kj_ref[step]

        @pl.when(kj == 0)
        def _():
            m_sc[...] = jnp.full((t, 128), NEG, F32)
            l_sc[...] = jnp.zeros((t, 128), F32)
            acc_sc[...] = jnp.zeros((t, V_HEAD), F32)

        def update(rows, s, v):
            m_prev = m_sc[rows, :]
            m_new = jnp.maximum(m_prev, jnp.max(s, axis=1, keepdims=True))
            p = jnp.exp2((s - _lanes(m_new, s.shape[1])) * MLA_C2)
            alpha = jnp.exp2((m_prev - m_new) * MLA_C2)
            l_sc[rows, :] = alpha * l_sc[rows, :] + jnp.sum(p, axis=1, keepdims=True)
            acc_sc[rows, :] = alpha * acc_sc[rows, :] + _dot(p.astype(BF16), v)
            m_sc[rows, :] = m_new

        @pl.when(kj < qi)
        def _():
            update(slice(0, t), _dot(q_ref[...], k_ref[...], NT), v_ref[...])

        @pl.when(kj == qi)
        def _():
            top = _dot(q_ref[:half, :], k_ref[:half, :], NT)
            update(slice(0, half), jnp.where(_causal_mask(top.shape, 0), top, NEG), v_ref[:half, :])
            bot = _dot(q_ref[half:, :], k_ref[...], NT)
            update(slice(half, t), jnp.where(_causal_mask(bot.shape, half), bot, NEG), v_ref[...])
            l = l_sc[...]
            o_ref[...] = acc_sc[...] / l
            lse_ref[0] = m_sc[...] * MLA_SCALE + jnp.log(l)

    return pl.pallas_call(
        body,
        name=name,
        grid_spec=_flash_grid(
            qi_tab.shape[0],
            [
                pl.BlockSpec((t, HEAD_PAD), lambda h, p, qi, kj: (qi[p], h)),
                pl.BlockSpec((t, HEAD_PAD), lambda h, p, qi, kj: (kj[p], h)),
                pl.BlockSpec((t, V_HEAD), lambda h, p, qi, kj: (kj[p], 2 * h + 1)),
            ],
            [
                pl.BlockSpec((t, V_HEAD), lambda h, p, qi, kj: (qi[p], h)),
                pl.BlockSpec((1, t, 128), lambda h, p, qi, kj: (h, qi[p], 0)),
            ],
            [pltpu.VMEM((t, 128), F32), pltpu.VMEM((t, 128), F32), pltpu.VMEM((t, V_HEAD), F32)],
        ),
        out_shape=[jax.ShapeDtypeStruct((S, MLA_HEADS * V_HEAD), F32), jax.ShapeDtypeStruct((MLA_HEADS, S, 128), F32)],
        compiler_params=pltpu.CompilerParams(dimension_semantics=("parallel", "arbitrary")),
    )(qi_tab, kj_tab, q_att, k_att, kv)


def _flash_delta(do, o, name, tr=512):
    S = o.shape[0]

    def body(do_ref, o_ref, d_ref):
        lane = lax.broadcasted_iota(jnp.int32, (tr, 128), 1)
        acc = jnp.zeros((tr, 128), F32)
        for h in range(MLA_HEADS):
            sl = slice(h * V_HEAD, (h + 1) * V_HEAD)
            acc = jnp.where(lane == h, jnp.sum(do_ref[:, sl].astype(F32) * o_ref[:, sl], axis=1, keepdims=True), acc)
        d_ref[...] = acc

    row = pl.BlockSpec((tr, MLA_HEADS * V_HEAD), lambda i: (i, 0))
    return pl.pallas_call(
        body,
        name=name,
        grid=(S // tr,),
        in_specs=[row, row],
        out_specs=pl.BlockSpec((tr, 128), lambda i: (i, 0)),
        out_shape=jax.ShapeDtypeStruct((S, 128), F32),
        compiler_params=pltpu.CompilerParams(dimension_semantics=("parallel",)),
    )(do, o)


def _flash2_bwd(q_att, k_att, kv, do, lse_row, delta_row, name, t=FLASH_T):
    S = q_att.shape[0]
    n = S // t
    qi_tab, kj_tab = _causal_pairs(n, by_key=True)
    last = qi_tab.shape[0] - 1
    half = t // 2

    def body(qi_ref, kj_ref, q_ref, k_ref, v_ref, do_ref, lse_ref, dl_ref, dq_ref, dk_ref, dv_ref, dk_sc, dv_sc):
        step = pl.program_id(1)
        qi, kj = qi_ref[step], kj_ref[step]

        @pl.when(step == 0)
        def _():
            dq_ref[...] = jnp.zeros((S, HEAD_PAD), F32)

        def update(k0, q0, st):
            nk, nq = st.shape
            kr, qr = slice(k0, k0 + nk), slice(q0, q0 + nq)
            q, do_v = q_ref[qr, :], do_ref[qr, :]
            pt = jnp.exp2(st * MLA_C2 - lse_ref[0][:, qr] * LOG2E)
            dv_sc[kr, :] += _dot(pt.astype(BF16), do_v)
            dpt = _dot(v_ref[kr, :], do_v, NT)
            dst = (pt * (dpt - dl_ref[0][:, qr])).astype(BF16)
            dk_sc[kr, :] += _dot(dst, q)
            rows = pl.ds(pl.multiple_of(qi * t + q0, half), nq)
            dq_ref[rows, :] += _dot(dst, k_ref[kr, :], TN)

        @pl.when(qi == kj)
        def _():
            dk_sc[...] = jnp.zeros((t, HEAD_PAD), F32)
            dv_sc[...] = jnp.zeros((t, V_HEAD), F32)
            top = _dot(k_ref[:half, :], q_ref[...], NT)
            update(0, 0, jnp.where(_causal_mask(top.shape, 0, keys_first=True), top, NEG))
            bot = _dot(k_ref[half:, :], q_ref[half:, :], NT)
            update(half, half, jnp.where(_causal_mask(bot.shape, 0, keys_first=True), bot, NEG))

        @pl.when(qi > kj)
        def _():
            update(0, 0, _dot(k_ref[...], q_ref[...], NT))

        @pl.when(qi == n - 1)
        def _():
            dk_ref[...] = dk_sc[...] * MLA_SCALE
            dv_ref[...] = dv_sc[...]

        @pl.when(step == last)
        def _():
            dq_ref[...] = dq_ref[...] * MLA_SCALE

    qrow = lambda h, p, qi, kj: (qi[p], h)
    krow = lambda h, p, qi, kj: (kj[p], h)
    stat = pl.BlockSpec((1, 1, t), lambda h, p, qi, kj: (h, 0, qi[p]))
    return pl.pallas_call(
        body,
        name=name,
        grid_spec=_flash_grid(
            qi_tab.shape[0],
            [
                pl.BlockSpec((t, HEAD_PAD), qrow),
                pl.BlockSpec((t, HEAD_PAD), krow),
                pl.BlockSpec((t, V_HEAD), lambda h, p, qi, kj: (kj[p], 2 * h + 1)),
                pl.BlockSpec((t, V_HEAD), qrow),
                stat,
                stat,
            ],
            [
                pl.BlockSpec((S, HEAD_PAD), lambda h, p, qi, kj: (0, h)),
                pl.BlockSpec((t, HEAD_PAD), krow),
                pl.BlockSpec((t, V_HEAD), krow),
            ],
            [pltpu.VMEM((t, HEAD_PAD), F32), pltpu.VMEM((t, V_HEAD), F32)],
        ),
        out_shape=[
            jax.ShapeDtypeStruct((S, MLA_HEADS * HEAD_PAD), F32),
            jax.ShapeDtypeStruct((S, MLA_HEADS * HEAD_PAD), F32),
            jax.ShapeDtypeStruct((S, MLA_HEADS * V_HEAD), F32),
        ],
        compiler_params=pltpu.CompilerParams(dimension_semantics=("parallel", "arbitrary")),
    )(qi_tab, kj_tab, q_att, k_att, kv, do, lse_row, delta_row)


DIL_SCALE = DIL_HD**-0.5


def _dil_bias():
    slopes = 2.0 ** (-ALIBI_MAX_BIAS * np.arange(1, DIL_HEADS + 1, dtype=np.float64) / DIL_HEADS)
    slopes = slopes.astype(np.float32).reshape(DIL_GROUPS, DIL_HG)
    p = np.arange(DIL_BLK)[:, None]
    kidx = np.arange(2 * DIL_BLK)[None, :]
    j = p + DIL_BLK - kidx
    out = np.zeros((DIL_GROUPS, DIL_HG, DIL_BLK, 2 * DIL_BLK), np.float32)
    for g, (window, dil) in enumerate(DIL_PATTERNS):
        valid = (j >= 0) & (j <= window // dil)
        for h in range(DIL_HG):
            alibi = -slopes[g, h] * (dil * j).astype(np.float32)
            out[g, h] = np.where(valid, alibi, np.float32(NEG))
    return jnp.asarray(out)


DIL_UNROLL = 4


def _unrolled_loop(lo, hi, fn, unroll=DIL_UNROLL):
    groups = (hi - lo) // unroll
    done = lo
    if groups > 1:

        def step(i, carry):
            for u in range(unroll):
                fn(lo + i * unroll + u)
            return carry

        lax.fori_loop(0, groups, step, 0)
        done = lo + groups * unroll
    for n in range(done, hi):
        fn(n)


def _dil_rows(r, n, count, dil):
    if dil == 1:
        if isinstance(n, int):
            return slice(n * DIL_BLK, (n + count) * DIL_BLK)
        return pl.ds(pl.multiple_of(n * DIL_BLK, DIL_BLK), count * DIL_BLK)
    return pl.ds(n * DIL_BLK * dil + r, count * DIL_BLK, stride=dil)


def _dil_each_block(S, dil, block):
    nb = S // dil // DIL_BLK
    if dil == 1:
        block(0, 0, True)
        _unrolled_loop(1, nb, lambda n: block(0, n, False))
    else:
        for r in range(dil):
            for n in range(nb):
                block(r, n, n == 0)


def _dil_col(g, part, h):
    return (g * 3 + part) * DIL_HG + h


def _dil_fwd_group(dqkv, bias_g, g, dil, name):
    S = dqkv.shape[0]

    def body(bias_ref, q_ref, k_ref, v_ref, o_ref, lse_ref):
        def block(r, n, first):
            cur = _dil_rows(r, n, 1, dil)
            both = cur if first else _dil_rows(r, n - 1, 2, dil)
            b = bias_ref[0][:, DIL_BLK:] if first else bias_ref[0]
            q, kk, vv = q_ref[cur, :].astype(BF16), k_ref[both, :].astype(BF16), v_ref[both, :].astype(BF16)
            s = _dot(q, kk, NT) * DIL_SCALE + b
            m = jnp.max(s, axis=1, keepdims=True)
            e = jnp.exp(s - m)
            l = jnp.sum(e, axis=1, keepdims=True)
            p = e * (1.0 / l)
            o_ref[cur, :] = _dot(p.astype(BF16), vv)
            lse_ref[cur, :] = jnp.broadcast_to(m + jnp.log(l), (DIL_BLK, 128))

        _dil_each_block(S, dil, block)

    def col(part):
        return pl.BlockSpec((S, DIL_HD), lambda h: (0, _dil_col(g, part, h)))

    out = pl.BlockSpec((S, DIL_HD), lambda h: (0, h))
    return pl.pallas_call(
        body,
        name=name,
        grid=(DIL_HG,),
        in_specs=[pl.BlockSpec((1, DIL_BLK, 2 * DIL_BLK), lambda h: (h, 0, 0)), col(0), col(1), col(2)],
        out_specs=[out, out],
        out_shape=[jax.ShapeDtypeStruct((S, DIL_OUT), F32), jax.ShapeDtypeStruct((S, DIL_OUT), F32)],
        compiler_params=pltpu.CompilerParams(dimension_semantics=("parallel",)),
    )(bias_g, dqkv, dqkv, dqkv)


def _dil_combine(os_, ls_, name, tr=512):
    S = os_[0].shape[0]

    def body(o0, o1, o2, l0, l1, l2, out_ref, lse_ref):
        a, b, c = l0[...], l1[...], l2[...]
        m = jnp.maximum(jnp.maximum(a, b), c)
        ea, eb, ec = jnp.exp(a - m), jnp.exp(b - m), jnp.exp(c - m)
        den = ea + eb + ec
        inv = 1.0 / den
        out_ref[...] = (ea * inv) * o0[...] + (eb * inv) * o1[...] + (ec * inv) * o2[...]
        lse_ref[...] = m + jnp.log(den)

    row = pl.BlockSpec((tr, DIL_OUT), lambda i: (i, 0))
    return pl.pallas_call(
        body,
        name=name,
        grid=(S // tr,),
        in_specs=[row] * 6,
        out_specs=[row, row],
        out_shape=[jax.ShapeDtypeStruct((S, DIL_OUT), F32)] * 2,
        compiler_params=pltpu.CompilerParams(dimension_semantics=("parallel",)),
    )(*os_, *ls_)


def _dil_rowdot(dod, od, name, tr=512):
    S = dod.shape[0]

    def body(d_ref, o_ref, dd_ref):
        for h in range(DIL_HG):
            sl = slice(h * 128, (h + 1) * 128)
            sm = jnp.sum(d_ref[:, sl] * o_ref[:, sl], axis=1, keepdims=True)
            dd_ref[:, sl] = jnp.broadcast_to(sm, (tr, 128))

    row = pl.BlockSpec((tr, DIL_OUT), lambda i: (i, 0))
    return pl.pallas_call(
        body,
        name=name,
        grid=(S // tr,),
        in_specs=[row, row],
        out_specs=row,
        out_shape=jax.ShapeDtypeStruct((S, DIL_OUT), F32),
        compiler_params=pltpu.CompilerParams(dimension_semantics=("parallel",)),
    )(dod, od)


def _dil_bwd_group(dqkv, bias_g, dod, dd, lse, grads, g, dil, name):
    S = dqkv.shape[0]

    def body(bias_ref, q_ref, k_ref, v_ref, do_ref, dd_ref, lse_ref, _, out_ref):
        out_ref[1] = jnp.zeros((S, DIL_HD), F32)
        out_ref[2] = jnp.zeros((S, DIL_HD), F32)

        def block(r, n, first):
            cur = _dil_rows(r, n, 1, dil)
            both = cur if first else _dil_rows(r, n - 1, 2, dil)
            b = bias_ref[0][:, DIL_BLK:] if first else bias_ref[0]
            q, kk, vv = q_ref[cur, :].astype(BF16), k_ref[both, :].astype(BF16), v_ref[both, :].astype(BF16)
            do = do_ref[cur, :].astype(BF16)
            s = _dot(q, kk, NT) * DIL_SCALE + b
            p = jnp.exp(s - lse_ref[cur, 0:1])
            dp = _dot(do, vv, NT)
            ds = ((p * (dp - dd_ref[cur, 0:1])) * DIL_SCALE).astype(BF16)
            out_ref[0, cur, :] = _dot(ds, kk)
            out_ref[1, both, :] += _dot(ds, q, TN)
            out_ref[2, both, :] += _dot(p.astype(BF16), do, TN)

        _dil_each_block(S, dil, block)

    def col(part):
        return pl.BlockSpec((S, DIL_HD), lambda h: (0, _dil_col(g, part, h)))

    nat = pl.BlockSpec((S, DIL_HD), lambda h: (0, h))
    return pl.pallas_call(
        body,
        name=name,
        grid=(DIL_HG,),
        in_specs=[pl.BlockSpec((1, DIL_BLK, 2 * DIL_BLK), lambda h: (h, 0, 0)), col(0), col(1), col(2), nat, nat, nat, ANY],
        out_specs=pl.BlockSpec((3, S, DIL_HD), lambda h: (g, 0, h)),
        out_shape=jax.ShapeDtypeStruct(grads.shape, F32),
        input_output_aliases={7: 0},
        compiler_params=pltpu.CompilerParams(dimension_semantics=("parallel",)),
    )(bias_g, dqkv, dqkv, dqkv, dod, dd, lse, grads)


def _merge_fwd(gates, o_a, o_b, name, tr=256):
    S = o_a.shape[0]

    def body(ga_ref, gb_ref, oa_ref, ob_ref, m_ref):
        m_ref[...] = (ga_ref[...] * oa_ref[...] + gb_ref[...] * ob_ref[...]).astype(BF16)

    row = pl.BlockSpec((tr, D_MODEL), lambda i: (i, 0))
    return pl.pallas_call(
        body,
        name=name,
        grid=(S // tr,),
        in_specs=[row, pl.BlockSpec((tr, D_MODEL), lambda i: (i, 1)), row, row],
        out_specs=row,
        out_shape=jax.ShapeDtypeStruct((S, D_MODEL), BF16),
        compiler_params=pltpu.CompilerParams(dimension_semantics=("parallel",)),
    )(gates, gates, o_a, o_b)


def _merge_bwd(dmrg, gates, o_a, o_b, name, tr=256):
    S = o_a.shape[0]

    def body(dm_ref, ga_ref, gb_ref, oa_ref, ob_ref, doa_ref, dob_ref, dga_ref, dgb_ref, dba_ref, dbb_ref):
        dm, ga, gb = dm_ref[...], ga_ref[...], gb_ref[...]
        doa_ref[...] = (dm * ga).astype(BF16)
        dob_ref[...] = (dm * gb).astype(BF16)
        dga = (dm * oa_ref[...]) * (ga * (1.0 - ga))
        dgb = (dm * ob_ref[...]) * (gb * (1.0 - gb))
        dga_ref[...] = dga.astype(BF16)
        dgb_ref[...] = dgb.astype(BF16)
        sa = jnp.sum(dga, axis=0, keepdims=True)
        sb = jnp.sum(dgb, axis=0, keepdims=True)

        @pl.when(pl.program_id(0) == 0)
        def _():
            dba_ref[...] = sa
            dbb_ref[...] = sb

        @pl.when(pl.program_id(0) != 0)
        def _():
            dba_ref[...] += sa
            dbb_ref[...] += sb

    row = pl.BlockSpec((tr, D_MODEL), lambda i: (i, 0))
    row1 = pl.BlockSpec((tr, D_MODEL), lambda i: (i, 1))
    vec = pl.BlockSpec((1, D_MODEL), lambda i: (0, 0))
    outs = pl.pallas_call(
        body,
        name=name,
        grid=(S // tr,),
        in_specs=[row, row, row1, row, row],
        out_specs=[row, row, row, row, vec, vec],
        out_shape=[jax.ShapeDtypeStruct((S, D_MODEL), BF16)] * 4 + [jax.ShapeDtypeStruct((1, D_MODEL), F32)] * 2,
        compiler_params=pltpu.CompilerParams(dimension_semantics=("arbitrary",)),
    )(dmrg, gates, gates, o_a, o_b)
    return outs


CONV_TR = 512
CONV_TC = 512
N_FFC = D_FF_PAD // CONV_TC


def _conv_taps(x, before, w_ref, b_ref):
    x0 = jnp.concatenate([before, x], axis=0)
    x1 = pltpu.roll(x0, 1, 0)
    x2 = pltpu.roll(x0, 2, 0)
    u = ((b_ref[...] + w_ref[0:1, :] * x2) + w_ref[1:2, :] * x1) + w_ref[2:3, :] * x0
    return u, x0, x1, x2


def _prev_halo(tr):
    return lambda i, j: (jnp.maximum(i * (tr // 8) - 1, 0), j)


def _ffn_fwd(u0, cw, cb, name):
    S = u0.shape[0]
    tr, tc = CONV_TR, CONV_TC

    def body(up_ref, gt_ref, hup_ref, hgt_ref, wu_ref, wg_ref, bu_ref, bg_ref, a_ref):
        live = (pl.program_id(0) > 0).astype(F32)
        up = _conv_taps(up_ref[...], hup_ref[...] * live, wu_ref, bu_ref)[0][8:]
        gt = _conv_taps(gt_ref[...], hgt_ref[...] * live, wg_ref, bg_ref)[0][8:]
        a_ref[...] = ((gt * jax.nn.sigmoid(gt)) * up).astype(BF16)

    return pl.pallas_call(
        body,
        name=name,
        grid=(S // tr, N_FFC),
        in_specs=[
            pl.BlockSpec((tr, tc), lambda i, j: (i, j)),
            pl.BlockSpec((tr, tc), lambda i, j: (i, j + N_FFC)),
            pl.BlockSpec((8, tc), _prev_halo(tr)),
            pl.BlockSpec((8, tc), lambda i, j: (jnp.maximum(i * (tr // 8) - 1, 0), j + N_FFC)),
            pl.BlockSpec((8, tc), lambda i, j: (0, j)),
            pl.BlockSpec((8, tc), lambda i, j: (0, j + N_FFC)),
            pl.BlockSpec((1, tc), lambda i, j: (0, j)),
            pl.BlockSpec((1, tc), lambda i, j: (0, j + N_FFC)),
        ],
        out_specs=pl.BlockSpec((tr, tc), lambda i, j: (i, j)),
        out_shape=jax.ShapeDtypeStruct((S, D_FF_PAD), BF16),
        compiler_params=pltpu.CompilerParams(dimension_semantics=("parallel", "parallel")),
    )(u0, u0, u0, u0, cw, cw, cb, cb)


def _ffn_bwd(u0, da, cw, cb, name):
    S = u0.shape[0]
    tr, tc = CONV_TR, CONV_TC
    nrow, te = S // tr, tr + 8

    def body(up_ref, gt_ref, hup_ref, hgt_ref, nup_ref, ngt_ref, da_ref, nda_ref, wu_ref, wg_ref, bu_ref, bg_ref, du0_ref, dcw_ref, dcb_ref):
        i = pl.program_id(1)
        prev_live = (i > 0).astype(F32)
        next_live = (i < nrow - 1).astype(F32)

        def conv(x_ref, nx_ref, h_ref, w_ref, b_ref):
            x = jnp.concatenate([x_ref[...], nx_ref[...] * next_live], axis=0)
            return [t[8:] for t in _conv_taps(x, h_ref[...] * prev_live, w_ref, b_ref)]

        up, xu0, xu1, xu2 = conv(up_ref, nup_ref, hup_ref, wu_ref, bu_ref)
        gt, xg0, xg1, xg2 = conv(gt_ref, ngt_ref, hgt_ref, wg_ref, bg_ref)
        da_v = jnp.concatenate([da_ref[...], nda_ref[...] * next_live], axis=0)
        sg = jax.nn.sigmoid(gt)
        d_up = da_v * (gt * sg)
        d_gt = (da_v * up) * (sg * (1.0 + gt * (1.0 - sg)))
        tap = lax.broadcasted_iota(jnp.int32, (8, tc), 0)

        def finish(half, du, x0, x1, x2, w_ref):
            n1 = pltpu.roll(du, te - 1, 0)
            n2 = pltpu.roll(du, te - 2, 0)
            du0 = (w_ref[2:3, :] * du + w_ref[1:2, :] * n1) + w_ref[0:1, :] * n2
            du0_ref[half] = du0[:tr].astype(BF16)
            d = du[:tr]
            dcw = jnp.where(
                tap == 0,
                jnp.sum(d * x2[:tr], axis=0, keepdims=True),
                jnp.where(tap == 1, jnp.sum(d * x1[:tr], axis=0, keepdims=True), jnp.where(tap == 2, jnp.sum(d * x0[:tr], axis=0, keepdims=True), 0.0)),
            )
            dcb = jnp.sum(d, axis=0, keepdims=True)

            @pl.when(i == 0)
            def _():
                dcw_ref[half] = dcw
                dcb_ref[half] = dcb

            @pl.when(i != 0)
            def _():
                dcw_ref[half] += dcw
                dcb_ref[half] += dcb

        finish(0, d_up, xu0, xu1, xu2, wu_ref)
        finish(1, d_gt, xg0, xg1, xg2, wg_ref)

    def prev8(off):
        return pl.BlockSpec((8, tc), lambda j, i: (jnp.maximum(i * (tr // 8) - 1, 0), j + off))

    def next8(off):
        return pl.BlockSpec((8, tc), lambda j, i: (jnp.minimum((i + 1) * (tr // 8), S // 8 - 1), j + off))

    return pl.pallas_call(
        body,
        name=name,
        grid=(N_FFC, nrow),
        in_specs=[
            pl.BlockSpec((tr, tc), lambda j, i: (i, j)),
            pl.BlockSpec((tr, tc), lambda j, i: (i, j + N_FFC)),
            prev8(0),
            prev8(N_FFC),
            next8(0),
            next8(N_FFC),
            pl.BlockSpec((tr, tc), lambda j, i: (i, j)),
            next8(0),
            pl.BlockSpec((8, tc), lambda j, i: (0, j)),
            pl.BlockSpec((8, tc), lambda j, i: (0, j + N_FFC)),
            pl.BlockSpec((1, tc), lambda j, i: (0, j)),
            pl.BlockSpec((1, tc), lambda j, i: (0, j + N_FFC)),
        ],
        out_specs=[
            pl.BlockSpec((2, tr, tc), lambda j, i: (0, i, j)),
            pl.BlockSpec((2, 8, tc), lambda j, i: (0, 0, j)),
            pl.BlockSpec((2, 1, tc), lambda j, i: (0, 0, j)),
        ],
        out_shape=[
            jax.ShapeDtypeStruct((2, S, D_FF_PAD), BF16),
            jax.ShapeDtypeStruct((2, 8, D_FF_PAD), F32),
            jax.ShapeDtypeStruct((2, 1, D_FF_PAD), F32),
        ],
        compiler_params=pltpu.CompilerParams(dimension_semantics=("parallel", "arbitrary")),
    )(u0, u0, u0, u0, u0, u0, da, da, cw, cw, cb, cb)


ADAMW_BLOCK_BYTES = 3 << 20


def _adamw(w, g, m, v, name):
    R, C = w.shape
    fits = [t for t in range(8, R + 1, 8) if R % t == 0 and t * C * 4 <= ADAMW_BLOCK_BYTES]
    tr = max(fits) if fits else R

    def body(w_ref, g_ref, m_ref, v_ref, d_ref, nm_ref, nv_ref):
        gv = g_ref[...]
        nm = ADAM_B1 * m_ref[...] + (1.0 - ADAM_B1) * gv
        nv = ADAM_B2 * v_ref[...] + (1.0 - ADAM_B2) * (gv * gv)
        m_hat = nm / (1.0 - ADAM_B1**ADAM_STEP)
        v_hat = nv / (1.0 - ADAM_B2**ADAM_STEP)
        d_ref[...] = -ADAM_LR * (m_hat / (jnp.sqrt(v_hat) + ADAM_EPS) + ADAM_WD * w_ref[...])
        nm_ref[...] = nm
        nv_ref[...] = nv

    blk = pl.BlockSpec((tr, C), lambda i: (i, 0))
    return pl.pallas_call(
        body,
        name=name,
        grid=(R // tr,),
        in_specs=[blk] * 4,
        out_specs=[blk] * 3,
        out_shape=[jax.ShapeDtypeStruct((R, C), F32)] * 3,
        compiler_params=pltpu.CompilerParams(dimension_semantics=("parallel",)),
    )(w, g, m, v)


ANY = pl.BlockSpec(memory_space=pl.ANY)


def _row_tile(rows):
    return max(t for t in range(16, 353, 16) if rows % t == 0)


def _pair_add(g, recv, core, name):
    _, R, C = g.shape
    tr = _row_tile(R)

    def body(core_ref, g_ref, r_ref, o_ref):
        o_ref[...] = (g_ref[...].astype(F32) + r_ref[...].astype(F32)).astype(o_ref.dtype)

    return pl.pallas_call(
        body,
        name=name,
        grid_spec=pltpu.PrefetchScalarGridSpec(
            num_scalar_prefetch=1,
            grid=(N_CHIP, R // tr),
            in_specs=[
                pl.BlockSpec((1, tr, C), lambda k, i, core_ref: (2 * k + core_ref[0], i, 0)),
                pl.BlockSpec((1, tr, C), lambda k, i, core_ref: (k, i, 0)),
            ],
            out_specs=pl.BlockSpec((1, tr, C), lambda k, i, core_ref: (k, i, 0)),
        ),
        out_shape=jax.ShapeDtypeStruct((N_CHIP, R, C), g.dtype),
        compiler_params=pltpu.CompilerParams(dimension_semantics=("parallel", "parallel")),
    )(core, g, recv)


HBM = pl.BlockSpec(memory_space=pltpu.HBM)
SEM = pl.BlockSpec(memory_space=pltpu.SEMAPHORE)
EFFECT = pltpu.SideEffectType.DATAFLOW_SIDE_EFFECTING
RELATIONS = tuple((dx, dy, dc) for dx in (0, 1) for dy in (0, 1) for dc in (0, 1))[1:]


def _related(rel):
    x, y, c = lax.axis_index("x"), lax.axis_index("y"), lax.axis_index("c")
    return (1 - x if rel[0] else x, 1 - y if rel[1] else y, 1 - c if rel[2] else c)


def _dev_index(pos):
    return 4 * pos[0] + 2 * pos[1] + pos[2]


def _peers(chips):
    if chips:
        return [r for r in RELATIONS if not r[2]], N_CHIP, lambda pos: 2 * pos[0] + pos[1]
    return list(RELATIONS), N_DEV, _dev_index


def _exchange_start(srcs, by_slot, after, name, chips=False):
    n = len(srcs)
    extra = [] if after is None else [after]
    rels, slots, slot_of = _peers(chips)
    lands = [lax.empty((slots,) + (s.shape[1:] if by_slot else s.shape), s.dtype) for s in srcs]
    nsem = len(rels) * n

    def body(*refs):
        src_refs, land_refs = refs[:n], refs[n : 2 * n]
        send_sems, recv_sems = refs[2 * n + len(extra)], refs[2 * n + len(extra) + 1]
        token = refs[-1]
        me = slot_of(_related((0, 0, 0)))
        for a in range(n):
            for k, rel in enumerate(rels):
                peer = _related(rel)
                pltpu.make_async_remote_copy(
                    src_ref=src_refs[a].at[slot_of(peer)] if by_slot else src_refs[a],
                    dst_ref=land_refs[a].at[me],
                    send_sem=send_sems.at[len(rels) * a + k],
                    recv_sem=recv_sems.at[len(rels) * a + k],
                    device_id=peer,
                    device_id_type=MESH,
                ).start()
        token[...] = jnp.zeros_like(token)

    def hbm(a):
        return pltpu.HBM(a.shape, a.dtype)

    outs = pl.pallas_call(
        body,
        name=name,
        out_shape=(
            pltpu.SemaphoreType.DMA((nsem,)),
            pltpu.SemaphoreType.DMA((nsem,)),
            *[hbm(s) for s in srcs],
            *[hbm(l) for l in lands],
            jax.ShapeDtypeStruct((8, 128), F32),
        ),
        in_specs=[HBM] * (2 * n) + [ANY] * len(extra),
        out_specs=(SEM, SEM, *[HBM] * (2 * n), pl.BlockSpec(memory_space=pltpu.VMEM)),
        input_output_aliases={i: 2 + i for i in range(2 * n)},
        compiler_params=pltpu.CompilerParams(has_side_effects=EFFECT),
    )(*[pltpu.with_memory_space_constraint(a, pltpu.HBM) for a in list(srcs) + lands], *extra)
    return (outs[0], outs[1], list(outs[2 : 2 + n]), list(outs[2 + n : 2 + 2 * n])), outs[-1]


def _exchange_wait(handle, by_slot, after, name, chips=False):
    send_sems, recv_sems, srcs, lands = handle
    n = len(srcs)
    rels = _peers(chips)[0]

    def body(*refs):
        src_refs, land_refs = refs[:n], refs[n : 2 * n]
        s_sems, r_sems = refs[2 * n], refs[2 * n + 1]
        for a in range(n):
            for k, rel in enumerate(rels):
                copy = pltpu.make_async_remote_copy(
                    src_ref=src_refs[a].at[0] if by_slot else src_refs[a],
                    dst_ref=land_refs[a].at[0],
                    send_sem=s_sems.at[len(rels) * a + k],
                    recv_sem=r_sems.at[len(rels) * a + k],
                    device_id=_related(rel),
                    device_id_type=MESH,
                )
                copy.wait_send()
                copy.wait_recv()

    outs = pl.pallas_call(
        body,
        name=name,
        out_shape=tuple(pltpu.HBM(a.shape, a.dtype) for a in srcs + lands),
        in_specs=[HBM] * (2 * n) + [SEM, SEM, ANY],
        out_specs=tuple([HBM] * (2 * n)),
        input_output_aliases={i: i for i in range(2 * n)},
        compiler_params=pltpu.CompilerParams(has_side_effects=EFFECT),
    )(*srcs, *lands, send_sems, recv_sems, after)
    return list(outs[:n]), list(outs[n:])


def _pair_start(g, name):
    land = lax.empty((N_CHIP,) + g.shape[1:], g.dtype)

    def body(g_ref, land_ref, send_sems, recv_sems, g_thru, land_thru, token):
        c = lax.axis_index("c")
        for k in range(N_CHIP):
            pltpu.make_async_remote_copy(
                src_ref=g_ref.at[2 * k + (1 - c)],
                dst_ref=land_ref.at[k],
                send_sem=send_sems.at[k],
                recv_sem=recv_sems.at[k],
                device_id=_related((0, 0, 1)),
                device_id_type=MESH,
            ).start()
        token[...] = jnp.zeros_like(token)

    outs = pl.pallas_call(
        body,
        name=name,
        out_shape=(
            pltpu.SemaphoreType.DMA((N_CHIP,)),
            pltpu.SemaphoreType.DMA((N_CHIP,)),
            pltpu.HBM(g.shape, g.dtype),
            pltpu.HBM(land.shape, land.dtype),
            jax.ShapeDtypeStruct((8, 128), F32),
        ),
        in_specs=[HBM, HBM],
        out_specs=(SEM, SEM, HBM, HBM, pl.BlockSpec(memory_space=pltpu.VMEM)),
        input_output_aliases={0: 2, 1: 3},
        compiler_params=pltpu.CompilerParams(has_side_effects=EFFECT),
    )(pltpu.with_memory_space_constraint(g, pltpu.HBM), pltpu.with_memory_space_constraint(land, pltpu.HBM))
    return outs[:4], outs[4]


def _pair_wait(handle, after, name):
    send_sems, recv_sems, g, land = handle

    def body(g_ref, land_ref, s_sems, r_sems, _, g_out, land_out):
        for k in range(N_CHIP):
            copy = pltpu.make_async_remote_copy(
                src_ref=g_ref.at[0],
                dst_ref=land_ref.at[0],
                send_sem=s_sems.at[k],
                recv_sem=r_sems.at[k],
                device_id=_related((0, 0, 1)),
                device_id_type=MESH,
            )
            copy.wait_send()
            copy.wait_recv()

    return pl.pallas_call(
        body,
        name=name,
        out_shape=(pltpu.HBM(g.shape, g.dtype), pltpu.HBM(land.shape, land.dtype)),
        in_specs=[HBM, HBM, SEM, SEM, ANY],
        out_specs=(HBM, HBM),
        input_output_aliases={0: 0, 1: 1},
        compiler_params=pltpu.CompilerParams(has_side_effects=EFFECT),
    )(g, land, send_sems, recv_sems, after)


NEAR = ((0, 0, 1), (1, 0, 0), (0, 1, 0), (1, 1, 0))


def _gather2_start(blocks, name):
    n = len(blocks)
    lands = [lax.empty((N_DEV,) + b.shape, b.dtype) for b in blocks]

    def body(*refs):
        src_refs, land_refs = refs[:n], refs[n : 2 * n]
        send_sems, recv_sems, token = refs[2 * n], refs[2 * n + 1], refs[-1]
        me = _dev_index(_related((0, 0, 0)))
        for a in range(n):
            for k, rel in enumerate(NEAR):
                pltpu.make_async_remote_copy(
                    src_ref=src_refs[a],
                    dst_ref=land_refs[a].at[me],
                    send_sem=send_sems.at[len(NEAR) * a + k],
                    recv_sem=recv_sems.at[len(NEAR) * a + k],
                    device_id=_related(rel),
                    device_id_type=MESH,
                ).start()
        token[...] = jnp.zeros_like(token)

    nsem = len(NEAR) * n
    outs = pl.pallas_call(
        body,
        name=name,
        out_shape=(
            pltpu.SemaphoreType.DMA((nsem,)),
            pltpu.SemaphoreType.DMA((nsem,)),
            *[pltpu.HBM(a.shape, a.dtype) for a in list(blocks) + lands],
            jax.ShapeDtypeStruct((8, 128), F32),
        ),
        in_specs=[HBM] * (2 * n),
        out_specs=(SEM, SEM, *[HBM] * (2 * n), pl.BlockSpec(memory_space=pltpu.VMEM)),
        input_output_aliases={i: 2 + i for i in range(2 * n)},
        compiler_params=pltpu.CompilerParams(has_side_effects=EFFECT),
    )(*[pltpu.with_memory_space_constraint(a, pltpu.HBM) for a in list(blocks) + lands])
    return (outs[0], outs[1], list(outs[2 : 2 + n]), list(outs[2 + n : 2 + 2 * n])), outs[-1]


def _gather2_forward(handle, after, name):
    send1, recv1, srcs, lands = handle
    n = len(srcs)

    def body(*refs):
        src_refs, land_refs = refs[:n], refs[n : 2 * n]
        s1, r1 = refs[2 * n], refs[2 * n + 1]
        s2, r2 = refs[-2], refs[-1]
        sibling = _related(NEAR[0])
        for a in range(n):
            for k, rel in enumerate(NEAR):
                first = pltpu.make_async_remote_copy(
                    src_ref=src_refs[a],
                    dst_ref=land_refs[a].at[0],
                    send_sem=s1.at[len(NEAR) * a + k],
                    recv_sem=r1.at[len(NEAR) * a + k],
                    device_id=_related(rel),
                    device_id_type=MESH,
                )
                first.wait_send()
                first.wait_recv()
                if k:
                    slot = land_refs[a].at[_dev_index(_related(rel))]
                    pltpu.make_async_remote_copy(
                        src_ref=slot,
                        dst_ref=slot,
                        send_sem=s2.at[3 * a + k - 1],
                        recv_sem=r2.at[3 * a + k - 1],
                        device_id=sibling,
                        device_id_type=MESH,
                    ).start()

    outs = pl.pallas_call(
        body,
        name=name,
        out_shape=(
            *[pltpu.HBM(a.shape, a.dtype) for a in srcs + lands],
            pltpu.SemaphoreType.DMA((3 * n,)),
            pltpu.SemaphoreType.DMA((3 * n,)),
        ),
        in_specs=[HBM] * (2 * n) + [SEM, SEM, ANY],
        out_specs=(*[HBM] * (2 * n), SEM, SEM),
        input_output_aliases={i: i for i in range(2 * n)},
        compiler_params=pltpu.CompilerParams(has_side_effects=EFFECT),
    )(*srcs, *lands, send1, recv1, after)
    return outs[-2], outs[-1], list(outs[:n]), list(outs[n : 2 * n])


def _gather2_wait(handle, name):
    send2, recv2, srcs, lands = handle
    n = len(srcs)

    def body(*refs):
        land_refs = refs[n : 2 * n]
        s2, r2 = refs[2 * n], refs[2 * n + 1]
        for a in range(n):
            for j in range(3):
                passed = pltpu.make_async_remote_copy(
                    src_ref=land_refs[a].at[0],
                    dst_ref=land_refs[a].at[0],
                    send_sem=s2.at[3 * a + j],
                    recv_sem=r2.at[3 * a + j],
                    device_id=_related(NEAR[0]),
                    device_id_type=MESH,
                )
                passed.wait_send()
                passed.wait_recv()

    outs = pl.pallas_call(
        body,
        name=name,
        out_shape=tuple(pltpu.HBM(a.shape, a.dtype) for a in srcs + lands),
        in_specs=[HBM] * (2 * n) + [SEM, SEM],
        out_specs=tuple([HBM] * (2 * n)),
        input_output_aliases={i: i for i in range(2 * n)},
        compiler_params=pltpu.CompilerParams(has_side_effects=EFFECT),
    )(*srcs, *lands, send2, recv2)
    return list(outs[:n]), list(outs[n:])


def _slot_sum(parts, name, keep=None):
    n, R, C = parts.shape
    if keep is not None:
        tc = 256
        rows = sum(size for _, size in keep)

        def kept(p_ref, o_ref):
            acc = p_ref[0].astype(F32)
            for k in range(1, n):
                acc = acc + p_ref[k].astype(F32)
            off = 0
            for start, size in keep:
                o_ref[off : off + size, :] = acc[start : start + size, :]
                off += size

        return pl.pallas_call(
            kept,
            name=name,
            grid=(C // tc,),
            in_specs=[pl.BlockSpec((n, R, tc), lambda j: (0, 0, j))],
            out_specs=pl.BlockSpec((rows, tc), lambda j: (0, j)),
            out_shape=jax.ShapeDtypeStruct((rows, C), F32),
            compiler_params=pltpu.CompilerParams(dimension_semantics=("parallel",)),
        )(parts)
    tr = _row_tile(R) if R % 16 == 0 else R

    def body(p_ref, o_ref):
        acc = p_ref[0].astype(F32)
        for k in range(1, n):
            acc = acc + p_ref[k].astype(F32)
        o_ref[...] = acc

    return pl.pallas_call(
        body,
        name=name,
        grid=(R // tr,),
        in_specs=[pl.BlockSpec((n, tr, C), lambda i: (0, i, 0))],
        out_specs=pl.BlockSpec((tr, C), lambda i: (i, 0)),
        out_shape=jax.ShapeDtypeStruct((R, C), F32),
        compiler_params=pltpu.CompilerParams(dimension_semantics=("parallel",)),
    )(parts)


W_IN_TC = 256
W_IN_BOUNDS = (0, LAT, LAT + 3 * DIL_QKV, LAT + 3 * DIL_QKV + D_MODEL, D_IN)


def _dqkv_chunks():
    return [((g * 3 + part) * DIL_OUT, LAT + part * DIL_QKV + g * DIL_OUT) for g in range(DIL_GROUPS) for part in range(3)]


def _w_in_regroup(slots, after, name):
    tc = W_IN_TC

    def body(s_ref, _, lat_ref, dqkv_ref, g_ref, buf):
        for j in range(N_DEV):
            buf[j * IN_ROWS : (j + 1) * IN_ROWS, :] = s_ref[j].astype(F32)[:IN_ROWS, :]
        lat_ref[:LAT, :] = buf[:LAT, :].astype(BF16)
        lat_ref[LAT:, :] = jnp.zeros((LAT_PAD - LAT, tc), BF16)
        for dst, src in _dqkv_chunks():
            dqkv_ref[dst : dst + DIL_OUT, :] = buf[src : src + DIL_OUT, :].astype(BF16)
        g_ref[...] = buf[W_IN_BOUNDS[2] :, :].astype(BF16)

    def col(rows):
        return pl.BlockSpec((rows, tc), lambda k: (0, k))

    return pl.pallas_call(
        body,
        name=name,
        grid=(D_MODEL // tc,),
        in_specs=[pl.BlockSpec((N_DEV, IN_ROWS_PAD, tc), lambda k: (0, 0, k)), pl.BlockSpec((8, 128), lambda k: (0, 0))],
        out_specs=[col(LAT_PAD), col(3 * DIL_QKV), col(2 * D_MODEL)],
        out_shape=[
            jax.ShapeDtypeStruct((LAT_PAD, D_MODEL), BF16),
            jax.ShapeDtypeStruct((3 * DIL_QKV, D_MODEL), BF16),
            jax.ShapeDtypeStruct((2 * D_MODEL, D_MODEL), BF16),
        ],
        scratch_shapes=[pltpu.VMEM((D_IN, tc), F32)],
        compiler_params=pltpu.CompilerParams(dimension_semantics=("parallel",)),
    )(slots, after)


def _w_in_grad_regroup(g_lat, g_dqkv, g_ga, g_gb, name):
    tc = W_IN_TC

    def body(lat_ref, dqkv_ref, ga_ref, gb_ref, o_ref, buf):
        b = W_IN_BOUNDS
        buf[b[0] : b[1], :] = lat_ref[:LAT, :].astype(F32)
        for dst, src in _dqkv_chunks():
            buf[src : src + DIL_OUT, :] = dqkv_ref[dst : dst + DIL_OUT, :].astype(F32)
        buf[b[2] : b[3], :] = ga_ref[...].astype(F32)
        buf[b[3] : b[4], :] = gb_ref[...].astype(F32)
        fill = jnp.zeros((IN_ROWS_PAD - IN_ROWS, tc), F32)
        for j in range(N_DEV):
            o_ref[j] = jnp.concatenate([buf[j * IN_ROWS : (j + 1) * IN_ROWS, :], fill], axis=0).astype(BF16)

    def col(rows):
        return pl.BlockSpec((rows, tc), lambda k: (0, k))

    return pl.pallas_call(
        body,
        name=name,
        grid=(D_MODEL // tc,),
        in_specs=[col(LAT_PAD), col(3 * DIL_QKV), col(D_MODEL), col(D_MODEL)],
        out_specs=pl.BlockSpec((N_DEV, IN_ROWS_PAD, tc), lambda k: (0, 0, k)),
        out_shape=jax.ShapeDtypeStruct((N_DEV, IN_ROWS_PAD, D_MODEL), BF16),
        scratch_shapes=[pltpu.VMEM((D_IN, tc), F32)],
        compiler_params=pltpu.CompilerParams(dimension_semantics=("parallel",)),
    )(g_lat, g_dqkv, g_ga, g_gb)


def _ffn_pad(a, axis):
    a = jnp.moveaxis(a, axis, -1)
    g = a.reshape(a.shape[:-1] + (2 * N_DEV, FF_GROUP))
    g = jnp.pad(g, [(0, 0)] * (g.ndim - 1) + [(0, FF_GROUP_PAD - FF_GROUP)])
    return jnp.moveaxis(g.reshape(a.shape[:-1] + (2 * D_FF_PAD,)), -1, axis)


def _ffn_unpad(a, axis):
    a = jnp.moveaxis(a, axis, -1)
    g = a.reshape(a.shape[:-1] + (2 * N_DEV, FF_GROUP_PAD))[..., :FF_GROUP]
    return jnp.moveaxis(g.reshape(a.shape[:-1] + (2 * D_FF,)), -1, axis)


MISC = (("w_o_mla", (256, 1024)), ("w_o_dil", (256, 512)), ("w_uq", (192, 512)), ("w_ukv", (256, 256)))
BIG_WEIGHTS = ("w_in", "w_up", "w_down", "w_out") + tuple(n for n, _ in MISC)


def _exchange_blocks(w):
    def t(a):
        return a.astype(BF16).T

    up = t(w["w_up"]).reshape(2, FF_GROUP, D_MODEL)
    return [
        jnp.pad(t(w["w_in"]), ((0, IN_ROWS_PAD - IN_ROWS), (0, 0))),
        jnp.pad(up, ((0, 0), (0, FF_GROUP_PAD - FF_GROUP), (0, 0))).reshape(2 * FF_GROUP_PAD, D_MODEL),
        jnp.pad(w["w_down"].astype(BF16), ((0, FF_GROUP_PAD - FF_GROUP), (0, 0))),
        w["w_out"].astype(BF16),
        jnp.concatenate([t(w[n]).reshape(-1, D_MODEL) for n, _ in MISC], axis=0),
    ]


def _misc_split(misc):
    out, off = {}, 0
    for n, (r, c) in MISC:
        rows = r * c // D_MODEL
        out[n] = misc[..., off : off + rows, :].reshape(misc.shape[:-2] + (r, c))
        off += rows
    return out


def _small_matrices(g_misc):
    misc = _misc_split(g_misc)
    uq_t = jnp.pad(misc["w_uq"], ((0, 0), (0, HEAD_PAD - QK_NOPE - QK_ROPE), (0, 0)))
    return {
        "uq_t": uq_t.reshape(MLA_HEADS * HEAD_PAD, Q_LORA),
        "ukv_t": misc["w_ukv"].reshape(MLA_HEADS * HEAD_PAD, KV_LORA),
        "o_mla_t": misc["w_o_mla"].reshape(D_MODEL, MLA_HEADS * V_HEAD),
        "o_dil_t": misc["w_o_dil"].reshape(D_MODEL, DIL_OUT),
    }


def _small_grad_blocks(g):
    uq_t = g["uq_t"].reshape(MLA_HEADS, HEAD_PAD, Q_LORA)[:, : QK_NOPE + QK_ROPE]
    misc = {"w_o_mla": g["o_mla_t"], "w_o_dil": g["o_dil_t"], "w_uq": uq_t, "w_ukv": g["ukv_t"]}
    return [
        g["w_out"].reshape(N_DEV, -1, D_MODEL),
        jnp.concatenate([misc[n].reshape(N_DEV, -1, D_MODEL) for n, _ in MISC], axis=1),
    ]


def _grad_shards(sums):
    s_in, s_out, s_misc, s_up, s_down = sums
    out = {
        "w_in": s_in.T,
        "w_up": s_up.T,
        "w_down": s_down,
        "w_out": s_out,
    }
    out.update({n: v.T for n, v in _misc_split(s_misc).items()})
    return out


def _local_step(x, h, tgt, wt, conv_w, small, small_matrices, ffn_weight, send_ffn_grads, send_small_grads, send_w_in_grads, forward_w_in_grads):
    S = x.shape[0]
    lat_t, dqkv_t, g_t = wt
    cw = jnp.pad(_ffn_pad(conv_w, 1), ((0, 5), (0, 0)))
    cb = _ffn_pad(small["conv_b"], 1)
    cos_t, sin_t = _rope_tables(S)
    bias = _dil_bias()
    g1, g2, g3 = small["attn_norm_g"], small["ffn_norm_g"], small["final_norm_g"]
    gq, gkv = small["q_norm_g"], small["kv_norm_g"]

    lat = _mm(h, lat_t, "nt", F32, 1024, LAT_PAD, D_MODEL, "proj_lat")
    dqkv = _mm(h, dqkv_t, "nt", F32, 1024, 1536, D_MODEL, "proj_dqkv")
    gates = _mm(h, g_t, "nt", F32, 1024, 1024, D_MODEL, "proj_gates", bias=small["b_gate"], act="sigmoid")
    sm = small_matrices(gates)
    uq_t, ukv_t, o_mla_t, o_dil_t = sm["uq_t"], sm["ukv_t"], sm["o_mla_t"], sm["o_dil_t"]
    cqn, ckvn, kpe = _mla_prep1(lat, gq, gkv, cos_t, sin_t, "mla_prep1")
    q_raw = _mm(cqn, uq_t, "nt", F32, 1024, 1024, Q_LORA, "mla_uq")
    kv = _mm(ckvn, ukv_t, "nt", BF16, 1024, 1024, KV_LORA, "mla_ukv")
    q_att, k_att = _mla_prep2(q_raw, kv, kpe, cos_t, sin_t, "mla_prep2")
    o, lse = _flash2_fwd(q_att, k_att, kv, "mla_flash_fwd")
    o_a = _mm(o, o_mla_t, "nt", F32, 1024, 1024, MLA_HEADS * V_HEAD, "mla_out")

    d_os, d_ls = [], []
    for g, (_, dil) in enumerate(DIL_PATTERNS):
        og, lg = _dil_fwd_group(dqkv, bias[g], g, dil, f"dil_fwd_{g}")
        d_os.append(og)
        d_ls.append(lg)
    od, dil_lse = _dil_combine(d_os, d_ls, "dil_combine")
    o_b = _mm(od, o_dil_t, "nt", F32, 1024, 1024, DIL_OUT, "dil_out")

    mrg = _merge_fwd(gates, o_a, o_b, "merge_fwd")
    w_out = ffn_weight("w_out", mrg)
    x1, h2 = _mm_res_rms(mrg, w_out, x, g2, "mix_out")
    up_t = ffn_weight("up_t", h2)
    u0 = _mm(h2, up_t, "nt", F32, 2048, 1024, D_MODEL, "ffn_up")
    a = _ffn_fwd(u0, cw, cb, "ffn_conv_fwd")
    w_down = ffn_weight("w_down", a)
    x2 = _mm(a, w_down, "nn", F32, 1024, 1024, D_FF_PAD // 2, "ffn_down", res=x1)
    loss_part, dx2, dx2b, dg3 = _final_loss(x2, g3, tgt, "final_loss")

    da = _mm(dx2b, w_down, "nt", F32, 1024, D_FF_PAD // 4, D_MODEL, "ffn_down_dx")
    gw_down = _mm(a, dx2b, "tn", BF16, 512, D_MODEL, S, "ffn_down_dw")
    du0, dcw, dcb = _ffn_bwd(u0, da, cw, cb, "ffn_conv_bwd")
    du0 = du0.reshape(2 * S, D_FF_PAD)
    gw_up_t = _mm(du0, h2, "tn", BF16, 512, D_MODEL, S, "ffn_up_dw", a_halves=2)
    sent = send_ffn_grads(gw_up_t, gw_down)
    dh2 = _mm(du0, up_t, "nn", F32, 1024, 1024, D_FF_PAD // 2, "ffn_up_dx", a_halves=2)
    dx1, dx1b, dg2 = _rms_bwd(dh2, x1, g2 + sent, dx2, "rms_ffn_bwd")

    dmrg = _mm(dx1b, w_out, "nt", F32, 1024, 1024, D_MODEL, "mix_out_dx")
    gw_out = _mm(mrg, dx1b, "tn", BF16, 512, D_MODEL, S, "mix_out_dw")
    do_a, do_b, dga, dgb, dba, dbb = _merge_bwd(dmrg, gates, o_a, o_b, "merge_bwd")

    do = _mm(do_a, o_mla_t, "nn", BF16, 1024, 1024, D_MODEL, "mla_out_dx")
    gw_o_mla_t = _mm(do_a, o, "tn", BF16, 1024, 1024, 1024, "mla_out_dw")
    dod = _mm(do_b, o_dil_t, "nn", F32, 1024, DIL_OUT, D_MODEL, "dil_out_dx")
    gw_o_dil_t = _mm(do_b, od, "tn", BF16, 1024, DIL_OUT, 1024, "dil_out_dw")

    delta = _flash_delta(do, o, "mla_flash_delta")
    lse_row = lse[:, :, 0][:, None, :]
    delta_row = delta[:, :MLA_HEADS].T[:, None, :]
    dq_att, dk_att, dv = _flash2_bwd(q_att, k_att, kv, do, lse_row, delta_row, "mla_flash_bwd")
    dq_raw, dkv, dkpe = _mla_post(dq_att, dk_att, dv, cos_t, sin_t, "mla_post")
    dcqn = _mm(dq_raw, uq_t, "nn", F32, 1024, Q_LORA, MLA_HEADS * HEAD_PAD, "mla_uq_dx")
    gw_uq_t = _mm(dq_raw, cqn, "tn", BF16, 1024, Q_LORA, 1024, "mla_uq_dw")
    dckvn = _mm(dkv, ukv_t, "nn", F32, 1024, KV_LORA, MLA_HEADS * HEAD_PAD, "mla_ukv_dx")
    gw_ukv_t = _mm(dkv, ckvn, "tn", BF16, 1024, KV_LORA, 1024, "mla_ukv_dw")
    sent = send_small_grads({"uq_t": gw_uq_t, "ukv_t": gw_ukv_t, "o_mla_t": gw_o_mla_t, "o_dil_t": gw_o_dil_t, "w_out": gw_out})
    dlat, dgq, dgkv = _lat_bwd(dcqn, dckvn, dkpe, lat, gq + sent, gkv, "lat_bwd")

    dd = _dil_rowdot(dod, od, "dil_rowdot")
    ddqkv = lax.empty((3 * DIL_GROUPS, S, DIL_OUT), F32)
    for g, (_, dil) in enumerate(DIL_PATTERNS):
        ddqkv = _dil_bwd_group(dqkv, bias[g], dod, dd, dil_lse, ddqkv, g, dil, f"dil_bwd_{g}")
    gw_lat_t = _mm(dlat, h, "tn", BF16, LAT_PAD, 1024, S, "proj_lat_dw")
    gw_dqkv_t = _mm(ddqkv.reshape(3 * DIL_GROUPS * S, DIL_OUT), h, "tn", BF16, 512, D_MODEL, S, "proj_dqkv_dw", a_halves=3 * DIL_GROUPS)
    gw_ga_t = _mm(dga, h, "tn", BF16, 512, D_MODEL, S, "proj_ga_dw")
    gw_gb_t = _mm(dgb, h, "tn", BF16, 512, D_MODEL, S, "proj_gb_dw")
    sent = send_w_in_grads(gw_lat_t, gw_dqkv_t, gw_ga_t, gw_gb_t)
    dh = _mm(dlat + sent.astype(BF16), lat_t, "nn", F32, 1024, 1024, LAT_PAD, "proj_lat_dx")
    dh = _stacked_mm(ddqkv, dqkv_t, dh, forward_w_in_grads(dh), "proj_dqkv_dx")
    dh = _mm(dga, g_t, "nn", F32, 1024, 1024, D_MODEL, "proj_ga_dx", res=dh)
    grad_x, dg1 = _mm_rms_bwd(dgb, g_t, 1, dh, x, g1, dx1, "proj_gb_dx_rms_attn_bwd")

    small_grads = {
        "attn_norm_g": dg1,
        "b_gate": jnp.concatenate([dba, dbb], axis=1),
        "q_norm_g": dgq,
        "kv_norm_g": dgkv,
        "ffn_norm_g": dg2,
        "conv_b": _ffn_unpad(jnp.concatenate([dcb[0], dcb[1]], axis=1), 1),
        "final_norm_g": dg3,
        "conv_w": _ffn_unpad(jnp.concatenate([dcw[0, :3], dcw[1, :3]], axis=1), 1),
    }
    return loss_part, grad_x, small_grads


SMALL_ORDER = ("attn_norm_g", "b_gate", "q_norm_g", "kv_norm_g", "ffn_norm_g", "conv_b", "final_norm_g", "conv_w")
WEIGHT_ORDER = (
    "attn_norm_g", "w_in", "b_gate", "q_norm_g", "w_uq", "kv_norm_g", "w_ukv", "w_o_mla", "w_o_dil", "w_out",
    "ffn_norm_g", "w_up", "conv_w", "conv_b", "w_down", "final_norm_g",
)


def kernel(x, attn_norm_g, w_in, b_gate, q_norm_g, w_uq, kv_norm_g, w_ukv, w_o_mla, w_o_dil, w_out, ffn_norm_g, w_up, conv_w, conv_b, w_down, final_norm_g, loss_target, m_attn_norm_g, m_w_in, m_b_gate, m_q_norm_g, m_w_uq, m_kv_norm_g, m_w_ukv, m_w_o_mla, m_w_o_dil, m_w_out, m_ffn_norm_g, m_w_up, m_conv_w, m_conv_b, m_w_down, m_final_norm_g, v_attn_norm_g, v_w_in, v_b_gate, v_q_norm_g, v_w_uq, v_kv_norm_g, v_w_ukv, v_w_o_mla, v_w_o_dil, v_w_out, v_ffn_norm_g, v_w_up, v_conv_w, v_conv_b, v_w_down, v_final_norm_g):
    env = dict(locals())
    dev = 4 * lax.axis_index("x") + 2 * lax.axis_index("y") + lax.axis_index("c")
    core = lax.axis_index("c").astype(jnp.int32).reshape(1)

    def two_d(a):
        return a.reshape(-1, a.shape[-1])

    w = {n: two_d(env[n]) for n in WEIGHT_ORDER}
    m = {n: two_d(env["m_" + n]) for n in WEIGHT_ORDER}
    v = {n: two_d(env["v_" + n]) for n in WEIGHT_ORDER}

    chip = 2 * lax.axis_index("x") + lax.axis_index("y")

    def own_slot_in(lands, own, slot=dev):
        return [lax.dynamic_update_slice(l, o[None], (slot, 0, 0)) for l, o in zip(lands, own)]

    b_in = _exchange_blocks(w)[0]
    r, c = CONV_SHARD
    conv = jnp.pad(w["conv_w"].reshape(-1), (0, 8 * SMALL_COLS - r * c)).reshape(8, SMALL_COLS)
    first_level, token = _gather2_start([b_in, conv], "ag_w_in_start")
    tied = {n: w[n] + token[0, 0] for n in BIG_WEIGHTS}
    _, b_up, b_down, b_out, b_misc = _exchange_blocks(tied)
    h = _rms_fwd(x[0], w["attn_norm_g"] + token[0, 0], "rms_attn")
    prepared = b_up[:1, :1] + b_down[:1, :1] + b_out[:1, :1] + b_misc[:1, :1] + h[:1, :1]
    own, lands = _gather2_wait(_gather2_forward(first_level, prepared, "ag_w_in_forward"), "ag_w_in_wait")
    g_in, conv = own_slot_in(lands, own)
    misc_gather, started = _exchange_start([b_misc], False, conv, "ag_small_start")
    ffn_gathers, started2 = {}, started
    for key, block in (("w_out", b_out), ("up_t", b_up), ("w_down", b_down)):
        ffn_gathers[key], started2 = _exchange_start([block], False, started2, f"ag_{key}_start")
    wt = _w_in_regroup(g_in, started2, "w_in_regroup")
    conv = conv.reshape(N_DEV, 8 * SMALL_COLS)[:, : r * c].reshape(N_DEV, r, c)
    conv_w_full = conv.transpose(1, 0, 2).reshape(r, N_DEV * c)
    small = {n: w[n] for n in SMALL_ORDER if n != "conv_w"}

    def small_matrices(after):
        own, lands = _exchange_wait(misc_gather, False, after, "ag_small_wait")
        return _small_matrices(own_slot_in(lands, own)[0])

    def ffn_weight(key, after):
        own, lands = _exchange_wait(ffn_gathers[key], False, after, f"ag_{key}_wait")
        return own_slot_in(lands, own)[0].reshape(-1, D_MODEL)

    reduces = {}

    def send_ffn_grads(gw_up_t, gw_down):
        blocks = [gw_up_t.reshape(N_DEV, 2 * FF_GROUP_PAD, D_MODEL), gw_down.reshape(N_DEV, FF_GROUP_PAD, D_MODEL)]
        reduces["ffn"], token = _exchange_start(blocks, True, None, "rs_ffn_start")
        return token[0, 0]

    def send_small_grads(g):
        reduces["small"], token = _exchange_start(_small_grad_blocks(g), True, None, "rs_small_start")
        return token[0, 0]

    def send_w_in_grads(g_lat, g_dqkv, g_ga, g_gb):
        e_in = _w_in_grad_regroup(g_lat, g_dqkv, g_ga, g_gb, "w_in_grad_regroup")
        reduces["pair"], token = _pair_start(e_in, "rs_w_in_pair_start")
        return token[0, 0]

    def forward_w_in_grads(after):
        e_in, recv = _pair_wait(reduces.pop("pair"), after, "rs_w_in_pair_wait")
        pair = _pair_add(e_in, recv, core, "rs_w_in_pair_add")
        reduces["w_in"], token = _exchange_start([pair], True, None, "rs_w_in_start", chips=True)
        return token

    loss_part, grad_x, small_grads = _local_step(
        x[0], h, loss_target[0], wt, conv_w_full, small, small_matrices, ffn_weight,
        send_ffn_grads, send_small_grads, send_w_in_grads, forward_w_in_grads,
    )
    loss = lax.psum(loss_part[0, 0], AXES)
    sflat = jnp.concatenate([small_grads[n].reshape(-1) for n in SMALL_ORDER])
    sflat = jnp.pad(sflat, (0, SMALL_ROWS * SMALL_COLS - sflat.shape[0])).reshape(SMALL_ROWS, SMALL_COLS)
    vec_gather, _ = _exchange_start([sflat], False, None, "rs_vec_start")

    def finish(key, by_chip, name, keeps):
        sent, lands = _exchange_wait(reduces[key], True, grad_x, name + "_wait", chips=by_chip)
        slot = chip if by_chip else dev
        own = [lax.dynamic_index_in_dim(s, slot, 0, keepdims=False) for s in sent]
        return [_slot_sum(p, f"{name}_sum_{i}", keeps[i]) for i, p in enumerate(own_slot_in(lands, own, slot))]

    group = (0, FF_GROUP)
    (s_in,) = finish("w_in", True, "rs_w_in", [((0, IN_ROWS),)])
    s_out, s_misc = finish("small", False, "rs_small", [None, None])
    s_up, s_down = finish("ffn", False, "rs_ffn", [(group, (FF_GROUP_PAD, FF_GROUP)), (group,)])
    gshard = _grad_shards([s_in, s_out, s_misc, s_up, s_down])

    updates = {n: _adamw(w[n], gshard[n], m[n], v[n], "adamw_" + n) for n in BIG_WEIGHTS}

    big_done = sum(updates[n][0][:1, :1] for n in BIG_WEIGHTS)
    own, lands = _exchange_wait(vec_gather, False, big_done, "rs_vec_wait")
    ssum = _slot_sum(own_slot_in(lands, own)[0], "small_sum").reshape(-1)
    gsmall, off = {}, 0
    for n in SMALL_ORDER:
        shape = (3, 2 * D_FF) if n == "conv_w" else w[n].shape
        size = shape[0] * shape[1]
        gsmall[n] = ssum[off : off + size].reshape(shape)
        off += size
    gsmall["conv_w"] = lax.dynamic_slice_in_dim(gsmall["conv_w"], dev * CONV_SHARD[1], CONV_SHARD[1], axis=1)
    updates.update({n: _adamw(w[n], gsmall[n], m[n], v[n], "adamw_" + n) for n in SMALL_ORDER})

    g_all = {**gshard, **gsmall}
    out_g, out_d, out_m, out_v = [], [], [], []
    for n in WEIGHT_ORDER:
        d, nm, nv = updates[n]
        shape = env[n].shape
        out_g.append(g_all[n].reshape(shape))
        out_d.append(d.reshape(shape))
        out_m.append(nm.reshape(shape))
        out_v.append(nv.reshape(shape))
    return (loss, grad_x[None], *out_g, *out_d, *out_m, *out_v)
```

```python
import jax
import jax.numpy as jnp
import numpy as np
from jax import lax
from jax.experimental import pallas as pl
from jax.experimental.pallas import tpu as pltpu

F32 = jnp.float32
BF16 = jnp.bfloat16

N_DEV = 8
N_CHIP = 4
AXES = ("x", "y", "c")
MESH = pl.DeviceIdType.MESH

D_MODEL = 2048
MLA_HEADS = 8
QK_NOPE = 128
QK_ROPE = 64
V_HEAD = 128
Q_LORA = 512
KV_LORA = 256
ROPE_THETA = 10000.0
HEAD_PAD = 256
DIL_PATTERNS = ((128, 1), (512, 4), (2048, 16))
DIL_GROUPS = 3
DIL_HG = 4
DIL_HEADS = 12
DIL_HD = 128
DIL_BLK = 128
DIL_QKV = DIL_HEADS * DIL_HD
DIL_OUT = DIL_HG * DIL_HD
ALIBI_MAX_BIAS = 8.0
D_FF = 5504
D_FF_PAD = 5632
NORM_EPS = 1e-6
LAT = Q_LORA + KV_LORA + QK_ROPE
LAT_PAD = 896
D_IN = LAT + 3 * DIL_QKV + 2 * D_MODEL
NEG = -1e30

ADAM_LR = 0.001
ADAM_B1 = 0.9
ADAM_B2 = 0.999
ADAM_EPS = 1e-08
ADAM_WD = 0.01
ADAM_STEP = 10

SMALL_ROWS = 56
SMALL_COLS = 1024

IN_ROWS = 1192
IN_ROWS_PAD = 1200
FF_GROUP = D_FF // N_DEV
FF_GROUP_PAD = D_FF_PAD // N_DEV
CONV_SHARD = (3, 1376)

NT = (((1,), (1,)), ((), ()))
TN = (((0,), (0,)), ((), ()))


def _dot(a, b, dims=(((1,), (0,)), ((), ()))):
    return lax.dot_general(a, b, dims, preferred_element_type=F32)


def _mm(a, b, mode, out_dtype, tm, tn, tk, name, bias=None, act=None, res=None, b_koff=0, a_halves=1):
    H = a_halves
    if mode == "nn":
        (M, K), (K2, N) = (a.shape[0] // H, a.shape[1] * H), b.shape
        assert (b_koff + 1) * K <= K2, (name, a.shape, b.shape)
        koff, K2 = b_koff * (K // tk), K
        kper, mrows = a.shape[1] // tk, M // tm
        a_spec = pl.BlockSpec((tm, tk), lambda i, j, k: (i + (k // kper) * mrows, k % kper))
        b_spec = pl.BlockSpec((tk, tn), lambda i, j, k: (k + koff, j))
        dims = (((1,), (0,)), ((), ()))
    elif mode == "nt":
        (M, K), (N, K2) = a.shape, b.shape
        a_spec = pl.BlockSpec((tm, tk), lambda i, j, k: (i, k))
        b_spec = pl.BlockSpec((tn, tk), lambda i, j, k: (j, k))
        dims = NT
    else:
        (K, M), (K2, N) = (a.shape[0] // H, a.shape[1] * H), b.shape
        mper, krows = a.shape[1] // tm, K // tk
        a_spec = pl.BlockSpec((tk, tm), lambda i, j, k: (k + (i // mper) * krows, i % mper))
        b_spec = pl.BlockSpec((tk, tn), lambda i, j, k: (k, j))
        dims = TN
    assert K == K2 and M % tm == 0 and N % tn == 0 and K % tk == 0, (name, a.shape, b.shape)
    nk = K // tk
    has_bias, has_res = bias is not None, res is not None

    def body(*refs):
        refs = list(refs)
        a_ref, b_ref = refs[0], refs[1]
        pos = 2
        bias_ref = res_ref = None
        if has_bias:
            bias_ref = refs[pos]
            pos += 1
        if has_res:
            res_ref = refs[pos]
            pos += 1
        o_ref = refs[pos]
        p = _dot(a_ref[...].astype(BF16), b_ref[...].astype(BF16), dims)

        def finish(acc):
            if has_bias:
                acc = acc + bias_ref[...]
            if act == "sigmoid":
                acc = jax.nn.sigmoid(acc)
            if has_res:
                acc = res_ref[...] + acc
            o_ref[...] = acc.astype(o_ref.dtype)

        if nk == 1:
            finish(p)
        else:
            acc_ref = refs[pos + 1]
            k = pl.program_id(2)

            @pl.when(k == 0)
            def _():
                acc_ref[...] = p

            @pl.when(k != 0)
            def _():
                acc_ref[...] += p

            @pl.when(k == nk - 1)
            def _():
                finish(acc_ref[...])

    in_specs = [a_spec, b_spec]
    args = [a, b]
    if has_bias:
        in_specs.append(pl.BlockSpec((1, tn), lambda i, j, k: (0, j)))
        args.append(bias)
    if has_res:
        in_specs.append(pl.BlockSpec((tm, tn), lambda i, j, k: (i, j)))
        args.append(res)
    return pl.pallas_call(
        body,
        name=name,
        grid=(M // tm, N // tn, nk),
        in_specs=in_specs,
        out_specs=pl.BlockSpec((tm, tn), lambda i, j, k: (i, j)),
        out_shape=jax.ShapeDtypeStruct((M, N), out_dtype),
        scratch_shapes=[pltpu.VMEM((tm, tn), F32)] if nk > 1 else [],
        compiler_params=pltpu.CompilerParams(dimension_semantics=("parallel", "parallel", "arbitrary")),
    )(*args)


def _stacked_mm(pieces, w_t, res, after, name, tm=512, tn=1024):
    P, M, W = pieces.shape
    N = w_t.shape[1]

    def body(a_ref, b_ref, r_ref, _, o_ref):
        acc = r_ref[...]
        for p in range(P):
            acc = acc + _dot(a_ref[p].astype(BF16), b_ref[p * W : (p + 1) * W, :])
        o_ref[...] = acc

    tile = pl.BlockSpec((tm, tn), lambda i, j: (i, j))
    return pl.pallas_call(
        body,
        name=name,
        grid=(M // tm, N // tn),
        in_specs=[
            pl.BlockSpec((P, tm, W), lambda i, j: (0, i, 0)),
            pl.BlockSpec((P * W, tn), lambda i, j: (0, j)),
            tile,
            pl.BlockSpec((8, 128), lambda i, j: (0, 0)),
        ],
        out_specs=tile,
        out_shape=jax.ShapeDtypeStruct((M, N), F32),
        compiler_params=pltpu.CompilerParams(dimension_semantics=("parallel", "parallel")),
    )(pieces, w_t, res, after)


def _rstd(x):
    return lax.rsqrt(jnp.mean(x * x, axis=-1, keepdims=True) + NORM_EPS)


def _rms_bwd_math(dy, x, g):
    r = _rstd(x)
    xh = x * r
    dg = jnp.sum(dy * xh, axis=0, keepdims=True)
    dxh = dy * g
    dx = r * (dxh - xh * jnp.mean(dxh * xh, axis=-1, keepdims=True))
    return dx, dg


def _rms_fwd(x, g, name, tr=256):
    S, D = x.shape

    def body(x_ref, g_ref, o_ref):
        xv = x_ref[...]
        o_ref[...] = ((xv * _rstd(xv)) * g_ref[...]).astype(o_ref.dtype)

    return pl.pallas_call(
        body,
        name=name,
        grid=(S // tr,),
        in_specs=[pl.BlockSpec((tr, D), lambda i: (i, 0)), pl.BlockSpec((1, D), lambda i: (0, 0))],
        out_specs=pl.BlockSpec((tr, D), lambda i: (i, 0)),
        out_shape=jax.ShapeDtypeStruct((S, D), BF16),
        compiler_params=pltpu.CompilerParams(dimension_semantics=("parallel",)),
    )(x, g)


def _rms_bwd(dy, x, g, res, name, tr=256):
    S, D = x.shape

    def body(dy_ref, x_ref, g_ref, res_ref, dx_ref, dxb_ref, dg_ref):
        dx, dg = _rms_bwd_math(dy_ref[...], x_ref[...], g_ref[...])
        dx = dx + res_ref[...]
        dx_ref[...] = dx
        dxb_ref[...] = dx.astype(BF16)

        @pl.when(pl.program_id(0) == 0)
        def _():
            dg_ref[...] = dg

        @pl.when(pl.program_id(0) != 0)
        def _():
            dg_ref[...] += dg

    row = pl.BlockSpec((tr, D), lambda i: (i, 0))
    vec = pl.BlockSpec((1, D), lambda i: (0, 0))
    return pl.pallas_call(
        body,
        name=name,
        grid=(S // tr,),
        in_specs=[row, row, vec, row],
        out_specs=[row, row, vec],
        out_shape=[jax.ShapeDtypeStruct((S, D), F32), jax.ShapeDtypeStruct((S, D), BF16), jax.ShapeDtypeStruct((1, D), F32)],
        compiler_params=pltpu.CompilerParams(dimension_semantics=("arbitrary",)),
    )(dy, x, g, res)


def _mm_res_rms(a, b, res, g, name, tm=256):
    M, K = a.shape
    D = b.shape[1]

    def body(a_ref, b_ref, res_ref, g_ref, y_ref, h_ref):
        y = res_ref[...] + _dot(a_ref[...], b_ref[...])
        y_ref[...] = y
        h_ref[...] = ((y * _rstd(y)) * g_ref[...]).astype(BF16)

    row = pl.BlockSpec((tm, D), lambda i: (i, 0))
    return pl.pallas_call(
        body,
        name=name,
        grid=(M // tm,),
        in_specs=[pl.BlockSpec((tm, K), lambda i: (i, 0)), pl.BlockSpec((K, D), lambda i: (0, 0)), row, pl.BlockSpec((1, D), lambda i: (0, 0))],
        out_specs=[row, row],
        out_shape=[jax.ShapeDtypeStruct((M, D), F32), jax.ShapeDtypeStruct((M, D), BF16)],
        compiler_params=pltpu.CompilerParams(dimension_semantics=("parallel",)),
    )(a, b, res, g)


def _mm_rms_bwd(a, b, b_koff, dy_part, x, g, res, name, tm=256):
    M, K = a.shape
    D = x.shape[1]

    def body(a_ref, b_ref, dyp_ref, x_ref, g_ref, res_ref, dx_ref, dg_ref):
        dy = dyp_ref[...] + _dot(a_ref[...], b_ref[...])
        dx, dg = _rms_bwd_math(dy, x_ref[...], g_ref[...])
        dx_ref[...] = dx + res_ref[...]

        @pl.when(pl.program_id(0) == 0)
        def _():
            dg_ref[...] = dg

        @pl.when(pl.program_id(0) != 0)
        def _():
            dg_ref[...] += dg

    row = pl.BlockSpec((tm, D), lambda i: (i, 0))
    vec = pl.BlockSpec((1, D), lambda i: (0, 0))
    return pl.pallas_call(
        body,
        name=name,
        grid=(M // tm,),
        in_specs=[pl.BlockSpec((tm, K), lambda i: (i, 0)), pl.BlockSpec((K, D), lambda i: (b_koff, 0)), row, row, vec, row],
        out_specs=[row, vec],
        out_shape=[jax.ShapeDtypeStruct((M, D), F32), jax.ShapeDtypeStruct((1, D), F32)],
        compiler_params=pltpu.CompilerParams(dimension_semantics=("arbitrary",)),
    )(a, b, dy_part, x, g, res)


def _final_loss(x2, g, tgt, name, tr=256):
    S, D = x2.shape

    def body(x_ref, g_ref, t_ref, loss_ref, dx_ref, dxb_ref, dg_ref):
        xv, gv = x_ref[...], g_ref[...]
        y = (xv * _rstd(xv)) * gv
        e = y - t_ref[...]
        part = 0.5 * jnp.sum(jnp.mean(e * e, axis=-1, keepdims=True), axis=0, keepdims=True)
        dx, dg = _rms_bwd_math(e * (1.0 / D), xv, gv)
        dx_ref[...] = dx
        dxb_ref[...] = dx.astype(BF16)
        part = jnp.broadcast_to(part, (1, 128))

        @pl.when(pl.program_id(0) == 0)
        def _():
            dg_ref[...] = dg
            loss_ref[...] = part

        @pl.when(pl.program_id(0) != 0)
        def _():
            dg_ref[...] += dg
            loss_ref[...] += part

    row = pl.BlockSpec((tr, D), lambda i: (i, 0))
    vec = pl.BlockSpec((1, D), lambda i: (0, 0))
    return pl.pallas_call(
        body,
        name=name,
        grid=(S // tr,),
        in_specs=[row, vec, row],
        out_specs=[pl.BlockSpec((1, 128), lambda i: (0, 0)), row, row, vec],
        out_shape=[
            jax.ShapeDtypeStruct((1, 128), F32),
            jax.ShapeDtypeStruct((S, D), F32),
            jax.ShapeDtypeStruct((S, D), BF16),
            jax.ShapeDtypeStruct((1, D), F32),
        ],
        compiler_params=pltpu.CompilerParams(dimension_semantics=("arbitrary",)),
    )(x2, g, tgt)


def _rope_tables(S):
    pos = jnp.arange(S, dtype=F32)
    inv_freq = ROPE_THETA ** (-jnp.arange(0, QK_ROPE, 2, dtype=F32) / QK_ROPE)
    ang = pos[:, None] * inv_freq[None, :]
    cos, sin = jnp.cos(ang), jnp.sin(ang)
    zero = jnp.zeros((S, 128 - QK_ROPE), F32)
    return jnp.concatenate([cos, cos, zero], axis=1), jnp.concatenate([-sin, sin, zero], axis=1)


def _rope_tile(x, cos_t, sin_t):
    lane = lax.broadcasted_iota(jnp.int32, x.shape, 1)
    partner = jnp.where(lane < QK_ROPE // 2, pltpu.roll(x, 128 - QK_ROPE // 2, 1), pltpu.roll(x, QK_ROPE // 2, 1))
    return x * cos_t + partner * sin_t


def _mla_prep1(lat, gq, gkv, cos_t, sin_t, name, tr=256):
    S = lat.shape[0]

    def body(lat_ref, gq_ref, gkv_ref, cos_ref, sin_ref, cq_ref, ckv_ref, kpe_ref):
        cq = lat_ref[:, :Q_LORA]
        ckv = lat_ref[:, Q_LORA : Q_LORA + KV_LORA]
        cq_ref[...] = ((cq * _rstd(cq)) * gq_ref[...]).astype(BF16)
        ckv_ref[...] = ((ckv * _rstd(ckv)) * gkv_ref[...]).astype(BF16)
        kpe_ref[...] = _rope_tile(lat_ref[:, Q_LORA + KV_LORA :], cos_ref[...], sin_ref[...]).astype(BF16)

    def row(n):
        return pl.BlockSpec((tr, n), lambda i: (i, 0))

    def vec(n):
        return pl.BlockSpec((1, n), lambda i: (0, 0))

    return pl.pallas_call(
        body,
        name=name,
        grid=(S // tr,),
        in_specs=[row(LAT_PAD), vec(Q_LORA), vec(KV_LORA), row(128), row(128)],
        out_specs=[row(Q_LORA), row(KV_LORA), row(128)],
        out_shape=[
            jax.ShapeDtypeStruct((S, Q_LORA), BF16),
            jax.ShapeDtypeStruct((S, KV_LORA), BF16),
            jax.ShapeDtypeStruct((S, 128), BF16),
        ],
        compiler_params=pltpu.CompilerParams(dimension_semantics=("parallel",)),
    )(lat, gq, gkv, cos_t, sin_t)


def _mla_prep2(q_raw, kv, kpe, cos_t, sin_t, name, tr=256):
    S = q_raw.shape[0]
    W = MLA_HEADS * HEAD_PAD

    def body(q_ref, kv_ref, kpe_ref, cos_ref, sin_ref, qa_ref, ka_ref):
        cos_v, sin_v, kpe_v = cos_ref[...], sin_ref[...], kpe_ref[...]
        for h in range(MLA_HEADS):
            lo = h * HEAD_PAD
            qa_ref[:, lo : lo + 128] = q_ref[:, lo : lo + 128].astype(BF16)
            qa_ref[:, lo + 128 : lo + 256] = _rope_tile(q_ref[:, lo + 128 : lo + 256], cos_v, sin_v).astype(BF16)
            ka_ref[:, lo : lo + 128] = kv_ref[:, lo : lo + 128]
            ka_ref[:, lo + 128 : lo + 256] = kpe_v

    def row(n):
        return pl.BlockSpec((tr, n), lambda i: (i, 0))

    return pl.pallas_call(
        body,
        name=name,
        grid=(S // tr,),
        in_specs=[row(W), row(W), row(128), row(128), row(128)],
        out_specs=[row(W), row(W)],
        out_shape=[jax.ShapeDtypeStruct((S, W), BF16), jax.ShapeDtypeStruct((S, W), BF16)],
        compiler_params=pltpu.CompilerParams(dimension_semantics=("parallel",)),
    )(q_raw, kv, kpe, cos_t, sin_t)


def _mla_post(dq_att, dk_att, dv, cos_t, sin_t, name, tr=256):
    S = dq_att.shape[0]
    W = MLA_HEADS * HEAD_PAD

    def body(dq_ref, dk_ref, dv_ref, cos_ref, sin_ref, dqr_ref, dkv_ref, dkpe_ref):
        cos_v, nsin_v = cos_ref[...], -sin_ref[...]
        kpe = jnp.zeros((tr, 128), F32)
        for h in range(MLA_HEADS):
            lo = h * HEAD_PAD
            dqr_ref[:, lo : lo + 128] = dq_ref[:, lo : lo + 128].astype(BF16)
            dqr_ref[:, lo + 128 : lo + 256] = _rope_tile(dq_ref[:, lo + 128 : lo + 256], cos_v, nsin_v).astype(BF16)
            dkv_ref[:, lo : lo + 128] = dk_ref[:, lo : lo + 128].astype(BF16)
            dkv_ref[:, lo + 128 : lo + 256] = dv_ref[:, h * 128 : (h + 1) * 128].astype(BF16)
            kpe = kpe + dk_ref[:, lo + 128 : lo + 256]
        dkpe_ref[...] = _rope_tile(kpe, cos_v, nsin_v)

    def row(n):
        return pl.BlockSpec((tr, n), lambda i: (i, 0))

    return pl.pallas_call(
        body,
        name=name,
        grid=(S // tr,),
        in_specs=[row(W), row(W), row(MLA_HEADS * V_HEAD), row(128), row(128)],
        out_specs=[row(W), row(W), row(128)],
        out_shape=[jax.ShapeDtypeStruct((S, W), BF16), jax.ShapeDtypeStruct((S, W), BF16), jax.ShapeDtypeStruct((S, 128), F32)],
        compiler_params=pltpu.CompilerParams(dimension_semantics=("parallel",)),
    )(dq_att, dk_att, dv, cos_t, sin_t)


def _lat_bwd(dcqn, dckvn, dkpe, lat, gq, gkv, name, tr=256):
    S = lat.shape[0]

    def body(dcq_ref, dckv_ref, dkpe_ref, lat_ref, gq_ref, gkv_ref, dlat_ref, dgq_ref, dgkv_ref):
        dq, dgq = _rms_bwd_math(dcq_ref[...], lat_ref[:, :Q_LORA], gq_ref[...])
        dkv, dgkv = _rms_bwd_math(dckv_ref[...], lat_ref[:, Q_LORA : Q_LORA + KV_LORA], gkv_ref[...])
        dlat_ref[:, :Q_LORA] = dq.astype(BF16)
        dlat_ref[:, Q_LORA : Q_LORA + KV_LORA] = dkv.astype(BF16)
        dlat_ref[:, Q_LORA + KV_LORA :] = dkpe_ref[...].astype(BF16)

        @pl.when(pl.program_id(0) == 0)
        def _():
            dgq_ref[...] = dgq
            dgkv_ref[...] = dgkv

        @pl.when(pl.program_id(0) != 0)
        def _():
            dgq_ref[...] += dgq
            dgkv_ref[...] += dgkv

    def row(n):
        return pl.BlockSpec((tr, n), lambda i: (i, 0))

    def vec(n):
        return pl.BlockSpec((1, n), lambda i: (0, 0))

    return pl.pallas_call(
        body,
        name=name,
        grid=(S // tr,),
        in_specs=[row(Q_LORA), row(KV_LORA), row(128), row(LAT_PAD), vec(Q_LORA), vec(KV_LORA)],
        out_specs=[row(LAT_PAD), vec(Q_LORA), vec(KV_LORA)],
        out_shape=[
            jax.ShapeDtypeStruct((S, LAT_PAD), BF16),
            jax.ShapeDtypeStruct((1, Q_LORA), F32),
            jax.ShapeDtypeStruct((1, KV_LORA), F32),
        ],
        compiler_params=pltpu.CompilerParams(dimension_semantics=("arbitrary",)),
    )(dcqn, dckvn, dkpe, lat, gq, gkv)


MLA_SCALE = (QK_NOPE + QK_ROPE) ** -0.5
LOG2E = 1.4426950408889634
MLA_C2 = MLA_SCALE * LOG2E
FLASH_T = 1024


def _causal_pairs(n, by_key):
    pairs = [(i, j) for j in range(n) for i in range(j, n)] if by_key else [(i, j) for i in range(n) for j in range(i + 1)]
    return jnp.asarray([p[0] for p in pairs], jnp.int32), jnp.asarray([p[1] for p in pairs], jnp.int32)


def _causal_mask(shape, shift, keys_first=False):
    q_axis, k_axis = (1, 0) if keys_first else (0, 1)
    return lax.broadcasted_iota(jnp.int32, shape, k_axis) <= lax.broadcasted_iota(jnp.int32, shape, q_axis) + shift


def _lanes(x, n):
    return jnp.tile(x, (1, n // 128))


def _flash_grid(npairs, in_specs, out_specs, scratch):
    return pltpu.PrefetchScalarGridSpec(
        num_scalar_prefetch=2, grid=(MLA_HEADS, npairs), in_specs=in_specs, out_specs=out_specs, scratch_shapes=scratch
    )


def _flash2_fwd(q_att, k_att, kv, name, t=FLASH_T):
    S = q_att.shape[0]
    half = t // 2
    qi_tab, kj_tab = _causal_pairs(S // t, by_key=False)

    def body(qi_ref, kj_ref, q_ref, k_ref, v_ref, o_ref, lse_ref, m_sc, l_sc, acc_sc):
        step = pl.program_id(1)
        qi, kj = qi_ref[step], kj_ref[step]

        @pl.when(kj == 0)
        def _():
            m_sc[...] = jnp.full((t, 128), NEG, F32)
            l_sc[...] = jnp.zeros((t, 128), F32)
            acc_sc[...] = jnp.zeros((t, V_HEAD), F32)

        def update(rows, s, v):
            m_prev = m_sc[rows, :]
            m_new = jnp.maximum(m_prev, jnp.max(s, axis=1, keepdims=True))
            p = jnp.exp2((s - _lanes(m_new, s.shape[1])) * MLA_C2)
            alpha = jnp.exp2((m_prev - m_new) * MLA_C2)
            l_sc[rows, :] = alpha * l_sc[rows, :] + jnp.sum(p, axis=1, keepdims=True)
            acc_sc[rows, :] = alpha * acc_sc[rows, :] + _dot(p.astype(BF16), v)
            m_sc[rows, :] = m_new

        @pl.when(kj < qi)
        def _():
            update(slice(0, t), _dot(q_ref[...], k_ref[...], NT), v_ref[...])

        @pl.when(kj == qi)
        def _():
            top = _dot(q_ref[:half, :], k_ref[:half, :], NT)
            update(slice(0, half), jnp.where(_causal_mask(top.shape, 0), top, NEG), v_ref[:half, :])
            bot = _dot(q_ref[half:, :], k_ref[...], NT)
            update(slice(half, t), jnp.where(_causal_mask(bot.shape, half), bot, NEG), v_ref[...])
            l = l_sc[...]
            o_ref[...] = acc_sc[...] / l
            lse_ref[0] = m_sc[...] * MLA_SCALE + jnp.log(l)

    return pl.pallas_call(
        body,
        name=name,
        grid_spec=_flash_grid(
            qi_tab.shape[0],
            [
                pl.BlockSpec((t, HEAD_PAD), lambda h, p, qi, kj: (qi[p], h)),
                pl.BlockSpec((t, HEAD_PAD), lambda h, p, qi, kj: (kj[p], h)),
                pl.BlockSpec((t, V_HEAD), lambda h, p, qi, kj: (kj[p], 2 * h + 1)),
            ],
            [
                pl.BlockSpec((t, V_HEAD), lambda h, p, qi, kj: (qi[p], h)),
                pl.BlockSpec((1, t, 128), lambda h, p, qi, kj: (h, qi[p], 0)),
            ],
            [pltpu.VMEM((t, 128), F32), pltpu.VMEM((t, 128), F32), pltpu.VMEM((t, V_HEAD), F32)],
        ),
        out_shape=[jax.ShapeDtypeStruct((S, MLA_HEADS * V_HEAD), F32), jax.ShapeDtypeStruct((MLA_HEADS, S, 128), F32)],
        compiler_params=pltpu.CompilerParams(dimension_semantics=("parallel", "arbitrary")),
    )(qi_tab, kj_tab, q_att, k_att, kv)


def _flash_delta(do, o, name, tr=512):
    S = o.shape[0]

    def body(do_ref, o_ref, d_ref):
        lane = lax.broadcasted_iota(jnp.int32, (tr, 128), 1)
        acc = jnp.zeros((tr, 128), F32)
        for h in range(MLA_HEADS):
            sl = slice(h * V_HEAD, (h + 1) * V_HEAD)
            acc = jnp.where(lane == h, jnp.sum(do_ref[:, sl].astype(F32) * o_ref[:, sl], axis=1, keepdims=True), acc)
        d_ref[...] = acc

    row = pl.BlockSpec((tr, MLA_HEADS * V_HEAD), lambda i: (i, 0))
    return pl.pallas_call(
        body,
        name=name,
        grid=(S // tr,),
        in_specs=[row, row],
        out_specs=pl.BlockSpec((tr, 128), lambda i: (i, 0)),
        out_shape=jax.ShapeDtypeStruct((S, 128), F32),
        compiler_params=pltpu.CompilerParams(dimension_semantics=("parallel",)),
    )(do, o)


def _flash2_bwd(q_att, k_att, kv, do, lse_row, delta_row, name, t=FLASH_T):
    S = q_att.shape[0]
    n = S // t
    qi_tab, kj_tab = _causal_pairs(n, by_key=True)
    last = qi_tab.shape[0] - 1
    half = t // 2

    def body(qi_ref, kj_ref, q_ref, k_ref, v_ref, do_ref, lse_ref, dl_ref, dq_ref, dk_ref, dv_ref, dk_sc, dv_sc):
        step = pl.program_id(1)
        qi, kj = qi_ref[step], kj_ref[step]

        @pl.when(step == 0)
        def _():
            dq_ref[...] = jnp.zeros((S, HEAD_PAD), F32)

        def update(k0, q0, st):
            nk, nq = st.shape
            kr, qr = slice(k0, k0 + nk), slice(q0, q0 + nq)
            q, do_v = q_ref[qr, :], do_ref[qr, :]
            pt = jnp.exp2(st * MLA_C2 - lse_ref[0][:, qr] * LOG2E)
            dv_sc[kr, :] += _dot(pt.astype(BF16), do_v)
            dpt = _dot(v_ref[kr, :], do_v, NT)
            dst = (pt * (dpt - dl_ref[0][:, qr])).astype(BF16)
            dk_sc[kr, :] += _dot(dst, q)
            rows = pl.ds(pl.multiple_of(qi * t + q0, half), nq)
            dq_ref[rows, :] += _dot(dst, k_ref[kr, :], TN)

        @pl.when(qi == kj)
        def _():
            dk_sc[...] = jnp.zeros((t, HEAD_PAD), F32)
            dv_sc[...] = jnp.zeros((t, V_HEAD), F32)
            top = _dot(k_ref[:half, :], q_ref[...], NT)
            update(0, 0, jnp.where(_causal_mask(top.shape, 0, keys_first=True), top, NEG))
            bot = _dot(k_ref[half:, :], q_ref[half:, :], NT)
            update(half, half, jnp.where(_causal_mask(bot.shape, 0, keys_first=True), bot, NEG))

        @pl.when(qi > kj)
        def _():
            update(0, 0, _dot(k_ref[...], q_ref[...], NT))

        @pl.when(qi == n - 1)
        def _():
            dk_ref[...] = dk_sc[...] * MLA_SCALE
            dv_ref[...] = dv_sc[...]

        @pl.when(step == last)
        def _():
            dq_ref[...] = dq_ref[...] * MLA_SCALE

    qrow = lambda h, p, qi, kj: (qi[p], h)
    krow = lambda h, p, qi, kj: (kj[p], h)
    stat = pl.BlockSpec((1, 1, t), lambda h, p, qi, kj: (h, 0, qi[p]))
    return pl.pallas_call(
        body,
        name=name,
        grid_spec=_flash_grid(
            qi_tab.shape[0],
            [
                pl.BlockSpec((t, HEAD_PAD), qrow),
                pl.BlockSpec((t, HEAD_PAD), krow),
                pl.BlockSpec((t, V_HEAD), lambda h, p, qi, kj: (kj[p], 2 * h + 1)),
                pl.BlockSpec((t, V_HEAD), qrow),
                stat,
                stat,
            ],
            [
                pl.BlockSpec((S, HEAD_PAD), lambda h, p, qi, kj: (0, h)),
                pl.BlockSpec((t, HEAD_PAD), krow),
                pl.BlockSpec((t, V_HEAD), krow),
            ],
            [pltpu.VMEM((t, HEAD_PAD), F32), pltpu.VMEM((t, V_HEAD), F32)],
        ),
        out_shape=[
            jax.ShapeDtypeStruct((S, MLA_HEADS * HEAD_PAD), F32),
            jax.ShapeDtypeStruct((S, MLA_HEADS * HEAD_PAD), F32),
            jax.ShapeDtypeStruct((S, MLA_HEADS * V_HEAD), F32),
        ],
        compiler_params=pltpu.CompilerParams(dimension_semantics=("parallel", "arbitrary")),
    )(qi_tab, kj_tab, q_att, k_att, kv, do, lse_row, delta_row)


DIL_SCALE = DIL_HD**-0.5


def _dil_bias():
    slopes = 2.0 ** (-ALIBI_MAX_BIAS * np.arange(1, DIL_HEADS + 1, dtype=np.float64) / DIL_HEADS)
    slopes = slopes.astype(np.float32).reshape(DIL_GROUPS, DIL_HG)
    p = np.arange(DIL_BLK)[:, None]
    kidx = np.arange(2 * DIL_BLK)[None, :]
    j = p + DIL_BLK - kidx
    out = np.zeros((DIL_GROUPS, DIL_HG, DIL_BLK, 2 * DIL_BLK), np.float32)
    for g, (window, dil) in enumerate(DIL_PATTERNS):
        valid = (j >= 0) & (j <= window // dil)
        for h in range(DIL_HG):
            alibi = -slopes[g, h] * (dil * j).astype(np.float32)
            out[g, h] = np.where(valid, alibi, np.float32(NEG))
    return jnp.asarray(out)


DIL_UNROLL = 4


def _unrolled_loop(lo, hi, fn, unroll=DIL_UNROLL):
    groups = (hi - lo) // unroll
    done = lo
    if groups > 1:

        def step(i, carry):
            for u in range(unroll):
                fn(lo + i * unroll + u)
            return carry

        lax.fori_loop(0, groups, step, 0)
        done = lo + groups * unroll
    for n in range(done, hi):
        fn(n)


def _dil_rows(r, n, count, dil):
    if dil == 1:
        if isinstance(n, int):
            return slice(n * DIL_BLK, (n + count) * DIL_BLK)
        return pl.ds(pl.multiple_of(n * DIL_BLK, DIL_BLK), count * DIL_BLK)
    return pl.ds(n * DIL_BLK * dil + r, count * DIL_BLK, stride=dil)


def _dil_each_block(S, dil, block):
    nb = S // dil // DIL_BLK
    if dil == 1:
        block(0, 0, True)
        _unrolled_loop(1, nb, lambda n: block(0, n, False))
    else:
        for r in range(dil):
            for n in range(nb):
                block(r, n, n == 0)


def _dil_col(g, part, h):
    return (g * 3 + part) * DIL_HG + h


def _dil_fwd_group(dqkv, bias_g, g, dil, name):
    S = dqkv.shape[0]

    def body(bias_ref, q_ref, k_ref, v_ref, o_ref, lse_ref):
        def block(r, n, first):
            cur = _dil_rows(r, n, 1, dil)
            both = cur if first else _dil_rows(r, n - 1, 2, dil)
            b = bias_ref[0][:, DIL_BLK:] if first else bias_ref[0]
            q, kk, vv = q_ref[cur, :].astype(BF16), k_ref[both, :].astype(BF16), v_ref[both, :].astype(BF16)
            s = _dot(q, kk, NT) * DIL_SCALE + b
            m = jnp.max(s, axis=1, keepdims=True)
            e = jnp.exp(s - m)
            l = jnp.sum(e, axis=1, keepdims=True)
            p = e * (1.0 / l)
            o_ref[cur, :] = _dot(p.astype(BF16), vv)
            lse_ref[cur, :] = jnp.broadcast_to(m + jnp.log(l), (DIL_BLK, 128))

        _dil_each_block(S, dil, block)

    def col(part):
        return pl.BlockSpec((S, DIL_HD), lambda h: (0, _dil_col(g, part, h)))

    out = pl.BlockSpec((S, DIL_HD), lambda h: (0, h))
    return pl.pallas_call(
        body,
        name=name,
        grid=(DIL_HG,),
        in_specs=[pl.BlockSpec((1, DIL_BLK, 2 * DIL_BLK), lambda h: (h, 0, 0)), col(0), col(1), col(2)],
        out_specs=[out, out],
        out_shape=[jax.ShapeDtypeStruct((S, DIL_OUT), F32), jax.ShapeDtypeStruct((S, DIL_OUT), F32)],
        compiler_params=pltpu.CompilerParams(dimension_semantics=("parallel",)),
    )(bias_g, dqkv, dqkv, dqkv)


def _dil_combine(os_, ls_, name, tr=512):
    S = os_[0].shape[0]

    def body(o0, o1, o2, l0, l1, l2, out_ref, lse_ref):
        a, b, c = l0[...], l1[...], l2[...]
        m = jnp.maximum(jnp.maximum(a, b), c)
        ea, eb, ec = jnp.exp(a - m), jnp.exp(b - m), jnp.exp(c - m)
        den = ea + eb + ec
        inv = 1.0 / den
        out_ref[...] = (ea * inv) * o0[...] + (eb * inv) * o1[...] + (ec * inv) * o2[...]
        lse_ref[...] = m + jnp.log(den)

    row = pl.BlockSpec((tr, DIL_OUT), lambda i: (i, 0))
    return pl.pallas_call(
        body,
        name=name,
        grid=(S // tr,),
        in_specs=[row] * 6,
        out_specs=[row, row],
        out_shape=[jax.ShapeDtypeStruct((S, DIL_OUT), F32)] * 2,
        compiler_params=pltpu.CompilerParams(dimension_semantics=("parallel",)),
    )(*os_, *ls_)


def _dil_rowdot(dod, od, name, tr=512):
    S = dod.shape[0]

    def body(d_ref, o_ref, dd_ref):
        for h in range(DIL_HG):
            sl = slice(h * 128, (h + 1) * 128)
            sm = jnp.sum(d_ref[:, sl] * o_ref[:, sl], axis=1, keepdims=True)
            dd_ref[:, sl] = jnp.broadcast_to(sm, (tr, 128))

    row = pl.BlockSpec((tr, DIL_OUT), lambda i: (i, 0))
    return pl.pallas_call(
        body,
        name=name,
        grid=(S // tr,),
        in_specs=[row, row],
        out_specs=row,
        out_shape=jax.ShapeDtypeStruct((S, DIL_OUT), F32),
        compiler_params=pltpu.CompilerParams(dimension_semantics=("parallel",)),
    )(dod, od)


def _dil_bwd_group(dqkv, bias_g, dod, dd, lse, grads, g, dil, name):
    S = dqkv.shape[0]

    def body(bias_ref, q_ref, k_ref, v_ref, do_ref, dd_ref, lse_ref, _, out_ref):
        out_ref[1] = jnp.zeros((S, DIL_HD), F32)
        out_ref[2] = jnp.zeros((S, DIL_HD), F32)

        def block(r, n, first):
            cur = _dil_rows(r, n, 1, dil)
            both = cur if first else _dil_rows(r, n - 1, 2, dil)
            b = bias_ref[0][:, DIL_BLK:] if first else bias_ref[0]
            q, kk, vv = q_ref[cur, :].astype(BF16), k_ref[both, :].astype(BF16), v_ref[both, :].astype(BF16)
            do = do_ref[cur, :].astype(BF16)
            s = _dot(q, kk, NT) * DIL_SCALE + b
            p = jnp.exp(s - lse_ref[cur, 0:1])
            dp = _dot(do, vv, NT)
            ds = ((p * (dp - dd_ref[cur, 0:1])) * DIL_SCALE).astype(BF16)
            out_ref[0, cur, :] = _dot(ds, kk)
            out_ref[1, both, :] += _dot(ds, q, TN)
            out_ref[2, both, :] += _dot(p.astype(BF16), do, TN)

        _dil_each_block(S, dil, block)

    def col(part):
        return pl.BlockSpec((S, DIL_HD), lambda h: (0, _dil_col(g, part, h)))

    nat = pl.BlockSpec((S, DIL_HD), lambda h: (0, h))
    return pl.pallas_call(
        body,
        name=name,
        grid=(DIL_HG,),
        in_specs=[pl.BlockSpec((1, DIL_BLK, 2 * DIL_BLK), lambda h: (h, 0, 0)), col(0), col(1), col(2), nat, nat, nat, ANY],
        out_specs=pl.BlockSpec((3, S, DIL_HD), lambda h: (g, 0, h)),
        out_shape=jax.ShapeDtypeStruct(grads.shape, F32),
        input_output_aliases={7: 0},
        compiler_params=pltpu.CompilerParams(dimension_semantics=("parallel",)),
    )(bias_g, dqkv, dqkv, dqkv, dod, dd, lse, grads)


def _merge_fwd(gates, o_a, o_b, name, tr=256):
    S = o_a.shape[0]

    def body(ga_ref, gb_ref, oa_ref, ob_ref, m_ref):
        m_ref[...] = (ga_ref[...] * oa_ref[...] + gb_ref[...] * ob_ref[...]).astype(BF16)

    row = pl.BlockSpec((tr, D_MODEL), lambda i: (i, 0))
    return pl.pallas_call(
        body,
        name=name,
        grid=(S // tr,),
        in_specs=[row, pl.BlockSpec((tr, D_MODEL), lambda i: (i, 1)), row, row],
        out_specs=row,
        out_shape=jax.ShapeDtypeStruct((S, D_MODEL), BF16),
        compiler_params=pltpu.CompilerParams(dimension_semantics=("parallel",)),
    )(gates, gates, o_a, o_b)


def _merge_bwd(dmrg, gates, o_a, o_b, name, tr=256):
    S = o_a.shape[0]

    def body(dm_ref, ga_ref, gb_ref, oa_ref, ob_ref, doa_ref, dob_ref, dga_ref, dgb_ref, dba_ref, dbb_ref):
        dm, ga, gb = dm_ref[...], ga_ref[...], gb_ref[...]
        doa_ref[...] = (dm * ga).astype(BF16)
        dob_ref[...] = (dm * gb).astype(BF16)
        dga = (dm * oa_ref[...]) * (ga * (1.0 - ga))
        dgb = (dm * ob_ref[...]) * (gb * (1.0 - gb))
        dga_ref[...] = dga.astype(BF16)
        dgb_ref[...] = dgb.astype(BF16)
        sa = jnp.sum(dga, axis=0, keepdims=True)
        sb = jnp.sum(dgb, axis=0, keepdims=True)

        @pl.when(pl.program_id(0) == 0)
        def _():
            dba_ref[...] = sa
            dbb_ref[...] = sb

        @pl.when(pl.program_id(0) != 0)
        def _():
            dba_ref[...] += sa
            dbb_ref[...] += sb

    row = pl.BlockSpec((tr, D_MODEL), lambda i: (i, 0))
    row1 = pl.BlockSpec((tr, D_MODEL), lambda i: (i, 1))
    vec = pl.BlockSpec((1, D_MODEL), lambda i: (0, 0))
    outs = pl.pallas_call(
        body,
        name=name,
        grid=(S // tr,),
        in_specs=[row, row, row1, row, row],
        out_specs=[row, row, row, row, vec, vec],
        out_shape=[jax.ShapeDtypeStruct((S, D_MODEL), BF16)] * 4 + [jax.ShapeDtypeStruct((1, D_MODEL), F32)] * 2,
        compiler_params=pltpu.CompilerParams(dimension_semantics=("arbitrary",)),
    )(dmrg, gates, gates, o_a, o_b)
    return outs


CONV_TR = 512
CONV_TC = 512
N_FFC = D_FF_PAD // CONV_TC


def _conv_taps(x, before, w_ref, b_ref):
    x0 = jnp.concatenate([before, x], axis=0)
    x1 = pltpu.roll(x0, 1, 0)
    x2 = pltpu.roll(x0, 2, 0)
    u = ((b_ref[...] + w_ref[0:1, :] * x2) + w_ref[1:2, :] * x1) + w_ref[2:3, :] * x0
    return u, x0, x1, x2


def _prev_halo(tr):
    return lambda i, j: (jnp.maximum(i * (tr // 8) - 1, 0), j)


def _ffn_fwd(u0, cw, cb, name):
    S = u0.shape[0]
    tr, tc = CONV_TR, CONV_TC

    def body(up_ref, gt_ref, hup_ref, hgt_ref, wu_ref, wg_ref, bu_ref, bg_ref, a_ref):
        live = (pl.program_id(0) > 0).astype(F32)
        up = _conv_taps(up_ref[...], hup_ref[...] * live, wu_ref, bu_ref)[0][8:]
        gt = _conv_taps(gt_ref[...], hgt_ref[...] * live, wg_ref, bg_ref)[0][8:]
        a_ref[...] = ((gt * jax.nn.sigmoid(gt)) * up).astype(BF16)

    return pl.pallas_call(
        body,
        name=name,
        grid=(S // tr, N_FFC),
        in_specs=[
            pl.BlockSpec((tr, tc), lambda i, j: (i, j)),
            pl.BlockSpec((tr, tc), lambda i, j: (i, j + N_FFC)),
            pl.BlockSpec((8, tc), _prev_halo(tr)),
            pl.BlockSpec((8, tc), lambda i, j: (jnp.maximum(i * (tr // 8) - 1, 0), j + N_FFC)),
            pl.BlockSpec((8, tc), lambda i, j: (0, j)),
            pl.BlockSpec((8, tc), lambda i, j: (0, j + N_FFC)),
            pl.BlockSpec((1, tc), lambda i, j: (0, j)),
            pl.BlockSpec((1, tc), lambda i, j: (0, j + N_FFC)),
        ],
        out_specs=pl.BlockSpec((tr, tc), lambda i, j: (i, j)),
        out_shape=jax.ShapeDtypeStruct((S, D_FF_PAD), BF16),
        compiler_params=pltpu.CompilerParams(dimension_semantics=("parallel", "parallel")),
    )(u0, u0, u0, u0, cw, cw, cb, cb)


def _ffn_bwd(u0, da, cw, cb, name):
    S = u0.shape[0]
    tr, tc = CONV_TR, CONV_TC
    nrow, te = S // tr, tr + 8

    def body(up_ref, gt_ref, hup_ref, hgt_ref, nup_ref, ngt_ref, da_ref, nda_ref, wu_ref, wg_ref, bu_ref, bg_ref, du0_ref, dcw_ref, dcb_ref):
        i = pl.program_id(1)
        prev_live = (i > 0).astype(F32)
        next_live = (i < nrow - 1).astype(F32)

        def conv(x_ref, nx_ref, h_ref, w_ref, b_ref):
            x = jnp.concatenate([x_ref[...], nx_ref[...] * next_live], axis=0)
            return [t[8:] for t in _conv_taps(x, h_ref[...] * prev_live, w_ref, b_ref)]

        up, xu0, xu1, xu2 = conv(up_ref, nup_ref, hup_ref, wu_ref, bu_ref)
        gt, xg0, xg1, xg2 = conv(gt_ref, ngt_ref, hgt_ref, wg_ref, bg_ref)
        da_v = jnp.concatenate([da_ref[...], nda_ref[...] * next_live], axis=0)
        sg = jax.nn.sigmoid(gt)
        d_up = da_v * (gt * sg)
        d_gt = (da_v * up) * (sg * (1.0 + gt * (1.0 - sg)))
        tap = lax.broadcasted_iota(jnp.int32, (8, tc), 0)

        def finish(half, du, x0, x1, x2, w_ref):
            n1 = pltpu.roll(du, te - 1, 0)
            n2 = pltpu.roll(du, te - 2, 0)
            du0 = (w_ref[2:3, :] * du + w_ref[1:2, :] * n1) + w_ref[0:1, :] * n2
            du0_ref[half] = du0[:tr].astype(BF16)
            d = du[:tr]
            dcw = jnp.where(
                tap == 0,
                jnp.sum(d * x2[:tr], axis=0, keepdims=True),
                jnp.where(tap == 1, jnp.sum(d * x1[:tr], axis=0, keepdims=True), jnp.where(tap == 2, jnp.sum(d * x0[:tr], axis=0, keepdims=True), 0.0)),
            )
            dcb = jnp.sum(d, axis=0, keepdims=True)

            @pl.when(i == 0)
            def _():
                dcw_ref[half] = dcw
                dcb_ref[half] = dcb

            @pl.when(i != 0)
            def _():
                dcw_ref[half] += dcw
                dcb_ref[half] += dcb

        finish(0, d_up, xu0, xu1, xu2, wu_ref)
        finish(1, d_gt, xg0, xg1, xg2, wg_ref)

    def prev8(off):
        return pl.BlockSpec((8, tc), lambda j, i: (jnp.maximum(i * (tr // 8) - 1, 0), j + off))

    def next8(off):
        return pl.BlockSpec((8, tc), lambda j, i: (jnp.minimum((i + 1) * (tr // 8), S // 8 - 1), j + off))

    return pl.pallas_call(
        body,
        name=name,
        grid=(N_FFC, nrow),
        in_specs=[
            pl.BlockSpec((tr, tc), lambda j, i: (i, j)),
            pl.BlockSpec((tr, tc), lambda j, i: (i, j + N_FFC)),
            prev8(0),
            prev8(N_FFC),
            next8(0),
            next8(N_FFC),
            pl.BlockSpec((tr, tc), lambda j, i: (i, j)),
            next8(0),
            pl.BlockSpec((8, tc), lambda j, i: (0, j)),
            pl.BlockSpec((8, tc), lambda j, i: (0, j + N_FFC)),
            pl.BlockSpec((1, tc), lambda j, i: (0, j)),
            pl.BlockSpec((1, tc), lambda j, i: (0, j + N_FFC)),
        ],
        out_specs=[
            pl.BlockSpec((2, tr, tc), lambda j, i: (0, i, j)),
            pl.BlockSpec((2, 8, tc), lambda j, i: (0, 0, j)),
            pl.BlockSpec((2, 1, tc), lambda j, i: (0, 0, j)),
        ],
        out_shape=[
            jax.ShapeDtypeStruct((2, S, D_FF_PAD), BF16),
            jax.ShapeDtypeStruct((2, 8, D_FF_PAD), F32),
            jax.ShapeDtypeStruct((2, 1, D_FF_PAD), F32),
        ],
        compiler_params=pltpu.CompilerParams(dimension_semantics=("parallel", "arbitrary")),
    )(u0, u0, u0, u0, u0, u0, da, da, cw, cw, cb, cb)


ADAMW_BLOCK_BYTES = 3 << 20


def _adamw(w, g, m, v, name):
    R, C = w.shape
    fits = [t for t in range(8, R + 1, 8) if R % t == 0 and t * C * 4 <= ADAMW_BLOCK_BYTES]
    tr = max(fits) if fits else R

    def body(w_ref, g_ref, m_ref, v_ref, d_ref, nm_ref, nv_ref):
        gv = g_ref[...]
        nm = ADAM_B1 * m_ref[...] + (1.0 - ADAM_B1) * gv
        nv = ADAM_B2 * v_ref[...] + (1.0 - ADAM_B2) * (gv * gv)
        m_hat = nm / (1.0 - ADAM_B1**ADAM_STEP)
        v_hat = nv / (1.0 - ADAM_B2**ADAM_STEP)
        d_ref[...] = -ADAM_LR * (m_hat / (jnp.sqrt(v_hat) + ADAM_EPS) + ADAM_WD * w_ref[...])
        nm_ref[...] = nm
        nv_ref[...] = nv

    blk = pl.BlockSpec((tr, C), lambda i: (i, 0))
    return pl.pallas_call(
        body,
        name=name,
        grid=(R // tr,),
        in_specs=[blk] * 4,
        out_specs=[blk] * 3,
        out_shape=[jax.ShapeDtypeStruct((R, C), F32)] * 3,
        compiler_params=pltpu.CompilerParams(dimension_semantics=("parallel",)),
    )(w, g, m, v)


ANY = pl.BlockSpec(memory_space=pl.ANY)


def _row_tile(rows):
    return max(t for t in range(16, 353, 16) if rows % t == 0)


def _pair_add(g, recv, core, name):
    _, R, C = g.shape
    tr = _row_tile(R)

    def body(core_ref, g_ref, r_ref, o_ref):
        o_ref[...] = (g_ref[...].astype(F32) + r_ref[...].astype(F32)).astype(o_ref.dtype)

    return pl.pallas_call(
        body,
        name=name,
        grid_spec=pltpu.PrefetchScalarGridSpec(
            num_scalar_prefetch=1,
            grid=(N_CHIP, R // tr),
            in_specs=[
                pl.BlockSpec((1, tr, C), lambda k, i, core_ref: (2 * k + core_ref[0], i, 0)),
                pl.BlockSpec((1, tr, C), lambda k, i, core_ref: (k, i, 0)),
            ],
            out_specs=pl.BlockSpec((1, tr, C), lambda k, i, core_ref: (k, i, 0)),
        ),
        out_shape=jax.ShapeDtypeStruct((N_CHIP, R, C), g.dtype),
        compiler_params=pltpu.CompilerParams(dimension_semantics=("parallel", "parallel")),
    )(core, g, recv)


HBM = pl.BlockSpec(memory_space=pltpu.HBM)
SEM = pl.BlockSpec(memory_space=pltpu.SEMAPHORE)
EFFECT = pltpu.SideEffectType.DATAFLOW_SIDE_EFFECTING
RELATIONS = tuple((dx, dy, dc) for dx in (0, 1) for dy in (0, 1) for dc in (0, 1))[1:]


def _related(rel):
    x, y, c = lax.axis_index("x"), lax.axis_index("y"), lax.axis_index("c")
    return (1 - x if rel[0] else x, 1 - y if rel[1] else y, 1 - c if rel[2] else c)


def _dev_index(pos):
    return 4 * pos[0] + 2 * pos[1] + pos[2]


def _peers(chips):
    if chips:
        return [r for r in RELATIONS if not r[2]], N_CHIP, lambda pos: 2 * pos[0] + pos[1]
    return list(RELATIONS), N_DEV, _dev_index


def _exchange_start(srcs, by_slot, after, name, chips=False):
    n = len(srcs)
    extra = [] if after is None else [after]
    rels, slots, slot_of = _peers(chips)
    lands = [lax.empty((slots,) + (s.shape[1:] if by_slot else s.shape), s.dtype) for s in srcs]
    nsem = len(rels) * n

    def body(*refs):
        src_refs, land_refs = refs[:n], refs[n : 2 * n]
        send_sems, recv_sems = refs[2 * n + len(extra)], refs[2 * n + len(extra) + 1]
        token = refs[-1]
        me = slot_of(_related((0, 0, 0)))
        for a in range(n):
            for k, rel in enumerate(rels):
                peer = _related(rel)
                pltpu.make_async_remote_copy(
                    src_ref=src_refs[a].at[slot_of(peer)] if by_slot else src_refs[a],
                    dst_ref=land_refs[a].at[me],
                    send_sem=send_sems.at[len(rels) * a + k],
                    recv_sem=recv_sems.at[len(rels) * a + k],
                    device_id=peer,
                    device_id_type=MESH,
                ).start()
        token[...] = jnp.zeros_like(token)

    def hbm(a):
        return pltpu.HBM(a.shape, a.dtype)

    outs = pl.pallas_call(
        body,
        name=name,
        out_shape=(
            pltpu.SemaphoreType.DMA((nsem,)),
            pltpu.SemaphoreType.DMA((nsem,)),
            *[hbm(s) for s in srcs],
            *[hbm(l) for l in lands],
            jax.ShapeDtypeStruct((8, 128), F32),
        ),
        in_specs=[HBM] * (2 * n) + [ANY] * len(extra),
        out_specs=(SEM, SEM, *[HBM] * (2 * n), pl.BlockSpec(memory_space=pltpu.VMEM)),
        input_output_aliases={i: 2 + i for i in range(2 * n)},
        compiler_params=pltpu.CompilerParams(has_side_effects=EFFECT),
    )(*[pltpu.with_memory_space_constraint(a, pltpu.HBM) for a in list(srcs) + lands], *extra)
    return (outs[0], outs[1], list(outs[2 : 2 + n]), list(outs[2 + n : 2 + 2 * n])), outs[-1]


def _exchange_wait(handle, by_slot, after, name, chips=False):
    send_sems, recv_sems, srcs, lands = handle
    n = len(srcs)
    rels = _peers(chips)[0]

    def body(*refs):
        src_refs, land_refs = refs[:n], refs[n : 2 * n]
        s_sems, r_sems = refs[2 * n], refs[2 * n + 1]
        for a in range(n):
            for k, rel in enumerate(rels):
                copy = pltpu.make_async_remote_copy(
                    src_ref=src_refs[a].at[0] if by_slot else src_refs[a],
                    dst_ref=land_refs[a].at[0],
                    send_sem=s_sems.at[len(rels) * a + k],
                    recv_sem=r_sems.at[len(rels) * a + k],
                    device_id=_related(rel),
                    device_id_type=MESH,
                )
                copy.wait_send()
                copy.wait_recv()

    outs = pl.pallas_call(
        body,
        name=name,
        out_shape=tuple(pltpu.HBM(a.shape, a.dtype) for a in srcs + lands),
        in_specs=[HBM] * (2 * n) + [SEM, SEM, ANY],
        out_specs=tuple([HBM] * (2 * n)),
        input_output_aliases={i: i for i in range(2 * n)},
        compiler_params=pltpu.CompilerParams(has_side_effects=EFFECT),
    )(*srcs, *lands, send_sems, recv_sems, after)
    return list(outs[:n]), list(outs[n:])


def _pair_start(g, name):
    land = lax.empty((N_CHIP,) + g.shape[1:], g.dtype)

    def body(g_ref, land_ref, send_sems, recv_sems, g_thru, land_thru, token):
        c = lax.axis_index("c")
        for k in range(N_CHIP):
            pltpu.make_async_remote_copy(
                src_ref=g_ref.at[2 * k + (1 - c)],
                dst_ref=land_ref.at[k],
                send_sem=send_sems.at[k],
                recv_sem=recv_sems.at[k],
                device_id=_related((0, 0, 1)),
                device_id_type=MESH,
            ).start()
        token[...] = jnp.zeros_like(token)

    outs = pl.pallas_call(
        body,
        name=name,
        out_shape=(
            pltpu.SemaphoreType.DMA((N_CHIP,)),
            pltpu.SemaphoreType.DMA((N_CHIP,)),
            pltpu.HBM(g.shape, g.dtype),
            pltpu.HBM(land.shape, land.dtype),
            jax.ShapeDtypeStruct((8, 128), F32),
        ),
        in_specs=[HBM, HBM],
        out_specs=(SEM, SEM, HBM, HBM, pl.BlockSpec(memory_space=pltpu.VMEM)),
        input_output_aliases={0: 2, 1: 3},
        compiler_params=pltpu.CompilerParams(has_side_effects=EFFECT),
    )(pltpu.with_memory_space_constraint(g, pltpu.HBM), pltpu.with_memory_space_constraint(land, pltpu.HBM))
    return outs[:4], outs[4]


def _pair_wait(handle, after, name):
    send_sems, recv_sems, g, land = handle

    def body(g_ref, land_ref, s_sems, r_sems, _, g_out, land_out):
        for k in range(N_CHIP):
            copy = pltpu.make_async_remote_copy(
                src_ref=g_ref.at[0],
                dst_ref=land_ref.at[0],
                send_sem=s_sems.at[k],
                recv_sem=r_sems.at[k],
                device_id=_related((0, 0, 1)),
                device_id_type=MESH,
            )
            copy.wait_send()
            copy.wait_recv()

    return pl.pallas_call(
        body,
        name=name,
        out_shape=(pltpu.HBM(g.shape, g.dtype), pltpu.HBM(land.shape, land.dtype)),
        in_specs=[HBM, HBM, SEM, SEM, ANY],
        out_specs=(HBM, HBM),
        input_output_aliases={0: 0, 1: 1},
        compiler_params=pltpu.CompilerParams(has_side_effects=EFFECT),
    )(g, land, send_sems, recv_sems, after)


NEAR = ((0, 0, 1), (1, 0, 0), (0, 1, 0), (1, 1, 0))


def _gather2_start(blocks, name):
    n = len(blocks)
    lands = [lax.empty((N_DEV,) + b.shape, b.dtype) for b in blocks]

    def body(*refs):
        src_refs, land_refs = refs[:n], refs[n : 2 * n]
        send_sems, recv_sems, token = refs[2 * n], refs[2 * n + 1], refs[-1]
        me = _dev_index(_related((0, 0, 0)))
        for a in range(n):
            for k, rel in enumerate(NEAR):
                pltpu.make_async_remote_copy(
                    src_ref=src_refs[a],
                    dst_ref=land_refs[a].at[me],
                    send_sem=send_sems.at[len(NEAR) * a + k],
                    recv_sem=recv_sems.at[len(NEAR) * a + k],
                    device_id=_related(rel),
                    device_id_type=MESH,
                ).start()
        token[...] = jnp.zeros_like(token)

    nsem = len(NEAR) * n
    outs = pl.pallas_call(
        body,
        name=name,
        out_shape=(
            pltpu.SemaphoreType.DMA((nsem,)),
            pltpu.SemaphoreType.DMA((nsem,)),
            *[pltpu.HBM(a.shape, a.dtype) for a in list(blocks) + lands],
            jax.ShapeDtypeStruct((8, 128), F32),
        ),
        in_specs=[HBM] * (2 * n),
        out_specs=(SEM, SEM, *[HBM] * (2 * n), pl.BlockSpec(memory_space=pltpu.VMEM)),
        input_output_aliases={i: 2 + i for i in range(2 * n)},
        compiler_params=pltpu.CompilerParams(has_side_effects=EFFECT),
    )(*[pltpu.with_memory_space_constraint(a, pltpu.HBM) for a in list(blocks) + lands])
    return (outs[0], outs[1], list(outs[2 : 2 + n]), list(outs[2 + n : 2 + 2 * n])), outs[-1]


def _gather2_forward(handle, after, name):
    send1, recv1, srcs, lands = handle
    n = len(srcs)

    def body(*refs):
        src_refs, land_refs = refs[:n], refs[n : 2 * n]
        s1, r1 = refs[2 * n], refs[2 * n + 1]
        s2, r2 = refs[-2], refs[-1]
        sibling = _related(NEAR[0])
        for a in range(n):
            for k, rel in enumerate(NEAR):
                first = pltpu.make_async_remote_copy(
                    src_ref=src_refs[a],
                    dst_ref=land_refs[a].at[0],
                    send_sem=s1.at[len(NEAR) * a + k],
                    recv_sem=r1.at[len(NEAR) * a + k],
                    device_id=_related(rel),
                    device_id_type=MESH,
                )
                first.wait_send()
                first.wait_recv()
                if k:
                    slot = land_refs[a].at[_dev_index(_related(rel))]
                    pltpu.make_async_remote_copy(
                        src_ref=slot,
                        dst_ref=slot,
                        send_sem=s2.at[3 * a + k - 1],
                        recv_sem=r2.at[3 * a + k - 1],
                        device_id=sibling,
                        device_id_type=MESH,
                    ).start()

    outs = pl.pallas_call(
        body,
        name=name,
        out_shape=(
            *[pltpu.HBM(a.shape, a.dtype) for a in srcs + lands],
            pltpu.SemaphoreType.DMA((3 * n,)),
            pltpu.SemaphoreType.DMA((3 * n,)),
        ),
        in_specs=[HBM] * (2 * n) + [SEM, SEM, ANY],
        out_specs=(*[HBM] * (2 * n), SEM, SEM),
        input_output_aliases={i: i for i in range(2 * n)},
        compiler_params=pltpu.CompilerParams(has_side_effects=EFFECT),
    )(*srcs, *lands, send1, recv1, after)
    return outs[-2], outs[-1], list(outs[:n]), list(outs[n : 2 * n])


def _gather2_wait(handle, name):
    send2, recv2, srcs, lands = handle
    n = len(srcs)

    def body(*refs):
        land_refs = refs[n : 2 * n]
        s2, r2 = refs[2 * n], refs[2 * n + 1]
        for a in range(n):
            for j in range(3):
                passed = pltpu.make_async_remote_copy(
                    src_ref=land_refs[a].at[0],
                    dst_ref=land_refs[a].at[0],
                    send_sem=s2.at[3 * a + j],
                    recv_sem=r2.at[3 * a + j],
                    device_id=_related(NEAR[0]),
                    device_id_type=MESH,
                )
                passed.wait_send()
                passed.wait_recv()

    outs = pl.pallas_call(
        body,
        name=name,
        out_shape=tuple(pltpu.HBM(a.shape, a.dtype) for a in srcs + lands),
        in_specs=[HBM] * (2 * n) + [SEM, SEM],
        out_specs=tuple([HBM] * (2 * n)),
        input_output_aliases={i: i for i in range(2 * n)},
        compiler_params=pltpu.CompilerParams(has_side_effects=EFFECT),
    )(*srcs, *lands, send2, recv2)
    return list(outs[:n]), list(outs[n:])


def _slot_sum(parts, name, keep=None):
    n, R, C = parts.shape
    if keep is not None:
        tc = 256
        rows = sum(size for _, size in keep)

        def kept(p_ref, o_ref):
            acc = p_ref[0].astype(F32)
            for k in range(1, n):
                acc = acc + p_ref[k].astype(F32)
            off = 0
            for start, size in keep:
                o_ref[off : off + size, :] = acc[start : start + size, :]
                off += size

        return pl.pallas_call(
            kept,
            name=name,
            grid=(C // tc,),
            in_specs=[pl.BlockSpec((n, R, tc), lambda j: (0, 0, j))],
            out_specs=pl.BlockSpec((rows, tc), lambda j: (0, j)),
            out_shape=jax.ShapeDtypeStruct((rows, C), F32),
            compiler_params=pltpu.CompilerParams(dimension_semantics=("parallel",)),
        )(parts)
    tr = _row_tile(R) if R % 16 == 0 else R

    def body(p_ref, o_ref):
        acc = p_ref[0].astype(F32)
        for k in range(1, n):
            acc = acc + p_ref[k].astype(F32)
        o_ref[...] = acc

    return pl.pallas_call(
        body,
        name=name,
        grid=(R // tr,),
        in_specs=[pl.BlockSpec((n, tr, C), lambda i: (0, i, 0))],
        out_specs=pl.BlockSpec((tr, C), lambda i: (i, 0)),
        out_shape=jax.ShapeDtypeStruct((R, C), F32),
        compiler_params=pltpu.CompilerParams(dimension_semantics=("parallel",)),
    )(parts)


W_IN_TC = 256
W_IN_BOUNDS = (0, LAT, LAT + 3 * DIL_QKV, LAT + 3 * DIL_QKV + D_MODEL, D_IN)


def _dqkv_chunks():
    return [((g * 3 + part) * DIL_OUT, LAT + part * DIL_QKV + g * DIL_OUT) for g in range(DIL_GROUPS) for part in range(3)]


def _w_in_regroup(slots, after, name):
    tc = W_IN_TC

    def body(s_ref, _, lat_ref, dqkv_ref, g_ref, buf):
        for j in range(N_DEV):
            buf[j * IN_ROWS : (j + 1) * IN_ROWS, :] = s_ref[j].astype(F32)[:IN_ROWS, :]
        lat_ref[:LAT, :] = buf[:LAT, :].astype(BF16)
        lat_ref[LAT:, :] = jnp.zeros((LAT_PAD - LAT, tc), BF16)
        for dst, src in _dqkv_chunks():
            dqkv_ref[dst : dst + DIL_OUT, :] = buf[src : src + DIL_OUT, :].astype(BF16)
        g_ref[...] = buf[W_IN_BOUNDS[2] :, :].astype(BF16)

    def col(rows):
        return pl.BlockSpec((rows, tc), lambda k: (0, k))

    return pl.pallas_call(
        body,
        name=name,
        grid=(D_MODEL // tc,),
        in_specs=[pl.BlockSpec((N_DEV, IN_ROWS_PAD, tc), lambda k: (0, 0, k)), pl.BlockSpec((8, 128), lambda k: (0, 0))],
        out_specs=[col(LAT_PAD), col(3 * DIL_QKV), col(2 * D_MODEL)],
        out_shape=[
            jax.ShapeDtypeStruct((LAT_PAD, D_MODEL), BF16),
            jax.ShapeDtypeStruct((3 * DIL_QKV, D_MODEL), BF16),
            jax.ShapeDtypeStruct((2 * D_MODEL, D_MODEL), BF16),
        ],
        scratch_shapes=[pltpu.VMEM((D_IN, tc), F32)],
        compiler_params=pltpu.CompilerParams(dimension_semantics=("parallel",)),
    )(slots, after)


def _w_in_grad_regroup(g_lat, g_dqkv, g_ga, g_gb, name):
    tc = W_IN_TC

    def body(lat_ref, dqkv_ref, ga_ref, gb_ref, o_ref, buf):
        b = W_IN_BOUNDS
        buf[b[0] : b[1], :] = lat_ref[:LAT, :].astype(F32)
        for dst, src in _dqkv_chunks():
            buf[src : src + DIL_OUT, :] = dqkv_ref[dst : dst + DIL_OUT, :].astype(F32)
        buf[b[2] : b[3], :] = ga_ref[...].astype(F32)
        buf[b[3] : b[4], :] = gb_ref[...].astype(F32)
        fill = jnp.zeros((IN_ROWS_PAD - IN_ROWS, tc), F32)
        for j in range(N_DEV):
            o_ref[j] = jnp.concatenate([buf[j * IN_ROWS : (j + 1) * IN_ROWS, :], fill], axis=0).astype(BF16)

    def col(rows):
        return pl.BlockSpec((rows, tc), lambda k: (0, k))

    return pl.pallas_call(
        body,
        name=name,
        grid=(D_MODEL // tc,),
        in_specs=[col(LAT_PAD), col(3 * DIL_QKV), col(D_MODEL), col(D_MODEL)],
        out_specs=pl.BlockSpec((N_DEV, IN_ROWS_PAD, tc), lambda k: (0, 0, k)),
        out_shape=jax.ShapeDtypeStruct((N_DEV, IN_ROWS_PAD, D_MODEL), BF16),
        scratch_shapes=[pltpu.VMEM((D_IN, tc), F32)],
        compiler_params=pltpu.CompilerParams(dimension_semantics=("parallel",)),
    )(g_lat, g_dqkv, g_ga, g_gb)


def _ffn_pad(a, axis):
    a = jnp.moveaxis(a, axis, -1)
    g = a.reshape(a.shape[:-1] + (2 * N_DEV, FF_GROUP))
    g = jnp.pad(g, [(0, 0)] * (g.ndim - 1) + [(0, FF_GROUP_PAD - FF_GROUP)])
    return jnp.moveaxis(g.reshape(a.shape[:-1] + (2 * D_FF_PAD,)), -1, axis)


def _ffn_unpad(a, axis):
    a = jnp.moveaxis(a, axis, -1)
    g = a.reshape(a.shape[:-1] + (2 * N_DEV, FF_GROUP_PAD))[..., :FF_GROUP]
    return jnp.moveaxis(g.reshape(a.shape[:-1] + (2 * D_FF,)), -1, axis)


MISC = (("w_o_mla", (256, 1024)), ("w_o_dil", (256, 512)), ("w_uq", (192, 512)), ("w_ukv", (256, 256)))
BIG_WEIGHTS = ("w_in", "w_up", "w_down", "w_out") + tuple(n for n, _ in MISC)


def _exchange_blocks(w):
    def t(a):
        return a.astype(BF16).T

    up = t(w["w_up"]).reshape(2, FF_GROUP, D_MODEL)
    return [
        jnp.pad(t(w["w_in"]), ((0, IN_ROWS_PAD - IN_ROWS), (0, 0))),
        jnp.pad(up, ((0, 0), (0, FF_GROUP_PAD - FF_GROUP), (0, 0))).reshape(2 * FF_GROUP_PAD, D_MODEL),
        jnp.pad(w["w_down"].astype(BF16), ((0, FF_GROUP_PAD - FF_GROUP), (0, 0))),
        w["w_out"].astype(BF16),
        jnp.concatenate([t(w[n]).reshape(-1, D_MODEL) for n, _ in MISC], axis=0),
    ]


def _misc_split(misc):
    out, off = {}, 0
    for n, (r, c) in MISC:
        rows = r * c // D_MODEL
        out[n] = misc[..., off : off + rows, :].reshape(misc.shape[:-2] + (r, c))
        off += rows
    return out


def _small_matrices(g_misc):
    misc = _misc_split(g_misc)
    uq_t = jnp.pad(misc["w_uq"], ((0, 0), (0, HEAD_PAD - QK_NOPE - QK_ROPE), (0, 0)))
    return {
        "uq_t": uq_t.reshape(MLA_HEADS * HEAD_PAD, Q_LORA),
        "ukv_t": misc["w_ukv"].reshape(MLA_HEADS * HEAD_PAD, KV_LORA),
        "o_mla_t": misc["w_o_mla"].reshape(D_MODEL, MLA_HEADS * V_HEAD),
        "o_dil_t": misc["w_o_dil"].reshape(D_MODEL, DIL_OUT),
    }


def _small_grad_blocks(g):
    uq_t = g["uq_t"].reshape(MLA_HEADS, HEAD_PAD, Q_LORA)[:, : QK_NOPE + QK_ROPE]
    misc = {"w_o_mla": g["o_mla_t"], "w_o_dil": g["o_dil_t"], "w_uq": uq_t, "w_ukv": g["ukv_t"]}
    return [
        g["w_out"].reshape(N_DEV, -1, D_MODEL),
        jnp.concatenate([misc[n].reshape(N_DEV, -1, D_MODEL) for n, _ in MISC], axis=1),
    ]


def _grad_shards(sums):
    s_in, s_out, s_misc, s_up, s_down = sums
    out = {
        "w_in": s_in.T,
        "w_up": s_up.T,
        "w_down": s_down,
        "w_out": s_out,
    }
    out.update({n: v.T for n, v in _misc_split(s_misc).items()})
    return out


def _local_step(x, h, tgt, wt, conv_w, small, small_matrices, ffn_weight, send_ffn_grads, send_small_grads, send_w_in_grads, forward_w_in_grads):
    S = x.shape[0]
    lat_t, dqkv_t, g_t = wt
    cw = jnp.pad(_ffn_pad(conv_w, 1), ((0, 5), (0, 0)))
    cb = _ffn_pad(small["conv_b"], 1)
    cos_t, sin_t = _rope_tables(S)
    bias = _dil_bias()
    g1, g2, g3 = small["attn_norm_g"], small["ffn_norm_g"], small["final_norm_g"]
    gq, gkv = small["q_norm_g"], small["kv_norm_g"]

    lat = _mm(h, lat_t, "nt", F32, 1024, LAT_PAD, D_MODEL, "proj_lat")
    dqkv = _mm(h, dqkv_t, "nt", F32, 1024, 1536, D_MODEL, "proj_dqkv")
    gates = _mm(h, g_t, "nt", F32, 1024, 1024, D_MODEL, "proj_gates", bias=small["b_gate"], act="sigmoid")
    sm = small_matrices(gates)
    uq_t, ukv_t, o_mla_t, o_dil_t = sm["uq_t"], sm["ukv_t"], sm["o_mla_t"], sm["o_dil_t"]
    cqn, ckvn, kpe = _mla_prep1(lat, gq, gkv, cos_t, sin_t, "mla_prep1")
    q_raw = _mm(cqn, uq_t, "nt", F32, 1024, 1024, Q_LORA, "mla_uq")
    kv = _mm(ckvn, ukv_t, "nt", BF16, 1024, 1024, KV_LORA, "mla_ukv")
    q_att, k_att = _mla_prep2(q_raw, kv, kpe, cos_t, sin_t, "mla_prep2")
    o, lse = _flash2_fwd(q_att, k_att, kv, "mla_flash_fwd")
    o_a = _mm(o, o_mla_t, "nt", F32, 1024, 1024, MLA_HEADS * V_HEAD, "mla_out")

    d_os, d_ls = [], []
    for g, (_, dil) in enumerate(DIL_PATTERNS):
        og, lg = _dil_fwd_group(dqkv, bias[g], g, dil, f"dil_fwd_{g}")
        d_os.append(og)
        d_ls.append(lg)
    od, dil_lse = _dil_combine(d_os, d_ls, "dil_combine")
    o_b = _mm(od, o_dil_t, "nt", F32, 1024, 1024, DIL_OUT, "dil_out")

    mrg = _merge_fwd(gates, o_a, o_b, "merge_fwd")
    w_out = ffn_weight("w_out", mrg)
    x1, h2 = _mm_res_rms(mrg, w_out, x, g2, "mix_out")
    up_t = ffn_weight("up_t", h2)
    u0 = _mm(h2, up_t, "nt", F32, 2048, 1024, D_MODEL, "ffn_up")
    a = _ffn_fwd(u0, cw, cb, "ffn_conv_fwd")
    w_down = ffn_weight("w_down", a)
    x2 = _mm(a, w_down, "nn", F32, 1024, 1024, D_FF_PAD // 2, "ffn_down", res=x1)
    loss_part, dx2, dx2b, dg3 = _final_loss(x2, g3, tgt, "final_loss")

    da = _mm(dx2b, w_down, "nt", F32, 1024, D_FF_PAD // 4, D_MODEL, "ffn_down_dx")
    gw_down = _mm(a, dx2b, "tn", BF16, 512, D_MODEL, S, "ffn_down_dw")
    du0, dcw, dcb = _ffn_bwd(u0, da, cw, cb, "ffn_conv_bwd")
    du0 = du0.reshape(2 * S, D_FF_PAD)
    gw_up_t = _mm(du0, h2, "tn", BF16, 512, D_MODEL, S, "ffn_up_dw", a_halves=2)
    sent = send_ffn_grads(gw_up_t, gw_down)
    dh2 = _mm(du0, up_t, "nn", F32, 1024, 1024, D_FF_PAD // 2, "ffn_up_dx", a_halves=2)
    dx1, dx1b, dg2 = _rms_bwd(dh2, x1, g2 + sent, dx2, "rms_ffn_bwd")

    dmrg = _mm(dx1b, w_out, "nt", F32, 1024, D_MODEL, D_MODEL, "mix_out_dx")
    gw_out = _mm(mrg, dx1b, "tn", BF16, 512, D_MODEL, S, "mix_out_dw")
    do_a, do_b, dga, dgb, dba, dbb = _merge_bwd(dmrg, gates, o_a, o_b, "merge_bwd")

    do = _mm(do_a, o_mla_t, "nn", BF16, 1024, 1024, D_MODEL, "mla_out_dx")
    gw_o_mla_t = _mm(do_a, o, "tn", BF16, 1024, 1024, 1024, "mla_out_dw")
    dod = _mm(do_b, o_dil_t, "nn", F32, 1024, DIL_OUT, D_MODEL, "dil_out_dx")
    gw_o_dil_t = _mm(do_b, od, "tn", BF16, 1024, DIL_OUT, 1024, "dil_out_dw")

    delta = _flash_delta(do, o, "mla_flash_delta")
    lse_row = lse[:, :, 0][:, None, :]
    delta_row = delta[:, :MLA_HEADS].T[:, None, :]
    dq_att, dk_att, dv = _flash2_bwd(q_att, k_att, kv, do, lse_row, delta_row, "mla_flash_bwd")
    dq_raw, dkv, dkpe = _mla_post(dq_att, dk_att, dv, cos_t, sin_t, "mla_post")
    dcqn = _mm(dq_raw, uq_t, "nn", F32, 1024, Q_LORA, MLA_HEADS * HEAD_PAD, "mla_uq_dx")
    gw_uq_t = _mm(dq_raw, cqn, "tn", BF16, 1024, Q_LORA, 1024, "mla_uq_dw")
    dckvn = _mm(dkv, ukv_t, "nn", F32, 1024, KV_LORA, MLA_HEADS * HEAD_PAD, "mla_ukv_dx")
    gw_ukv_t = _mm(dkv, ckvn, "tn", BF16, 1024, KV_LORA, 1024, "mla_ukv_dw")
    sent = send_small_grads({"uq_t": gw_uq_t, "ukv_t": gw_ukv_t, "o_mla_t": gw_o_mla_t, "o_dil_t": gw_o_dil_t, "w_out": gw_out})
    dlat, dgq, dgkv = _lat_bwd(dcqn, dckvn, dkpe, lat, gq + sent, gkv, "lat_bwd")

    dd = _dil_rowdot(dod, od, "dil_rowdot")
    ddqkv = lax.empty((3 * DIL_GROUPS, S, DIL_OUT), F32)
    for g, (_, dil) in enumerate(DIL_PATTERNS):
        ddqkv = _dil_bwd_group(dqkv, bias[g], dod, dd, dil_lse, ddqkv, g, dil, f"dil_bwd_{g}")
    gw_lat_t = _mm(dlat, h, "tn", BF16, LAT_PAD, 1024, S, "proj_lat_dw")
    gw_dqkv_t = _mm(ddqkv.reshape(3 * DIL_GROUPS * S, DIL_OUT), h, "tn", BF16, 512, D_MODEL, S, "proj_dqkv_dw", a_halves=3 * DIL_GROUPS)
    gw_ga_t = _mm(dga, h, "tn", BF16, 512, D_MODEL, S, "proj_ga_dw")
    gw_gb_t = _mm(dgb, h, "tn", BF16, 512, D_MODEL, S, "proj_gb_dw")
    sent = send_w_in_grads(gw_lat_t, gw_dqkv_t, gw_ga_t, gw_gb_t)
    dh = _mm(dlat + sent.astype(BF16), lat_t, "nn", F32, 1024, 1024, LAT_PAD, "proj_lat_dx")
    dh = _stacked_mm(ddqkv, dqkv_t, dh, forward_w_in_grads(dh), "proj_dqkv_dx")
    dh = _mm(dga, g_t, "nn", F32, 1024, 1024, D_MODEL, "proj_ga_dx", res=dh)
    grad_x, dg1 = _mm_rms_bwd(dgb, g_t, 1, dh, x, g1, dx1, "proj_gb_dx_rms_attn_bwd")

    small_grads = {
        "attn_norm_g": dg1,
        "b_gate": jnp.concatenate([dba, dbb], axis=1),
        "q_norm_g": dgq,
        "kv_norm_g": dgkv,
        "ffn_norm_g": dg2,
        "conv_b": _ffn_unpad(jnp.concatenate([dcb[0], dcb[1]], axis=1), 1),
        "final_norm_g": dg3,
        "conv_w": _ffn_unpad(jnp.concatenate([dcw[0, :3], dcw[1, :3]], axis=1), 1),
    }
    return loss_part, grad_x, small_grads


SMALL_ORDER = ("attn_norm_g", "b_gate", "q_norm_g", "kv_norm_g", "ffn_norm_g", "conv_b", "final_norm_g", "conv_w")
WEIGHT_ORDER = (
    "attn_norm_g", "w_in", "b_gate", "q_norm_g", "w_uq", "kv_norm_g", "w_ukv", "w_o_mla", "w_o_dil", "w_out",
    "ffn_norm_g", "w_up", "conv_w", "conv_b", "w_down", "final_norm_g",
)


def kernel(x, attn_norm_g, w_in, b_gate, q_norm_g, w_uq, kv_norm_g, w_ukv, w_o_mla, w_o_dil, w_out, ffn_norm_g, w_up, conv_w, conv_b, w_down, final_norm_g, loss_target, m_attn_norm_g, m_w_in, m_b_gate, m_q_norm_g, m_w_uq, m_kv_norm_g, m_w_ukv, m_w_o_mla, m_w_o_dil, m_w_out, m_ffn_norm_g, m_w_up, m_conv_w, m_conv_b, m_w_down, m_final_norm_g, v_attn_norm_g, v_w_in, v_b_gate, v_q_norm_g, v_w_uq, v_kv_norm_g, v_w_ukv, v_w_o_mla, v_w_o_dil, v_w_out, v_ffn_norm_g, v_w_up, v_conv_w, v_conv_b, v_w_down, v_final_norm_g):
    env = dict(locals())
    dev = 4 * lax.axis_index("x") + 2 * lax.axis_index("y") + lax.axis_index("c")
    core = lax.axis_index("c").astype(jnp.int32).reshape(1)

    def two_d(a):
        return a.reshape(-1, a.shape[-1])

    w = {n: two_d(env[n]) for n in WEIGHT_ORDER}
    m = {n: two_d(env["m_" + n]) for n in WEIGHT_ORDER}
    v = {n: two_d(env["v_" + n]) for n in WEIGHT_ORDER}

    chip = 2 * lax.axis_index("x") + lax.axis_index("y")

    def own_slot_in(lands, own, slot=dev):
        return [lax.dynamic_update_slice(l, o[None], (slot, 0, 0)) for l, o in zip(lands, own)]

    b_in = _exchange_blocks(w)[0]
    r, c = CONV_SHARD
    conv = jnp.pad(w["conv_w"].reshape(-1), (0, 8 * SMALL_COLS - r * c)).reshape(8, SMALL_COLS)
    first_level, token = _gather2_start([b_in, conv], "ag_w_in_start")
    tied = {n: w[n] + token[0, 0] for n in BIG_WEIGHTS}
    _, b_up, b_down, b_out, b_misc = _exchange_blocks(tied)
    h = _rms_fwd(x[0], w["attn_norm_g"] + token[0, 0], "rms_attn")
    prepared = b_up[:1, :1] + b_down[:1, :1] + b_out[:1, :1] + b_misc[:1, :1] + h[:1, :1]
    own, lands = _gather2_wait(_gather2_forward(first_level, prepared, "ag_w_in_forward"), "ag_w_in_wait")
    g_in, conv = own_slot_in(lands, own)
    misc_gather, started = _exchange_start([b_misc], False, conv, "ag_small_start")
    ffn_gathers, started2 = {}, started
    for key, block in (("w_out", b_out), ("up_t", b_up), ("w_down", b_down)):
        ffn_gathers[key], started2 = _exchange_start([block], False, started2, f"ag_{key}_start")
    wt = _w_in_regroup(g_in, started2, "w_in_regroup")
    conv = conv.reshape(N_DEV, 8 * SMALL_COLS)[:, : r * c].reshape(N_DEV, r, c)
    conv_w_full = conv.transpose(1, 0, 2).reshape(r, N_DEV * c)
    small = {n: w[n] for n in SMALL_ORDER if n != "conv_w"}

    def small_matrices(after):
        own, lands = _exchange_wait(misc_gather, False, after, "ag_small_wait")
        return _small_matrices(own_slot_in(lands, own)[0])

    def ffn_weight(key, after):
        own, lands = _exchange_wait(ffn_gathers[key], False, after, f"ag_{key}_wait")
        return own_slot_in(lands, own)[0].reshape(-1, D_MODEL)

    reduces = {}

    def send_ffn_grads(gw_up_t, gw_down):
        blocks = [gw_up_t.reshape(N_DEV, 2 * FF_GROUP_PAD, D_MODEL), gw_down.reshape(N_DEV, FF_GROUP_PAD, D_MODEL)]
        reduces["ffn"], token = _exchange_start(blocks, True, None, "rs_ffn_start")
        return token[0, 0]

    def send_small_grads(g):
        reduces["small"], token = _exchange_start(_small_grad_blocks(g), True, None, "rs_small_start")
        return token[0, 0]

    def send_w_in_grads(g_lat, g_dqkv, g_ga, g_gb):
        e_in = _w_in_grad_regroup(g_lat, g_dqkv, g_ga, g_gb, "w_in_grad_regroup")
        reduces["pair"], token = _pair_start(e_in, "rs_w_in_pair_start")
        return token[0, 0]

    def forward_w_in_grads(after):
        e_in, recv = _pair_wait(reduces.pop("pair"), after, "rs_w_in_pair_wait")
        pair = _pair_add(e_in, recv, core, "rs_w_in_pair_add")
        reduces["w_in"], token = _exchange_start([pair], True, None, "rs_w_in_start", chips=True)
        return token

    loss_part, grad_x, small_grads = _local_step(
        x[0], h, loss_target[0], wt, conv_w_full, small, small_matrices, ffn_weight,
        send_ffn_grads, send_small_grads, send_w_in_grads, forward_w_in_grads,
    )
    loss = lax.psum(loss_part[0, 0], AXES)
    sflat = jnp.concatenate([small_grads[n].reshape(-1) for n in SMALL_ORDER])
    sflat = jnp.pad(sflat, (0, SMALL_ROWS * SMALL_COLS - sflat.shape[0])).reshape(SMALL_ROWS, SMALL_COLS)
    vec_gather, _ = _exchange_start([sflat], False, None, "rs_vec_start")

    def finish(key, by_chip, name, keeps):
        sent, lands = _exchange_wait(reduces[key], True, grad_x, name + "_wait", chips=by_chip)
        slot = chip if by_chip else dev
        own = [lax.dynamic_index_in_dim(s, slot, 0, keepdims=False) for s in sent]
        return [_slot_sum(p, f"{name}_sum_{i}", keeps[i]) for i, p in enumerate(own_slot_in(lands, own, slot))]

    group = (0, FF_GROUP)
    (s_in,) = finish("w_in", True, "rs_w_in", [((0, IN_ROWS),)])
    s_out, s_misc = finish("small", False, "rs_small", [None, None])
    s_up, s_down = finish("ffn", False, "rs_ffn", [(group, (FF_GROUP_PAD, FF_GROUP)), (group,)])
    gshard = _grad_shards([s_in, s_out, s_misc, s_up, s_down])

    updates = {n: _adamw(w[n], gshard[n], m[n], v[n], "adamw_" + n) for n in BIG_WEIGHTS}

    big_done = sum(updates[n][0][:1, :1] for n in BIG_WEIGHTS)
    own, lands = _exchange_wait(vec_gather, False, big_done, "rs_vec_wait")
    ssum = _slot_sum(own_slot_in(lands, own)[0], "small_sum").reshape(-1)
    gsmall, off = {}, 0
    for n in SMALL_ORDER:
        shape = (3, 2 * D_FF) if n == "conv_w" else w[n].shape
        size = shape[0] * shape[1]
        gsmall[n] = ssum[off : off + size].reshape(shape)
        off += size
    gsmall["conv_w"] = lax.dynamic_slice_in_dim(gsmall["conv_w"], dev * CONV_SHARD[1], CONV_SHARD[1], axis=1)
    updates.update({n: _adamw(w[n], gsmall[n], m[n], v[n], "adamw_" + n) for n in SMALL_ORDER})

    g_all = {**gshard, **gsmall}
    out_g, out_d, out_m, out_v = [], [], [], []
    for n in WEIGHT_ORDER:
        d, nm, nv = updates[n]
        shape = env[n].shape
        out_g.append(g_all[n].reshape(shape))
        out_d.append(d.reshape(shape))
        out_m.append(nm.reshape(shape))
        out_v.append(nv.reshape(shape))
    return (loss, grad_x[None], *out_g, *out_d, *out_m, *out_v)
```
